```python
import jax, jax.numpy as jnp
from jax import lax
import numpy as np

D_MODEL = 1024
BATCH = 8
SEQ = 8192
DEPTH = 2

CHUNK = 64
N_MIXERS = 2
N_HEADS = 16
HEAD_DIM = D_MODEL // N_HEADS
D_FF = 4 * D_MODEL
CONV_WIDTH = 31
Q_BLOCK = 128
RMS_EPS = 1e-6
LN_EPS = 1e-5
MASK_VALUE = -1e30
N_CONV_LAYERS = (DEPTH + 1) // 2
N_ATTN_LAYERS = DEPTH // 2

kernel_name = "hybrid_conformer_conv_fox_sqrelu_sandwich"


def rmsnorm(x, g):
    xf = x.astype(jnp.float32)
    y = xf * lax.rsqrt(jnp.mean(jnp.square(xf), axis=-1, keepdims=True) + RMS_EPS)
    return (y * g.astype(jnp.float32)).astype(x.dtype)


def layernorm(x, g, b):
    xf = x.astype(jnp.float32)
    mu = jnp.mean(xf, axis=-1, keepdims=True)
    var = jnp.mean(jnp.square(xf - mu), axis=-1, keepdims=True)
    y = (xf - mu) * lax.rsqrt(var + LN_EPS)
    return (y * g.astype(jnp.float32) + b.astype(jnp.float32)).astype(x.dtype)


def conformer_conv(h, pw1_w, pw1_b, dw_w, dw_b, ln_g, ln_b, pw2_w, pw2_b):
    a = h @ pw1_w + pw1_b
    u, gate = jnp.split(a, 2, axis=-1)
    u = u * jax.nn.sigmoid(gate)
    y = lax.conv_general_dilated(
        u, dw_w[:, None, :],
        window_strides=(1,),
        padding=((CONV_WIDTH - 1, 0),),
        dimension_numbers=("NWC", "WIO", "NWC"),
        feature_group_count=D_MODEL) + dw_b
    y = layernorm(y, ln_g, ln_b)
    y = jax.nn.silu(y)
    return y @ pw2_w + pw2_b


def forgetting_attention(h, w_in, b_f, w_o):
    B, S, _ = h.shape
    proj = h @ w_in
    q = proj[..., :D_MODEL]
    k = proj[..., D_MODEL:2 * D_MODEL]
    v = proj[..., 2 * D_MODEL:3 * D_MODEL]
    f_logit = proj[..., 3 * D_MODEL:].astype(jnp.float32) + b_f.astype(jnp.float32)
    log_f = jax.nn.log_sigmoid(f_logit)
    F = jnp.cumsum(log_f, axis=1).transpose(0, 2, 1)

    def heads(t):
        return t.reshape(B, S, N_HEADS, HEAD_DIM).transpose(0, 2, 1, 3)
    q, k, v = heads(q), heads(k), heads(v)

    n_blocks = S // Q_BLOCK
    qb = q.reshape(B, N_HEADS, n_blocks, Q_BLOCK, HEAD_DIM).transpose(2, 0, 1, 3, 4)
    Fq = F.reshape(B, N_HEADS, n_blocks, Q_BLOCK).transpose(2, 0, 1, 3)
    kpos = jnp.arange(S)
    scale = HEAD_DIM ** -0.5

    def attend(args):
        q_blk, f_blk, blk = args
        qpos = blk * Q_BLOCK + jnp.arange(Q_BLOCK)
        s = jnp.einsum("bhqd,bhkd->bhqk", q_blk, k,
                       preferred_element_type=jnp.float32) * scale
        s = s + (f_blk[..., None] - F[:, :, None, :])
        s = jnp.where(kpos[None, :] <= qpos[:, None], s, MASK_VALUE)
        p = jax.nn.softmax(s, axis=-1)
        return jnp.einsum("bhqk,bhkd->bhqd", p.astype(v.dtype), v)

    o = lax.map(attend, (qb, Fq, jnp.arange(n_blocks)))
    o = o.transpose(1, 0, 3, 2, 4).reshape(B, S, D_MODEL)
    return o @ w_o


def sqrelu_mlp(h, w_up, w_down):
    return jnp.square(jax.nn.relu(h @ w_up)) @ w_down


def _fwd_setup_inputs(seed: int = 0) -> dict:
    key = jax.random.key(seed)
    ks = jax.random.split(key, 20)
    f32 = jnp.float32
    D, H = D_MODEL, N_HEADS
    nrm = lambda k, shape, s: jax.random.normal(k, shape, f32) * s
    return {
        "x": jax.random.normal(ks[0], (BATCH, SEQ, D), f32),
        "g_mix_pre": 1.0 + nrm(ks[1], (DEPTH, D), 0.02),
        "g_mix_post": 1.0 + nrm(ks[2], (DEPTH, D), 0.02),
        "g_ffn_pre": 1.0 + nrm(ks[3], (DEPTH, D), 0.02),
        "g_ffn_post": 1.0 + nrm(ks[4], (DEPTH, D), 0.02),
        "conv_pw1_w": nrm(ks[5], (N_CONV_LAYERS, D, 2 * D), D ** -0.5),
        "conv_pw1_b": nrm(ks[6], (N_CONV_LAYERS, 2 * D), 0.02),
        "conv_dw_w": nrm(ks[7], (N_CONV_LAYERS, CONV_WIDTH, D), CONV_WIDTH ** -0.5),
        "conv_dw_b": nrm(ks[8], (N_CONV_LAYERS, D), 0.02),
        "conv_ln_g": 1.0 + nrm(ks[9], (N_CONV_LAYERS, D), 0.02),
        "conv_ln_b": nrm(ks[10], (N_CONV_LAYERS, D), 0.02),
        "conv_pw2_w": nrm(ks[11], (N_CONV_LAYERS, D, D), D ** -0.5),
        "conv_pw2_b": nrm(ks[12], (N_CONV_LAYERS, D), 0.02),
        "attn_w_in": nrm(ks[13], (N_ATTN_LAYERS, D, 3 * D + H), D ** -0.5),
        "attn_b_f": 1.0 + nrm(ks[14], (N_ATTN_LAYERS, H), 0.5),
        "attn_w_o": nrm(ks[15], (N_ATTN_LAYERS, D, D), D ** -0.5),
        "mlp_w_up": nrm(ks[16], (DEPTH, D, D_FF), D ** -0.5),
        "mlp_w_down": nrm(ks[17], (DEPTH, D_FF, D), D_FF ** -0.5),
    }


def _fwd_reference(x, g_mix_pre, g_mix_post, g_ffn_pre, g_ffn_post,
              conv_pw1_w, conv_pw1_b, conv_dw_w, conv_dw_b, conv_ln_g, conv_ln_b,
              conv_pw2_w, conv_pw2_b, attn_w_in, attn_b_f, attn_w_o,
              mlp_w_up, mlp_w_down):
    for i in range(DEPTH):
        j = i // N_MIXERS
        h = rmsnorm(x, g_mix_pre[i])
        if i % N_MIXERS == 0:
            m = conformer_conv(h, conv_pw1_w[j], conv_pw1_b[j], conv_dw_w[j], conv_dw_b[j],
                               conv_ln_g[j], conv_ln_b[j], conv_pw2_w[j], conv_pw2_b[j])
        else:
            m = forgetting_attention(h, attn_w_in[j], attn_b_f[j], attn_w_o[j])
        x = x + rmsnorm(m, g_mix_post[i])
        h = rmsnorm(x, g_ffn_pre[i])
        x = x + rmsnorm(sqrelu_mlp(h, mlp_w_up[i], mlp_w_down[i]), g_ffn_post[i])
    return x


import jax as _jax
import jax.numpy as _jnp

TWIN_FORMAT = 'train_step'
FWD_PARAMS = ['x', 'g_mix_pre', 'g_mix_post', 'g_ffn_pre', 'g_ffn_post', 'conv_pw1_w', 'conv_pw1_b', 'conv_dw_w', 'conv_dw_b', 'conv_ln_g', 'conv_ln_b', 'conv_pw2_w', 'conv_pw2_b', 'attn_w_in', 'attn_b_f', 'attn_w_o', 'mlp_w_up', 'mlp_w_down']
TWIN_WEIGHTS = ['g_mix_pre', 'g_mix_post', 'g_ffn_pre', 'g_ffn_post', 'conv_pw1_w', 'conv_pw1_b', 'conv_dw_w', 'conv_dw_b', 'conv_ln_g', 'conv_ln_b', 'conv_pw2_w', 'conv_pw2_b', 'attn_w_in', 'attn_b_f', 'attn_w_o', 'mlp_w_up', 'mlp_w_down']
TWIN_DIFF_INPUT = 'x'
TWIN_INPUTS = ['x', 'g_mix_pre', 'g_mix_post', 'g_ffn_pre', 'g_ffn_post', 'conv_pw1_w', 'conv_pw1_b', 'conv_dw_w', 'conv_dw_b', 'conv_ln_g', 'conv_ln_b', 'conv_pw2_w', 'conv_pw2_b', 'attn_w_in', 'attn_b_f', 'attn_w_o', 'mlp_w_up', 'mlp_w_down', 'loss_target', 'm_g_mix_pre', 'm_g_mix_post', 'm_g_ffn_pre', 'm_g_ffn_post', 'm_conv_pw1_w', 'm_conv_pw1_b', 'm_conv_dw_w', 'm_conv_dw_b', 'm_conv_ln_g', 'm_conv_ln_b', 'm_conv_pw2_w', 'm_conv_pw2_b', 'm_attn_w_in', 'm_attn_b_f', 'm_attn_w_o', 'm_mlp_w_up', 'm_mlp_w_down', 'v_g_mix_pre', 'v_g_mix_post', 'v_g_ffn_pre', 'v_g_ffn_post', 'v_conv_pw1_w', 'v_conv_pw1_b', 'v_conv_dw_w', 'v_conv_dw_b', 'v_conv_ln_g', 'v_conv_ln_b', 'v_conv_pw2_w', 'v_conv_pw2_b', 'v_attn_w_in', 'v_attn_b_f', 'v_attn_w_o', 'v_mlp_w_up', 'v_mlp_w_down']
TWIN_OUTPUTS = ['loss', 'grad_x', 'grad_g_mix_pre', 'grad_g_mix_post', 'grad_g_ffn_pre', 'grad_g_ffn_post', 'grad_conv_pw1_w', 'grad_conv_pw1_b', 'grad_conv_dw_w', 'grad_conv_dw_b', 'grad_conv_ln_g', 'grad_conv_ln_b', 'grad_conv_pw2_w', 'grad_conv_pw2_b', 'grad_attn_w_in', 'grad_attn_b_f', 'grad_attn_w_o', 'grad_mlp_w_up', 'grad_mlp_w_down', 'delta_g_mix_pre', 'delta_g_mix_post', 'delta_g_ffn_pre', 'delta_g_ffn_post', 'delta_conv_pw1_w', 'delta_conv_pw1_b', 'delta_conv_dw_w', 'delta_conv_dw_b', 'delta_conv_ln_g', 'delta_conv_ln_b', 'delta_conv_pw2_w', 'delta_conv_pw2_b', 'delta_attn_w_in', 'delta_attn_b_f', 'delta_attn_w_o', 'delta_mlp_w_up', 'delta_mlp_w_down', 'new_m_g_mix_pre', 'new_m_g_mix_post', 'new_m_g_ffn_pre', 'new_m_g_ffn_post', 'new_m_conv_pw1_w', 'new_m_conv_pw1_b', 'new_m_conv_dw_w', 'new_m_conv_dw_b', 'new_m_conv_ln_g', 'new_m_conv_ln_b', 'new_m_conv_pw2_w', 'new_m_conv_pw2_b', 'new_m_attn_w_in', 'new_m_attn_b_f', 'new_m_attn_w_o', 'new_m_mlp_w_up', 'new_m_mlp_w_down', 'new_v_g_mix_pre', 'new_v_g_mix_post', 'new_v_g_ffn_pre', 'new_v_g_ffn_post', 'new_v_conv_pw1_w', 'new_v_conv_pw1_b', 'new_v_conv_dw_w', 'new_v_conv_dw_b', 'new_v_conv_ln_g', 'new_v_conv_ln_b', 'new_v_conv_pw2_w', 'new_v_conv_pw2_b', 'new_v_attn_w_in', 'new_v_attn_b_f', 'new_v_attn_w_o', 'new_v_mlp_w_up', 'new_v_mlp_w_down']
TWIN_LEAF_KINDS = {'loss': 'loss', 'grad_x': 'grad_x', 'grad_g_mix_pre': 'grad_w', 'grad_g_mix_post': 'grad_w', 'grad_g_ffn_pre': 'grad_w', 'grad_g_ffn_post': 'grad_w', 'grad_conv_pw1_w': 'grad_w', 'grad_conv_pw1_b': 'grad_w', 'grad_conv_dw_w': 'grad_w', 'grad_conv_dw_b': 'grad_w', 'grad_conv_ln_g': 'grad_w', 'grad_conv_ln_b': 'grad_w', 'grad_conv_pw2_w': 'grad_w', 'grad_conv_pw2_b': 'grad_w', 'grad_attn_w_in': 'grad_w', 'grad_attn_b_f': 'grad_w', 'grad_attn_w_o': 'grad_w', 'grad_mlp_w_up': 'grad_w', 'grad_mlp_w_down': 'grad_w', 'delta_g_mix_pre': 'delta_w', 'delta_g_mix_post': 'delta_w', 'delta_g_ffn_pre': 'delta_w', 'delta_g_ffn_post': 'delta_w', 'delta_conv_pw1_w': 'delta_w', 'delta_conv_pw1_b': 'delta_w', 'delta_conv_dw_w': 'delta_w', 'delta_conv_dw_b': 'delta_w', 'delta_conv_ln_g': 'delta_w', 'delta_conv_ln_b': 'delta_w', 'delta_conv_pw2_w': 'delta_w', 'delta_conv_pw2_b': 'delta_w', 'delta_attn_w_in': 'delta_w', 'delta_attn_b_f': 'delta_w', 'delta_attn_w_o': 'delta_w', 'delta_mlp_w_up': 'delta_w', 'delta_mlp_w_down': 'delta_w', 'new_m_g_mix_pre': 'new_m', 'new_m_g_mix_post': 'new_m', 'new_m_g_ffn_pre': 'new_m', 'new_m_g_ffn_post': 'new_m', 'new_m_conv_pw1_w': 'new_m', 'new_m_conv_pw1_b': 'new_m', 'new_m_conv_dw_w': 'new_m', 'new_m_conv_dw_b': 'new_m', 'new_m_conv_ln_g': 'new_m', 'new_m_conv_ln_b': 'new_m', 'new_m_conv_pw2_w': 'new_m', 'new_m_conv_pw2_b': 'new_m', 'new_m_attn_w_in': 'new_m', 'new_m_attn_b_f': 'new_m', 'new_m_attn_w_o': 'new_m', 'new_m_mlp_w_up': 'new_m', 'new_m_mlp_w_down': 'new_m', 'new_v_g_mix_pre': 'new_v', 'new_v_g_mix_post': 'new_v', 'new_v_g_ffn_pre': 'new_v', 'new_v_g_ffn_post': 'new_v', 'new_v_conv_pw1_w': 'new_v', 'new_v_conv_pw1_b': 'new_v', 'new_v_conv_dw_w': 'new_v', 'new_v_conv_dw_b': 'new_v', 'new_v_conv_ln_g': 'new_v', 'new_v_conv_ln_b': 'new_v', 'new_v_conv_pw2_w': 'new_v', 'new_v_conv_pw2_b': 'new_v', 'new_v_attn_w_in': 'new_v', 'new_v_attn_b_f': 'new_v', 'new_v_attn_w_o': 'new_v', 'new_v_mlp_w_up': 'new_v', 'new_v_mlp_w_down': 'new_v'}


def _forward(args):
    return _fwd_reference(*[args[k] for k in FWD_PARAMS])


def _output_shape():
    def fwd():
        inp = _fwd_setup_inputs(0)
        return _fwd_reference(*[inp[k] for k in FWD_PARAMS])
    out = _jax.eval_shape(fwd)
    return out.shape, out.dtype

N_MICROBATCH = 1
ADAM_LR = 0.001
ADAM_B1 = 0.9
ADAM_B2 = 0.999
ADAM_EPS = 1e-08
ADAM_WD = 0.01
ADAM_STEP = 10
PER_EXAMPLE_BATCH_AXIS = {'x': 0, 'loss_target': 0}
SHARED_INPUTS = []
_WEIGHT_DTYPES = {'g_mix_pre': _jnp.float32, 'g_mix_post': _jnp.float32, 'g_ffn_pre': _jnp.float32, 'g_ffn_post': _jnp.float32, 'conv_pw1_w': _jnp.float32, 'conv_pw1_b': _jnp.float32, 'conv_dw_w': _jnp.float32, 'conv_dw_b': _jnp.float32, 'conv_ln_g': _jnp.float32, 'conv_ln_b': _jnp.float32, 'conv_pw2_w': _jnp.float32, 'conv_pw2_b': _jnp.float32, 'attn_w_in': _jnp.float32, 'attn_b_f': _jnp.float32, 'attn_w_o': _jnp.float32, 'mlp_w_up': _jnp.float32, 'mlp_w_down': _jnp.float32}
MOMENT_SCALE = {'g_mix_pre': 1.857327e+01, 'g_mix_post': 7.413332e+01, 'g_ffn_pre': 1.332774e+01, 'g_ffn_post': 7.397693e+01, 'conv_pw1_w': 1.143246e+00, 'conv_pw1_b': 4.320198e+01, 'conv_dw_w': 7.595293e+00, 'conv_dw_b': 1.205419e+02, 'conv_ln_g': 4.860528e+01, 'conv_ln_b': 6.881771e+01, 'conv_pw2_w': 2.860314e+01, 'conv_pw2_b': 1.352061e+02, 'attn_w_in': 1.646622e+01, 'attn_b_f': 1.160661e+01, 'attn_w_o': 2.905486e+01, 'mlp_w_up': 6.766376e+00, 'mlp_w_down': 2.930564e+01}


def _to_microbatches(a, axis):
    t = _jnp.moveaxis(a, axis, 0)
    t = t.reshape((N_MICROBATCH, t.shape[0] // N_MICROBATCH) + t.shape[1:])
    return _jnp.moveaxis(t, 1, axis + 1)


def setup_inputs(seed: int = 0) -> dict:
    inp = _fwd_setup_inputs(seed)
    key = _jax.random.fold_in(_jax.random.key(seed), 7919)
    shape, _ = _output_shape()
    out = dict(inp)
    out["loss_target"] = _jax.random.normal(_jax.random.fold_in(key, 0), shape, _jnp.float32)
    for i, name in enumerate(TWIN_WEIGHTS):
        w = inp[name].astype(_jnp.float32)
        if MOMENT_SCALE is None:
            s = _jnp.sqrt(_jnp.mean(_jnp.square(w)) + 1e-30)
        else:
            s = MOMENT_SCALE[name]
        km, kv = _jax.random.split(_jax.random.fold_in(key, i + 1))
        out[name] = w
        out["m_" + name] = s * _jax.random.normal(km, w.shape, _jnp.float32)
        out["v_" + name] = (s * s) * _jax.random.uniform(kv, w.shape, _jnp.float32, 0.5, 1.5)
    if N_MICROBATCH > 1:
        for name, axis in PER_EXAMPLE_BATCH_AXIS.items():
            out[name] = _to_microbatches(out[name], axis)
    return {'x': out['x'], 'g_mix_pre': out['g_mix_pre'], 'g_mix_post': out['g_mix_post'], 'g_ffn_pre': out['g_ffn_pre'], 'g_ffn_post': out['g_ffn_post'], 'conv_pw1_w': out['conv_pw1_w'], 'conv_pw1_b': out['conv_pw1_b'], 'conv_dw_w': out['conv_dw_w'], 'conv_dw_b': out['conv_dw_b'], 'conv_ln_g': out['conv_ln_g'], 'conv_ln_b': out['conv_ln_b'], 'conv_pw2_w': out['conv_pw2_w'], 'conv_pw2_b': out['conv_pw2_b'], 'attn_w_in': out['attn_w_in'], 'attn_b_f': out['attn_b_f'], 'attn_w_o': out['attn_w_o'], 'mlp_w_up': out['mlp_w_up'], 'mlp_w_down': out['mlp_w_down'], 'loss_target': out['loss_target'], 'm_g_mix_pre': out['m_g_mix_pre'], 'm_g_mix_post': out['m_g_mix_post'], 'm_g_ffn_pre': out['m_g_ffn_pre'], 'm_g_ffn_post': out['m_g_ffn_post'], 'm_conv_pw1_w': out['m_conv_pw1_w'], 'm_conv_pw1_b': out['m_conv_pw1_b'], 'm_conv_dw_w': out['m_conv_dw_w'], 'm_conv_dw_b': out['m_conv_dw_b'], 'm_conv_ln_g': out['m_conv_ln_g'], 'm_conv_ln_b': out['m_conv_ln_b'], 'm_conv_pw2_w': out['m_conv_pw2_w'], 'm_conv_pw2_b': out['m_conv_pw2_b'], 'm_attn_w_in': out['m_attn_w_in'], 'm_attn_b_f': out['m_attn_b_f'], 'm_attn_w_o': out['m_attn_w_o'], 'm_mlp_w_up': out['m_mlp_w_up'], 'm_mlp_w_down': out['m_mlp_w_down'], 'v_g_mix_pre': out['v_g_mix_pre'], 'v_g_mix_post': out['v_g_mix_post'], 'v_g_ffn_pre': out['v_g_ffn_pre'], 'v_g_ffn_post': out['v_g_ffn_post'], 'v_conv_pw1_w': out['v_conv_pw1_w'], 'v_conv_pw1_b': out['v_conv_pw1_b'], 'v_conv_dw_w': out['v_conv_dw_w'], 'v_conv_dw_b': out['v_conv_dw_b'], 'v_conv_ln_g': out['v_conv_ln_g'], 'v_conv_ln_b': out['v_conv_ln_b'], 'v_conv_pw2_w': out['v_conv_pw2_w'], 'v_conv_pw2_b': out['v_conv_pw2_b'], 'v_attn_w_in': out['v_attn_w_in'], 'v_attn_b_f': out['v_attn_b_f'], 'v_attn_w_o': out['v_attn_w_o'], 'v_mlp_w_up': out['v_mlp_w_up'], 'v_mlp_w_down': out['v_mlp_w_down']}


def _loss(weights, diff, rest, loss_target):
    with _jax.named_scope("forward"):
        args = {**rest, TWIN_DIFF_INPUT: diff, **{k: w.astype(_WEIGHT_DTYPES[k]) for k, w in weights.items()}}
        y = _forward(args)
    with _jax.named_scope("loss_head"):
        err = _jnp.square(y.astype(_jnp.float32) - loss_target)
        return 0.5 * _jnp.sum(_jnp.mean(err, axis=-1)) if err.ndim else 0.5 * err


def _adamw(w, g, m, v):
    m = ADAM_B1 * m + (1.0 - ADAM_B1) * g
    v = ADAM_B2 * v + (1.0 - ADAM_B2) * _jnp.square(g)
    m_hat = m / (1.0 - ADAM_B1 ** ADAM_STEP)
    v_hat = v / (1.0 - ADAM_B2 ** ADAM_STEP)
    delta = -ADAM_LR * (m_hat / (_jnp.sqrt(v_hat) + ADAM_EPS) + ADAM_WD * w)
    return delta, m, v


def reference(x, g_mix_pre, g_mix_post, g_ffn_pre, g_ffn_post, conv_pw1_w, conv_pw1_b, conv_dw_w, conv_dw_b, conv_ln_g, conv_ln_b, conv_pw2_w, conv_pw2_b, attn_w_in, attn_b_f, attn_w_o, mlp_w_up, mlp_w_down, loss_target, m_g_mix_pre, m_g_mix_post, m_g_ffn_pre, m_g_ffn_post, m_conv_pw1_w, m_conv_pw1_b, m_conv_dw_w, m_conv_dw_b, m_conv_ln_g, m_conv_ln_b, m_conv_pw2_w, m_conv_pw2_b, m_attn_w_in, m_attn_b_f, m_attn_w_o, m_mlp_w_up, m_mlp_w_down, v_g_mix_pre, v_g_mix_post, v_g_ffn_pre, v_g_ffn_post, v_conv_pw1_w, v_conv_pw1_b, v_conv_dw_w, v_conv_dw_b, v_conv_ln_g, v_conv_ln_b, v_conv_pw2_w, v_conv_pw2_b, v_attn_w_in, v_attn_b_f, v_attn_w_o, v_mlp_w_up, v_mlp_w_down):
    given = dict(x=x, g_mix_pre=g_mix_pre, g_mix_post=g_mix_post, g_ffn_pre=g_ffn_pre, g_ffn_post=g_ffn_post, conv_pw1_w=conv_pw1_w, conv_pw1_b=conv_pw1_b, conv_dw_w=conv_dw_w, conv_dw_b=conv_dw_b, conv_ln_g=conv_ln_g, conv_ln_b=conv_ln_b, conv_pw2_w=conv_pw2_w, conv_pw2_b=conv_pw2_b, attn_w_in=attn_w_in, attn_b_f=attn_b_f, attn_w_o=attn_w_o, mlp_w_up=mlp_w_up, mlp_w_down=mlp_w_down, loss_target=loss_target, m_g_mix_pre=m_g_mix_pre, m_g_mix_post=m_g_mix_post, m_g_ffn_pre=m_g_ffn_pre, m_g_ffn_post=m_g_ffn_post, m_conv_pw1_w=m_conv_pw1_w, m_conv_pw1_b=m_conv_pw1_b, m_conv_dw_w=m_conv_dw_w, m_conv_dw_b=m_conv_dw_b, m_conv_ln_g=m_conv_ln_g, m_conv_ln_b=m_conv_ln_b, m_conv_pw2_w=m_conv_pw2_w, m_conv_pw2_b=m_conv_pw2_b, m_attn_w_in=m_attn_w_in, m_attn_b_f=m_attn_b_f, m_attn_w_o=m_attn_w_o, m_mlp_w_up=m_mlp_w_up, m_mlp_w_down=m_mlp_w_down, v_g_mix_pre=v_g_mix_pre, v_g_mix_post=v_g_mix_post, v_g_ffn_pre=v_g_ffn_pre, v_g_ffn_post=v_g_ffn_post, v_conv_pw1_w=v_conv_pw1_w, v_conv_pw1_b=v_conv_pw1_b, v_conv_dw_w=v_conv_dw_w, v_conv_dw_b=v_conv_dw_b, v_conv_ln_g=v_conv_ln_g, v_conv_ln_b=v_conv_ln_b, v_conv_pw2_w=v_conv_pw2_w, v_conv_pw2_b=v_conv_pw2_b, v_attn_w_in=v_attn_w_in, v_attn_b_f=v_attn_b_f, v_attn_w_o=v_attn_w_o, v_mlp_w_up=v_mlp_w_up, v_mlp_w_down=v_mlp_w_down)
    weights = {n: given[n] for n in TWIN_WEIGHTS}
    shared = {n: given[n] for n in SHARED_INPUTS}
    per_example = {n: given[n] for n in ['x']}
    grad_fn = _jax.value_and_grad(_loss, argnums=(0, 1))

    def one_microbatch(ex, loss_target):
        ex = dict(ex)
        diff = ex.pop(TWIN_DIFF_INPUT)
        return grad_fn(weights, diff, {**shared, **ex}, loss_target)

    if N_MICROBATCH == 1:
        loss, (grad_w, grad_x) = one_microbatch(per_example, given["loss_target"])
    else:
        def body(carry, xs):
            loss_sum, grad_sum = carry
            l_k, (gw_k, gx_k) = one_microbatch(xs[0], xs[1])
            with _jax.named_scope("update"):
                return (loss_sum + l_k, _jax.tree.map(_jnp.add, grad_sum, gw_k)), gx_k

        init = (_jnp.zeros((), _jnp.float32), _jax.tree.map(_jnp.zeros_like, weights))
        (loss, grad_w), grad_x = _jax.lax.scan(body, init, (per_example, given["loss_target"]))
    with _jax.named_scope("update"):
        delta_w, new_m, new_v = {}, {}, {}
        for n in TWIN_WEIGHTS:
            delta_w[n], new_m[n], new_v[n] = _adamw(weights[n], grad_w[n], given["m_" + n], given["v_" + n])
    return (loss, grad_x, *[grad_w[n] for n in TWIN_WEIGHTS], *[delta_w[n] for n in TWIN_WEIGHTS],
            *[new_m[n] for n in TWIN_WEIGHTS], *[new_v[n] for n in TWIN_WEIGHTS])
```

```python
import functools
import math

import numpy as np
import jax
import jax.numpy as jnp
from jax import lax
from jax.experimental import pallas as pl
from jax.experimental.pallas import tpu as pltpu

F32 = jnp.float32
BF16 = jnp.bfloat16

RMS_EPS = 1e-6
LN_EPS = 1e-5
MASK_VALUE = -1e30
ADAM_LR = 0.001
ADAM_B1 = 0.9
ADAM_B2 = 0.999
ADAM_EPS = 1e-08
ADAM_WD = 0.01
ADAM_STEP = 10

N_DEV = 8
LANES = 128
SUBLANES = 8
CONV_HALO = 32
CONV_ROWS = 32
VMEM_LIMIT = 56 * 1024 * 1024

_pcall = pl.pallas_call


def _params(sem=None):
    if sem is None:
        return pltpu.CompilerParams(vmem_limit_bytes=VMEM_LIMIT)
    return pltpu.CompilerParams(dimension_semantics=sem, vmem_limit_bytes=VMEM_LIMIT)


def _dot(a, b):
    return jnp.dot(a, b, preferred_element_type=F32)


def _dot_nt(a, b):
    return lax.dot_general(a, b, (((1,), (1,)), ((), ())), preferred_element_type=F32)


def _dot_tn(a, b):
    return lax.dot_general(a, b, (((0,), (0,)), ((), ())), preferred_element_type=F32)


def _full(shape):
    nd = len(shape)
    return pl.BlockSpec(shape, lambda *g: (0,) * nd, pipeline_mode=pl.Buffered(1))


def _acc(shape):
    nd = len(shape)
    return pl.BlockSpec(shape, lambda *g: (0,) * nd)


def _rows(tm, cols):
    return pl.BlockSpec((tm, cols), lambda i: (i, 0))


def _rms(x, g):
    r = lax.rsqrt(jnp.mean(x * x, axis=-1, keepdims=True) + RMS_EPS)
    return x * r * g


def _rms_bwd(x, g, dy):
    r = lax.rsqrt(jnp.mean(x * x, axis=-1, keepdims=True) + RMS_EPS)
    n = x * r
    dg = jnp.sum(dy * n, axis=0, keepdims=True)
    dn = dy * g
    dx = r * (dn - n * jnp.mean(dn * n, axis=-1, keepdims=True))
    return dx, dg


def _sigmoid(x):
    return 1.0 / (1.0 + jnp.exp(-x))


def _mesh_pos():
    return lax.axis_index("x"), lax.axis_index("y"), lax.axis_index("c")


def _dev_index(px, py, pc):
    return 4 * px + 2 * py + pc


def _all_gather(items, name):
    n = len(items)
    arrs = []
    for it in items:
        if not any(it[0] is a for a in arrs):
            arrs.append(it[0])
    n_in = len(arrs)
    which = [[it[0] is a for a in arrs].index(True) for it in items]
    shapes = [it[0].shape if it[1] is None else it[0].shape[1:] for it in items]
    dtypes = [it[2] for it in items]

    def body(*refs):
        ins = [refs[w] for w in which]
        outs = refs[n_in:n_in + n]
        stage = refs[n_in + n:n_in + 2 * n]
        send_sems, recv_sems, local_sems = refs[n_in + 2 * n:]
        x, y, c = _mesh_pos()
        me, sib = (x, y, c), (x, y, 1 - c)
        chips = [(1 - x, y), (x, 1 - y), (1 - x, 1 - y)]

        def slot(a, block):
            return outs[a].at[_dev_index(*block)]

        def rcopy(a, k, block, to, src=None):
            dst = slot(a, block)
            return pltpu.make_async_remote_copy(
                src_ref=dst if src is None else src, dst_ref=dst,
                send_sem=send_sems.at[a, k], recv_sem=recv_sems.at[a, k],
                device_id=to, device_id_type=pl.DeviceIdType.MESH)

        mine = []
        for a in range(n):
            src = ins[a] if items[a][1] is None else ins[a].at[items[a][1]]
            stage[a][...] = src[...].astype(dtypes[a])
            cp = pltpu.make_async_copy(stage[a], slot(a, me), local_sems.at[a])
            cp.start()
            mine.append(cp)
        first = []
        for a in range(n):
            first.append(rcopy(a, 0, me, sib, src=stage[a]))
            for j, chip in enumerate(chips):
                first.append(rcopy(a, 1 + j, me, (*chip, c), src=stage[a]))
        for cp in first:
            cp.start()
        passed = []
        for j, chip in enumerate(chips):
            for a in range(n):
                rcopy(a, 1 + j, (*chip, c), me).wait_recv()
                cp = rcopy(a, 4 + j, (*chip, c), sib)
                cp.start()
                passed.append(cp)
        for a in range(n):
            rcopy(a, 0, sib, me).wait_recv()
            for j, chip in enumerate(chips):
                rcopy(a, 4 + j, (*chip, 1 - c), me).wait_recv()
        for cp in first + passed:
            cp.wait_send()
        for cp in mine:
            cp.wait()

    return _pcall(
        body, name=name,
        out_shape=[jax.ShapeDtypeStruct((N_DEV,) + tuple(s), d) for s, d in zip(shapes, dtypes)],
        in_specs=[pl.BlockSpec(memory_space=pltpu.VMEM)] * n_in,
        out_specs=[pl.BlockSpec(memory_space=pl.ANY)] * n,
        scratch_shapes=[pltpu.VMEM(tuple(s), d) for s, d in zip(shapes, dtypes)]
        + [pltpu.SemaphoreType.DMA((n, 7)), pltpu.SemaphoreType.DMA((n, 7)), pltpu.SemaphoreType.DMA((n,))],
        compiler_params=pltpu.CompilerParams(vmem_limit_bytes=VMEM_LIMIT),
    )(*arrs)


def _exchange(items, out_shapes, small, name):
    n = len(items)
    n_out = len(out_shapes)
    arrs = [it[0] for it in items]

    def body(*refs):
        ins = refs[:n]
        sm = refs[n]
        outs = refs[n + 1:n + 1 + n_out]
        smo = refs[n + 1 + n_out]
        send_sems, recv_sems, local_sems = refs[n + 2 + n_out:]
        x, y, c = _mesh_pos()
        me_id = _dev_index(x, y, c)

        def peer(r):
            p = ((1 - x) if r & 4 else x, (1 - y) if r & 2 else y, (1 - c) if r & 1 else c)
            return p, _dev_index(*p)

        def land(a, sender_id):
            if a == n:
                return smo.at[sender_id]
            _, oi, sub = items[a]
            return outs[oi].at[sender_id] if sub is None else outs[oi].at[sender_id, sub]

        def give(a, to_id):
            return sm if a == n else ins[a].at[to_id]

        sends, mine = [], []
        for a in range(n + 1):
            cp = pltpu.make_async_copy(give(a, me_id), land(a, me_id), local_sems.at[a])
            cp.start()
            mine.append(cp)
            for r in range(1, N_DEV):
                p, pid = peer(r)
                cp = pltpu.make_async_remote_copy(
                    src_ref=give(a, pid), dst_ref=land(a, me_id),
                    send_sem=send_sems.at[a, r - 1], recv_sem=recv_sems.at[a, r - 1],
                    device_id=p, device_id_type=pl.DeviceIdType.MESH)
                cp.start()
                sends.append(cp)
        for a in range(n + 1):
            for r in range(1, N_DEV):
                p, pid = peer(r)
                pltpu.make_async_remote_copy(
                    src_ref=land(a, pid), dst_ref=land(a, pid),
                    send_sem=send_sems.at[a, r - 1], recv_sem=recv_sems.at[a, r - 1],
                    device_id=p, device_id_type=pl.DeviceIdType.MESH).wait_recv()
        for cp in sends:
            cp.wait_send()
        for cp in mine:
            cp.wait()

    res = _pcall(
        body, name=name,
        out_shape=[jax.ShapeDtypeStruct(tuple(s), d) for s, d in out_shapes]
        + [jax.ShapeDtypeStruct((N_DEV,) + small.shape, small.dtype)],
        in_specs=[pl.BlockSpec(memory_space=pl.ANY)] * (n + 1),
        out_specs=[pl.BlockSpec(memory_space=pl.ANY)] * (n_out + 1),
        scratch_shapes=[pltpu.SemaphoreType.DMA((n + 1, 7)), pltpu.SemaphoreType.DMA((n + 1, 7)),
                        pltpu.SemaphoreType.DMA((n + 1,))],
    )(*arrs, small)
    return res[:n_out], res[n_out]


def _conv_in_fwd(x, g_pre, w1g, b1, tm):
    T, D = x.shape
    ns, _, cs = w1g.shape
    half = ns // 2

    def body(x_ref, g_ref, w_ref, b_ref, a_ref, u_ref):
        h = _rms(x_ref[...], g_ref[...]).astype(BF16)
        parts = []
        for s in range(ns):
            a_s = _dot(h, w_ref[s]) + b_ref[:, s * cs:(s + 1) * cs]
            a_ref[:, s * cs:(s + 1) * cs] = a_s
            parts.append(a_s)
        for s in range(half):
            u_ref[:, s * cs:(s + 1) * cs] = parts[s] * _sigmoid(parts[s + half])

    return _pcall(
        body, name="conv_in_fwd", grid=(T // tm,),
        in_specs=[_rows(tm, D), _full((1, D)), _full(w1g.shape), _full((1, 2 * D))],
        out_specs=[_rows(tm, 2 * D), _rows(tm, D)],
        out_shape=[jax.ShapeDtypeStruct((T, 2 * D), F32), jax.ShapeDtypeStruct((T, D), F32)],
        compiler_params=_params(("arbitrary",)),
    )(x, g_pre, w1g, b1)


def _dwconv_fwd(u, w32, b, tm, lc, width):
    T, D = u.shape
    hb = tm // CONV_HALO

    def body(u_ref, halo_ref, w_ref, b_ref, y_ref, ext_ref):
        i = pl.program_id(0)
        ext_ref[0:CONV_HALO, :] = jnp.where(i > 0, halo_ref[...], 0.0)
        ext_ref[CONV_HALO:, :] = u_ref[...]
        for r0 in range(0, tm, CONV_ROWS):
            for l0 in range(0, lc, LANES):
                ls = slice(l0, l0 + LANES)
                acc = jnp.zeros((CONV_ROWS, LANES), F32) + b_ref[:, ls]
                for j in range(width):
                    off = CONV_HALO - (width - 1) + j + r0
                    acc = acc + w_ref[j:j + 1, ls] * ext_ref[off:off + CONV_ROWS, ls]
                y_ref[r0:r0 + CONV_ROWS, ls] = acc

    return _pcall(
        body, name="dwconv_fwd", grid=(T // tm, D // lc),
        in_specs=[pl.BlockSpec((tm, lc), lambda i, l: (i, l)),
                  pl.BlockSpec((CONV_HALO, lc), lambda i, l: (jnp.maximum(i * hb - 1, 0), l)),
                  pl.BlockSpec((32, lc), lambda i, l: (0, l)),
                  pl.BlockSpec((1, lc), lambda i, l: (0, l))],
        out_specs=pl.BlockSpec((tm, lc), lambda i, l: (i, l)),
        out_shape=jax.ShapeDtypeStruct((T, D), F32),
        scratch_shapes=[pltpu.VMEM((tm + CONV_HALO, lc), F32)],
        compiler_params=_params(("arbitrary", "arbitrary")),
    )(u, u, w32, b)


def _ln_parts(y, g, b):
    mu = jnp.mean(y, axis=-1, keepdims=True)
    yc = y - mu
    rstd = lax.rsqrt(jnp.mean(yc * yc, axis=-1, keepdims=True) + LN_EPS)
    yhat = yc * rstd
    return yhat, rstd, yhat * g + b


def _conv_out_fwd(y, x, ln_g, ln_b, w2, b2, g_post, tm):
    T, D = x.shape

    def body(y_ref, x_ref, lg_ref, lb_ref, w_ref, b_ref, g_ref, m_ref, xo_ref):
        _, _, yn = _ln_parts(y_ref[...], lg_ref[...], lb_ref[...])
        z = (yn * _sigmoid(yn)).astype(BF16)
        m = _dot(z, w_ref[...]) + b_ref[...]
        m_ref[...] = m
        xo_ref[...] = x_ref[...] + _rms(m, g_ref[...])

    return _pcall(
        body, name="conv_out_fwd", grid=(T // tm,),
        in_specs=[_rows(tm, D), _rows(tm, D), _full((1, D)), _full((1, D)), _full((D, D)), _full((1, D)),
                  _full((1, D))],
        out_specs=[_rows(tm, D), _rows(tm, D)],
        out_shape=[jax.ShapeDtypeStruct((T, D), F32), jax.ShapeDtypeStruct((T, D), F32)],
        compiler_params=_params(("arbitrary",)),
    )(y, x, ln_g, ln_b, w2, b2, g_post)


def _conv_out_bwd(dxo, m, y, ln_g, ln_b, w2, g_post, tm):
    T, D = m.shape

    def body(dxo_ref, m_ref, y_ref, lg_ref, lb_ref, w_ref, g_ref, dy_ref, dm_ref, z_ref, sums_ref):
        i = pl.program_id(0)
        dm, dgpost = _rms_bwd(m_ref[...], g_ref[...], dxo_ref[...])
        dmb = dm.astype(BF16)
        dm_ref[...] = dmb
        yhat, rstd, yn = _ln_parts(y_ref[...], lg_ref[...], lb_ref[...])
        sg = _sigmoid(yn)
        z_ref[...] = (yn * sg).astype(BF16)
        dz = _dot_nt(dmb, w_ref[...])
        dyn = dz * (sg + yn * sg * (1.0 - sg))
        dyh = dyn * lg_ref[...]
        dy = rstd * (dyh - jnp.mean(dyh, axis=-1, keepdims=True)
                     - yhat * jnp.mean(dyh * yhat, axis=-1, keepdims=True))
        dy_ref[...] = dy

        @pl.when(i == 0)
        def _():
            sums_ref[...] = jnp.zeros_like(sums_ref)
        sums_ref[0:1, :] += dgpost
        sums_ref[1:2, :] += jnp.sum(dyn * yhat, axis=0, keepdims=True)
        sums_ref[2:3, :] += jnp.sum(dyn, axis=0, keepdims=True)
        sums_ref[3:4, :] += jnp.sum(dm, axis=0, keepdims=True)
        sums_ref[4:5, :] += jnp.sum(dy, axis=0, keepdims=True)

    return _pcall(
        body, name="conv_out_bwd", grid=(T // tm,),
        in_specs=[_rows(tm, D), _rows(tm, D), _rows(tm, D), _full((1, D)), _full((1, D)), _full((D, D)),
                  _full((1, D))],
        out_specs=[_rows(tm, D), _rows(tm, D), _rows(tm, D), _acc((SUBLANES, D))],
        out_shape=[jax.ShapeDtypeStruct((T, D), F32), jax.ShapeDtypeStruct((T, D), BF16),
                   jax.ShapeDtypeStruct((T, D), BF16), jax.ShapeDtypeStruct((SUBLANES, D), F32)],
        compiler_params=_params(("arbitrary",)),
    )(dxo, m, y, ln_g, ln_b, w2, g_post)


def _dwconv_bwd(dy, u, w32, tm, lc, width):
    T, D = u.shape
    hb = tm // CONV_HALO
    nt = T // tm
    last_halo = T // CONV_HALO - 1

    def body(dy_ref, dyn_ref, u_ref, up_ref, w_ref, du_ref, dw_ref, exty_ref, extu_ref, acc_ref):
        i = pl.program_id(1)
        exty_ref[0:tm, :] = dy_ref[...]
        exty_ref[tm:, :] = jnp.where(i < nt - 1, dyn_ref[...], 0.0)
        extu_ref[0:CONV_HALO, :] = jnp.where(i > 0, up_ref[...], 0.0)
        extu_ref[CONV_HALO:, :] = u_ref[...]

        @pl.when(i == 0)
        def _():
            acc_ref[...] = jnp.zeros_like(acc_ref)

        for r0 in range(0, tm, CONV_ROWS):
            for l0 in range(0, lc, LANES):
                ls = slice(l0, l0 + LANES)
                dyc = exty_ref[r0:r0 + CONV_ROWS, ls]
                du = jnp.zeros((CONV_ROWS, LANES), F32)
                for j in range(width):
                    oy = r0 + (width - 1) - j
                    du = du + w_ref[j:j + 1, ls] * exty_ref[oy:oy + CONV_ROWS, ls]
                    ou = CONV_HALO - (width - 1) + j + r0
                    prod = dyc * extu_ref[ou:ou + CONV_ROWS, ls]
                    acc_ref[j, :, ls] += prod.reshape(CONV_ROWS // SUBLANES, SUBLANES, LANES).sum(axis=0)
                du_ref[r0:r0 + CONV_ROWS, ls] = du

        @pl.when(i == nt - 1)
        def _():
            for j in range(32):
                dw_ref[j:j + 1, :] = jnp.sum(acc_ref[j], axis=0, keepdims=True)

    return _pcall(
        body, name="dwconv_bwd", grid=(D // lc, nt),
        in_specs=[pl.BlockSpec((tm, lc), lambda l, i: (i, l)),
                  pl.BlockSpec((CONV_HALO, lc), lambda l, i: (jnp.minimum((i + 1) * hb, last_halo), l)),
                  pl.BlockSpec((tm, lc), lambda l, i: (i, l)),
                  pl.BlockSpec((CONV_HALO, lc), lambda l, i: (jnp.maximum(i * hb - 1, 0), l)),
                  pl.BlockSpec((32, lc), lambda l, i: (0, l))],
        out_specs=[pl.BlockSpec((tm, lc), lambda l, i: (i, l)),
                   pl.BlockSpec((32, lc), lambda l, i: (0, l))],
        out_shape=[jax.ShapeDtypeStruct((T, D), F32), jax.ShapeDtypeStruct((32, D), F32)],
        scratch_shapes=[pltpu.VMEM((tm + CONV_HALO, lc), F32), pltpu.VMEM((tm + CONV_HALO, lc), F32),
                        pltpu.VMEM((32, SUBLANES, lc), F32)],
        compiler_params=_params(("arbitrary", "arbitrary")),
    )(dy, dy, u, u, w32)


def _conv_in_bwd(dxo, du, a, x, g_pre, w1g, tm):
    T, D = x.shape
    ns, _, cs = w1g.shape
    half = ns // 2

    def body(dxo_ref, du_ref, a_ref, x_ref, g_ref, w_ref, dxi_ref, h_ref, da_ref, sums_ref, db_ref):
        i = pl.program_id(0)
        xv = x_ref[...]
        h_ref[...] = _rms(xv, g_ref[...]).astype(BF16)
        dh = jnp.zeros((tm, D), F32)
        dbs = [None] * ns
        for s in range(half):
            a_u = a_ref[:, s * cs:(s + 1) * cs]
            sg = _sigmoid(a_ref[:, (s + half) * cs:(s + half + 1) * cs])
            du_s = du_ref[:, s * cs:(s + 1) * cs]
            da_u = du_s * sg
            da_g = du_s * a_u * sg * (1.0 - sg)
            for s2, v in ((s, da_u), (s + half, da_g)):
                vb = v.astype(BF16)
                da_ref[:, s2 * cs:(s2 + 1) * cs] = vb
                dbs[s2] = jnp.sum(v, axis=0, keepdims=True)
                dh = dh + _dot_nt(vb, w_ref[s2])
        dxi, dgpre = _rms_bwd(xv, g_ref[...], dh)
        dxi_ref[...] = dxo_ref[...] + dxi

        @pl.when(i == 0)
        def _():
            sums_ref[...] = jnp.zeros_like(sums_ref)
            db_ref[...] = jnp.zeros_like(db_ref)
        sums_ref[0:1, :] += dgpre
        for s in range(ns):
            db_ref[:, s * cs:(s + 1) * cs] += dbs[s]

    return _pcall(
        body, name="conv_in_bwd", grid=(T // tm,),
        in_specs=[_rows(tm, D), _rows(tm, D), _rows(tm, 2 * D), _rows(tm, D), _full((1, D)),
                  _full(w1g.shape)],
        out_specs=[_rows(tm, D), _rows(tm, D), _rows(tm, 2 * D), _acc((SUBLANES, D)), _acc((1, 2 * D))],
        out_shape=[jax.ShapeDtypeStruct((T, D), F32), jax.ShapeDtypeStruct((T, D), BF16),
                   jax.ShapeDtypeStruct((T, 2 * D), BF16), jax.ShapeDtypeStruct((SUBLANES, D), F32),
                   jax.ShapeDtypeStruct((1, 2 * D), F32)],
        compiler_params=_params(("arbitrary",)),
    )(dxo, du, a, x, g_pre, w1g)


def _mlp_fwd(x, g_pre, wug, wdg, g_post, tm, name):
    T, D = x.shape
    ns, _, fs = wug.shape

    def body(x_ref, gp_ref, wu_ref, wd_ref, gq_ref, up_ref, m_ref, xo_ref):
        xv = x_ref[...]
        h = _rms(xv, gp_ref[...]).astype(BF16)
        acc = jnp.zeros((tm, D), F32)
        for s in range(ns):
            up = _dot(h, wu_ref[s]).astype(BF16)
            up_ref[:, s * fs:(s + 1) * fs] = up
            act = jnp.square(jnp.maximum(up.astype(F32), 0.0)).astype(BF16)
            acc = acc + _dot(act, wd_ref[s])
        m_ref[...] = acc
        xo_ref[...] = xv + _rms(acc, gq_ref[...])

    return _pcall(
        body, name=name, grid=(T // tm,),
        in_specs=[_rows(tm, D), _full((1, D)), _full(wug.shape), _full(wdg.shape), _full((1, D))],
        out_specs=[_rows(tm, ns * fs), _rows(tm, D), _rows(tm, D)],
        out_shape=[jax.ShapeDtypeStruct((T, ns * fs), BF16), jax.ShapeDtypeStruct((T, D), F32),
                   jax.ShapeDtypeStruct((T, D), F32)],
        compiler_params=_params(("arbitrary",)),
    )(x, g_pre, wug, wdg, g_post)


def _mlp_bwd(dxo, m, x, up, g_pre, wug, wdg, g_post, tm, name):
    T, D = x.shape
    ns, _, fs = wug.shape

    def body(dxo_ref, m_ref, x_ref, up_ref, gp_ref, wu_ref, wd_ref, gq_ref,
             dxi_ref, h_ref, dm_ref, dup_ref, sums_ref):
        i = pl.program_id(0)
        dxo = dxo_ref[...]
        dm, dgpost = _rms_bwd(m_ref[...], gq_ref[...], dxo)
        dmb = dm.astype(BF16)
        dm_ref[...] = dmb
        xv = x_ref[...]
        h_ref[...] = _rms(xv, gp_ref[...]).astype(BF16)
        dh = jnp.zeros((tm, D), F32)
        for s in range(ns):
            dact = _dot_nt(dmb, wd_ref[s])
            up = up_ref[:, s * fs:(s + 1) * fs].astype(F32)
            dup = (dact * (2.0 * jnp.maximum(up, 0.0))).astype(BF16)
            dup_ref[:, s * fs:(s + 1) * fs] = dup
            dh = dh + _dot_nt(dup, wu_ref[s])
        dxi, dgpre = _rms_bwd(xv, gp_ref[...], dh)
        dxi_ref[...] = dxo + dxi

        @pl.when(i == 0)
        def _():
            sums_ref[...] = jnp.zeros_like(sums_ref)
        sums_ref[0:1, :] += dgpost
        sums_ref[1:2, :] += dgpre

    return _pcall(
        body, name=name, grid=(T // tm,),
        in_specs=[_rows(tm, D), _rows(tm, D), _rows(tm, D), _rows(tm, ns * fs), _full((1, D)),
                  _full(wug.shape), _full(wdg.shape), _full((1, D))],
        out_specs=[_rows(tm, D), _rows(tm, D), _rows(tm, D), _rows(tm, ns * fs), _acc((SUBLANES, D))],
        out_shape=[jax.ShapeDtypeStruct((T, D), F32), jax.ShapeDtypeStruct((T, D), BF16),
                   jax.ShapeDtypeStruct((T, D), BF16), jax.ShapeDtypeStruct((T, ns * fs), BF16),
                   jax.ShapeDtypeStruct((SUBLANES, D), F32)],
        compiler_params=_params(("arbitrary",)),
    )(dxo, m, x, up, g_pre, wug, wdg, g_post)


def _mm_tn(a, g, *, nj, a_cols, g_cols, a_by_j, g_by_j, tk, name, act=False):
    T = a.shape[0]
    nk = T // tk

    def body(a_ref, g_ref, o_ref, acc_ref):
        k = pl.program_id(1)
        av = a_ref[...]
        if act:
            av = jnp.square(jnp.maximum(av.astype(F32), 0.0)).astype(BF16)
        p = _dot_tn(av, g_ref[...])

        @pl.when(k == 0)
        def _():
            acc_ref[...] = p

        @pl.when(k > 0)
        def _():
            acc_ref[...] += p

        @pl.when(k == nk - 1)
        def _():
            o_ref[...] = acc_ref[...].astype(BF16)

    return _pcall(
        body, name=name, grid=(nj, nk),
        in_specs=[pl.BlockSpec((tk, a_cols), (lambda j, k: (k, j)) if a_by_j else (lambda j, k: (k, 0))),
                  pl.BlockSpec((tk, g_cols), (lambda j, k: (k, j)) if g_by_j else (lambda j, k: (k, 0)))],
        out_specs=pl.BlockSpec((None, a_cols, g_cols), lambda j, k: (j, 0, 0)),
        out_shape=jax.ShapeDtypeStruct((nj, a_cols, g_cols), BF16),
        scratch_shapes=[pltpu.VMEM((a_cols, g_cols), F32)],
        compiler_params=_params(("arbitrary", "arbitrary")),
    )(a, g)


def _attn_in_fwd(x, g_pre, wqkv, wf, bf, tm, q_mul, n_heads):
    T, D = x.shape

    def body(x_ref, g_ref, w_ref, wf_ref, bf_ref, q_ref, k_ref, v_ref, lf_ref):
        h = _rms(x_ref[...], g_ref[...]).astype(BF16)
        q = _dot(h, w_ref[:, 0:D])
        if q_mul != 1.0:
            q = q * q_mul
        q_ref[...] = q.astype(BF16)
        k_ref[...] = _dot(h, w_ref[:, D:2 * D]).astype(BF16)
        v_ref[...] = _dot(h, w_ref[:, 2 * D:3 * D]).astype(BF16)
        fl = _dot(h, wf_ref[...]) + bf_ref[...]
        lf = jnp.minimum(fl, 0.0) - jnp.log(1.0 + jnp.exp(-jnp.abs(fl)))
        lane = lax.broadcasted_iota(jnp.int32, (1, LANES), 1)
        lf_ref[...] = jnp.where(lane < n_heads, lf, 0.0)

    return _pcall(
        body, name="attn_in_fwd", grid=(T // tm,),
        in_specs=[_rows(tm, D), _full((1, D)), _full((D, 3 * D)), _full((D, LANES)), _full((1, LANES))],
        out_specs=[_rows(tm, D), _rows(tm, D), _rows(tm, D), _rows(tm, LANES)],
        out_shape=[jax.ShapeDtypeStruct((T, D), BF16)] * 3 + [jax.ShapeDtypeStruct((T, LANES), F32)],
        compiler_params=_params(("arbitrary",)),
    )(x, g_pre, wqkv, wf, bf)


def _cumsum_rows(v, v2, tb, reverse, name):
    T, C = v.shape
    nb = T // tb

    def body(v_ref, v2_ref, o_ref, carry_ref):
        i = pl.program_id(0)

        @pl.when(i == 0)
        def _():
            carry_ref[...] = jnp.zeros_like(carry_ref)
        r = lax.broadcasted_iota(jnp.int32, (tb, tb), 0)
        c = lax.broadcasted_iota(jnp.int32, (tb, tb), 1)
        tri = jnp.where((c >= r) if reverse else (c <= r), 1.0, 0.0).astype(F32)
        out = jnp.dot(tri, v_ref[...] + v2_ref[...], precision=lax.Precision.HIGHEST,
                      preferred_element_type=F32) + carry_ref[...]
        o_ref[...] = out
        carry_ref[...] = out[0:1, :] if reverse else out[tb - 1:tb, :]

    idx = (lambda i: (nb - 1 - i, 0)) if reverse else (lambda i: (i, 0))
    return _pcall(
        body, name=name, grid=(nb,),
        in_specs=[pl.BlockSpec((tb, C), idx), pl.BlockSpec((tb, C), idx)],
        out_specs=pl.BlockSpec((tb, C), idx),
        out_shape=jax.ShapeDtypeStruct((T, C), F32),
        scratch_shapes=[pltpu.VMEM((1, C), F32)],
        compiler_params=_params(("arbitrary",)),
    )(v, v2)


def _flash_fwd(q, k, v, fq_aux, fk_rows, *, dh, tq, s_mul):
    T, D = q.shape
    G = D // LANES
    hpg = LANES // dh
    nq = T // tq
    k3 = k.reshape(nq, tq, D)
    v3 = v.reshape(nq, tq, D)

    def body(q_ref, k_ref, v_ref, fq_ref, fk_ref, o_ref, o32_ref, lse_ref):
        i = pl.program_id(1)
        lane = lax.broadcasted_iota(jnp.int32, (1, LANES), 1)
        row = lax.broadcasted_iota(jnp.int32, (tq, tq), 0)
        col = lax.broadcasted_iota(jnp.int32, (tq, tq), 1)
        causal = col <= row
        q2 = q_ref[...]
        fq = fq_ref[...]
        o_all = jnp.zeros((tq, LANES), F32)
        lse_all = jnp.zeros((tq, LANES), F32)
        for hh in range(hpg):
            hmask = (lane >= hh * dh) & (lane < (hh + 1) * dh)
            qm = jnp.where(hmask, q2, jnp.zeros_like(q2))
            fq_col = fq[:, hh:hh + 1]

            def step(j, carry, masked):
                m, l, acc = carry
                kj = k_ref[j]
                vj = v_ref[j]
                s = _dot_nt(qm, kj)
                if s_mul != 1.0:
                    s = s * s_mul
                s = s + (fq_col - fk_ref[j, hh:hh + 1, :])
                if masked:
                    s = jnp.where(causal, s, MASK_VALUE)
                m_new = jnp.maximum(m, jnp.max(s, axis=1, keepdims=True))
                alpha = jnp.exp(m - m_new)
                p = jnp.exp(s - m_new)
                l = alpha * l + jnp.sum(p, axis=1, keepdims=True)
                acc = alpha * acc + _dot(p.astype(BF16), vj)
                return m_new, l, acc

            init = (jnp.full((tq, 1), MASK_VALUE, F32), jnp.zeros((tq, 1), F32),
                    jnp.zeros((tq, LANES), F32))
            carry = lax.fori_loop(0, i, lambda j, cr: step(j, cr, False), init)
            m, l, acc = step(i, carry, True)
            o_all = jnp.where(hmask, acc * (1.0 / l), o_all)
            lse_all = jnp.where(lane == hh, m + jnp.log(l), lse_all)
        o_ref[...] = o_all.astype(BF16)
        o32_ref[...] = o_all
        lse_ref[...] = lse_all

    return _pcall(
        body, name="flash_fwd", grid=(G, nq),
        in_specs=[pl.BlockSpec((tq, LANES), lambda g, i: (i, g)),
                  pl.BlockSpec((nq, tq, LANES), lambda g, i: (0, 0, g)),
                  pl.BlockSpec((nq, tq, LANES), lambda g, i: (0, 0, g)),
                  pl.BlockSpec((None, tq, LANES), lambda g, i: (g, i, 0)),
                  pl.BlockSpec((None, nq, SUBLANES, tq), lambda g, i: (g, 0, 0, 0))],
        out_specs=[pl.BlockSpec((tq, LANES), lambda g, i: (i, g)),
                   pl.BlockSpec((tq, LANES), lambda g, i: (i, g)),
                   pl.BlockSpec((None, tq, LANES), lambda g, i: (g, i, 0))],
        out_shape=[jax.ShapeDtypeStruct((T, D), BF16), jax.ShapeDtypeStruct((T, D), F32),
                   jax.ShapeDtypeStruct((G, T, LANES), F32)],
        compiler_params=_params(("arbitrary", "arbitrary")),
    )(q, k3, v3, fq_aux, fk_rows)


def _flash_bwd(q, k, v, do, fq_aux, lse_aux, dl_aux, fk_rows, *, dh, tq, s_mul, dq_mul):
    T, D = q.shape
    G = D // LANES
    hpg = LANES // dh
    nq = T // tq
    k3 = k.reshape(nq, tq, D)
    v3 = v.reshape(nq, tq, D)

    def body(q_ref, k_ref, v_ref, do_ref, fq_ref, lse_ref, dl_ref, fk_ref,
             dq_ref, dk_ref, dv_ref, dfq_ref, dfk_ref, dk_acc, dv_acc):
        i = pl.program_id(1)

        @pl.when(i == 0)
        def _():
            dk_acc[...] = jnp.zeros_like(dk_acc)
            dv_acc[...] = jnp.zeros_like(dv_acc)
            dfk_ref[...] = jnp.zeros_like(dfk_ref)

        lane = lax.broadcasted_iota(jnp.int32, (1, LANES), 1)
        row = lax.broadcasted_iota(jnp.int32, (tq, tq), 0)
        col = lax.broadcasted_iota(jnp.int32, (tq, tq), 1)
        causal = col <= row
        q2 = q_ref[...]
        do2 = do_ref[...]
        cq = fq_ref[...] - lse_ref[...]
        dl = dl_ref[...]
        dq_all = jnp.zeros((tq, LANES), F32)
        dfq_all = jnp.zeros((tq, LANES), F32)
        for hh in range(hpg):
            hmask = (lane >= hh * dh) & (lane < (hh + 1) * dh)
            qm = jnp.where(hmask, q2, jnp.zeros_like(q2))
            dom = jnp.where(hmask, do2, jnp.zeros_like(do2))
            c_col = cq[:, hh:hh + 1]
            dl_col = dl[:, hh:hh + 1]

            def step(j, carry, masked):
                dq_acc, rs_acc = carry
                kj = k_ref[j]
                vj = v_ref[j]
                s = _dot_nt(qm, kj)
                if s_mul != 1.0:
                    s = s * s_mul
                e = s + (c_col - fk_ref[j, hh:hh + 1, :])
                if masked:
                    e = jnp.where(causal, e, MASK_VALUE)
                p = jnp.exp(e)
                dp = _dot_nt(dom, vj)
                ds = p * (dp - dl_col)
                dfk_ref[j, hh:hh + 1, :] += -jnp.sum(ds, axis=0, keepdims=True)
                dsb = ds.astype(BF16)
                dk_acc[j] += _dot_tn(dsb, qm)
                dv_acc[j] += _dot_tn(p.astype(BF16), dom)
                return dq_acc + _dot(dsb, kj), rs_acc + jnp.sum(ds, axis=1, keepdims=True)

            init = (jnp.zeros((tq, LANES), F32), jnp.zeros((tq, 1), F32))
            carry = lax.fori_loop(0, i, lambda j, cr: step(j, cr, False), init)
            dq_h, rs_h = step(i, carry, True)
            dq_all = jnp.where(hmask, dq_h, dq_all)
            dfq_all = jnp.where(lane == hh, rs_h, dfq_all)
        dq_ref[...] = (dq_all * dq_mul).astype(BF16)
        dfq_ref[...] = dfq_all

        @pl.when(i == nq - 1)
        def _():
            dkv = dk_acc[...]
            if s_mul != 1.0:
                dkv = dkv * s_mul
            dk_ref[...] = dkv.astype(BF16)
            dv_ref[...] = dv_acc[...].astype(BF16)

    blk = pl.BlockSpec((tq, LANES), lambda g, i: (i, g))
    res = pl.BlockSpec((nq, tq, LANES), lambda g, i: (0, 0, g))
    aux = pl.BlockSpec((None, tq, LANES), lambda g, i: (g, i, 0))
    rows = pl.BlockSpec((None, nq, SUBLANES, tq), lambda g, i: (g, 0, 0, 0))
    dq, dk3, dv3, dfq, dfk = _pcall(
        body, name="flash_bwd", grid=(G, nq),
        in_specs=[blk, res, res, blk, aux, aux, aux, rows],
        out_specs=[blk, res, res, aux, rows],
        out_shape=[jax.ShapeDtypeStruct((T, D), BF16), jax.ShapeDtypeStruct((nq, tq, D), BF16),
                   jax.ShapeDtypeStruct((nq, tq, D), BF16), jax.ShapeDtypeStruct((G, T, LANES), F32),
                   jax.ShapeDtypeStruct((G, nq, SUBLANES, tq), F32)],
        scratch_shapes=[pltpu.VMEM((nq, tq, LANES), F32), pltpu.VMEM((nq, tq, LANES), F32)],
        compiler_params=_params(("arbitrary", "arbitrary")),
    )(q, k3, v3, do, fq_aux, lse_aux, dl_aux, fk_rows)
    return dq, dk3.reshape(T, D), dv3.reshape(T, D), dfq, dfk


def _attn_out_fwd(o, x, wo, g_post, tm):
    T, D = x.shape

    def body(o_ref, x_ref, w_ref, g_ref, m_ref, xo_ref):
        m = _dot(o_ref[...], w_ref[...])
        m_ref[...] = m
        xo_ref[...] = x_ref[...] + _rms(m, g_ref[...])

    return _pcall(
        body, name="attn_out_fwd", grid=(T // tm,),
        in_specs=[_rows(tm, D), _rows(tm, D), _full((D, D)), _full((1, D))],
        out_specs=[_rows(tm, D), _rows(tm, D)],
        out_shape=[jax.ShapeDtypeStruct((T, D), F32), jax.ShapeDtypeStruct((T, D), F32)],
        compiler_params=_params(("arbitrary",)),
    )(o, x, wo, g_post)


def _attn_out_bwd(dxo, m, o, wo, g_post, head_ind, tm):
    T, D = m.shape

    def body(dxo_ref, m_ref, o_ref, w_ref, g_ref, ind_ref, dm_ref, do_ref, dl_ref, sums_ref):
        i = pl.program_id(0)
        dm, dgpost = _rms_bwd(m_ref[...], g_ref[...], dxo_ref[...])
        dmb = dm.astype(BF16)
        dm_ref[...] = dmb
        dob = _dot_nt(dmb, w_ref[...]).astype(BF16)
        do_ref[...] = dob
        dl_ref[...] = jnp.dot(dob.astype(F32) * o_ref[...], ind_ref[...], precision=lax.Precision.HIGHEST,
                              preferred_element_type=F32)

        @pl.when(i == 0)
        def _():
            sums_ref[...] = jnp.zeros_like(sums_ref)
        sums_ref[0:1, :] += dgpost

    return _pcall(
        body, name="attn_out_bwd", grid=(T // tm,),
        in_specs=[_rows(tm, D), _rows(tm, D), _rows(tm, D), _full((D, D)), _full((1, D)), _full((D, LANES))],
        out_specs=[_rows(tm, D), _rows(tm, D), _rows(tm, LANES), _acc((SUBLANES, D))],
        out_shape=[jax.ShapeDtypeStruct((T, D), BF16), jax.ShapeDtypeStruct((T, D), BF16),
                   jax.ShapeDtypeStruct((T, LANES), F32), jax.ShapeDtypeStruct((SUBLANES, D), F32)],
        compiler_params=_params(("arbitrary",)),
    )(dxo, m, o, wo, g_post, head_ind)


def _attn_in_bwd(dxo, x, g_pre, dq, dk, dv, dlf, lf, wqkv, wf, tm, n_heads):
    T, D = x.shape

    def body(dxo_ref, x_ref, g_ref, dq_ref, dk_ref, dv_ref, dlf_ref, lf_ref, w_ref, wf_ref,
             dxi_ref, h_ref, df_ref, sums_ref, dbf_ref):
        i = pl.program_id(0)
        xv = x_ref[...]
        h_ref[...] = _rms(xv, g_ref[...]).astype(BF16)
        lane = lax.broadcasted_iota(jnp.int32, (1, LANES), 1)
        df = jnp.where(lane < n_heads, dlf_ref[...] * (1.0 - jnp.exp(lf_ref[...])), 0.0)
        dfb = df.astype(BF16)
        df_ref[...] = dfb
        dh = (_dot_nt(dq_ref[...], w_ref[:, 0:D]) + _dot_nt(dk_ref[...], w_ref[:, D:2 * D])
              + _dot_nt(dv_ref[...], w_ref[:, 2 * D:3 * D]) + _dot_nt(dfb, wf_ref[...]))
        dxi, dgpre = _rms_bwd(xv, g_ref[...], dh)
        dxi_ref[...] = dxo_ref[...] + dxi

        @pl.when(i == 0)
        def _():
            sums_ref[...] = jnp.zeros_like(sums_ref)
            dbf_ref[...] = jnp.zeros_like(dbf_ref)
        sums_ref[0:1, :] += dgpre
        dbf_ref[...] += jnp.sum(df, axis=0, keepdims=True)

    return _pcall(
        body, name="attn_in_bwd", grid=(T // tm,),
        in_specs=[_rows(tm, D), _rows(tm, D), _full((1, D)), _rows(tm, D), _rows(tm, D), _rows(tm, D),
                  _rows(tm, LANES), _rows(tm, LANES), _full((D, 3 * D)), _full((D, LANES))],
        out_specs=[_rows(tm, D), _rows(tm, D), _rows(tm, LANES), _acc((SUBLANES, D)), _acc((1, LANES))],
        out_shape=[jax.ShapeDtypeStruct((T, D), F32), jax.ShapeDtypeStruct((T, D), BF16),
                   jax.ShapeDtypeStruct((T, LANES), BF16), jax.ShapeDtypeStruct((SUBLANES, D), F32),
                   jax.ShapeDtypeStruct((1, LANES), F32)],
        compiler_params=_params(("arbitrary",)),
    )(dxo, x, g_pre, dq, dk, dv, dlf, lf, wqkv, wf)


def _loss_head(y, target, tm):
    T, D = y.shape

    def body(y_ref, t_ref, dy_ref, loss_ref):
        i = pl.program_id(0)
        err = y_ref[...] - t_ref[...]
        dy_ref[...] = err * (1.0 / D)
        part = 0.5 * jnp.sum(jnp.mean(err * err, axis=-1, keepdims=True), axis=0, keepdims=True)

        @pl.when(i == 0)
        def _():
            loss_ref[...] = jnp.zeros_like(loss_ref)
        loss_ref[...] += part

    return _pcall(
        body, name="loss_head", grid=(T // tm,),
        in_specs=[_rows(tm, D), _rows(tm, D)],
        out_specs=[_rows(tm, D), _acc((SUBLANES, LANES))],
        out_shape=[jax.ShapeDtypeStruct((T, D), F32), jax.ShapeDtypeStruct((SUBLANES, LANES), F32)],
        compiler_params=_params(("arbitrary",)),
    )(y, target)


def _adamw(recv, w, m, v, tr, name):
    R, C = w.shape
    c1 = 1.0 - ADAM_B1 ** ADAM_STEP
    c2 = 1.0 - ADAM_B2 ** ADAM_STEP

    def body(r_ref, w_ref, m_ref, v_ref, g_ref, d_ref, nm_ref, nv_ref):
        g = r_ref[0].astype(F32)
        for s in range(1, N_DEV):
            g = g + r_ref[s].astype(F32)
        nm = ADAM_B1 * m_ref[...] + (1.0 - ADAM_B1) * g
        nv = ADAM_B2 * v_ref[...] + (1.0 - ADAM_B2) * jnp.square(g)
        m_hat = nm / c1
        v_hat = nv / c2
        g_ref[...] = g
        d_ref[...] = -ADAM_LR * (m_hat / (jnp.sqrt(v_hat) + ADAM_EPS) + ADAM_WD * w_ref[...])
        nm_ref[...] = nm
        nv_ref[...] = nv

    return _pcall(
        body, name=name, grid=(R // tr,),
        in_specs=[pl.BlockSpec((N_DEV, tr, C), lambda i: (0, i, 0))] + [_rows(tr, C)] * 3,
        out_specs=[_rows(tr, C)] * 4,
        out_shape=[jax.ShapeDtypeStruct((R, C), F32)] * 4,
        compiler_params=_params(("arbitrary",)),
    )(recv, w, m, v)


def _row_block(rows, cols):
    cap = max(SUBLANES, (256 * 1024) // max(cols, 1))
    best = None
    for t in range(SUBLANES, rows + 1, SUBLANES):
        if rows % t == 0 and t <= cap:
            best = t
    return rows if best is None else best


def kernel(x, g_mix_pre, g_mix_post, g_ffn_pre, g_ffn_post, conv_pw1_w, conv_pw1_b, conv_dw_w, conv_dw_b, conv_ln_g, conv_ln_b, conv_pw2_w, conv_pw2_b, attn_w_in, attn_b_f, attn_w_o, mlp_w_up, mlp_w_down, loss_target, m_g_mix_pre, m_g_mix_post, m_g_ffn_pre, m_g_ffn_post, m_conv_pw1_w, m_conv_pw1_b, m_conv_dw_w, m_conv_dw_b, m_conv_ln_g, m_conv_ln_b, m_conv_pw2_w, m_conv_pw2_b, m_attn_w_in, m_attn_b_f, m_attn_w_o, m_mlp_w_up, m_mlp_w_down, v_g_mix_pre, v_g_mix_post, v_g_ffn_pre, v_g_ffn_post, v_conv_pw1_w, v_conv_pw1_b, v_conv_dw_w, v_conv_dw_b, v_conv_ln_g, v_conv_ln_b, v_conv_pw2_w, v_conv_pw2_b, v_attn_w_in, v_attn_b_f, v_attn_w_o, v_mlp_w_up, v_mlp_w_down):
    _, T, D = x.shape
    H = attn_b_f.shape[-1]
    dh = D // H
    width = conv_dw_w.shape[1]
    cin = attn_w_in.shape[-1]
    fs = mlp_w_up.shape[-1]
    G = D // LANES
    hpg = LANES // dh
    assert T % 4 == 0 and D % LANES == 0 and LANES % dh == 0 and width <= CONV_HALO and H <= LANES

    tm = min(512, T // 4)
    tmb = min(256, T // 4)
    tq = tm
    tmc = min(256, T // 4)
    lc = min(256, D)
    tkw = min(1024, T // 4)
    tb = min(256, T // 4)
    nq = T // tq

    scale = float(dh) ** -0.5
    mant, _ = math.frexp(scale)
    q_mul = scale if mant == 0.5 else 1.0
    s_mul = 1.0 if mant == 0.5 else scale

    x2 = x.reshape(T, D)
    tgt = loss_target.reshape(T, D)

    gathered = _all_gather(
        [(conv_pw1_w, 0, BF16), (conv_dw_w, 0, F32), (conv_pw2_w, 0, BF16), (attn_w_in, 0, BF16),
         (attn_w_o, 0, BF16), (mlp_w_up, 0, BF16), (mlp_w_up, 1, BF16), (mlp_w_down, 0, BF16),
         (mlp_w_down, 1, BF16)], "gather_weights")
    w1g, dwg, w2g, wing, wog, wu0, wu1, wd0, wd1 = gathered
    w2 = w2g.reshape(D, D)
    wo = wog.reshape(D, D)
    dw_full = jnp.transpose(dwg, (1, 0, 2)).reshape(width, D)
    w32 = jnp.pad(dw_full, ((0, 32 - width), (0, 0)))
    win = jnp.transpose(wing, (1, 0, 2)).reshape(D, N_DEV * cin)
    wqkv = win[:, :3 * D]
    wf = jnp.pad(win[:, 3 * D:], ((0, 0), (0, LANES - H)))
    bf = jnp.pad(attn_b_f, ((0, 0), (0, LANES - H)))

    row = lambda a, i: a[i:i + 1]

    a0, u0 = _conv_in_fwd(x2, row(g_mix_pre, 0), w1g, conv_pw1_b, tm)
    y0 = _dwconv_fwd(u0, w32, conv_dw_b, tmc, lc, width)
    m0, x_1 = _conv_out_fwd(y0, x2, conv_ln_g, conv_ln_b, w2, conv_pw2_b, row(g_mix_post, 0), tm)
    up0, n0, x_2 = _mlp_fwd(x_1, row(g_ffn_pre, 0), wu0, wd0, row(g_ffn_post, 0), tm, "mlp0_fwd")

    q, k, v, lf = _attn_in_fwd(x_2, row(g_mix_pre, 1), wqkv, wf, bf, tm, q_mul, H)
    fcum = _cumsum_rows(lf, jnp.zeros_like(lf), tb, False, "forget_cumsum")

    def to_aux(t):
        t = jnp.transpose(t[:, :H].reshape(T, G, hpg), (1, 0, 2))
        return jnp.pad(t, ((0, 0), (0, 0), (0, LANES - hpg)))

    fq_aux = to_aux(fcum)
    fk_rows = jnp.transpose(fcum[:, :H].T.reshape(G, hpg, nq, tq), (0, 2, 1, 3))
    fk_rows = jnp.pad(fk_rows, ((0, 0), (0, 0), (0, SUBLANES - hpg), (0, 0)))
    o, o32, lse_aux = _flash_fwd(q, k, v, fq_aux, fk_rows, dh=dh, tq=tq, s_mul=s_mul)
    m1, x_3 = _attn_out_fwd(o, x_2, wo, row(g_mix_post, 1), tm)
    up1, n1, x_4 = _mlp_fwd(x_3, row(g_ffn_pre, 1), wu1, wd1, row(g_ffn_post, 1), tm, "mlp1_fwd")

    dx, loss_blk = _loss_head(x_4, tgt, tm)

    def mlp_back(dx, n_l, x_in, up_l, l, wu, wd):
        dxi, h, dm, dup, sums = _mlp_bwd(dx, n_l, x_in, up_l, row(g_ffn_pre, l), wu, wd, row(g_ffn_post, l),
                                         tmb, "mlp%d_bwd" % l)
        dwu = _mm_tn(h, dup, nj=N_DEV, a_cols=D, g_cols=fs, a_by_j=False, g_by_j=True, tk=tkw,
                     name="mlp%d_dwu" % l)
        dwd = _mm_tn(up_l, dm, nj=N_DEV, a_cols=fs, g_cols=D, a_by_j=True, g_by_j=False, tk=tkw,
                     name="mlp%d_dwd" % l, act=True)
        return dxi, dwu, dwd, sums

    dx, dwu1, dwd1, s_mlp1 = mlp_back(dx, n1, x_3, up1, 1, wu1, wd1)

    head_ind = jnp.asarray((np.arange(D)[:, None] // dh == np.arange(LANES)[None, :]).astype(np.float32))
    dm1, do, delta, s_ao = _attn_out_bwd(dx, m1, o32, wo, row(g_mix_post, 1), head_ind, tm)
    dwo = _mm_tn(o, dm1, nj=1, a_cols=D, g_cols=D, a_by_j=False, g_by_j=False, tk=tkw, name="attn_dwo")
    dq, dk, dv, dfq, dfk = _flash_bwd(q, k, v, do, fq_aux, lse_aux, to_aux(delta), fk_rows,
                                      dh=dh, tq=tq, s_mul=s_mul, dq_mul=scale)
    df_k = jnp.pad(jnp.transpose(dfk[:, :, :hpg, :], (0, 2, 1, 3)).reshape(H, T).T, ((0, 0), (0, LANES - H)))
    df_q = jnp.pad(jnp.transpose(dfq[:, :, :hpg], (1, 0, 2)).reshape(T, H), ((0, 0), (0, LANES - H)))
    dlf = _cumsum_rows(df_q, df_k, tb, True, "forget_cumsum_bwd")
    dx, h_at, df, s_ai, dbf = _attn_in_bwd(dx, x_2, row(g_mix_pre, 1), dq, dk, dv, dlf, lf, wqkv, wf, tm, H)
    dwq = _mm_tn(h_at, dq, nj=1, a_cols=D, g_cols=D, a_by_j=False, g_by_j=False, tk=tkw, name="attn_dwq")
    dwk = _mm_tn(h_at, dk, nj=1, a_cols=D, g_cols=D, a_by_j=False, g_by_j=False, tk=tkw, name="attn_dwk")
    dwv = _mm_tn(h_at, dv, nj=1, a_cols=D, g_cols=D, a_by_j=False, g_by_j=False, tk=tkw, name="attn_dwv")
    dwf = _mm_tn(h_at, df, nj=1, a_cols=D, g_cols=LANES, a_by_j=False, g_by_j=False, tk=tkw, name="attn_dwf")
    dwin = jnp.concatenate([dwq[0], dwk[0], dwv[0], dwf[0][:, :H]], axis=1)
    dwin = jnp.transpose(dwin.reshape(D, N_DEV, cin), (1, 0, 2))

    dx, dwu0, dwd0, s_mlp0 = mlp_back(dx, n0, x_1, up0, 0, wu0, wd0)

    dy0, dm0, z0, s_co = _conv_out_bwd(dx, m0, y0, conv_ln_g, conv_ln_b, w2, row(g_mix_post, 0), tm)
    dw2 = _mm_tn(z0, dm0, nj=1, a_cols=D, g_cols=D, a_by_j=False, g_by_j=False, tk=tkw, name="conv_dw2")
    du0, ddw = _dwconv_bwd(dy0, u0, w32, tmc, lc, width)
    grad_x, h_cv, da0, s_ci, db1 = _conv_in_bwd(dx, du0, a0, x2, row(g_mix_pre, 0), w1g, tm)
    dw1 = _mm_tn(h_cv, da0, nj=N_DEV, a_cols=D, g_cols=(2 * D) // N_DEV, a_by_j=False, g_by_j=True, tk=tkw,
                 name="conv_dw1")
    ddw_s = jnp.transpose(ddw[:width].reshape(width, N_DEV, D // N_DEV), (1, 0, 2))

    def pad_row(a):
        return jnp.pad(a, ((0, 0), (0, D - a.shape[1])))

    def pack(gmp, gmq, gfp, gfq, b1, dwb, lng, lnb, b2, bfv, last):
        return jnp.concatenate([gmp, gmq, gfp, gfq, b1.reshape(2, D), dwb, lng, lnb, b2, pad_row(bfv), last],
                               axis=0)

    zero_row = jnp.zeros((1, D), F32)
    small_g = pack(
        jnp.concatenate([row(s_ci, 0), row(s_ai, 0)], axis=0),
        jnp.concatenate([row(s_co, 0), row(s_ao, 0)], axis=0),
        jnp.concatenate([row(s_mlp0, 1), row(s_mlp1, 1)], axis=0),
        jnp.concatenate([row(s_mlp0, 0), row(s_mlp1, 0)], axis=0),
        db1, row(s_co, 4), row(s_co, 1), row(s_co, 2), row(s_co, 3), dbf[:, :H],
        pad_row(loss_blk[0:1, 0:1]))
    rs = D // N_DEV
    (r_w1, r_dw, r_w2, r_win, r_wo, r_wu, r_wd), r_small = _exchange(
        [(dw1, 0, None), (ddw_s, 1, None), (dw2.reshape(N_DEV, rs, D), 2, None), (dwin, 3, None),
         (dwo.reshape(N_DEV, rs, D), 4, None), (dwu0, 5, 0), (dwu1, 5, 1), (dwd0, 6, 0), (dwd1, 6, 1)],
        [((N_DEV, D, (2 * D) // N_DEV), BF16), ((N_DEV, width, D // N_DEV), F32), ((N_DEV, rs, D), BF16),
         ((N_DEV, D, cin), BF16), ((N_DEV, rs, D), BF16), ((N_DEV, 2, D, fs), BF16),
         ((N_DEV, 2, fs, D), BF16)],
        small_g, "exchange_grads")

    def opt(recv, w, m, v, name):
        shp = w.shape
        C = shp[-1]
        R = int(np.prod(shp[:-1]))
        outs = _adamw(recv.reshape(N_DEV, R, C), w.reshape(R, C), m.reshape(R, C), v.reshape(R, C),
                      _row_block(R, C), name)
        return [t.reshape(shp) for t in outs]

    big = {
        "conv_pw1_w": opt(r_w1, conv_pw1_w, m_conv_pw1_w, v_conv_pw1_w, "adamw_pw1"),
        "conv_dw_w": opt(r_dw, conv_dw_w, m_conv_dw_w, v_conv_dw_w, "adamw_dw"),
        "conv_pw2_w": opt(r_w2, conv_pw2_w, m_conv_pw2_w, v_conv_pw2_w, "adamw_pw2"),
        "attn_w_in": opt(r_win, attn_w_in, m_attn_w_in, v_attn_w_in, "adamw_win"),
        "attn_w_o": opt(r_wo, attn_w_o, m_attn_w_o, v_attn_w_o, "adamw_wo"),
        "mlp_w_up": opt(r_wu, mlp_w_up, m_mlp_w_up, v_mlp_w_up, "adamw_wup"),
        "mlp_w_down": opt(r_wd, mlp_w_down, m_mlp_w_down, v_mlp_w_down, "adamw_wdown"),
    }
    small_w = pack(g_mix_pre, g_mix_post, g_ffn_pre, g_ffn_post, conv_pw1_b, conv_dw_b, conv_ln_g, conv_ln_b,
                   conv_pw2_b, attn_b_f, zero_row)
    small_m = pack(m_g_mix_pre, m_g_mix_post, m_g_ffn_pre, m_g_ffn_post, m_conv_pw1_b, m_conv_dw_b, m_conv_ln_g,
                   m_conv_ln_b, m_conv_pw2_b, m_attn_b_f, zero_row)
    small_v = pack(v_g_mix_pre, v_g_mix_post, v_g_ffn_pre, v_g_ffn_post, v_conv_pw1_b, v_conv_dw_b, v_conv_ln_g,
                   v_conv_ln_b, v_conv_pw2_b, v_attn_b_f, zero_row)
    sm = _adamw(r_small, small_w, small_m, small_v, small_w.shape[0], "adamw_small")
    loss = sm[0][15, 0]

    def unpack(t):
        return {"g_mix_pre": t[0:2], "g_mix_post": t[2:4], "g_ffn_pre": t[4:6], "g_ffn_post": t[6:8],
                "conv_pw1_b": t[8:10].reshape(1, 2 * D), "conv_dw_b": t[10:11], "conv_ln_g": t[11:12],
                "conv_ln_b": t[12:13], "conv_pw2_b": t[13:14], "attn_b_f": t[14:15, :H]}

    small = [unpack(t) for t in sm]
    names = ["g_mix_pre", "g_mix_post", "g_ffn_pre", "g_ffn_post", "conv_pw1_w", "conv_pw1_b", "conv_dw_w",
             "conv_dw_b", "conv_ln_g", "conv_ln_b", "conv_pw2_w", "conv_pw2_b", "attn_w_in", "attn_b_f",
             "attn_w_o", "mlp_w_up", "mlp_w_down"]
    outs = [loss, grad_x.reshape(1, T, D)]
    for kind in range(4):
        for nme in names:
            outs.append(big[nme][kind] if nme in big else small[kind][nme])
    return tuple(outs)
```

```python
import functools
import math

import numpy as np
import jax
import jax.numpy as jnp
from jax import lax
from jax.experimental import pallas as pl
from jax.experimental.pallas import tpu as pltpu

F32 = jnp.float32
BF16 = jnp.bfloat16

RMS_EPS = 1e-6
LN_EPS = 1e-5
MASK_VALUE = -1e30
ADAM_LR = 0.001
ADAM_B1 = 0.9
ADAM_B2 = 0.999
ADAM_EPS = 1e-08
ADAM_WD = 0.01
ADAM_STEP = 10

N_DEV = 8
LANES = 128
SUBLANES = 8
CONV_HALO = 32
CONV_ROWS = 32
FLASH_ROWS = 32
VMEM_LIMIT = 56 * 1024 * 1024

_pcall = pl.pallas_call


def _params(sem=None):
    if sem is None:
        return pltpu.CompilerParams(vmem_limit_bytes=VMEM_LIMIT)
    return pltpu.CompilerParams(dimension_semantics=sem, vmem_limit_bytes=VMEM_LIMIT)


def _dot(a, b):
    return jnp.dot(a, b, preferred_element_type=F32)


def _dot_nt(a, b):
    return lax.dot_general(a, b, (((1,), (1,)), ((), ())), preferred_element_type=F32)


def _dot_tn(a, b):
    return lax.dot_general(a, b, (((0,), (0,)), ((), ())), preferred_element_type=F32)


def _full(shape):
    nd = len(shape)
    return pl.BlockSpec(shape, lambda *g: (0,) * nd, pipeline_mode=pl.Buffered(1))


def _acc(shape):
    nd = len(shape)
    return pl.BlockSpec(shape, lambda *g: (0,) * nd)


def _rows(tm, cols):
    return pl.BlockSpec((tm, cols), lambda i: (i, 0))


def _rms(x, g):
    r = lax.rsqrt(jnp.mean(x * x, axis=-1, keepdims=True) + RMS_EPS)
    return x * r * g


def _rms_bwd(x, g, dy):
    r = lax.rsqrt(jnp.mean(x * x, axis=-1, keepdims=True) + RMS_EPS)
    n = x * r
    dg = jnp.sum(dy * n, axis=0, keepdims=True)
    dn = dy * g
    dx = r * (dn - n * jnp.mean(dn * n, axis=-1, keepdims=True))
    return dx, dg


def _sigmoid(x):
    return 1.0 / (1.0 + jnp.exp(-x))


def _mesh_pos():
    return lax.axis_index("x"), lax.axis_index("y"), lax.axis_index("c")


def _dev_index(px, py, pc):
    return 4 * px + 2 * py + pc


def _all_gather(items, name):
    n = len(items)
    arrs = []
    for it in items:
        if not any(it[0] is a for a in arrs):
            arrs.append(it[0])
    n_in = len(arrs)
    which = [[it[0] is a for a in arrs].index(True) for it in items]
    shapes = [it[0].shape if it[1] is None else it[0].shape[1:] for it in items]
    dtypes = [it[2] for it in items]

    def body(*refs):
        ins = [refs[w] for w in which]
        outs = refs[n_in:n_in + n]
        stage = refs[n_in + n:n_in + 2 * n]
        send_sems, recv_sems, local_sems = refs[n_in + 2 * n:]
        x, y, c = _mesh_pos()
        me, sib = (x, y, c), (x, y, 1 - c)
        chips = [(1 - x, y), (x, 1 - y), (1 - x, 1 - y)]

        def slot(a, block):
            return outs[a].at[_dev_index(*block)]

        def rcopy(a, k, block, to, src=None):
            dst = slot(a, block)
            return pltpu.make_async_remote_copy(
                src_ref=dst if src is None else src, dst_ref=dst,
                send_sem=send_sems.at[a, k], recv_sem=recv_sems.at[a, k],
                device_id=to, device_id_type=pl.DeviceIdType.MESH)

        mine = []
        for a in range(n):
            src = ins[a] if items[a][1] is None else ins[a].at[items[a][1]]
            stage[a][...] = src[...].astype(dtypes[a])
            cp = pltpu.make_async_copy(stage[a], slot(a, me), local_sems.at[a])
            cp.start()
            mine.append(cp)
        first = []
        for a in range(n):
            first.append(rcopy(a, 0, me, sib, src=stage[a]))
            for j, chip in enumerate(chips):
                first.append(rcopy(a, 1 + j, me, (*chip, c), src=stage[a]))
        for cp in first:
            cp.start()
        passed = []
        for j, chip in enumerate(chips):
            for a in range(n):
                rcopy(a, 1 + j, (*chip, c), me).wait_recv()
                cp = rcopy(a, 4 + j, (*chip, c), sib)
                cp.start()
                passed.append(cp)
        for a in range(n):
            rcopy(a, 0, sib, me).wait_recv()
            for j, chip in enumerate(chips):
                rcopy(a, 4 + j, (*chip, 1 - c), me).wait_recv()
        for cp in first + passed:
            cp.wait_send()
        for cp in mine:
            cp.wait()

    return _pcall(
        body, name=name,
        out_shape=[jax.ShapeDtypeStruct((N_DEV,) + tuple(s), d) for s, d in zip(shapes, dtypes)],
        in_specs=[pl.BlockSpec(memory_space=pltpu.VMEM)] * n_in,
        out_specs=[pl.BlockSpec(memory_space=pl.ANY)] * n,
        scratch_shapes=[pltpu.VMEM(tuple(s), d) for s, d in zip(shapes, dtypes)]
        + [pltpu.SemaphoreType.DMA((n, 7)), pltpu.SemaphoreType.DMA((n, 7)), pltpu.SemaphoreType.DMA((n,))],
        compiler_params=pltpu.CompilerParams(vmem_limit_bytes=VMEM_LIMIT),
    )(*arrs)


def _exchange(items, out_shapes, small, name):
    n = len(items)
    n_out = len(out_shapes)
    arrs = [it[0] for it in items]

    def body(*refs):
        ins = refs[:n]
        sm = refs[n]
        outs = refs[n + 1:n + 1 + n_out]
        smo = refs[n + 1 + n_out]
        send_sems, recv_sems, local_sems = refs[n + 2 + n_out:]
        x, y, c = _mesh_pos()
        me_id = _dev_index(x, y, c)

        def peer(r):
            p = ((1 - x) if r & 4 else x, (1 - y) if r & 2 else y, (1 - c) if r & 1 else c)
            return p, _dev_index(*p)

        def land(a, sender_id):
            if a == n:
                return smo.at[sender_id]
            _, oi, sub = items[a]
            return outs[oi].at[sender_id] if sub is None else outs[oi].at[sender_id, sub]

        def give(a, to_id):
            return sm if a == n else ins[a].at[to_id]

        sends, mine = [], []
        for a in range(n + 1):
            cp = pltpu.make_async_copy(give(a, me_id), land(a, me_id), local_sems.at[a])
            cp.start()
            mine.append(cp)
            for r in range(1, N_DEV):
                p, pid = peer(r)
                cp = pltpu.make_async_remote_copy(
                    src_ref=give(a, pid), dst_ref=land(a, me_id),
                    send_sem=send_sems.at[a, r - 1], recv_sem=recv_sems.at[a, r - 1],
                    device_id=p, device_id_type=pl.DeviceIdType.MESH)
                cp.start()
                sends.append(cp)
        for a in range(n + 1):
            for r in range(1, N_DEV):
                p, pid = peer(r)
                pltpu.make_async_remote_copy(
                    src_ref=land(a, pid), dst_ref=land(a, pid),
                    send_sem=send_sems.at[a, r - 1], recv_sem=recv_sems.at[a, r - 1],
                    device_id=p, device_id_type=pl.DeviceIdType.MESH).wait_recv()
        for cp in sends:
            cp.wait_send()
        for cp in mine:
            cp.wait()

    res = _pcall(
        body, name=name,
        out_shape=[jax.ShapeDtypeStruct(tuple(s), d) for s, d in out_shapes]
        + [jax.ShapeDtypeStruct((N_DEV,) + small.shape, small.dtype)],
        in_specs=[pl.BlockSpec(memory_space=pl.ANY)] * (n + 1),
        out_specs=[pl.BlockSpec(memory_space=pl.ANY)] * (n_out + 1),
        scratch_shapes=[pltpu.SemaphoreType.DMA((n + 1, 7)), pltpu.SemaphoreType.DMA((n + 1, 7)),
                        pltpu.SemaphoreType.DMA((n + 1,))],
    )(*arrs, small)
    return res[:n_out], res[n_out]


def _conv_in_fwd(x, g_pre, w1g, b1, tm):
    T, D = x.shape
    ns, _, cs = w1g.shape
    half = ns // 2

    def body(x_ref, g_ref, w_ref, b_ref, a_ref, u_ref):
        h = _rms(x_ref[...], g_ref[...]).astype(BF16)
        parts = []
        for s in range(ns):
            a_s = _dot(h, w_ref[s]) + b_ref[:, s * cs:(s + 1) * cs]
            a_ref[:, s * cs:(s + 1) * cs] = a_s
            parts.append(a_s)
        for s in range(half):
            u_ref[:, s * cs:(s + 1) * cs] = parts[s] * _sigmoid(parts[s + half])

    return _pcall(
        body, name="conv_in_fwd", grid=(T // tm,),
        in_specs=[_rows(tm, D), _full((1, D)), _full(w1g.shape), _full((1, 2 * D))],
        out_specs=[_rows(tm, 2 * D), _rows(tm, D)],
        out_shape=[jax.ShapeDtypeStruct((T, 2 * D), F32), jax.ShapeDtypeStruct((T, D), F32)],
        compiler_params=_params(("arbitrary",)),
    )(x, g_pre, w1g, b1)


def _dwconv_fwd(u, w32, b, tm, lc, width):
    T, D = u.shape
    hb = tm // CONV_HALO

    def body(u_ref, halo_ref, w_ref, b_ref, y_ref, ext_ref):
        i = pl.program_id(0)
        ext_ref[0:CONV_HALO, :] = jnp.where(i > 0, halo_ref[...], 0.0)
        ext_ref[CONV_HALO:, :] = u_ref[...]
        for r0 in range(0, tm, CONV_ROWS):
            for l0 in range(0, lc, LANES):
                ls = slice(l0, l0 + LANES)
                acc = jnp.zeros((CONV_ROWS, LANES), F32) + b_ref[:, ls]
                for j in range(width):
                    off = CONV_HALO - (width - 1) + j + r0
                    acc = acc + w_ref[j:j + 1, ls] * ext_ref[off:off + CONV_ROWS, ls]
                y_ref[r0:r0 + CONV_ROWS, ls] = acc

    return _pcall(
        body, name="dwconv_fwd", grid=(T // tm, D // lc),
        in_specs=[pl.BlockSpec((tm, lc), lambda i, l: (i, l)),
                  pl.BlockSpec((CONV_HALO, lc), lambda i, l: (jnp.maximum(i * hb - 1, 0), l)),
                  pl.BlockSpec((32, lc), lambda i, l: (0, l)),
                  pl.BlockSpec((1, lc), lambda i, l: (0, l))],
        out_specs=pl.BlockSpec((tm, lc), lambda i, l: (i, l)),
        out_shape=jax.ShapeDtypeStruct((T, D), F32),
        scratch_shapes=[pltpu.VMEM((tm + CONV_HALO, lc), F32)],
        compiler_params=_params(("arbitrary", "arbitrary")),
    )(u, u, w32, b)


def _ln_parts(y, g, b):
    mu = jnp.mean(y, axis=-1, keepdims=True)
    yc = y - mu
    rstd = lax.rsqrt(jnp.mean(yc * yc, axis=-1, keepdims=True) + LN_EPS)
    yhat = yc * rstd
    return yhat, rstd, yhat * g + b


def _conv_out_fwd(y, x, ln_g, ln_b, w2, b2, g_post, tm):
    T, D = x.shape

    def body(y_ref, x_ref, lg_ref, lb_ref, w_ref, b_ref, g_ref, m_ref, xo_ref):
        _, _, yn = _ln_parts(y_ref[...], lg_ref[...], lb_ref[...])
        z = (yn * _sigmoid(yn)).astype(BF16)
        m = _dot(z, w_ref[...]) + b_ref[...]
        m_ref[...] = m
        xo_ref[...] = x_ref[...] + _rms(m, g_ref[...])

    return _pcall(
        body, name="conv_out_fwd", grid=(T // tm,),
        in_specs=[_rows(tm, D), _rows(tm, D), _full((1, D)), _full((1, D)), _full((D, D)), _full((1, D)),
                  _full((1, D))],
        out_specs=[_rows(tm, D), _rows(tm, D)],
        out_shape=[jax.ShapeDtypeStruct((T, D), F32), jax.ShapeDtypeStruct((T, D), F32)],
        compiler_params=_params(("arbitrary",)),
    )(y, x, ln_g, ln_b, w2, b2, g_post)


def _conv_out_bwd(dxo, m, y, ln_g, ln_b, w2, g_post, tm):
    T, D = m.shape

    def body(dxo_ref, m_ref, y_ref, lg_ref, lb_ref, w_ref, g_ref, dy_ref, dm_ref, z_ref, sums_ref):
        i = pl.program_id(0)
        dm, dgpost = _rms_bwd(m_ref[...], g_ref[...], dxo_ref[...])
        dmb = dm.astype(BF16)
        dm_ref[...] = dmb
        yhat, rstd, yn = _ln_parts(y_ref[...], lg_ref[...], lb_ref[...])
        sg = _sigmoid(yn)
        z_ref[...] = (yn * sg).astype(BF16)
        dz = _dot_nt(dmb, w_ref[...])
        dyn = dz * (sg + yn * sg * (1.0 - sg))
        dyh = dyn * lg_ref[...]
        dy = rstd * (dyh - jnp.mean(dyh, axis=-1, keepdims=True)
                     - yhat * jnp.mean(dyh * yhat, axis=-1, keepdims=True))
        dy_ref[...] = dy

        @pl.when(i == 0)
        def _():
            sums_ref[...] = jnp.zeros_like(sums_ref)
        sums_ref[0:1, :] += dgpost
        sums_ref[1:2, :] += jnp.sum(dyn * yhat, axis=0, keepdims=True)
        sums_ref[2:3, :] += jnp.sum(dyn, axis=0, keepdims=True)
        sums_ref[3:4, :] += jnp.sum(dm, axis=0, keepdims=True)
        sums_ref[4:5, :] += jnp.sum(dy, axis=0, keepdims=True)

    return _pcall(
        body, name="conv_out_bwd", grid=(T // tm,),
        in_specs=[_rows(tm, D), _rows(tm, D), _rows(tm, D), _full((1, D)), _full((1, D)), _full((D, D)),
                  _full((1, D))],
        out_specs=[_rows(tm, D), _rows(tm, D), _rows(tm, D), _acc((SUBLANES, D))],
        out_shape=[jax.ShapeDtypeStruct((T, D), F32), jax.ShapeDtypeStruct((T, D), BF16),
                   jax.ShapeDtypeStruct((T, D), BF16), jax.ShapeDtypeStruct((SUBLANES, D), F32)],
        compiler_params=_params(("arbitrary",)),
    )(dxo, m, y, ln_g, ln_b, w2, g_post)


def _dwconv_bwd(dy, u, w32, tm, lc, width):
    T, D = u.shape
    hb = tm // CONV_HALO
    nt = T // tm
    last_halo = T // CONV_HALO - 1

    def body(dy_ref, dyn_ref, u_ref, up_ref, w_ref, du_ref, dw_ref, exty_ref, extu_ref, acc_ref):
        i = pl.program_id(1)
        exty_ref[0:tm, :] = dy_ref[...]
        exty_ref[tm:, :] = jnp.where(i < nt - 1, dyn_ref[...], 0.0)
        extu_ref[0:CONV_HALO, :] = jnp.where(i > 0, up_ref[...], 0.0)
        extu_ref[CONV_HALO:, :] = u_ref[...]

        @pl.when(i == 0)
        def _():
            acc_ref[...] = jnp.zeros_like(acc_ref)

        for r0 in range(0, tm, CONV_ROWS):
            for l0 in range(0, lc, LANES):
                ls = slice(l0, l0 + LANES)
                dyc = exty_ref[r0:r0 + CONV_ROWS, ls]
                du = jnp.zeros((CONV_ROWS, LANES), F32)
                for j in range(width):
                    oy = r0 + (width - 1) - j
                    du = du + w_ref[j:j + 1, ls] * exty_ref[oy:oy + CONV_ROWS, ls]
                    ou = CONV_HALO - (width - 1) + j + r0
                    prod = dyc * extu_ref[ou:ou + CONV_ROWS, ls]
                    acc_ref[j, :, ls] += prod.reshape(CONV_ROWS // SUBLANES, SUBLANES, LANES).sum(axis=0)
                du_ref[r0:r0 + CONV_ROWS, ls] = du

        @pl.when(i == nt - 1)
        def _():
            for j in range(32):
                dw_ref[j:j + 1, :] = jnp.sum(acc_ref[j], axis=0, keepdims=True)

    return _pcall(
        body, name="dwconv_bwd", grid=(D // lc, nt),
        in_specs=[pl.BlockSpec((tm, lc), lambda l, i: (i, l)),
                  pl.BlockSpec((CONV_HALO, lc), lambda l, i: (jnp.minimum((i + 1) * hb, last_halo), l)),
                  pl.BlockSpec((tm, lc), lambda l, i: (i, l)),
                  pl.BlockSpec((CONV_HALO, lc), lambda l, i: (jnp.maximum(i * hb - 1, 0), l)),
                  pl.BlockSpec((32, lc), lambda l, i: (0, l))],
        out_specs=[pl.BlockSpec((tm, lc), lambda l, i: (i, l)),
                   pl.BlockSpec((32, lc), lambda l, i: (0, l))],
        out_shape=[jax.ShapeDtypeStruct((T, D), F32), jax.ShapeDtypeStruct((32, D), F32)],
        scratch_shapes=[pltpu.VMEM((tm + CONV_HALO, lc), F32), pltpu.VMEM((tm + CONV_HALO, lc), F32),
                        pltpu.VMEM((32, SUBLANES, lc), F32)],
        compiler_params=_params(("arbitrary", "arbitrary")),
    )(dy, dy, u, u, w32)


def _conv_in_bwd(dxo, du, a, x, g_pre, w1g, tm):
    T, D = x.shape
    ns, _, cs = w1g.shape
    half = ns // 2

    def body(dxo_ref, du_ref, a_ref, x_ref, g_ref, w_ref, dxi_ref, h_ref, da_ref, sums_ref, db_ref):
        i = pl.program_id(0)
        xv = x_ref[...]
        h_ref[...] = _rms(xv, g_ref[...]).astype(BF16)
        dh = jnp.zeros((tm, D), F32)
        dbs = [None] * ns
        for s in range(half):
            a_u = a_ref[:, s * cs:(s + 1) * cs]
            sg = _sigmoid(a_ref[:, (s + half) * cs:(s + half + 1) * cs])
            du_s = du_ref[:, s * cs:(s + 1) * cs]
            da_u = du_s * sg
            da_g = du_s * a_u * sg * (1.0 - sg)
            for s2, v in ((s, da_u), (s + half, da_g)):
                vb = v.astype(BF16)
                da_ref[:, s2 * cs:(s2 + 1) * cs] = vb
                dbs[s2] = jnp.sum(v, axis=0, keepdims=True)
                dh = dh + _dot_nt(vb, w_ref[s2])
        dxi, dgpre = _rms_bwd(xv, g_ref[...], dh)
        dxi_ref[...] = dxo_ref[...] + dxi

        @pl.when(i == 0)
        def _():
            sums_ref[...] = jnp.zeros_like(sums_ref)
            db_ref[...] = jnp.zeros_like(db_ref)
        sums_ref[0:1, :] += dgpre
        for s in range(ns):
            db_ref[:, s * cs:(s + 1) * cs] += dbs[s]

    return _pcall(
        body, name="conv_in_bwd", grid=(T // tm,),
        in_specs=[_rows(tm, D), _rows(tm, D), _rows(tm, 2 * D), _rows(tm, D), _full((1, D)),
                  _full(w1g.shape)],
        out_specs=[_rows(tm, D), _rows(tm, D), _rows(tm, 2 * D), _acc((SUBLANES, D)), _acc((1, 2 * D))],
        out_shape=[jax.ShapeDtypeStruct((T, D), F32), jax.ShapeDtypeStruct((T, D), BF16),
                   jax.ShapeDtypeStruct((T, 2 * D), BF16), jax.ShapeDtypeStruct((SUBLANES, D), F32),
                   jax.ShapeDtypeStruct((1, 2 * D), F32)],
        compiler_params=_params(("arbitrary",)),
    )(dxo, du, a, x, g_pre, w1g)


def _mlp_fwd(x, g_pre, wug, wdg, g_post, tm, name):
    T, D = x.shape
    ns, _, fs = wug.shape

    def body(x_ref, gp_ref, wu_ref, wd_ref, gq_ref, up_ref, m_ref, xo_ref):
        xv = x_ref[...]
        h = _rms(xv, gp_ref[...]).astype(BF16)
        acc = jnp.zeros((tm, D), F32)
        for s in range(ns):
            up = _dot(h, wu_ref[s]).astype(BF16)
            up_ref[:, s * fs:(s + 1) * fs] = up
            act = jnp.square(jnp.maximum(up.astype(F32), 0.0)).astype(BF16)
            acc = acc + _dot(act, wd_ref[s])
        m_ref[...] = acc
        xo_ref[...] = xv + _rms(acc, gq_ref[...])

    return _pcall(
        body, name=name, grid=(T // tm,),
        in_specs=[_rows(tm, D), _full((1, D)), _full(wug.shape), _full(wdg.shape), _full((1, D))],
        out_specs=[_rows(tm, ns * fs), _rows(tm, D), _rows(tm, D)],
        out_shape=[jax.ShapeDtypeStruct((T, ns * fs), BF16), jax.ShapeDtypeStruct((T, D), F32),
                   jax.ShapeDtypeStruct((T, D), F32)],
        compiler_params=_params(("arbitrary",)),
    )(x, g_pre, wug, wdg, g_post)


def _mlp_bwd(dxo, m, x, up, g_pre, wug, wdg, g_post, tm, name):
    T, D = x.shape
    ns, _, fs = wug.shape

    def body(dxo_ref, m_ref, x_ref, up_ref, gp_ref, wu_ref, wd_ref, gq_ref,
             dxi_ref, h_ref, dm_ref, dup_ref, sums_ref):
        i = pl.program_id(0)
        dxo = dxo_ref[...]
        dm, dgpost = _rms_bwd(m_ref[...], gq_ref[...], dxo)
        dmb = dm.astype(BF16)
        dm_ref[...] = dmb
        xv = x_ref[...]
        h_ref[...] = _rms(xv, gp_ref[...]).astype(BF16)
        dh = jnp.zeros((tm, D), F32)
        for s in range(ns):
            dact = _dot_nt(dmb, wd_ref[s])
            up = up_ref[:, s * fs:(s + 1) * fs].astype(F32)
            dup = (dact * (2.0 * jnp.maximum(up, 0.0))).astype(BF16)
            dup_ref[:, s * fs:(s + 1) * fs] = dup
            dh = dh + _dot_nt(dup, wu_ref[s])
        dxi, dgpre = _rms_bwd(xv, gp_ref[...], dh)
        dxi_ref[...] = dxo + dxi

        @pl.when(i == 0)
        def _():
            sums_ref[...] = jnp.zeros_like(sums_ref)
        sums_ref[0:1, :] += dgpost
        sums_ref[1:2, :] += dgpre

    return _pcall(
        body, name=name, grid=(T // tm,),
        in_specs=[_rows(tm, D), _rows(tm, D), _rows(tm, D), _rows(tm, ns * fs), _full((1, D)),
                  _full(wug.shape), _full(wdg.shape), _full((1, D))],
        out_specs=[_rows(tm, D), _rows(tm, D), _rows(tm, D), _rows(tm, ns * fs), _acc((SUBLANES, D))],
        out_shape=[jax.ShapeDtypeStruct((T, D), F32), jax.ShapeDtypeStruct((T, D), BF16),
                   jax.ShapeDtypeStruct((T, D), BF16), jax.ShapeDtypeStruct((T, ns * fs), BF16),
                   jax.ShapeDtypeStruct((SUBLANES, D), F32)],
        compiler_params=_params(("arbitrary",)),
    )(dxo, m, x, up, g_pre, wug, wdg, g_post)


def _mm_tn(a, g, *, nj, a_cols, g_cols, a_by_j, g_by_j, tk, name, act=False):
    T = a.shape[0]
    nk = T // tk

    def body(a_ref, g_ref, o_ref, acc_ref):
        k = pl.program_id(1)
        av = a_ref[...]
        if act:
            av = jnp.square(jnp.maximum(av.astype(F32), 0.0)).astype(BF16)
        p = _dot_tn(av, g_ref[...])

        @pl.when(k == 0)
        def _():
            acc_ref[...] = p

        @pl.when(k > 0)
        def _():
            acc_ref[...] += p

        @pl.when(k == nk - 1)
        def _():
            o_ref[...] = acc_ref[...].astype(BF16)

    return _pcall(
        body, name=name, grid=(nj, nk),
        in_specs=[pl.BlockSpec((tk, a_cols), (lambda j, k: (k, j)) if a_by_j else (lambda j, k: (k, 0))),
                  pl.BlockSpec((tk, g_cols), (lambda j, k: (k, j)) if g_by_j else (lambda j, k: (k, 0)))],
        out_specs=pl.BlockSpec((None, a_cols, g_cols), lambda j, k: (j, 0, 0)),
        out_shape=jax.ShapeDtypeStruct((nj, a_cols, g_cols), BF16),
        scratch_shapes=[pltpu.VMEM((a_cols, g_cols), F32)],
        compiler_params=_params(("arbitrary", "arbitrary")),
    )(a, g)


def _attn_in_fwd(x, g_pre, wqkv, wf, bf, tm, q_mul, n_heads):
    T, D = x.shape

    def body(x_ref, g_ref, w_ref, wf_ref, bf_ref, q_ref, k_ref, v_ref, lf_ref):
        h = _rms(x_ref[...], g_ref[...]).astype(BF16)
        q = _dot(h, w_ref[:, 0:D])
        if q_mul != 1.0:
            q = q * q_mul
        q_ref[...] = q.astype(BF16)
        k_ref[...] = _dot(h, w_ref[:, D:2 * D]).astype(BF16)
        v_ref[...] = _dot(h, w_ref[:, 2 * D:3 * D]).astype(BF16)
        fl = _dot(h, wf_ref[...]) + bf_ref[...]
        lf = jnp.minimum(fl, 0.0) - jnp.log(1.0 + jnp.exp(-jnp.abs(fl)))
        lane = lax.broadcasted_iota(jnp.int32, (1, LANES), 1)
        lf_ref[...] = jnp.where(lane < n_heads, lf, 0.0)

    return _pcall(
        body, name="attn_in_fwd", grid=(T // tm,),
        in_specs=[_rows(tm, D), _full((1, D)), _full((D, 3 * D)), _full((D, LANES)), _full((1, LANES))],
        out_specs=[_rows(tm, D), _rows(tm, D), _rows(tm, D), _rows(tm, LANES)],
        out_shape=[jax.ShapeDtypeStruct((T, D), BF16)] * 3 + [jax.ShapeDtypeStruct((T, LANES), F32)],
        compiler_params=_params(("arbitrary",)),
    )(x, g_pre, wqkv, wf, bf)


def _cumsum_rows(v, v2, tb, reverse, name):
    T, C = v.shape
    nb = T // tb

    def body(v_ref, v2_ref, o_ref, carry_ref):
        i = pl.program_id(0)

        @pl.when(i == 0)
        def _():
            carry_ref[...] = jnp.zeros_like(carry_ref)
        r = lax.broadcasted_iota(jnp.int32, (tb, tb), 0)
        c = lax.broadcasted_iota(jnp.int32, (tb, tb), 1)
        tri = jnp.where((c >= r) if reverse else (c <= r), 1.0, 0.0).astype(F32)
        out = jnp.dot(tri, v_ref[...] + v2_ref[...], precision=lax.Precision.HIGHEST,
                      preferred_element_type=F32) + carry_ref[...]
        o_ref[...] = out
        carry_ref[...] = out[0:1, :] if reverse else out[tb - 1:tb, :]

    idx = (lambda i: (nb - 1 - i, 0)) if reverse else (lambda i: (i, 0))
    return _pcall(
        body, name=name, grid=(nb,),
        in_specs=[pl.BlockSpec((tb, C), idx), pl.BlockSpec((tb, C), idx)],
        out_specs=pl.BlockSpec((tb, C), idx),
        out_shape=jax.ShapeDtypeStruct((T, C), F32),
        scratch_shapes=[pltpu.VMEM((1, C), F32)],
        compiler_params=_params(("arbitrary",)),
    )(v, v2)


def _flash_fwd(q, k, v, fq_aux, fk_rows, *, dh, tq, s_mul):
    T, D = q.shape
    G = D // LANES
    hpg = LANES // dh
    nq = T // tq
    k3 = k.reshape(nq, tq, D)
    v3 = v.reshape(nq, tq, D)

    rc = min(FLASH_ROWS, tq)

    def body(q_ref, k_ref, v_ref, fq_ref, fk_ref, o_ref, o32_ref, lse_ref, s_scr, p_scr):
        i = pl.program_id(1)
        lane = lax.broadcasted_iota(jnp.int32, (1, LANES), 1)
        q2 = q_ref[...]
        hmasks = [(lane >= hh * dh) & (lane < (hh + 1) * dh) for hh in range(hpg)]
        qms = [jnp.where(hm, q2, jnp.zeros_like(q2)) for hm in hmasks]

        nlb = tq // LANES
        sum_lane = [((hh + 1) % hpg) * dh for hh in range(hpg)]

        def scores(j, slot):
            kj = k_ref[j]
            for hh in range(hpg):
                s = _dot_nt(qms[hh], kj)
                s_scr[slot, hh] = s if s_mul == 1.0 else s * s_mul

        def soft(j, slot, carry, masked):
            vj = v_ref[j]
            out = []
            for hh in range(hpg):
                m_b, acc = carry[hh]
                fk_row = fk_ref[j, hh:hh + 1, :]
                mx = []
                for r0 in range(0, tq, rc):
                    rs = slice(r0, r0 + rc)
                    s = s_scr[slot, hh, rs, :] - fk_row
                    if masked:
                        ri = r0 + lax.broadcasted_iota(jnp.int32, (rc, tq), 0)
                        ci = lax.broadcasted_iota(jnp.int32, (rc, tq), 1)
                        s = jnp.where(ci <= ri, s, MASK_VALUE)
                    s_scr[slot, hh, rs, :] = s
                    c = s[:, 0:LANES]
                    for cb in range(1, nlb):
                        c = jnp.maximum(c, s[:, cb * LANES:(cb + 1) * LANES])
                    mx.append(c)
                row_max = jnp.max(jnp.concatenate(mx, axis=0), axis=1, keepdims=True)
                m_new = jnp.maximum(m_b, row_max)
                alpha = jnp.exp(m_b - m_new)
                for r0 in range(0, tq, rc):
                    rs = slice(r0, r0 + rc)
                    m_c = m_new[rs]
                    for cb in range(nlb):
                        cs = slice(cb * LANES, (cb + 1) * LANES)
                        p_scr[hh, rs, cs] = jnp.exp(s_scr[slot, hh, rs, cs] - m_c).astype(BF16)
                v_one = jnp.where(hmasks[hh], vj, jnp.ones_like(vj))
                out.append((m_new, alpha * acc + _dot(p_scr[hh], v_one)))
            return tuple(out)

        def step(j, carry, masked):
            scores(j, 0)
            return soft(j, 0, carry, masked)

        init = tuple((jnp.full((tq, LANES), MASK_VALUE, F32), jnp.zeros((tq, LANES), F32)) for _ in range(hpg))
        carry = lax.fori_loop(0, i, lambda j, cr: step(j, cr, False), init)
        carry = step(i, carry, True)
        fq = fq_ref[...]
        o_all = jnp.zeros((tq, LANES), F32)
        lse_all = jnp.zeros((tq, LANES), F32)
        for hh in range(hpg):
            m_b, acc = carry[hh]
            l = acc[:, sum_lane[hh]:sum_lane[hh] + 1]
            o_all = jnp.where(hmasks[hh], acc * (1.0 / l), o_all)
            lse_all = jnp.where(lane == hh, m_b[:, 0:1] + jnp.log(l) + fq[:, hh:hh + 1], lse_all)
        o_ref[...] = o_all.astype(BF16)
        o32_ref[...] = o_all
        lse_ref[...] = lse_all

    return _pcall(
        body, name="flash_fwd", grid=(G, nq),
        in_specs=[pl.BlockSpec((tq, LANES), lambda g, i: (i, g)),
                  pl.BlockSpec((nq, tq, LANES), lambda g, i: (0, 0, g)),
                  pl.BlockSpec((nq, tq, LANES), lambda g, i: (0, 0, g)),
                  pl.BlockSpec((None, tq, LANES), lambda g, i: (g, i, 0)),
                  pl.BlockSpec((None, nq, SUBLANES, tq), lambda g, i: (g, 0, 0, 0))],
        out_specs=[pl.BlockSpec((tq, LANES), lambda g, i: (i, g)),
                   pl.BlockSpec((tq, LANES), lambda g, i: (i, g)),
                   pl.BlockSpec((None, tq, LANES), lambda g, i: (g, i, 0))],
        out_shape=[jax.ShapeDtypeStruct((T, D), BF16), jax.ShapeDtypeStruct((T, D), F32),
                   jax.ShapeDtypeStruct((G, T, LANES), F32)],
        scratch_shapes=[pltpu.VMEM((1, hpg, tq, tq), F32), pltpu.VMEM((hpg, tq, tq), BF16)],
        compiler_params=_params(("arbitrary", "arbitrary")),
    )(q, k3, v3, fq_aux, fk_rows)


def _flash_bwd(q, k, v, do, fq_aux, lse_aux, dl_aux, fk_rows, *, dh, tq, s_mul, dq_mul):
    T, D = q.shape
    G = D // LANES
    hpg = LANES // dh
    nq = T // tq
    k3 = k.reshape(nq, tq, D)
    v3 = v.reshape(nq, tq, D)
    rc = min(FLASH_ROWS, tq)

    def body(q_ref, k_ref, v_ref, do_ref, fq_ref, lse_ref, dl_ref, fk_ref,
             dq_ref, dk_ref, dv_ref, dfq_ref, dfk_ref, dk_acc, dv_acc, s_scr, dp_scr, p_scr, ds_scr):
        i = pl.program_id(1)

        @pl.when(i == 0)
        def _():
            dk_acc[...] = jnp.zeros_like(dk_acc)
            dv_acc[...] = jnp.zeros_like(dv_acc)
            dfk_ref[...] = jnp.zeros_like(dfk_ref)

        lane = lax.broadcasted_iota(jnp.int32, (1, LANES), 1)
        q2 = q_ref[...]
        do2 = do_ref[...]
        cq = fq_ref[...] - lse_ref[...]
        dl = dl_ref[...]
        hmasks = [(lane >= hh * dh) & (lane < (hh + 1) * dh) for hh in range(hpg)]
        qms = [jnp.where(hm, q2, jnp.zeros_like(q2)) for hm in hmasks]
        doms = [jnp.where(hm, do2, jnp.zeros_like(do2)) for hm in hmasks]
        c_bs = [jnp.broadcast_to(cq[:, hh:hh + 1], (tq, LANES)) for hh in range(hpg)]
        dl_bs = [jnp.broadcast_to(dl[:, hh:hh + 1], (tq, LANES)) for hh in range(hpg)]
        nlb = tq // LANES

        def step(j, carry, masked):
            kj = k_ref[j]
            vj = v_ref[j]
            for hh in range(hpg):
                s = _dot_nt(qms[hh], kj)
                s_scr[hh] = s if s_mul == 1.0 else s * s_mul
                dp_scr[hh] = _dot_nt(doms[hh], vj)
            out = []
            for hh in range(hpg):
                dq_acc, rs_p = carry[hh]
                fk_row = fk_ref[j, hh:hh + 1, :]
                rsums = []
                csums = [jnp.zeros((SUBLANES, LANES), F32) for _ in range(nlb)]
                for r0 in range(0, tq, rc):
                    rs = slice(r0, r0 + rc)
                    c_c = c_bs[hh][rs]
                    dl_c = dl_bs[hh][rs]
                    tot = None
                    for cb in range(nlb):
                        cs = slice(cb * LANES, (cb + 1) * LANES)
                        e = (s_scr[hh, rs, cs] - fk_row[:, cs]) + c_c
                        if masked:
                            ri = r0 + lax.broadcasted_iota(jnp.int32, (rc, LANES), 0)
                            ci = cb * LANES + lax.broadcasted_iota(jnp.int32, (rc, LANES), 1)
                            e = jnp.where(ci <= ri, e, MASK_VALUE)
                        p = jnp.exp(e)
                        ds = p * (dp_scr[hh, rs, cs] - dl_c)
                        p_scr[hh, rs, cs] = p.astype(BF16)
                        ds_scr[hh, rs, cs] = ds.astype(BF16)
                        tot = ds if tot is None else tot + ds
                        csums[cb] = csums[cb] + ds.reshape(rc // SUBLANES, SUBLANES, LANES).sum(axis=0)
                    rsums.append(tot)
                col = jnp.concatenate([jnp.sum(c, axis=0, keepdims=True) for c in csums], axis=1)
                dfk_ref[j, hh:hh + 1, :] += -col
                out.append((dq_acc + _dot(ds_scr[hh], kj), rs_p + jnp.concatenate(rsums, axis=0)))
            dk_new = _dot_tn(ds_scr[0], qms[0])
            dv_new = _dot_tn(p_scr[0], doms[0])
            for hh in range(1, hpg):
                dk_new = dk_new + _dot_tn(ds_scr[hh], qms[hh])
                dv_new = dv_new + _dot_tn(p_scr[hh], doms[hh])
            dk_acc[j] += dk_new
            dv_acc[j] += dv_new
            return tuple(out)

        init = tuple((jnp.zeros((tq, LANES), F32), jnp.zeros((tq, LANES), F32)) for _ in range(hpg))
        carry = lax.fori_loop(0, i, lambda j, cr: step(j, cr, False), init)
        carry = step(i, carry, True)
        dq_all = jnp.zeros((tq, LANES), F32)
        dfq_all = jnp.zeros((tq, LANES), F32)
        for hh in range(hpg):
            dq_h, rs_p = carry[hh]
            dq_all = jnp.where(hmasks[hh], dq_h, dq_all)
            dfq_all = jnp.where(lane == hh, jnp.sum(rs_p, axis=1, keepdims=True), dfq_all)
        dq_ref[...] = (dq_all * dq_mul).astype(BF16)
        dfq_ref[...] = dfq_all

        @pl.when(i == nq - 1)
        def _():
            dkv = dk_acc[...]
            if s_mul != 1.0:
                dkv = dkv * s_mul
            dk_ref[...] = dkv.astype(BF16)
            dv_ref[...] = dv_acc[...].astype(BF16)

    blk = pl.BlockSpec((tq, LANES), lambda g, i: (i, g))
    res = pl.BlockSpec((nq, tq, LANES), lambda g, i: (0, 0, g))
    aux = pl.BlockSpec((None, tq, LANES), lambda g, i: (g, i, 0))
    rows = pl.BlockSpec((None, nq, SUBLANES, tq), lambda g, i: (g, 0, 0, 0))
    dq, dk3, dv3, dfq, dfk = _pcall(
        body, name="flash_bwd", grid=(G, nq),
        in_specs=[blk, res, res, blk, aux, aux, aux, rows],
        out_specs=[blk, res, res, aux, rows],
        out_shape=[jax.ShapeDtypeStruct((T, D), BF16), jax.ShapeDtypeStruct((nq, tq, D), BF16),
                   jax.ShapeDtypeStruct((nq, tq, D), BF16), jax.ShapeDtypeStruct((G, T, LANES), F32),
                   jax.ShapeDtypeStruct((G, nq, SUBLANES, tq), F32)],
        scratch_shapes=[pltpu.VMEM((nq, tq, LANES), F32), pltpu.VMEM((nq, tq, LANES), F32),
                        pltpu.VMEM((hpg, tq, tq), F32), pltpu.VMEM((hpg, tq, tq), F32),
                        pltpu.VMEM((hpg, tq, tq), BF16), pltpu.VMEM((hpg, tq, tq), BF16)],
        compiler_params=_params(("arbitrary", "arbitrary")),
    )(q, k3, v3, do, fq_aux, lse_aux, dl_aux, fk_rows)
    return dq, dk3.reshape(T, D), dv3.reshape(T, D), dfq, dfk


def _attn_out_fwd(o, x, wo, g_post, tm):
    T, D = x.shape

    def body(o_ref, x_ref, w_ref, g_ref, m_ref, xo_ref):
        m = _dot(o_ref[...], w_ref[...])
        m_ref[...] = m
        xo_ref[...] = x_ref[...] + _rms(m, g_ref[...])

    return _pcall(
        body, name="attn_out_fwd", grid=(T // tm,),
        in_specs=[_rows(tm, D), _rows(tm, D), _full((D, D)), _full((1, D))],
        out_specs=[_rows(tm, D), _rows(tm, D)],
        out_shape=[jax.ShapeDtypeStruct((T, D), F32), jax.ShapeDtypeStruct((T, D), F32)],
        compiler_params=_params(("arbitrary",)),
    )(o, x, wo, g_post)


def _attn_out_bwd(dxo, m, o, wo, g_post, head_ind, tm):
    T, D = m.shape

    def body(dxo_ref, m_ref, o_ref, w_ref, g_ref, ind_ref, dm_ref, do_ref, dl_ref, sums_ref):
        i = pl.program_id(0)
        dm, dgpost = _rms_bwd(m_ref[...], g_ref[...], dxo_ref[...])
        dmb = dm.astype(BF16)
        dm_ref[...] = dmb
        dob = _dot_nt(dmb, w_ref[...]).astype(BF16)
        do_ref[...] = dob
        dl_ref[...] = jnp.dot(dob.astype(F32) * o_ref[...], ind_ref[...], precision=lax.Precision.HIGHEST,
                              preferred_element_type=F32)

        @pl.when(i == 0)
        def _():
            sums_ref[...] = jnp.zeros_like(sums_ref)
        sums_ref[0:1, :] += dgpost

    return _pcall(
        body, name="attn_out_bwd", grid=(T // tm,),
        in_specs=[_rows(tm, D), _rows(tm, D), _rows(tm, D), _full((D, D)), _full((1, D)), _full((D, LANES))],
        out_specs=[_rows(tm, D), _rows(tm, D), _rows(tm, LANES), _acc((SUBLANES, D))],
        out_shape=[jax.ShapeDtypeStruct((T, D), BF16), jax.ShapeDtypeStruct((T, D), BF16),
                   jax.ShapeDtypeStruct((T, LANES), F32), jax.ShapeDtypeStruct((SUBLANES, D), F32)],
        compiler_params=_params(("arbitrary",)),
    )(dxo, m, o, wo, g_post, head_ind)


def _attn_in_bwd(dxo, x, g_pre, dq, dk, dv, dlf, lf, wqkv, wf, tm, n_heads):
    T, D = x.shape

    def body(dxo_ref, x_ref, g_ref, dq_ref, dk_ref, dv_ref, dlf_ref, lf_ref, w_ref, wf_ref,
             dxi_ref, h_ref, df_ref, sums_ref, dbf_ref):
        i = pl.program_id(0)
        xv = x_ref[...]
        h_ref[...] = _rms(xv, g_ref[...]).astype(BF16)
        lane = lax.broadcasted_iota(jnp.int32, (1, LANES), 1)
        df = jnp.where(lane < n_heads, dlf_ref[...] * (1.0 - jnp.exp(lf_ref[...])), 0.0)
        dfb = df.astype(BF16)
        df_ref[...] = dfb
        dh = (_dot_nt(dq_ref[...], w_ref[:, 0:D]) + _dot_nt(dk_ref[...], w_ref[:, D:2 * D])
              + _dot_nt(dv_ref[...], w_ref[:, 2 * D:3 * D]) + _dot_nt(dfb, wf_ref[...]))
        dxi, dgpre = _rms_bwd(xv, g_ref[...], dh)
        dxi_ref[...] = dxo_ref[...] + dxi

        @pl.when(i == 0)
        def _():
            sums_ref[...] = jnp.zeros_like(sums_ref)
            dbf_ref[...] = jnp.zeros_like(dbf_ref)
        sums_ref[0:1, :] += dgpre
        dbf_ref[...] += jnp.sum(df, axis=0, keepdims=True)

    return _pcall(
        body, name="attn_in_bwd", grid=(T // tm,),
        in_specs=[_rows(tm, D), _rows(tm, D), _full((1, D)), _rows(tm, D), _rows(tm, D), _rows(tm, D),
                  _rows(tm, LANES), _rows(tm, LANES), _full((D, 3 * D)), _full((D, LANES))],
        out_specs=[_rows(tm, D), _rows(tm, D), _rows(tm, LANES), _acc((SUBLANES, D)), _acc((1, LANES))],
        out_shape=[jax.ShapeDtypeStruct((T, D), F32), jax.ShapeDtypeStruct((T, D), BF16),
                   jax.ShapeDtypeStruct((T, LANES), BF16), jax.ShapeDtypeStruct((SUBLANES, D), F32),
                   jax.ShapeDtypeStruct((1, LANES), F32)],
        compiler_params=_params(("arbitrary",)),
    )(dxo, x, g_pre, dq, dk, dv, dlf, lf, wqkv, wf)


def _loss_head(y, target, tm):
    T, D = y.shape

    def body(y_ref, t_ref, dy_ref, loss_ref):
        i = pl.program_id(0)
        err = y_ref[...] - t_ref[...]
        dy_ref[...] = err * (1.0 / D)
        part = 0.5 * jnp.sum(jnp.mean(err * err, axis=-1, keepdims=True), axis=0, keepdims=True)

        @pl.when(i == 0)
        def _():
            loss_ref[...] = jnp.zeros_like(loss_ref)
        loss_ref[...] += part

    return _pcall(
        body, name="loss_head", grid=(T // tm,),
        in_specs=[_rows(tm, D), _rows(tm, D)],
        out_specs=[_rows(tm, D), _acc((SUBLANES, LANES))],
        out_shape=[jax.ShapeDtypeStruct((T, D), F32), jax.ShapeDtypeStruct((SUBLANES, LANES), F32)],
        compiler_params=_params(("arbitrary",)),
    )(y, target)


def _adamw(recv, w, m, v, tr, name):
    R, C = w.shape
    c1 = 1.0 - ADAM_B1 ** ADAM_STEP
    c2 = 1.0 - ADAM_B2 ** ADAM_STEP

    def body(r_ref, w_ref, m_ref, v_ref, g_ref, d_ref, nm_ref, nv_ref):
        g = r_ref[0].astype(F32)
        for s in range(1, N_DEV):
            g = g + r_ref[s].astype(F32)
        nm = ADAM_B1 * m_ref[...] + (1.0 - ADAM_B1) * g
        nv = ADAM_B2 * v_ref[...] + (1.0 - ADAM_B2) * jnp.square(g)
        m_hat = nm / c1
        v_hat = nv / c2
        g_ref[...] = g
        d_ref[...] = -ADAM_LR * (m_hat / (jnp.sqrt(v_hat) + ADAM_EPS) + ADAM_WD * w_ref[...])
        nm_ref[...] = nm
        nv_ref[...] = nv

    return _pcall(
        body, name=name, grid=(R // tr,),
        in_specs=[pl.BlockSpec((N_DEV, tr, C), lambda i: (0, i, 0))] + [_rows(tr, C)] * 3,
        out_specs=[_rows(tr, C)] * 4,
        out_shape=[jax.ShapeDtypeStruct((R, C), F32)] * 4,
        compiler_params=_params(("arbitrary",)),
    )(recv, w, m, v)


def _row_block(rows, cols):
    cap = max(SUBLANES, (256 * 1024) // max(cols, 1))
    best = None
    for t in range(SUBLANES, rows + 1, SUBLANES):
        if rows % t == 0 and t <= cap:
            best = t
    return rows if best is None else best


def kernel(x, g_mix_pre, g_mix_post, g_ffn_pre, g_ffn_post, conv_pw1_w, conv_pw1_b, conv_dw_w, conv_dw_b, conv_ln_g, conv_ln_b, conv_pw2_w, conv_pw2_b, attn_w_in, attn_b_f, attn_w_o, mlp_w_up, mlp_w_down, loss_target, m_g_mix_pre, m_g_mix_post, m_g_ffn_pre, m_g_ffn_post, m_conv_pw1_w, m_conv_pw1_b, m_conv_dw_w, m_conv_dw_b, m_conv_ln_g, m_conv_ln_b, m_conv_pw2_w, m_conv_pw2_b, m_attn_w_in, m_attn_b_f, m_attn_w_o, m_mlp_w_up, m_mlp_w_down, v_g_mix_pre, v_g_mix_post, v_g_ffn_pre, v_g_ffn_post, v_conv_pw1_w, v_conv_pw1_b, v_conv_dw_w, v_conv_dw_b, v_conv_ln_g, v_conv_ln_b, v_conv_pw2_w, v_conv_pw2_b, v_attn_w_in, v_attn_b_f, v_attn_w_o, v_mlp_w_up, v_mlp_w_down):
    _, T, D = x.shape
    H = attn_b_f.shape[-1]
    dh = D // H
    width = conv_dw_w.shape[1]
    cin = attn_w_in.shape[-1]
    fs = mlp_w_up.shape[-1]
    G = D // LANES
    hpg = LANES // dh
    assert T % 4 == 0 and D % LANES == 0 and LANES % dh == 0 and width <= CONV_HALO and H <= LANES

    tm = min(512, T // 4)
    tmb = min(256, T // 4)
    tq = tm
    tmc = min(256, T // 4)
    lc = min(256, D)
    tkw = min(1024, T // 4)
    tb = min(256, T // 4)
    nq = T // tq

    scale = float(dh) ** -0.5
    mant, _ = math.frexp(scale)
    q_mul = scale if mant == 0.5 else 1.0
    s_mul = 1.0 if mant == 0.5 else scale

    x2 = x.reshape(T, D)
    tgt = loss_target.reshape(T, D)

    gathered = _all_gather(
        [(conv_pw1_w, 0, BF16), (conv_dw_w, 0, F32), (conv_pw2_w, 0, BF16), (attn_w_in, 0, BF16),
         (attn_w_o, 0, BF16), (mlp_w_up, 0, BF16), (mlp_w_up, 1, BF16), (mlp_w_down, 0, BF16),
         (mlp_w_down, 1, BF16)], "gather_weights")
    w1g, dwg, w2g, wing, wog, wu0, wu1, wd0, wd1 = gathered
    w2 = w2g.reshape(D, D)
    wo = wog.reshape(D, D)
    dw_full = jnp.transpose(dwg, (1, 0, 2)).reshape(width, D)
    w32 = jnp.pad(dw_full, ((0, 32 - width), (0, 0)))
    win = jnp.transpose(wing, (1, 0, 2)).reshape(D, N_DEV * cin)
    wqkv = win[:, :3 * D]
    wf = jnp.pad(win[:, 3 * D:], ((0, 0), (0, LANES - H)))
    bf = jnp.pad(attn_b_f, ((0, 0), (0, LANES - H)))

    row = lambda a, i: a[i:i + 1]

    a0, u0 = _conv_in_fwd(x2, row(g_mix_pre, 0), w1g, conv_pw1_b, tm)
    y0 = _dwconv_fwd(u0, w32, conv_dw_b, tmc, lc, width)
    m0, x_1 = _conv_out_fwd(y0, x2, conv_ln_g, conv_ln_b, w2, conv_pw2_b, row(g_mix_post, 0), tm)
    up0, n0, x_2 = _mlp_fwd(x_1, row(g_ffn_pre, 0), wu0, wd0, row(g_ffn_post, 0), tm, "mlp0_fwd")

    q, k, v, lf = _attn_in_fwd(x_2, row(g_mix_pre, 1), wqkv, wf, bf, tm, q_mul, H)
    fcum = _cumsum_rows(lf, jnp.zeros_like(lf), tb, False, "forget_cumsum")

    def to_aux(t):
        t = jnp.transpose(t[:, :H].reshape(T, G, hpg), (1, 0, 2))
        return jnp.pad(t, ((0, 0), (0, 0), (0, LANES - hpg)))

    fq_aux = to_aux(fcum)
    fk_rows = jnp.transpose(fcum[:, :H].T.reshape(G, hpg, nq, tq), (0, 2, 1, 3))
    fk_rows = jnp.pad(fk_rows, ((0, 0), (0, 0), (0, SUBLANES - hpg), (0, 0)))
    o, o32, lse_aux = _flash_fwd(q, k, v, fq_aux, fk_rows, dh=dh, tq=tq, s_mul=s_mul)
    m1, x_3 = _attn_out_fwd(o, x_2, wo, row(g_mix_post, 1), tm)
    up1, n1, x_4 = _mlp_fwd(x_3, row(g_ffn_pre, 1), wu1, wd1, row(g_ffn_post, 1), tm, "mlp1_fwd")

    dx, loss_blk = _loss_head(x_4, tgt, tm)

    def mlp_back(dx, n_l, x_in, up_l, l, wu, wd):
        dxi, h, dm, dup, sums = _mlp_bwd(dx, n_l, x_in, up_l, row(g_ffn_pre, l), wu, wd, row(g_ffn_post, l),
                                         tmb, "mlp%d_bwd" % l)
        dwu = _mm_tn(h, dup, nj=N_DEV, a_cols=D, g_cols=fs, a_by_j=False, g_by_j=True, tk=tkw,
                     name="mlp%d_dwu" % l)
        dwd = _mm_tn(up_l, dm, nj=N_DEV, a_cols=fs, g_cols=D, a_by_j=True, g_by_j=False, tk=tkw,
                     name="mlp%d_dwd" % l, act=True)
        return dxi, dwu, dwd, sums

    dx, dwu1, dwd1, s_mlp1 = mlp_back(dx, n1, x_3, up1, 1, wu1, wd1)

    head_ind = jnp.asarray((np.arange(D)[:, None] // dh == np.arange(LANES)[None, :]).astype(np.float32))
    dm1, do, delta, s_ao = _attn_out_bwd(dx, m1, o32, wo, row(g_mix_post, 1), head_ind, tm)
    dwo = _mm_tn(o, dm1, nj=1, a_cols=D, g_cols=D, a_by_j=False, g_by_j=False, tk=tkw, name="attn_dwo")
    dq, dk, dv, dfq, dfk = _flash_bwd(q, k, v, do, fq_aux, lse_aux, to_aux(delta), fk_rows,
                                      dh=dh, tq=tq, s_mul=s_mul, dq_mul=scale)
    df_k = jnp.pad(jnp.transpose(dfk[:, :, :hpg, :], (0, 2, 1, 3)).reshape(H, T).T, ((0, 0), (0, LANES - H)))
    df_q = jnp.pad(jnp.transpose(dfq[:, :, :hpg], (1, 0, 2)).reshape(T, H), ((0, 0), (0, LANES - H)))
    dlf = _cumsum_rows(df_q, df_k, tb, True, "forget_cumsum_bwd")
    dx, h_at, df, s_ai, dbf = _attn_in_bwd(dx, x_2, row(g_mix_pre, 1), dq, dk, dv, dlf, lf, wqkv, wf, tm, H)
    dwq = _mm_tn(h_at, dq, nj=1, a_cols=D, g_cols=D, a_by_j=False, g_by_j=False, tk=tkw, name="attn_dwq")
    dwk = _mm_tn(h_at, dk, nj=1, a_cols=D, g_cols=D, a_by_j=False, g_by_j=False, tk=tkw, name="attn_dwk")
    dwv = _mm_tn(h_at, dv, nj=1, a_cols=D, g_cols=D, a_by_j=False, g_by_j=False, tk=tkw, name="attn_dwv")
    dwf = _mm_tn(h_at, df, nj=1, a_cols=D, g_cols=LANES, a_by_j=False, g_by_j=False, tk=tkw, name="attn_dwf")
    dwin = jnp.concatenate([dwq[0], dwk[0], dwv[0], dwf[0][:, :H]], axis=1)
    dwin = jnp.transpose(dwin.reshape(D, N_DEV, cin), (1, 0, 2))

    dx, dwu0, dwd0, s_mlp0 = mlp_back(dx, n0, x_1, up0, 0, wu0, wd0)

    dy0, dm0, z0, s_co = _conv_out_bwd(dx, m0, y0, conv_ln_g, conv_ln_b, w2, row(g_mix_post, 0), tm)
    dw2 = _mm_tn(z0, dm0, nj=1, a_cols=D, g_cols=D, a_by_j=False, g_by_j=False, tk=tkw, name="conv_dw2")
    du0, ddw = _dwconv_bwd(dy0, u0, w32, tmc, lc, width)
    grad_x, h_cv, da0, s_ci, db1 = _conv_in_bwd(dx, du0, a0, x2, row(g_mix_pre, 0), w1g, tm)
    dw1 = _mm_tn(h_cv, da0, nj=N_DEV, a_cols=D, g_cols=(2 * D) // N_DEV, a_by_j=False, g_by_j=True, tk=tkw,
                 name="conv_dw1")
    ddw_s = jnp.transpose(ddw[:width].reshape(width, N_DEV, D // N_DEV), (1, 0, 2))

    def pad_row(a):
        return jnp.pad(a, ((0, 0), (0, D - a.shape[1])))

    def pack(gmp, gmq, gfp, gfq, b1, dwb, lng, lnb, b2, bfv, last):
        return jnp.concatenate([gmp, gmq, gfp, gfq, b1.reshape(2, D), dwb, lng, lnb, b2, pad_row(bfv), last],
                               axis=0)

    zero_row = jnp.zeros((1, D), F32)
    small_g = pack(
        jnp.concatenate([row(s_ci, 0), row(s_ai, 0)], axis=0),
        jnp.concatenate([row(s_co, 0), row(s_ao, 0)], axis=0),
        jnp.concatenate([row(s_mlp0, 1), row(s_mlp1, 1)], axis=0),
        jnp.concatenate([row(s_mlp0, 0), row(s_mlp1, 0)], axis=0),
        db1, row(s_co, 4), row(s_co, 1), row(s_co, 2), row(s_co, 3), dbf[:, :H],
        pad_row(loss_blk[0:1, 0:1]))
    rs = D // N_DEV
    (r_w1, r_dw, r_w2, r_win, r_wo, r_wu, r_wd), r_small = _exchange(
        [(dw1, 0, None), (ddw_s, 1, None), (dw2.reshape(N_DEV, rs, D), 2, None), (dwin, 3, None),
         (dwo.reshape(N_DEV, rs, D), 4, None), (dwu0, 5, 0), (dwu1, 5, 1), (dwd0, 6, 0), (dwd1, 6, 1)],
        [((N_DEV, D, (2 * D) // N_DEV), BF16), ((N_DEV, width, D // N_DEV), F32), ((N_DEV, rs, D), BF16),
         ((N_DEV, D, cin), BF16), ((N_DEV, rs, D), BF16), ((N_DEV, 2, D, fs), BF16),
         ((N_DEV, 2, fs, D), BF16)],
        small_g, "exchange_grads")

    def opt(recv, w, m, v, name):
        shp = w.shape
        C = shp[-1]
        R = int(np.prod(shp[:-1]))
        outs = _adamw(recv.reshape(N_DEV, R, C), w.reshape(R, C), m.reshape(R, C), v.reshape(R, C),
                      _row_block(R, C), name)
        return [t.reshape(shp) for t in outs]

    big = {
        "conv_pw1_w": opt(r_w1, conv_pw1_w, m_conv_pw1_w, v_conv_pw1_w, "adamw_pw1"),
        "conv_dw_w": opt(r_dw, conv_dw_w, m_conv_dw_w, v_conv_dw_w, "adamw_dw"),
        "conv_pw2_w": opt(r_w2, conv_pw2_w, m_conv_pw2_w, v_conv_pw2_w, "adamw_pw2"),
        "attn_w_in": opt(r_win, attn_w_in, m_attn_w_in, v_attn_w_in, "adamw_win"),
        "attn_w_o": opt(r_wo, attn_w_o, m_attn_w_o, v_attn_w_o, "adamw_wo"),
        "mlp_w_up": opt(r_wu, mlp_w_up, m_mlp_w_up, v_mlp_w_up, "adamw_wup"),
        "mlp_w_down": opt(r_wd, mlp_w_down, m_mlp_w_down, v_mlp_w_down, "adamw_wdown"),
    }
    small_w = pack(g_mix_pre, g_mix_post, g_ffn_pre, g_ffn_post, conv_pw1_b, conv_dw_b, conv_ln_g, conv_ln_b,
                   conv_pw2_b, attn_b_f, zero_row)
    small_m = pack(m_g_mix_pre, m_g_mix_post, m_g_ffn_pre, m_g_ffn_post, m_conv_pw1_b, m_conv_dw_b, m_conv_ln_g,
                   m_conv_ln_b, m_conv_pw2_b, m_attn_b_f, zero_row)
    small_v = pack(v_g_mix_pre, v_g_mix_post, v_g_ffn_pre, v_g_ffn_post, v_conv_pw1_b, v_conv_dw_b, v_conv_ln_g,
                   v_conv_ln_b, v_conv_pw2_b, v_attn_b_f, zero_row)
    sm = _adamw(r_small, small_w, small_m, small_v, small_w.shape[0], "adamw_small")
    loss = sm[0][15, 0]

    def unpack(t):
        return {"g_mix_pre": t[0:2], "g_mix_post": t[2:4], "g_ffn_pre": t[4:6], "g_ffn_post": t[6:8],
                "conv_pw1_b": t[8:10].reshape(1, 2 * D), "conv_dw_b": t[10:11], "conv_ln_g": t[11:12],
                "conv_ln_b": t[12:13], "conv_pw2_b": t[13:14], "attn_b_f": t[14:15, :H]}

    small = [unpack(t) for t in sm]
    names = ["g_mix_pre", "g_mix_post", "g_ffn_pre", "g_ffn_post", "conv_pw1_w", "conv_pw1_b", "conv_dw_w",
             "conv_dw_b", "conv_ln_g", "conv_ln_b", "conv_pw2_w", "conv_pw2_b", "attn_w_in", "attn_b_f",
             "attn_w_o", "mlp_w_up", "mlp_w_down"]
    outs = [loss, grad_x.reshape(1, T, D)]
    for kind in range(4):
        for nme in names:
            outs.append(big[nme][kind] if nme in big else small[kind][nme])
    return tuple(outs)
```

```python
import functools
import math

import numpy as np
import jax
import jax.numpy as jnp
from jax import lax
from jax.experimental import pallas as pl
from jax.experimental.pallas import tpu as pltpu

F32 = jnp.float32
BF16 = jnp.bfloat16

RMS_EPS = 1e-6
LN_EPS = 1e-5
MASK_VALUE = -1e30
ADAM_LR = 0.001
ADAM_B1 = 0.9
ADAM_B2 = 0.999
ADAM_EPS = 1e-08
ADAM_WD = 0.01
ADAM_STEP = 10

N_DEV = 8
LANES = 128
SUBLANES = 8
CONV_HALO = 32
CONV_ROWS = 32
FLASH_ROWS = 32
VMEM_LIMIT = 56 * 1024 * 1024

_pcall = pl.pallas_call


def _params(sem=None):
    if sem is None:
        return pltpu.CompilerParams(vmem_limit_bytes=VMEM_LIMIT)
    return pltpu.CompilerParams(dimension_semantics=sem, vmem_limit_bytes=VMEM_LIMIT)


def _dot(a, b):
    return jnp.dot(a, b, preferred_element_type=F32)


def _dot_nt(a, b):
    return lax.dot_general(a, b, (((1,), (1,)), ((), ())), preferred_element_type=F32)


def _dot_tn(a, b):
    return lax.dot_general(a, b, (((0,), (0,)), ((), ())), preferred_element_type=F32)


def _full(shape):
    nd = len(shape)
    return pl.BlockSpec(shape, lambda *g: (0,) * nd, pipeline_mode=pl.Buffered(1))


def _acc(shape):
    nd = len(shape)
    return pl.BlockSpec(shape, lambda *g: (0,) * nd)


def _rows(tm, cols):
    return pl.BlockSpec((tm, cols), lambda i: (i, 0))


def _rms(x, g):
    r = lax.rsqrt(jnp.mean(x * x, axis=-1, keepdims=True) + RMS_EPS)
    return x * r * g


def _rms_bwd(x, g, dy):
    r = lax.rsqrt(jnp.mean(x * x, axis=-1, keepdims=True) + RMS_EPS)
    n = x * r
    dg = jnp.sum(dy * n, axis=0, keepdims=True)
    dn = dy * g
    dx = r * (dn - n * jnp.mean(dn * n, axis=-1, keepdims=True))
    return dx, dg


def _sigmoid(x):
    return 1.0 / (1.0 + jnp.exp(-x))


def _mesh_pos():
    return lax.axis_index("x"), lax.axis_index("y"), lax.axis_index("c")


def _dev_index(px, py, pc):
    return 4 * px + 2 * py + pc


def _all_gather(items, name):
    n = len(items)
    arrs = []
    for it in items:
        if not any(it[0] is a for a in arrs):
            arrs.append(it[0])
    n_in = len(arrs)
    which = [[it[0] is a for a in arrs].index(True) for it in items]
    shapes = [it[0].shape if it[1] is None else it[0].shape[1:] for it in items]
    dtypes = [it[2] for it in items]

    def body(*refs):
        ins = [refs[w] for w in which]
        outs = refs[n_in:n_in + n]
        stage = refs[n_in + n:n_in + 2 * n]
        send_sems, recv_sems, local_sems = refs[n_in + 2 * n:]
        x, y, c = _mesh_pos()
        me, sib = (x, y, c), (x, y, 1 - c)
        chips = [(1 - x, y), (x, 1 - y), (1 - x, 1 - y)]

        def slot(a, block):
            return outs[a].at[_dev_index(*block)]

        def rcopy(a, k, block, to, src=None):
            dst = slot(a, block)
            return pltpu.make_async_remote_copy(
                src_ref=dst if src is None else src, dst_ref=dst,
                send_sem=send_sems.at[a, k], recv_sem=recv_sems.at[a, k],
                device_id=to, device_id_type=pl.DeviceIdType.MESH)

        mine = []
        for a in range(n):
            src = ins[a] if items[a][1] is None else ins[a].at[items[a][1]]
            stage[a][...] = src[...].astype(dtypes[a])
            cp = pltpu.make_async_copy(stage[a], slot(a, me), local_sems.at[a])
            cp.start()
            mine.append(cp)
        first = []
        for a in range(n):
            first.append(rcopy(a, 0, me, sib, src=stage[a]))
            for j, chip in enumerate(chips):
                first.append(rcopy(a, 1 + j, me, (*chip, c), src=stage[a]))
        for cp in first:
            cp.start()
        passed = []
        for j, chip in enumerate(chips):
            for a in range(n):
                rcopy(a, 1 + j, (*chip, c), me).wait_recv()
                cp = rcopy(a, 4 + j, (*chip, c), sib)
                cp.start()
                passed.append(cp)
        for a in range(n):
            rcopy(a, 0, sib, me).wait_recv()
            for j, chip in enumerate(chips):
                rcopy(a, 4 + j, (*chip, 1 - c), me).wait_recv()
        for cp in first + passed:
            cp.wait_send()
        for cp in mine:
            cp.wait()

    return _pcall(
        body, name=name,
        out_shape=[jax.ShapeDtypeStruct((N_DEV,) + tuple(s), d) for s, d in zip(shapes, dtypes)],
        in_specs=[pl.BlockSpec(memory_space=pltpu.VMEM)] * n_in,
        out_specs=[pl.BlockSpec(memory_space=pl.ANY)] * n,
        scratch_shapes=[pltpu.VMEM(tuple(s), d) for s, d in zip(shapes, dtypes)]
        + [pltpu.SemaphoreType.DMA((n, 7)), pltpu.SemaphoreType.DMA((n, 7)), pltpu.SemaphoreType.DMA((n,))],
        compiler_params=pltpu.CompilerParams(vmem_limit_bytes=VMEM_LIMIT),
    )(*arrs)


def _exchange(items, out_shapes, small, name):
    n = len(items)
    n_out = len(out_shapes)
    arrs = [it[0] for it in items]

    def body(*refs):
        ins = refs[:n]
        sm = refs[n]
        outs = refs[n + 1:n + 1 + n_out]
        smo = refs[n + 1 + n_out]
        send_sems, recv_sems, local_sems = refs[n + 2 + n_out:]
        x, y, c = _mesh_pos()
        me_id = _dev_index(x, y, c)

        def peer(r):
            p = ((1 - x) if r & 4 else x, (1 - y) if r & 2 else y, (1 - c) if r & 1 else c)
            return p, _dev_index(*p)

        def land(a, sender_id):
            if a == n:
                return smo.at[sender_id]
            _, oi, sub = items[a]
            return outs[oi].at[sender_id] if sub is None else outs[oi].at[sender_id, sub]

        def give(a, to_id):
            return sm if a == n else ins[a].at[to_id]

        sends, mine = [], []
        for a in range(n + 1):
            cp = pltpu.make_async_copy(give(a, me_id), land(a, me_id), local_sems.at[a])
            cp.start()
            mine.append(cp)
            for r in range(1, N_DEV):
                p, pid = peer(r)
                cp = pltpu.make_async_remote_copy(
                    src_ref=give(a, pid), dst_ref=land(a, me_id),
                    send_sem=send_sems.at[a, r - 1], recv_sem=recv_sems.at[a, r - 1],
                    device_id=p, device_id_type=pl.DeviceIdType.MESH)
                cp.start()
                sends.append(cp)
        for a in range(n + 1):
            for r in range(1, N_DEV):
                p, pid = peer(r)
                pltpu.make_async_remote_copy(
                    src_ref=land(a, pid), dst_ref=land(a, pid),
                    send_sem=send_sems.at[a, r - 1], recv_sem=recv_sems.at[a, r - 1],
                    device_id=p, device_id_type=pl.DeviceIdType.MESH).wait_recv()
        for cp in sends:
            cp.wait_send()
        for cp in mine:
            cp.wait()

    res = _pcall(
        body, name=name,
        out_shape=[jax.ShapeDtypeStruct(tuple(s), d) for s, d in out_shapes]
        + [jax.ShapeDtypeStruct((N_DEV,) + small.shape, small.dtype)],
        in_specs=[pl.BlockSpec(memory_space=pl.ANY)] * (n + 1),
        out_specs=[pl.BlockSpec(memory_space=pl.ANY)] * (n_out + 1),
        scratch_shapes=[pltpu.SemaphoreType.DMA((n + 1, 7)), pltpu.SemaphoreType.DMA((n + 1, 7)),
                        pltpu.SemaphoreType.DMA((n + 1,))],
    )(*arrs, small)
    return res[:n_out], res[n_out]


_HBM = pl.BlockSpec(memory_space=pltpu.HBM)
_SEM = pl.BlockSpec(memory_space=pltpu.SEMAPHORE)
_EFFECT = pltpu.SideEffectType.DATAFLOW_SIDE_EFFECTING


def _peer(r, x, y, c):
    p = ((1 - x) if r & 4 else x, (1 - y) if r & 2 else y, (1 - c) if r & 1 else c)
    return p, _dev_index(*p)


def _src_ref(refs, item, me_id=None, to_id=None):
    si, sub, mode = item
    r = refs[si] if sub is None else refs[si].at[sub]
    if mode == "slot":
        return r.at[to_id]
    if mode == "own":
        return r.at[me_id]
    return r


def _place_own(srcs, items, land_shapes, name, cast=False):
    ns, n = len(srcs), len(items)

    def body(*refs):
        src = refs[:ns]
        land = refs[ns:ns + n]
        stage = refs[ns + n:ns + 2 * n] if cast else None
        sems = refs[-1]
        me_id = _dev_index(*_mesh_pos())
        cps = []
        for a, item in enumerate(items):
            s = _src_ref(src, item, to_id=me_id)
            if cast:
                stage[a][...] = s[...].astype(stage[a].dtype)
                s = stage[a]
            cp = pltpu.make_async_copy(s, land[a].at[me_id], sems.at[a])
            cp.start()
            cps.append(cp)
        for cp in cps:
            cp.wait()

    return _pcall(
        body, name=name,
        out_shape=[jax.ShapeDtypeStruct(tuple(s), d) for s, d in land_shapes],
        in_specs=[pl.BlockSpec(memory_space=pltpu.VMEM if cast else pl.ANY)] * ns,
        out_specs=[pl.BlockSpec(memory_space=pl.ANY)] * n,
        scratch_shapes=([pltpu.VMEM(tuple(s[1:]), d) for s, d in land_shapes] if cast else [])
        + [pltpu.SemaphoreType.DMA((n,))],
        compiler_params=pltpu.CompilerParams(vmem_limit_bytes=VMEM_LIMIT),
    )(*srcs)


def _push_start(srcs, lands, groups, name):
    ns, n = len(srcs), len(lands)
    ng = len(groups)
    assert sum(len(g) for g in groups) == n

    def body(*refs):
        ops = refs[:ns + n]
        land = refs[ns:ns + n]
        sems = refs[ns + n:ns + n + 2 * ng]
        token = refs[-1]
        x, y, c = _mesh_pos()
        me_id = _dev_index(x, y, c)
        a = 0
        for gi, grp in enumerate(groups):
            for k, item in enumerate(grp):
                for r in range(1, N_DEV):
                    p, pid = _peer(r, x, y, c)
                    pltpu.make_async_remote_copy(
                        src_ref=_src_ref(ops, item, me_id=me_id, to_id=pid), dst_ref=land[a].at[me_id],
                        send_sem=sems[2 * gi].at[k * (N_DEV - 1) + r - 1],
                        recv_sem=sems[2 * gi + 1].at[k * (N_DEV - 1) + r - 1],
                        device_id=p, device_id_type=pl.DeviceIdType.MESH).start()
                a += 1
        token[...] = jnp.zeros_like(token)

    sem_shapes = []
    for grp in groups:
        sem_shapes += [pltpu.SemaphoreType.DMA((len(grp) * (N_DEV - 1),))] * 2
    arrs = list(srcs) + list(lands)
    res = _pcall(
        body, name=name,
        out_shape=tuple(sem_shapes) + tuple(pltpu.HBM(a.shape, a.dtype) for a in arrs)
        + (jax.ShapeDtypeStruct((SUBLANES, LANES), F32),),
        in_specs=[_HBM] * (ns + n),
        out_specs=tuple([_SEM] * (2 * ng)) + tuple([_HBM] * (ns + n)) + (pl.BlockSpec(memory_space=pltpu.VMEM),),
        input_output_aliases={i: 2 * ng + i for i in range(ns + n)},
        compiler_params=pltpu.CompilerParams(has_side_effects=_EFFECT),
    )(*[pltpu.with_memory_space_constraint(a, pltpu.HBM) for a in arrs])
    sems = [(res[2 * gi], res[2 * gi + 1]) for gi in range(ng)]
    thru = res[2 * ng:2 * ng + ns + n]
    return sems, list(thru[:ns]), list(thru[ns:]), res[-1]


def _push_wait(sems, srcs, lands, group, after, name):
    ns, n = len(srcs), len(lands)
    assert len(group) == n

    def body(*refs):
        ops = refs[:ns + n]
        land = refs[ns:ns + n]
        send_sems, recv_sems = refs[ns + n], refs[ns + n + 1]
        x, y, c = _mesh_pos()
        me_id = _dev_index(x, y, c)
        for k, item in enumerate(group):
            for r in range(1, N_DEV):
                p, pid = _peer(r, x, y, c)
                cp = pltpu.make_async_remote_copy(
                    src_ref=_src_ref(ops, item, me_id=me_id, to_id=pid), dst_ref=land[k].at[pid],
                    send_sem=send_sems.at[k * (N_DEV - 1) + r - 1], recv_sem=recv_sems.at[k * (N_DEV - 1) + r - 1],
                    device_id=p, device_id_type=pl.DeviceIdType.MESH)
                cp.wait_send()
                cp.wait_recv()

    arrs = list(srcs) + list(lands)
    res = _pcall(
        body, name=name,
        out_shape=tuple(pltpu.HBM(a.shape, a.dtype) for a in arrs),
        in_specs=[_HBM] * (ns + n) + [_SEM, _SEM, pl.BlockSpec(memory_space=pl.ANY)],
        out_specs=tuple([_HBM] * (ns + n)),
        input_output_aliases={i: i for i in range(ns + n)},
        compiler_params=pltpu.CompilerParams(has_side_effects=_EFFECT),
    )(*arrs, sems[0], sems[1], after)
    return list(res[ns:])


def _conv_in_fwd(x, g_pre, w1g, b1, tm):
    T, D = x.shape
    ns, _, cs = w1g.shape
    half = ns // 2

    def body(x_ref, g_ref, w_ref, b_ref, a_ref, u_ref):
        h = _rms(x_ref[...], g_ref[...]).astype(BF16)
        parts = []
        for s in range(ns):
            a_s = _dot(h, w_ref[s]) + b_ref[:, s * cs:(s + 1) * cs]
            a_ref[:, s * cs:(s + 1) * cs] = a_s
            parts.append(a_s)
        for s in range(half):
            u_ref[:, s * cs:(s + 1) * cs] = parts[s] * _sigmoid(parts[s + half])

    return _pcall(
        body, name="conv_in_fwd", grid=(T // tm,),
        in_specs=[_rows(tm, D), _full((1, D)), _full(w1g.shape), _full((1, 2 * D))],
        out_specs=[_rows(tm, 2 * D), _rows(tm, D)],
        out_shape=[jax.ShapeDtypeStruct((T, 2 * D), F32), jax.ShapeDtypeStruct((T, D), F32)],
        compiler_params=_params(("arbitrary",)),
    )(x, g_pre, w1g, b1)


def _dwconv_fwd(u, w32, b, tm, lc, width):
    T, D = u.shape
    hb = tm // CONV_HALO

    def body(u_ref, halo_ref, w_ref, b_ref, y_ref, ext_ref):
        i = pl.program_id(0)
        ext_ref[0:CONV_HALO, :] = jnp.where(i > 0, halo_ref[...], 0.0)
        ext_ref[CONV_HALO:, :] = u_ref[...]
        for r0 in range(0, tm, CONV_ROWS):
            for l0 in range(0, lc, LANES):
                ls = slice(l0, l0 + LANES)
                acc = jnp.zeros((CONV_ROWS, LANES), F32) + b_ref[:, ls]
                for j in range(width):
                    off = CONV_HALO - (width - 1) + j + r0
                    acc = acc + w_ref[j:j + 1, ls] * ext_ref[off:off + CONV_ROWS, ls]
                y_ref[r0:r0 + CONV_ROWS, ls] = acc

    return _pcall(
        body, name="dwconv_fwd", grid=(T // tm, D // lc),
        in_specs=[pl.BlockSpec((tm, lc), lambda i, l: (i, l)),
                  pl.BlockSpec((CONV_HALO, lc), lambda i, l: (jnp.maximum(i * hb - 1, 0), l)),
                  pl.BlockSpec((32, lc), lambda i, l: (0, l)),
                  pl.BlockSpec((1, lc), lambda i, l: (0, l))],
        out_specs=pl.BlockSpec((tm, lc), lambda i, l: (i, l)),
        out_shape=jax.ShapeDtypeStruct((T, D), F32),
        scratch_shapes=[pltpu.VMEM((tm + CONV_HALO, lc), F32)],
        compiler_params=_params(("arbitrary", "arbitrary")),
    )(u, u, w32, b)


def _ln_parts(y, g, b):
    mu = jnp.mean(y, axis=-1, keepdims=True)
    yc = y - mu
    rstd = lax.rsqrt(jnp.mean(yc * yc, axis=-1, keepdims=True) + LN_EPS)
    yhat = yc * rstd
    return yhat, rstd, yhat * g + b


def _conv_out_fwd(y, x, ln_g, ln_b, w2, b2, g_post, tm):
    T, D = x.shape

    def body(y_ref, x_ref, lg_ref, lb_ref, w_ref, b_ref, g_ref, m_ref, xo_ref):
        _, _, yn = _ln_parts(y_ref[...], lg_ref[...], lb_ref[...])
        z = (yn * _sigmoid(yn)).astype(BF16)
        m = _dot(z, w_ref[...]) + b_ref[...]
        m_ref[...] = m
        xo_ref[...] = x_ref[...] + _rms(m, g_ref[...])

    return _pcall(
        body, name="conv_out_fwd", grid=(T // tm,),
        in_specs=[_rows(tm, D), _rows(tm, D), _full((1, D)), _full((1, D)), _full((D, D)), _full((1, D)),
                  _full((1, D))],
        out_specs=[_rows(tm, D), _rows(tm, D)],
        out_shape=[jax.ShapeDtypeStruct((T, D), F32), jax.ShapeDtypeStruct((T, D), F32)],
        compiler_params=_params(("arbitrary",)),
    )(y, x, ln_g, ln_b, w2, b2, g_post)


def _conv_out_bwd(dxo, m, y, ln_g, ln_b, w2, g_post, tm):
    T, D = m.shape

    def body(dxo_ref, m_ref, y_ref, lg_ref, lb_ref, w_ref, g_ref, dy_ref, dm_ref, z_ref, sums_ref):
        i = pl.program_id(0)
        dm, dgpost = _rms_bwd(m_ref[...], g_ref[...], dxo_ref[...])
        dmb = dm.astype(BF16)
        dm_ref[...] = dmb
        yhat, rstd, yn = _ln_parts(y_ref[...], lg_ref[...], lb_ref[...])
        sg = _sigmoid(yn)
        z_ref[...] = (yn * sg).astype(BF16)
        dz = _dot_nt(dmb, w_ref[...])
        dyn = dz * (sg + yn * sg * (1.0 - sg))
        dyh = dyn * lg_ref[...]
        dy = rstd * (dyh - jnp.mean(dyh, axis=-1, keepdims=True)
                     - yhat * jnp.mean(dyh * yhat, axis=-1, keepdims=True))
        dy_ref[...] = dy

        @pl.when(i == 0)
        def _():
            sums_ref[...] = jnp.zeros_like(sums_ref)
        sums_ref[0:1, :] += dgpost
        sums_ref[1:2, :] += jnp.sum(dyn * yhat, axis=0, keepdims=True)
        sums_ref[2:3, :] += jnp.sum(dyn, axis=0, keepdims=True)
        sums_ref[3:4, :] += jnp.sum(dm, axis=0, keepdims=True)
        sums_ref[4:5, :] += jnp.sum(dy, axis=0, keepdims=True)

    return _pcall(
        body, name="conv_out_bwd", grid=(T // tm,),
        in_specs=[_rows(tm, D), _rows(tm, D), _rows(tm, D), _full((1, D)), _full((1, D)), _full((D, D)),
                  _full((1, D))],
        out_specs=[_rows(tm, D), _rows(tm, D), _rows(tm, D), _acc((SUBLANES, D))],
        out_shape=[jax.ShapeDtypeStruct((T, D), F32), jax.ShapeDtypeStruct((T, D), BF16),
                   jax.ShapeDtypeStruct((T, D), BF16), jax.ShapeDtypeStruct((SUBLANES, D), F32)],
        compiler_params=_params(("arbitrary",)),
    )(dxo, m, y, ln_g, ln_b, w2, g_post)


def _dwconv_bwd(dy, u, w32, tm, lc, width):
    T, D = u.shape
    hb = tm // CONV_HALO
    nt = T // tm
    last_halo = T // CONV_HALO - 1

    def body(dy_ref, dyn_ref, u_ref, up_ref, w_ref, du_ref, dw_ref, exty_ref, extu_ref, acc_ref):
        i = pl.program_id(1)
        exty_ref[0:tm, :] = dy_ref[...]
        exty_ref[tm:, :] = jnp.where(i < nt - 1, dyn_ref[...], 0.0)
        extu_ref[0:CONV_HALO, :] = jnp.where(i > 0, up_ref[...], 0.0)
        extu_ref[CONV_HALO:, :] = u_ref[...]

        @pl.when(i == 0)
        def _():
            acc_ref[...] = jnp.zeros_like(acc_ref)

        for r0 in range(0, tm, CONV_ROWS):
            for l0 in range(0, lc, LANES):
                ls = slice(l0, l0 + LANES)
                dyc = exty_ref[r0:r0 + CONV_ROWS, ls]
                du = jnp.zeros((CONV_ROWS, LANES), F32)
                for j in range(width):
                    oy = r0 + (width - 1) - j
                    du = du + w_ref[j:j + 1, ls] * exty_ref[oy:oy + CONV_ROWS, ls]
                    ou = CONV_HALO - (width - 1) + j + r0
                    prod = dyc * extu_ref[ou:ou + CONV_ROWS, ls]
                    acc_ref[j, :, ls] += prod.reshape(CONV_ROWS // SUBLANES, SUBLANES, LANES).sum(axis=0)
                du_ref[r0:r0 + CONV_ROWS, ls] = du

        @pl.when(i == nt - 1)
        def _():
            for j in range(32):
                dw_ref[j:j + 1, :] = jnp.sum(acc_ref[j], axis=0, keepdims=True)

    return _pcall(
        body, name="dwconv_bwd", grid=(D // lc, nt),
        in_specs=[pl.BlockSpec((tm, lc), lambda l, i: (i, l)),
                  pl.BlockSpec((CONV_HALO, lc), lambda l, i: (jnp.minimum((i + 1) * hb, last_halo), l)),
                  pl.BlockSpec((tm, lc), lambda l, i: (i, l)),
                  pl.BlockSpec((CONV_HALO, lc), lambda l, i: (jnp.maximum(i * hb - 1, 0), l)),
                  pl.BlockSpec((32, lc), lambda l, i: (0, l))],
        out_specs=[pl.BlockSpec((tm, lc), lambda l, i: (i, l)),
                   pl.BlockSpec((32, lc), lambda l, i: (0, l))],
        out_shape=[jax.ShapeDtypeStruct((T, D), F32), jax.ShapeDtypeStruct((32, D), F32)],
        scratch_shapes=[pltpu.VMEM((tm + CONV_HALO, lc), F32), pltpu.VMEM((tm + CONV_HALO, lc), F32),
                        pltpu.VMEM((32, SUBLANES, lc), F32)],
        compiler_params=_params(("arbitrary", "arbitrary")),
    )(dy, dy, u, u, w32)


def _conv_in_bwd(dxo, du, a, x, g_pre, w1g, tm):
    T, D = x.shape
    ns, _, cs = w1g.shape
    half = ns // 2

    def body(dxo_ref, du_ref, a_ref, x_ref, g_ref, w_ref, dxi_ref, h_ref, da_ref, sums_ref, db_ref):
        i = pl.program_id(0)
        xv = x_ref[...]
        h_ref[...] = _rms(xv, g_ref[...]).astype(BF16)
        dh = jnp.zeros((tm, D), F32)
        dbs = [None] * ns
        for s in range(half):
            a_u = a_ref[:, s * cs:(s + 1) * cs]
            sg = _sigmoid(a_ref[:, (s + half) * cs:(s + half + 1) * cs])
            du_s = du_ref[:, s * cs:(s + 1) * cs]
            da_u = du_s * sg
            da_g = du_s * a_u * sg * (1.0 - sg)
            for s2, v in ((s, da_u), (s + half, da_g)):
                vb = v.astype(BF16)
                da_ref[:, s2 * cs:(s2 + 1) * cs] = vb
                dbs[s2] = jnp.sum(v, axis=0, keepdims=True)
                dh = dh + _dot_nt(vb, w_ref[s2])
        dxi, dgpre = _rms_bwd(xv, g_ref[...], dh)
        dxi_ref[...] = dxo_ref[...] + dxi

        @pl.when(i == 0)
        def _():
            sums_ref[...] = jnp.zeros_like(sums_ref)
            db_ref[...] = jnp.zeros_like(db_ref)
        sums_ref[0:1, :] += dgpre
        for s in range(ns):
            db_ref[:, s * cs:(s + 1) * cs] += dbs[s]

    return _pcall(
        body, name="conv_in_bwd", grid=(T // tm,),
        in_specs=[_rows(tm, D), _rows(tm, D), _rows(tm, 2 * D), _rows(tm, D), _full((1, D)),
                  _full(w1g.shape)],
        out_specs=[_rows(tm, D), _rows(tm, D), _rows(tm, 2 * D), _acc((SUBLANES, D)), _acc((1, 2 * D))],
        out_shape=[jax.ShapeDtypeStruct((T, D), F32), jax.ShapeDtypeStruct((T, D), BF16),
                   jax.ShapeDtypeStruct((T, 2 * D), BF16), jax.ShapeDtypeStruct((SUBLANES, D), F32),
                   jax.ShapeDtypeStruct((1, 2 * D), F32)],
        compiler_params=_params(("arbitrary",)),
    )(dxo, du, a, x, g_pre, w1g)


def _mlp_fwd(x, g_pre, wug, wdg, g_post, tm, name):
    T, D = x.shape
    ns, _, fs = wug.shape

    def body(x_ref, gp_ref, wu_ref, wd_ref, gq_ref, up_ref, m_ref, xo_ref):
        xv = x_ref[...]
        h = _rms(xv, gp_ref[...]).astype(BF16)
        acc = jnp.zeros((tm, D), F32)
        for s in range(ns):
            up = _dot(h, wu_ref[s]).astype(BF16)
            up_ref[:, s * fs:(s + 1) * fs] = up
            act = jnp.square(jnp.maximum(up.astype(F32), 0.0)).astype(BF16)
            acc = acc + _dot(act, wd_ref[s])
        m_ref[...] = acc
        xo_ref[...] = xv + _rms(acc, gq_ref[...])

    return _pcall(
        body, name=name, grid=(T // tm,),
        in_specs=[_rows(tm, D), _full((1, D)), _full(wug.shape), _full(wdg.shape), _full((1, D))],
        out_specs=[_rows(tm, ns * fs), _rows(tm, D), _rows(tm, D)],
        out_shape=[jax.ShapeDtypeStruct((T, ns * fs), BF16), jax.ShapeDtypeStruct((T, D), F32),
                   jax.ShapeDtypeStruct((T, D), F32)],
        compiler_params=_params(("arbitrary",)),
    )(x, g_pre, wug, wdg, g_post)


def _mlp_bwd(dxo, m, x, up, g_pre, wug, wdg, g_post, tm, name):
    T, D = x.shape
    ns, _, fs = wug.shape

    def body(dxo_ref, m_ref, x_ref, up_ref, gp_ref, wu_ref, wd_ref, gq_ref,
             dxi_ref, h_ref, dm_ref, dup_ref, sums_ref):
        i = pl.program_id(0)
        dxo = dxo_ref[...]
        dm, dgpost = _rms_bwd(m_ref[...], gq_ref[...], dxo)
        dmb = dm.astype(BF16)
        dm_ref[...] = dmb
        xv = x_ref[...]
        h_ref[...] = _rms(xv, gp_ref[...]).astype(BF16)
        dh = jnp.zeros((tm, D), F32)
        for s in range(ns):
            dact = _dot_nt(dmb, wd_ref[s])
            up = up_ref[:, s * fs:(s + 1) * fs].astype(F32)
            dup = (dact * (2.0 * jnp.maximum(up, 0.0))).astype(BF16)
            dup_ref[:, s * fs:(s + 1) * fs] = dup
            dh = dh + _dot_nt(dup, wu_ref[s])
        dxi, dgpre = _rms_bwd(xv, gp_ref[...], dh)
        dxi_ref[...] = dxo + dxi

        @pl.when(i == 0)
        def _():
            sums_ref[...] = jnp.zeros_like(sums_ref)
        sums_ref[0:1, :] += dgpost
        sums_ref[1:2, :] += dgpre

    return _pcall(
        body, name=name, grid=(T // tm,),
        in_specs=[_rows(tm, D), _rows(tm, D), _rows(tm, D), _rows(tm, ns * fs), _full((1, D)),
                  _full(wug.shape), _full(wdg.shape), _full((1, D))],
        out_specs=[_rows(tm, D), _rows(tm, D), _rows(tm, D), _rows(tm, ns * fs), _acc((SUBLANES, D))],
        out_shape=[jax.ShapeDtypeStruct((T, D), F32), jax.ShapeDtypeStruct((T, D), BF16),
                   jax.ShapeDtypeStruct((T, D), BF16), jax.ShapeDtypeStruct((T, ns * fs), BF16),
                   jax.ShapeDtypeStruct((SUBLANES, D), F32)],
        compiler_params=_params(("arbitrary",)),
    )(dxo, m, x, up, g_pre, wug, wdg, g_post)


def _mm_tn(a, g, *, nj, a_cols, g_cols, a_by_j, g_by_j, tk, name, act=False):
    T = a.shape[0]
    nk = T // tk

    def body(a_ref, g_ref, o_ref, acc_ref):
        k = pl.program_id(1)
        av = a_ref[...]
        if act:
            av = jnp.square(jnp.maximum(av.astype(F32), 0.0)).astype(BF16)
        p = _dot_tn(av, g_ref[...])

        @pl.when(k == 0)
        def _():
            acc_ref[...] = p

        @pl.when(k > 0)
        def _():
            acc_ref[...] += p

        @pl.when(k == nk - 1)
        def _():
            o_ref[...] = acc_ref[...].astype(BF16)

    return _pcall(
        body, name=name, grid=(nj, nk),
        in_specs=[pl.BlockSpec((tk, a_cols), (lambda j, k: (k, j)) if a_by_j else (lambda j, k: (k, 0))),
                  pl.BlockSpec((tk, g_cols), (lambda j, k: (k, j)) if g_by_j else (lambda j, k: (k, 0)))],
        out_specs=pl.BlockSpec((None, a_cols, g_cols), lambda j, k: (j, 0, 0)),
        out_shape=jax.ShapeDtypeStruct((nj, a_cols, g_cols), BF16),
        scratch_shapes=[pltpu.VMEM((a_cols, g_cols), F32)],
        compiler_params=_params(("arbitrary", "arbitrary")),
    )(a, g)


def _attn_in_fwd(x, g_pre, wqkv, wf, bf, tm, q_mul, n_heads):
    T, D = x.shape

    def body(x_ref, g_ref, w_ref, wf_ref, bf_ref, q_ref, k_ref, v_ref, lf_ref):
        h = _rms(x_ref[...], g_ref[...]).astype(BF16)
        q = _dot(h, w_ref[:, 0:D])
        if q_mul != 1.0:
            q = q * q_mul
        q_ref[...] = q.astype(BF16)
        k_ref[...] = _dot(h, w_ref[:, D:2 * D]).astype(BF16)
        v_ref[...] = _dot(h, w_ref[:, 2 * D:3 * D]).astype(BF16)
        fl = _dot(h, wf_ref[...]) + bf_ref[...]
        lf = jnp.minimum(fl, 0.0) - jnp.log(1.0 + jnp.exp(-jnp.abs(fl)))
        lane = lax.broadcasted_iota(jnp.int32, (1, LANES), 1)
        lf_ref[...] = jnp.where(lane < n_heads, lf, 0.0)

    return _pcall(
        body, name="attn_in_fwd", grid=(T // tm,),
        in_specs=[_rows(tm, D), _full((1, D)), _full((D, 3 * D)), _full((D, LANES)), _full((1, LANES))],
        out_specs=[_rows(tm, D), _rows(tm, D), _rows(tm, D), _rows(tm, LANES)],
        out_shape=[jax.ShapeDtypeStruct((T, D), BF16)] * 3 + [jax.ShapeDtypeStruct((T, LANES), F32)],
        compiler_params=_params(("arbitrary",)),
    )(x, g_pre, wqkv, wf, bf)


def _cumsum_rows(v, v2, tb, reverse, name):
    T, C = v.shape
    nb = T // tb

    def body(v_ref, v2_ref, o_ref, carry_ref):
        i = pl.program_id(0)

        @pl.when(i == 0)
        def _():
            carry_ref[...] = jnp.zeros_like(carry_ref)
        r = lax.broadcasted_iota(jnp.int32, (tb, tb), 0)
        c = lax.broadcasted_iota(jnp.int32, (tb, tb), 1)
        tri = jnp.where((c >= r) if reverse else (c <= r), 1.0, 0.0).astype(F32)
        out = jnp.dot(tri, v_ref[...] + v2_ref[...], precision=lax.Precision.HIGHEST,
                      preferred_element_type=F32) + carry_ref[...]
        o_ref[...] = out
        carry_ref[...] = out[0:1, :] if reverse else out[tb - 1:tb, :]

    idx = (lambda i: (nb - 1 - i, 0)) if reverse else (lambda i: (i, 0))
    return _pcall(
        body, name=name, grid=(nb,),
        in_specs=[pl.BlockSpec((tb, C), idx), pl.BlockSpec((tb, C), idx)],
        out_specs=pl.BlockSpec((tb, C), idx),
        out_shape=jax.ShapeDtypeStruct((T, C), F32),
        scratch_shapes=[pltpu.VMEM((1, C), F32)],
        compiler_params=_params(("arbitrary",)),
    )(v, v2)


def _flash_fwd(q, k, v, fq_aux, fk_rows, *, dh, tq, s_mul):
    T, D = q.shape
    G = D // LANES
    hpg = LANES // dh
    nq = T // tq
    k3 = k.reshape(nq, tq, D)
    v3 = v.reshape(nq, tq, D)

    rc = min(FLASH_ROWS, tq)

    def body(q_ref, k_ref, v_ref, fq_ref, fk_ref, o_ref, o32_ref, lse_ref, s_scr, p_scr):
        i = pl.program_id(1)
        lane = lax.broadcasted_iota(jnp.int32, (1, LANES), 1)
        q2 = q_ref[...]
        hmasks = [(lane >= hh * dh) & (lane < (hh + 1) * dh) for hh in range(hpg)]
        qms = [jnp.where(hm, q2, jnp.zeros_like(q2)) for hm in hmasks]

        nlb = tq // LANES
        sum_lane = [((hh + 1) % hpg) * dh for hh in range(hpg)]

        def scores(j, slot):
            kj = k_ref[j]
            for hh in range(hpg):
                s = _dot_nt(qms[hh], kj)
                s_scr[slot, hh] = s if s_mul == 1.0 else s * s_mul

        def soft(j, slot, carry, masked):
            vj = v_ref[j]
            out = []
            for hh in range(hpg):
                m_b, acc = carry[hh]
                fk_row = fk_ref[j, hh:hh + 1, :]
                mx = []
                for r0 in range(0, tq, rc):
                    rs = slice(r0, r0 + rc)
                    s = s_scr[slot, hh, rs, :] - fk_row
                    if masked:
                        ri = r0 + lax.broadcasted_iota(jnp.int32, (rc, tq), 0)
                        ci = lax.broadcasted_iota(jnp.int32, (rc, tq), 1)
                        s = jnp.where(ci <= ri, s, MASK_VALUE)
                    s_scr[slot, hh, rs, :] = s
                    c = s[:, 0:LANES]
                    for cb in range(1, nlb):
                        c = jnp.maximum(c, s[:, cb * LANES:(cb + 1) * LANES])
                    mx.append(c)
                row_max = jnp.max(jnp.concatenate(mx, axis=0), axis=1, keepdims=True)
                m_new = jnp.maximum(m_b, row_max)
                alpha = jnp.exp(m_b - m_new)
                for r0 in range(0, tq, rc):
                    rs = slice(r0, r0 + rc)
                    m_c = m_new[rs]
                    for cb in range(nlb):
                        cs = slice(cb * LANES, (cb + 1) * LANES)
                        p_scr[hh, rs, cs] = jnp.exp(s_scr[slot, hh, rs, cs] - m_c).astype(BF16)
                v_one = jnp.where(hmasks[hh], vj, jnp.ones_like(vj))
                out.append((m_new, alpha * acc + _dot(p_scr[hh], v_one)))
            return tuple(out)

        def step(j, carry, masked):
            scores(j, 0)
            return soft(j, 0, carry, masked)

        init = tuple((jnp.full((tq, LANES), MASK_VALUE, F32), jnp.zeros((tq, LANES), F32)) for _ in range(hpg))
        carry = lax.fori_loop(0, i, lambda j, cr: step(j, cr, False), init)
        carry = step(i, carry, True)
        fq = fq_ref[...]
        o_all = jnp.zeros((tq, LANES), F32)
        lse_all = jnp.zeros((tq, LANES), F32)
        for hh in range(hpg):
            m_b, acc = carry[hh]
            l = acc[:, sum_lane[hh]:sum_lane[hh] + 1]
            o_all = jnp.where(hmasks[hh], acc * (1.0 / l), o_all)
            lse_all = jnp.where(lane == hh, m_b[:, 0:1] + jnp.log(l) + fq[:, hh:hh + 1], lse_all)
        o_ref[...] = o_all.astype(BF16)
        o32_ref[...] = o_all
        lse_ref[...] = lse_all

    return _pcall(
        body, name="flash_fwd", grid=(G, nq),
        in_specs=[pl.BlockSpec((tq, LANES), lambda g, i: (i, g)),
                  pl.BlockSpec((nq, tq, LANES), lambda g, i: (0, 0, g)),
                  pl.BlockSpec((nq, tq, LANES), lambda g, i: (0, 0, g)),
                  pl.BlockSpec((None, tq, LANES), lambda g, i: (g, i, 0)),
                  pl.BlockSpec((None, nq, SUBLANES, tq), lambda g, i: (g, 0, 0, 0))],
        out_specs=[pl.BlockSpec((tq, LANES), lambda g, i: (i, g)),
                   pl.BlockSpec((tq, LANES), lambda g, i: (i, g)),
                   pl.BlockSpec((None, tq, LANES), lambda g, i: (g, i, 0))],
        out_shape=[jax.ShapeDtypeStruct((T, D), BF16), jax.ShapeDtypeStruct((T, D), F32),
                   jax.ShapeDtypeStruct((G, T, LANES), F32)],
        scratch_shapes=[pltpu.VMEM((1, hpg, tq, tq), F32), pltpu.VMEM((hpg, tq, tq), BF16)],
        compiler_params=_params(("arbitrary", "arbitrary")),
    )(q, k3, v3, fq_aux, fk_rows)


def _flash_bwd(q, k, v, do, fq_aux, lse_aux, dl_aux, fk_rows, *, dh, tq, s_mul, dq_mul):
    T, D = q.shape
    G = D // LANES
    hpg = LANES // dh
    nq = T // tq
    k3 = k.reshape(nq, tq, D)
    v3 = v.reshape(nq, tq, D)
    rc = min(FLASH_ROWS, tq)

    def body(q_ref, k_ref, v_ref, do_ref, fq_ref, lse_ref, dl_ref, fk_ref,
             dq_ref, dk_ref, dv_ref, dfq_ref, dfk_ref, dk_acc, dv_acc, s_scr, dp_scr, p_scr, ds_scr):
        i = pl.program_id(1)

        @pl.when(i == 0)
        def _():
            dk_acc[...] = jnp.zeros_like(dk_acc)
            dv_acc[...] = jnp.zeros_like(dv_acc)
            dfk_ref[...] = jnp.zeros_like(dfk_ref)

        lane = lax.broadcasted_iota(jnp.int32, (1, LANES), 1)
        q2 = q_ref[...]
        do2 = do_ref[...]
        cq = fq_ref[...] - lse_ref[...]
        dl = dl_ref[...]
        hmasks = [(lane >= hh * dh) & (lane < (hh + 1) * dh) for hh in range(hpg)]
        qms = [jnp.where(hm, q2, jnp.zeros_like(q2)) for hm in hmasks]
        doms = [jnp.where(hm, do2, jnp.zeros_like(do2)) for hm in hmasks]
        c_bs = [jnp.broadcast_to(cq[:, hh:hh + 1], (tq, LANES)) for hh in range(hpg)]
        dl_bs = [jnp.broadcast_to(dl[:, hh:hh + 1], (tq, LANES)) for hh in range(hpg)]
        nlb = tq // LANES

        def step(j, carry, masked):
            kj = k_ref[j]
            vj = v_ref[j]
            for hh in range(hpg):
                s = _dot_nt(qms[hh], kj)
                s_scr[hh] = s if s_mul == 1.0 else s * s_mul
                dp_scr[hh] = _dot_nt(doms[hh], vj)
            out = []
            for hh in range(hpg):
                dq_acc, rs_p = carry[hh]
                fk_row = fk_ref[j, hh:hh + 1, :]
                rsums = []
                csums = [jnp.zeros((SUBLANES, LANES), F32) for _ in range(nlb)]
                for r0 in range(0, tq, rc):
                    rs = slice(r0, r0 + rc)
                    c_c = c_bs[hh][rs]
                    dl_c = dl_bs[hh][rs]
                    tot = None
                    for cb in range(nlb):
                        cs = slice(cb * LANES, (cb + 1) * LANES)
                        e = (s_scr[hh, rs, cs] - fk_row[:, cs]) + c_c
                        if masked:
                            ri = r0 + lax.broadcasted_iota(jnp.int32, (rc, LANES), 0)
                            ci = cb * LANES + lax.broadcasted_iota(jnp.int32, (rc, LANES), 1)
                            e = jnp.where(ci <= ri, e, MASK_VALUE)
                        p = jnp.exp(e)
                        ds = p * (dp_scr[hh, rs, cs] - dl_c)
                        p_scr[hh, rs, cs] = p.astype(BF16)
                        ds_scr[hh, rs, cs] = ds.astype(BF16)
                        tot = ds if tot is None else tot + ds
                        csums[cb] = csums[cb] + ds.reshape(rc // SUBLANES, SUBLANES, LANES).sum(axis=0)
                    rsums.append(tot)
                col = jnp.concatenate([jnp.sum(c, axis=0, keepdims=True) for c in csums], axis=1)
                dfk_ref[j, hh:hh + 1, :] += -col
                out.append((dq_acc + _dot(ds_scr[hh], kj), rs_p + jnp.concatenate(rsums, axis=0)))
            dk_new = _dot_tn(ds_scr[0], qms[0])
            dv_new = _dot_tn(p_scr[0], doms[0])
            for hh in range(1, hpg):
                dk_new = dk_new + _dot_tn(ds_scr[hh], qms[hh])
                dv_new = dv_new + _dot_tn(p_scr[hh], doms[hh])
            dk_acc[j] += dk_new
            dv_acc[j] += dv_new
            return tuple(out)

        init = tuple((jnp.zeros((tq, LANES), F32), jnp.zeros((tq, LANES), F32)) for _ in range(hpg))
        carry = lax.fori_loop(0, i, lambda j, cr: step(j, cr, False), init)
        carry = step(i, carry, True)
        dq_all = jnp.zeros((tq, LANES), F32)
        dfq_all = jnp.zeros((tq, LANES), F32)
        for hh in range(hpg):
            dq_h, rs_p = carry[hh]
            dq_all = jnp.where(hmasks[hh], dq_h, dq_all)
            dfq_all = jnp.where(lane == hh, jnp.sum(rs_p, axis=1, keepdims=True), dfq_all)
        dq_ref[...] = (dq_all * dq_mul).astype(BF16)
        dfq_ref[...] = dfq_all

        @pl.when(i == nq - 1)
        def _():
            dkv = dk_acc[...]
            if s_mul != 1.0:
                dkv = dkv * s_mul
            dk_ref[...] = dkv.astype(BF16)
            dv_ref[...] = dv_acc[...].astype(BF16)

    blk = pl.BlockSpec((tq, LANES), lambda g, i: (i, g))
    res = pl.BlockSpec((nq, tq, LANES), lambda g, i: (0, 0, g))
    aux = pl.BlockSpec((None, tq, LANES), lambda g, i: (g, i, 0))
    rows = pl.BlockSpec((None, nq, SUBLANES, tq), lambda g, i: (g, 0, 0, 0))
    dq, dk3, dv3, dfq, dfk = _pcall(
        body, name="flash_bwd", grid=(G, nq),
        in_specs=[blk, res, res, blk, aux, aux, aux, rows],
        out_specs=[blk, res, res, aux, rows],
        out_shape=[jax.ShapeDtypeStruct((T, D), BF16), jax.ShapeDtypeStruct((nq, tq, D), BF16),
                   jax.ShapeDtypeStruct((nq, tq, D), BF16), jax.ShapeDtypeStruct((G, T, LANES), F32),
                   jax.ShapeDtypeStruct((G, nq, SUBLANES, tq), F32)],
        scratch_shapes=[pltpu.VMEM((nq, tq, LANES), F32), pltpu.VMEM((nq, tq, LANES), F32),
                        pltpu.VMEM((hpg, tq, tq), F32), pltpu.VMEM((hpg, tq, tq), F32),
                        pltpu.VMEM((hpg, tq, tq), BF16), pltpu.VMEM((hpg, tq, tq), BF16)],
        compiler_params=_params(("arbitrary", "arbitrary")),
    )(q, k3, v3, do, fq_aux, lse_aux, dl_aux, fk_rows)
    return dq, dk3.reshape(T, D), dv3.reshape(T, D), dfq, dfk


def _attn_out_fwd(o, x, wo, g_post, tm):
    T, D = x.shape

    def body(o_ref, x_ref, w_ref, g_ref, m_ref, xo_ref):
        m = _dot(o_ref[...], w_ref[...])
        m_ref[...] = m
        xo_ref[...] = x_ref[...] + _rms(m, g_ref[...])

    return _pcall(
        body, name="attn_out_fwd", grid=(T // tm,),
        in_specs=[_rows(tm, D), _rows(tm, D), _full((D, D)), _full((1, D))],
        out_specs=[_rows(tm, D), _rows(tm, D)],
        out_shape=[jax.ShapeDtypeStruct((T, D), F32), jax.ShapeDtypeStruct((T, D), F32)],
        compiler_params=_params(("arbitrary",)),
    )(o, x, wo, g_post)


def _attn_out_bwd(dxo, m, o, wo, g_post, head_ind, tm):
    T, D = m.shape

    def body(dxo_ref, m_ref, o_ref, w_ref, g_ref, ind_ref, dm_ref, do_ref, dl_ref, sums_ref):
        i = pl.program_id(0)
        dm, dgpost = _rms_bwd(m_ref[...], g_ref[...], dxo_ref[...])
        dmb = dm.astype(BF16)
        dm_ref[...] = dmb
        dob = _dot_nt(dmb, w_ref[...]).astype(BF16)
        do_ref[...] = dob
        dl_ref[...] = jnp.dot(dob.astype(F32) * o_ref[...], ind_ref[...], precision=lax.Precision.HIGHEST,
                              preferred_element_type=F32)

        @pl.when(i == 0)
        def _():
            sums_ref[...] = jnp.zeros_like(sums_ref)
        sums_ref[0:1, :] += dgpost

    return _pcall(
        body, name="attn_out_bwd", grid=(T // tm,),
        in_specs=[_rows(tm, D), _rows(tm, D), _rows(tm, D), _full((D, D)), _full((1, D)), _full((D, LANES))],
        out_specs=[_rows(tm, D), _rows(tm, D), _rows(tm, LANES), _acc((SUBLANES, D))],
        out_shape=[jax.ShapeDtypeStruct((T, D), BF16), jax.ShapeDtypeStruct((T, D), BF16),
                   jax.ShapeDtypeStruct((T, LANES), F32), jax.ShapeDtypeStruct((SUBLANES, D), F32)],
        compiler_params=_params(("arbitrary",)),
    )(dxo, m, o, wo, g_post, head_ind)


def _attn_in_bwd(dxo, x, g_pre, dq, dk, dv, dlf, lf, wqkv, wf, tm, n_heads):
    T, D = x.shape

    def body(dxo_ref, x_ref, g_ref, dq_ref, dk_ref, dv_ref, dlf_ref, lf_ref, w_ref, wf_ref,
             dxi_ref, h_ref, df_ref, sums_ref, dbf_ref):
        i = pl.program_id(0)
        xv = x_ref[...]
        h_ref[...] = _rms(xv, g_ref[...]).astype(BF16)
        lane = lax.broadcasted_iota(jnp.int32, (1, LANES), 1)
        df = jnp.where(lane < n_heads, dlf_ref[...] * (1.0 - jnp.exp(lf_ref[...])), 0.0)
        dfb = df.astype(BF16)
        df_ref[...] = dfb
        dh = (_dot_nt(dq_ref[...], w_ref[:, 0:D]) + _dot_nt(dk_ref[...], w_ref[:, D:2 * D])
              + _dot_nt(dv_ref[...], w_ref[:, 2 * D:3 * D]) + _dot_nt(dfb, wf_ref[...]))
        dxi, dgpre = _rms_bwd(xv, g_ref[...], dh)
        dxi_ref[...] = dxo_ref[...] + dxi

        @pl.when(i == 0)
        def _():
            sums_ref[...] = jnp.zeros_like(sums_ref)
            dbf_ref[...] = jnp.zeros_like(dbf_ref)
        sums_ref[0:1, :] += dgpre
        dbf_ref[...] += jnp.sum(df, axis=0, keepdims=True)

    return _pcall(
        body, name="attn_in_bwd", grid=(T // tm,),
        in_specs=[_rows(tm, D), _rows(tm, D), _full((1, D)), _rows(tm, D), _rows(tm, D), _rows(tm, D),
                  _rows(tm, LANES), _rows(tm, LANES), _full((D, 3 * D)), _full((D, LANES))],
        out_specs=[_rows(tm, D), _rows(tm, D), _rows(tm, LANES), _acc((SUBLANES, D)), _acc((1, LANES))],
        out_shape=[jax.ShapeDtypeStruct((T, D), F32), jax.ShapeDtypeStruct((T, D), BF16),
                   jax.ShapeDtypeStruct((T, LANES), BF16), jax.ShapeDtypeStruct((SUBLANES, D), F32),
                   jax.ShapeDtypeStruct((1, LANES), F32)],
        compiler_params=_params(("arbitrary",)),
    )(dxo, x, g_pre, dq, dk, dv, dlf, lf, wqkv, wf)


def _loss_head(y, target, tm):
    T, D = y.shape

    def body(y_ref, t_ref, dy_ref, loss_ref):
        i = pl.program_id(0)
        err = y_ref[...] - t_ref[...]
        dy_ref[...] = err * (1.0 / D)
        part = 0.5 * jnp.sum(jnp.mean(err * err, axis=-1, keepdims=True), axis=0, keepdims=True)

        @pl.when(i == 0)
        def _():
            loss_ref[...] = jnp.zeros_like(loss_ref)
        loss_ref[...] += part

    return _pcall(
        body, name="loss_head", grid=(T // tm,),
        in_specs=[_rows(tm, D), _rows(tm, D)],
        out_specs=[_rows(tm, D), _acc((SUBLANES, LANES))],
        out_shape=[jax.ShapeDtypeStruct((T, D), F32), jax.ShapeDtypeStruct((SUBLANES, LANES), F32)],
        compiler_params=_params(("arbitrary",)),
    )(y, target)


def _adamw(recvs, w, m, v, tr, name):
    L, R, C = w.shape
    assert len(recvs) == L
    c1 = 1.0 - ADAM_B1 ** ADAM_STEP
    c2 = 1.0 - ADAM_B2 ** ADAM_STEP

    def body(*refs):
        r_refs = refs[:L]
        w_ref, m_ref, v_ref, g_ref, d_ref, nm_ref, nv_ref = refs[L:]
        layer = pl.program_id(0)
        g = None
        for k in range(L):
            gk = r_refs[k][0].astype(F32)
            for s in range(1, N_DEV):
                gk = gk + r_refs[k][s].astype(F32)
            g = gk if g is None else jnp.where(layer == k, gk, g)
        nm = ADAM_B1 * m_ref[...] + (1.0 - ADAM_B1) * g
        nv = ADAM_B2 * v_ref[...] + (1.0 - ADAM_B2) * jnp.square(g)
        m_hat = nm / c1
        v_hat = nv / c2
        g_ref[...] = g
        d_ref[...] = -ADAM_LR * (m_hat / (jnp.sqrt(v_hat) + ADAM_EPS) + ADAM_WD * w_ref[...])
        nm_ref[...] = nm
        nv_ref[...] = nv

    def recv_spec(k):
        return pl.BlockSpec((N_DEV, tr, C), lambda l, i: (0, jnp.where(l == k, i, 0), 0))

    blk = pl.BlockSpec((None, tr, C), lambda l, i: (l, i, 0))
    return _pcall(
        body, name=name, grid=(L, R // tr),
        in_specs=[recv_spec(k) for k in range(L)] + [blk] * 3,
        out_specs=[blk] * 4,
        out_shape=[jax.ShapeDtypeStruct((L, R, C), F32)] * 4,
        compiler_params=_params(("arbitrary", "arbitrary")),
    )(*recvs, w, m, v)


def _row_block(rows, cols):
    cap = max(SUBLANES, (256 * 1024) // max(cols, 1))
    best = None
    for t in range(SUBLANES, rows + 1, SUBLANES):
        if rows % t == 0 and t <= cap:
            best = t
    return rows if best is None else best


def kernel(x, g_mix_pre, g_mix_post, g_ffn_pre, g_ffn_post, conv_pw1_w, conv_pw1_b, conv_dw_w, conv_dw_b, conv_ln_g, conv_ln_b, conv_pw2_w, conv_pw2_b, attn_w_in, attn_b_f, attn_w_o, mlp_w_up, mlp_w_down, loss_target, m_g_mix_pre, m_g_mix_post, m_g_ffn_pre, m_g_ffn_post, m_conv_pw1_w, m_conv_pw1_b, m_conv_dw_w, m_conv_dw_b, m_conv_ln_g, m_conv_ln_b, m_conv_pw2_w, m_conv_pw2_b, m_attn_w_in, m_attn_b_f, m_attn_w_o, m_mlp_w_up, m_mlp_w_down, v_g_mix_pre, v_g_mix_post, v_g_ffn_pre, v_g_ffn_post, v_conv_pw1_w, v_conv_pw1_b, v_conv_dw_w, v_conv_dw_b, v_conv_ln_g, v_conv_ln_b, v_conv_pw2_w, v_conv_pw2_b, v_attn_w_in, v_attn_b_f, v_attn_w_o, v_mlp_w_up, v_mlp_w_down):
    _, T, D = x.shape
    H = attn_b_f.shape[-1]
    dh = D // H
    width = conv_dw_w.shape[1]
    cin = attn_w_in.shape[-1]
    fs = mlp_w_up.shape[-1]
    G = D // LANES
    hpg = LANES // dh
    assert T % 4 == 0 and D % LANES == 0 and LANES % dh == 0 and width <= CONV_HALO and H <= LANES

    tm = min(512, T // 4)
    tmb = min(256, T // 4)
    tq = tm
    tmc = min(256, T // 4)
    lc = min(256, D)
    tkw = min(1024, T // 4)
    tb = min(256, T // 4)
    nq = T // tq

    scale = float(dh) ** -0.5
    mant, _ = math.frexp(scale)
    q_mul = scale if mant == 0.5 else 1.0
    s_mul = 1.0 if mant == 0.5 else scale

    x2 = x.reshape(T, D)
    tgt = loss_target.reshape(T, D)

    w_srcs = [conv_pw1_w, conv_dw_w, conv_pw2_w, mlp_w_up, mlp_w_down, attn_w_in, attn_w_o]
    w_items = [(0, 0, "whole"), (1, 0, "whole"), (2, 0, "whole"), (3, 0, "whole"), (4, 0, "whole"),
               (5, 0, "whole"), (6, 0, "whole"), (3, 1, "whole"), (4, 1, "whole")]
    w_lands = _place_own(
        w_srcs, w_items,
        [((N_DEV,) + w_srcs[si].shape[1:], F32 if si == 1 else BF16) for si, _, _ in w_items],
        "stage_weights", cast=True)
    w_groups = [[0, 1, 2], [3, 4], [5, 6], [7, 8]]
    g_sems, _, w_lands, g_token = _push_start(
        [], w_lands, [[(a, None, "own") for a in grp] for grp in w_groups], "gather_start")

    def gather_wait(gi, after):
        grp = w_groups[gi]
        return _push_wait(g_sems[gi], [], [w_lands[a] for a in grp], [(k, None, "own") for k in range(len(grp))],
                          after, "gather_wait%d" % gi)

    w1g, dwg, w2g = gather_wait(0, g_token)
    w2 = w2g.reshape(D, D)
    dw_full = jnp.transpose(dwg, (1, 0, 2)).reshape(width, D)
    w32 = jnp.pad(dw_full, ((0, 32 - width), (0, 0)))
    bf = jnp.pad(attn_b_f, ((0, 0), (0, LANES - H)))

    row = lambda a, i: a[i:i + 1]

    a0, u0 = _conv_in_fwd(x2, row(g_mix_pre, 0), w1g, conv_pw1_b, tm)
    y0 = _dwconv_fwd(u0, w32, conv_dw_b, tmc, lc, width)
    m0, x_1 = _conv_out_fwd(y0, x2, conv_ln_g, conv_ln_b, w2, conv_pw2_b, row(g_mix_post, 0), tm)
    wu0, wd0 = gather_wait(1, x_1)
    up0, n0, x_2 = _mlp_fwd(x_1, row(g_ffn_pre, 0), wu0, wd0, row(g_ffn_post, 0), tm, "mlp0_fwd")

    wing, wog = gather_wait(2, x_2)
    wo = wog.reshape(D, D)
    win = jnp.transpose(wing, (1, 0, 2)).reshape(D, N_DEV * cin)
    wqkv = win[:, :3 * D]
    wf = jnp.pad(win[:, 3 * D:], ((0, 0), (0, LANES - H)))
    q, k, v, lf = _attn_in_fwd(x_2, row(g_mix_pre, 1), wqkv, wf, bf, tm, q_mul, H)
    fcum = _cumsum_rows(lf, jnp.zeros_like(lf), tb, False, "forget_cumsum")

    def to_aux(t):
        t = jnp.transpose(t[:, :H].reshape(T, G, hpg), (1, 0, 2))
        return jnp.pad(t, ((0, 0), (0, 0), (0, LANES - hpg)))

    fq_aux = to_aux(fcum)
    fk_rows = jnp.transpose(fcum[:, :H].T.reshape(G, hpg, nq, tq), (0, 2, 1, 3))
    fk_rows = jnp.pad(fk_rows, ((0, 0), (0, 0), (0, SUBLANES - hpg), (0, 0)))
    o, o32, lse_aux = _flash_fwd(q, k, v, fq_aux, fk_rows, dh=dh, tq=tq, s_mul=s_mul)
    m1, x_3 = _attn_out_fwd(o, x_2, wo, row(g_mix_post, 1), tm)
    wu1, wd1 = gather_wait(3, x_3)
    up1, n1, x_4 = _mlp_fwd(x_3, row(g_ffn_pre, 1), wu1, wd1, row(g_ffn_post, 1), tm, "mlp1_fwd")

    dx, loss_blk = _loss_head(x_4, tgt, tm)

    def mlp_back(dx, n_l, x_in, up_l, l, wu, wd):
        dxi, h, dm, dup, sums = _mlp_bwd(dx, n_l, x_in, up_l, row(g_ffn_pre, l), wu, wd, row(g_ffn_post, l),
                                         tmb, "mlp%d_bwd" % l)
        dwu = _mm_tn(h, dup, nj=N_DEV, a_cols=D, g_cols=fs, a_by_j=False, g_by_j=True, tk=tkw,
                     name="mlp%d_dwu" % l)
        dwd = _mm_tn(up_l, dm, nj=N_DEV, a_cols=fs, g_cols=D, a_by_j=True, g_by_j=False, tk=tkw,
                     name="mlp%d_dwd" % l, act=True)
        return dxi, dwu, dwd, sums

    def push_grads(srcs, modes, after, name):
        items = [(i, None, mode) for i, mode in enumerate(modes)]
        shapes = [(s.shape if mode == "slot" else (N_DEV,) + s.shape, s.dtype) for s, mode in zip(srcs, modes)]
        lands = _place_own(srcs, items, shapes, name + "_own")
        sems, srcs_t, lands_t, token = _push_start(srcs, lands, [items], name + "_start")
        return (sems[0], srcs_t, lands_t, items), lax.optimization_barrier((after, token))[0]

    def pull_grads(handle, after, name):
        sems, srcs_t, lands_t, items = handle
        return _push_wait(sems, srcs_t, lands_t, items, after, name + "_wait")

    rs = D // N_DEV
    dx, dwu1, dwd1, s_mlp1 = mlp_back(dx, n1, x_3, up1, 1, wu1, wd1)
    h_mlp1, dx = push_grads([dwu1, dwd1], ["slot", "slot"], dx, "grads_mlp1")

    head_ind = jnp.asarray((np.arange(D)[:, None] // dh == np.arange(LANES)[None, :]).astype(np.float32))
    dm1, do, delta, s_ao = _attn_out_bwd(dx, m1, o32, wo, row(g_mix_post, 1), head_ind, tm)
    dwo = _mm_tn(o, dm1, nj=1, a_cols=D, g_cols=D, a_by_j=False, g_by_j=False, tk=tkw, name="attn_dwo")
    dq, dk, dv, dfq, dfk = _flash_bwd(q, k, v, do, fq_aux, lse_aux, to_aux(delta), fk_rows,
                                      dh=dh, tq=tq, s_mul=s_mul, dq_mul=scale)
    df_k = jnp.pad(jnp.transpose(dfk[:, :, :hpg, :], (0, 2, 1, 3)).reshape(H, T).T, ((0, 0), (0, LANES - H)))
    df_q = jnp.pad(jnp.transpose(dfq[:, :, :hpg], (1, 0, 2)).reshape(T, H), ((0, 0), (0, LANES - H)))
    dlf = _cumsum_rows(df_q, df_k, tb, True, "forget_cumsum_bwd")
    dx, h_at, df, s_ai, dbf = _attn_in_bwd(dx, x_2, row(g_mix_pre, 1), dq, dk, dv, dlf, lf, wqkv, wf, tm, H)
    dwq = _mm_tn(h_at, dq, nj=1, a_cols=D, g_cols=D, a_by_j=False, g_by_j=False, tk=tkw, name="attn_dwq")
    dwk = _mm_tn(h_at, dk, nj=1, a_cols=D, g_cols=D, a_by_j=False, g_by_j=False, tk=tkw, name="attn_dwk")
    dwv = _mm_tn(h_at, dv, nj=1, a_cols=D, g_cols=D, a_by_j=False, g_by_j=False, tk=tkw, name="attn_dwv")
    dwf = _mm_tn(h_at, df, nj=1, a_cols=D, g_cols=LANES, a_by_j=False, g_by_j=False, tk=tkw, name="attn_dwf")
    dwin = jnp.concatenate([dwq[0], dwk[0], dwv[0], dwf[0][:, :H]], axis=1)
    dwin = jnp.transpose(dwin.reshape(D, N_DEV, cin), (1, 0, 2))
    h_attn, dx = push_grads([dwin, dwo.reshape(N_DEV, rs, D)], ["slot", "slot"], dx, "grads_attn")

    dx, dwu0, dwd0, s_mlp0 = mlp_back(dx, n0, x_1, up0, 0, wu0, wd0)
    h_mlp0, dx = push_grads([dwu0, dwd0], ["slot", "slot"], dx, "grads_mlp0")

    dy0, dm0, z0, s_co = _conv_out_bwd(dx, m0, y0, conv_ln_g, conv_ln_b, w2, row(g_mix_post, 0), tm)
    dw2 = _mm_tn(z0, dm0, nj=1, a_cols=D, g_cols=D, a_by_j=False, g_by_j=False, tk=tkw, name="conv_dw2")
    du0, ddw = _dwconv_bwd(dy0, u0, w32, tmc, lc, width)
    grad_x, h_cv, da0, s_ci, db1 = _conv_in_bwd(dx, du0, a0, x2, row(g_mix_pre, 0), w1g, tm)
    dw1 = _mm_tn(h_cv, da0, nj=N_DEV, a_cols=D, g_cols=(2 * D) // N_DEV, a_by_j=False, g_by_j=True, tk=tkw,
                 name="conv_dw1")
    ddw_s = jnp.transpose(ddw[:width].reshape(width, N_DEV, D // N_DEV), (1, 0, 2))

    def pad_row(a):
        return jnp.pad(a, ((0, 0), (0, D - a.shape[1])))

    def pack(gmp, gmq, gfp, gfq, b1, dwb, lng, lnb, b2, bfv, last):
        return jnp.concatenate([gmp, gmq, gfp, gfq, b1.reshape(2, D), dwb, lng, lnb, b2, pad_row(bfv), last],
                               axis=0)

    zero_row = jnp.zeros((1, D), F32)
    small_g = pack(
        jnp.concatenate([row(s_ci, 0), row(s_ai, 0)], axis=0),
        jnp.concatenate([row(s_co, 0), row(s_ao, 0)], axis=0),
        jnp.concatenate([row(s_mlp0, 1), row(s_mlp1, 1)], axis=0),
        jnp.concatenate([row(s_mlp0, 0), row(s_mlp1, 0)], axis=0),
        db1, row(s_co, 4), row(s_co, 1), row(s_co, 2), row(s_co, 3), dbf[:, :H],
        pad_row(loss_blk[0:1, 0:1]))
    h_conv, grad_x = push_grads([dw1, ddw_s, dw2.reshape(N_DEV, rs, D), small_g], ["slot", "slot", "slot", "whole"],
                                grad_x, "grads_conv")

    def opt(recvs, w, m, v, name):
        shp = w.shape
        L, C = shp[0], shp[-1]
        R = int(np.prod(shp[1:-1]))
        outs = _adamw([r.reshape(N_DEV, R, C) for r in recvs], w.reshape(L, R, C), m.reshape(L, R, C),
                      v.reshape(L, R, C), _row_block(R, C), name)
        return [t.reshape(shp) for t in outs]

    big = {}
    r_wu1, r_wd1 = pull_grads(h_mlp1, grad_x, "grads_mlp1")
    r_win, r_wo = pull_grads(h_attn, r_wd1, "grads_attn")
    big["attn_w_in"] = opt([r_win], attn_w_in, m_attn_w_in, v_attn_w_in, "adamw_win")
    big["attn_w_o"] = opt([r_wo], attn_w_o, m_attn_w_o, v_attn_w_o, "adamw_wo")
    r_wu0, r_wd0 = pull_grads(h_mlp0, big["attn_w_o"][0], "grads_mlp0")
    big["mlp_w_up"] = opt([r_wu0, r_wu1], mlp_w_up, m_mlp_w_up, v_mlp_w_up, "adamw_wup")
    big["mlp_w_down"] = opt([r_wd0, r_wd1], mlp_w_down, m_mlp_w_down, v_mlp_w_down, "adamw_wdown")
    r_w1, r_dw, r_w2, r_small = pull_grads(h_conv, big["mlp_w_down"][0], "grads_conv")
    big["conv_pw1_w"] = opt([r_w1], conv_pw1_w, m_conv_pw1_w, v_conv_pw1_w, "adamw_pw1")
    big["conv_dw_w"] = opt([r_dw], conv_dw_w, m_conv_dw_w, v_conv_dw_w, "adamw_dw")
    big["conv_pw2_w"] = opt([r_w2], conv_pw2_w, m_conv_pw2_w, v_conv_pw2_w, "adamw_pw2")
    small_w = pack(g_mix_pre, g_mix_post, g_ffn_pre, g_ffn_post, conv_pw1_b, conv_dw_b, conv_ln_g, conv_ln_b,
                   conv_pw2_b, attn_b_f, zero_row)
    small_m = pack(m_g_mix_pre, m_g_mix_post, m_g_ffn_pre, m_g_ffn_post, m_conv_pw1_b, m_conv_dw_b, m_conv_ln_g,
                   m_conv_ln_b, m_conv_pw2_b, m_attn_b_f, zero_row)
    small_v = pack(v_g_mix_pre, v_g_mix_post, v_g_ffn_pre, v_g_ffn_post, v_conv_pw1_b, v_conv_dw_b, v_conv_ln_g,
                   v_conv_ln_b, v_conv_pw2_b, v_attn_b_f, zero_row)
    sm = _adamw([r_small], small_w[None], small_m[None], small_v[None], small_w.shape[0], "adamw_small")
    sm = [t[0] for t in sm]
    loss = sm[0][15, 0]

    def unpack(t):
        return {"g_mix_pre": t[0:2], "g_mix_post": t[2:4], "g_ffn_pre": t[4:6], "g_ffn_post": t[6:8],
                "conv_pw1_b": t[8:10].reshape(1, 2 * D), "conv_dw_b": t[10:11], "conv_ln_g": t[11:12],
                "conv_ln_b": t[12:13], "conv_pw2_b": t[13:14], "attn_b_f": t[14:15, :H]}

    small = [unpack(t) for t in sm]
    names = ["g_mix_pre", "g_mix_post", "g_ffn_pre", "g_ffn_post", "conv_pw1_w", "conv_pw1_b", "conv_dw_w",
             "conv_dw_b", "conv_ln_g", "conv_ln_b", "conv_pw2_w", "conv_pw2_b", "attn_w_in", "attn_b_f",
             "attn_w_o", "mlp_w_up", "mlp_w_down"]
    outs = [loss, grad_x.reshape(1, T, D)]
    for kind in range(4):
        for nme in names:
            outs.append(big[nme][kind] if nme in big else small[kind][nme])
    return tuple(outs)
```

```python
import functools
import math

import numpy as np
import jax
import jax.numpy as jnp
from jax import lax
from jax.experimental import pallas as pl
from jax.experimental.pallas import tpu as pltpu

F32 = jnp.float32
BF16 = jnp.bfloat16

RMS_EPS = 1e-6
LN_EPS = 1e-5
MASK_VALUE = -1e30
ADAM_LR = 0.001
ADAM_B1 = 0.9
ADAM_B2 = 0.999
ADAM_EPS = 1e-08
ADAM_WD = 0.01
ADAM_STEP = 10

N_DEV = 8
LANES = 128
SUBLANES = 8
CONV_HALO = 32
CONV_ROWS = 32
FLASH_ROWS = 32
VMEM_LIMIT = 56 * 1024 * 1024

_pcall = pl.pallas_call


def _params(sem=None):
    if sem is None:
        return pltpu.CompilerParams(vmem_limit_bytes=VMEM_LIMIT)
    return pltpu.CompilerParams(dimension_semantics=sem, vmem_limit_bytes=VMEM_LIMIT)


def _dot(a, b):
    return jnp.dot(a, b, preferred_element_type=F32)


def _dot_nt(a, b):
    return lax.dot_general(a, b, (((1,), (1,)), ((), ())), preferred_element_type=F32)


def _dot_tn(a, b):
    return lax.dot_general(a, b, (((0,), (0,)), ((), ())), preferred_element_type=F32)


def _full(shape):
    nd = len(shape)
    return pl.BlockSpec(shape, lambda *g: (0,) * nd, pipeline_mode=pl.Buffered(1))


def _acc(shape):
    nd = len(shape)
    return pl.BlockSpec(shape, lambda *g: (0,) * nd)


def _rows(tm, cols):
    return pl.BlockSpec((tm, cols), lambda i: (i, 0))


def _rms(x, g):
    r = lax.rsqrt(jnp.mean(x * x, axis=-1, keepdims=True) + RMS_EPS)
    return x * r * g


def _rms_bwd(x, g, dy):
    r = lax.rsqrt(jnp.mean(x * x, axis=-1, keepdims=True) + RMS_EPS)
    n = x * r
    dg = jnp.sum(dy * n, axis=0, keepdims=True)
    dn = dy * g
    dx = r * (dn - n * jnp.mean(dn * n, axis=-1, keepdims=True))
    return dx, dg


def _sigmoid(x):
    return 1.0 / (1.0 + jnp.exp(-x))


def _mesh_pos():
    return lax.axis_index("x"), lax.axis_index("y"), lax.axis_index("c")


def _dev_index(px, py, pc):
    return 4 * px + 2 * py + pc


_HBM = pl.BlockSpec(memory_space=pltpu.HBM)
_SEM = pl.BlockSpec(memory_space=pltpu.SEMAPHORE)
_EFFECT = pltpu.SideEffectType.DATAFLOW_SIDE_EFFECTING


def _peer(r, x, y, c):
    p = ((1 - x) if r & 4 else x, (1 - y) if r & 2 else y, (1 - c) if r & 1 else c)
    return p, _dev_index(*p)


def _src_ref(refs, item, me_id=None, to_id=None):
    si, sub, mode = item
    r = refs[si] if sub is None else refs[si].at[sub]
    if mode == "slot":
        return r.at[to_id]
    if mode == "own":
        return r.at[me_id]
    return r


def _place_own(srcs, items, land_shapes, name, cast=False):
    ns, n = len(srcs), len(items)

    def body(*refs):
        src = refs[:ns]
        land = refs[ns:ns + n]
        stage = refs[ns + n:ns + 2 * n] if cast else None
        sems = refs[-1]
        me_id = _dev_index(*_mesh_pos())
        cps = []
        for a, item in enumerate(items):
            s = _src_ref(src, item, to_id=me_id)
            if cast:
                if s.shape != stage[a].shape:
                    stage[a][...] = jnp.zeros_like(stage[a])
                    stage[a][:, 0:s.shape[-1]] = s[...].astype(stage[a].dtype)
                else:
                    stage[a][...] = s[...].astype(stage[a].dtype)
                s = stage[a]
            cp = pltpu.make_async_copy(s, land[a].at[me_id], sems.at[a])
            cp.start()
            cps.append(cp)
        for cp in cps:
            cp.wait()

    return _pcall(
        body, name=name,
        out_shape=[jax.ShapeDtypeStruct(tuple(s), d) for s, d in land_shapes],
        in_specs=[pl.BlockSpec(memory_space=pltpu.VMEM if cast else pl.ANY)] * ns,
        out_specs=[pl.BlockSpec(memory_space=pl.ANY)] * n,
        scratch_shapes=([pltpu.VMEM(tuple(s[1:]), d) for s, d in land_shapes] if cast else [])
        + [pltpu.SemaphoreType.DMA((n,))],
        compiler_params=pltpu.CompilerParams(vmem_limit_bytes=VMEM_LIMIT),
    )(*srcs)


def _seed_lands(srcs, modes, me, name):
    n = len(srcs)
    parts = [tuple(s.shape[1:]) if mode == "slot" else tuple(s.shape) for s, mode in zip(srcs, modes)]

    def body(me_ref, *refs):
        s = pl.program_id(0)
        for a in range(n):
            v = refs[a][...]
            refs[n + a][...] = jnp.where(s == me_ref[0], v, jnp.zeros_like(v))

    def in_spec(part, mode):
        nd = len(part)
        if mode == "slot":
            return pl.BlockSpec((None,) + part, lambda s, me_ref: (me_ref[0],) + (0,) * nd)
        return pl.BlockSpec(part, lambda s, me_ref: (0,) * nd)

    def out_spec(part):
        nd = len(part)
        return pl.BlockSpec((None,) + part, lambda s, me_ref: (s,) + (0,) * nd)

    return _pcall(
        body, name=name,
        grid_spec=pltpu.PrefetchScalarGridSpec(
            num_scalar_prefetch=1, grid=(N_DEV,),
            in_specs=[in_spec(p, m) for p, m in zip(parts, modes)],
            out_specs=[out_spec(p) for p in parts]),
        out_shape=[jax.ShapeDtypeStruct((N_DEV,) + p, s.dtype) for p, s in zip(parts, srcs)],
        compiler_params=_params(("arbitrary",)),
    )(me, *srcs)


def _push_start(srcs, lands, groups, name):
    ns, n = len(srcs), len(lands)
    ng = len(groups)
    assert sum(len(g) for g in groups) == n

    def body(*refs):
        ops = refs[:ns + n]
        land = refs[ns:ns + n]
        sems = refs[ns + n:ns + n + 2 * ng]
        token = refs[-1]
        x, y, c = _mesh_pos()
        me_id = _dev_index(x, y, c)
        a = 0
        for gi, grp in enumerate(groups):
            for k, item in enumerate(grp):
                for r in range(1, N_DEV):
                    p, pid = _peer(r, x, y, c)
                    pltpu.make_async_remote_copy(
                        src_ref=_src_ref(ops, item, me_id=me_id, to_id=pid), dst_ref=land[a].at[me_id],
                        send_sem=sems[2 * gi].at[k * (N_DEV - 1) + r - 1],
                        recv_sem=sems[2 * gi + 1].at[k * (N_DEV - 1) + r - 1],
                        device_id=p, device_id_type=pl.DeviceIdType.MESH).start()
                a += 1
        token[...] = jnp.zeros_like(token)

    sem_shapes = []
    for grp in groups:
        sem_shapes += [pltpu.SemaphoreType.DMA((len(grp) * (N_DEV - 1),))] * 2
    arrs = list(srcs) + list(lands)
    res = _pcall(
        body, name=name,
        out_shape=tuple(sem_shapes) + tuple(pltpu.HBM(a.shape, a.dtype) for a in arrs)
        + (jax.ShapeDtypeStruct((SUBLANES, LANES), F32),),
        in_specs=[_HBM] * (ns + n),
        out_specs=tuple([_SEM] * (2 * ng)) + tuple([_HBM] * (ns + n)) + (pl.BlockSpec(memory_space=pltpu.VMEM),),
        input_output_aliases={i: 2 * ng + i for i in range(ns + n)},
        compiler_params=pltpu.CompilerParams(has_side_effects=_EFFECT),
    )(*[pltpu.with_memory_space_constraint(a, pltpu.HBM) for a in arrs])
    sems = [(res[2 * gi], res[2 * gi + 1]) for gi in range(ng)]
    thru = res[2 * ng:2 * ng + ns + n]
    return sems, list(thru[:ns]), list(thru[ns:]), res[-1]


def _push_wait(sems, srcs, lands, group, after, name):
    ns, n = len(srcs), len(lands)
    assert len(group) == n

    def body(*refs):
        ops = refs[:ns + n]
        land = refs[ns:ns + n]
        send_sems, recv_sems = refs[ns + n], refs[ns + n + 1]
        x, y, c = _mesh_pos()
        me_id = _dev_index(x, y, c)
        for k, item in enumerate(group):
            for r in range(1, N_DEV):
                p, pid = _peer(r, x, y, c)
                cp = pltpu.make_async_remote_copy(
                    src_ref=_src_ref(ops, item, me_id=me_id, to_id=pid), dst_ref=land[k].at[pid],
                    send_sem=send_sems.at[k * (N_DEV - 1) + r - 1], recv_sem=recv_sems.at[k * (N_DEV - 1) + r - 1],
                    device_id=p, device_id_type=pl.DeviceIdType.MESH)
                cp.wait_send()
                cp.wait_recv()

    arrs = list(srcs) + list(lands)
    res = _pcall(
        body, name=name,
        out_shape=tuple(pltpu.HBM(a.shape, a.dtype) for a in arrs),
        in_specs=[_HBM] * (ns + n) + [_SEM, _SEM, pl.BlockSpec(memory_space=pl.ANY)],
        out_specs=tuple([_HBM] * (ns + n)),
        input_output_aliases={i: i for i in range(ns + n)},
        compiler_params=pltpu.CompilerParams(has_side_effects=_EFFECT),
    )(*arrs, sems[0], sems[1], after)
    return list(res[ns:])


def _conv_in_fwd(x, g_pre, w1g, b1, tm):
    T, D = x.shape
    ns, _, cs = w1g.shape
    half = ns // 2

    def body(x_ref, g_ref, w_ref, b_ref, a_ref, u_ref):
        h = _rms(x_ref[...], g_ref[...]).astype(BF16)
        parts = []
        for s in range(ns):
            a_s = _dot(h, w_ref[s]) + b_ref[:, s * cs:(s + 1) * cs]
            a_ref[:, s * cs:(s + 1) * cs] = a_s
            parts.append(a_s)
        for s in range(half):
            u_ref[:, s * cs:(s + 1) * cs] = parts[s] * _sigmoid(parts[s + half])

    return _pcall(
        body, name="conv_in_fwd", grid=(T // tm,),
        in_specs=[_rows(tm, D), _full((1, D)), _full(w1g.shape), _full((1, 2 * D))],
        out_specs=[_rows(tm, 2 * D), _rows(tm, D)],
        out_shape=[jax.ShapeDtypeStruct((T, 2 * D), F32), jax.ShapeDtypeStruct((T, D), F32)],
        compiler_params=_params(("arbitrary",)),
    )(x, g_pre, w1g, b1)


def _shifted_copies(ext_ref, sh_ref, tm):
    n = tm + CONV_HALO - SUBLANES
    for b in range(1, SUBLANES):
        sh_ref[b - 1, 0:n, :] = ext_ref[b:b + n, :]


def _shifted_rows(ext_ref, sh_ref, off, r0, ls):
    b = off % SUBLANES
    a8 = off - b + r0
    src = ext_ref if b == 0 else sh_ref.at[b - 1]
    return src[a8:a8 + CONV_ROWS, ls]


def _dwconv_fwd(u, w32, b, tm, lc, width):
    T, D = u.shape
    hb = tm // CONV_HALO

    def body(u_ref, halo_ref, w_ref, b_ref, y_ref, ext_ref, sh_ref):
        i = pl.program_id(0)
        ext_ref[0:CONV_HALO, :] = jnp.where(i > 0, halo_ref[...], 0.0)
        ext_ref[CONV_HALO:, :] = u_ref[...]
        _shifted_copies(ext_ref, sh_ref, tm)
        for r0 in range(0, tm, CONV_ROWS):
            for l0 in range(0, lc, LANES):
                ls = slice(l0, l0 + LANES)
                acc = jnp.zeros((CONV_ROWS, LANES), F32) + b_ref[:, ls]
                for j in range(width):
                    off = CONV_HALO - (width - 1) + j
                    acc = acc + w_ref[j:j + 1, ls] * _shifted_rows(ext_ref, sh_ref, off, r0, ls)
                y_ref[r0:r0 + CONV_ROWS, ls] = acc

    return _pcall(
        body, name="dwconv_fwd", grid=(T // tm, D // lc),
        in_specs=[pl.BlockSpec((tm, lc), lambda i, l: (i, l)),
                  pl.BlockSpec((CONV_HALO, lc), lambda i, l: (jnp.maximum(i * hb - 1, 0), l)),
                  pl.BlockSpec((32, lc), lambda i, l: (0, l)),
                  pl.BlockSpec((1, lc), lambda i, l: (0, l))],
        out_specs=pl.BlockSpec((tm, lc), lambda i, l: (i, l)),
        out_shape=jax.ShapeDtypeStruct((T, D), F32),
        scratch_shapes=[pltpu.VMEM((tm + CONV_HALO, lc), F32),
                        pltpu.VMEM((SUBLANES - 1, tm + CONV_HALO, lc), F32)],
        compiler_params=_params(("arbitrary", "arbitrary")),
    )(u, u, w32, b)


def _ln_parts(y, g, b):
    mu = jnp.mean(y, axis=-1, keepdims=True)
    yc = y - mu
    rstd = lax.rsqrt(jnp.mean(yc * yc, axis=-1, keepdims=True) + LN_EPS)
    yhat = yc * rstd
    return yhat, rstd, yhat * g + b


def _conv_out_fwd(y, x, ln_g, ln_b, w2, b2, g_post, tm):
    T, D = x.shape

    def body(y_ref, x_ref, lg_ref, lb_ref, w_ref, b_ref, g_ref, m_ref, xo_ref):
        _, _, yn = _ln_parts(y_ref[...], lg_ref[...], lb_ref[...])
        z = (yn * _sigmoid(yn)).astype(BF16)
        m = _dot(z, w_ref[...]) + b_ref[...]
        m_ref[...] = m
        xo_ref[...] = x_ref[...] + _rms(m, g_ref[...])

    return _pcall(
        body, name="conv_out_fwd", grid=(T // tm,),
        in_specs=[_rows(tm, D), _rows(tm, D), _full((1, D)), _full((1, D)), _full((D, D)), _full((1, D)),
                  _full((1, D))],
        out_specs=[_rows(tm, D), _rows(tm, D)],
        out_shape=[jax.ShapeDtypeStruct((T, D), F32), jax.ShapeDtypeStruct((T, D), F32)],
        compiler_params=_params(("arbitrary",)),
    )(y, x, ln_g, ln_b, w2, b2, g_post)


def _conv_out_bwd(dxo, m, y, ln_g, ln_b, w2, g_post, tm):
    T, D = m.shape

    def body(dxo_ref, m_ref, y_ref, lg_ref, lb_ref, w_ref, g_ref, dy_ref, dm_ref, z_ref, sums_ref):
        i = pl.program_id(0)
        dm, dgpost = _rms_bwd(m_ref[...], g_ref[...], dxo_ref[...])
        dmb = dm.astype(BF16)
        dm_ref[...] = dmb
        yhat, rstd, yn = _ln_parts(y_ref[...], lg_ref[...], lb_ref[...])
        sg = _sigmoid(yn)
        z_ref[...] = (yn * sg).astype(BF16)
        dz = _dot_nt(dmb, w_ref[...])
        dyn = dz * (sg + yn * sg * (1.0 - sg))
        dyh = dyn * lg_ref[...]
        dy = rstd * (dyh - jnp.mean(dyh, axis=-1, keepdims=True)
                     - yhat * jnp.mean(dyh * yhat, axis=-1, keepdims=True))
        dy_ref[...] = dy

        @pl.when(i == 0)
        def _():
            sums_ref[...] = jnp.zeros_like(sums_ref)
        sums_ref[0:1, :] += dgpost
        sums_ref[1:2, :] += jnp.sum(dyn * yhat, axis=0, keepdims=True)
        sums_ref[2:3, :] += jnp.sum(dyn, axis=0, keepdims=True)
        sums_ref[3:4, :] += jnp.sum(dm, axis=0, keepdims=True)
        sums_ref[4:5, :] += jnp.sum(dy, axis=0, keepdims=True)

    return _pcall(
        body, name="conv_out_bwd", grid=(T // tm,),
        in_specs=[_rows(tm, D), _rows(tm, D), _rows(tm, D), _full((1, D)), _full((1, D)), _full((D, D)),
                  _full((1, D))],
        out_specs=[_rows(tm, D), _rows(tm, D), _rows(tm, D), _acc((SUBLANES, D))],
        out_shape=[jax.ShapeDtypeStruct((T, D), F32), jax.ShapeDtypeStruct((T, D), BF16),
                   jax.ShapeDtypeStruct((T, D), BF16), jax.ShapeDtypeStruct((SUBLANES, D), F32)],
        compiler_params=_params(("arbitrary",)),
    )(dxo, m, y, ln_g, ln_b, w2, g_post)


def _dwconv_bwd(dy, u, w32, tm, lc, width):
    T, D = u.shape
    hb = tm // CONV_HALO
    nt = T // tm
    last_halo = T // CONV_HALO - 1

    def body(dy_ref, dyn_ref, u_ref, up_ref, w_ref, du_ref, dw_ref, exty_ref, extu_ref, acc_ref, shy_ref, shu_ref):
        i = pl.program_id(1)
        exty_ref[0:tm, :] = dy_ref[...]
        exty_ref[tm:, :] = jnp.where(i < nt - 1, dyn_ref[...], 0.0)
        extu_ref[0:CONV_HALO, :] = jnp.where(i > 0, up_ref[...], 0.0)
        extu_ref[CONV_HALO:, :] = u_ref[...]
        _shifted_copies(exty_ref, shy_ref, tm)
        _shifted_copies(extu_ref, shu_ref, tm)

        @pl.when(i == 0)
        def _():
            acc_ref[...] = jnp.zeros_like(acc_ref)

        for r0 in range(0, tm, CONV_ROWS):
            for l0 in range(0, lc, LANES):
                ls = slice(l0, l0 + LANES)
                dyc = exty_ref[r0:r0 + CONV_ROWS, ls]
                du = jnp.zeros((CONV_ROWS, LANES), F32)
                for j in range(width):
                    du = du + w_ref[j:j + 1, ls] * _shifted_rows(exty_ref, shy_ref, (width - 1) - j, r0, ls)
                    prod = dyc * _shifted_rows(extu_ref, shu_ref, CONV_HALO - (width - 1) + j, r0, ls)
                    acc_ref[j, :, ls] += prod.reshape(CONV_ROWS // SUBLANES, SUBLANES, LANES).sum(axis=0)
                du_ref[r0:r0 + CONV_ROWS, ls] = du

        @pl.when(i == nt - 1)
        def _():
            for j in range(32):
                dw_ref[j:j + 1, :] = jnp.sum(acc_ref[j], axis=0, keepdims=True)

    return _pcall(
        body, name="dwconv_bwd", grid=(D // lc, nt),
        in_specs=[pl.BlockSpec((tm, lc), lambda l, i: (i, l)),
                  pl.BlockSpec((CONV_HALO, lc), lambda l, i: (jnp.minimum((i + 1) * hb, last_halo), l)),
                  pl.BlockSpec((tm, lc), lambda l, i: (i, l)),
                  pl.BlockSpec((CONV_HALO, lc), lambda l, i: (jnp.maximum(i * hb - 1, 0), l)),
                  pl.BlockSpec((32, lc), lambda l, i: (0, l))],
        out_specs=[pl.BlockSpec((tm, lc), lambda l, i: (i, l)),
                   pl.BlockSpec((32, lc), lambda l, i: (0, l))],
        out_shape=[jax.ShapeDtypeStruct((T, D), F32), jax.ShapeDtypeStruct((32, D), F32)],
        scratch_shapes=[pltpu.VMEM((tm + CONV_HALO, lc), F32), pltpu.VMEM((tm + CONV_HALO, lc), F32),
                        pltpu.VMEM((32, SUBLANES, lc), F32),
                        pltpu.VMEM((SUBLANES - 1, tm + CONV_HALO, lc), F32),
                        pltpu.VMEM((SUBLANES - 1, tm + CONV_HALO, lc), F32)],
        compiler_params=_params(("arbitrary", "arbitrary")),
    )(dy, dy, u, u, w32)


def _conv_in_bwd(dxo, du, a, x, g_pre, w1g, tm):
    T, D = x.shape
    ns, _, cs = w1g.shape
    half = ns // 2

    def body(dxo_ref, du_ref, a_ref, x_ref, g_ref, w_ref, dxi_ref, h_ref, da_ref, sums_ref, db_ref):
        i = pl.program_id(0)
        xv = x_ref[...]
        h_ref[...] = _rms(xv, g_ref[...]).astype(BF16)
        dh = jnp.zeros((tm, D), F32)
        dbs = [None] * ns
        for s in range(half):
            a_u = a_ref[:, s * cs:(s + 1) * cs]
            sg = _sigmoid(a_ref[:, (s + half) * cs:(s + half + 1) * cs])
            du_s = du_ref[:, s * cs:(s + 1) * cs]
            da_u = du_s * sg
            da_g = du_s * a_u * sg * (1.0 - sg)
            for s2, v in ((s, da_u), (s + half, da_g)):
                vb = v.astype(BF16)
                da_ref[:, s2 * cs:(s2 + 1) * cs] = vb
                dbs[s2] = jnp.sum(v, axis=0, keepdims=True)
                dh = dh + _dot_nt(vb, w_ref[s2])
        dxi, dgpre = _rms_bwd(xv, g_ref[...], dh)
        dxi_ref[...] = dxo_ref[...] + dxi

        @pl.when(i == 0)
        def _():
            sums_ref[...] = jnp.zeros_like(sums_ref)
            db_ref[...] = jnp.zeros_like(db_ref)
        sums_ref[0:1, :] += dgpre
        for s in range(ns):
            db_ref[:, s * cs:(s + 1) * cs] += dbs[s]

    return _pcall(
        body, name="conv_in_bwd", grid=(T // tm,),
        in_specs=[_rows(tm, D), _rows(tm, D), _rows(tm, 2 * D), _rows(tm, D), _full((1, D)),
                  _full(w1g.shape)],
        out_specs=[_rows(tm, D), _rows(tm, D), _rows(tm, 2 * D), _acc((SUBLANES, D)), _acc((1, 2 * D))],
        out_shape=[jax.ShapeDtypeStruct((T, D), F32), jax.ShapeDtypeStruct((T, D), BF16),
                   jax.ShapeDtypeStruct((T, 2 * D), BF16), jax.ShapeDtypeStruct((SUBLANES, D), F32),
                   jax.ShapeDtypeStruct((1, 2 * D), F32)],
        compiler_params=_params(("arbitrary",)),
    )(dxo, du, a, x, g_pre, w1g)


def _mlp_fwd(x, g_pre, wug, wdg, g_post, tm, name):
    T, D = x.shape
    ns, _, fs = wug.shape

    def body(x_ref, gp_ref, wu_ref, wd_ref, gq_ref, up_ref, m_ref, xo_ref):
        xv = x_ref[...]
        h = _rms(xv, gp_ref[...]).astype(BF16)
        acc = jnp.zeros((tm, D), F32)
        for s in range(ns):
            up = _dot(h, wu_ref[s]).astype(BF16)
            up_ref[:, s * fs:(s + 1) * fs] = up
            act = jnp.square(jnp.maximum(up.astype(F32), 0.0)).astype(BF16)
            acc = acc + _dot(act, wd_ref[s])
        m_ref[...] = acc
        xo_ref[...] = xv + _rms(acc, gq_ref[...])

    return _pcall(
        body, name=name, grid=(T // tm,),
        in_specs=[_rows(tm, D), _full((1, D)), _full(wug.shape), _full(wdg.shape), _full((1, D))],
        out_specs=[_rows(tm, ns * fs), _rows(tm, D), _rows(tm, D)],
        out_shape=[jax.ShapeDtypeStruct((T, ns * fs), BF16), jax.ShapeDtypeStruct((T, D), F32),
                   jax.ShapeDtypeStruct((T, D), F32)],
        compiler_params=_params(("arbitrary",)),
    )(x, g_pre, wug, wdg, g_post)


def _mlp_bwd(dxo, m, x, up, g_pre, wug, wdg, g_post, tm, name):
    T, D = x.shape
    ns, _, fs = wug.shape

    def body(dxo_ref, m_ref, x_ref, up_ref, gp_ref, wu_ref, wd_ref, gq_ref,
             dxi_ref, h_ref, dm_ref, dup_ref, sums_ref):
        i = pl.program_id(0)
        dxo = dxo_ref[...]
        dm, dgpost = _rms_bwd(m_ref[...], gq_ref[...], dxo)
        dmb = dm.astype(BF16)
        dm_ref[...] = dmb
        xv = x_ref[...]
        h_ref[...] = _rms(xv, gp_ref[...]).astype(BF16)
        dh = jnp.zeros((tm, D), F32)
        for s in range(ns):
            dact = _dot_nt(dmb, wd_ref[s])
            up = up_ref[:, s * fs:(s + 1) * fs].astype(F32)
            dup = (dact * (2.0 * jnp.maximum(up, 0.0))).astype(BF16)
            dup_ref[:, s * fs:(s + 1) * fs] = dup
            dh = dh + _dot_nt(dup, wu_ref[s])
        dxi, dgpre = _rms_bwd(xv, gp_ref[...], dh)
        dxi_ref[...] = dxo + dxi

        @pl.when(i == 0)
        def _():
            sums_ref[...] = jnp.zeros_like(sums_ref)
        sums_ref[0:1, :] += dgpost
        sums_ref[1:2, :] += dgpre

    return _pcall(
        body, name=name, grid=(T // tm,),
        in_specs=[_rows(tm, D), _rows(tm, D), _rows(tm, D), _rows(tm, ns * fs), _full((1, D)),
                  _full(wug.shape), _full(wdg.shape), _full((1, D))],
        out_specs=[_rows(tm, D), _rows(tm, D), _rows(tm, D), _rows(tm, ns * fs), _acc((SUBLANES, D))],
        out_shape=[jax.ShapeDtypeStruct((T, D), F32), jax.ShapeDtypeStruct((T, D), BF16),
                   jax.ShapeDtypeStruct((T, D), BF16), jax.ShapeDtypeStruct((T, ns * fs), BF16),
                   jax.ShapeDtypeStruct((SUBLANES, D), F32)],
        compiler_params=_params(("arbitrary",)),
    )(dxo, m, x, up, g_pre, wug, wdg, g_post)


def _mm_tn(a, g, *, nj, a_cols, g_cols, a_by_j, g_by_j, tk, name, act=False):
    T = a.shape[0]
    nk = T // tk

    def body(a_ref, g_ref, o_ref, acc_ref):
        k = pl.program_id(1)
        av = a_ref[...]
        if act:
            av = jnp.square(jnp.maximum(av.astype(F32), 0.0)).astype(BF16)
        p = _dot_tn(av, g_ref[...])

        @pl.when(k == 0)
        def _():
            acc_ref[...] = p

        @pl.when(k > 0)
        def _():
            acc_ref[...] += p

        @pl.when(k == nk - 1)
        def _():
            o_ref[...] = acc_ref[...].astype(BF16)

    return _pcall(
        body, name=name, grid=(nj, nk),
        in_specs=[pl.BlockSpec((tk, a_cols), (lambda j, k: (k, j)) if a_by_j else (lambda j, k: (k, 0))),
                  pl.BlockSpec((tk, g_cols), (lambda j, k: (k, j)) if g_by_j else (lambda j, k: (k, 0)))],
        out_specs=pl.BlockSpec((None, a_cols, g_cols), lambda j, k: (j, 0, 0)),
        out_shape=jax.ShapeDtypeStruct((nj, a_cols, g_cols), BF16),
        scratch_shapes=[pltpu.VMEM((a_cols, g_cols), F32)],
        compiler_params=_params(("arbitrary", "arbitrary")),
    )(a, g)


def _attn_in_fwd(x, g_pre, wqkv, wf, bf, tm, q_mul, n_heads):
    T, D = x.shape

    def body(x_ref, g_ref, w_ref, wf_ref, bf_ref, q_ref, k_ref, v_ref, lf_ref):
        h = _rms(x_ref[...], g_ref[...]).astype(BF16)
        q = _dot(h, w_ref[:, 0:D])
        if q_mul != 1.0:
            q = q * q_mul
        q_ref[...] = q.astype(BF16)
        k_ref[...] = _dot(h, w_ref[:, D:2 * D]).astype(BF16)
        v_ref[...] = _dot(h, w_ref[:, 2 * D:3 * D]).astype(BF16)
        fl = _dot(h, wf_ref[...]) + bf_ref[...]
        lf = jnp.minimum(fl, 0.0) - jnp.log(1.0 + jnp.exp(-jnp.abs(fl)))
        lane = lax.broadcasted_iota(jnp.int32, (1, LANES), 1)
        lf_ref[...] = jnp.where(lane < n_heads, lf, 0.0)

    return _pcall(
        body, name="attn_in_fwd", grid=(T // tm,),
        in_specs=[_rows(tm, D), _full((1, D)), _full((D, 3 * D)), _full((D, LANES)), _full((1, LANES))],
        out_specs=[_rows(tm, D), _rows(tm, D), _rows(tm, D), _rows(tm, LANES)],
        out_shape=[jax.ShapeDtypeStruct((T, D), BF16)] * 3 + [jax.ShapeDtypeStruct((T, LANES), F32)],
        compiler_params=_params(("arbitrary",)),
    )(x, g_pre, wqkv, wf, bf)


def _cumsum_rows(v, v2, tb, reverse, name):
    T, C = v.shape
    nb = T // tb

    def body(v_ref, v2_ref, o_ref, carry_ref):
        i = pl.program_id(0)

        @pl.when(i == 0)
        def _():
            carry_ref[...] = jnp.zeros_like(carry_ref)
        r = lax.broadcasted_iota(jnp.int32, (tb, tb), 0)
        c = lax.broadcasted_iota(jnp.int32, (tb, tb), 1)
        tri = jnp.where((c >= r) if reverse else (c <= r), 1.0, 0.0).astype(F32)
        out = jnp.dot(tri, v_ref[...] + v2_ref[...], precision=lax.Precision.HIGHEST,
                      preferred_element_type=F32) + carry_ref[...]
        o_ref[...] = out
        carry_ref[...] = out[0:1, :] if reverse else out[tb - 1:tb, :]

    idx = (lambda i: (nb - 1 - i, 0)) if reverse else (lambda i: (i, 0))
    return _pcall(
        body, name=name, grid=(nb,),
        in_specs=[pl.BlockSpec((tb, C), idx), pl.BlockSpec((tb, C), idx)],
        out_specs=pl.BlockSpec((tb, C), idx),
        out_shape=jax.ShapeDtypeStruct((T, C), F32),
        scratch_shapes=[pltpu.VMEM((1, C), F32)],
        compiler_params=_params(("arbitrary",)),
    )(v, v2)


def _flash_fwd(q, k, v, fq_aux, fk_rows, *, dh, tq, s_mul):
    T, D = q.shape
    G = D // LANES
    hpg = LANES // dh
    nq = T // tq
    k3 = k.reshape(nq, tq, D)
    v3 = v.reshape(nq, tq, D)

    rc = min(FLASH_ROWS, tq)

    def body(q_ref, k_ref, v_ref, fq_ref, fk_ref, o_ref, o32_ref, lse_ref, s_scr, p_scr):
        i = pl.program_id(1)
        lane = lax.broadcasted_iota(jnp.int32, (1, LANES), 1)
        q2 = q_ref[...]
        hmasks = [(lane >= hh * dh) & (lane < (hh + 1) * dh) for hh in range(hpg)]
        qms = [jnp.where(hm, q2, jnp.zeros_like(q2)) for hm in hmasks]

        nlb = tq // LANES
        sum_lane = [((hh + 1) % hpg) * dh for hh in range(hpg)]

        def scores(j, slot):
            kj = k_ref[j]
            for hh in range(hpg):
                s = _dot_nt(qms[hh], kj)
                s_scr[slot, hh] = s if s_mul == 1.0 else s * s_mul

        def soft(j, slot, carry, masked):
            vj = v_ref[j]
            out = []
            for hh in range(hpg):
                m_b, acc = carry[hh]
                fk_row = fk_ref[j, hh:hh + 1, :]
                mx = []
                for r0 in range(0, tq, rc):
                    rs = slice(r0, r0 + rc)
                    s = s_scr[slot, hh, rs, :] - fk_row
                    if masked:
                        ri = r0 + lax.broadcasted_iota(jnp.int32, (rc, tq), 0)
                        ci = lax.broadcasted_iota(jnp.int32, (rc, tq), 1)
                        s = jnp.where(ci <= ri, s, MASK_VALUE)
                    s_scr[slot, hh, rs, :] = s
                    c = s[:, 0:LANES]
                    for cb in range(1, nlb):
                        c = jnp.maximum(c, s[:, cb * LANES:(cb + 1) * LANES])
                    mx.append(c)
                row_max = jnp.max(jnp.concatenate(mx, axis=0), axis=1, keepdims=True)
                m_new = jnp.maximum(m_b, row_max)
                alpha = jnp.exp(m_b - m_new)
                for r0 in range(0, tq, rc):
                    rs = slice(r0, r0 + rc)
                    m_c = m_new[rs]
                    for cb in range(nlb):
                        cs = slice(cb * LANES, (cb + 1) * LANES)
                        p_scr[hh, rs, cs] = jnp.exp(s_scr[slot, hh, rs, cs] - m_c).astype(BF16)
                v_one = jnp.where(hmasks[hh], vj, jnp.ones_like(vj))
                out.append((m_new, alpha * acc + _dot(p_scr[hh], v_one)))
            return tuple(out)

        def step(j, carry, masked):
            scores(j, 0)
            return soft(j, 0, carry, masked)

        init = tuple((jnp.full((tq, LANES), MASK_VALUE, F32), jnp.zeros((tq, LANES), F32)) for _ in range(hpg))
        carry = lax.fori_loop(0, i, lambda j, cr: step(j, cr, False), init)
        carry = step(i, carry, True)
        fq = fq_ref[...]
        o_all = jnp.zeros((tq, LANES), F32)
        lse_all = jnp.zeros((tq, LANES), F32)
        for hh in range(hpg):
            m_b, acc = carry[hh]
            l = acc[:, sum_lane[hh]:sum_lane[hh] + 1]
            o_all = jnp.where(hmasks[hh], acc * (1.0 / l), o_all)
            lse_all = jnp.where(lane == hh, m_b[:, 0:1] + jnp.log(l) + fq[:, hh:hh + 1], lse_all)
        o_ref[...] = o_all.astype(BF16)
        o32_ref[...] = o_all
        lse_ref[...] = lse_all

    return _pcall(
        body, name="flash_fwd", grid=(G, nq),
        in_specs=[pl.BlockSpec((tq, LANES), lambda g, i: (i, g)),
                  pl.BlockSpec((nq, tq, LANES), lambda g, i: (0, 0, g)),
                  pl.BlockSpec((nq, tq, LANES), lambda g, i: (0, 0, g)),
                  pl.BlockSpec((None, tq, LANES), lambda g, i: (g, i, 0)),
                  pl.BlockSpec((None, nq, SUBLANES, tq), lambda g, i: (g, 0, 0, 0))],
        out_specs=[pl.BlockSpec((tq, LANES), lambda g, i: (i, g)),
                   pl.BlockSpec((tq, LANES), lambda g, i: (i, g)),
                   pl.BlockSpec((None, tq, LANES), lambda g, i: (g, i, 0))],
        out_shape=[jax.ShapeDtypeStruct((T, D), BF16), jax.ShapeDtypeStruct((T, D), F32),
                   jax.ShapeDtypeStruct((G, T, LANES), F32)],
        scratch_shapes=[pltpu.VMEM((1, hpg, tq, tq), F32), pltpu.VMEM((hpg, tq, tq), BF16)],
        compiler_params=_params(("arbitrary", "arbitrary")),
    )(q, k3, v3, fq_aux, fk_rows)


def _flash_bwd(q, k, v, do, fq_aux, lse_aux, dl_aux, fk_rows, *, dh, tq, s_mul, dq_mul):
    T, D = q.shape
    G = D // LANES
    hpg = LANES // dh
    nq = T // tq
    k3 = k.reshape(nq, tq, D)
    v3 = v.reshape(nq, tq, D)
    rc = min(FLASH_ROWS, tq)

    def body(q_ref, k_ref, v_ref, do_ref, fq_ref, lse_ref, dl_ref, fk_ref,
             dq_ref, dk_ref, dv_ref, dfq_ref, dfk_ref, dk_acc, dv_acc, s_scr, dp_scr, p_scr, ds_scr):
        i = pl.program_id(1)

        @pl.when(i == 0)
        def _():
            dk_acc[...] = jnp.zeros_like(dk_acc)
            dv_acc[...] = jnp.zeros_like(dv_acc)
            dfk_ref[...] = jnp.zeros_like(dfk_ref)

        lane = lax.broadcasted_iota(jnp.int32, (1, LANES), 1)
        q2 = q_ref[...]
        do2 = do_ref[...]
        cq = fq_ref[...] - lse_ref[...]
        dl = dl_ref[...]
        hmasks = [(lane >= hh * dh) & (lane < (hh + 1) * dh) for hh in range(hpg)]
        qms = [jnp.where(hm, q2, jnp.zeros_like(q2)) for hm in hmasks]
        doms = [jnp.where(hm, do2, jnp.zeros_like(do2)) for hm in hmasks]
        c_bs = [jnp.broadcast_to(cq[:, hh:hh + 1], (tq, LANES)) for hh in range(hpg)]
        dl_bs = [jnp.broadcast_to(dl[:, hh:hh + 1], (tq, LANES)) for hh in range(hpg)]
        nlb = tq // LANES

        def step(j, carry, masked):
            kj = k_ref[j]
            vj = v_ref[j]
            for hh in range(hpg):
                s = _dot_nt(qms[hh], kj)
                s_scr[hh] = s if s_mul == 1.0 else s * s_mul
                dp_scr[hh] = _dot_nt(doms[hh], vj)
            out = []
            for hh in range(hpg):
                dq_acc, rs_p = carry[hh]
                fk_row = fk_ref[j, hh:hh + 1, :]
                rsums = []
                csums = [jnp.zeros((SUBLANES, LANES), F32) for _ in range(nlb)]
                for r0 in range(0, tq, rc):
                    rs = slice(r0, r0 + rc)
                    c_c = c_bs[hh][rs]
                    dl_c = dl_bs[hh][rs]
                    tot = None
                    for cb in range(nlb):
                        cs = slice(cb * LANES, (cb + 1) * LANES)
                        e = (s_scr[hh, rs, cs] - fk_row[:, cs]) + c_c
                        if masked:
                            ri = r0 + lax.broadcasted_iota(jnp.int32, (rc, LANES), 0)
                            ci = cb * LANES + lax.broadcasted_iota(jnp.int32, (rc, LANES), 1)
                            e = jnp.where(ci <= ri, e, MASK_VALUE)
                        p = jnp.exp(e)
                        ds = p * (dp_scr[hh, rs, cs] - dl_c)
                        p_scr[hh, rs, cs] = p.astype(BF16)
                        ds_scr[hh, rs, cs] = ds.astype(BF16)
                        tot = ds if tot is None else tot + ds
                        csums[cb] = csums[cb] + ds.reshape(rc // SUBLANES, SUBLANES, LANES).sum(axis=0)
                    rsums.append(tot)
                col = jnp.concatenate([jnp.sum(c, axis=0, keepdims=True) for c in csums], axis=1)
                dfk_ref[j, hh:hh + 1, :] += -col
                out.append((dq_acc + _dot(ds_scr[hh], kj), rs_p + jnp.concatenate(rsums, axis=0)))
            dk_new = _dot_tn(ds_scr[0], qms[0])
            dv_new = _dot_tn(p_scr[0], doms[0])
            for hh in range(1, hpg):
                dk_new = dk_new + _dot_tn(ds_scr[hh], qms[hh])
                dv_new = dv_new + _dot_tn(p_scr[hh], doms[hh])
            dk_acc[j] += dk_new
            dv_acc[j] += dv_new
            return tuple(out)

        init = tuple((jnp.zeros((tq, LANES), F32), jnp.zeros((tq, LANES), F32)) for _ in range(hpg))
        carry = lax.fori_loop(0, i, lambda j, cr: step(j, cr, False), init)
        carry = step(i, carry, True)
        dq_all = jnp.zeros((tq, LANES), F32)
        dfq_all = jnp.zeros((tq, LANES), F32)
        for hh in range(hpg):
            dq_h, rs_p = carry[hh]
            dq_all = jnp.where(hmasks[hh], dq_h, dq_all)
            dfq_all = jnp.where(lane == hh, jnp.sum(rs_p, axis=1, keepdims=True), dfq_all)
        dq_ref[...] = (dq_all * dq_mul).astype(BF16)
        dfq_ref[...] = dfq_all

        @pl.when(i == nq - 1)
        def _():
            dkv = dk_acc[...]
            if s_mul != 1.0:
                dkv = dkv * s_mul
            dk_ref[...] = dkv.astype(BF16)
            dv_ref[...] = dv_acc[...].astype(BF16)

    blk = pl.BlockSpec((tq, LANES), lambda g, i: (i, g))
    res = pl.BlockSpec((nq, tq, LANES), lambda g, i: (0, 0, g))
    aux = pl.BlockSpec((None, tq, LANES), lambda g, i: (g, i, 0))
    rows = pl.BlockSpec((None, nq, SUBLANES, tq), lambda g, i: (g, 0, 0, 0))
    dq, dk3, dv3, dfq, dfk = _pcall(
        body, name="flash_bwd", grid=(G, nq),
        in_specs=[blk, res, res, blk, aux, aux, aux, rows],
        out_specs=[blk, res, res, aux, rows],
        out_shape=[jax.ShapeDtypeStruct((T, D), BF16), jax.ShapeDtypeStruct((nq, tq, D), BF16),
                   jax.ShapeDtypeStruct((nq, tq, D), BF16), jax.ShapeDtypeStruct((G, T, LANES), F32),
                   jax.ShapeDtypeStruct((G, nq, SUBLANES, tq), F32)],
        scratch_shapes=[pltpu.VMEM((nq, tq, LANES), F32), pltpu.VMEM((nq, tq, LANES), F32),
                        pltpu.VMEM((hpg, tq, tq), F32), pltpu.VMEM((hpg, tq, tq), F32),
                        pltpu.VMEM((hpg, tq, tq), BF16), pltpu.VMEM((hpg, tq, tq), BF16)],
        compiler_params=_params(("arbitrary", "arbitrary")),
    )(q, k3, v3, do, fq_aux, lse_aux, dl_aux, fk_rows)
    return dq, dk3.reshape(T, D), dv3.reshape(T, D), dfq, dfk


def _attn_out_fwd(o, x, wo, g_post, tm):
    T, D = x.shape

    def body(o_ref, x_ref, w_ref, g_ref, m_ref, xo_ref):
        m = _dot(o_ref[...], w_ref[...])
        m_ref[...] = m
        xo_ref[...] = x_ref[...] + _rms(m, g_ref[...])

    return _pcall(
        body, name="attn_out_fwd", grid=(T // tm,),
        in_specs=[_rows(tm, D), _rows(tm, D), _full((D, D)), _full((1, D))],
        out_specs=[_rows(tm, D), _rows(tm, D)],
        out_shape=[jax.ShapeDtypeStruct((T, D), F32), jax.ShapeDtypeStruct((T, D), F32)],
        compiler_params=_params(("arbitrary",)),
    )(o, x, wo, g_post)


def _attn_out_bwd(dxo, m, o, wo, g_post, head_ind, tm):
    T, D = m.shape

    def body(dxo_ref, m_ref, o_ref, w_ref, g_ref, ind_ref, dm_ref, do_ref, dl_ref, sums_ref):
        i = pl.program_id(0)
        dm, dgpost = _rms_bwd(m_ref[...], g_ref[...], dxo_ref[...])
        dmb = dm.astype(BF16)
        dm_ref[...] = dmb
        dob = _dot_nt(dmb, w_ref[...]).astype(BF16)
        do_ref[...] = dob
        dl_ref[...] = jnp.dot(dob.astype(F32) * o_ref[...], ind_ref[...], precision=lax.Precision.HIGHEST,
                              preferred_element_type=F32)

        @pl.when(i == 0)
        def _():
            sums_ref[...] = jnp.zeros_like(sums_ref)
        sums_ref[0:1, :] += dgpost

    return _pcall(
        body, name="attn_out_bwd", grid=(T // tm,),
        in_specs=[_rows(tm, D), _rows(tm, D), _rows(tm, D), _full((D, D)), _full((1, D)), _full((D, LANES))],
        out_specs=[_rows(tm, D), _rows(tm, D), _rows(tm, LANES), _acc((SUBLANES, D))],
        out_shape=[jax.ShapeDtypeStruct((T, D), BF16), jax.ShapeDtypeStruct((T, D), BF16),
                   jax.ShapeDtypeStruct((T, LANES), F32), jax.ShapeDtypeStruct((SUBLANES, D), F32)],
        compiler_params=_params(("arbitrary",)),
    )(dxo, m, o, wo, g_post, head_ind)


def _attn_in_bwd(dxo, x, g_pre, dq, dk, dv, dlf, lf, wqkv, wf, tm, n_heads):
    T, D = x.shape

    def body(dxo_ref, x_ref, g_ref, dq_ref, dk_ref, dv_ref, dlf_ref, lf_ref, w_ref, wf_ref,
             dxi_ref, h_ref, df_ref, sums_ref, dbf_ref):
        i = pl.program_id(0)
        xv = x_ref[...]
        h_ref[...] = _rms(xv, g_ref[...]).astype(BF16)
        lane = lax.broadcasted_iota(jnp.int32, (1, LANES), 1)
        df = jnp.where(lane < n_heads, dlf_ref[...] * (1.0 - jnp.exp(lf_ref[...])), 0.0)
        dfb = df.astype(BF16)
        df_ref[...] = dfb
        dh = (_dot_nt(dq_ref[...], w_ref[:, 0:D]) + _dot_nt(dk_ref[...], w_ref[:, D:2 * D])
              + _dot_nt(dv_ref[...], w_ref[:, 2 * D:3 * D]) + _dot_nt(dfb, wf_ref[...]))
        dxi, dgpre = _rms_bwd(xv, g_ref[...], dh)
        dxi_ref[...] = dxo_ref[...] + dxi

        @pl.when(i == 0)
        def _():
            sums_ref[...] = jnp.zeros_like(sums_ref)
            dbf_ref[...] = jnp.zeros_like(dbf_ref)
        sums_ref[0:1, :] += dgpre
        dbf_ref[...] += jnp.sum(df, axis=0, keepdims=True)

    return _pcall(
        body, name="attn_in_bwd", grid=(T // tm,),
        in_specs=[_rows(tm, D), _rows(tm, D), _full((1, D)), _rows(tm, D), _rows(tm, D), _rows(tm, D),
                  _rows(tm, LANES), _rows(tm, LANES), _full((D, 3 * D)), _full((D, LANES))],
        out_specs=[_rows(tm, D), _rows(tm, D), _rows(tm, LANES), _acc((SUBLANES, D)), _acc((1, LANES))],
        out_shape=[jax.ShapeDtypeStruct((T, D), F32), jax.ShapeDtypeStruct((T, D), BF16),
                   jax.ShapeDtypeStruct((T, LANES), BF16), jax.ShapeDtypeStruct((SUBLANES, D), F32),
                   jax.ShapeDtypeStruct((1, LANES), F32)],
        compiler_params=_params(("arbitrary",)),
    )(dxo, x, g_pre, dq, dk, dv, dlf, lf, wqkv, wf)


def _loss_head(y, target, tm):
    T, D = y.shape

    def body(y_ref, t_ref, dy_ref, loss_ref):
        i = pl.program_id(0)
        err = y_ref[...] - t_ref[...]
        dy_ref[...] = err * (1.0 / D)
        part = 0.5 * jnp.sum(jnp.mean(err * err, axis=-1, keepdims=True), axis=0, keepdims=True)

        @pl.when(i == 0)
        def _():
            loss_ref[...] = jnp.zeros_like(loss_ref)
        loss_ref[...] += part

    return _pcall(
        body, name="loss_head", grid=(T // tm,),
        in_specs=[_rows(tm, D), _rows(tm, D)],
        out_specs=[_rows(tm, D), _acc((SUBLANES, LANES))],
        out_shape=[jax.ShapeDtypeStruct((T, D), F32), jax.ShapeDtypeStruct((SUBLANES, LANES), F32)],
        compiler_params=_params(("arbitrary",)),
    )(y, target)


def _adamw(recvs, w, m, v, tr, name):
    L, R, C = w.shape
    assert len(recvs) == L
    c1 = 1.0 - ADAM_B1 ** ADAM_STEP
    c2 = 1.0 - ADAM_B2 ** ADAM_STEP

    def body(*refs):
        r_refs = refs[:L]
        w_ref, m_ref, v_ref, g_ref, d_ref, nm_ref, nv_ref = refs[L:]
        layer = pl.program_id(0)
        g = None
        for k in range(L):
            gk = r_refs[k][0, :, 0:C].astype(F32)
            for s in range(1, N_DEV):
                gk = gk + r_refs[k][s, :, 0:C].astype(F32)
            g = gk if g is None else jnp.where(layer == k, gk, g)
        nm = ADAM_B1 * m_ref[...] + (1.0 - ADAM_B1) * g
        nv = ADAM_B2 * v_ref[...] + (1.0 - ADAM_B2) * jnp.square(g)
        m_hat = nm / c1
        v_hat = nv / c2
        g_ref[...] = g
        d_ref[...] = -ADAM_LR * (m_hat / (jnp.sqrt(v_hat) + ADAM_EPS) + ADAM_WD * w_ref[...])
        nm_ref[...] = nm
        nv_ref[...] = nv

    def recv_spec(k):
        return pl.BlockSpec((N_DEV, tr, recvs[k].shape[-1]), lambda l, i: (0, jnp.where(l == k, i, 0), 0))

    blk = pl.BlockSpec((None, tr, C), lambda l, i: (l, i, 0))
    return _pcall(
        body, name=name, grid=(L, R // tr),
        in_specs=[recv_spec(k) for k in range(L)] + [blk] * 3,
        out_specs=[blk] * 4,
        out_shape=[jax.ShapeDtypeStruct((L, R, C), F32)] * 4,
        compiler_params=_params(("arbitrary", "arbitrary")),
    )(*recvs, w, m, v)


def _row_block(rows, cols):
    cap = max(SUBLANES, (256 * 1024) // max(cols, 1))
    best = None
    for t in range(SUBLANES, rows + 1, SUBLANES):
        if rows % t == 0 and t <= cap:
            best = t
    return rows if best is None else best


def kernel(x, g_mix_pre, g_mix_post, g_ffn_pre, g_ffn_post, conv_pw1_w, conv_pw1_b, conv_dw_w, conv_dw_b, conv_ln_g, conv_ln_b, conv_pw2_w, conv_pw2_b, attn_w_in, attn_b_f, attn_w_o, mlp_w_up, mlp_w_down, loss_target, m_g_mix_pre, m_g_mix_post, m_g_ffn_pre, m_g_ffn_post, m_conv_pw1_w, m_conv_pw1_b, m_conv_dw_w, m_conv_dw_b, m_conv_ln_g, m_conv_ln_b, m_conv_pw2_w, m_conv_pw2_b, m_attn_w_in, m_attn_b_f, m_attn_w_o, m_mlp_w_up, m_mlp_w_down, v_g_mix_pre, v_g_mix_post, v_g_ffn_pre, v_g_ffn_post, v_conv_pw1_w, v_conv_pw1_b, v_conv_dw_w, v_conv_dw_b, v_conv_ln_g, v_conv_ln_b, v_conv_pw2_w, v_conv_pw2_b, v_attn_w_in, v_attn_b_f, v_attn_w_o, v_mlp_w_up, v_mlp_w_down):
    _, T, D = x.shape
    H = attn_b_f.shape[-1]
    dh = D // H
    width = conv_dw_w.shape[1]
    cin = attn_w_in.shape[-1]
    fs = mlp_w_up.shape[-1]
    G = D // LANES
    hpg = LANES // dh
    assert T % 4 == 0 and D % LANES == 0 and LANES % dh == 0 and width <= CONV_HALO and H <= LANES

    tm = min(512, T // 4)
    tmb = min(256, T // 4)
    tq = tm
    tmc = min(256, T // 4)
    lc = min(256, D)
    tkw = min(2048, T // 4)
    tb = min(256, T // 4)
    nq = T // tq

    scale = float(dh) ** -0.5
    mant, _ = math.frexp(scale)
    q_mul = scale if mant == 0.5 else 1.0
    s_mul = 1.0 if mant == 0.5 else scale

    x2 = x.reshape(T, D)
    tgt = loss_target.reshape(T, D)

    w_srcs = [conv_pw1_w, conv_dw_w, conv_pw2_w, mlp_w_up, mlp_w_down, attn_w_in, attn_w_o]
    w_items = [(0, 0, "whole"), (1, 0, "whole"), (2, 0, "whole"), (3, 0, "whole"), (4, 0, "whole"),
               (5, 0, "whole"), (6, 0, "whole"), (3, 1, "whole"), (4, 1, "whole")]
    cin_w = -(-cin // LANES) * LANES
    w_lands = _place_own(
        w_srcs, w_items,
        [((N_DEV, D, cin_w) if si == 5 else (N_DEV,) + w_srcs[si].shape[1:], F32 if si == 1 else BF16)
         for si, _, _ in w_items],
        "stage_weights", cast=True)
    me_arr = _dev_index(*_mesh_pos()).astype(jnp.int32).reshape(1)
    w_groups = [[0, 1, 2], [3, 4], [5, 6], [7, 8]]
    g_sems, _, w_lands, g_token = _push_start(
        [], w_lands, [[(a, None, "own") for a in grp] for grp in w_groups], "gather_start")

    def gather_wait(gi, after):
        grp = w_groups[gi]
        return _push_wait(g_sems[gi], [], [w_lands[a] for a in grp], [(k, None, "own") for k in range(len(grp))],
                          after, "gather_wait%d" % gi)

    w1g, dwg, w2g = gather_wait(0, g_token)
    w2 = w2g.reshape(D, D)
    dw_full = jnp.transpose(dwg, (1, 0, 2)).reshape(width, D)
    w32 = jnp.pad(dw_full, ((0, 32 - width), (0, 0)))
    bf = jnp.pad(attn_b_f, ((0, 0), (0, LANES - H)))

    row = lambda a, i: a[i:i + 1]

    a0, u0 = _conv_in_fwd(x2, row(g_mix_pre, 0), w1g, conv_pw1_b, tm)
    y0 = _dwconv_fwd(u0, w32, conv_dw_b, tmc, lc, width)
    m0, x_1 = _conv_out_fwd(y0, x2, conv_ln_g, conv_ln_b, w2, conv_pw2_b, row(g_mix_post, 0), tm)
    wu0, wd0 = gather_wait(1, x_1)
    up0, n0, x_2 = _mlp_fwd(x_1, row(g_ffn_pre, 0), wu0, wd0, row(g_ffn_post, 0), tm, "mlp0_fwd")

    wing, wog = gather_wait(2, x_2)
    wo = wog.reshape(D, D)
    win = jnp.transpose(wing[:, :, :cin], (1, 0, 2)).reshape(D, N_DEV * cin)
    wqkv = win[:, :3 * D]
    wf = jnp.pad(win[:, 3 * D:], ((0, 0), (0, LANES - H)))
    q, k, v, lf = _attn_in_fwd(x_2, row(g_mix_pre, 1), wqkv, wf, bf, tm, q_mul, H)
    fcum = _cumsum_rows(lf, jnp.zeros_like(lf), tb, False, "forget_cumsum")

    def to_aux(t):
        t = jnp.transpose(t[:, :H].reshape(T, G, hpg), (1, 0, 2))
        return jnp.pad(t, ((0, 0), (0, 0), (0, LANES - hpg)))

    fq_aux = to_aux(fcum)
    fk_rows = jnp.transpose(fcum[:, :H].T.reshape(G, hpg, nq, tq), (0, 2, 1, 3))
    fk_rows = jnp.pad(fk_rows, ((0, 0), (0, 0), (0, SUBLANES - hpg), (0, 0)))
    o, o32, lse_aux = _flash_fwd(q, k, v, fq_aux, fk_rows, dh=dh, tq=tq, s_mul=s_mul)
    m1, x_3 = _attn_out_fwd(o, x_2, wo, row(g_mix_post, 1), tm)
    wu1, wd1 = gather_wait(3, x_3)
    up1, n1, x_4 = _mlp_fwd(x_3, row(g_ffn_pre, 1), wu1, wd1, row(g_ffn_post, 1), tm, "mlp1_fwd")

    dx, loss_blk = _loss_head(x_4, tgt, tm)

    def mlp_back(dx, n_l, x_in, up_l, l, wu, wd):
        dxi, h, dm, dup, sums = _mlp_bwd(dx, n_l, x_in, up_l, row(g_ffn_pre, l), wu, wd, row(g_ffn_post, l),
                                         tmb, "mlp%d_bwd" % l)
        dwu = _mm_tn(h, dup, nj=N_DEV, a_cols=D, g_cols=fs, a_by_j=False, g_by_j=True, tk=tkw,
                     name="mlp%d_dwu" % l)
        dwd = _mm_tn(up_l, dm, nj=N_DEV, a_cols=fs, g_cols=D, a_by_j=True, g_by_j=False, tk=tkw,
                     name="mlp%d_dwd" % l, act=True)
        return dxi, dwu, dwd, sums

    def push_grads(srcs, modes, after, name):
        items = [(i, None, mode) for i, mode in enumerate(modes)]
        lands = _seed_lands(srcs, modes, me_arr, name + "_own")
        sems, srcs_t, lands_t, token = _push_start(srcs, lands, [items], name + "_start")
        return (sems[0], srcs_t, lands_t, items), lax.optimization_barrier((after, token))[0]

    def pull_grads(handle, after, name):
        sems, srcs_t, lands_t, items = handle
        return _push_wait(sems, srcs_t, lands_t, items, after, name + "_wait")

    rs = D // N_DEV
    dx, dwu1, dwd1, s_mlp1 = mlp_back(dx, n1, x_3, up1, 1, wu1, wd1)
    h_mlp1, dx = push_grads([dwu1, dwd1], ["slot", "slot"], dx, "grads_mlp1")

    head_ind = jnp.asarray((np.arange(D)[:, None] // dh == np.arange(LANES)[None, :]).astype(np.float32))
    dm1, do, delta, s_ao = _attn_out_bwd(dx, m1, o32, wo, row(g_mix_post, 1), head_ind, tm)
    dwo = _mm_tn(o, dm1, nj=1, a_cols=D, g_cols=D, a_by_j=False, g_by_j=False, tk=tkw, name="attn_dwo")
    dq, dk, dv, dfq, dfk = _flash_bwd(q, k, v, do, fq_aux, lse_aux, to_aux(delta), fk_rows,
                                      dh=dh, tq=tq, s_mul=s_mul, dq_mul=scale)
    df_k = jnp.pad(jnp.transpose(dfk[:, :, :hpg, :], (0, 2, 1, 3)).reshape(H, T).T, ((0, 0), (0, LANES - H)))
    df_q = jnp.pad(jnp.transpose(dfq[:, :, :hpg], (1, 0, 2)).reshape(T, H), ((0, 0), (0, LANES - H)))
    dlf = _cumsum_rows(df_q, df_k, tb, True, "forget_cumsum_bwd")
    dx, h_at, df, s_ai, dbf = _attn_in_bwd(dx, x_2, row(g_mix_pre, 1), dq, dk, dv, dlf, lf, wqkv, wf, tm, H)
    dwq = _mm_tn(h_at, dq, nj=1, a_cols=D, g_cols=D, a_by_j=False, g_by_j=False, tk=tkw, name="attn_dwq")
    dwk = _mm_tn(h_at, dk, nj=1, a_cols=D, g_cols=D, a_by_j=False, g_by_j=False, tk=tkw, name="attn_dwk")
    dwv = _mm_tn(h_at, dv, nj=1, a_cols=D, g_cols=D, a_by_j=False, g_by_j=False, tk=tkw, name="attn_dwv")
    dwf = _mm_tn(h_at, df, nj=1, a_cols=D, g_cols=LANES, a_by_j=False, g_by_j=False, tk=tkw, name="attn_dwf")
    dwin = jnp.concatenate([dwq[0], dwk[0], dwv[0], dwf[0][:, :H]], axis=1)
    dwin = jnp.pad(jnp.transpose(dwin.reshape(D, N_DEV, cin), (1, 0, 2)), ((0, 0), (0, 0), (0, cin_w - cin)))
    h_attn, dx = push_grads([dwin, dwo.reshape(N_DEV, rs, D)], ["slot", "slot"], dx, "grads_attn")

    dx, dwu0, dwd0, s_mlp0 = mlp_back(dx, n0, x_1, up0, 0, wu0, wd0)
    h_mlp0, dx = push_grads([dwu0, dwd0], ["slot", "slot"], dx, "grads_mlp0")

    dy0, dm0, z0, s_co = _conv_out_bwd(dx, m0, y0, conv_ln_g, conv_ln_b, w2, row(g_mix_post, 0), tm)
    dw2 = _mm_tn(z0, dm0, nj=1, a_cols=D, g_cols=D, a_by_j=False, g_by_j=False, tk=tkw, name="conv_dw2")
    du0, ddw = _dwconv_bwd(dy0, u0, w32, tmc, lc, width)
    grad_x, h_cv, da0, s_ci, db1 = _conv_in_bwd(dx, du0, a0, x2, row(g_mix_pre, 0), w1g, tm)
    dw1 = _mm_tn(h_cv, da0, nj=N_DEV, a_cols=D, g_cols=(2 * D) // N_DEV, a_by_j=False, g_by_j=True, tk=tkw,
                 name="conv_dw1")
    ddw_s = jnp.transpose(ddw[:width].reshape(width, N_DEV, D // N_DEV), (1, 0, 2))

    def pad_row(a):
        return jnp.pad(a, ((0, 0), (0, D - a.shape[1])))

    def pack(gmp, gmq, gfp, gfq, b1, dwb, lng, lnb, b2, bfv, last):
        return jnp.concatenate([gmp, gmq, gfp, gfq, b1.reshape(2, D), dwb, lng, lnb, b2, pad_row(bfv), last],
                               axis=0)

    zero_row = jnp.zeros((1, D), F32)
    small_g = pack(
        jnp.concatenate([row(s_ci, 0), row(s_ai, 0)], axis=0),
        jnp.concatenate([row(s_co, 0), row(s_ao, 0)], axis=0),
        jnp.concatenate([row(s_mlp0, 1), row(s_mlp1, 1)], axis=0),
        jnp.concatenate([row(s_mlp0, 0), row(s_mlp1, 0)], axis=0),
        db1, row(s_co, 4), row(s_co, 1), row(s_co, 2), row(s_co, 3), dbf[:, :H],
        pad_row(loss_blk[0:1, 0:1]))
    h_conv, grad_x = push_grads([dw1, ddw_s, dw2.reshape(N_DEV, rs, D), small_g], ["slot", "slot", "slot", "whole"],
                                grad_x, "grads_conv")

    def opt(recvs, w, m, v, name):
        shp = w.shape
        L, C = shp[0], shp[-1]
        R = int(np.prod(shp[1:-1]))
        outs = _adamw([r.reshape(N_DEV, R, r.shape[-1]) for r in recvs], w.reshape(L, R, C), m.reshape(L, R, C),
                      v.reshape(L, R, C), _row_block(R, C), name)
        return [t.reshape(shp) for t in outs]

    big = {}
    r_wu1, r_wd1 = pull_grads(h_mlp1, grad_x, "grads_mlp1")
    r_win, r_wo = pull_grads(h_attn, r_wd1, "grads_attn")
    big["attn_w_in"] = opt([r_win], attn_w_in, m_attn_w_in, v_attn_w_in, "adamw_win")
    big["attn_w_o"] = opt([r_wo], attn_w_o, m_attn_w_o, v_attn_w_o, "adamw_wo")
    r_wu0, r_wd0 = pull_grads(h_mlp0, big["attn_w_o"][0], "grads_mlp0")
    big["mlp_w_up"] = opt([r_wu0, r_wu1], mlp_w_up, m_mlp_w_up, v_mlp_w_up, "adamw_wup")
    big["mlp_w_down"] = opt([r_wd0, r_wd1], mlp_w_down, m_mlp_w_down, v_mlp_w_down, "adamw_wdown")
    r_w1, r_dw, r_w2, r_small = pull_grads(h_conv, big["mlp_w_down"][0], "grads_conv")
    big["conv_pw1_w"] = opt([r_w1], conv_pw1_w, m_conv_pw1_w, v_conv_pw1_w, "adamw_pw1")
    big["conv_dw_w"] = opt([r_dw], conv_dw_w, m_conv_dw_w, v_conv_dw_w, "adamw_dw")
    big["conv_pw2_w"] = opt([r_w2], conv_pw2_w, m_conv_pw2_w, v_conv_pw2_w, "adamw_pw2")
    small_w = pack(g_mix_pre, g_mix_post, g_ffn_pre, g_ffn_post, conv_pw1_b, conv_dw_b, conv_ln_g, conv_ln_b,
                   conv_pw2_b, attn_b_f, zero_row)
    small_m = pack(m_g_mix_pre, m_g_mix_post, m_g_ffn_pre, m_g_ffn_post, m_conv_pw1_b, m_conv_dw_b, m_conv_ln_g,
                   m_conv_ln_b, m_conv_pw2_b, m_attn_b_f, zero_row)
    small_v = pack(v_g_mix_pre, v_g_mix_post, v_g_ffn_pre, v_g_ffn_post, v_conv_pw1_b, v_conv_dw_b, v_conv_ln_g,
                   v_conv_ln_b, v_conv_pw2_b, v_attn_b_f, zero_row)
    sm = _adamw([r_small], small_w[None], small_m[None], small_v[None], small_w.shape[0], "adamw_small")
    sm = [t[0] for t in sm]
    loss = sm[0][15, 0]

    def unpack(t):
        return {"g_mix_pre": t[0:2], "g_mix_post": t[2:4], "g_ffn_pre": t[4:6], "g_ffn_post": t[6:8],
                "conv_pw1_b": t[8:10].reshape(1, 2 * D), "conv_dw_b": t[10:11], "conv_ln_g": t[11:12],
                "conv_ln_b": t[12:13], "conv_pw2_b": t[13:14], "attn_b_f": t[14:15, :H]}

    small = [unpack(t) for t in sm]
    names = ["g_mix_pre", "g_mix_post", "g_ffn_pre", "g_ffn_post", "conv_pw1_w", "conv_pw1_b", "conv_dw_w",
             "conv_dw_b", "conv_ln_g", "conv_ln_b", "conv_pw2_w", "conv_pw2_b", "attn_w_in", "attn_b_f",
             "attn_w_o", "mlp_w_up", "mlp_w_down"]
    outs = [loss, grad_x.reshape(1, T, D)]
    for kind in range(4):
        for nme in names:
            outs.append(big[nme][kind] if nme in big else small[kind][nme])
    return tuple(outs)
```

```python
import functools
import math

import numpy as np
import jax
import jax.numpy as jnp
from jax import lax
from jax.experimental import pallas as pl
from jax.experimental.pallas import tpu as pltpu

F32 = jnp.float32
BF16 = jnp.bfloat16

RMS_EPS = 1e-6
LN_EPS = 1e-5
MASK_VALUE = -1e30
ADAM_LR = 0.001
ADAM_B1 = 0.9
ADAM_B2 = 0.999
ADAM_EPS = 1e-08
ADAM_WD = 0.01
ADAM_STEP = 10

N_DEV = 8
LANES = 128
SUBLANES = 8
CONV_HALO = 32
CONV_ROWS = 32
FLASH_ROWS = 32
VMEM_LIMIT = 56 * 1024 * 1024

_pcall = pl.pallas_call


def _params(sem=None):
    if sem is None:
        return pltpu.CompilerParams(vmem_limit_bytes=VMEM_LIMIT)
    return pltpu.CompilerParams(dimension_semantics=sem, vmem_limit_bytes=VMEM_LIMIT)


def _dot(a, b):
    return jnp.dot(a, b, preferred_element_type=F32)


def _dot_nt(a, b):
    return lax.dot_general(a, b, (((1,), (1,)), ((), ())), preferred_element_type=F32)


def _dot_tn(a, b):
    return lax.dot_general(a, b, (((0,), (0,)), ((), ())), preferred_element_type=F32)


def _full(shape):
    nd = len(shape)
    return pl.BlockSpec(shape, lambda *g: (0,) * nd, pipeline_mode=pl.Buffered(1))


def _acc(shape):
    nd = len(shape)
    return pl.BlockSpec(shape, lambda *g: (0,) * nd)


def _rows(tm, cols):
    return pl.BlockSpec((tm, cols), lambda i: (i, 0))


def _rms(x, g):
    r = lax.rsqrt(jnp.mean(x * x, axis=-1, keepdims=True) + RMS_EPS)
    return x * r * g


def _rms_bwd(x, g, dy):
    r = lax.rsqrt(jnp.mean(x * x, axis=-1, keepdims=True) + RMS_EPS)
    n = x * r
    dg = jnp.sum(dy * n, axis=0, keepdims=True)
    dn = dy * g
    dx = r * (dn - n * jnp.mean(dn * n, axis=-1, keepdims=True))
    return dx, dg


def _sigmoid(x):
    return 1.0 / (1.0 + jnp.exp(-x))


def _mesh_pos():
    return lax.axis_index("x"), lax.axis_index("y"), lax.axis_index("c")


def _dev_index(px, py, pc):
    return 4 * px + 2 * py + pc


_HBM = pl.BlockSpec(memory_space=pltpu.HBM)
_SEM = pl.BlockSpec(memory_space=pltpu.SEMAPHORE)
_EFFECT = pltpu.SideEffectType.DATAFLOW_SIDE_EFFECTING


def _peer(r, x, y, c):
    p = ((1 - x) if r & 4 else x, (1 - y) if r & 2 else y, (1 - c) if r & 1 else c)
    return p, _dev_index(*p)


def _src_ref(refs, item, me_id=None, to_id=None):
    si, sub, mode = item
    r = refs[si] if sub is None else refs[si].at[sub]
    if mode == "slot":
        return r.at[to_id]
    if mode == "own":
        return r.at[me_id]
    return r


def _place_own(srcs, items, land_shapes, name, cast=False):
    ns, n = len(srcs), len(items)

    def body(*refs):
        src = refs[:ns]
        land = refs[ns:ns + n]
        stage = refs[ns + n:ns + 2 * n] if cast else None
        sems = refs[-1]
        me_id = _dev_index(*_mesh_pos())
        cps = []
        for a, item in enumerate(items):
            s = _src_ref(src, item, to_id=me_id)
            if cast:
                if s.shape != stage[a].shape:
                    stage[a][...] = jnp.zeros_like(stage[a])
                    stage[a][:, 0:s.shape[-1]] = s[...].astype(stage[a].dtype)
                else:
                    stage[a][...] = s[...].astype(stage[a].dtype)
                s = stage[a]
            cp = pltpu.make_async_copy(s, land[a].at[me_id], sems.at[a])
            cp.start()
            cps.append(cp)
        for cp in cps:
            cp.wait()

    return _pcall(
        body, name=name,
        out_shape=[jax.ShapeDtypeStruct(tuple(s), d) for s, d in land_shapes],
        in_specs=[pl.BlockSpec(memory_space=pltpu.VMEM if cast else pl.ANY)] * ns,
        out_specs=[pl.BlockSpec(memory_space=pl.ANY)] * n,
        scratch_shapes=([pltpu.VMEM(tuple(s[1:]), d) for s, d in land_shapes] if cast else [])
        + [pltpu.SemaphoreType.DMA((n,))],
        compiler_params=pltpu.CompilerParams(vmem_limit_bytes=VMEM_LIMIT),
    )(*srcs)


def _seed_lands(srcs, modes, me, name):
    n = len(srcs)
    parts = [tuple(s.shape[1:]) if mode == "slot" else tuple(s.shape) for s, mode in zip(srcs, modes)]

    def body(me_ref, *refs):
        s = pl.program_id(0)
        for a in range(n):
            v = refs[a][...]
            refs[n + a][...] = jnp.where(s == me_ref[0], v, jnp.zeros_like(v))

    def in_spec(part, mode):
        nd = len(part)
        if mode == "slot":
            return pl.BlockSpec((None,) + part, lambda s, me_ref: (me_ref[0],) + (0,) * nd)
        return pl.BlockSpec(part, lambda s, me_ref: (0,) * nd)

    def out_spec(part):
        nd = len(part)
        return pl.BlockSpec((None,) + part, lambda s, me_ref: (s,) + (0,) * nd)

    return _pcall(
        body, name=name,
        grid_spec=pltpu.PrefetchScalarGridSpec(
            num_scalar_prefetch=1, grid=(N_DEV,),
            in_specs=[in_spec(p, m) for p, m in zip(parts, modes)],
            out_specs=[out_spec(p) for p in parts]),
        out_shape=[jax.ShapeDtypeStruct((N_DEV,) + p, s.dtype) for p, s in zip(parts, srcs)],
        compiler_params=_params(("arbitrary",)),
    )(me, *srcs)


def _push_start(srcs, lands, groups, name):
    ns, n = len(srcs), len(lands)
    ng = len(groups)
    assert sum(len(g) for g in groups) == n

    def body(*refs):
        ops = refs[:ns + n]
        land = refs[ns:ns + n]
        sems = refs[ns + n:ns + n + 2 * ng]
        token = refs[-1]
        x, y, c = _mesh_pos()
        me_id = _dev_index(x, y, c)
        a = 0
        for gi, grp in enumerate(groups):
            for k, item in enumerate(grp):
                for r in range(1, N_DEV):
                    p, pid = _peer(r, x, y, c)
                    pltpu.make_async_remote_copy(
                        src_ref=_src_ref(ops, item, me_id=me_id, to_id=pid), dst_ref=land[a].at[me_id],
                        send_sem=sems[2 * gi].at[k * (N_DEV - 1) + r - 1],
                        recv_sem=sems[2 * gi + 1].at[k * (N_DEV - 1) + r - 1],
                        device_id=p, device_id_type=pl.DeviceIdType.MESH).start()
                a += 1
        token[...] = jnp.zeros_like(token)

    sem_shapes = []
    for grp in groups:
        sem_shapes += [pltpu.SemaphoreType.DMA((len(grp) * (N_DEV - 1),))] * 2
    arrs = list(srcs) + list(lands)
    res = _pcall(
        body, name=name,
        out_shape=tuple(sem_shapes) + tuple(pltpu.HBM(a.shape, a.dtype) for a in arrs)
        + (jax.ShapeDtypeStruct((SUBLANES, LANES), F32),),
        in_specs=[_HBM] * (ns + n),
        out_specs=tuple([_SEM] * (2 * ng)) + tuple([_HBM] * (ns + n)) + (pl.BlockSpec(memory_space=pltpu.VMEM),),
        input_output_aliases={i: 2 * ng + i for i in range(ns + n)},
        compiler_params=pltpu.CompilerParams(has_side_effects=_EFFECT),
    )(*[pltpu.with_memory_space_constraint(a, pltpu.HBM) for a in arrs])
    sems = [(res[2 * gi], res[2 * gi + 1]) for gi in range(ng)]
    thru = res[2 * ng:2 * ng + ns + n]
    return sems, list(thru[:ns]), list(thru[ns:]), res[-1]


def _push_wait(sems, srcs, lands, group, after, name):
    ns, n = len(srcs), len(lands)
    assert len(group) == n

    def body(*refs):
        ops = refs[:ns + n]
        land = refs[ns:ns + n]
        send_sems, recv_sems = refs[ns + n], refs[ns + n + 1]
        x, y, c = _mesh_pos()
        me_id = _dev_index(x, y, c)
        for k, item in enumerate(group):
            for r in range(1, N_DEV):
                p, pid = _peer(r, x, y, c)
                cp = pltpu.make_async_remote_copy(
                    src_ref=_src_ref(ops, item, me_id=me_id, to_id=pid), dst_ref=land[k].at[pid],
                    send_sem=send_sems.at[k * (N_DEV - 1) + r - 1], recv_sem=recv_sems.at[k * (N_DEV - 1) + r - 1],
                    device_id=p, device_id_type=pl.DeviceIdType.MESH)
                cp.wait_send()
                cp.wait_recv()

    arrs = list(srcs) + list(lands)
    res = _pcall(
        body, name=name,
        out_shape=tuple(pltpu.HBM(a.shape, a.dtype) for a in arrs),
        in_specs=[_HBM] * (ns + n) + [_SEM, _SEM, pl.BlockSpec(memory_space=pl.ANY)],
        out_specs=tuple([_HBM] * (ns + n)),
        input_output_aliases={i: i for i in range(ns + n)},
        compiler_params=pltpu.CompilerParams(has_side_effects=_EFFECT),
    )(*arrs, sems[0], sems[1], after)
    return list(res[ns:])


def _conv_in_fwd(x, g_pre, w1g, b1, tm):
    T, D = x.shape
    ns, _, cs = w1g.shape
    half = ns // 2

    def body(x_ref, g_ref, w_ref, b_ref, a_ref, u_ref):
        h = _rms(x_ref[...], g_ref[...]).astype(BF16)
        parts = []
        for s in range(ns):
            a_s = _dot(h, w_ref[s]) + b_ref[:, s * cs:(s + 1) * cs]
            a_ref[:, s * cs:(s + 1) * cs] = a_s
            parts.append(a_s)
        for s in range(half):
            u_ref[:, s * cs:(s + 1) * cs] = parts[s] * _sigmoid(parts[s + half])

    return _pcall(
        body, name="conv_in_fwd", grid=(T // tm,),
        in_specs=[_rows(tm, D), _full((1, D)), _full(w1g.shape), _full((1, 2 * D))],
        out_specs=[_rows(tm, 2 * D), _rows(tm, D)],
        out_shape=[jax.ShapeDtypeStruct((T, 2 * D), F32), jax.ShapeDtypeStruct((T, D), F32)],
        compiler_params=_params(("arbitrary",)),
    )(x, g_pre, w1g, b1)


def _shifted_copies(ext_ref, sh_ref, tm):
    n = tm + CONV_HALO - SUBLANES
    for b in range(1, SUBLANES):
        sh_ref[b - 1, 0:n, :] = ext_ref[b:b + n, :]


def _shifted_rows(ext_ref, sh_ref, off, r0, ls):
    b = off % SUBLANES
    a8 = off - b + r0
    src = ext_ref if b == 0 else sh_ref.at[b - 1]
    return src[a8:a8 + CONV_ROWS, ls]


def _dwconv_fwd(u, w32, b, tm, lc, width):
    T, D = u.shape
    hb = tm // CONV_HALO

    def body(u_ref, halo_ref, w_ref, b_ref, y_ref, ext_ref, sh_ref):
        i = pl.program_id(0)
        ext_ref[0:CONV_HALO, :] = jnp.where(i > 0, halo_ref[...], 0.0)
        ext_ref[CONV_HALO:, :] = u_ref[...]
        _shifted_copies(ext_ref, sh_ref, tm)
        for r0 in range(0, tm, CONV_ROWS):
            for l0 in range(0, lc, LANES):
                ls = slice(l0, l0 + LANES)
                acc = jnp.zeros((CONV_ROWS, LANES), F32) + b_ref[:, ls]
                for j in range(width):
                    off = CONV_HALO - (width - 1) + j
                    acc = acc + w_ref[j:j + 1, ls] * _shifted_rows(ext_ref, sh_ref, off, r0, ls)
                y_ref[r0:r0 + CONV_ROWS, ls] = acc

    return _pcall(
        body, name="dwconv_fwd", grid=(T // tm, D // lc),
        in_specs=[pl.BlockSpec((tm, lc), lambda i, l: (i, l)),
                  pl.BlockSpec((CONV_HALO, lc), lambda i, l: (jnp.maximum(i * hb - 1, 0), l)),
                  pl.BlockSpec((32, lc), lambda i, l: (0, l)),
                  pl.BlockSpec((1, lc), lambda i, l: (0, l))],
        out_specs=pl.BlockSpec((tm, lc), lambda i, l: (i, l)),
        out_shape=jax.ShapeDtypeStruct((T, D), F32),
        scratch_shapes=[pltpu.VMEM((tm + CONV_HALO, lc), F32),
                        pltpu.VMEM((SUBLANES - 1, tm + CONV_HALO, lc), F32)],
        compiler_params=_params(("arbitrary", "arbitrary")),
    )(u, u, w32, b)


def _ln_parts(y, g, b):
    mu = jnp.mean(y, axis=-1, keepdims=True)
    yc = y - mu
    rstd = lax.rsqrt(jnp.mean(yc * yc, axis=-1, keepdims=True) + LN_EPS)
    yhat = yc * rstd
    return yhat, rstd, yhat * g + b


def _conv_out_fwd(y, x, ln_g, ln_b, w2, b2, g_post, tm):
    T, D = x.shape

    def body(y_ref, x_ref, lg_ref, lb_ref, w_ref, b_ref, g_ref, m_ref, xo_ref):
        _, _, yn = _ln_parts(y_ref[...], lg_ref[...], lb_ref[...])
        z = (yn * _sigmoid(yn)).astype(BF16)
        m = _dot(z, w_ref[...]) + b_ref[...]
        m_ref[...] = m
        xo_ref[...] = x_ref[...] + _rms(m, g_ref[...])

    return _pcall(
        body, name="conv_out_fwd", grid=(T // tm,),
        in_specs=[_rows(tm, D), _rows(tm, D), _full((1, D)), _full((1, D)), _full((D, D)), _full((1, D)),
                  _full((1, D))],
        out_specs=[_rows(tm, D), _rows(tm, D)],
        out_shape=[jax.ShapeDtypeStruct((T, D), F32), jax.ShapeDtypeStruct((T, D), F32)],
        compiler_params=_params(("arbitrary",)),
    )(y, x, ln_g, ln_b, w2, b2, g_post)


def _conv_out_bwd(dxo, m, y, ln_g, ln_b, w2, g_post, tm, dep):
    T, D = m.shape

    def body(dxo_ref, m_ref, y_ref, lg_ref, lb_ref, w_ref, g_ref, dep_ref, dy_ref, dm_ref, z_ref, sums_ref):
        i = pl.program_id(0)
        dm, dgpost = _rms_bwd(m_ref[...], g_ref[...], dxo_ref[...])
        dmb = dm.astype(BF16)
        dm_ref[...] = dmb
        yhat, rstd, yn = _ln_parts(y_ref[...], lg_ref[...], lb_ref[...])
        sg = _sigmoid(yn)
        z_ref[...] = (yn * sg).astype(BF16)
        dz = _dot_nt(dmb, w_ref[...])
        dyn = dz * (sg + yn * sg * (1.0 - sg))
        dyh = dyn * lg_ref[...]
        dy = rstd * (dyh - jnp.mean(dyh, axis=-1, keepdims=True)
                     - yhat * jnp.mean(dyh * yhat, axis=-1, keepdims=True))
        dy_ref[...] = dy

        @pl.when(i == 0)
        def _():
            sums_ref[...] = jnp.zeros_like(sums_ref)
        sums_ref[0:1, :] += dgpost
        sums_ref[1:2, :] += jnp.sum(dyn * yhat, axis=0, keepdims=True)
        sums_ref[2:3, :] += jnp.sum(dyn, axis=0, keepdims=True)
        sums_ref[3:4, :] += jnp.sum(dm, axis=0, keepdims=True)
        sums_ref[4:5, :] += jnp.sum(dy, axis=0, keepdims=True)

    return _pcall(
        body, name="conv_out_bwd", grid=(T // tm,),
        in_specs=[_rows(tm, D), _rows(tm, D), _rows(tm, D), _full((1, D)), _full((1, D)), _full((D, D)),
                  _full((1, D)), pl.BlockSpec(memory_space=pl.ANY)],
        out_specs=[_rows(tm, D), _rows(tm, D), _rows(tm, D), _acc((SUBLANES, D))],
        out_shape=[jax.ShapeDtypeStruct((T, D), F32), jax.ShapeDtypeStruct((T, D), BF16),
                   jax.ShapeDtypeStruct((T, D), BF16), jax.ShapeDtypeStruct((SUBLANES, D), F32)],
        compiler_params=_params(("arbitrary",)),
    )(dxo, m, y, ln_g, ln_b, w2, g_post, dep)


def _dwconv_bwd(dy, u, w32, tm, lc, width):
    T, D = u.shape
    hb = tm // CONV_HALO
    nt = T // tm
    last_halo = T // CONV_HALO - 1

    def body(dy_ref, dyn_ref, u_ref, up_ref, w_ref, du_ref, dw_ref, exty_ref, extu_ref, acc_ref, shy_ref, shu_ref):
        i = pl.program_id(1)
        exty_ref[0:tm, :] = dy_ref[...]
        exty_ref[tm:, :] = jnp.where(i < nt - 1, dyn_ref[...], 0.0)
        extu_ref[0:CONV_HALO, :] = jnp.where(i > 0, up_ref[...], 0.0)
        extu_ref[CONV_HALO:, :] = u_ref[...]
        _shifted_copies(exty_ref, shy_ref, tm)
        _shifted_copies(extu_ref, shu_ref, tm)

        @pl.when(i == 0)
        def _():
            acc_ref[...] = jnp.zeros_like(acc_ref)

        for r0 in range(0, tm, CONV_ROWS):
            for l0 in range(0, lc, LANES):
                ls = slice(l0, l0 + LANES)
                dyc = exty_ref[r0:r0 + CONV_ROWS, ls]
                du = jnp.zeros((CONV_ROWS, LANES), F32)
                for j in range(width):
                    du = du + w_ref[j:j + 1, ls] * _shifted_rows(exty_ref, shy_ref, (width - 1) - j, r0, ls)
                    prod = dyc * _shifted_rows(extu_ref, shu_ref, CONV_HALO - (width - 1) + j, r0, ls)
                    acc_ref[j, :, ls] += prod.reshape(CONV_ROWS // SUBLANES, SUBLANES, LANES).sum(axis=0)
                du_ref[r0:r0 + CONV_ROWS, ls] = du

        @pl.when(i == nt - 1)
        def _():
            for j in range(32):
                dw_ref[j:j + 1, :] = jnp.sum(acc_ref[j], axis=0, keepdims=True)

    return _pcall(
        body, name="dwconv_bwd", grid=(D // lc, nt),
        in_specs=[pl.BlockSpec((tm, lc), lambda l, i: (i, l)),
                  pl.BlockSpec((CONV_HALO, lc), lambda l, i: (jnp.minimum((i + 1) * hb, last_halo), l)),
                  pl.BlockSpec((tm, lc), lambda l, i: (i, l)),
                  pl.BlockSpec((CONV_HALO, lc), lambda l, i: (jnp.maximum(i * hb - 1, 0), l)),
                  pl.BlockSpec((32, lc), lambda l, i: (0, l))],
        out_specs=[pl.BlockSpec((tm, lc), lambda l, i: (i, l)),
                   pl.BlockSpec((32, lc), lambda l, i: (0, l))],
        out_shape=[jax.ShapeDtypeStruct((T, D), F32), jax.ShapeDtypeStruct((32, D), F32)],
        scratch_shapes=[pltpu.VMEM((tm + CONV_HALO, lc), F32), pltpu.VMEM((tm + CONV_HALO, lc), F32),
                        pltpu.VMEM((32, SUBLANES, lc), F32),
                        pltpu.VMEM((SUBLANES - 1, tm + CONV_HALO, lc), F32),
                        pltpu.VMEM((SUBLANES - 1, tm + CONV_HALO, lc), F32)],
        compiler_params=_params(("arbitrary", "arbitrary")),
    )(dy, dy, u, u, w32)


def _conv_in_bwd(dxo, du, a, x, g_pre, w1g, tm):
    T, D = x.shape
    ns, _, cs = w1g.shape
    half = ns // 2

    def body(dxo_ref, du_ref, a_ref, x_ref, g_ref, w_ref, dxi_ref, h_ref, da_ref, sums_ref, db_ref):
        i = pl.program_id(0)
        xv = x_ref[...]
        h_ref[...] = _rms(xv, g_ref[...]).astype(BF16)
        dh = jnp.zeros((tm, D), F32)
        dbs = [None] * ns
        for s in range(half):
            a_u = a_ref[:, s * cs:(s + 1) * cs]
            sg = _sigmoid(a_ref[:, (s + half) * cs:(s + half + 1) * cs])
            du_s = du_ref[:, s * cs:(s + 1) * cs]
            da_u = du_s * sg
            da_g = du_s * a_u * sg * (1.0 - sg)
            for s2, v in ((s, da_u), (s + half, da_g)):
                vb = v.astype(BF16)
                da_ref[:, s2 * cs:(s2 + 1) * cs] = vb
                dbs[s2] = jnp.sum(v, axis=0, keepdims=True)
                dh = dh + _dot_nt(vb, w_ref[s2])
        dxi, dgpre = _rms_bwd(xv, g_ref[...], dh)
        dxi_ref[...] = dxo_ref[...] + dxi

        @pl.when(i == 0)
        def _():
            sums_ref[...] = jnp.zeros_like(sums_ref)
            db_ref[...] = jnp.zeros_like(db_ref)
        sums_ref[0:1, :] += dgpre
        for s in range(ns):
            db_ref[:, s * cs:(s + 1) * cs] += dbs[s]

    return _pcall(
        body, name="conv_in_bwd", grid=(T // tm,),
        in_specs=[_rows(tm, D), _rows(tm, D), _rows(tm, 2 * D), _rows(tm, D), _full((1, D)),
                  _full(w1g.shape)],
        out_specs=[_rows(tm, D), _rows(tm, D), _rows(tm, 2 * D), _acc((SUBLANES, D)), _acc((1, 2 * D))],
        out_shape=[jax.ShapeDtypeStruct((T, D), F32), jax.ShapeDtypeStruct((T, D), BF16),
                   jax.ShapeDtypeStruct((T, 2 * D), BF16), jax.ShapeDtypeStruct((SUBLANES, D), F32),
                   jax.ShapeDtypeStruct((1, 2 * D), F32)],
        compiler_params=_params(("arbitrary",)),
    )(dxo, du, a, x, g_pre, w1g)


def _mlp_fwd(x, g_pre, wug, wdg, g_post, tm, name):
    T, D = x.shape
    ns, _, fs = wug.shape

    def body(x_ref, gp_ref, wu_ref, wd_ref, gq_ref, up_ref, m_ref, xo_ref):
        xv = x_ref[...]
        h = _rms(xv, gp_ref[...]).astype(BF16)
        acc = jnp.zeros((tm, D), F32)
        for s in range(ns):
            up = _dot(h, wu_ref[s]).astype(BF16)
            up_ref[:, s * fs:(s + 1) * fs] = up
            act = jnp.square(jnp.maximum(up.astype(F32), 0.0)).astype(BF16)
            acc = acc + _dot(act, wd_ref[s])
        m_ref[...] = acc
        xo_ref[...] = xv + _rms(acc, gq_ref[...])

    return _pcall(
        body, name=name, grid=(T // tm,),
        in_specs=[_rows(tm, D), _full((1, D)), _full(wug.shape), _full(wdg.shape), _full((1, D))],
        out_specs=[_rows(tm, ns * fs), _rows(tm, D), _rows(tm, D)],
        out_shape=[jax.ShapeDtypeStruct((T, ns * fs), BF16), jax.ShapeDtypeStruct((T, D), F32),
                   jax.ShapeDtypeStruct((T, D), F32)],
        compiler_params=_params(("arbitrary",)),
    )(x, g_pre, wug, wdg, g_post)


def _mlp_bwd(dxo, m, x, up, g_pre, wug, wdg, g_post, tm, name, dep):
    T, D = x.shape
    ns, _, fs = wug.shape

    def body(dxo_ref, m_ref, x_ref, up_ref, gp_ref, wu_ref, wd_ref, gq_ref, dep_ref,
             dxi_ref, h_ref, dm_ref, dup_ref, sums_ref):
        i = pl.program_id(0)
        dxo = dxo_ref[...]
        dm, dgpost = _rms_bwd(m_ref[...], gq_ref[...], dxo)
        dmb = dm.astype(BF16)
        dm_ref[...] = dmb
        xv = x_ref[...]
        h_ref[...] = _rms(xv, gp_ref[...]).astype(BF16)
        dh = jnp.zeros((tm, D), F32)
        for s in range(ns):
            dact = _dot_nt(dmb, wd_ref[s])
            up = up_ref[:, s * fs:(s + 1) * fs].astype(F32)
            dup = (dact * (2.0 * jnp.maximum(up, 0.0))).astype(BF16)
            dup_ref[:, s * fs:(s + 1) * fs] = dup
            dh = dh + _dot_nt(dup, wu_ref[s])
        dxi, dgpre = _rms_bwd(xv, gp_ref[...], dh)
        dxi_ref[...] = dxo + dxi

        @pl.when(i == 0)
        def _():
            sums_ref[...] = jnp.zeros_like(sums_ref)
        sums_ref[0:1, :] += dgpost
        sums_ref[1:2, :] += dgpre

    return _pcall(
        body, name=name, grid=(T // tm,),
        in_specs=[_rows(tm, D), _rows(tm, D), _rows(tm, D), _rows(tm, ns * fs), _full((1, D)),
                  _full(wug.shape), _full(wdg.shape), _full((1, D)), pl.BlockSpec(memory_space=pl.ANY)],
        out_specs=[_rows(tm, D), _rows(tm, D), _rows(tm, D), _rows(tm, ns * fs), _acc((SUBLANES, D))],
        out_shape=[jax.ShapeDtypeStruct((T, D), F32), jax.ShapeDtypeStruct((T, D), BF16),
                   jax.ShapeDtypeStruct((T, D), BF16), jax.ShapeDtypeStruct((T, ns * fs), BF16),
                   jax.ShapeDtypeStruct((SUBLANES, D), F32)],
        compiler_params=_params(("arbitrary",)),
    )(dxo, m, x, up, g_pre, wug, wdg, g_post, dep)


def _mm_tn(a, g, *, nj, a_cols, g_cols, a_by_j, g_by_j, tk, name, act=False):
    T = a.shape[0]
    nk = T // tk

    def body(a_ref, g_ref, o_ref, acc_ref):
        k = pl.program_id(1)
        av = a_ref[...]
        if act:
            av = jnp.square(jnp.maximum(av.astype(F32), 0.0)).astype(BF16)
        p = _dot_tn(av, g_ref[...])

        @pl.when(k == 0)
        def _():
            acc_ref[...] = p

        @pl.when(k > 0)
        def _():
            acc_ref[...] += p

        @pl.when(k == nk - 1)
        def _():
            o_ref[...] = acc_ref[...].astype(BF16)

    return _pcall(
        body, name=name, grid=(nj, nk),
        in_specs=[pl.BlockSpec((tk, a_cols), (lambda j, k: (k, j)) if a_by_j else (lambda j, k: (k, 0))),
                  pl.BlockSpec((tk, g_cols), (lambda j, k: (k, j)) if g_by_j else (lambda j, k: (k, 0)))],
        out_specs=pl.BlockSpec((None, a_cols, g_cols), lambda j, k: (j, 0, 0)),
        out_shape=jax.ShapeDtypeStruct((nj, a_cols, g_cols), BF16),
        scratch_shapes=[pltpu.VMEM((a_cols, g_cols), F32)],
        compiler_params=_params(("arbitrary", "arbitrary")),
    )(a, g)


def _attn_in_fwd(x, g_pre, wqkv, wf, bf, tm, q_mul, n_heads):
    T, D = x.shape

    def body(x_ref, g_ref, w_ref, wf_ref, bf_ref, q_ref, k_ref, v_ref, lf_ref):
        h = _rms(x_ref[...], g_ref[...]).astype(BF16)
        q = _dot(h, w_ref[:, 0:D])
        if q_mul != 1.0:
            q = q * q_mul
        q_ref[...] = q.astype(BF16)
        k_ref[...] = _dot(h, w_ref[:, D:2 * D]).astype(BF16)
        v_ref[...] = _dot(h, w_ref[:, 2 * D:3 * D]).astype(BF16)
        fl = _dot(h, wf_ref[...]) + bf_ref[...]
        lf = jnp.minimum(fl, 0.0) - jnp.log(1.0 + jnp.exp(-jnp.abs(fl)))
        lane = lax.broadcasted_iota(jnp.int32, (1, LANES), 1)
        lf_ref[...] = jnp.where(lane < n_heads, lf, 0.0)

    return _pcall(
        body, name="attn_in_fwd", grid=(T // tm,),
        in_specs=[_rows(tm, D), _full((1, D)), _full((D, 3 * D)), _full((D, LANES)), _full((1, LANES))],
        out_specs=[_rows(tm, D), _rows(tm, D), _rows(tm, D), _rows(tm, LANES)],
        out_shape=[jax.ShapeDtypeStruct((T, D), BF16)] * 3 + [jax.ShapeDtypeStruct((T, LANES), F32)],
        compiler_params=_params(("arbitrary",)),
    )(x, g_pre, wqkv, wf, bf)


def _cumsum_rows(v, v2, tb, reverse, name):
    T, C = v.shape
    nb = T // tb

    def body(v_ref, v2_ref, o_ref, carry_ref):
        i = pl.program_id(0)

        @pl.when(i == 0)
        def _():
            carry_ref[...] = jnp.zeros_like(carry_ref)
        r = lax.broadcasted_iota(jnp.int32, (tb, tb), 0)
        c = lax.broadcasted_iota(jnp.int32, (tb, tb), 1)
        tri = jnp.where((c >= r) if reverse else (c <= r), 1.0, 0.0).astype(F32)
        out = jnp.dot(tri, v_ref[...] + v2_ref[...], precision=lax.Precision.HIGHEST,
                      preferred_element_type=F32) + carry_ref[...]
        o_ref[...] = out
        carry_ref[...] = out[0:1, :] if reverse else out[tb - 1:tb, :]

    idx = (lambda i: (nb - 1 - i, 0)) if reverse else (lambda i: (i, 0))
    return _pcall(
        body, name=name, grid=(nb,),
        in_specs=[pl.BlockSpec((tb, C), idx), pl.BlockSpec((tb, C), idx)],
        out_specs=pl.BlockSpec((tb, C), idx),
        out_shape=jax.ShapeDtypeStruct((T, C), F32),
        scratch_shapes=[pltpu.VMEM((1, C), F32)],
        compiler_params=_params(("arbitrary",)),
    )(v, v2)


def _flash_fwd(q, k, v, fq_aux, fk_rows, *, dh, tq, s_mul):
    T, D = q.shape
    G = D // LANES
    hpg = LANES // dh
    nq = T // tq
    k3 = k.reshape(nq, tq, D)
    v3 = v.reshape(nq, tq, D)

    rc = min(FLASH_ROWS, tq)

    def body(q_ref, k_ref, v_ref, fq_ref, fk_ref, o_ref, o32_ref, lse_ref, s_scr, p_scr):
        i = pl.program_id(1)
        lane = lax.broadcasted_iota(jnp.int32, (1, LANES), 1)
        q2 = q_ref[...]
        hmasks = [(lane >= hh * dh) & (lane < (hh + 1) * dh) for hh in range(hpg)]
        qms = [jnp.where(hm, q2, jnp.zeros_like(q2)) for hm in hmasks]

        nlb = tq // LANES
        sum_lane = [((hh + 1) % hpg) * dh for hh in range(hpg)]

        def scores(j, slot):
            kj = k_ref[j]
            for hh in range(hpg):
                s = _dot_nt(qms[hh], kj)
                s_scr[slot, hh] = s if s_mul == 1.0 else s * s_mul

        def soft(j, slot, carry, masked):
            vj = v_ref[j]
            out = []
            for hh in range(hpg):
                m_b, acc = carry[hh]
                fk_row = fk_ref[j, hh:hh + 1, :]
                mx = []
                for r0 in range(0, tq, rc):
                    rs = slice(r0, r0 + rc)
                    s = s_scr[slot, hh, rs, :] - fk_row
                    if masked:
                        ri = r0 + lax.broadcasted_iota(jnp.int32, (rc, tq), 0)
                        ci = lax.broadcasted_iota(jnp.int32, (rc, tq), 1)
                        s = jnp.where(ci <= ri, s, MASK_VALUE)
                    s_scr[slot, hh, rs, :] = s
                    c = s[:, 0:LANES]
                    for cb in range(1, nlb):
                        c = jnp.maximum(c, s[:, cb * LANES:(cb + 1) * LANES])
                    mx.append(c)
                row_max = jnp.max(jnp.concatenate(mx, axis=0), axis=1, keepdims=True)
                m_new = jnp.maximum(m_b, row_max)
                alpha = jnp.exp(m_b - m_new)
                for r0 in range(0, tq, rc):
                    rs = slice(r0, r0 + rc)
                    m_c = m_new[rs]
                    for cb in range(nlb):
                        cs = slice(cb * LANES, (cb + 1) * LANES)
                        p_scr[hh, rs, cs] = jnp.exp(s_scr[slot, hh, rs, cs] - m_c).astype(BF16)
                v_one = jnp.where(hmasks[hh], vj, jnp.ones_like(vj))
                out.append((m_new, alpha * acc + _dot(p_scr[hh], v_one)))
            return tuple(out)

        def step(j, carry, masked):
            scores(j, 0)
            return soft(j, 0, carry, masked)

        init = tuple((jnp.full((tq, LANES), MASK_VALUE, F32), jnp.zeros((tq, LANES), F32)) for _ in range(hpg))
        carry = lax.fori_loop(0, i, lambda j, cr: step(j, cr, False), init)
        carry = step(i, carry, True)
        fq = fq_ref[...]
        o_all = jnp.zeros((tq, LANES), F32)
        lse_all = jnp.zeros((tq, LANES), F32)
        for hh in range(hpg):
            m_b, acc = carry[hh]
            l = acc[:, sum_lane[hh]:sum_lane[hh] + 1]
            o_all = jnp.where(hmasks[hh], acc * (1.0 / l), o_all)
            lse_all = jnp.where(lane == hh, m_b[:, 0:1] + jnp.log(l) + fq[:, hh:hh + 1], lse_all)
        o_ref[...] = o_all.astype(BF16)
        o32_ref[...] = o_all
        lse_ref[...] = lse_all

    return _pcall(
        body, name="flash_fwd", grid=(G, nq),
        in_specs=[pl.BlockSpec((tq, LANES), lambda g, i: (i, g)),
                  pl.BlockSpec((nq, tq, LANES), lambda g, i: (0, 0, g)),
                  pl.BlockSpec((nq, tq, LANES), lambda g, i: (0, 0, g)),
                  pl.BlockSpec((None, tq, LANES), lambda g, i: (g, i, 0)),
                  pl.BlockSpec((None, nq, SUBLANES, tq), lambda g, i: (g, 0, 0, 0))],
        out_specs=[pl.BlockSpec((tq, LANES), lambda g, i: (i, g)),
                   pl.BlockSpec((tq, LANES), lambda g, i: (i, g)),
                   pl.BlockSpec((None, tq, LANES), lambda g, i: (g, i, 0))],
        out_shape=[jax.ShapeDtypeStruct((T, D), BF16), jax.ShapeDtypeStruct((T, D), F32),
                   jax.ShapeDtypeStruct((G, T, LANES), F32)],
        scratch_shapes=[pltpu.VMEM((1, hpg, tq, tq), F32), pltpu.VMEM((hpg, tq, tq), BF16)],
        compiler_params=_params(("arbitrary", "arbitrary")),
    )(q, k3, v3, fq_aux, fk_rows)


def _flash_bwd(q, k, v, do, fq_aux, lse_aux, dl_aux, fk_rows, *, dh, tq, s_mul, dq_mul):
    T, D = q.shape
    G = D // LANES
    hpg = LANES // dh
    nq = T // tq
    k3 = k.reshape(nq, tq, D)
    v3 = v.reshape(nq, tq, D)
    rc = min(FLASH_ROWS, tq)

    def body(q_ref, k_ref, v_ref, do_ref, fq_ref, lse_ref, dl_ref, fk_ref,
             dq_ref, dk_ref, dv_ref, dfq_ref, dfk_ref, dk_acc, dv_acc, s_scr, dp_scr, p_scr, ds_scr):
        i = pl.program_id(1)

        @pl.when(i == 0)
        def _():
            dk_acc[...] = jnp.zeros_like(dk_acc)
            dv_acc[...] = jnp.zeros_like(dv_acc)
            dfk_ref[...] = jnp.zeros_like(dfk_ref)

        lane = lax.broadcasted_iota(jnp.int32, (1, LANES), 1)
        q2 = q_ref[...]
        do2 = do_ref[...]
        cq = fq_ref[...] - lse_ref[...]
        dl = dl_ref[...]
        hmasks = [(lane >= hh * dh) & (lane < (hh + 1) * dh) for hh in range(hpg)]
        qms = [jnp.where(hm, q2, jnp.zeros_like(q2)) for hm in hmasks]
        doms = [jnp.where(hm, do2, jnp.zeros_like(do2)) for hm in hmasks]
        c_bs = [jnp.broadcast_to(cq[:, hh:hh + 1], (tq, LANES)) for hh in range(hpg)]
        dl_bs = [jnp.broadcast_to(dl[:, hh:hh + 1], (tq, LANES)) for hh in range(hpg)]
        nlb = tq // LANES

        def step(j, carry, masked):
            kj = k_ref[j]
            vj = v_ref[j]
            for hh in range(hpg):
                s = _dot_nt(qms[hh], kj)
                s_scr[hh] = s if s_mul == 1.0 else s * s_mul
                dp_scr[hh] = _dot_nt(doms[hh], vj)
            out = []
            for hh in range(hpg):
                dq_acc, rs_p = carry[hh]
                fk_row = fk_ref[j, hh:hh + 1, :]
                rsums = []
                csums = [jnp.zeros((SUBLANES, LANES), F32) for _ in range(nlb)]
                for r0 in range(0, tq, rc):
                    rs = slice(r0, r0 + rc)
                    c_c = c_bs[hh][rs]
                    dl_c = dl_bs[hh][rs]
                    tot = None
                    for cb in range(nlb):
                        cs = slice(cb * LANES, (cb + 1) * LANES)
                        e = (s_scr[hh, rs, cs] - fk_row[:, cs]) + c_c
                        if masked:
                            ri = r0 + lax.broadcasted_iota(jnp.int32, (rc, LANES), 0)
                            ci = cb * LANES + lax.broadcasted_iota(jnp.int32, (rc, LANES), 1)
                            e = jnp.where(ci <= ri, e, MASK_VALUE)
                        p = jnp.exp(e)
                        ds = p * (dp_scr[hh, rs, cs] - dl_c)
                        p_scr[hh, rs, cs] = p.astype(BF16)
                        ds_scr[hh, rs, cs] = ds.astype(BF16)
                        tot = ds if tot is None else tot + ds
                        csums[cb] = csums[cb] + ds.reshape(rc // SUBLANES, SUBLANES, LANES).sum(axis=0)
                    rsums.append(tot)
                col = jnp.concatenate([jnp.sum(c, axis=0, keepdims=True) for c in csums], axis=1)
                dfk_ref[j, hh:hh + 1, :] += -col
                out.append((dq_acc + _dot(ds_scr[hh], kj), rs_p + jnp.concatenate(rsums, axis=0)))
            dk_new = _dot_tn(ds_scr[0], qms[0])
            dv_new = _dot_tn(p_scr[0], doms[0])
            for hh in range(1, hpg):
                dk_new = dk_new + _dot_tn(ds_scr[hh], qms[hh])
                dv_new = dv_new + _dot_tn(p_scr[hh], doms[hh])
            dk_acc[j] += dk_new
            dv_acc[j] += dv_new
            return tuple(out)

        init = tuple((jnp.zeros((tq, LANES), F32), jnp.zeros((tq, LANES), F32)) for _ in range(hpg))
        carry = lax.fori_loop(0, i, lambda j, cr: step(j, cr, False), init)
        carry = step(i, carry, True)
        dq_all = jnp.zeros((tq, LANES), F32)
        dfq_all = jnp.zeros((tq, LANES), F32)
        for hh in range(hpg):
            dq_h, rs_p = carry[hh]
            dq_all = jnp.where(hmasks[hh], dq_h, dq_all)
            dfq_all = jnp.where(lane == hh, jnp.sum(rs_p, axis=1, keepdims=True), dfq_all)
        dq_ref[...] = (dq_all * dq_mul).astype(BF16)
        dfq_ref[...] = dfq_all

        @pl.when(i == nq - 1)
        def _():
            dkv = dk_acc[...]
            if s_mul != 1.0:
                dkv = dkv * s_mul
            dk_ref[...] = dkv.astype(BF16)
            dv_ref[...] = dv_acc[...].astype(BF16)

    blk = pl.BlockSpec((tq, LANES), lambda g, i: (i, g))
    res = pl.BlockSpec((nq, tq, LANES), lambda g, i: (0, 0, g))
    aux = pl.BlockSpec((None, tq, LANES), lambda g, i: (g, i, 0))
    rows = pl.BlockSpec((None, nq, SUBLANES, tq), lambda g, i: (g, 0, 0, 0))
    dq, dk3, dv3, dfq, dfk = _pcall(
        body, name="flash_bwd", grid=(G, nq),
        in_specs=[blk, res, res, blk, aux, aux, aux, rows],
        out_specs=[blk, res, res, aux, rows],
        out_shape=[jax.ShapeDtypeStruct((T, D), BF16), jax.ShapeDtypeStruct((nq, tq, D), BF16),
                   jax.ShapeDtypeStruct((nq, tq, D), BF16), jax.ShapeDtypeStruct((G, T, LANES), F32),
                   jax.ShapeDtypeStruct((G, nq, SUBLANES, tq), F32)],
        scratch_shapes=[pltpu.VMEM((nq, tq, LANES), F32), pltpu.VMEM((nq, tq, LANES), F32),
                        pltpu.VMEM((hpg, tq, tq), F32), pltpu.VMEM((hpg, tq, tq), F32),
                        pltpu.VMEM((hpg, tq, tq), BF16), pltpu.VMEM((hpg, tq, tq), BF16)],
        compiler_params=_params(("arbitrary", "arbitrary")),
    )(q, k3, v3, do, fq_aux, lse_aux, dl_aux, fk_rows)
    return dq, dk3.reshape(T, D), dv3.reshape(T, D), dfq, dfk


def _attn_out_fwd(o, x, wo, g_post, tm):
    T, D = x.shape

    def body(o_ref, x_ref, w_ref, g_ref, m_ref, xo_ref):
        m = _dot(o_ref[...], w_ref[...])
        m_ref[...] = m
        xo_ref[...] = x_ref[...] + _rms(m, g_ref[...])

    return _pcall(
        body, name="attn_out_fwd", grid=(T // tm,),
        in_specs=[_rows(tm, D), _rows(tm, D), _full((D, D)), _full((1, D))],
        out_specs=[_rows(tm, D), _rows(tm, D)],
        out_shape=[jax.ShapeDtypeStruct((T, D), F32), jax.ShapeDtypeStruct((T, D), F32)],
        compiler_params=_params(("arbitrary",)),
    )(o, x, wo, g_post)


def _attn_out_bwd(dxo, m, o, wo, g_post, head_ind, tm, dep):
    T, D = m.shape

    def body(dxo_ref, m_ref, o_ref, w_ref, g_ref, ind_ref, dep_ref, dm_ref, do_ref, dl_ref, sums_ref):
        i = pl.program_id(0)
        dm, dgpost = _rms_bwd(m_ref[...], g_ref[...], dxo_ref[...])
        dmb = dm.astype(BF16)
        dm_ref[...] = dmb
        dob = _dot_nt(dmb, w_ref[...]).astype(BF16)
        do_ref[...] = dob
        dl_ref[...] = jnp.dot(dob.astype(F32) * o_ref[...], ind_ref[...], precision=lax.Precision.HIGHEST,
                              preferred_element_type=F32)

        @pl.when(i == 0)
        def _():
            sums_ref[...] = jnp.zeros_like(sums_ref)
        sums_ref[0:1, :] += dgpost

    return _pcall(
        body, name="attn_out_bwd", grid=(T // tm,),
        in_specs=[_rows(tm, D), _rows(tm, D), _rows(tm, D), _full((D, D)), _full((1, D)), _full((D, LANES)),
                  pl.BlockSpec(memory_space=pl.ANY)],
        out_specs=[_rows(tm, D), _rows(tm, D), _rows(tm, LANES), _acc((SUBLANES, D))],
        out_shape=[jax.ShapeDtypeStruct((T, D), BF16), jax.ShapeDtypeStruct((T, D), BF16),
                   jax.ShapeDtypeStruct((T, LANES), F32), jax.ShapeDtypeStruct((SUBLANES, D), F32)],
        compiler_params=_params(("arbitrary",)),
    )(dxo, m, o, wo, g_post, head_ind, dep)


def _attn_in_bwd(dxo, x, g_pre, dq, dk, dv, dlf, lf, wqkv, wf, tm, n_heads):
    T, D = x.shape

    def body(dxo_ref, x_ref, g_ref, dq_ref, dk_ref, dv_ref, dlf_ref, lf_ref, w_ref, wf_ref,
             dxi_ref, h_ref, df_ref, sums_ref, dbf_ref):
        i = pl.program_id(0)
        xv = x_ref[...]
        h_ref[...] = _rms(xv, g_ref[...]).astype(BF16)
        lane = lax.broadcasted_iota(jnp.int32, (1, LANES), 1)
        df = jnp.where(lane < n_heads, dlf_ref[...] * (1.0 - jnp.exp(lf_ref[...])), 0.0)
        dfb = df.astype(BF16)
        df_ref[...] = dfb
        dh = (_dot_nt(dq_ref[...], w_ref[:, 0:D]) + _dot_nt(dk_ref[...], w_ref[:, D:2 * D])
              + _dot_nt(dv_ref[...], w_ref[:, 2 * D:3 * D]) + _dot_nt(dfb, wf_ref[...]))
        dxi, dgpre = _rms_bwd(xv, g_ref[...], dh)
        dxi_ref[...] = dxo_ref[...] + dxi

        @pl.when(i == 0)
        def _():
            sums_ref[...] = jnp.zeros_like(sums_ref)
            dbf_ref[...] = jnp.zeros_like(dbf_ref)
        sums_ref[0:1, :] += dgpre
        dbf_ref[...] += jnp.sum(df, axis=0, keepdims=True)

    return _pcall(
        body, name="attn_in_bwd", grid=(T // tm,),
        in_specs=[_rows(tm, D), _rows(tm, D), _full((1, D)), _rows(tm, D), _rows(tm, D), _rows(tm, D),
                  _rows(tm, LANES), _rows(tm, LANES), _full((D, 3 * D)), _full((D, LANES))],
        out_specs=[_rows(tm, D), _rows(tm, D), _rows(tm, LANES), _acc((SUBLANES, D)), _acc((1, LANES))],
        out_shape=[jax.ShapeDtypeStruct((T, D), F32), jax.ShapeDtypeStruct((T, D), BF16),
                   jax.ShapeDtypeStruct((T, LANES), BF16), jax.ShapeDtypeStruct((SUBLANES, D), F32),
                   jax.ShapeDtypeStruct((1, LANES), F32)],
        compiler_params=_params(("arbitrary",)),
    )(dxo, x, g_pre, dq, dk, dv, dlf, lf, wqkv, wf)


def _loss_head(y, target, tm):
    T, D = y.shape

    def body(y_ref, t_ref, dy_ref, loss_ref):
        i = pl.program_id(0)
        err = y_ref[...] - t_ref[...]
        dy_ref[...] = err * (1.0 / D)
        part = 0.5 * jnp.sum(jnp.mean(err * err, axis=-1, keepdims=True), axis=0, keepdims=True)

        @pl.when(i == 0)
        def _():
            loss_ref[...] = jnp.zeros_like(loss_ref)
        loss_ref[...] += part

    return _pcall(
        body, name="loss_head", grid=(T // tm,),
        in_specs=[_rows(tm, D), _rows(tm, D)],
        out_specs=[_rows(tm, D), _acc((SUBLANES, LANES))],
        out_shape=[jax.ShapeDtypeStruct((T, D), F32), jax.ShapeDtypeStruct((SUBLANES, LANES), F32)],
        compiler_params=_params(("arbitrary",)),
    )(y, target)


def _adamw(recvs, w, m, v, tr, name):
    L, R, C = w.shape
    assert len(recvs) == L
    c1 = 1.0 - ADAM_B1 ** ADAM_STEP
    c2 = 1.0 - ADAM_B2 ** ADAM_STEP

    def body(*refs):
        r_refs = refs[:L]
        w_ref, m_ref, v_ref, g_ref, d_ref, nm_ref, nv_ref = refs[L:]
        layer = pl.program_id(0)
        g = None
        for k in range(L):
            gk = r_refs[k][0, :, 0:C].astype(F32)
            for s in range(1, N_DEV):
                gk = gk + r_refs[k][s, :, 0:C].astype(F32)
            g = gk if g is None else jnp.where(layer == k, gk, g)
        nm = ADAM_B1 * m_ref[...] + (1.0 - ADAM_B1) * g
        nv = ADAM_B2 * v_ref[...] + (1.0 - ADAM_B2) * jnp.square(g)
        m_hat = nm / c1
        v_hat = nv / c2
        g_ref[...] = g
        d_ref[...] = -ADAM_LR * (m_hat / (jnp.sqrt(v_hat) + ADAM_EPS) + ADAM_WD * w_ref[...])
        nm_ref[...] = nm
        nv_ref[...] = nv

    def recv_spec(k):
        return pl.BlockSpec((N_DEV, tr, recvs[k].shape[-1]), lambda l, i: (0, jnp.where(l == k, i, 0), 0))

    blk = pl.BlockSpec((None, tr, C), lambda l, i: (l, i, 0))
    return _pcall(
        body, name=name, grid=(L, R // tr),
        in_specs=[recv_spec(k) for k in range(L)] + [blk] * 3,
        out_specs=[blk] * 4,
        out_shape=[jax.ShapeDtypeStruct((L, R, C), F32)] * 4,
        compiler_params=_params(("arbitrary", "arbitrary")),
    )(*recvs, w, m, v)


def _row_block(rows, cols):
    cap = max(SUBLANES, (256 * 1024) // max(cols, 1))
    best = None
    for t in range(SUBLANES, rows + 1, SUBLANES):
        if rows % t == 0 and t <= cap:
            best = t
    return rows if best is None else best


def kernel(x, g_mix_pre, g_mix_post, g_ffn_pre, g_ffn_post, conv_pw1_w, conv_pw1_b, conv_dw_w, conv_dw_b, conv_ln_g, conv_ln_b, conv_pw2_w, conv_pw2_b, attn_w_in, attn_b_f, attn_w_o, mlp_w_up, mlp_w_down, loss_target, m_g_mix_pre, m_g_mix_post, m_g_ffn_pre, m_g_ffn_post, m_conv_pw1_w, m_conv_pw1_b, m_conv_dw_w, m_conv_dw_b, m_conv_ln_g, m_conv_ln_b, m_conv_pw2_w, m_conv_pw2_b, m_attn_w_in, m_attn_b_f, m_attn_w_o, m_mlp_w_up, m_mlp_w_down, v_g_mix_pre, v_g_mix_post, v_g_ffn_pre, v_g_ffn_post, v_conv_pw1_w, v_conv_pw1_b, v_conv_dw_w, v_conv_dw_b, v_conv_ln_g, v_conv_ln_b, v_conv_pw2_w, v_conv_pw2_b, v_attn_w_in, v_attn_b_f, v_attn_w_o, v_mlp_w_up, v_mlp_w_down):
    _, T, D = x.shape
    H = attn_b_f.shape[-1]
    dh = D // H
    width = conv_dw_w.shape[1]
    cin = attn_w_in.shape[-1]
    fs = mlp_w_up.shape[-1]
    G = D // LANES
    hpg = LANES // dh
    assert T % 4 == 0 and D % LANES == 0 and LANES % dh == 0 and width <= CONV_HALO and H <= LANES

    tm = min(512, T // 4)
    tmb = min(256, T // 4)
    tq = tm
    tmc = min(256, T // 4)
    lc = min(256, D)
    tkw = min(2048, T // 4)
    tb = min(256, T // 4)
    nq = T // tq

    scale = float(dh) ** -0.5
    mant, _ = math.frexp(scale)
    q_mul = scale if mant == 0.5 else 1.0
    s_mul = 1.0 if mant == 0.5 else scale

    x2 = x.reshape(T, D)
    tgt = loss_target.reshape(T, D)

    w_srcs = [conv_pw1_w, conv_dw_w, conv_pw2_w, mlp_w_up, mlp_w_down, attn_w_in, attn_w_o]
    w_items = [(0, 0, "whole"), (1, 0, "whole"), (2, 0, "whole"), (3, 0, "whole"), (4, 0, "whole"),
               (5, 0, "whole"), (6, 0, "whole"), (3, 1, "whole"), (4, 1, "whole")]
    cin_w = -(-cin // LANES) * LANES
    w_lands = _place_own(
        w_srcs, w_items,
        [((N_DEV, D, cin_w) if si == 5 else (N_DEV,) + w_srcs[si].shape[1:], F32 if si == 1 else BF16)
         for si, _, _ in w_items],
        "stage_weights", cast=True)
    me_arr = _dev_index(*_mesh_pos()).astype(jnp.int32).reshape(1)
    w_groups = [[0, 1, 2], [3, 4], [5, 6], [7, 8]]
    g_sems, _, w_lands, g_token = _push_start(
        [], w_lands, [[(a, None, "own") for a in grp] for grp in w_groups], "gather_start")

    def gather_wait(gi, after):
        grp = w_groups[gi]
        return _push_wait(g_sems[gi], [], [w_lands[a] for a in grp], [(k, None, "own") for k in range(len(grp))],
                          after, "gather_wait%d" % gi)

    w1g, dwg, w2g = gather_wait(0, g_token)
    w2 = w2g.reshape(D, D)
    dw_full = jnp.transpose(dwg, (1, 0, 2)).reshape(width, D)
    w32 = jnp.pad(dw_full, ((0, 32 - width), (0, 0)))
    bf = jnp.pad(attn_b_f, ((0, 0), (0, LANES - H)))

    row = lambda a, i: a[i:i + 1]

    a0, u0 = _conv_in_fwd(x2, row(g_mix_pre, 0), w1g, conv_pw1_b, tm)
    y0 = _dwconv_fwd(u0, w32, conv_dw_b, tmc, lc, width)
    m0, x_1 = _conv_out_fwd(y0, x2, conv_ln_g, conv_ln_b, w2, conv_pw2_b, row(g_mix_post, 0), tm)
    wu0, wd0 = gather_wait(1, x_1)
    up0, n0, x_2 = _mlp_fwd(x_1, row(g_ffn_pre, 0), wu0, wd0, row(g_ffn_post, 0), tm, "mlp0_fwd")

    wing, wog = gather_wait(2, x_2)
    wo = wog.reshape(D, D)
    win = jnp.transpose(wing[:, :, :cin], (1, 0, 2)).reshape(D, N_DEV * cin)
    wqkv = win[:, :3 * D]
    wf = jnp.pad(win[:, 3 * D:], ((0, 0), (0, LANES - H)))
    q, k, v, lf = _attn_in_fwd(x_2, row(g_mix_pre, 1), wqkv, wf, bf, tm, q_mul, H)
    fcum = _cumsum_rows(lf, jnp.zeros_like(lf), tb, False, "forget_cumsum")

    def to_aux(t):
        t = jnp.transpose(t[:, :H].reshape(T, G, hpg), (1, 0, 2))
        return jnp.pad(t, ((0, 0), (0, 0), (0, LANES - hpg)))

    fq_aux = to_aux(fcum)
    fk_rows = jnp.transpose(fcum[:, :H].T.reshape(G, hpg, nq, tq), (0, 2, 1, 3))
    fk_rows = jnp.pad(fk_rows, ((0, 0), (0, 0), (0, SUBLANES - hpg), (0, 0)))
    o, o32, lse_aux = _flash_fwd(q, k, v, fq_aux, fk_rows, dh=dh, tq=tq, s_mul=s_mul)
    m1, x_3 = _attn_out_fwd(o, x_2, wo, row(g_mix_post, 1), tm)
    wu1, wd1 = gather_wait(3, x_3)
    up1, n1, x_4 = _mlp_fwd(x_3, row(g_ffn_pre, 1), wu1, wd1, row(g_ffn_post, 1), tm, "mlp1_fwd")

    dx, loss_blk = _loss_head(x_4, tgt, tm)

    def mlp_back(dx, n_l, x_in, up_l, l, wu, wd, dep):
        dxi, h, dm, dup, sums = _mlp_bwd(dx, n_l, x_in, up_l, row(g_ffn_pre, l), wu, wd, row(g_ffn_post, l),
                                         tmb, "mlp%d_bwd" % l, dep)
        dwu = _mm_tn(h, dup, nj=N_DEV, a_cols=D, g_cols=fs, a_by_j=False, g_by_j=True, tk=tkw,
                     name="mlp%d_dwu" % l)
        dwd = _mm_tn(up_l, dm, nj=N_DEV, a_cols=fs, g_cols=D, a_by_j=True, g_by_j=False, tk=tkw,
                     name="mlp%d_dwd" % l, act=True)
        return dxi, dwu, dwd, sums

    def push_grads(srcs, modes, name):
        items = [(i, None, mode) for i, mode in enumerate(modes)]
        lands = _seed_lands(srcs, modes, me_arr, name + "_own")
        sems, srcs_t, lands_t, token = _push_start(srcs, lands, [items], name + "_start")
        return (sems[0], srcs_t, lands_t, items), token

    def pull_grads(handle, after, name):
        sems, srcs_t, lands_t, items = handle
        return _push_wait(sems, srcs_t, lands_t, items, after, name + "_wait")

    rs = D // N_DEV
    dx, dwu1, dwd1, s_mlp1 = mlp_back(dx, n1, x_3, up1, 1, wu1, wd1, loss_blk)
    h_mlp1, tok = push_grads([dwu1, dwd1], ["slot", "slot"], "grads_mlp1")

    head_ind = jnp.asarray((np.arange(D)[:, None] // dh == np.arange(LANES)[None, :]).astype(np.float32))
    dm1, do, delta, s_ao = _attn_out_bwd(dx, m1, o32, wo, row(g_mix_post, 1), head_ind, tm, tok)
    dwo = _mm_tn(o, dm1, nj=1, a_cols=D, g_cols=D, a_by_j=False, g_by_j=False, tk=tkw, name="attn_dwo")
    dq, dk, dv, dfq, dfk = _flash_bwd(q, k, v, do, fq_aux, lse_aux, to_aux(delta), fk_rows,
                                      dh=dh, tq=tq, s_mul=s_mul, dq_mul=scale)
    df_k = jnp.pad(jnp.transpose(dfk[:, :, :hpg, :], (0, 2, 1, 3)).reshape(H, T).T, ((0, 0), (0, LANES - H)))
    df_q = jnp.pad(jnp.transpose(dfq[:, :, :hpg], (1, 0, 2)).reshape(T, H), ((0, 0), (0, LANES - H)))
    dlf = _cumsum_rows(df_q, df_k, tb, True, "forget_cumsum_bwd")
    dx, h_at, df, s_ai, dbf = _attn_in_bwd(dx, x_2, row(g_mix_pre, 1), dq, dk, dv, dlf, lf, wqkv, wf, tm, H)
    dwq = _mm_tn(h_at, dq, nj=1, a_cols=D, g_cols=D, a_by_j=False, g_by_j=False, tk=tkw, name="attn_dwq")
    dwk = _mm_tn(h_at, dk, nj=1, a_cols=D, g_cols=D, a_by_j=False, g_by_j=False, tk=tkw, name="attn_dwk")
    dwv = _mm_tn(h_at, dv, nj=1, a_cols=D, g_cols=D, a_by_j=False, g_by_j=False, tk=tkw, name="attn_dwv")
    dwf = _mm_tn(h_at, df, nj=1, a_cols=D, g_cols=LANES, a_by_j=False, g_by_j=False, tk=tkw, name="attn_dwf")
    dwin = jnp.concatenate([dwq[0], dwk[0], dwv[0], dwf[0][:, :H]], axis=1)
    dwin = jnp.pad(jnp.transpose(dwin.reshape(D, N_DEV, cin), (1, 0, 2)), ((0, 0), (0, 0), (0, cin_w - cin)))
    h_attn, tok = push_grads([dwin, dwo.reshape(N_DEV, rs, D)], ["slot", "slot"], "grads_attn")

    dx, dwu0, dwd0, s_mlp0 = mlp_back(dx, n0, x_1, up0, 0, wu0, wd0, tok)
    h_mlp0, tok = push_grads([dwu0, dwd0], ["slot", "slot"], "grads_mlp0")

    dy0, dm0, z0, s_co = _conv_out_bwd(dx, m0, y0, conv_ln_g, conv_ln_b, w2, row(g_mix_post, 0), tm, tok)
    dw2 = _mm_tn(z0, dm0, nj=1, a_cols=D, g_cols=D, a_by_j=False, g_by_j=False, tk=tkw, name="conv_dw2")
    du0, ddw = _dwconv_bwd(dy0, u0, w32, tmc, lc, width)
    grad_x, h_cv, da0, s_ci, db1 = _conv_in_bwd(dx, du0, a0, x2, row(g_mix_pre, 0), w1g, tm)
    dw1 = _mm_tn(h_cv, da0, nj=N_DEV, a_cols=D, g_cols=(2 * D) // N_DEV, a_by_j=False, g_by_j=True, tk=tkw,
                 name="conv_dw1")
    ddw_s = jnp.transpose(ddw[:width].reshape(width, N_DEV, D // N_DEV), (1, 0, 2))

    def pad_row(a):
        return jnp.pad(a, ((0, 0), (0, D - a.shape[1])))

    def pack(gmp, gmq, gfp, gfq, b1, dwb, lng, lnb, b2, bfv, last):
        return jnp.concatenate([gmp, gmq, gfp, gfq, b1.reshape(2, D), dwb, lng, lnb, b2, pad_row(bfv), last],
                               axis=0)

    zero_row = jnp.zeros((1, D), F32)
    small_g = pack(
        jnp.concatenate([row(s_ci, 0), row(s_ai, 0)], axis=0),
        jnp.concatenate([row(s_co, 0), row(s_ao, 0)], axis=0),
        jnp.concatenate([row(s_mlp0, 1), row(s_mlp1, 1)], axis=0),
        jnp.concatenate([row(s_mlp0, 0), row(s_mlp1, 0)], axis=0),
        db1, row(s_co, 4), row(s_co, 1), row(s_co, 2), row(s_co, 3), dbf[:, :H],
        pad_row(loss_blk[0:1, 0:1]))
    h_conv, _ = push_grads([dw1, ddw_s, dw2.reshape(N_DEV, rs, D), small_g], ["slot", "slot", "slot", "whole"],
                           "grads_conv")

    def opt(recvs, w, m, v, name):
        shp = w.shape
        L, C = shp[0], shp[-1]
        R = int(np.prod(shp[1:-1]))
        outs = _adamw([r.reshape(N_DEV, R, r.shape[-1]) for r in recvs], w.reshape(L, R, C), m.reshape(L, R, C),
                      v.reshape(L, R, C), _row_block(R, C), name)
        return [t.reshape(shp) for t in outs]

    big = {}
    r_wu1, r_wd1 = pull_grads(h_mlp1, grad_x, "grads_mlp1")
    r_win, r_wo = pull_grads(h_attn, r_wd1, "grads_attn")
    big["attn_w_in"] = opt([r_win], attn_w_in, m_attn_w_in, v_attn_w_in, "adamw_win")
    big["attn_w_o"] = opt([r_wo], attn_w_o, m_attn_w_o, v_attn_w_o, "adamw_wo")
    r_wu0, r_wd0 = pull_grads(h_mlp0, big["attn_w_o"][0], "grads_mlp0")
    big["mlp_w_up"] = opt([r_wu0, r_wu1], mlp_w_up, m_mlp_w_up, v_mlp_w_up, "adamw_wup")
    big["mlp_w_down"] = opt([r_wd0, r_wd1], mlp_w_down, m_mlp_w_down, v_mlp_w_down, "adamw_wdown")
    r_w1, r_dw, r_w2, r_small = pull_grads(h_conv, big["mlp_w_down"][0], "grads_conv")
    big["conv_pw1_w"] = opt([r_w1], conv_pw1_w, m_conv_pw1_w, v_conv_pw1_w, "adamw_pw1")
    big["conv_dw_w"] = opt([r_dw], conv_dw_w, m_conv_dw_w, v_conv_dw_w, "adamw_dw")
    big["conv_pw2_w"] = opt([r_w2], conv_pw2_w, m_conv_pw2_w, v_conv_pw2_w, "adamw_pw2")
    small_w = pack(g_mix_pre, g_mix_post, g_ffn_pre, g_ffn_post, conv_pw1_b, conv_dw_b, conv_ln_g, conv_ln_b,
                   conv_pw2_b, attn_b_f, zero_row)
    small_m = pack(m_g_mix_pre, m_g_mix_post, m_g_ffn_pre, m_g_ffn_post, m_conv_pw1_b, m_conv_dw_b, m_conv_ln_g,
                   m_conv_ln_b, m_conv_pw2_b, m_attn_b_f, zero_row)
    small_v = pack(v_g_mix_pre, v_g_mix_post, v_g_ffn_pre, v_g_ffn_post, v_conv_pw1_b, v_conv_dw_b, v_conv_ln_g,
                   v_conv_ln_b, v_conv_pw2_b, v_attn_b_f, zero_row)
    sm = _adamw([r_small], small_w[None], small_m[None], small_v[None], small_w.shape[0], "adamw_small")
    sm = [t[0] for t in sm]
    loss = sm[0][15, 0]

    def unpack(t):
        return {"g_mix_pre": t[0:2], "g_mix_post": t[2:4], "g_ffn_pre": t[4:6], "g_ffn_post": t[6:8],
                "conv_pw1_b": t[8:10].reshape(1, 2 * D), "conv_dw_b": t[10:11], "conv_ln_g": t[11:12],
                "conv_ln_b": t[12:13], "conv_pw2_b": t[13:14], "attn_b_f": t[14:15, :H]}

    small = [unpack(t) for t in sm]
    names = ["g_mix_pre", "g_mix_post", "g_ffn_pre", "g_ffn_post", "conv_pw1_w", "conv_pw1_b", "conv_dw_w",
             "conv_dw_b", "conv_ln_g", "conv_ln_b", "conv_pw2_w", "conv_pw2_b", "attn_w_in", "attn_b_f",
             "attn_w_o", "mlp_w_up", "mlp_w_down"]
    outs = [loss, grad_x.reshape(1, T, D)]
    for kind in range(4):
        for nme in names:
            outs.append(big[nme][kind] if nme in big else small[kind][nme])
    return tuple(outs)
```

```python
import functools
import math

import numpy as np
import jax
import jax.numpy as jnp
from jax import lax
from jax.experimental import pallas as pl
from jax.experimental.pallas import tpu as pltpu

F32 = jnp.float32
BF16 = jnp.bfloat16

RMS_EPS = 1e-6
LN_EPS = 1e-5
MASK_VALUE = -1e30
ADAM_LR = 0.001
ADAM_B1 = 0.9
ADAM_B2 = 0.999
ADAM_EPS = 1e-08
ADAM_WD = 0.01
ADAM_STEP = 10

N_DEV = 8
LANES = 128
SUBLANES = 8
CONV_HALO = 32
CONV_ROWS = 32
FLASH_ROWS = 32
VMEM_LIMIT = 56 * 1024 * 1024

_pcall = pl.pallas_call


def _params(sem=None):
    if sem is None:
        return pltpu.CompilerParams(vmem_limit_bytes=VMEM_LIMIT)
    return pltpu.CompilerParams(dimension_semantics=sem, vmem_limit_bytes=VMEM_LIMIT)


def _dot(a, b):
    return jnp.dot(a, b, preferred_element_type=F32)


def _dot_nt(a, b):
    return lax.dot_general(a, b, (((1,), (1,)), ((), ())), preferred_element_type=F32)


def _dot_tn(a, b):
    return lax.dot_general(a, b, (((0,), (0,)), ((), ())), preferred_element_type=F32)


def _full(shape):
    nd = len(shape)
    return pl.BlockSpec(shape, lambda *g: (0,) * nd, pipeline_mode=pl.Buffered(1))


def _acc(shape):
    nd = len(shape)
    return pl.BlockSpec(shape, lambda *g: (0,) * nd)


def _rows(tm, cols):
    return pl.BlockSpec((tm, cols), lambda i: (i, 0))


def _rms(x, g):
    r = lax.rsqrt(jnp.mean(x * x, axis=-1, keepdims=True) + RMS_EPS)
    return x * r * g


def _rms_bwd(x, g, dy):
    r = lax.rsqrt(jnp.mean(x * x, axis=-1, keepdims=True) + RMS_EPS)
    n = x * r
    dg = jnp.sum(dy * n, axis=0, keepdims=True)
    dn = dy * g
    dx = r * (dn - n * jnp.mean(dn * n, axis=-1, keepdims=True))
    return dx, dg


def _sigmoid(x):
    return 1.0 / (1.0 + jnp.exp(-x))


def _mesh_pos():
    return lax.axis_index("x"), lax.axis_index("y"), lax.axis_index("c")


def _dev_index(px, py, pc):
    return 4 * px + 2 * py + pc


_HBM = pl.BlockSpec(memory_space=pltpu.HBM)
_SEM = pl.BlockSpec(memory_space=pltpu.SEMAPHORE)
_EFFECT = pltpu.SideEffectType.DATAFLOW_SIDE_EFFECTING


def _peer(r, x, y, c):
    p = ((1 - x) if r & 4 else x, (1 - y) if r & 2 else y, (1 - c) if r & 1 else c)
    return p, _dev_index(*p)


def _src_ref(refs, item, me_id=None, to_id=None):
    si, sub, mode = item
    r = refs[si] if sub is None else refs[si].at[sub]
    if mode == "slot":
        return r.at[to_id]
    if mode == "own":
        return r.at[me_id]
    return r


def _place_own(srcs, items, land_shapes, name, cast=False):
    ns, n = len(srcs), len(items)

    def body(*refs):
        src = refs[:ns]
        land = refs[ns:ns + n]
        stage = refs[ns + n:ns + 2 * n] if cast else None
        sems = refs[-1]
        me_id = _dev_index(*_mesh_pos())
        cps = []
        for a, item in enumerate(items):
            s = _src_ref(src, item, to_id=me_id)
            if cast:
                if s.shape != stage[a].shape:
                    stage[a][...] = jnp.zeros_like(stage[a])
                    stage[a][:, 0:s.shape[-1]] = s[...].astype(stage[a].dtype)
                else:
                    stage[a][...] = s[...].astype(stage[a].dtype)
                s = stage[a]
            cp = pltpu.make_async_copy(s, land[a].at[me_id], sems.at[a])
            cp.start()
            cps.append(cp)
        for cp in cps:
            cp.wait()

    return _pcall(
        body, name=name,
        out_shape=[jax.ShapeDtypeStruct(tuple(s), d) for s, d in land_shapes],
        in_specs=[pl.BlockSpec(memory_space=pltpu.VMEM if cast else pl.ANY)] * ns,
        out_specs=[pl.BlockSpec(memory_space=pl.ANY)] * n,
        scratch_shapes=([pltpu.VMEM(tuple(s[1:]), d) for s, d in land_shapes] if cast else [])
        + [pltpu.SemaphoreType.DMA((n,))],
        compiler_params=pltpu.CompilerParams(vmem_limit_bytes=VMEM_LIMIT),
    )(*srcs)


def _seed_lands(srcs, modes, me, name):
    n = len(srcs)
    parts = [tuple(s.shape[1:]) if mode == "slot" else tuple(s.shape) for s, mode in zip(srcs, modes)]

    def body(me_ref, *refs):
        s = pl.program_id(0)
        for a in range(n):
            v = refs[a][...]
            refs[n + a][...] = jnp.where(s == me_ref[0], v, jnp.zeros_like(v))

    def in_spec(part, mode):
        nd = len(part)
        if mode == "slot":
            return pl.BlockSpec((None,) + part, lambda s, me_ref: (me_ref[0],) + (0,) * nd)
        return pl.BlockSpec(part, lambda s, me_ref: (0,) * nd)

    def out_spec(part):
        nd = len(part)
        return pl.BlockSpec((None,) + part, lambda s, me_ref: (s,) + (0,) * nd)

    return _pcall(
        body, name=name,
        grid_spec=pltpu.PrefetchScalarGridSpec(
            num_scalar_prefetch=1, grid=(N_DEV,),
            in_specs=[in_spec(p, m) for p, m in zip(parts, modes)],
            out_specs=[out_spec(p) for p in parts]),
        out_shape=[jax.ShapeDtypeStruct((N_DEV,) + p, s.dtype) for p, s in zip(parts, srcs)],
        compiler_params=_params(("arbitrary",)),
    )(me, *srcs)


def _push_start(srcs, lands, groups, name):
    ns, n = len(srcs), len(lands)
    ng = len(groups)
    assert sum(len(g) for g in groups) == n

    def body(*refs):
        ops = refs[:ns + n]
        land = refs[ns:ns + n]
        sems = refs[ns + n:ns + n + 2 * ng]
        token = refs[-1]
        x, y, c = _mesh_pos()
        me_id = _dev_index(x, y, c)
        a = 0
        for gi, grp in enumerate(groups):
            for k, item in enumerate(grp):
                for r in range(1, N_DEV):
                    p, pid = _peer(r, x, y, c)
                    pltpu.make_async_remote_copy(
                        src_ref=_src_ref(ops, item, me_id=me_id, to_id=pid), dst_ref=land[a].at[me_id],
                        send_sem=sems[2 * gi].at[k * (N_DEV - 1) + r - 1],
                        recv_sem=sems[2 * gi + 1].at[k * (N_DEV - 1) + r - 1],
                        device_id=p, device_id_type=pl.DeviceIdType.MESH).start()
                a += 1
        token[...] = jnp.zeros_like(token)

    sem_shapes = []
    for grp in groups:
        sem_shapes += [pltpu.SemaphoreType.DMA((len(grp) * (N_DEV - 1),))] * 2
    arrs = list(srcs) + list(lands)
    res = _pcall(
        body, name=name,
        out_shape=tuple(sem_shapes) + tuple(pltpu.HBM(a.shape, a.dtype) for a in arrs)
        + (jax.ShapeDtypeStruct((SUBLANES, LANES), F32),),
        in_specs=[_HBM] * (ns + n),
        out_specs=tuple([_SEM] * (2 * ng)) + tuple([_HBM] * (ns + n)) + (pl.BlockSpec(memory_space=pltpu.VMEM),),
        input_output_aliases={i: 2 * ng + i for i in range(ns + n)},
        compiler_params=pltpu.CompilerParams(has_side_effects=_EFFECT),
    )(*[pltpu.with_memory_space_constraint(a, pltpu.HBM) for a in arrs])
    sems = [(res[2 * gi], res[2 * gi + 1]) for gi in range(ng)]
    thru = res[2 * ng:2 * ng + ns + n]
    return sems, list(thru[:ns]), list(thru[ns:]), res[-1]


def _push_wait(sems, srcs, lands, group, after, name):
    ns, n = len(srcs), len(lands)
    assert len(group) == n

    def body(*refs):
        ops = refs[:ns + n]
        land = refs[ns:ns + n]
        send_sems, recv_sems = refs[ns + n], refs[ns + n + 1]
        x, y, c = _mesh_pos()
        me_id = _dev_index(x, y, c)
        for k, item in enumerate(group):
            for r in range(1, N_DEV):
                p, pid = _peer(r, x, y, c)
                cp = pltpu.make_async_remote_copy(
                    src_ref=_src_ref(ops, item, me_id=me_id, to_id=pid), dst_ref=land[k].at[pid],
                    send_sem=send_sems.at[k * (N_DEV - 1) + r - 1], recv_sem=recv_sems.at[k * (N_DEV - 1) + r - 1],
                    device_id=p, device_id_type=pl.DeviceIdType.MESH)
                cp.wait_send()
                cp.wait_recv()

    arrs = list(srcs) + list(lands)
    res = _pcall(
        body, name=name,
        out_shape=tuple(pltpu.HBM(a.shape, a.dtype) for a in arrs),
        in_specs=[_HBM] * (ns + n) + [_SEM, _SEM, pl.BlockSpec(memory_space=pl.ANY)],
        out_specs=tuple([_HBM] * (ns + n)),
        input_output_aliases={i: i for i in range(ns + n)},
        compiler_params=pltpu.CompilerParams(has_side_effects=_EFFECT),
    )(*arrs, sems[0], sems[1], after)
    return list(res[ns:])


def _conv_in_fwd(x, g_pre, w1g, b1, tm):
    T, D = x.shape
    ns, _, cs = w1g.shape
    half = ns // 2

    def body(x_ref, g_ref, w_ref, b_ref, a_ref, u_ref):
        h = _rms(x_ref[...], g_ref[...]).astype(BF16)
        parts = []
        for s in range(ns):
            a_s = _dot(h, w_ref[s]) + b_ref[:, s * cs:(s + 1) * cs]
            a_ref[:, s * cs:(s + 1) * cs] = a_s
            parts.append(a_s)
        for s in range(half):
            u_ref[:, s * cs:(s + 1) * cs] = parts[s] * _sigmoid(parts[s + half])

    return _pcall(
        body, name="conv_in_fwd", grid=(T // tm,),
        in_specs=[_rows(tm, D), _full((1, D)), _full(w1g.shape), _full((1, 2 * D))],
        out_specs=[_rows(tm, 2 * D), _rows(tm, D)],
        out_shape=[jax.ShapeDtypeStruct((T, 2 * D), F32), jax.ShapeDtypeStruct((T, D), F32)],
        compiler_params=_params(("arbitrary",)),
    )(x, g_pre, w1g, b1)


def _shifted_copies(ext_ref, sh_ref, tm):
    n = tm + CONV_HALO - SUBLANES
    for b in range(1, SUBLANES):
        sh_ref[b - 1, 0:n, :] = ext_ref[b:b + n, :]


def _shifted_rows(ext_ref, sh_ref, off, r0, ls):
    b = off % SUBLANES
    a8 = off - b + r0
    src = ext_ref if b == 0 else sh_ref.at[b - 1]
    return src[a8:a8 + CONV_ROWS, ls]


def _dwconv_fwd(u, w32, b, tm, lc, width):
    T, D = u.shape
    hb = tm // CONV_HALO

    def body(u_ref, halo_ref, w_ref, b_ref, y_ref, ext_ref, sh_ref):
        i = pl.program_id(0)
        ext_ref[0:CONV_HALO, :] = jnp.where(i > 0, halo_ref[...], 0.0)
        ext_ref[CONV_HALO:, :] = u_ref[...]
        _shifted_copies(ext_ref, sh_ref, tm)
        for r0 in range(0, tm, CONV_ROWS):
            for l0 in range(0, lc, LANES):
                ls = slice(l0, l0 + LANES)
                acc = jnp.zeros((CONV_ROWS, LANES), F32) + b_ref[:, ls]
                for j in range(width):
                    off = CONV_HALO - (width - 1) + j
                    acc = acc + w_ref[j:j + 1, ls] * _shifted_rows(ext_ref, sh_ref, off, r0, ls)
                y_ref[r0:r0 + CONV_ROWS, ls] = acc

    return _pcall(
        body, name="dwconv_fwd", grid=(T // tm, D // lc),
        in_specs=[pl.BlockSpec((tm, lc), lambda i, l: (i, l)),
                  pl.BlockSpec((CONV_HALO, lc), lambda i, l: (jnp.maximum(i * hb - 1, 0), l)),
                  pl.BlockSpec((32, lc), lambda i, l: (0, l)),
                  pl.BlockSpec((1, lc), lambda i, l: (0, l))],
        out_specs=pl.BlockSpec((tm, lc), lambda i, l: (i, l)),
        out_shape=jax.ShapeDtypeStruct((T, D), F32),
        scratch_shapes=[pltpu.VMEM((tm + CONV_HALO, lc), F32),
                        pltpu.VMEM((SUBLANES - 1, tm + CONV_HALO, lc), F32)],
        compiler_params=_params(("arbitrary", "arbitrary")),
    )(u, u, w32, b)


def _ln_parts(y, g, b):
    mu = jnp.mean(y, axis=-1, keepdims=True)
    yc = y - mu
    rstd = lax.rsqrt(jnp.mean(yc * yc, axis=-1, keepdims=True) + LN_EPS)
    yhat = yc * rstd
    return yhat, rstd, yhat * g + b


def _conv_out_fwd(y, x, ln_g, ln_b, w2, b2, g_post, tm):
    T, D = x.shape

    def body(y_ref, x_ref, lg_ref, lb_ref, w_ref, b_ref, g_ref, m_ref, xo_ref):
        _, _, yn = _ln_parts(y_ref[...], lg_ref[...], lb_ref[...])
        z = (yn * _sigmoid(yn)).astype(BF16)
        m = _dot(z, w_ref[...]) + b_ref[...]
        m_ref[...] = m
        xo_ref[...] = x_ref[...] + _rms(m, g_ref[...])

    return _pcall(
        body, name="conv_out_fwd", grid=(T // tm,),
        in_specs=[_rows(tm, D), _rows(tm, D), _full((1, D)), _full((1, D)), _full((D, D)), _full((1, D)),
                  _full((1, D))],
        out_specs=[_rows(tm, D), _rows(tm, D)],
        out_shape=[jax.ShapeDtypeStruct((T, D), F32), jax.ShapeDtypeStruct((T, D), F32)],
        compiler_params=_params(("arbitrary",)),
    )(y, x, ln_g, ln_b, w2, b2, g_post)


def _conv_out_bwd(dxo, m, y, ln_g, ln_b, w2, g_post, tm, dep):
    T, D = m.shape

    def body(dxo_ref, m_ref, y_ref, lg_ref, lb_ref, w_ref, g_ref, dep_ref, dy_ref, dm_ref, z_ref, sums_ref):
        i = pl.program_id(0)
        dm, dgpost = _rms_bwd(m_ref[...], g_ref[...], dxo_ref[...])
        dmb = dm.astype(BF16)
        dm_ref[...] = dmb
        yhat, rstd, yn = _ln_parts(y_ref[...], lg_ref[...], lb_ref[...])
        sg = _sigmoid(yn)
        z_ref[...] = (yn * sg).astype(BF16)
        dz = _dot_nt(dmb, w_ref[...])
        dyn = dz * (sg + yn * sg * (1.0 - sg))
        dyh = dyn * lg_ref[...]
        dy = rstd * (dyh - jnp.mean(dyh, axis=-1, keepdims=True)
                     - yhat * jnp.mean(dyh * yhat, axis=-1, keepdims=True))
        dy_ref[...] = dy

        @pl.when(i == 0)
        def _():
            sums_ref[...] = jnp.zeros_like(sums_ref)
        sums_ref[0:1, :] += dgpost
        sums_ref[1:2, :] += jnp.sum(dyn * yhat, axis=0, keepdims=True)
        sums_ref[2:3, :] += jnp.sum(dyn, axis=0, keepdims=True)
        sums_ref[3:4, :] += jnp.sum(dm, axis=0, keepdims=True)
        sums_ref[4:5, :] += jnp.sum(dy, axis=0, keepdims=True)

    return _pcall(
        body, name="conv_out_bwd", grid=(T // tm,),
        in_specs=[_rows(tm, D), _rows(tm, D), _rows(tm, D), _full((1, D)), _full((1, D)), _full((D, D)),
                  _full((1, D)), pl.BlockSpec(memory_space=pl.ANY)],
        out_specs=[_rows(tm, D), _rows(tm, D), _rows(tm, D), _acc((SUBLANES, D))],
        out_shape=[jax.ShapeDtypeStruct((T, D), F32), jax.ShapeDtypeStruct((T, D), BF16),
                   jax.ShapeDtypeStruct((T, D), BF16), jax.ShapeDtypeStruct((SUBLANES, D), F32)],
        compiler_params=_params(("arbitrary",)),
    )(dxo, m, y, ln_g, ln_b, w2, g_post, dep)


def _dwconv_bwd(dy, u, w32, tm, lc, width):
    T, D = u.shape
    hb = tm // CONV_HALO
    nt = T // tm
    last_halo = T // CONV_HALO - 1

    def body(dy_ref, dyn_ref, u_ref, up_ref, w_ref, du_ref, dw_ref, exty_ref, extu_ref, acc_ref, shy_ref, shu_ref):
        i = pl.program_id(1)
        exty_ref[0:tm, :] = dy_ref[...]
        exty_ref[tm:, :] = jnp.where(i < nt - 1, dyn_ref[...], 0.0)
        extu_ref[0:CONV_HALO, :] = jnp.where(i > 0, up_ref[...], 0.0)
        extu_ref[CONV_HALO:, :] = u_ref[...]
        _shifted_copies(exty_ref, shy_ref, tm)
        _shifted_copies(extu_ref, shu_ref, tm)

        @pl.when(i == 0)
        def _():
            acc_ref[...] = jnp.zeros_like(acc_ref)

        for r0 in range(0, tm, CONV_ROWS):
            for l0 in range(0, lc, LANES):
                ls = slice(l0, l0 + LANES)
                dyc = exty_ref[r0:r0 + CONV_ROWS, ls]
                du = jnp.zeros((CONV_ROWS, LANES), F32)
                for j in range(width):
                    du = du + w_ref[j:j + 1, ls] * _shifted_rows(exty_ref, shy_ref, (width - 1) - j, r0, ls)
                    prod = dyc * _shifted_rows(extu_ref, shu_ref, CONV_HALO - (width - 1) + j, r0, ls)
                    acc_ref[j, :, ls] += prod.reshape(CONV_ROWS // SUBLANES, SUBLANES, LANES).sum(axis=0)
                du_ref[r0:r0 + CONV_ROWS, ls] = du

        @pl.when(i == nt - 1)
        def _():
            for j in range(32):
                dw_ref[j:j + 1, :] = jnp.sum(acc_ref[j], axis=0, keepdims=True)

    return _pcall(
        body, name="dwconv_bwd", grid=(D // lc, nt),
        in_specs=[pl.BlockSpec((tm, lc), lambda l, i: (i, l)),
                  pl.BlockSpec((CONV_HALO, lc), lambda l, i: (jnp.minimum((i + 1) * hb, last_halo), l)),
                  pl.BlockSpec((tm, lc), lambda l, i: (i, l)),
                  pl.BlockSpec((CONV_HALO, lc), lambda l, i: (jnp.maximum(i * hb - 1, 0), l)),
                  pl.BlockSpec((32, lc), lambda l, i: (0, l))],
        out_specs=[pl.BlockSpec((tm, lc), lambda l, i: (i, l)),
                   pl.BlockSpec((32, lc), lambda l, i: (0, l))],
        out_shape=[jax.ShapeDtypeStruct((T, D), F32), jax.ShapeDtypeStruct((32, D), F32)],
        scratch_shapes=[pltpu.VMEM((tm + CONV_HALO, lc), F32), pltpu.VMEM((tm + CONV_HALO, lc), F32),
                        pltpu.VMEM((32, SUBLANES, lc), F32),
                        pltpu.VMEM((SUBLANES - 1, tm + CONV_HALO, lc), F32),
                        pltpu.VMEM((SUBLANES - 1, tm + CONV_HALO, lc), F32)],
        compiler_params=_params(("arbitrary", "arbitrary")),
    )(dy, dy, u, u, w32)


def _conv_in_bwd(dxo, du, a, x, g_pre, w1g, tm):
    T, D = x.shape
    ns, _, cs = w1g.shape
    half = ns // 2

    def body(dxo_ref, du_ref, a_ref, x_ref, g_ref, w_ref, dxi_ref, h_ref, da_ref, sums_ref, db_ref):
        i = pl.program_id(0)
        xv = x_ref[...]
        h_ref[...] = _rms(xv, g_ref[...]).astype(BF16)
        dh = jnp.zeros((tm, D), F32)
        dbs = [None] * ns
        for s in range(half):
            a_u = a_ref[:, s * cs:(s + 1) * cs]
            sg = _sigmoid(a_ref[:, (s + half) * cs:(s + half + 1) * cs])
            du_s = du_ref[:, s * cs:(s + 1) * cs]
            da_u = du_s * sg
            da_g = du_s * a_u * sg * (1.0 - sg)
            for s2, v in ((s, da_u), (s + half, da_g)):
                vb = v.astype(BF16)
                da_ref[:, s2 * cs:(s2 + 1) * cs] = vb
                dbs[s2] = jnp.sum(v, axis=0, keepdims=True)
                dh = dh + _dot_nt(vb, w_ref[s2])
        dxi, dgpre = _rms_bwd(xv, g_ref[...], dh)
        dxi_ref[...] = dxo_ref[...] + dxi

        @pl.when(i == 0)
        def _():
            sums_ref[...] = jnp.zeros_like(sums_ref)
            db_ref[...] = jnp.zeros_like(db_ref)
        sums_ref[0:1, :] += dgpre
        for s in range(ns):
            db_ref[:, s * cs:(s + 1) * cs] += dbs[s]

    return _pcall(
        body, name="conv_in_bwd", grid=(T // tm,),
        in_specs=[_rows(tm, D), _rows(tm, D), _rows(tm, 2 * D), _rows(tm, D), _full((1, D)),
                  _full(w1g.shape)],
        out_specs=[_rows(tm, D), _rows(tm, D), _rows(tm, 2 * D), _acc((SUBLANES, D)), _acc((1, 2 * D))],
        out_shape=[jax.ShapeDtypeStruct((T, D), F32), jax.ShapeDtypeStruct((T, D), BF16),
                   jax.ShapeDtypeStruct((T, 2 * D), BF16), jax.ShapeDtypeStruct((SUBLANES, D), F32),
                   jax.ShapeDtypeStruct((1, 2 * D), F32)],
        compiler_params=_params(("arbitrary",)),
    )(dxo, du, a, x, g_pre, w1g)


def _mlp_fwd(x, g_pre, wug, wdg, g_post, tm, name):
    T, D = x.shape
    ns, _, fs = wug.shape

    def body(x_ref, gp_ref, wu_ref, wd_ref, gq_ref, up_ref, m_ref, xo_ref):
        xv = x_ref[...]
        h = _rms(xv, gp_ref[...]).astype(BF16)
        acc = jnp.zeros((tm, D), F32)
        for s in range(ns):
            up = _dot(h, wu_ref[s]).astype(BF16)
            up_ref[:, s * fs:(s + 1) * fs] = up
            act = jnp.square(jnp.maximum(up.astype(F32), 0.0)).astype(BF16)
            acc = acc + _dot(act, wd_ref[s])
        m_ref[...] = acc
        xo_ref[...] = xv + _rms(acc, gq_ref[...])

    return _pcall(
        body, name=name, grid=(T // tm,),
        in_specs=[_rows(tm, D), _full((1, D)), _full(wug.shape), _full(wdg.shape), _full((1, D))],
        out_specs=[_rows(tm, ns * fs), _rows(tm, D), _rows(tm, D)],
        out_shape=[jax.ShapeDtypeStruct((T, ns * fs), BF16), jax.ShapeDtypeStruct((T, D), F32),
                   jax.ShapeDtypeStruct((T, D), F32)],
        compiler_params=_params(("arbitrary",)),
    )(x, g_pre, wug, wdg, g_post)


def _mlp_bwd(dxo, m, x, up, g_pre, wug, wdg, g_post, tm, name, dep):
    T, D = x.shape
    ns, _, fs = wug.shape

    def body(dxo_ref, m_ref, x_ref, up_ref, gp_ref, wu_ref, wd_ref, gq_ref, dep_ref,
             dxi_ref, h_ref, dm_ref, dup_ref, sums_ref):
        i = pl.program_id(0)
        dxo = dxo_ref[...]
        dm, dgpost = _rms_bwd(m_ref[...], gq_ref[...], dxo)
        dmb = dm.astype(BF16)
        dm_ref[...] = dmb
        xv = x_ref[...]
        h_ref[...] = _rms(xv, gp_ref[...]).astype(BF16)
        dh = jnp.zeros((tm, D), F32)
        for s in range(ns):
            dact = _dot_nt(dmb, wd_ref[s])
            up = up_ref[:, s * fs:(s + 1) * fs].astype(F32)
            dup = (dact * (2.0 * jnp.maximum(up, 0.0))).astype(BF16)
            dup_ref[:, s * fs:(s + 1) * fs] = dup
            dh = dh + _dot_nt(dup, wu_ref[s])
        dxi, dgpre = _rms_bwd(xv, gp_ref[...], dh)
        dxi_ref[...] = dxo + dxi

        @pl.when(i == 0)
        def _():
            sums_ref[...] = jnp.zeros_like(sums_ref)
        sums_ref[0:1, :] += dgpost
        sums_ref[1:2, :] += dgpre

    return _pcall(
        body, name=name, grid=(T // tm,),
        in_specs=[_rows(tm, D), _rows(tm, D), _rows(tm, D), _rows(tm, ns * fs), _full((1, D)),
                  _full(wug.shape), _full(wdg.shape), _full((1, D)), pl.BlockSpec(memory_space=pl.ANY)],
        out_specs=[_rows(tm, D), _rows(tm, D), _rows(tm, D), _rows(tm, ns * fs), _acc((SUBLANES, D))],
        out_shape=[jax.ShapeDtypeStruct((T, D), F32), jax.ShapeDtypeStruct((T, D), BF16),
                   jax.ShapeDtypeStruct((T, D), BF16), jax.ShapeDtypeStruct((T, ns * fs), BF16),
                   jax.ShapeDtypeStruct((SUBLANES, D), F32)],
        compiler_params=_params(("arbitrary",)),
    )(dxo, m, x, up, g_pre, wug, wdg, g_post, dep)


def _mm_tn(a, g, *, nj, a_cols, g_cols, a_by_j, g_by_j, tk, name, act=False):
    T = a.shape[0]
    nk = T // tk

    def body(a_ref, g_ref, o_ref, acc_ref):
        k = pl.program_id(1)
        av = a_ref[...]
        if act:
            av = jnp.square(jnp.maximum(av.astype(F32), 0.0)).astype(BF16)
        p = _dot_tn(av, g_ref[...])

        @pl.when(k == 0)
        def _():
            acc_ref[...] = p

        @pl.when(k > 0)
        def _():
            acc_ref[...] += p

        @pl.when(k == nk - 1)
        def _():
            o_ref[...] = acc_ref[...].astype(BF16)

    return _pcall(
        body, name=name, grid=(nj, nk),
        in_specs=[pl.BlockSpec((tk, a_cols), (lambda j, k: (k, j)) if a_by_j else (lambda j, k: (k, 0))),
                  pl.BlockSpec((tk, g_cols), (lambda j, k: (k, j)) if g_by_j else (lambda j, k: (k, 0)))],
        out_specs=pl.BlockSpec((None, a_cols, g_cols), lambda j, k: (j, 0, 0)),
        out_shape=jax.ShapeDtypeStruct((nj, a_cols, g_cols), BF16),
        scratch_shapes=[pltpu.VMEM((a_cols, g_cols), F32)],
        compiler_params=_params(("arbitrary", "arbitrary")),
    )(a, g)


def _attn_in_fwd(x, g_pre, wqkv, wf, bf, tm, q_mul, n_heads):
    T, D = x.shape

    def body(x_ref, g_ref, w_ref, wf_ref, bf_ref, q_ref, k_ref, v_ref, lf_ref):
        h = _rms(x_ref[...], g_ref[...]).astype(BF16)
        q = _dot(h, w_ref[:, 0:D])
        if q_mul != 1.0:
            q = q * q_mul
        q_ref[...] = q.astype(BF16)
        k_ref[...] = _dot(h, w_ref[:, D:2 * D]).astype(BF16)
        v_ref[...] = _dot(h, w_ref[:, 2 * D:3 * D]).astype(BF16)
        fl = _dot(h, wf_ref[...]) + bf_ref[...]
        lf = jnp.minimum(fl, 0.0) - jnp.log(1.0 + jnp.exp(-jnp.abs(fl)))
        lane = lax.broadcasted_iota(jnp.int32, (1, LANES), 1)
        lf_ref[...] = jnp.where(lane < n_heads, lf, 0.0)

    return _pcall(
        body, name="attn_in_fwd", grid=(T // tm,),
        in_specs=[_rows(tm, D), _full((1, D)), _full((D, 3 * D)), _full((D, LANES)), _full((1, LANES))],
        out_specs=[_rows(tm, D), _rows(tm, D), _rows(tm, D), _rows(tm, LANES)],
        out_shape=[jax.ShapeDtypeStruct((T, D), BF16)] * 3 + [jax.ShapeDtypeStruct((T, LANES), F32)],
        compiler_params=_params(("arbitrary",)),
    )(x, g_pre, wqkv, wf, bf)


def _cumsum_rows(v, v2, tb, reverse, name):
    T, C = v.shape
    nb = T // tb

    def body(v_ref, v2_ref, o_ref, carry_ref):
        i = pl.program_id(0)

        @pl.when(i == 0)
        def _():
            carry_ref[...] = jnp.zeros_like(carry_ref)
        r = lax.broadcasted_iota(jnp.int32, (tb, tb), 0)
        c = lax.broadcasted_iota(jnp.int32, (tb, tb), 1)
        tri = jnp.where((c >= r) if reverse else (c <= r), 1.0, 0.0).astype(F32)
        out = jnp.dot(tri, v_ref[...] + v2_ref[...], precision=lax.Precision.HIGHEST,
                      preferred_element_type=F32) + carry_ref[...]
        o_ref[...] = out
        carry_ref[...] = out[0:1, :] if reverse else out[tb - 1:tb, :]

    idx = (lambda i: (nb - 1 - i, 0)) if reverse else (lambda i: (i, 0))
    return _pcall(
        body, name=name, grid=(nb,),
        in_specs=[pl.BlockSpec((tb, C), idx), pl.BlockSpec((tb, C), idx)],
        out_specs=pl.BlockSpec((tb, C), idx),
        out_shape=jax.ShapeDtypeStruct((T, C), F32),
        scratch_shapes=[pltpu.VMEM((1, C), F32)],
        compiler_params=_params(("arbitrary",)),
    )(v, v2)


def _flash_fwd(q, k, v, fq_aux, fk_rows, *, dh, tq, s_mul):
    T, D = q.shape
    G = D // LANES
    hpg = LANES // dh
    nq = T // tq
    k3 = k.reshape(nq, tq, D)
    v3 = v.reshape(nq, tq, D)

    rc = min(FLASH_ROWS, tq)

    def body(q_ref, k_ref, v_ref, fq_ref, fk_ref, o_ref, o32_ref, lse_ref, s_scr, p_scr):
        i = pl.program_id(1)
        lane = lax.broadcasted_iota(jnp.int32, (1, LANES), 1)
        q2 = q_ref[...]
        hmasks = [(lane >= hh * dh) & (lane < (hh + 1) * dh) for hh in range(hpg)]
        qms = [jnp.where(hm, q2, jnp.zeros_like(q2)) for hm in hmasks]

        nlb = tq // LANES
        sum_lane = [((hh + 1) % hpg) * dh for hh in range(hpg)]

        def scores(j, slot):
            kj = k_ref[j]
            for hh in range(hpg):
                s = _dot_nt(qms[hh], kj)
                s_scr[slot, hh] = s if s_mul == 1.0 else s * s_mul

        def soft(j, slot, carry, masked):
            vj = v_ref[j]
            out = []
            for hh in range(hpg):
                m_b, acc = carry[hh]
                fk_row = fk_ref[j, hh:hh + 1, :]
                mx = []
                for r0 in range(0, tq, rc):
                    rs = slice(r0, r0 + rc)
                    s = s_scr[slot, hh, rs, :] - fk_row
                    if masked:
                        ri = r0 + lax.broadcasted_iota(jnp.int32, (rc, tq), 0)
                        ci = lax.broadcasted_iota(jnp.int32, (rc, tq), 1)
                        s = jnp.where(ci <= ri, s, MASK_VALUE)
                    s_scr[slot, hh, rs, :] = s
                    c = s[:, 0:LANES]
                    for cb in range(1, nlb):
                        c = jnp.maximum(c, s[:, cb * LANES:(cb + 1) * LANES])
                    mx.append(c)
                row_max = jnp.max(jnp.concatenate(mx, axis=0), axis=1, keepdims=True)
                m_new = jnp.maximum(m_b, row_max)
                alpha = jnp.exp(m_b - m_new)
                for r0 in range(0, tq, rc):
                    rs = slice(r0, r0 + rc)
                    m_c = m_new[rs]
                    for cb in range(nlb):
                        cs = slice(cb * LANES, (cb + 1) * LANES)
                        p_scr[hh, rs, cs] = jnp.exp(s_scr[slot, hh, rs, cs] - m_c).astype(BF16)
                v_one = jnp.where(hmasks[hh], vj, jnp.ones_like(vj))
                out.append((m_new, alpha * acc + _dot(p_scr[hh], v_one)))
            return tuple(out)

        def step(j, carry, masked):
            scores(j, 0)
            return soft(j, 0, carry, masked)

        init = tuple((jnp.full((tq, LANES), MASK_VALUE, F32), jnp.zeros((tq, LANES), F32)) for _ in range(hpg))
        carry = lax.fori_loop(0, i, lambda j, cr: step(j, cr, False), init)
        carry = step(i, carry, True)
        fq = fq_ref[...]
        o_all = jnp.zeros((tq, LANES), F32)
        lse_all = jnp.zeros((tq, LANES), F32)
        for hh in range(hpg):
            m_b, acc = carry[hh]
            l = acc[:, sum_lane[hh]:sum_lane[hh] + 1]
            o_all = jnp.where(hmasks[hh], acc * (1.0 / l), o_all)
            lse_all = jnp.where(lane == hh, m_b[:, 0:1] + jnp.log(l) + fq[:, hh:hh + 1], lse_all)
        o_ref[...] = o_all.astype(BF16)
        o32_ref[...] = o_all
        lse_ref[...] = lse_all

    return _pcall(
        body, name="flash_fwd", grid=(G, nq),
        in_specs=[pl.BlockSpec((tq, LANES), lambda g, i: (i, g)),
                  pl.BlockSpec((nq, tq, LANES), lambda g, i: (0, 0, g)),
                  pl.BlockSpec((nq, tq, LANES), lambda g, i: (0, 0, g)),
                  pl.BlockSpec((None, tq, LANES), lambda g, i: (g, i, 0)),
                  pl.BlockSpec((None, nq, SUBLANES, tq), lambda g, i: (g, 0, 0, 0))],
        out_specs=[pl.BlockSpec((tq, LANES), lambda g, i: (i, g)),
                   pl.BlockSpec((tq, LANES), lambda g, i: (i, g)),
                   pl.BlockSpec((None, tq, LANES), lambda g, i: (g, i, 0))],
        out_shape=[jax.ShapeDtypeStruct((T, D), BF16), jax.ShapeDtypeStruct((T, D), F32),
                   jax.ShapeDtypeStruct((G, T, LANES), F32)],
        scratch_shapes=[pltpu.VMEM((1, hpg, tq, tq), F32), pltpu.VMEM((hpg, tq, tq), BF16)],
        compiler_params=_params(("arbitrary", "arbitrary")),
    )(q, k3, v3, fq_aux, fk_rows)


def _flash_bwd(q, k, v, do, fq_aux, lse_aux, dl_aux, fk_rows, *, dh, tq, s_mul, dq_mul):
    T, D = q.shape
    G = D // LANES
    hpg = LANES // dh
    nq = T // tq
    k3 = k.reshape(nq, tq, D)
    v3 = v.reshape(nq, tq, D)
    rc = min(FLASH_ROWS, tq)

    def body(q_ref, k_ref, v_ref, do_ref, fq_ref, lse_ref, dl_ref, fk_ref,
             dq_ref, dk_ref, dv_ref, dfq_ref, dfk_ref, dk_acc, dv_acc, s_scr, dp_scr, p_scr, ds_scr):
        i = pl.program_id(1)

        @pl.when(i == 0)
        def _():
            dk_acc[...] = jnp.zeros_like(dk_acc)
            dv_acc[...] = jnp.zeros_like(dv_acc)
            dfk_ref[...] = jnp.zeros_like(dfk_ref)

        lane = lax.broadcasted_iota(jnp.int32, (1, LANES), 1)
        q2 = q_ref[...]
        do2 = do_ref[...]
        cq = fq_ref[...] - lse_ref[...]
        dl = dl_ref[...]
        hmasks = [(lane >= hh * dh) & (lane < (hh + 1) * dh) for hh in range(hpg)]
        qms = [jnp.where(hm, q2, jnp.zeros_like(q2)) for hm in hmasks]
        doms = [jnp.where(hm, do2, jnp.zeros_like(do2)) for hm in hmasks]
        c_bs = [jnp.broadcast_to(cq[:, hh:hh + 1], (tq, LANES)) for hh in range(hpg)]
        dl_bs = [jnp.broadcast_to(dl[:, hh:hh + 1], (tq, LANES)) for hh in range(hpg)]
        nlb = tq // LANES

        def step(j, carry, masked):
            kj = k_ref[j]
            vj = v_ref[j]
            for hh in range(hpg):
                s = _dot_nt(qms[hh], kj)
                s_scr[hh] = s if s_mul == 1.0 else s * s_mul
                dp_scr[hh] = _dot_nt(doms[hh], vj)
            out = []
            for hh in range(hpg):
                dq_acc, rs_p = carry[hh]
                fk_row = fk_ref[j, hh:hh + 1, :]
                rsums = []
                csums = [jnp.zeros((SUBLANES, LANES), F32) for _ in range(nlb)]
                for r0 in range(0, tq, rc):
                    rs = slice(r0, r0 + rc)
                    c_c = c_bs[hh][rs]
                    dl_c = dl_bs[hh][rs]
                    tot = None
                    for cb in range(nlb):
                        cs = slice(cb * LANES, (cb + 1) * LANES)
                        e = (s_scr[hh, rs, cs] - fk_row[:, cs]) + c_c
                        if masked:
                            ri = r0 + lax.broadcasted_iota(jnp.int32, (rc, LANES), 0)
                            ci = cb * LANES + lax.broadcasted_iota(jnp.int32, (rc, LANES), 1)
                            e = jnp.where(ci <= ri, e, MASK_VALUE)
                        p = jnp.exp(e)
                        ds = p * (dp_scr[hh, rs, cs] - dl_c)
                        p_scr[hh, rs, cs] = p.astype(BF16)
                        ds_scr[hh, rs, cs] = ds.astype(BF16)
                        tot = ds if tot is None else tot + ds
                        csums[cb] = csums[cb] + ds.reshape(rc // SUBLANES, SUBLANES, LANES).sum(axis=0)
                    rsums.append(tot)
                col = jnp.concatenate([jnp.sum(c, axis=0, keepdims=True) for c in csums], axis=1)
                dfk_ref[j, hh:hh + 1, :] += -col
                out.append((dq_acc + _dot(ds_scr[hh], kj), rs_p + jnp.concatenate(rsums, axis=0)))
            dk_new = _dot_tn(ds_scr[0], qms[0])
            dv_new = _dot_tn(p_scr[0], doms[0])
            for hh in range(1, hpg):
                dk_new = dk_new + _dot_tn(ds_scr[hh], qms[hh])
                dv_new = dv_new + _dot_tn(p_scr[hh], doms[hh])
            dk_acc[j] += dk_new
            dv_acc[j] += dv_new
            return tuple(out)

        init = tuple((jnp.zeros((tq, LANES), F32), jnp.zeros((tq, LANES), F32)) for _ in range(hpg))
        carry = lax.fori_loop(0, i, lambda j, cr: step(j, cr, False), init)
        carry = step(i, carry, True)
        dq_all = jnp.zeros((tq, LANES), F32)
        dfq_all = jnp.zeros((tq, LANES), F32)
        for hh in range(hpg):
            dq_h, rs_p = carry[hh]
            dq_all = jnp.where(hmasks[hh], dq_h, dq_all)
            dfq_all = jnp.where(lane == hh, jnp.sum(rs_p, axis=1, keepdims=True), dfq_all)
        dq_ref[...] = (dq_all * dq_mul).astype(BF16)
        dfq_ref[...] = dfq_all

        @pl.when(i == nq - 1)
        def _():
            dkv = dk_acc[...]
            if s_mul != 1.0:
                dkv = dkv * s_mul
            dk_ref[...] = dkv.astype(BF16)
            dv_ref[...] = dv_acc[...].astype(BF16)

    blk = pl.BlockSpec((tq, LANES), lambda g, i: (i, g))
    res = pl.BlockSpec((nq, tq, LANES), lambda g, i: (0, 0, g))
    aux = pl.BlockSpec((None, tq, LANES), lambda g, i: (g, i, 0))
    rows = pl.BlockSpec((None, nq, SUBLANES, tq), lambda g, i: (g, 0, 0, 0))
    dq, dk3, dv3, dfq, dfk = _pcall(
        body, name="flash_bwd", grid=(G, nq),
        in_specs=[blk, res, res, blk, aux, aux, aux, rows],
        out_specs=[blk, res, res, aux, rows],
        out_shape=[jax.ShapeDtypeStruct((T, D), BF16), jax.ShapeDtypeStruct((nq, tq, D), BF16),
                   jax.ShapeDtypeStruct((nq, tq, D), BF16), jax.ShapeDtypeStruct((G, T, LANES), F32),
                   jax.ShapeDtypeStruct((G, nq, SUBLANES, tq), F32)],
        scratch_shapes=[pltpu.VMEM((nq, tq, LANES), F32), pltpu.VMEM((nq, tq, LANES), F32),
                        pltpu.VMEM((hpg, tq, tq), F32), pltpu.VMEM((hpg, tq, tq), F32),
                        pltpu.VMEM((hpg, tq, tq), BF16), pltpu.VMEM((hpg, tq, tq), BF16)],
        compiler_params=_params(("arbitrary", "arbitrary")),
    )(q, k3, v3, do, fq_aux, lse_aux, dl_aux, fk_rows)
    return dq, dk3.reshape(T, D), dv3.reshape(T, D), dfq, dfk


def _attn_out_fwd(o, x, wo, g_post, tm):
    T, D = x.shape

    def body(o_ref, x_ref, w_ref, g_ref, m_ref, xo_ref):
        m = _dot(o_ref[...], w_ref[...])
        m_ref[...] = m
        xo_ref[...] = x_ref[...] + _rms(m, g_ref[...])

    return _pcall(
        body, name="attn_out_fwd", grid=(T // tm,),
        in_specs=[_rows(tm, D), _rows(tm, D), _full((D, D)), _full((1, D))],
        out_specs=[_rows(tm, D), _rows(tm, D)],
        out_shape=[jax.ShapeDtypeStruct((T, D), F32), jax.ShapeDtypeStruct((T, D), F32)],
        compiler_params=_params(("arbitrary",)),
    )(o, x, wo, g_post)


def _attn_out_bwd(dxo, m, o, wo, g_post, head_ind, tm, dep):
    T, D = m.shape

    def body(dxo_ref, m_ref, o_ref, w_ref, g_ref, ind_ref, dep_ref, dm_ref, do_ref, dl_ref, sums_ref):
        i = pl.program_id(0)
        dm, dgpost = _rms_bwd(m_ref[...], g_ref[...], dxo_ref[...])
        dmb = dm.astype(BF16)
        dm_ref[...] = dmb
        dob = _dot_nt(dmb, w_ref[...]).astype(BF16)
        do_ref[...] = dob
        dl_ref[...] = jnp.dot(dob.astype(F32) * o_ref[...], ind_ref[...], precision=lax.Precision.HIGHEST,
                              preferred_element_type=F32)

        @pl.when(i == 0)
        def _():
            sums_ref[...] = jnp.zeros_like(sums_ref)
        sums_ref[0:1, :] += dgpost

    return _pcall(
        body, name="attn_out_bwd", grid=(T // tm,),
        in_specs=[_rows(tm, D), _rows(tm, D), _rows(tm, D), _full((D, D)), _full((1, D)), _full((D, LANES)),
                  pl.BlockSpec(memory_space=pl.ANY)],
        out_specs=[_rows(tm, D), _rows(tm, D), _rows(tm, LANES), _acc((SUBLANES, D))],
        out_shape=[jax.ShapeDtypeStruct((T, D), BF16), jax.ShapeDtypeStruct((T, D), BF16),
                   jax.ShapeDtypeStruct((T, LANES), F32), jax.ShapeDtypeStruct((SUBLANES, D), F32)],
        compiler_params=_params(("arbitrary",)),
    )(dxo, m, o, wo, g_post, head_ind, dep)


def _attn_in_bwd(dxo, x, g_pre, dq, dk, dv, dlf, lf, wqkv, wf, tm, n_heads):
    T, D = x.shape

    def body(dxo_ref, x_ref, g_ref, dq_ref, dk_ref, dv_ref, dlf_ref, lf_ref, w_ref, wf_ref,
             dxi_ref, h_ref, df_ref, sums_ref, dbf_ref):
        i = pl.program_id(0)
        xv = x_ref[...]
        h_ref[...] = _rms(xv, g_ref[...]).astype(BF16)
        lane = lax.broadcasted_iota(jnp.int32, (1, LANES), 1)
        df = jnp.where(lane < n_heads, dlf_ref[...] * (1.0 - jnp.exp(lf_ref[...])), 0.0)
        dfb = df.astype(BF16)
        df_ref[...] = dfb
        dh = (_dot_nt(dq_ref[...], w_ref[:, 0:D]) + _dot_nt(dk_ref[...], w_ref[:, D:2 * D])
              + _dot_nt(dv_ref[...], w_ref[:, 2 * D:3 * D]) + _dot_nt(dfb, wf_ref[...]))
        dxi, dgpre = _rms_bwd(xv, g_ref[...], dh)
        dxi_ref[...] = dxo_ref[...] + dxi

        @pl.when(i == 0)
        def _():
            sums_ref[...] = jnp.zeros_like(sums_ref)
            dbf_ref[...] = jnp.zeros_like(dbf_ref)
        sums_ref[0:1, :] += dgpre
        dbf_ref[...] += jnp.sum(df, axis=0, keepdims=True)

    return _pcall(
        body, name="attn_in_bwd", grid=(T // tm,),
        in_specs=[_rows(tm, D), _rows(tm, D), _full((1, D)), _rows(tm, D), _rows(tm, D), _rows(tm, D),
                  _rows(tm, LANES), _rows(tm, LANES), _full((D, 3 * D)), _full((D, LANES))],
        out_specs=[_rows(tm, D), _rows(tm, D), _rows(tm, LANES), _acc((SUBLANES, D)), _acc((1, LANES))],
        out_shape=[jax.ShapeDtypeStruct((T, D), F32), jax.ShapeDtypeStruct((T, D), BF16),
                   jax.ShapeDtypeStruct((T, LANES), BF16), jax.ShapeDtypeStruct((SUBLANES, D), F32),
                   jax.ShapeDtypeStruct((1, LANES), F32)],
        compiler_params=_params(("arbitrary",)),
    )(dxo, x, g_pre, dq, dk, dv, dlf, lf, wqkv, wf)


def _loss_head(y, target, tm):
    T, D = y.shape

    def body(y_ref, t_ref, dy_ref, loss_ref):
        i = pl.program_id(0)
        err = y_ref[...] - t_ref[...]
        dy_ref[...] = err * (1.0 / D)
        part = 0.5 * jnp.sum(jnp.mean(err * err, axis=-1, keepdims=True), axis=0, keepdims=True)

        @pl.when(i == 0)
        def _():
            loss_ref[...] = jnp.zeros_like(loss_ref)
        loss_ref[...] += part

    return _pcall(
        body, name="loss_head", grid=(T // tm,),
        in_specs=[_rows(tm, D), _rows(tm, D)],
        out_specs=[_rows(tm, D), _acc((SUBLANES, LANES))],
        out_shape=[jax.ShapeDtypeStruct((T, D), F32), jax.ShapeDtypeStruct((SUBLANES, LANES), F32)],
        compiler_params=_params(("arbitrary",)),
    )(y, target)


def _adamw(recvs, w, m, v, tr, name):
    L, R, C = w.shape
    assert len(recvs) == L
    c1 = 1.0 - ADAM_B1 ** ADAM_STEP
    c2 = 1.0 - ADAM_B2 ** ADAM_STEP

    def body(*refs):
        r_refs = refs[:L]
        w_ref, m_ref, v_ref, g_ref, d_ref, nm_ref, nv_ref = refs[L:]
        layer = pl.program_id(0)
        g = None
        for k in range(L):
            gk = r_refs[k][0, :, 0:C].astype(F32)
            for s in range(1, N_DEV):
                gk = gk + r_refs[k][s, :, 0:C].astype(F32)
            g = gk if g is None else jnp.where(layer == k, gk, g)
        nm = ADAM_B1 * m_ref[...] + (1.0 - ADAM_B1) * g
        nv = ADAM_B2 * v_ref[...] + (1.0 - ADAM_B2) * jnp.square(g)
        m_hat = nm / c1
        v_hat = nv / c2
        g_ref[...] = g
        d_ref[...] = -ADAM_LR * (m_hat / (jnp.sqrt(v_hat) + ADAM_EPS) + ADAM_WD * w_ref[...])
        nm_ref[...] = nm
        nv_ref[...] = nv

    def recv_spec(k):
        return pl.BlockSpec((N_DEV, tr, recvs[k].shape[-1]), lambda l, i: (0, jnp.where(l == k, i, 0), 0))

    blk = pl.BlockSpec((None, tr, C), lambda l, i: (l, i, 0))
    return _pcall(
        body, name=name, grid=(L, R // tr),
        in_specs=[recv_spec(k) for k in range(L)] + [blk] * 3,
        out_specs=[blk] * 4,
        out_shape=[jax.ShapeDtypeStruct((L, R, C), F32)] * 4,
        compiler_params=_params(("arbitrary", "arbitrary")),
    )(*recvs, w, m, v)


def _row_block(rows, cols):
    cap = max(SUBLANES, (256 * 1024) // max(cols, 1))
    best = None
    for t in range(SUBLANES, rows + 1, SUBLANES):
        if rows % t == 0 and t <= cap:
            best = t
    return rows if best is None else best


def kernel(x, g_mix_pre, g_mix_post, g_ffn_pre, g_ffn_post, conv_pw1_w, conv_pw1_b, conv_dw_w, conv_dw_b, conv_ln_g, conv_ln_b, conv_pw2_w, conv_pw2_b, attn_w_in, attn_b_f, attn_w_o, mlp_w_up, mlp_w_down, loss_target, m_g_mix_pre, m_g_mix_post, m_g_ffn_pre, m_g_ffn_post, m_conv_pw1_w, m_conv_pw1_b, m_conv_dw_w, m_conv_dw_b, m_conv_ln_g, m_conv_ln_b, m_conv_pw2_w, m_conv_pw2_b, m_attn_w_in, m_attn_b_f, m_attn_w_o, m_mlp_w_up, m_mlp_w_down, v_g_mix_pre, v_g_mix_post, v_g_ffn_pre, v_g_ffn_post, v_conv_pw1_w, v_conv_pw1_b, v_conv_dw_w, v_conv_dw_b, v_conv_ln_g, v_conv_ln_b, v_conv_pw2_w, v_conv_pw2_b, v_attn_w_in, v_attn_b_f, v_attn_w_o, v_mlp_w_up, v_mlp_w_down):
    _, T, D = x.shape
    H = attn_b_f.shape[-1]
    dh = D // H
    width = conv_dw_w.shape[1]
    cin = attn_w_in.shape[-1]
    fs = mlp_w_up.shape[-1]
    G = D // LANES
    hpg = LANES // dh
    assert T % 4 == 0 and D % LANES == 0 and LANES % dh == 0 and width <= CONV_HALO and H <= LANES

    tm = min(512, T // 4)
    tmb = min(256, T // 4)
    tq = tm
    tqf = min(1024, T // 4)
    tmc = min(256, T // 4)
    lc = min(256, D)
    tkw = min(2048, T // 4)
    tb = min(256, T // 4)
    nq = T // tq

    scale = float(dh) ** -0.5
    mant, _ = math.frexp(scale)
    q_mul = scale if mant == 0.5 else 1.0
    s_mul = 1.0 if mant == 0.5 else scale

    x2 = x.reshape(T, D)
    tgt = loss_target.reshape(T, D)

    w_srcs = [conv_pw1_w, conv_dw_w, conv_pw2_w, mlp_w_up, mlp_w_down, attn_w_in, attn_w_o]
    w_items = [(0, 0, "whole"), (1, 0, "whole"), (2, 0, "whole"), (3, 0, "whole"), (4, 0, "whole"),
               (5, 0, "whole"), (6, 0, "whole"), (3, 1, "whole"), (4, 1, "whole")]
    cin_w = -(-cin // LANES) * LANES
    w_lands = _place_own(
        w_srcs, w_items,
        [((N_DEV, D, cin_w) if si == 5 else (N_DEV,) + w_srcs[si].shape[1:], F32 if si == 1 else BF16)
         for si, _, _ in w_items],
        "stage_weights", cast=True)
    me_arr = _dev_index(*_mesh_pos()).astype(jnp.int32).reshape(1)
    w_groups = [[0], [1, 2], [3, 4], [5, 6], [7, 8]]
    g_sems, _, w_lands, g_token = _push_start(
        [], w_lands, [[(a, None, "own") for a in grp] for grp in w_groups], "gather_start")

    def gather_wait(gi, after):
        grp = w_groups[gi]
        return _push_wait(g_sems[gi], [], [w_lands[a] for a in grp], [(k, None, "own") for k in range(len(grp))],
                          after, "gather_wait%d" % gi)

    (w1g,) = gather_wait(0, g_token)
    bf = jnp.pad(attn_b_f, ((0, 0), (0, LANES - H)))

    row = lambda a, i: a[i:i + 1]

    a0, u0 = _conv_in_fwd(x2, row(g_mix_pre, 0), w1g, conv_pw1_b, tm)
    dwg, w2g = gather_wait(1, u0)
    w2 = w2g.reshape(D, D)
    dw_full = jnp.transpose(dwg, (1, 0, 2)).reshape(width, D)
    w32 = jnp.pad(dw_full, ((0, 32 - width), (0, 0)))
    y0 = _dwconv_fwd(u0, w32, conv_dw_b, tmc, lc, width)
    m0, x_1 = _conv_out_fwd(y0, x2, conv_ln_g, conv_ln_b, w2, conv_pw2_b, row(g_mix_post, 0), tm)
    wu0, wd0 = gather_wait(2, x_1)
    up0, n0, x_2 = _mlp_fwd(x_1, row(g_ffn_pre, 0), wu0, wd0, row(g_ffn_post, 0), tm, "mlp0_fwd")

    wing, wog = gather_wait(3, x_2)
    wo = wog.reshape(D, D)
    win = jnp.transpose(wing[:, :, :cin], (1, 0, 2)).reshape(D, N_DEV * cin)
    wqkv = win[:, :3 * D]
    wf = jnp.pad(win[:, 3 * D:], ((0, 0), (0, LANES - H)))
    q, k, v, lf = _attn_in_fwd(x_2, row(g_mix_pre, 1), wqkv, wf, bf, tm, q_mul, H)
    fcum = _cumsum_rows(lf, jnp.zeros_like(lf), tb, False, "forget_cumsum")

    def to_aux(t):
        t = jnp.transpose(t[:, :H].reshape(T, G, hpg), (1, 0, 2))
        return jnp.pad(t, ((0, 0), (0, 0), (0, LANES - hpg)))

    fq_aux = to_aux(fcum)

    def key_rows(blk):
        r = jnp.transpose(fcum[:, :H].T.reshape(G, hpg, T // blk, blk), (0, 2, 1, 3))
        return jnp.pad(r, ((0, 0), (0, 0), (0, SUBLANES - hpg), (0, 0)))

    fk_rows = key_rows(tq)
    o, o32, lse_aux = _flash_fwd(q, k, v, fq_aux, key_rows(tqf), dh=dh, tq=tqf, s_mul=s_mul)
    m1, x_3 = _attn_out_fwd(o, x_2, wo, row(g_mix_post, 1), tm)
    wu1, wd1 = gather_wait(4, x_3)
    up1, n1, x_4 = _mlp_fwd(x_3, row(g_ffn_pre, 1), wu1, wd1, row(g_ffn_post, 1), tm, "mlp1_fwd")

    dx, loss_blk = _loss_head(x_4, tgt, tm)

    def mlp_back(dx, n_l, x_in, up_l, l, wu, wd, dep):
        dxi, h, dm, dup, sums = _mlp_bwd(dx, n_l, x_in, up_l, row(g_ffn_pre, l), wu, wd, row(g_ffn_post, l),
                                         tmb, "mlp%d_bwd" % l, dep)
        dwu = _mm_tn(h, dup, nj=N_DEV, a_cols=D, g_cols=fs, a_by_j=False, g_by_j=True, tk=tkw,
                     name="mlp%d_dwu" % l)
        dwd = _mm_tn(up_l, dm, nj=N_DEV, a_cols=fs, g_cols=D, a_by_j=True, g_by_j=False, tk=tkw,
                     name="mlp%d_dwd" % l, act=True)
        return dxi, dwu, dwd, sums

    def push_grads(srcs, modes, name):
        items = [(i, None, mode) for i, mode in enumerate(modes)]
        lands = _seed_lands(srcs, modes, me_arr, name + "_own")
        sems, srcs_t, lands_t, token = _push_start(srcs, lands, [items], name + "_start")
        return (sems[0], srcs_t, lands_t, items), token

    def pull_grads(handle, after, name):
        sems, srcs_t, lands_t, items = handle
        return _push_wait(sems, srcs_t, lands_t, items, after, name + "_wait")

    rs = D // N_DEV
    dx, dwu1, dwd1, s_mlp1 = mlp_back(dx, n1, x_3, up1, 1, wu1, wd1, loss_blk)
    h_mlp1, tok = push_grads([dwu1, dwd1], ["slot", "slot"], "grads_mlp1")

    head_ind = jnp.asarray((np.arange(D)[:, None] // dh == np.arange(LANES)[None, :]).astype(np.float32))
    dm1, do, delta, s_ao = _attn_out_bwd(dx, m1, o32, wo, row(g_mix_post, 1), head_ind, tm, tok)
    dwo = _mm_tn(o, dm1, nj=1, a_cols=D, g_cols=D, a_by_j=False, g_by_j=False, tk=tkw, name="attn_dwo")
    dq, dk, dv, dfq, dfk = _flash_bwd(q, k, v, do, fq_aux, lse_aux, to_aux(delta), fk_rows,
                                      dh=dh, tq=tq, s_mul=s_mul, dq_mul=scale)
    df_k = jnp.pad(jnp.transpose(dfk[:, :, :hpg, :], (0, 2, 1, 3)).reshape(H, T).T, ((0, 0), (0, LANES - H)))
    df_q = jnp.pad(jnp.transpose(dfq[:, :, :hpg], (1, 0, 2)).reshape(T, H), ((0, 0), (0, LANES - H)))
    dlf = _cumsum_rows(df_q, df_k, tb, True, "forget_cumsum_bwd")
    dx, h_at, df, s_ai, dbf = _attn_in_bwd(dx, x_2, row(g_mix_pre, 1), dq, dk, dv, dlf, lf, wqkv, wf, tm, H)
    dwq = _mm_tn(h_at, dq, nj=1, a_cols=D, g_cols=D, a_by_j=False, g_by_j=False, tk=tkw, name="attn_dwq")
    dwk = _mm_tn(h_at, dk, nj=1, a_cols=D, g_cols=D, a_by_j=False, g_by_j=False, tk=tkw, name="attn_dwk")
    dwv = _mm_tn(h_at, dv, nj=1, a_cols=D, g_cols=D, a_by_j=False, g_by_j=False, tk=tkw, name="attn_dwv")
    dwf = _mm_tn(h_at, df, nj=1, a_cols=D, g_cols=LANES, a_by_j=False, g_by_j=False, tk=tkw, name="attn_dwf")
    dwin = jnp.concatenate([dwq[0], dwk[0], dwv[0], dwf[0][:, :H]], axis=1)
    dwin = jnp.pad(jnp.transpose(dwin.reshape(D, N_DEV, cin), (1, 0, 2)), ((0, 0), (0, 0), (0, cin_w - cin)))
    h_attn, tok = push_grads([dwin, dwo.reshape(N_DEV, rs, D)], ["slot", "slot"], "grads_attn")

    dx, dwu0, dwd0, s_mlp0 = mlp_back(dx, n0, x_1, up0, 0, wu0, wd0, tok)
    h_mlp0, tok = push_grads([dwu0, dwd0], ["slot", "slot"], "grads_mlp0")

    dy0, dm0, z0, s_co = _conv_out_bwd(dx, m0, y0, conv_ln_g, conv_ln_b, w2, row(g_mix_post, 0), tm, tok)
    dw2 = _mm_tn(z0, dm0, nj=1, a_cols=D, g_cols=D, a_by_j=False, g_by_j=False, tk=tkw, name="conv_dw2")
    du0, ddw = _dwconv_bwd(dy0, u0, w32, tmc, lc, width)
    grad_x, h_cv, da0, s_ci, db1 = _conv_in_bwd(dx, du0, a0, x2, row(g_mix_pre, 0), w1g, tm)
    dw1 = _mm_tn(h_cv, da0, nj=N_DEV, a_cols=D, g_cols=(2 * D) // N_DEV, a_by_j=False, g_by_j=True, tk=tkw,
                 name="conv_dw1")
    ddw_s = jnp.transpose(ddw[:width].reshape(width, N_DEV, D // N_DEV), (1, 0, 2))

    def pad_row(a):
        return jnp.pad(a, ((0, 0), (0, D - a.shape[1])))

    def pack(gmp, gmq, gfp, gfq, b1, dwb, lng, lnb, b2, bfv, last):
        return jnp.concatenate([gmp, gmq, gfp, gfq, b1.reshape(2, D), dwb, lng, lnb, b2, pad_row(bfv), last],
                               axis=0)

    zero_row = jnp.zeros((1, D), F32)
    small_g = pack(
        jnp.concatenate([row(s_ci, 0), row(s_ai, 0)], axis=0),
        jnp.concatenate([row(s_co, 0), row(s_ao, 0)], axis=0),
        jnp.concatenate([row(s_mlp0, 1), row(s_mlp1, 1)], axis=0),
        jnp.concatenate([row(s_mlp0, 0), row(s_mlp1, 0)], axis=0),
        db1, row(s_co, 4), row(s_co, 1), row(s_co, 2), row(s_co, 3), dbf[:, :H],
        pad_row(loss_blk[0:1, 0:1]))
    h_conv, tok = push_grads([dw1, ddw_s, dw2.reshape(N_DEV, rs, D), small_g], ["slot", "slot", "slot", "whole"],
                             "grads_conv")

    def opt(recvs, w, m, v, name):
        shp = w.shape
        L, C = shp[0], shp[-1]
        R = int(np.prod(shp[1:-1]))
        outs = _adamw([r.reshape(N_DEV, R, r.shape[-1]) for r in recvs], w.reshape(L, R, C), m.reshape(L, R, C),
                      v.reshape(L, R, C), _row_block(R, C), name)
        return [t.reshape(shp) for t in outs]

    big = {}
    r_wu1, r_wd1 = pull_grads(h_mlp1, tok, "grads_mlp1")
    r_win, r_wo = pull_grads(h_attn, r_wd1, "grads_attn")
    big["attn_w_in"] = opt([r_win], attn_w_in, m_attn_w_in, v_attn_w_in, "adamw_win")
    big["attn_w_o"] = opt([r_wo], attn_w_o, m_attn_w_o, v_attn_w_o, "adamw_wo")
    r_wu0, r_wd0 = pull_grads(h_mlp0, big["attn_w_o"][0], "grads_mlp0")
    big["mlp_w_up"] = opt([r_wu0, r_wu1], mlp_w_up, m_mlp_w_up, v_mlp_w_up, "adamw_wup")
    big["mlp_w_down"] = opt([r_wd0, r_wd1], mlp_w_down, m_mlp_w_down, v_mlp_w_down, "adamw_wdown")
    r_w1, r_dw, r_w2, r_small = pull_grads(h_conv, big["mlp_w_down"][0], "grads_conv")
    big["conv_pw1_w"] = opt([r_w1], conv_pw1_w, m_conv_pw1_w, v_conv_pw1_w, "adamw_pw1")
    big["conv_dw_w"] = opt([r_dw], conv_dw_w, m_conv_dw_w, v_conv_dw_w, "adamw_dw")
    big["conv_pw2_w"] = opt([r_w2], conv_pw2_w, m_conv_pw2_w, v_conv_pw2_w, "adamw_pw2")
    small_w = pack(g_mix_pre, g_mix_post, g_ffn_pre, g_ffn_post, conv_pw1_b, conv_dw_b, conv_ln_g, conv_ln_b,
                   conv_pw2_b, attn_b_f, zero_row)
    small_m = pack(m_g_mix_pre, m_g_mix_post, m_g_ffn_pre, m_g_ffn_post, m_conv_pw1_b, m_conv_dw_b, m_conv_ln_g,
                   m_conv_ln_b, m_conv_pw2_b, m_attn_b_f, zero_row)
    small_v = pack(v_g_mix_pre, v_g_mix_post, v_g_ffn_pre, v_g_ffn_post, v_conv_pw1_b, v_conv_dw_b, v_conv_ln_g,
                   v_conv_ln_b, v_conv_pw2_b, v_attn_b_f, zero_row)
    sm = _adamw([r_small], small_w[None], small_m[None], small_v[None], small_w.shape[0], "adamw_small")
    sm = [t[0] for t in sm]
    loss = sm[0][15, 0]

    def unpack(t):
        return {"g_mix_pre": t[0:2], "g_mix_post": t[2:4], "g_ffn_pre": t[4:6], "g_ffn_post": t[6:8],
                "conv_pw1_b": t[8:10].reshape(1, 2 * D), "conv_dw_b": t[10:11], "conv_ln_g": t[11:12],
                "conv_ln_b": t[12:13], "conv_pw2_b": t[13:14], "attn_b_f": t[14:15, :H]}

    small = [unpack(t) for t in sm]
    names = ["g_mix_pre", "g_mix_post", "g_ffn_pre", "g_ffn_post", "conv_pw1_w", "conv_pw1_b", "conv_dw_w",
             "conv_dw_b", "conv_ln_g", "conv_ln_b", "conv_pw2_w", "conv_pw2_b", "attn_w_in", "attn_b_f",
             "attn_w_o", "mlp_w_up", "mlp_w_down"]
    outs = [loss, grad_x.reshape(1, T, D)]
    for kind in range(4):
        for nme in names:
            outs.append(big[nme][kind] if nme in big else small[kind][nme])
    return tuple(outs)
```

```python
import functools
import math

import numpy as np
import jax
import jax.numpy as jnp
from jax import lax
from jax.experimental import pallas as pl
from jax.experimental.pallas import tpu as pltpu

F32 = jnp.float32
BF16 = jnp.bfloat16

RMS_EPS = 1e-6
LN_EPS = 1e-5
MASK_VALUE = -1e30
ADAM_LR = 0.001
ADAM_B1 = 0.9
ADAM_B2 = 0.999
ADAM_EPS = 1e-08
ADAM_WD = 0.01
ADAM_STEP = 10

N_DEV = 8
LANES = 128
SUBLANES = 8
CONV_HALO = 32
CONV_ROWS = 32
FLASH_ROWS = 32
VMEM_LIMIT = 56 * 1024 * 1024

_pcall = pl.pallas_call


def _params(sem=None):
    if sem is None:
        return pltpu.CompilerParams(vmem_limit_bytes=VMEM_LIMIT)
    return pltpu.CompilerParams(dimension_semantics=sem, vmem_limit_bytes=VMEM_LIMIT)


def _dot(a, b):
    return jnp.dot(a, b, preferred_element_type=F32)


def _dot_nt(a, b):
    return lax.dot_general(a, b, (((1,), (1,)), ((), ())), preferred_element_type=F32)


def _dot_tn(a, b):
    return lax.dot_general(a, b, (((0,), (0,)), ((), ())), preferred_element_type=F32)


def _full(shape):
    nd = len(shape)
    return pl.BlockSpec(shape, lambda *g: (0,) * nd, pipeline_mode=pl.Buffered(1))


def _acc(shape):
    nd = len(shape)
    return pl.BlockSpec(shape, lambda *g: (0,) * nd)


def _rows(tm, cols):
    return pl.BlockSpec((tm, cols), lambda i: (i, 0))


def _rms(x, g):
    r = lax.rsqrt(jnp.mean(x * x, axis=-1, keepdims=True) + RMS_EPS)
    return x * r * g


def _rms_bwd(x, g, dy):
    r = lax.rsqrt(jnp.mean(x * x, axis=-1, keepdims=True) + RMS_EPS)
    n = x * r
    dg = jnp.sum(dy * n, axis=0, keepdims=True)
    dn = dy * g
    dx = r * (dn - n * jnp.mean(dn * n, axis=-1, keepdims=True))
    return dx, dg


def _sigmoid(x):
    return 1.0 / (1.0 + jnp.exp(-x))


def _mesh_pos():
    return lax.axis_index("x"), lax.axis_index("y"), lax.axis_index("c")


def _dev_index(px, py, pc):
    return 4 * px + 2 * py + pc


_HBM = pl.BlockSpec(memory_space=pltpu.HBM)
_SEM = pl.BlockSpec(memory_space=pltpu.SEMAPHORE)
_EFFECT = pltpu.SideEffectType.DATAFLOW_SIDE_EFFECTING


def _peer(r, x, y, c):
    p = ((1 - x) if r & 4 else x, (1 - y) if r & 2 else y, (1 - c) if r & 1 else c)
    return p, _dev_index(*p)


def _src_ref(refs, item, me_id=None, to_id=None):
    si, sub, mode = item
    r = refs[si] if sub is None else refs[si].at[sub]
    if mode == "slot":
        return r.at[to_id]
    if mode == "own":
        return r.at[me_id]
    return r


def _place_own(srcs, items, land_shapes, name, cast=False):
    ns, n = len(srcs), len(items)

    def body(*refs):
        src = refs[:ns]
        land = refs[ns:ns + n]
        stage = refs[ns + n:ns + 2 * n] if cast else None
        sems = refs[-1]
        me_id = _dev_index(*_mesh_pos())
        cps = []
        for a, item in enumerate(items):
            s = _src_ref(src, item, to_id=me_id)
            if cast:
                if s.shape != stage[a].shape:
                    stage[a][...] = jnp.zeros_like(stage[a])
                    stage[a][:, 0:s.shape[-1]] = s[...].astype(stage[a].dtype)
                else:
                    stage[a][...] = s[...].astype(stage[a].dtype)
                s = stage[a]
            cp = pltpu.make_async_copy(s, land[a].at[me_id], sems.at[a])
            cp.start()
            cps.append(cp)
        for cp in cps:
            cp.wait()

    return _pcall(
        body, name=name,
        out_shape=[jax.ShapeDtypeStruct(tuple(s), d) for s, d in land_shapes],
        in_specs=[pl.BlockSpec(memory_space=pltpu.VMEM if cast else pl.ANY)] * ns,
        out_specs=[pl.BlockSpec(memory_space=pl.ANY)] * n,
        scratch_shapes=([pltpu.VMEM(tuple(s[1:]), d) for s, d in land_shapes] if cast else [])
        + [pltpu.SemaphoreType.DMA((n,))],
        compiler_params=pltpu.CompilerParams(vmem_limit_bytes=VMEM_LIMIT),
    )(*srcs)


def _seed_lands(srcs, modes, me, name):
    n = len(srcs)
    parts = [tuple(s.shape[1:]) if mode == "slot" else tuple(s.shape) for s, mode in zip(srcs, modes)]

    def body(me_ref, *refs):
        s = pl.program_id(0)
        for a in range(n):
            v = refs[a][...]
            refs[n + a][...] = jnp.where(s == me_ref[0], v, jnp.zeros_like(v))

    def in_spec(part, mode):
        nd = len(part)
        if mode == "slot":
            return pl.BlockSpec((None,) + part, lambda s, me_ref: (me_ref[0],) + (0,) * nd)
        return pl.BlockSpec(part, lambda s, me_ref: (0,) * nd)

    def out_spec(part):
        nd = len(part)
        return pl.BlockSpec((None,) + part, lambda s, me_ref: (s,) + (0,) * nd)

    return _pcall(
        body, name=name,
        grid_spec=pltpu.PrefetchScalarGridSpec(
            num_scalar_prefetch=1, grid=(N_DEV,),
            in_specs=[in_spec(p, m) for p, m in zip(parts, modes)],
            out_specs=[out_spec(p) for p in parts]),
        out_shape=[jax.ShapeDtypeStruct((N_DEV,) + p, s.dtype) for p, s in zip(parts, srcs)],
        compiler_params=_params(("arbitrary",)),
    )(me, *srcs)


def _push_start(srcs, lands, groups, name):
    ns, n = len(srcs), len(lands)
    ng = len(groups)
    assert sum(len(g) for g in groups) == n

    def body(*refs):
        ops = refs[:ns + n]
        land = refs[ns:ns + n]
        sems = refs[ns + n:ns + n + 2 * ng]
        token = refs[-1]
        x, y, c = _mesh_pos()
        me_id = _dev_index(x, y, c)
        a = 0
        for gi, grp in enumerate(groups):
            for k, item in enumerate(grp):
                for r in range(1, N_DEV):
                    p, pid = _peer(r, x, y, c)
                    pltpu.make_async_remote_copy(
                        src_ref=_src_ref(ops, item, me_id=me_id, to_id=pid), dst_ref=land[a].at[me_id],
                        send_sem=sems[2 * gi].at[k * (N_DEV - 1) + r - 1],
                        recv_sem=sems[2 * gi + 1].at[k * (N_DEV - 1) + r - 1],
                        device_id=p, device_id_type=pl.DeviceIdType.MESH).start()
                a += 1
        token[...] = jnp.zeros_like(token)

    sem_shapes = []
    for grp in groups:
        sem_shapes += [pltpu.SemaphoreType.DMA((len(grp) * (N_DEV - 1),))] * 2
    arrs = list(srcs) + list(lands)
    res = _pcall(
        body, name=name,
        out_shape=tuple(sem_shapes) + tuple(pltpu.HBM(a.shape, a.dtype) for a in arrs)
        + (jax.ShapeDtypeStruct((SUBLANES, LANES), F32),),
        in_specs=[_HBM] * (ns + n),
        out_specs=tuple([_SEM] * (2 * ng)) + tuple([_HBM] * (ns + n)) + (pl.BlockSpec(memory_space=pltpu.VMEM),),
        input_output_aliases={i: 2 * ng + i for i in range(ns + n)},
        compiler_params=pltpu.CompilerParams(has_side_effects=_EFFECT),
    )(*[pltpu.with_memory_space_constraint(a, pltpu.HBM) for a in arrs])
    sems = [(res[2 * gi], res[2 * gi + 1]) for gi in range(ng)]
    thru = res[2 * ng:2 * ng + ns + n]
    return sems, list(thru[:ns]), list(thru[ns:]), res[-1]


def _push_wait(sems, srcs, lands, group, after, name):
    ns, n = len(srcs), len(lands)
    assert len(group) == n

    def body(*refs):
        ops = refs[:ns + n]
        land = refs[ns:ns + n]
        send_sems, recv_sems = refs[ns + n], refs[ns + n + 1]
        x, y, c = _mesh_pos()
        me_id = _dev_index(x, y, c)
        for k, item in enumerate(group):
            for r in range(1, N_DEV):
                p, pid = _peer(r, x, y, c)
                cp = pltpu.make_async_remote_copy(
                    src_ref=_src_ref(ops, item, me_id=me_id, to_id=pid), dst_ref=land[k].at[pid],
                    send_sem=send_sems.at[k * (N_DEV - 1) + r - 1], recv_sem=recv_sems.at[k * (N_DEV - 1) + r - 1],
                    device_id=p, device_id_type=pl.DeviceIdType.MESH)
                cp.wait_send()
                cp.wait_recv()

    arrs = list(srcs) + list(lands)
    res = _pcall(
        body, name=name,
        out_shape=tuple(pltpu.HBM(a.shape, a.dtype) for a in arrs),
        in_specs=[_HBM] * (ns + n) + [_SEM, _SEM, pl.BlockSpec(memory_space=pl.ANY)],
        out_specs=tuple([_HBM] * (ns + n)),
        input_output_aliases={i: i for i in range(ns + n)},
        compiler_params=pltpu.CompilerParams(has_side_effects=_EFFECT),
    )(*arrs, sems[0], sems[1], after)
    return list(res[ns:])


def _conv_in_fwd(x, g_pre, w1g, b1, tm):
    T, D = x.shape
    ns, _, cs = w1g.shape
    half = ns // 2

    def body(x_ref, g_ref, w_ref, b_ref, a_ref, u_ref):
        h = _rms(x_ref[...], g_ref[...]).astype(BF16)
        parts = []
        for s in range(ns):
            a_s = _dot(h, w_ref[s]) + b_ref[:, s * cs:(s + 1) * cs]
            a_ref[:, s * cs:(s + 1) * cs] = a_s
            parts.append(a_s)
        for s in range(half):
            u_ref[:, s * cs:(s + 1) * cs] = parts[s] * _sigmoid(parts[s + half])

    return _pcall(
        body, name="conv_in_fwd", grid=(T // tm,),
        in_specs=[_rows(tm, D), _full((1, D)), _full(w1g.shape), _full((1, 2 * D))],
        out_specs=[_rows(tm, 2 * D), _rows(tm, D)],
        out_shape=[jax.ShapeDtypeStruct((T, 2 * D), F32), jax.ShapeDtypeStruct((T, D), F32)],
        compiler_params=_params(("arbitrary",)),
    )(x, g_pre, w1g, b1)


def _shifted_copies(ext_ref, sh_ref, tm):
    n = tm + CONV_HALO - SUBLANES
    for b in range(1, SUBLANES):
        sh_ref[b - 1, 0:n, :] = ext_ref[b:b + n, :]


def _shifted_rows(ext_ref, sh_ref, off, r0, ls):
    b = off % SUBLANES
    a8 = off - b + r0
    src = ext_ref if b == 0 else sh_ref.at[b - 1]
    return src[a8:a8 + CONV_ROWS, ls]


def _dwconv_fwd(u, w32, b, tm, lc, width):
    T, D = u.shape
    hb = tm // CONV_HALO

    def body(u_ref, halo_ref, w_ref, b_ref, y_ref, ext_ref, sh_ref):
        i = pl.program_id(0)
        ext_ref[0:CONV_HALO, :] = jnp.where(i > 0, halo_ref[...], 0.0)
        ext_ref[CONV_HALO:, :] = u_ref[...]
        _shifted_copies(ext_ref, sh_ref, tm)
        for r0 in range(0, tm, CONV_ROWS):
            for l0 in range(0, lc, LANES):
                ls = slice(l0, l0 + LANES)
                acc = jnp.zeros((CONV_ROWS, LANES), F32) + b_ref[:, ls]
                for j in range(width):
                    off = CONV_HALO - (width - 1) + j
                    acc = acc + w_ref[j:j + 1, ls] * _shifted_rows(ext_ref, sh_ref, off, r0, ls)
                y_ref[r0:r0 + CONV_ROWS, ls] = acc

    return _pcall(
        body, name="dwconv_fwd", grid=(T // tm, D // lc),
        in_specs=[pl.BlockSpec((tm, lc), lambda i, l: (i, l)),
                  pl.BlockSpec((CONV_HALO, lc), lambda i, l: (jnp.maximum(i * hb - 1, 0), l)),
                  pl.BlockSpec((32, lc), lambda i, l: (0, l)),
                  pl.BlockSpec((1, lc), lambda i, l: (0, l))],
        out_specs=pl.BlockSpec((tm, lc), lambda i, l: (i, l)),
        out_shape=jax.ShapeDtypeStruct((T, D), F32),
        scratch_shapes=[pltpu.VMEM((tm + CONV_HALO, lc), F32),
                        pltpu.VMEM((SUBLANES - 1, tm + CONV_HALO, lc), F32)],
        compiler_params=_params(("arbitrary", "arbitrary")),
    )(u, u, w32, b)


def _ln_parts(y, g, b):
    mu = jnp.mean(y, axis=-1, keepdims=True)
    yc = y - mu
    rstd = lax.rsqrt(jnp.mean(yc * yc, axis=-1, keepdims=True) + LN_EPS)
    yhat = yc * rstd
    return yhat, rstd, yhat * g + b


def _conv_out_fwd(y, x, ln_g, ln_b, w2, b2, g_post, tm):
    T, D = x.shape

    def body(y_ref, x_ref, lg_ref, lb_ref, w_ref, b_ref, g_ref, m_ref, xo_ref):
        _, _, yn = _ln_parts(y_ref[...], lg_ref[...], lb_ref[...])
        z = (yn * _sigmoid(yn)).astype(BF16)
        m = _dot(z, w_ref[...]) + b_ref[...]
        m_ref[...] = m
        xo_ref[...] = x_ref[...] + _rms(m, g_ref[...])

    return _pcall(
        body, name="conv_out_fwd", grid=(T // tm,),
        in_specs=[_rows(tm, D), _rows(tm, D), _full((1, D)), _full((1, D)), _full((D, D)), _full((1, D)),
                  _full((1, D))],
        out_specs=[_rows(tm, D), _rows(tm, D)],
        out_shape=[jax.ShapeDtypeStruct((T, D), F32), jax.ShapeDtypeStruct((T, D), F32)],
        compiler_params=_params(("arbitrary",)),
    )(y, x, ln_g, ln_b, w2, b2, g_post)


def _conv_out_bwd(dxo, m, y, ln_g, ln_b, w2, g_post, tm, dep):
    T, D = m.shape

    def body(dxo_ref, m_ref, y_ref, lg_ref, lb_ref, w_ref, g_ref, dep_ref, dy_ref, dm_ref, z_ref, sums_ref):
        i = pl.program_id(0)
        dm, dgpost = _rms_bwd(m_ref[...], g_ref[...], dxo_ref[...])
        dmb = dm.astype(BF16)
        dm_ref[...] = dmb
        yhat, rstd, yn = _ln_parts(y_ref[...], lg_ref[...], lb_ref[...])
        sg = _sigmoid(yn)
        z_ref[...] = (yn * sg).astype(BF16)
        dz = _dot_nt(dmb, w_ref[...])
        dyn = dz * (sg + yn * sg * (1.0 - sg))
        dyh = dyn * lg_ref[...]
        dy = rstd * (dyh - jnp.mean(dyh, axis=-1, keepdims=True)
                     - yhat * jnp.mean(dyh * yhat, axis=-1, keepdims=True))
        dy_ref[...] = dy

        @pl.when(i == 0)
        def _():
            sums_ref[...] = jnp.zeros_like(sums_ref)
        sums_ref[0:1, :] += dgpost
        sums_ref[1:2, :] += jnp.sum(dyn * yhat, axis=0, keepdims=True)
        sums_ref[2:3, :] += jnp.sum(dyn, axis=0, keepdims=True)
        sums_ref[3:4, :] += jnp.sum(dm, axis=0, keepdims=True)
        sums_ref[4:5, :] += jnp.sum(dy, axis=0, keepdims=True)

    return _pcall(
        body, name="conv_out_bwd", grid=(T // tm,),
        in_specs=[_rows(tm, D), _rows(tm, D), _rows(tm, D), _full((1, D)), _full((1, D)), _full((D, D)),
                  _full((1, D)), pl.BlockSpec(memory_space=pl.ANY)],
        out_specs=[_rows(tm, D), _rows(tm, D), _rows(tm, D), _acc((SUBLANES, D))],
        out_shape=[jax.ShapeDtypeStruct((T, D), F32), jax.ShapeDtypeStruct((T, D), BF16),
                   jax.ShapeDtypeStruct((T, D), BF16), jax.ShapeDtypeStruct((SUBLANES, D), F32)],
        compiler_params=_params(("arbitrary",)),
    )(dxo, m, y, ln_g, ln_b, w2, g_post, dep)


def _dwconv_bwd(dy, u, w32, tm, lc, width):
    T, D = u.shape
    hb = tm // CONV_HALO
    nt = T // tm
    last_halo = T // CONV_HALO - 1

    def body(dy_ref, dyn_ref, u_ref, up_ref, w_ref, du_ref, dw_ref, exty_ref, extu_ref, acc_ref, shy_ref, shu_ref):
        i = pl.program_id(1)
        exty_ref[0:tm, :] = dy_ref[...]
        exty_ref[tm:, :] = jnp.where(i < nt - 1, dyn_ref[...], 0.0)
        extu_ref[0:CONV_HALO, :] = jnp.where(i > 0, up_ref[...], 0.0)
        extu_ref[CONV_HALO:, :] = u_ref[...]
        _shifted_copies(exty_ref, shy_ref, tm)
        _shifted_copies(extu_ref, shu_ref, tm)

        @pl.when(i == 0)
        def _():
            acc_ref[...] = jnp.zeros_like(acc_ref)

        for r0 in range(0, tm, CONV_ROWS):
            for l0 in range(0, lc, LANES):
                ls = slice(l0, l0 + LANES)
                dyc = exty_ref[r0:r0 + CONV_ROWS, ls]
                du = jnp.zeros((CONV_ROWS, LANES), F32)
                for j in range(width):
                    du = du + w_ref[j:j + 1, ls] * _shifted_rows(exty_ref, shy_ref, (width - 1) - j, r0, ls)
                    prod = dyc * _shifted_rows(extu_ref, shu_ref, CONV_HALO - (width - 1) + j, r0, ls)
                    acc_ref[j, :, ls] += prod.reshape(CONV_ROWS // SUBLANES, SUBLANES, LANES).sum(axis=0)
                du_ref[r0:r0 + CONV_ROWS, ls] = du

        @pl.when(i == nt - 1)
        def _():
            for j in range(32):
                dw_ref[j:j + 1, :] = jnp.sum(acc_ref[j], axis=0, keepdims=True)

    return _pcall(
        body, name="dwconv_bwd", grid=(D // lc, nt),
        in_specs=[pl.BlockSpec((tm, lc), lambda l, i: (i, l)),
                  pl.BlockSpec((CONV_HALO, lc), lambda l, i: (jnp.minimum((i + 1) * hb, last_halo), l)),
                  pl.BlockSpec((tm, lc), lambda l, i: (i, l)),
                  pl.BlockSpec((CONV_HALO, lc), lambda l, i: (jnp.maximum(i * hb - 1, 0), l)),
                  pl.BlockSpec((32, lc), lambda l, i: (0, l))],
        out_specs=[pl.BlockSpec((tm, lc), lambda l, i: (i, l)),
                   pl.BlockSpec((32, lc), lambda l, i: (0, l))],
        out_shape=[jax.ShapeDtypeStruct((T, D), F32), jax.ShapeDtypeStruct((32, D), F32)],
        scratch_shapes=[pltpu.VMEM((tm + CONV_HALO, lc), F32), pltpu.VMEM((tm + CONV_HALO, lc), F32),
                        pltpu.VMEM((32, SUBLANES, lc), F32),
                        pltpu.VMEM((SUBLANES - 1, tm + CONV_HALO, lc), F32),
                        pltpu.VMEM((SUBLANES - 1, tm + CONV_HALO, lc), F32)],
        compiler_params=_params(("arbitrary", "arbitrary")),
    )(dy, dy, u, u, w32)


def _conv_in_bwd(dxo, du, a, x, g_pre, w1g, tm):
    T, D = x.shape
    ns, _, cs = w1g.shape
    half = ns // 2

    def body(dxo_ref, du_ref, a_ref, x_ref, g_ref, w_ref, dxi_ref, h_ref, da_ref, sums_ref, db_ref):
        i = pl.program_id(0)
        xv = x_ref[...]
        h_ref[...] = _rms(xv, g_ref[...]).astype(BF16)
        dh = jnp.zeros((tm, D), F32)
        dbs = [None] * ns
        for s in range(half):
            a_u = a_ref[:, s * cs:(s + 1) * cs]
            sg = _sigmoid(a_ref[:, (s + half) * cs:(s + half + 1) * cs])
            du_s = du_ref[:, s * cs:(s + 1) * cs]
            da_u = du_s * sg
            da_g = du_s * a_u * sg * (1.0 - sg)
            for s2, v in ((s, da_u), (s + half, da_g)):
                vb = v.astype(BF16)
                da_ref[:, s2 * cs:(s2 + 1) * cs] = vb
                dbs[s2] = jnp.sum(v, axis=0, keepdims=True)
                dh = dh + _dot_nt(vb, w_ref[s2])
        dxi, dgpre = _rms_bwd(xv, g_ref[...], dh)
        dxi_ref[...] = dxo_ref[...] + dxi

        @pl.when(i == 0)
        def _():
            sums_ref[...] = jnp.zeros_like(sums_ref)
            db_ref[...] = jnp.zeros_like(db_ref)
        sums_ref[0:1, :] += dgpre
        for s in range(ns):
            db_ref[:, s * cs:(s + 1) * cs] += dbs[s]

    return _pcall(
        body, name="conv_in_bwd", grid=(T // tm,),
        in_specs=[_rows(tm, D), _rows(tm, D), _rows(tm, 2 * D), _rows(tm, D), _full((1, D)),
                  _full(w1g.shape)],
        out_specs=[_rows(tm, D), _rows(tm, D), _rows(tm, 2 * D), _acc((SUBLANES, D)), _acc((1, 2 * D))],
        out_shape=[jax.ShapeDtypeStruct((T, D), F32), jax.ShapeDtypeStruct((T, D), BF16),
                   jax.ShapeDtypeStruct((T, 2 * D), BF16), jax.ShapeDtypeStruct((SUBLANES, D), F32),
                   jax.ShapeDtypeStruct((1, 2 * D), F32)],
        compiler_params=_params(("arbitrary",)),
    )(dxo, du, a, x, g_pre, w1g)


def _mlp_fwd(x, g_pre, wug, wdg, g_post, tm, name):
    T, D = x.shape
    ns, _, fs = wug.shape

    def body(x_ref, gp_ref, wu_ref, wd_ref, gq_ref, up_ref, m_ref, xo_ref):
        xv = x_ref[...]
        h = _rms(xv, gp_ref[...]).astype(BF16)
        acc = jnp.zeros((tm, D), F32)
        for s in range(ns):
            up = _dot(h, wu_ref[s]).astype(BF16)
            up_ref[:, s * fs:(s + 1) * fs] = up
            act = jnp.square(jnp.maximum(up.astype(F32), 0.0)).astype(BF16)
            acc = acc + _dot(act, wd_ref[s])
        m_ref[...] = acc
        xo_ref[...] = xv + _rms(acc, gq_ref[...])

    return _pcall(
        body, name=name, grid=(T // tm,),
        in_specs=[_rows(tm, D), _full((1, D)), _full(wug.shape), _full(wdg.shape), _full((1, D))],
        out_specs=[_rows(tm, ns * fs), _rows(tm, D), _rows(tm, D)],
        out_shape=[jax.ShapeDtypeStruct((T, ns * fs), BF16), jax.ShapeDtypeStruct((T, D), F32),
                   jax.ShapeDtypeStruct((T, D), F32)],
        compiler_params=_params(("arbitrary",)),
    )(x, g_pre, wug, wdg, g_post)


def _mlp_bwd(dxo, m, x, up, g_pre, wug, wdg, g_post, tm, name, dep):
    T, D = x.shape
    ns, _, fs = wug.shape

    def body(dxo_ref, m_ref, x_ref, up_ref, gp_ref, wu_ref, wd_ref, gq_ref, dep_ref,
             dxi_ref, h_ref, dm_ref, dup_ref, sums_ref):
        i = pl.program_id(0)
        dxo = dxo_ref[...]
        dm, dgpost = _rms_bwd(m_ref[...], gq_ref[...], dxo)
        dmb = dm.astype(BF16)
        dm_ref[...] = dmb
        xv = x_ref[...]
        h_ref[...] = _rms(xv, gp_ref[...]).astype(BF16)
        dh = jnp.zeros((tm, D), F32)
        for s in range(ns):
            dact = _dot_nt(dmb, wd_ref[s])
            up = up_ref[:, s * fs:(s + 1) * fs].astype(F32)
            dup = (dact * (2.0 * jnp.maximum(up, 0.0))).astype(BF16)
            dup_ref[:, s * fs:(s + 1) * fs] = dup
            dh = dh + _dot_nt(dup, wu_ref[s])
        dxi, dgpre = _rms_bwd(xv, gp_ref[...], dh)
        dxi_ref[...] = dxo + dxi

        @pl.when(i == 0)
        def _():
            sums_ref[...] = jnp.zeros_like(sums_ref)
        sums_ref[0:1, :] += dgpost
        sums_ref[1:2, :] += dgpre

    return _pcall(
        body, name=name, grid=(T // tm,),
        in_specs=[_rows(tm, D), _rows(tm, D), _rows(tm, D), _rows(tm, ns * fs), _full((1, D)),
                  _full(wug.shape), _full(wdg.shape), _full((1, D)), pl.BlockSpec(memory_space=pl.ANY)],
        out_specs=[_rows(tm, D), _rows(tm, D), _rows(tm, D), _rows(tm, ns * fs), _acc((SUBLANES, D))],
        out_shape=[jax.ShapeDtypeStruct((T, D), F32), jax.ShapeDtypeStruct((T, D), BF16),
                   jax.ShapeDtypeStruct((T, D), BF16), jax.ShapeDtypeStruct((T, ns * fs), BF16),
                   jax.ShapeDtypeStruct((SUBLANES, D), F32)],
        compiler_params=_params(("arbitrary",)),
    )(dxo, m, x, up, g_pre, wug, wdg, g_post, dep)


def _mm_tn(a, g, *, nj, a_cols, g_cols, a_by_j, g_by_j, tk, name, act=False):
    T = a.shape[0]
    nk = T // tk

    def body(a_ref, g_ref, o_ref, acc_ref):
        k = pl.program_id(1)
        av = a_ref[...]
        if act:
            av = jnp.square(jnp.maximum(av.astype(F32), 0.0)).astype(BF16)
        p = _dot_tn(av, g_ref[...])

        @pl.when(k == 0)
        def _():
            acc_ref[...] = p

        @pl.when(k > 0)
        def _():
            acc_ref[...] += p

        @pl.when(k == nk - 1)
        def _():
            o_ref[...] = acc_ref[...].astype(BF16)

    return _pcall(
        body, name=name, grid=(nj, nk),
        in_specs=[pl.BlockSpec((tk, a_cols), (lambda j, k: (k, j)) if a_by_j else (lambda j, k: (k, 0))),
                  pl.BlockSpec((tk, g_cols), (lambda j, k: (k, j)) if g_by_j else (lambda j, k: (k, 0)))],
        out_specs=pl.BlockSpec((None, a_cols, g_cols), lambda j, k: (j, 0, 0)),
        out_shape=jax.ShapeDtypeStruct((nj, a_cols, g_cols), BF16),
        scratch_shapes=[pltpu.VMEM((a_cols, g_cols), F32)],
        compiler_params=_params(("arbitrary", "arbitrary")),
    )(a, g)


def _attn_in_fwd(x, g_pre, wqkv, wf, bf, tm, q_mul, n_heads):
    T, D = x.shape

    def body(x_ref, g_ref, w_ref, wf_ref, bf_ref, q_ref, k_ref, v_ref, lf_ref):
        h = _rms(x_ref[...], g_ref[...]).astype(BF16)
        q = _dot(h, w_ref[:, 0:D])
        if q_mul != 1.0:
            q = q * q_mul
        q_ref[...] = q.astype(BF16)
        k_ref[...] = _dot(h, w_ref[:, D:2 * D]).astype(BF16)
        v_ref[...] = _dot(h, w_ref[:, 2 * D:3 * D]).astype(BF16)
        fl = _dot(h, wf_ref[...]) + bf_ref[...]
        lf = jnp.minimum(fl, 0.0) - jnp.log(1.0 + jnp.exp(-jnp.abs(fl)))
        lane = lax.broadcasted_iota(jnp.int32, (1, LANES), 1)
        lf_ref[...] = jnp.where(lane < n_heads, lf, 0.0)

    return _pcall(
        body, name="attn_in_fwd", grid=(T // tm,),
        in_specs=[_rows(tm, D), _full((1, D)), _full((D, 3 * D)), _full((D, LANES)), _full((1, LANES))],
        out_specs=[_rows(tm, D), _rows(tm, D), _rows(tm, D), _rows(tm, LANES)],
        out_shape=[jax.ShapeDtypeStruct((T, D), BF16)] * 3 + [jax.ShapeDtypeStruct((T, LANES), F32)],
        compiler_params=_params(("arbitrary",)),
    )(x, g_pre, wqkv, wf, bf)


def _cumsum_rows(v, v2, tb, reverse, name):
    T, C = v.shape
    nb = T // tb

    def body(v_ref, v2_ref, o_ref, carry_ref):
        i = pl.program_id(0)

        @pl.when(i == 0)
        def _():
            carry_ref[...] = jnp.zeros_like(carry_ref)
        r = lax.broadcasted_iota(jnp.int32, (tb, tb), 0)
        c = lax.broadcasted_iota(jnp.int32, (tb, tb), 1)
        tri = jnp.where((c >= r) if reverse else (c <= r), 1.0, 0.0).astype(F32)
        out = jnp.dot(tri, v_ref[...] + v2_ref[...], precision=lax.Precision.HIGHEST,
                      preferred_element_type=F32) + carry_ref[...]
        o_ref[...] = out
        carry_ref[...] = out[0:1, :] if reverse else out[tb - 1:tb, :]

    idx = (lambda i: (nb - 1 - i, 0)) if reverse else (lambda i: (i, 0))
    return _pcall(
        body, name=name, grid=(nb,),
        in_specs=[pl.BlockSpec((tb, C), idx), pl.BlockSpec((tb, C), idx)],
        out_specs=pl.BlockSpec((tb, C), idx),
        out_shape=jax.ShapeDtypeStruct((T, C), F32),
        scratch_shapes=[pltpu.VMEM((1, C), F32)],
        compiler_params=_params(("arbitrary",)),
    )(v, v2)


def _head_col(v, lane, h):
    return jnp.sum(jnp.where(lane == h, v, 0.0), axis=1, keepdims=True)


def _flash_fwd(q, k, v, fq_aux, fk_rows, *, dh, tq, s_mul):
    T, D = q.shape
    G = D // LANES
    hpg = LANES // dh
    nq = T // tq
    k3 = k.reshape(nq, tq, D)
    v3 = v.reshape(nq, tq, D)

    rc = min(FLASH_ROWS, tq)

    def body(q_ref, k_ref, v_ref, fq_ref, fk_ref, o_ref, o32_ref, lse_ref, s_scr, p_scr):
        i = pl.program_id(1)
        lane = lax.broadcasted_iota(jnp.int32, (1, LANES), 1)
        q2 = q_ref[...]
        hmasks = [(lane >= hh * dh) & (lane < (hh + 1) * dh) for hh in range(hpg)]
        qms = [jnp.where(hm, q2, jnp.zeros_like(q2)) for hm in hmasks]

        nlb = tq // LANES
        sum_lane = [((hh + 1) % hpg) * dh for hh in range(hpg)]

        def scores(j, slot):
            kj = k_ref[j]
            for hh in range(hpg):
                s = _dot_nt(qms[hh], kj)
                s_scr[slot, hh] = s if s_mul == 1.0 else s * s_mul

        def soft(j, slot, carry, masked):
            vj = v_ref[j]
            out = []
            for hh in range(hpg):
                m_b, acc = carry[hh]
                fk_row = fk_ref[j, hh:hh + 1, :]
                def future(r0, cb):
                    return masked and cb * LANES > r0 + rc - 1

                mx = []
                for r0 in range(0, tq, rc):
                    rs = slice(r0, r0 + rc)
                    c = None
                    for cb in range(nlb):
                        if future(r0, cb):
                            continue
                        cs = slice(cb * LANES, (cb + 1) * LANES)
                        s = s_scr[slot, hh, rs, cs] - fk_row[:, cs]
                        if masked and (cb + 1) * LANES - 1 > r0:
                            ri = r0 + lax.broadcasted_iota(jnp.int32, (rc, LANES), 0)
                            ci = cb * LANES + lax.broadcasted_iota(jnp.int32, (rc, LANES), 1)
                            s = jnp.where(ci <= ri, s, MASK_VALUE)
                        s_scr[slot, hh, rs, cs] = s
                        c = s if c is None else jnp.maximum(c, s)
                    mx.append(c)
                row_max = jnp.max(jnp.concatenate(mx, axis=0), axis=1, keepdims=True)
                m_new = jnp.maximum(m_b, row_max)
                alpha = jnp.exp(m_b - m_new)
                for r0 in range(0, tq, rc):
                    rs = slice(r0, r0 + rc)
                    m_c = m_new[rs]
                    for cb in range(nlb):
                        cs = slice(cb * LANES, (cb + 1) * LANES)
                        if future(r0, cb):
                            p_scr[hh, rs, cs] = jnp.zeros((rc, LANES), BF16)
                        else:
                            p_scr[hh, rs, cs] = jnp.exp(s_scr[slot, hh, rs, cs] - m_c).astype(BF16)
                v_one = jnp.where(hmasks[hh], vj, jnp.ones_like(vj))
                out.append((m_new, alpha * acc + _dot(p_scr[hh], v_one)))
            return tuple(out)

        def step(j, carry, masked):
            scores(j, 0)
            return soft(j, 0, carry, masked)

        init = tuple((jnp.full((tq, LANES), MASK_VALUE, F32), jnp.zeros((tq, LANES), F32)) for _ in range(hpg))
        carry = lax.fori_loop(0, i, lambda j, cr: step(j, cr, False), init)
        carry = step(i, carry, True)
        fq = fq_ref[...]
        g0 = pl.program_id(0) * hpg
        o_all = jnp.zeros((tq, LANES), F32)
        lse_all = jnp.zeros((tq, LANES), F32)
        for hh in range(hpg):
            m_b, acc = carry[hh]
            l = acc[:, sum_lane[hh]:sum_lane[hh] + 1]
            o_all = jnp.where(hmasks[hh], acc * (1.0 / l), o_all)
            lse_all = jnp.where(lane == hh, m_b[:, 0:1] + jnp.log(l) + _head_col(fq, lane, g0 + hh), lse_all)
        o_ref[...] = o_all.astype(BF16)
        o32_ref[...] = o_all
        lse_ref[...] = lse_all

    return _pcall(
        body, name="flash_fwd", grid=(G, nq),
        in_specs=[pl.BlockSpec((tq, LANES), lambda g, i: (i, g)),
                  pl.BlockSpec((nq, tq, LANES), lambda g, i: (0, 0, g)),
                  pl.BlockSpec((nq, tq, LANES), lambda g, i: (0, 0, g)),
                  pl.BlockSpec((tq, LANES), lambda g, i: (i, 0)),
                  pl.BlockSpec((None, nq, SUBLANES, tq), lambda g, i: (g, 0, 0, 0))],
        out_specs=[pl.BlockSpec((tq, LANES), lambda g, i: (i, g)),
                   pl.BlockSpec((tq, LANES), lambda g, i: (i, g)),
                   pl.BlockSpec((None, tq, LANES), lambda g, i: (g, i, 0))],
        out_shape=[jax.ShapeDtypeStruct((T, D), BF16), jax.ShapeDtypeStruct((T, D), F32),
                   jax.ShapeDtypeStruct((G, T, LANES), F32)],
        scratch_shapes=[pltpu.VMEM((1, hpg, tq, tq), F32), pltpu.VMEM((hpg, tq, tq), BF16)],
        compiler_params=_params(("arbitrary", "arbitrary")),
    )(q, k3, v3, fq_aux, fk_rows)


def _flash_bwd(q, k, v, do, fq_aux, lse_aux, dl_aux, fk_rows, *, dh, tq, tk, s_mul, dq_mul):
    T, D = q.shape
    G = D // LANES
    hpg = LANES // dh
    nq = T // tq
    nk = T // tk
    per = tq // tk
    k3 = k.reshape(nk, tk, D)
    v3 = v.reshape(nk, tk, D)
    rc = min(FLASH_ROWS, tq)

    def body(q_ref, k_ref, v_ref, do_ref, fq_ref, lse_ref, dl_ref, fk_ref,
             dq_ref, dk_ref, dv_ref, dfq_ref, dfk_ref, dk_acc, dv_acc, s_scr, dp_scr, p_scr, ds_scr):
        i = pl.program_id(1)

        @pl.when(i == 0)
        def _():
            dk_acc[...] = jnp.zeros_like(dk_acc)
            dv_acc[...] = jnp.zeros_like(dv_acc)
            dfk_ref[...] = jnp.zeros_like(dfk_ref)

        lane = lax.broadcasted_iota(jnp.int32, (1, LANES), 1)
        q2 = q_ref[...]
        do2 = do_ref[...]
        g0 = pl.program_id(0) * hpg
        fq = fq_ref[...]
        lse = lse_ref[...]
        dl = dl_ref[...]
        hmasks = [(lane >= hh * dh) & (lane < (hh + 1) * dh) for hh in range(hpg)]
        qms = [jnp.where(hm, q2, jnp.zeros_like(q2)) for hm in hmasks]
        doms = [jnp.where(hm, do2, jnp.zeros_like(do2)) for hm in hmasks]
        c_bs = [jnp.broadcast_to(_head_col(fq, lane, g0 + hh) - lse[:, hh:hh + 1], (tq, LANES))
                for hh in range(hpg)]
        dl_bs = [jnp.broadcast_to(_head_col(dl, lane, g0 + hh), (tq, LANES)) for hh in range(hpg)]
        nlb = tk // LANES

        def step(j, carry, off):
            masked = off is not None
            kj = k_ref[j]
            vj = v_ref[j]
            for hh in range(hpg):
                s = _dot_nt(qms[hh], kj)
                s_scr[hh] = s if s_mul == 1.0 else s * s_mul
                dp_scr[hh] = _dot_nt(doms[hh], vj)
            out = []
            for hh in range(hpg):
                dq_acc, rs_p = carry[hh]
                fk_row = fk_ref[j, hh:hh + 1, :]
                rsums = []
                csums = [jnp.zeros((SUBLANES, LANES), F32) for _ in range(nlb)]
                for r0 in range(0, tq, rc):
                    rs = slice(r0, r0 + rc)
                    if masked and r0 + rc <= off:
                        p_scr[hh, rs, :] = jnp.zeros((rc, tk), BF16)
                        ds_scr[hh, rs, :] = jnp.zeros((rc, tk), BF16)
                        rsums.append(jnp.zeros((rc, LANES), F32))
                        continue
                    c_c = c_bs[hh][rs]
                    dl_c = dl_bs[hh][rs]
                    tot = jnp.zeros((rc, LANES), F32)
                    for cb in range(nlb):
                        cs = slice(cb * LANES, (cb + 1) * LANES)
                        if masked and off + cb * LANES > r0 + rc - 1:
                            p_scr[hh, rs, cs] = jnp.zeros((rc, LANES), BF16)
                            ds_scr[hh, rs, cs] = jnp.zeros((rc, LANES), BF16)
                            continue
                        e = (s_scr[hh, rs, cs] - fk_row[:, cs]) + c_c
                        if masked and off + (cb + 1) * LANES - 1 > r0:
                            ri = r0 + lax.broadcasted_iota(jnp.int32, (rc, LANES), 0)
                            ci = off + cb * LANES + lax.broadcasted_iota(jnp.int32, (rc, LANES), 1)
                            e = jnp.where(ci <= ri, e, MASK_VALUE)
                        p = jnp.exp(e)
                        ds = p * (dp_scr[hh, rs, cs] - dl_c)
                        p_scr[hh, rs, cs] = p.astype(BF16)
                        ds_scr[hh, rs, cs] = ds.astype(BF16)
                        tot = tot + ds
                        csums[cb] = csums[cb] + ds.reshape(rc // SUBLANES, SUBLANES, LANES).sum(axis=0)
                    rsums.append(tot)
                col = jnp.concatenate([jnp.sum(c, axis=0, keepdims=True) for c in csums], axis=1)
                dfk_ref[j, hh:hh + 1, :] += -col
                out.append((dq_acc + _dot(ds_scr[hh], kj), rs_p + jnp.concatenate(rsums, axis=0)))
            dk_new = _dot_tn(ds_scr[0], qms[0])
            dv_new = _dot_tn(p_scr[0], doms[0])
            for hh in range(1, hpg):
                dk_new = dk_new + _dot_tn(ds_scr[hh], qms[hh])
                dv_new = dv_new + _dot_tn(p_scr[hh], doms[hh])
            dk_acc[j] += dk_new
            dv_acc[j] += dv_new
            return tuple(out)

        init = tuple((jnp.zeros((tq, LANES), F32), jnp.zeros((tq, LANES), F32)) for _ in range(hpg))
        carry = lax.fori_loop(0, i * per, lambda j, cr: step(j, cr, None), init)
        for d in range(per):
            carry = step(i * per + d, carry, d * tk)
        dq_all = jnp.zeros((tq, LANES), F32)
        dfq_all = jnp.zeros((tq, LANES), F32)
        for hh in range(hpg):
            dq_h, rs_p = carry[hh]
            dq_all = jnp.where(hmasks[hh], dq_h, dq_all)
            dfq_all = jnp.where(lane == hh, jnp.sum(rs_p, axis=1, keepdims=True), dfq_all)
        dq_ref[...] = (dq_all * dq_mul).astype(BF16)
        dfq_ref[...] = dfq_all

        @pl.when(i == nq - 1)
        def _():
            dkv = dk_acc[...]
            if s_mul != 1.0:
                dkv = dkv * s_mul
            dk_ref[...] = dkv.astype(BF16)
            dv_ref[...] = dv_acc[...].astype(BF16)

    blk = pl.BlockSpec((tq, LANES), lambda g, i: (i, g))
    res = pl.BlockSpec((nk, tk, LANES), lambda g, i: (0, 0, g))
    res_in = pl.BlockSpec((nk, tk, LANES), lambda g, i: (0, 0, g), pipeline_mode=pl.Buffered(1))
    aux = pl.BlockSpec((None, tq, LANES), lambda g, i: (g, i, 0))
    heads = pl.BlockSpec((tq, LANES), lambda g, i: (i, 0))
    rows = pl.BlockSpec((None, nk, SUBLANES, tk), lambda g, i: (g, 0, 0, 0))
    dq, dk3, dv3, dfq, dfk = _pcall(
        body, name="flash_bwd", grid=(G, nq),
        in_specs=[blk, res_in, res_in, blk, heads, aux, heads, rows],
        out_specs=[blk, res, res, aux, rows],
        out_shape=[jax.ShapeDtypeStruct((T, D), BF16), jax.ShapeDtypeStruct((nk, tk, D), BF16),
                   jax.ShapeDtypeStruct((nk, tk, D), BF16), jax.ShapeDtypeStruct((G, T, LANES), F32),
                   jax.ShapeDtypeStruct((G, nk, SUBLANES, tk), F32)],
        scratch_shapes=[pltpu.VMEM((nk, tk, LANES), F32), pltpu.VMEM((nk, tk, LANES), F32),
                        pltpu.VMEM((hpg, tq, tk), F32), pltpu.VMEM((hpg, tq, tk), F32),
                        pltpu.VMEM((hpg, tq, tk), BF16), pltpu.VMEM((hpg, tq, tk), BF16)],
        compiler_params=_params(("arbitrary", "arbitrary")),
    )(q, k3, v3, do, fq_aux, lse_aux, dl_aux, fk_rows)
    return dq, dk3.reshape(T, D), dv3.reshape(T, D), dfq, dfk


def _attn_out_fwd(o, x, wo, g_post, tm):
    T, D = x.shape

    def body(o_ref, x_ref, w_ref, g_ref, m_ref, xo_ref):
        m = _dot(o_ref[...], w_ref[...])
        m_ref[...] = m
        xo_ref[...] = x_ref[...] + _rms(m, g_ref[...])

    return _pcall(
        body, name="attn_out_fwd", grid=(T // tm,),
        in_specs=[_rows(tm, D), _rows(tm, D), _full((D, D)), _full((1, D))],
        out_specs=[_rows(tm, D), _rows(tm, D)],
        out_shape=[jax.ShapeDtypeStruct((T, D), F32), jax.ShapeDtypeStruct((T, D), F32)],
        compiler_params=_params(("arbitrary",)),
    )(o, x, wo, g_post)


def _attn_out_bwd(dxo, m, o, wo, g_post, head_ind, tm, dep):
    T, D = m.shape

    def body(dxo_ref, m_ref, o_ref, w_ref, g_ref, ind_ref, dep_ref, dm_ref, do_ref, dl_ref, sums_ref):
        i = pl.program_id(0)
        dm, dgpost = _rms_bwd(m_ref[...], g_ref[...], dxo_ref[...])
        dmb = dm.astype(BF16)
        dm_ref[...] = dmb
        dob = _dot_nt(dmb, w_ref[...]).astype(BF16)
        do_ref[...] = dob
        dl_ref[...] = jnp.dot(dob.astype(F32) * o_ref[...], ind_ref[...], precision=lax.Precision.HIGHEST,
                              preferred_element_type=F32)

        @pl.when(i == 0)
        def _():
            sums_ref[...] = jnp.zeros_like(sums_ref)
        sums_ref[0:1, :] += dgpost

    return _pcall(
        body, name="attn_out_bwd", grid=(T // tm,),
        in_specs=[_rows(tm, D), _rows(tm, D), _rows(tm, D), _full((D, D)), _full((1, D)), _full((D, LANES)),
                  pl.BlockSpec(memory_space=pl.ANY)],
        out_specs=[_rows(tm, D), _rows(tm, D), _rows(tm, LANES), _acc((SUBLANES, D))],
        out_shape=[jax.ShapeDtypeStruct((T, D), BF16), jax.ShapeDtypeStruct((T, D), BF16),
                   jax.ShapeDtypeStruct((T, LANES), F32), jax.ShapeDtypeStruct((SUBLANES, D), F32)],
        compiler_params=_params(("arbitrary",)),
    )(dxo, m, o, wo, g_post, head_ind, dep)


def _attn_in_bwd(dxo, x, g_pre, dq, dk, dv, dlf, lf, wqkv, wf, tm, n_heads):
    T, D = x.shape

    def body(dxo_ref, x_ref, g_ref, dq_ref, dk_ref, dv_ref, dlf_ref, lf_ref, w_ref, wf_ref,
             dxi_ref, h_ref, df_ref, sums_ref, dbf_ref):
        i = pl.program_id(0)
        xv = x_ref[...]
        h_ref[...] = _rms(xv, g_ref[...]).astype(BF16)
        lane = lax.broadcasted_iota(jnp.int32, (1, LANES), 1)
        df = jnp.where(lane < n_heads, dlf_ref[...] * (1.0 - jnp.exp(lf_ref[...])), 0.0)
        dfb = df.astype(BF16)
        df_ref[...] = dfb
        dh = (_dot_nt(dq_ref[...], w_ref[:, 0:D]) + _dot_nt(dk_ref[...], w_ref[:, D:2 * D])
              + _dot_nt(dv_ref[...], w_ref[:, 2 * D:3 * D]) + _dot_nt(dfb, wf_ref[...]))
        dxi, dgpre = _rms_bwd(xv, g_ref[...], dh)
        dxi_ref[...] = dxo_ref[...] + dxi

        @pl.when(i == 0)
        def _():
            sums_ref[...] = jnp.zeros_like(sums_ref)
            dbf_ref[...] = jnp.zeros_like(dbf_ref)
        sums_ref[0:1, :] += dgpre
        dbf_ref[...] += jnp.sum(df, axis=0, keepdims=True)

    return _pcall(
        body, name="attn_in_bwd", grid=(T // tm,),
        in_specs=[_rows(tm, D), _rows(tm, D), _full((1, D)), _rows(tm, D), _rows(tm, D), _rows(tm, D),
                  _rows(tm, LANES), _rows(tm, LANES), _full((D, 3 * D)), _full((D, LANES))],
        out_specs=[_rows(tm, D), _rows(tm, D), _rows(tm, LANES), _acc((SUBLANES, D)), _acc((1, LANES))],
        out_shape=[jax.ShapeDtypeStruct((T, D), F32), jax.ShapeDtypeStruct((T, D), BF16),
                   jax.ShapeDtypeStruct((T, LANES), BF16), jax.ShapeDtypeStruct((SUBLANES, D), F32),
                   jax.ShapeDtypeStruct((1, LANES), F32)],
        compiler_params=_params(("arbitrary",)),
    )(dxo, x, g_pre, dq, dk, dv, dlf, lf, wqkv, wf)


def _loss_head(y, target, tm):
    T, D = y.shape

    def body(y_ref, t_ref, dy_ref, loss_ref):
        i = pl.program_id(0)
        err = y_ref[...] - t_ref[...]
        dy_ref[...] = err * (1.0 / D)
        part = 0.5 * jnp.sum(jnp.mean(err * err, axis=-1, keepdims=True), axis=0, keepdims=True)

        @pl.when(i == 0)
        def _():
            loss_ref[...] = jnp.zeros_like(loss_ref)
        loss_ref[...] += part

    return _pcall(
        body, name="loss_head", grid=(T // tm,),
        in_specs=[_rows(tm, D), _rows(tm, D)],
        out_specs=[_rows(tm, D), _acc((SUBLANES, LANES))],
        out_shape=[jax.ShapeDtypeStruct((T, D), F32), jax.ShapeDtypeStruct((SUBLANES, LANES), F32)],
        compiler_params=_params(("arbitrary",)),
    )(y, target)


def _adamw(recvs, w, m, v, tr, name):
    L, R, C = w.shape
    assert len(recvs) == L
    c1 = 1.0 - ADAM_B1 ** ADAM_STEP
    c2 = 1.0 - ADAM_B2 ** ADAM_STEP

    def body(*refs):
        r_refs = refs[:L]
        w_ref, m_ref, v_ref, g_ref, d_ref, nm_ref, nv_ref = refs[L:]
        layer = pl.program_id(0)
        g = None
        for k in range(L):
            gk = r_refs[k][0, :, 0:C].astype(F32)
            for s in range(1, N_DEV):
                gk = gk + r_refs[k][s, :, 0:C].astype(F32)
            g = gk if g is None else jnp.where(layer == k, gk, g)
        nm = ADAM_B1 * m_ref[...] + (1.0 - ADAM_B1) * g
        nv = ADAM_B2 * v_ref[...] + (1.0 - ADAM_B2) * jnp.square(g)
        m_hat = nm / c1
        v_hat = nv / c2
        g_ref[...] = g
        d_ref[...] = -ADAM_LR * (m_hat / (jnp.sqrt(v_hat) + ADAM_EPS) + ADAM_WD * w_ref[...])
        nm_ref[...] = nm
        nv_ref[...] = nv

    def recv_spec(k):
        return pl.BlockSpec((N_DEV, tr, recvs[k].shape[-1]), lambda l, i: (0, jnp.where(l == k, i, 0), 0))

    blk = pl.BlockSpec((None, tr, C), lambda l, i: (l, i, 0))
    return _pcall(
        body, name=name, grid=(L, R // tr),
        in_specs=[recv_spec(k) for k in range(L)] + [blk] * 3,
        out_specs=[blk] * 4,
        out_shape=[jax.ShapeDtypeStruct((L, R, C), F32)] * 4,
        compiler_params=_params(("arbitrary", "arbitrary")),
    )(*recvs, w, m, v)


def _row_block(rows, cols):
    cap = max(SUBLANES, (256 * 1024) // max(cols, 1))
    best = None
    for t in range(SUBLANES, rows + 1, SUBLANES):
        if rows % t == 0 and t <= cap:
            best = t
    return rows if best is None else best


def kernel(x, g_mix_pre, g_mix_post, g_ffn_pre, g_ffn_post, conv_pw1_w, conv_pw1_b, conv_dw_w, conv_dw_b, conv_ln_g, conv_ln_b, conv_pw2_w, conv_pw2_b, attn_w_in, attn_b_f, attn_w_o, mlp_w_up, mlp_w_down, loss_target, m_g_mix_pre, m_g_mix_post, m_g_ffn_pre, m_g_ffn_post, m_conv_pw1_w, m_conv_pw1_b, m_conv_dw_w, m_conv_dw_b, m_conv_ln_g, m_conv_ln_b, m_conv_pw2_w, m_conv_pw2_b, m_attn_w_in, m_attn_b_f, m_attn_w_o, m_mlp_w_up, m_mlp_w_down, v_g_mix_pre, v_g_mix_post, v_g_ffn_pre, v_g_ffn_post, v_conv_pw1_w, v_conv_pw1_b, v_conv_dw_w, v_conv_dw_b, v_conv_ln_g, v_conv_ln_b, v_conv_pw2_w, v_conv_pw2_b, v_attn_w_in, v_attn_b_f, v_attn_w_o, v_mlp_w_up, v_mlp_w_down):
    _, T, D = x.shape
    H = attn_b_f.shape[-1]
    dh = D // H
    width = conv_dw_w.shape[1]
    cin = attn_w_in.shape[-1]
    fs = mlp_w_up.shape[-1]
    G = D // LANES
    hpg = LANES // dh
    assert T % 4 == 0 and D % LANES == 0 and LANES % dh == 0 and width <= CONV_HALO and H <= LANES

    tm = min(512, T // 4)
    tmb = min(256, T // 4)
    tqf = min(1024, T // 4)
    tkb = tm
    tmc = min(256, T // 4)
    lc = min(256, D)
    tkw = min(2048, T // 4)
    tb = min(256, T // 4)

    scale = float(dh) ** -0.5
    mant, _ = math.frexp(scale)
    q_mul = scale if mant == 0.5 else 1.0
    s_mul = 1.0 if mant == 0.5 else scale

    x2 = x.reshape(T, D)
    tgt = loss_target.reshape(T, D)

    w_srcs = [conv_pw1_w, conv_dw_w, conv_pw2_w, mlp_w_up, mlp_w_down, attn_w_in, attn_w_o]
    w_items = [(0, 0, "whole"), (1, 0, "whole"), (2, 0, "whole"), (3, 0, "whole"), (4, 0, "whole"),
               (5, 0, "whole"), (6, 0, "whole"), (3, 1, "whole"), (4, 1, "whole")]
    cin_w = -(-cin // LANES) * LANES
    w_lands = _place_own(
        w_srcs, w_items,
        [((N_DEV, D, cin_w) if si == 5 else (N_DEV,) + w_srcs[si].shape[1:], F32 if si == 1 else BF16)
         for si, _, _ in w_items],
        "stage_weights", cast=True)
    me_arr = _dev_index(*_mesh_pos()).astype(jnp.int32).reshape(1)
    w_groups = [[0], [1, 2], [3, 4], [5, 6], [7, 8]]
    g_sems, _, w_lands, g_token = _push_start(
        [], w_lands, [[(a, None, "own") for a in grp] for grp in w_groups], "gather_start")

    def gather_wait(gi, after):
        grp = w_groups[gi]
        return _push_wait(g_sems[gi], [], [w_lands[a] for a in grp], [(k, None, "own") for k in range(len(grp))],
                          after, "gather_wait%d" % gi)

    (w1g,) = gather_wait(0, g_token)
    bf = jnp.pad(attn_b_f, ((0, 0), (0, LANES - H)))

    row = lambda a, i: a[i:i + 1]

    a0, u0 = _conv_in_fwd(x2, row(g_mix_pre, 0), w1g, conv_pw1_b, tm)
    dwg, w2g = gather_wait(1, u0)
    w2 = w2g.reshape(D, D)
    dw_full = jnp.transpose(dwg, (1, 0, 2)).reshape(width, D)
    w32 = jnp.pad(dw_full, ((0, 32 - width), (0, 0)))
    y0 = _dwconv_fwd(u0, w32, conv_dw_b, tmc, lc, width)
    m0, x_1 = _conv_out_fwd(y0, x2, conv_ln_g, conv_ln_b, w2, conv_pw2_b, row(g_mix_post, 0), tm)
    wu0, wd0 = gather_wait(2, x_1)
    up0, n0, x_2 = _mlp_fwd(x_1, row(g_ffn_pre, 0), wu0, wd0, row(g_ffn_post, 0), tm, "mlp0_fwd")

    wing, wog = gather_wait(3, x_2)
    wo = wog.reshape(D, D)
    win = jnp.transpose(wing[:, :, :cin], (1, 0, 2)).reshape(D, N_DEV * cin)
    wqkv = win[:, :3 * D]
    wf = jnp.pad(win[:, 3 * D:], ((0, 0), (0, LANES - H)))
    q, k, v, lf = _attn_in_fwd(x_2, row(g_mix_pre, 1), wqkv, wf, bf, tm, q_mul, H)
    fcum = _cumsum_rows(lf, jnp.zeros_like(lf), tb, False, "forget_cumsum")


    def key_rows(blk):
        r = jnp.transpose(fcum[:, :H].T.reshape(G, hpg, T // blk, blk), (0, 2, 1, 3))
        return jnp.pad(r, ((0, 0), (0, 0), (0, SUBLANES - hpg), (0, 0)))

    fk_rows = key_rows(tkb)
    o, o32, lse_aux = _flash_fwd(q, k, v, fcum, key_rows(tqf), dh=dh, tq=tqf, s_mul=s_mul)
    m1, x_3 = _attn_out_fwd(o, x_2, wo, row(g_mix_post, 1), tm)
    wu1, wd1 = gather_wait(4, x_3)
    up1, n1, x_4 = _mlp_fwd(x_3, row(g_ffn_pre, 1), wu1, wd1, row(g_ffn_post, 1), tm, "mlp1_fwd")

    dx, loss_blk = _loss_head(x_4, tgt, tm)

    def mlp_back(dx, n_l, x_in, up_l, l, wu, wd, dep):
        dxi, h, dm, dup, sums = _mlp_bwd(dx, n_l, x_in, up_l, row(g_ffn_pre, l), wu, wd, row(g_ffn_post, l),
                                         tmb, "mlp%d_bwd" % l, dep)
        dwu = _mm_tn(h, dup, nj=N_DEV, a_cols=D, g_cols=fs, a_by_j=False, g_by_j=True, tk=tkw,
                     name="mlp%d_dwu" % l)
        dwd = _mm_tn(up_l, dm, nj=N_DEV, a_cols=fs, g_cols=D, a_by_j=True, g_by_j=False, tk=tkw,
                     name="mlp%d_dwd" % l, act=True)
        return dxi, dwu, dwd, sums

    def push_grads(srcs, modes, name):
        items = [(i, None, mode) for i, mode in enumerate(modes)]
        lands = _seed_lands(srcs, modes, me_arr, name + "_own")
        sems, srcs_t, lands_t, token = _push_start(srcs, lands, [items], name + "_start")
        return (sems[0], srcs_t, lands_t, items), token

    def pull_grads(handle, after, name):
        sems, srcs_t, lands_t, items = handle
        return _push_wait(sems, srcs_t, lands_t, items, after, name + "_wait")

    rs = D // N_DEV
    dx, dwu1, dwd1, s_mlp1 = mlp_back(dx, n1, x_3, up1, 1, wu1, wd1, loss_blk)
    h_mlp1, tok = push_grads([dwu1, dwd1], ["slot", "slot"], "grads_mlp1")

    head_ind = jnp.asarray((np.arange(D)[:, None] // dh == np.arange(LANES)[None, :]).astype(np.float32))
    dm1, do, delta, s_ao = _attn_out_bwd(dx, m1, o32, wo, row(g_mix_post, 1), head_ind, tm, tok)
    dwo = _mm_tn(o, dm1, nj=1, a_cols=D, g_cols=D, a_by_j=False, g_by_j=False, tk=tkw, name="attn_dwo")
    dq, dk, dv, dfq, dfk = _flash_bwd(q, k, v, do, fcum, lse_aux, delta, fk_rows,
                                      dh=dh, tq=tkb, tk=tkb, s_mul=s_mul, dq_mul=scale)
    df_k = jnp.pad(jnp.transpose(dfk[:, :, :hpg, :], (0, 2, 1, 3)).reshape(H, T).T, ((0, 0), (0, LANES - H)))
    df_q = jnp.pad(jnp.transpose(dfq[:, :, :hpg], (1, 0, 2)).reshape(T, H), ((0, 0), (0, LANES - H)))
    dlf = _cumsum_rows(df_q, df_k, tb, True, "forget_cumsum_bwd")
    dx, h_at, df, s_ai, dbf = _attn_in_bwd(dx, x_2, row(g_mix_pre, 1), dq, dk, dv, dlf, lf, wqkv, wf, tm, H)
    dwq = _mm_tn(h_at, dq, nj=1, a_cols=D, g_cols=D, a_by_j=False, g_by_j=False, tk=tkw, name="attn_dwq")
    dwk = _mm_tn(h_at, dk, nj=1, a_cols=D, g_cols=D, a_by_j=False, g_by_j=False, tk=tkw, name="attn_dwk")
    dwv = _mm_tn(h_at, dv, nj=1, a_cols=D, g_cols=D, a_by_j=False, g_by_j=False, tk=tkw, name="attn_dwv")
    dwf = _mm_tn(h_at, df, nj=1, a_cols=D, g_cols=LANES, a_by_j=False, g_by_j=False, tk=tkw, name="attn_dwf")
    dwin = jnp.concatenate([dwq[0], dwk[0], dwv[0], dwf[0][:, :H]], axis=1)
    dwin = jnp.pad(jnp.transpose(dwin.reshape(D, N_DEV, cin), (1, 0, 2)), ((0, 0), (0, 0), (0, cin_w - cin)))
    h_attn, tok = push_grads([dwin, dwo.reshape(N_DEV, rs, D)], ["slot", "slot"], "grads_attn")

    dx, dwu0, dwd0, s_mlp0 = mlp_back(dx, n0, x_1, up0, 0, wu0, wd0, tok)
    h_mlp0, tok = push_grads([dwu0, dwd0], ["slot", "slot"], "grads_mlp0")

    dy0, dm0, z0, s_co = _conv_out_bwd(dx, m0, y0, conv_ln_g, conv_ln_b, w2, row(g_mix_post, 0), tm, tok)
    dw2 = _mm_tn(z0, dm0, nj=1, a_cols=D, g_cols=D, a_by_j=False, g_by_j=False, tk=tkw, name="conv_dw2")
    du0, ddw = _dwconv_bwd(dy0, u0, w32, tmc, lc, width)
    grad_x, h_cv, da0, s_ci, db1 = _conv_in_bwd(dx, du0, a0, x2, row(g_mix_pre, 0), w1g, tm)
    dw1 = _mm_tn(h_cv, da0, nj=N_DEV, a_cols=D, g_cols=(2 * D) // N_DEV, a_by_j=False, g_by_j=True, tk=tkw,
                 name="conv_dw1")
    ddw_s = jnp.transpose(ddw[:width].reshape(width, N_DEV, D // N_DEV), (1, 0, 2))

    def pad_row(a):
        return jnp.pad(a, ((0, 0), (0, D - a.shape[1])))

    def pack(gmp, gmq, gfp, gfq, b1, dwb, lng, lnb, b2, bfv, last):
        return jnp.concatenate([gmp, gmq, gfp, gfq, b1.reshape(2, D), dwb, lng, lnb, b2, pad_row(bfv), last],
                               axis=0)

    zero_row = jnp.zeros((1, D), F32)
    small_g = pack(
        jnp.concatenate([row(s_ci, 0), row(s_ai, 0)], axis=0),
        jnp.concatenate([row(s_co, 0), row(s_ao, 0)], axis=0),
        jnp.concatenate([row(s_mlp0, 1), row(s_mlp1, 1)], axis=0),
        jnp.concatenate([row(s_mlp0, 0), row(s_mlp1, 0)], axis=0),
        db1, row(s_co, 4), row(s_co, 1), row(s_co, 2), row(s_co, 3), dbf[:, :H],
        pad_row(loss_blk[0:1, 0:1]))
    h_conv, tok = push_grads([dw1, ddw_s, dw2.reshape(N_DEV, rs, D), small_g], ["slot", "slot", "slot", "whole"],
                             "grads_conv")

    def opt(recvs, w, m, v, name):
        shp = w.shape
        L, C = shp[0], shp[-1]
        R = int(np.prod(shp[1:-1]))
        outs = _adamw([r.reshape(N_DEV, R, r.shape[-1]) for r in recvs], w.reshape(L, R, C), m.reshape(L, R, C),
                      v.reshape(L, R, C), _row_block(R, C), name)
        return [t.reshape(shp) for t in outs]

    big = {}
    r_wu1, r_wd1 = pull_grads(h_mlp1, tok, "grads_mlp1")
    r_win, r_wo = pull_grads(h_attn, r_wd1, "grads_attn")
    big["attn_w_in"] = opt([r_win], attn_w_in, m_attn_w_in, v_attn_w_in, "adamw_win")
    big["attn_w_o"] = opt([r_wo], attn_w_o, m_attn_w_o, v_attn_w_o, "adamw_wo")
    r_wu0, r_wd0 = pull_grads(h_mlp0, big["attn_w_o"][0], "grads_mlp0")
    big["mlp_w_up"] = opt([r_wu0, r_wu1], mlp_w_up, m_mlp_w_up, v_mlp_w_up, "adamw_wup")
    big["mlp_w_down"] = opt([r_wd0, r_wd1], mlp_w_down, m_mlp_w_down, v_mlp_w_down, "adamw_wdown")
    r_w1, r_dw, r_w2, r_small = pull_grads(h_conv, big["mlp_w_down"][0], "grads_conv")
    big["conv_pw1_w"] = opt([r_w1], conv_pw1_w, m_conv_pw1_w, v_conv_pw1_w, "adamw_pw1")
    big["conv_dw_w"] = opt([r_dw], conv_dw_w, m_conv_dw_w, v_conv_dw_w, "adamw_dw")
    big["conv_pw2_w"] = opt([r_w2], conv_pw2_w, m_conv_pw2_w, v_conv_pw2_w, "adamw_pw2")
    small_w = pack(g_mix_pre, g_mix_post, g_ffn_pre, g_ffn_post, conv_pw1_b, conv_dw_b, conv_ln_g, conv_ln_b,
                   conv_pw2_b, attn_b_f, zero_row)
    small_m = pack(m_g_mix_pre, m_g_mix_post, m_g_ffn_pre, m_g_ffn_post, m_conv_pw1_b, m_conv_dw_b, m_conv_ln_g,
                   m_conv_ln_b, m_conv_pw2_b, m_attn_b_f, zero_row)
    small_v = pack(v_g_mix_pre, v_g_mix_post, v_g_ffn_pre, v_g_ffn_post, v_conv_pw1_b, v_conv_dw_b, v_conv_ln_g,
                   v_conv_ln_b, v_conv_pw2_b, v_attn_b_f, zero_row)
    sm = _adamw([r_small], small_w[None], small_m[None], small_v[None], small_w.shape[0], "adamw_small")
    sm = [t[0] for t in sm]
    loss = sm[0][15, 0]

    def unpack(t):
        return {"g_mix_pre": t[0:2], "g_mix_post": t[2:4], "g_ffn_pre": t[4:6], "g_ffn_post": t[6:8],
                "conv_pw1_b": t[8:10].reshape(1, 2 * D), "conv_dw_b": t[10:11], "conv_ln_g": t[11:12],
                "conv_ln_b": t[12:13], "conv_pw2_b": t[13:14], "attn_b_f": t[14:15, :H]}

    small = [unpack(t) for t in sm]
    names = ["g_mix_pre", "g_mix_post", "g_ffn_pre", "g_ffn_post", "conv_pw1_w", "conv_pw1_b", "conv_dw_w",
             "conv_dw_b", "conv_ln_g", "conv_ln_b", "conv_pw2_w", "conv_pw2_b", "attn_w_in", "attn_b_f",
             "attn_w_o", "mlp_w_up", "mlp_w_down"]
    outs = [loss, grad_x.reshape(1, T, D)]
    for kind in range(4):
        for nme in names:
            outs.append(big[nme][kind] if nme in big else small[kind][nme])
    return tuple(outs)
```

```python
import functools
import math

import numpy as np
import jax
import jax.numpy as jnp
from jax import lax
from jax.experimental import pallas as pl
from jax.experimental.pallas import tpu as pltpu

F32 = jnp.float32
BF16 = jnp.bfloat16

RMS_EPS = 1e-6
LN_EPS = 1e-5
MASK_VALUE = -1e30
ADAM_LR = 0.001
ADAM_B1 = 0.9
ADAM_B2 = 0.999
ADAM_EPS = 1e-08
ADAM_WD = 0.01
ADAM_STEP = 10

N_DEV = 8
LANES = 128
SUBLANES = 8
CONV_HALO = 32
CONV_ROWS = 32
FLASH_ROWS = 32
VMEM_LIMIT = 56 * 1024 * 1024

_pcall = pl.pallas_call


def _params(sem=None):
    if sem is None:
        return pltpu.CompilerParams(vmem_limit_bytes=VMEM_LIMIT)
    return pltpu.CompilerParams(dimension_semantics=sem, vmem_limit_bytes=VMEM_LIMIT)


def _dot(a, b):
    return jnp.dot(a, b, preferred_element_type=F32)


def _dot_nt(a, b):
    return lax.dot_general(a, b, (((1,), (1,)), ((), ())), preferred_element_type=F32)


def _dot_tn(a, b):
    return lax.dot_general(a, b, (((0,), (0,)), ((), ())), preferred_element_type=F32)


def _full(shape):
    nd = len(shape)
    return pl.BlockSpec(shape, lambda *g: (0,) * nd, pipeline_mode=pl.Buffered(1))


def _acc(shape):
    nd = len(shape)
    return pl.BlockSpec(shape, lambda *g: (0,) * nd)


def _rows(tm, cols):
    return pl.BlockSpec((tm, cols), lambda i: (i, 0))


def _rms(x, g):
    r = lax.rsqrt(jnp.mean(x * x, axis=-1, keepdims=True) + RMS_EPS)
    return x * r * g


def _rms_bwd(x, g, dy):
    r = lax.rsqrt(jnp.mean(x * x, axis=-1, keepdims=True) + RMS_EPS)
    n = x * r
    dg = jnp.sum(dy * n, axis=0, keepdims=True)
    dn = dy * g
    dx = r * (dn - n * jnp.mean(dn * n, axis=-1, keepdims=True))
    return dx, dg


def _sigmoid(x):
    return 1.0 / (1.0 + jnp.exp(-x))


def _mesh_pos():
    return lax.axis_index("x"), lax.axis_index("y"), lax.axis_index("c")


def _dev_index(px, py, pc):
    return 4 * px + 2 * py + pc


_HBM = pl.BlockSpec(memory_space=pltpu.HBM)
_SEM = pl.BlockSpec(memory_space=pltpu.SEMAPHORE)
_EFFECT = pltpu.SideEffectType.DATAFLOW_SIDE_EFFECTING


def _peer(r, x, y, c):
    p = ((1 - x) if r & 4 else x, (1 - y) if r & 2 else y, (1 - c) if r & 1 else c)
    return p, _dev_index(*p)


def _src_ref(refs, item, me_id=None, to_id=None):
    si, sub, mode = item
    r = refs[si] if sub is None else refs[si].at[sub]
    if mode == "slot":
        return r.at[to_id]
    if mode == "own":
        return r.at[me_id]
    return r


def _place_own(srcs, items, land_shapes, name, cast=False):
    ns, n = len(srcs), len(items)

    def body(*refs):
        src = refs[:ns]
        land = refs[ns:ns + n]
        stage = refs[ns + n:ns + 2 * n] if cast else None
        sems = refs[-1]
        me_id = _dev_index(*_mesh_pos())
        cps = []
        for a, item in enumerate(items):
            s = _src_ref(src, item, to_id=me_id)
            if cast:
                if s.shape != stage[a].shape:
                    stage[a][...] = jnp.zeros_like(stage[a])
                    stage[a][:, 0:s.shape[-1]] = s[...].astype(stage[a].dtype)
                else:
                    stage[a][...] = s[...].astype(stage[a].dtype)
                s = stage[a]
            cp = pltpu.make_async_copy(s, land[a].at[me_id], sems.at[a])
            cp.start()
            cps.append(cp)
        for cp in cps:
            cp.wait()

    return _pcall(
        body, name=name,
        out_shape=[jax.ShapeDtypeStruct(tuple(s), d) for s, d in land_shapes],
        in_specs=[pl.BlockSpec(memory_space=pltpu.VMEM if cast else pl.ANY)] * ns,
        out_specs=[pl.BlockSpec(memory_space=pl.ANY)] * n,
        scratch_shapes=([pltpu.VMEM(tuple(s[1:]), d) for s, d in land_shapes] if cast else [])
        + [pltpu.SemaphoreType.DMA((n,))],
        compiler_params=pltpu.CompilerParams(vmem_limit_bytes=VMEM_LIMIT),
    )(*srcs)


def _seed_lands(srcs, modes, me, name):
    n = len(srcs)
    parts = [tuple(s.shape[1:]) if mode == "slot" else tuple(s.shape) for s, mode in zip(srcs, modes)]

    def body(me_ref, *refs):
        s = pl.program_id(0)
        for a in range(n):
            v = refs[a][...]
            refs[n + a][...] = jnp.where(s == me_ref[0], v, jnp.zeros_like(v))

    def in_spec(part, mode):
        nd = len(part)
        if mode == "slot":
            return pl.BlockSpec((None,) + part, lambda s, me_ref: (me_ref[0],) + (0,) * nd)
        return pl.BlockSpec(part, lambda s, me_ref: (0,) * nd)

    def out_spec(part):
        nd = len(part)
        return pl.BlockSpec((None,) + part, lambda s, me_ref: (s,) + (0,) * nd)

    return _pcall(
        body, name=name,
        grid_spec=pltpu.PrefetchScalarGridSpec(
            num_scalar_prefetch=1, grid=(N_DEV,),
            in_specs=[in_spec(p, m) for p, m in zip(parts, modes)],
            out_specs=[out_spec(p) for p in parts]),
        out_shape=[jax.ShapeDtypeStruct((N_DEV,) + p, s.dtype) for p, s in zip(parts, srcs)],
        compiler_params=_params(("arbitrary",)),
    )(me, *srcs)


def _push_start(srcs, lands, groups, name):
    ns, n = len(srcs), len(lands)
    ng = len(groups)
    assert sum(len(g) for g in groups) == n

    def body(*refs):
        ops = refs[:ns + n]
        land = refs[ns:ns + n]
        sems = refs[ns + n:ns + n + 2 * ng]
        token = refs[-1]
        x, y, c = _mesh_pos()
        me_id = _dev_index(x, y, c)
        a = 0
        for gi, grp in enumerate(groups):
            for k, item in enumerate(grp):
                for r in range(1, N_DEV):
                    p, pid = _peer(r, x, y, c)
                    pltpu.make_async_remote_copy(
                        src_ref=_src_ref(ops, item, me_id=me_id, to_id=pid), dst_ref=land[a].at[me_id],
                        send_sem=sems[2 * gi].at[k * (N_DEV - 1) + r - 1],
                        recv_sem=sems[2 * gi + 1].at[k * (N_DEV - 1) + r - 1],
                        device_id=p, device_id_type=pl.DeviceIdType.MESH).start()
                a += 1
        token[...] = jnp.zeros_like(token)

    sem_shapes = []
    for grp in groups:
        sem_shapes += [pltpu.SemaphoreType.DMA((len(grp) * (N_DEV - 1),))] * 2
    arrs = list(srcs) + list(lands)
    res = _pcall(
        body, name=name,
        out_shape=tuple(sem_shapes) + tuple(pltpu.HBM(a.shape, a.dtype) for a in arrs)
        + (jax.ShapeDtypeStruct((SUBLANES, LANES), F32),),
        in_specs=[_HBM] * (ns + n),
        out_specs=tuple([_SEM] * (2 * ng)) + tuple([_HBM] * (ns + n)) + (pl.BlockSpec(memory_space=pltpu.VMEM),),
        input_output_aliases={i: 2 * ng + i for i in range(ns + n)},
        compiler_params=pltpu.CompilerParams(has_side_effects=_EFFECT),
    )(*[pltpu.with_memory_space_constraint(a, pltpu.HBM) for a in arrs])
    sems = [(res[2 * gi], res[2 * gi + 1]) for gi in range(ng)]
    thru = res[2 * ng:2 * ng + ns + n]
    return sems, list(thru[:ns]), list(thru[ns:]), res[-1]


def _push_wait(sems, srcs, lands, group, after, name):
    ns, n = len(srcs), len(lands)
    assert len(group) == n

    def body(*refs):
        ops = refs[:ns + n]
        land = refs[ns:ns + n]
        send_sems, recv_sems = refs[ns + n], refs[ns + n + 1]
        x, y, c = _mesh_pos()
        me_id = _dev_index(x, y, c)
        for k, item in enumerate(group):
            for r in range(1, N_DEV):
                p, pid = _peer(r, x, y, c)
                cp = pltpu.make_async_remote_copy(
                    src_ref=_src_ref(ops, item, me_id=me_id, to_id=pid), dst_ref=land[k].at[pid],
                    send_sem=send_sems.at[k * (N_DEV - 1) + r - 1], recv_sem=recv_sems.at[k * (N_DEV - 1) + r - 1],
                    device_id=p, device_id_type=pl.DeviceIdType.MESH)
                cp.wait_send()
                cp.wait_recv()

    arrs = list(srcs) + list(lands)
    res = _pcall(
        body, name=name,
        out_shape=tuple(pltpu.HBM(a.shape, a.dtype) for a in arrs),
        in_specs=[_HBM] * (ns + n) + [_SEM, _SEM, pl.BlockSpec(memory_space=pl.ANY)],
        out_specs=tuple([_HBM] * (ns + n)),
        input_output_aliases={i: i for i in range(ns + n)},
        compiler_params=pltpu.CompilerParams(has_side_effects=_EFFECT),
    )(*arrs, sems[0], sems[1], after)
    return list(res[ns:])


def _conv_in_fwd(x, g_pre, w1g, b1, tm):
    T, D = x.shape
    ns, _, cs = w1g.shape
    half = ns // 2

    def body(x_ref, g_ref, w_ref, b_ref, a_ref, u_ref):
        h = _rms(x_ref[...], g_ref[...]).astype(BF16)
        parts = []
        for s in range(ns):
            a_s = _dot(h, w_ref[s]) + b_ref[:, s * cs:(s + 1) * cs]
            a_ref[:, s * cs:(s + 1) * cs] = a_s
            parts.append(a_s)
        for s in range(half):
            u_ref[:, s * cs:(s + 1) * cs] = parts[s] * _sigmoid(parts[s + half])

    return _pcall(
        body, name="conv_in_fwd", grid=(T // tm,),
        in_specs=[_rows(tm, D), _full((1, D)), _full(w1g.shape), _full((1, 2 * D))],
        out_specs=[_rows(tm, 2 * D), _rows(tm, D)],
        out_shape=[jax.ShapeDtypeStruct((T, 2 * D), F32), jax.ShapeDtypeStruct((T, D), F32)],
        compiler_params=_params(("arbitrary",)),
    )(x, g_pre, w1g, b1)


def _shifted_copies(ext_ref, sh_ref, tm):
    n = tm + CONV_HALO - SUBLANES
    for b in range(1, SUBLANES):
        sh_ref[b - 1, 0:n, :] = ext_ref[b:b + n, :]


def _shifted_rows(ext_ref, sh_ref, off, r0, ls):
    b = off % SUBLANES
    a8 = off - b + r0
    src = ext_ref if b == 0 else sh_ref.at[b - 1]
    return src[a8:a8 + CONV_ROWS, ls]


def _dwconv_fwd(u, w32, b, tm, lc, width):
    T, D = u.shape
    hb = tm // CONV_HALO

    def body(u_ref, halo_ref, w_ref, b_ref, y_ref, ext_ref, sh_ref):
        i = pl.program_id(0)
        ext_ref[0:CONV_HALO, :] = jnp.where(i > 0, halo_ref[...], 0.0)
        ext_ref[CONV_HALO:, :] = u_ref[...]
        _shifted_copies(ext_ref, sh_ref, tm)
        for r0 in range(0, tm, CONV_ROWS):
            for l0 in range(0, lc, LANES):
                ls = slice(l0, l0 + LANES)
                acc = jnp.zeros((CONV_ROWS, LANES), F32) + b_ref[:, ls]
                for j in range(width):
                    off = CONV_HALO - (width - 1) + j
                    acc = acc + w_ref[j:j + 1, ls] * _shifted_rows(ext_ref, sh_ref, off, r0, ls)
                y_ref[r0:r0 + CONV_ROWS, ls] = acc

    return _pcall(
        body, name="dwconv_fwd", grid=(T // tm, D // lc),
        in_specs=[pl.BlockSpec((tm, lc), lambda i, l: (i, l)),
                  pl.BlockSpec((CONV_HALO, lc), lambda i, l: (jnp.maximum(i * hb - 1, 0), l)),
                  pl.BlockSpec((32, lc), lambda i, l: (0, l)),
                  pl.BlockSpec((1, lc), lambda i, l: (0, l))],
        out_specs=pl.BlockSpec((tm, lc), lambda i, l: (i, l)),
        out_shape=jax.ShapeDtypeStruct((T, D), F32),
        scratch_shapes=[pltpu.VMEM((tm + CONV_HALO, lc), F32),
                        pltpu.VMEM((SUBLANES - 1, tm + CONV_HALO, lc), F32)],
        compiler_params=_params(("arbitrary", "arbitrary")),
    )(u, u, w32, b)


def _ln_parts(y, g, b):
    mu = jnp.mean(y, axis=-1, keepdims=True)
    yc = y - mu
    rstd = lax.rsqrt(jnp.mean(yc * yc, axis=-1, keepdims=True) + LN_EPS)
    yhat = yc * rstd
    return yhat, rstd, yhat * g + b


def _conv_out_fwd(y, x, ln_g, ln_b, w2, b2, g_post, tm):
    T, D = x.shape

    def body(y_ref, x_ref, lg_ref, lb_ref, w_ref, b_ref, g_ref, m_ref, xo_ref):
        _, _, yn = _ln_parts(y_ref[...], lg_ref[...], lb_ref[...])
        z = (yn * _sigmoid(yn)).astype(BF16)
        m = _dot(z, w_ref[...]) + b_ref[...]
        m_ref[...] = m
        xo_ref[...] = x_ref[...] + _rms(m, g_ref[...])

    return _pcall(
        body, name="conv_out_fwd", grid=(T // tm,),
        in_specs=[_rows(tm, D), _rows(tm, D), _full((1, D)), _full((1, D)), _full((D, D)), _full((1, D)),
                  _full((1, D))],
        out_specs=[_rows(tm, D), _rows(tm, D)],
        out_shape=[jax.ShapeDtypeStruct((T, D), F32), jax.ShapeDtypeStruct((T, D), F32)],
        compiler_params=_params(("arbitrary",)),
    )(y, x, ln_g, ln_b, w2, b2, g_post)


def _conv_out_bwd(dxo, m, y, ln_g, ln_b, w2, g_post, tm, dep):
    T, D = m.shape

    def body(dxo_ref, m_ref, y_ref, lg_ref, lb_ref, w_ref, g_ref, dep_ref, dy_ref, dm_ref, z_ref, sums_ref):
        i = pl.program_id(0)
        dm, dgpost = _rms_bwd(m_ref[...], g_ref[...], dxo_ref[...])
        dmb = dm.astype(BF16)
        dm_ref[...] = dmb
        yhat, rstd, yn = _ln_parts(y_ref[...], lg_ref[...], lb_ref[...])
        sg = _sigmoid(yn)
        z_ref[...] = (yn * sg).astype(BF16)
        dz = _dot_nt(dmb, w_ref[...])
        dyn = dz * (sg + yn * sg * (1.0 - sg))
        dyh = dyn * lg_ref[...]
        dy = rstd * (dyh - jnp.mean(dyh, axis=-1, keepdims=True)
                     - yhat * jnp.mean(dyh * yhat, axis=-1, keepdims=True))
        dy_ref[...] = dy

        @pl.when(i == 0)
        def _():
            sums_ref[...] = jnp.zeros_like(sums_ref)
        sums_ref[0:1, :] += dgpost
        sums_ref[1:2, :] += jnp.sum(dyn * yhat, axis=0, keepdims=True)
        sums_ref[2:3, :] += jnp.sum(dyn, axis=0, keepdims=True)
        sums_ref[3:4, :] += jnp.sum(dm, axis=0, keepdims=True)
        sums_ref[4:5, :] += jnp.sum(dy, axis=0, keepdims=True)

    return _pcall(
        body, name="conv_out_bwd", grid=(T // tm,),
        in_specs=[_rows(tm, D), _rows(tm, D), _rows(tm, D), _full((1, D)), _full((1, D)), _full((D, D)),
                  _full((1, D)), pl.BlockSpec(memory_space=pl.ANY)],
        out_specs=[_rows(tm, D), _rows(tm, D), _rows(tm, D), _acc((SUBLANES, D))],
        out_shape=[jax.ShapeDtypeStruct((T, D), F32), jax.ShapeDtypeStruct((T, D), BF16),
                   jax.ShapeDtypeStruct((T, D), BF16), jax.ShapeDtypeStruct((SUBLANES, D), F32)],
        compiler_params=_params(("arbitrary",)),
    )(dxo, m, y, ln_g, ln_b, w2, g_post, dep)


def _dwconv_bwd(dy, u, w32, tm, lc, width):
    T, D = u.shape
    hb = tm // CONV_HALO
    nt = T // tm
    last_halo = T // CONV_HALO - 1

    def body(dy_ref, dyn_ref, u_ref, up_ref, w_ref, du_ref, dw_ref, exty_ref, extu_ref, acc_ref, shy_ref, shu_ref):
        i = pl.program_id(1)
        exty_ref[0:tm, :] = dy_ref[...]
        exty_ref[tm:, :] = jnp.where(i < nt - 1, dyn_ref[...], 0.0)
        extu_ref[0:CONV_HALO, :] = jnp.where(i > 0, up_ref[...], 0.0)
        extu_ref[CONV_HALO:, :] = u_ref[...]
        _shifted_copies(exty_ref, shy_ref, tm)
        _shifted_copies(extu_ref, shu_ref, tm)

        @pl.when(i == 0)
        def _():
            acc_ref[...] = jnp.zeros_like(acc_ref)

        for r0 in range(0, tm, CONV_ROWS):
            for l0 in range(0, lc, LANES):
                ls = slice(l0, l0 + LANES)
                dyc = exty_ref[r0:r0 + CONV_ROWS, ls]
                du = jnp.zeros((CONV_ROWS, LANES), F32)
                for j in range(width):
                    du = du + w_ref[j:j + 1, ls] * _shifted_rows(exty_ref, shy_ref, (width - 1) - j, r0, ls)
                    prod = dyc * _shifted_rows(extu_ref, shu_ref, CONV_HALO - (width - 1) + j, r0, ls)
                    acc_ref[j, :, ls] += prod.reshape(CONV_ROWS // SUBLANES, SUBLANES, LANES).sum(axis=0)
                du_ref[r0:r0 + CONV_ROWS, ls] = du

        @pl.when(i == nt - 1)
        def _():
            for j in range(32):
                dw_ref[j:j + 1, :] = jnp.sum(acc_ref[j], axis=0, keepdims=True)

    return _pcall(
        body, name="dwconv_bwd", grid=(D // lc, nt),
        in_specs=[pl.BlockSpec((tm, lc), lambda l, i: (i, l)),
                  pl.BlockSpec((CONV_HALO, lc), lambda l, i: (jnp.minimum((i + 1) * hb, last_halo), l)),
                  pl.BlockSpec((tm, lc), lambda l, i: (i, l)),
                  pl.BlockSpec((CONV_HALO, lc), lambda l, i: (jnp.maximum(i * hb - 1, 0), l)),
                  pl.BlockSpec((32, lc), lambda l, i: (0, l))],
        out_specs=[pl.BlockSpec((tm, lc), lambda l, i: (i, l)),
                   pl.BlockSpec((32, lc), lambda l, i: (0, l))],
        out_shape=[jax.ShapeDtypeStruct((T, D), F32), jax.ShapeDtypeStruct((32, D), F32)],
        scratch_shapes=[pltpu.VMEM((tm + CONV_HALO, lc), F32), pltpu.VMEM((tm + CONV_HALO, lc), F32),
                        pltpu.VMEM((32, SUBLANES, lc), F32),
                        pltpu.VMEM((SUBLANES - 1, tm + CONV_HALO, lc), F32),
                        pltpu.VMEM((SUBLANES - 1, tm + CONV_HALO, lc), F32)],
        compiler_params=_params(("arbitrary", "arbitrary")),
    )(dy, dy, u, u, w32)


def _conv_in_bwd(dxo, du, a, x, g_pre, w1g, tm):
    T, D = x.shape
    ns, _, cs = w1g.shape
    half = ns // 2

    def body(dxo_ref, du_ref, a_ref, x_ref, g_ref, w_ref, dxi_ref, h_ref, da_ref, sums_ref, db_ref):
        i = pl.program_id(0)
        xv = x_ref[...]
        h_ref[...] = _rms(xv, g_ref[...]).astype(BF16)
        dh = jnp.zeros((tm, D), F32)
        dbs = [None] * ns
        for s in range(half):
            a_u = a_ref[:, s * cs:(s + 1) * cs]
            sg = _sigmoid(a_ref[:, (s + half) * cs:(s + half + 1) * cs])
            du_s = du_ref[:, s * cs:(s + 1) * cs]
            da_u = du_s * sg
            da_g = du_s * a_u * sg * (1.0 - sg)
            for s2, v in ((s, da_u), (s + half, da_g)):
                vb = v.astype(BF16)
                da_ref[:, s2 * cs:(s2 + 1) * cs] = vb
                dbs[s2] = jnp.sum(v, axis=0, keepdims=True)
                dh = dh + _dot_nt(vb, w_ref[s2])
        dxi, dgpre = _rms_bwd(xv, g_ref[...], dh)
        dxi_ref[...] = dxo_ref[...] + dxi

        @pl.when(i == 0)
        def _():
            sums_ref[...] = jnp.zeros_like(sums_ref)
            db_ref[...] = jnp.zeros_like(db_ref)
        sums_ref[0:1, :] += dgpre
        for s in range(ns):
            db_ref[:, s * cs:(s + 1) * cs] += dbs[s]

    return _pcall(
        body, name="conv_in_bwd", grid=(T // tm,),
        in_specs=[_rows(tm, D), _rows(tm, D), _rows(tm, 2 * D), _rows(tm, D), _full((1, D)),
                  _full(w1g.shape)],
        out_specs=[_rows(tm, D), _rows(tm, D), _rows(tm, 2 * D), _acc((SUBLANES, D)), _acc((1, 2 * D))],
        out_shape=[jax.ShapeDtypeStruct((T, D), F32), jax.ShapeDtypeStruct((T, D), BF16),
                   jax.ShapeDtypeStruct((T, 2 * D), BF16), jax.ShapeDtypeStruct((SUBLANES, D), F32),
                   jax.ShapeDtypeStruct((1, 2 * D), F32)],
        compiler_params=_params(("arbitrary",)),
    )(dxo, du, a, x, g_pre, w1g)


def _mlp_fwd(x, g_pre, wug, wdg, g_post, tm, name):
    T, D = x.shape
    ns, _, fs = wug.shape

    def body(x_ref, gp_ref, wu_ref, wd_ref, gq_ref, up_ref, m_ref, xo_ref):
        xv = x_ref[...]
        h = _rms(xv, gp_ref[...]).astype(BF16)
        acc = jnp.zeros((tm, D), F32)
        for s in range(ns):
            up = _dot(h, wu_ref[s]).astype(BF16)
            up_ref[:, s * fs:(s + 1) * fs] = up
            act = jnp.square(jnp.maximum(up.astype(F32), 0.0)).astype(BF16)
            acc = acc + _dot(act, wd_ref[s])
        m_ref[...] = acc
        xo_ref[...] = xv + _rms(acc, gq_ref[...])

    return _pcall(
        body, name=name, grid=(T // tm,),
        in_specs=[_rows(tm, D), _full((1, D)), _full(wug.shape), _full(wdg.shape), _full((1, D))],
        out_specs=[_rows(tm, ns * fs), _rows(tm, D), _rows(tm, D)],
        out_shape=[jax.ShapeDtypeStruct((T, ns * fs), BF16), jax.ShapeDtypeStruct((T, D), F32),
                   jax.ShapeDtypeStruct((T, D), F32)],
        compiler_params=_params(("arbitrary",)),
    )(x, g_pre, wug, wdg, g_post)


def _mlp_bwd(dxo, m, x, up, g_pre, wug, wdg, g_post, tm, name, dep):
    T, D = x.shape
    ns, _, fs = wug.shape

    def body(dxo_ref, m_ref, x_ref, up_ref, gp_ref, wu_ref, wd_ref, gq_ref, dep_ref,
             dxi_ref, h_ref, dm_ref, dup_ref, sums_ref):
        i = pl.program_id(0)
        dxo = dxo_ref[...]
        dm, dgpost = _rms_bwd(m_ref[...], gq_ref[...], dxo)
        dmb = dm.astype(BF16)
        dm_ref[...] = dmb
        xv = x_ref[...]
        h_ref[...] = _rms(xv, gp_ref[...]).astype(BF16)
        dh = jnp.zeros((tm, D), F32)
        for s in range(ns):
            dact = _dot_nt(dmb, wd_ref[s])
            up = up_ref[:, s * fs:(s + 1) * fs].astype(F32)
            dup = (dact * (2.0 * jnp.maximum(up, 0.0))).astype(BF16)
            dup_ref[:, s * fs:(s + 1) * fs] = dup
            dh = dh + _dot_nt(dup, wu_ref[s])
        dxi, dgpre = _rms_bwd(xv, gp_ref[...], dh)
        dxi_ref[...] = dxo + dxi

        @pl.when(i == 0)
        def _():
            sums_ref[...] = jnp.zeros_like(sums_ref)
        sums_ref[0:1, :] += dgpost
        sums_ref[1:2, :] += dgpre

    return _pcall(
        body, name=name, grid=(T // tm,),
        in_specs=[_rows(tm, D), _rows(tm, D), _rows(tm, D), _rows(tm, ns * fs), _full((1, D)),
                  _full(wug.shape), _full(wdg.shape), _full((1, D)), pl.BlockSpec(memory_space=pl.ANY)],
        out_specs=[_rows(tm, D), _rows(tm, D), _rows(tm, D), _rows(tm, ns * fs), _acc((SUBLANES, D))],
        out_shape=[jax.ShapeDtypeStruct((T, D), F32), jax.ShapeDtypeStruct((T, D), BF16),
                   jax.ShapeDtypeStruct((T, D), BF16), jax.ShapeDtypeStruct((T, ns * fs), BF16),
                   jax.ShapeDtypeStruct((SUBLANES, D), F32)],
        compiler_params=_params(("arbitrary",)),
    )(dxo, m, x, up, g_pre, wug, wdg, g_post, dep)


def _mm_tn(a, g, *, nj, a_cols, g_cols, a_by_j, g_by_j, tk, name, act=False):
    T = a.shape[0]
    nk = T // tk

    def body(a_ref, g_ref, o_ref, acc_ref):
        k = pl.program_id(1)
        av = a_ref[...]
        if act:
            av = jnp.square(jnp.maximum(av.astype(F32), 0.0)).astype(BF16)
        p = _dot_tn(av, g_ref[...])

        @pl.when(k == 0)
        def _():
            acc_ref[...] = p

        @pl.when(k > 0)
        def _():
            acc_ref[...] += p

        @pl.when(k == nk - 1)
        def _():
            o_ref[...] = acc_ref[...].astype(BF16)

    return _pcall(
        body, name=name, grid=(nj, nk),
        in_specs=[pl.BlockSpec((tk, a_cols), (lambda j, k: (k, j)) if a_by_j else (lambda j, k: (k, 0))),
                  pl.BlockSpec((tk, g_cols), (lambda j, k: (k, j)) if g_by_j else (lambda j, k: (k, 0)))],
        out_specs=pl.BlockSpec((None, a_cols, g_cols), lambda j, k: (j, 0, 0)),
        out_shape=jax.ShapeDtypeStruct((nj, a_cols, g_cols), BF16),
        scratch_shapes=[pltpu.VMEM((a_cols, g_cols), F32)],
        compiler_params=_params(("arbitrary", "arbitrary")),
    )(a, g)


def _attn_in_fwd(x, g_pre, wqkv, wf, bf, tm, q_mul, n_heads):
    T, D = x.shape

    def body(x_ref, g_ref, w_ref, wf_ref, bf_ref, q_ref, k_ref, v_ref, lf_ref):
        h = _rms(x_ref[...], g_ref[...]).astype(BF16)
        q = _dot(h, w_ref[:, 0:D])
        if q_mul != 1.0:
            q = q * q_mul
        q_ref[...] = q.astype(BF16)
        k_ref[...] = _dot(h, w_ref[:, D:2 * D]).astype(BF16)
        v_ref[...] = _dot(h, w_ref[:, 2 * D:3 * D]).astype(BF16)
        fl = _dot(h, wf_ref[...]) + bf_ref[...]
        lf = jnp.minimum(fl, 0.0) - jnp.log(1.0 + jnp.exp(-jnp.abs(fl)))
        lane = lax.broadcasted_iota(jnp.int32, (1, LANES), 1)
        lf_ref[...] = jnp.where(lane < n_heads, lf, 0.0)

    return _pcall(
        body, name="attn_in_fwd", grid=(T // tm,),
        in_specs=[_rows(tm, D), _full((1, D)), _full((D, 3 * D)), _full((D, LANES)), _full((1, LANES))],
        out_specs=[_rows(tm, D), _rows(tm, D), _rows(tm, D), _rows(tm, LANES)],
        out_shape=[jax.ShapeDtypeStruct((T, D), BF16)] * 3 + [jax.ShapeDtypeStruct((T, LANES), F32)],
        compiler_params=_params(("arbitrary",)),
    )(x, g_pre, wqkv, wf, bf)


def _cumsum_rows(v, v2, tb, reverse, name):
    T, C = v.shape
    nb = T // tb

    def body(v_ref, v2_ref, o_ref, carry_ref):
        i = pl.program_id(0)

        @pl.when(i == 0)
        def _():
            carry_ref[...] = jnp.zeros_like(carry_ref)
        r = lax.broadcasted_iota(jnp.int32, (tb, tb), 0)
        c = lax.broadcasted_iota(jnp.int32, (tb, tb), 1)
        tri = jnp.where((c >= r) if reverse else (c <= r), 1.0, 0.0).astype(F32)
        out = jnp.dot(tri, v_ref[...] + v2_ref[...], precision=lax.Precision.HIGHEST,
                      preferred_element_type=F32) + carry_ref[...]
        o_ref[...] = out
        carry_ref[...] = out[0:1, :] if reverse else out[tb - 1:tb, :]

    idx = (lambda i: (nb - 1 - i, 0)) if reverse else (lambda i: (i, 0))
    return _pcall(
        body, name=name, grid=(nb,),
        in_specs=[pl.BlockSpec((tb, C), idx), pl.BlockSpec((tb, C), idx)],
        out_specs=pl.BlockSpec((tb, C), idx),
        out_shape=jax.ShapeDtypeStruct((T, C), F32),
        scratch_shapes=[pltpu.VMEM((1, C), F32)],
        compiler_params=_params(("arbitrary",)),
    )(v, v2)


def _head_col(v, lane, h):
    return jnp.sum(jnp.where(lane == h, v, 0.0), axis=1, keepdims=True)


def _flash_fwd(q, k, v, fq_aux, fk_rows, *, dh, tq, s_mul):
    T, D = q.shape
    G = D // LANES
    hpg = LANES // dh
    nq = T // tq
    k3 = k.reshape(nq, tq, D)
    v3 = v.reshape(nq, tq, D)

    rc = min(FLASH_ROWS, tq)

    def body(q_ref, k_ref, v_ref, fq_ref, fk_ref, o_ref, o32_ref, lse_ref, s_scr, p_scr):
        i = pl.program_id(1)
        lane = lax.broadcasted_iota(jnp.int32, (1, LANES), 1)
        q2 = q_ref[...]
        hmasks = [(lane >= hh * dh) & (lane < (hh + 1) * dh) for hh in range(hpg)]
        qms = [jnp.where(hm, q2, jnp.zeros_like(q2)) for hm in hmasks]

        nlb = tq // LANES
        sum_lane = [((hh + 1) % hpg) * dh for hh in range(hpg)]

        def scores(j, slot):
            kj = k_ref[j]
            for hh in range(hpg):
                s = _dot_nt(qms[hh], kj)
                s_scr[slot, hh] = s if s_mul == 1.0 else s * s_mul

        def soft(j, slot, carry, masked):
            vj = v_ref[j]
            out = []
            for hh in range(hpg):
                m_b, acc = carry[hh]
                fk_row = fk_ref[j, hh:hh + 1, :]
                def future(r0, cb):
                    return masked and cb * LANES > r0 + rc - 1

                mx = []
                for r0 in range(0, tq, rc):
                    rs = slice(r0, r0 + rc)
                    c = None
                    for cb in range(nlb):
                        if future(r0, cb):
                            continue
                        cs = slice(cb * LANES, (cb + 1) * LANES)
                        s = s_scr[slot, hh, rs, cs] - fk_row[:, cs]
                        if masked and (cb + 1) * LANES - 1 > r0:
                            ri = r0 + lax.broadcasted_iota(jnp.int32, (rc, LANES), 0)
                            ci = cb * LANES + lax.broadcasted_iota(jnp.int32, (rc, LANES), 1)
                            s = jnp.where(ci <= ri, s, MASK_VALUE)
                        s_scr[slot, hh, rs, cs] = s
                        c = s if c is None else jnp.maximum(c, s)
                    mx.append(c)
                row_max = jnp.max(jnp.concatenate(mx, axis=0), axis=1, keepdims=True)
                m_new = jnp.maximum(m_b, row_max)
                alpha = jnp.exp(m_b - m_new)
                for r0 in range(0, tq, rc):
                    rs = slice(r0, r0 + rc)
                    m_c = m_new[rs]
                    for cb in range(nlb):
                        cs = slice(cb * LANES, (cb + 1) * LANES)
                        if future(r0, cb):
                            p_scr[hh, rs, cs] = jnp.zeros((rc, LANES), BF16)
                        else:
                            p_scr[hh, rs, cs] = jnp.exp(s_scr[slot, hh, rs, cs] - m_c).astype(BF16)
                v_one = jnp.where(hmasks[hh], vj, jnp.ones_like(vj))
                out.append((m_new, alpha * acc + _dot(p_scr[hh], v_one)))
            return tuple(out)

        def step(j, carry, masked):
            scores(j, 0)
            return soft(j, 0, carry, masked)

        init = tuple((jnp.full((tq, LANES), MASK_VALUE, F32), jnp.zeros((tq, LANES), F32)) for _ in range(hpg))
        carry = lax.fori_loop(0, i, lambda j, cr: step(j, cr, False), init)
        carry = step(i, carry, True)
        fq = fq_ref[...]
        g0 = pl.program_id(0) * hpg
        o_all = jnp.zeros((tq, LANES), F32)
        lse_all = jnp.zeros((tq, LANES), F32)
        for hh in range(hpg):
            m_b, acc = carry[hh]
            l = acc[:, sum_lane[hh]:sum_lane[hh] + 1]
            o_all = jnp.where(hmasks[hh], acc * (1.0 / l), o_all)
            lse_all = jnp.where(lane == hh, m_b[:, 0:1] + jnp.log(l) + _head_col(fq, lane, g0 + hh), lse_all)
        o_ref[...] = o_all.astype(BF16)
        o32_ref[...] = o_all
        lse_ref[...] = lse_all

    return _pcall(
        body, name="flash_fwd", grid=(G, nq),
        in_specs=[pl.BlockSpec((tq, LANES), lambda g, i: (i, g)),
                  pl.BlockSpec((nq, tq, LANES), lambda g, i: (0, 0, g)),
                  pl.BlockSpec((nq, tq, LANES), lambda g, i: (0, 0, g)),
                  pl.BlockSpec((tq, LANES), lambda g, i: (i, 0)),
                  pl.BlockSpec((None, nq, SUBLANES, tq), lambda g, i: (g, 0, 0, 0))],
        out_specs=[pl.BlockSpec((tq, LANES), lambda g, i: (i, g)),
                   pl.BlockSpec((tq, LANES), lambda g, i: (i, g)),
                   pl.BlockSpec((None, tq, LANES), lambda g, i: (g, i, 0))],
        out_shape=[jax.ShapeDtypeStruct((T, D), BF16), jax.ShapeDtypeStruct((T, D), F32),
                   jax.ShapeDtypeStruct((G, T, LANES), F32)],
        scratch_shapes=[pltpu.VMEM((1, hpg, tq, tq), F32), pltpu.VMEM((hpg, tq, tq), BF16)],
        compiler_params=_params(("arbitrary", "arbitrary")),
    )(q, k3, v3, fq_aux, fk_rows)


def _flash_bwd(q, k, v, do, fq_aux, lse_aux, dl_aux, fk_rows, *, dh, tq, tk, s_mul, dq_mul):
    T, D = q.shape
    G = D // LANES
    hpg = LANES // dh
    nq = T // tq
    nk = T // tk
    per = tq // tk
    k3 = k.reshape(nk, tk, D)
    v3 = v.reshape(nk, tk, D)
    rc = min(FLASH_ROWS, tq)

    def body(q_ref, k_ref, v_ref, do_ref, fq_ref, lse_ref, dl_ref, fk_ref,
             dq_ref, dk_ref, dv_ref, dfq_ref, dfk_ref, dk_acc, dv_acc, s_scr, dp_scr, p_scr, ds_scr):
        i = pl.program_id(1)

        @pl.when(i == 0)
        def _():
            dk_acc[...] = jnp.zeros_like(dk_acc)
            dv_acc[...] = jnp.zeros_like(dv_acc)
            dfk_ref[...] = jnp.zeros_like(dfk_ref)

        lane = lax.broadcasted_iota(jnp.int32, (1, LANES), 1)
        q2 = q_ref[...]
        do2 = do_ref[...]
        g0 = pl.program_id(0) * hpg
        fq = fq_ref[...]
        lse = lse_ref[...]
        dl = dl_ref[...]
        hmasks = [(lane >= hh * dh) & (lane < (hh + 1) * dh) for hh in range(hpg)]
        qms = [jnp.where(hm, q2, jnp.zeros_like(q2)) for hm in hmasks]
        doms = [jnp.where(hm, do2, jnp.zeros_like(do2)) for hm in hmasks]
        q2t = q2.astype(F32).T.astype(BF16)
        do2t = do2.astype(F32).T.astype(BF16)
        c_bs = [jnp.broadcast_to(_head_col(fq, lane, g0 + hh) - lse[:, hh:hh + 1], (tq, LANES))
                for hh in range(hpg)]
        dl_bs = [jnp.broadcast_to(_head_col(dl, lane, g0 + hh), (tq, LANES)) for hh in range(hpg)]
        nlb = tk // LANES

        def step(j, carry, off):
            masked = off is not None
            kj = k_ref[j]
            vj = v_ref[j]
            for hh in range(hpg):
                s = _dot_nt(qms[hh], kj)
                s_scr[hh] = s if s_mul == 1.0 else s * s_mul
                dp_scr[hh] = _dot_nt(doms[hh], vj)
            out = []
            for hh in range(hpg):
                dq_acc, rs_p = carry[hh]
                fk_row = fk_ref[j, hh:hh + 1, :]
                rsums = []
                csums = [jnp.zeros((SUBLANES, LANES), F32) for _ in range(nlb)]
                for r0 in range(0, tq, rc):
                    rs = slice(r0, r0 + rc)
                    if masked and r0 + rc <= off:
                        p_scr[hh, rs, :] = jnp.zeros((rc, tk), BF16)
                        ds_scr[hh, rs, :] = jnp.zeros((rc, tk), BF16)
                        rsums.append(jnp.zeros((rc, LANES), F32))
                        continue
                    c_c = c_bs[hh][rs]
                    dl_c = dl_bs[hh][rs]
                    tot = jnp.zeros((rc, LANES), F32)
                    for cb in range(nlb):
                        cs = slice(cb * LANES, (cb + 1) * LANES)
                        if masked and off + cb * LANES > r0 + rc - 1:
                            p_scr[hh, rs, cs] = jnp.zeros((rc, LANES), BF16)
                            ds_scr[hh, rs, cs] = jnp.zeros((rc, LANES), BF16)
                            continue
                        e = (s_scr[hh, rs, cs] - fk_row[:, cs]) + c_c
                        if masked and off + (cb + 1) * LANES - 1 > r0:
                            ri = r0 + lax.broadcasted_iota(jnp.int32, (rc, LANES), 0)
                            ci = off + cb * LANES + lax.broadcasted_iota(jnp.int32, (rc, LANES), 1)
                            e = jnp.where(ci <= ri, e, MASK_VALUE)
                        p = jnp.exp(e)
                        ds = p * (dp_scr[hh, rs, cs] - dl_c)
                        p_scr[hh, rs, cs] = p.astype(BF16)
                        ds_scr[hh, rs, cs] = ds.astype(BF16)
                        tot = tot + ds
                        csums[cb] = csums[cb] + ds.reshape(rc // SUBLANES, SUBLANES, LANES).sum(axis=0)
                    rsums.append(tot)
                col = jnp.concatenate([jnp.sum(c, axis=0, keepdims=True) for c in csums], axis=1)
                dfk_ref[j, hh:hh + 1, :] += -col
                out.append((dq_acc + _dot(ds_scr[hh], kj), rs_p + jnp.concatenate(rsums, axis=0)))
            dk_acc[j] += jnp.concatenate(
                [_dot(q2t[hh * dh:(hh + 1) * dh], ds_scr[hh]) for hh in range(hpg)], axis=0)
            dv_acc[j] += jnp.concatenate(
                [_dot(do2t[hh * dh:(hh + 1) * dh], p_scr[hh]) for hh in range(hpg)], axis=0)
            return tuple(out)

        init = tuple((jnp.zeros((tq, LANES), F32), jnp.zeros((tq, LANES), F32)) for _ in range(hpg))
        carry = lax.fori_loop(0, i * per, lambda j, cr: step(j, cr, None), init)
        for d in range(per):
            carry = step(i * per + d, carry, d * tk)
        dq_all = jnp.zeros((tq, LANES), F32)
        dfq_all = jnp.zeros((tq, LANES), F32)
        for hh in range(hpg):
            dq_h, rs_p = carry[hh]
            dq_all = jnp.where(hmasks[hh], dq_h, dq_all)
            dfq_all = jnp.where(lane == hh, jnp.sum(rs_p, axis=1, keepdims=True), dfq_all)
        dq_ref[...] = (dq_all * dq_mul).astype(BF16)
        dfq_ref[...] = dfq_all

        @pl.when(i == nq - 1)
        def _():
            for jj in range(nk):
                dkv = dk_acc[jj].T
                if s_mul != 1.0:
                    dkv = dkv * s_mul
                dk_ref[jj] = dkv.astype(BF16)
                dv_ref[jj] = dv_acc[jj].T.astype(BF16)

    blk = pl.BlockSpec((tq, LANES), lambda g, i: (i, g))
    res = pl.BlockSpec((nk, tk, LANES), lambda g, i: (0, 0, g))
    res_in = pl.BlockSpec((nk, tk, LANES), lambda g, i: (0, 0, g), pipeline_mode=pl.Buffered(1))
    aux = pl.BlockSpec((None, tq, LANES), lambda g, i: (g, i, 0))
    heads = pl.BlockSpec((tq, LANES), lambda g, i: (i, 0))
    rows = pl.BlockSpec((None, nk, SUBLANES, tk), lambda g, i: (g, 0, 0, 0))
    dq, dk3, dv3, dfq, dfk = _pcall(
        body, name="flash_bwd", grid=(G, nq),
        in_specs=[blk, res_in, res_in, blk, heads, aux, heads, rows],
        out_specs=[blk, res, res, aux, rows],
        out_shape=[jax.ShapeDtypeStruct((T, D), BF16), jax.ShapeDtypeStruct((nk, tk, D), BF16),
                   jax.ShapeDtypeStruct((nk, tk, D), BF16), jax.ShapeDtypeStruct((G, T, LANES), F32),
                   jax.ShapeDtypeStruct((G, nk, SUBLANES, tk), F32)],
        scratch_shapes=[pltpu.VMEM((nk, LANES, tk), F32), pltpu.VMEM((nk, LANES, tk), F32),
                        pltpu.VMEM((hpg, tq, tk), F32), pltpu.VMEM((hpg, tq, tk), F32),
                        pltpu.VMEM((hpg, tq, tk), BF16), pltpu.VMEM((hpg, tq, tk), BF16)],
        compiler_params=_params(("arbitrary", "arbitrary")),
    )(q, k3, v3, do, fq_aux, lse_aux, dl_aux, fk_rows)
    return dq, dk3.reshape(T, D), dv3.reshape(T, D), dfq, dfk


def _attn_out_fwd(o, x, wo, g_post, tm):
    T, D = x.shape

    def body(o_ref, x_ref, w_ref, g_ref, m_ref, xo_ref):
        m = _dot(o_ref[...], w_ref[...])
        m_ref[...] = m
        xo_ref[...] = x_ref[...] + _rms(m, g_ref[...])

    return _pcall(
        body, name="attn_out_fwd", grid=(T // tm,),
        in_specs=[_rows(tm, D), _rows(tm, D), _full((D, D)), _full((1, D))],
        out_specs=[_rows(tm, D), _rows(tm, D)],
        out_shape=[jax.ShapeDtypeStruct((T, D), F32), jax.ShapeDtypeStruct((T, D), F32)],
        compiler_params=_params(("arbitrary",)),
    )(o, x, wo, g_post)


def _attn_out_bwd(dxo, m, o, wo, g_post, head_ind, tm, dep):
    T, D = m.shape

    def body(dxo_ref, m_ref, o_ref, w_ref, g_ref, ind_ref, dep_ref, dm_ref, do_ref, dl_ref, sums_ref):
        i = pl.program_id(0)
        dm, dgpost = _rms_bwd(m_ref[...], g_ref[...], dxo_ref[...])
        dmb = dm.astype(BF16)
        dm_ref[...] = dmb
        dob = _dot_nt(dmb, w_ref[...]).astype(BF16)
        do_ref[...] = dob
        dl_ref[...] = jnp.dot(dob.astype(F32) * o_ref[...], ind_ref[...], precision=lax.Precision.HIGHEST,
                              preferred_element_type=F32)

        @pl.when(i == 0)
        def _():
            sums_ref[...] = jnp.zeros_like(sums_ref)
        sums_ref[0:1, :] += dgpost

    return _pcall(
        body, name="attn_out_bwd", grid=(T // tm,),
        in_specs=[_rows(tm, D), _rows(tm, D), _rows(tm, D), _full((D, D)), _full((1, D)), _full((D, LANES)),
                  pl.BlockSpec(memory_space=pl.ANY)],
        out_specs=[_rows(tm, D), _rows(tm, D), _rows(tm, LANES), _acc((SUBLANES, D))],
        out_shape=[jax.ShapeDtypeStruct((T, D), BF16), jax.ShapeDtypeStruct((T, D), BF16),
                   jax.ShapeDtypeStruct((T, LANES), F32), jax.ShapeDtypeStruct((SUBLANES, D), F32)],
        compiler_params=_params(("arbitrary",)),
    )(dxo, m, o, wo, g_post, head_ind, dep)


def _attn_in_bwd(dxo, x, g_pre, dq, dk, dv, dlf, lf, wqkv, wf, tm, n_heads):
    T, D = x.shape

    def body(dxo_ref, x_ref, g_ref, dq_ref, dk_ref, dv_ref, dlf_ref, lf_ref, w_ref, wf_ref,
             dxi_ref, h_ref, df_ref, sums_ref, dbf_ref):
        i = pl.program_id(0)
        xv = x_ref[...]
        h_ref[...] = _rms(xv, g_ref[...]).astype(BF16)
        lane = lax.broadcasted_iota(jnp.int32, (1, LANES), 1)
        df = jnp.where(lane < n_heads, dlf_ref[...] * (1.0 - jnp.exp(lf_ref[...])), 0.0)
        dfb = df.astype(BF16)
        df_ref[...] = dfb
        dh = (_dot_nt(dq_ref[...], w_ref[:, 0:D]) + _dot_nt(dk_ref[...], w_ref[:, D:2 * D])
              + _dot_nt(dv_ref[...], w_ref[:, 2 * D:3 * D]) + _dot_nt(dfb, wf_ref[...]))
        dxi, dgpre = _rms_bwd(xv, g_ref[...], dh)
        dxi_ref[...] = dxo_ref[...] + dxi

        @pl.when(i == 0)
        def _():
            sums_ref[...] = jnp.zeros_like(sums_ref)
            dbf_ref[...] = jnp.zeros_like(dbf_ref)
        sums_ref[0:1, :] += dgpre
        dbf_ref[...] += jnp.sum(df, axis=0, keepdims=True)

    return _pcall(
        body, name="attn_in_bwd", grid=(T // tm,),
        in_specs=[_rows(tm, D), _rows(tm, D), _full((1, D)), _rows(tm, D), _rows(tm, D), _rows(tm, D),
                  _rows(tm, LANES), _rows(tm, LANES), _full((D, 3 * D)), _full((D, LANES))],
        out_specs=[_rows(tm, D), _rows(tm, D), _rows(tm, LANES), _acc((SUBLANES, D)), _acc((1, LANES))],
        out_shape=[jax.ShapeDtypeStruct((T, D), F32), jax.ShapeDtypeStruct((T, D), BF16),
                   jax.ShapeDtypeStruct((T, LANES), BF16), jax.ShapeDtypeStruct((SUBLANES, D), F32),
                   jax.ShapeDtypeStruct((1, LANES), F32)],
        compiler_params=_params(("arbitrary",)),
    )(dxo, x, g_pre, dq, dk, dv, dlf, lf, wqkv, wf)


def _loss_head(y, target, tm):
    T, D = y.shape

    def body(y_ref, t_ref, dy_ref, loss_ref):
        i = pl.program_id(0)
        err = y_ref[...] - t_ref[...]
        dy_ref[...] = err * (1.0 / D)
        part = 0.5 * jnp.sum(jnp.mean(err * err, axis=-1, keepdims=True), axis=0, keepdims=True)

        @pl.when(i == 0)
        def _():
            loss_ref[...] = jnp.zeros_like(loss_ref)
        loss_ref[...] += part

    return _pcall(
        body, name="loss_head", grid=(T // tm,),
        in_specs=[_rows(tm, D), _rows(tm, D)],
        out_specs=[_rows(tm, D), _acc((SUBLANES, LANES))],
        out_shape=[jax.ShapeDtypeStruct((T, D), F32), jax.ShapeDtypeStruct((SUBLANES, LANES), F32)],
        compiler_params=_params(("arbitrary",)),
    )(y, target)


def _adamw(recvs, w, m, v, tr, name):
    L, R, C = w.shape
    assert len(recvs) == L
    c1 = 1.0 - ADAM_B1 ** ADAM_STEP
    c2 = 1.0 - ADAM_B2 ** ADAM_STEP

    def body(*refs):
        r_refs = refs[:L]
        w_ref, m_ref, v_ref, g_ref, d_ref, nm_ref, nv_ref = refs[L:]
        layer = pl.program_id(0)
        g = None
        for k in range(L):
            gk = r_refs[k][0, :, 0:C].astype(F32)
            for s in range(1, N_DEV):
                gk = gk + r_refs[k][s, :, 0:C].astype(F32)
            g = gk if g is None else jnp.where(layer == k, gk, g)
        nm = ADAM_B1 * m_ref[...] + (1.0 - ADAM_B1) * g
        nv = ADAM_B2 * v_ref[...] + (1.0 - ADAM_B2) * jnp.square(g)
        m_hat = nm / c1
        v_hat = nv / c2
        g_ref[...] = g
        d_ref[...] = -ADAM_LR * (m_hat / (jnp.sqrt(v_hat) + ADAM_EPS) + ADAM_WD * w_ref[...])
        nm_ref[...] = nm
        nv_ref[...] = nv

    def recv_spec(k):
        return pl.BlockSpec((N_DEV, tr, recvs[k].shape[-1]), lambda l, i: (0, jnp.where(l == k, i, 0), 0))

    blk = pl.BlockSpec((None, tr, C), lambda l, i: (l, i, 0))
    return _pcall(
        body, name=name, grid=(L, R // tr),
        in_specs=[recv_spec(k) for k in range(L)] + [blk] * 3,
        out_specs=[blk] * 4,
        out_shape=[jax.ShapeDtypeStruct((L, R, C), F32)] * 4,
        compiler_params=_params(("arbitrary", "arbitrary")),
    )(*recvs, w, m, v)


def _row_block(rows, cols):
    cap = max(SUBLANES, (256 * 1024) // max(cols, 1))
    best = None
    for t in range(SUBLANES, rows + 1, SUBLANES):
        if rows % t == 0 and t <= cap:
            best = t
    return rows if best is None else best


def kernel(x, g_mix_pre, g_mix_post, g_ffn_pre, g_ffn_post, conv_pw1_w, conv_pw1_b, conv_dw_w, conv_dw_b, conv_ln_g, conv_ln_b, conv_pw2_w, conv_pw2_b, attn_w_in, attn_b_f, attn_w_o, mlp_w_up, mlp_w_down, loss_target, m_g_mix_pre, m_g_mix_post, m_g_ffn_pre, m_g_ffn_post, m_conv_pw1_w, m_conv_pw1_b, m_conv_dw_w, m_conv_dw_b, m_conv_ln_g, m_conv_ln_b, m_conv_pw2_w, m_conv_pw2_b, m_attn_w_in, m_attn_b_f, m_attn_w_o, m_mlp_w_up, m_mlp_w_down, v_g_mix_pre, v_g_mix_post, v_g_ffn_pre, v_g_ffn_post, v_conv_pw1_w, v_conv_pw1_b, v_conv_dw_w, v_conv_dw_b, v_conv_ln_g, v_conv_ln_b, v_conv_pw2_w, v_conv_pw2_b, v_attn_w_in, v_attn_b_f, v_attn_w_o, v_mlp_w_up, v_mlp_w_down):
    _, T, D = x.shape
    H = attn_b_f.shape[-1]
    dh = D // H
    width = conv_dw_w.shape[1]
    cin = attn_w_in.shape[-1]
    fs = mlp_w_up.shape[-1]
    G = D // LANES
    hpg = LANES // dh
    assert T % 4 == 0 and D % LANES == 0 and LANES % dh == 0 and width <= CONV_HALO and H <= LANES

    tm = min(512, T // 4)
    tmb = min(256, T // 4)
    tqf = min(1024, T // 4)
    tkb = tm
    tmc = min(256, T // 4)
    lc = min(256, D)
    tkw = min(2048, T // 4)
    tb = min(256, T // 4)

    scale = float(dh) ** -0.5
    mant, _ = math.frexp(scale)
    q_mul = scale if mant == 0.5 else 1.0
    s_mul = 1.0 if mant == 0.5 else scale

    x2 = x.reshape(T, D)
    tgt = loss_target.reshape(T, D)

    w_srcs = [conv_pw1_w, conv_dw_w, conv_pw2_w, mlp_w_up, mlp_w_down, attn_w_in, attn_w_o]
    w_items = [(0, 0, "whole"), (1, 0, "whole"), (2, 0, "whole"), (3, 0, "whole"), (4, 0, "whole"),
               (5, 0, "whole"), (6, 0, "whole"), (3, 1, "whole"), (4, 1, "whole")]
    cin_w = -(-cin // LANES) * LANES
    w_lands = _place_own(
        w_srcs, w_items,
        [((N_DEV, D, cin_w) if si == 5 else (N_DEV,) + w_srcs[si].shape[1:], F32 if si == 1 else BF16)
         for si, _, _ in w_items],
        "stage_weights", cast=True)
    me_arr = _dev_index(*_mesh_pos()).astype(jnp.int32).reshape(1)
    w_groups = [[0], [1, 2], [3, 4], [5, 6], [7, 8]]
    g_sems, _, w_lands, g_token = _push_start(
        [], w_lands, [[(a, None, "own") for a in grp] for grp in w_groups], "gather_start")

    def gather_wait(gi, after):
        grp = w_groups[gi]
        return _push_wait(g_sems[gi], [], [w_lands[a] for a in grp], [(k, None, "own") for k in range(len(grp))],
                          after, "gather_wait%d" % gi)

    (w1g,) = gather_wait(0, g_token)
    bf = jnp.pad(attn_b_f, ((0, 0), (0, LANES - H)))

    row = lambda a, i: a[i:i + 1]

    a0, u0 = _conv_in_fwd(x2, row(g_mix_pre, 0), w1g, conv_pw1_b, tm)
    dwg, w2g = gather_wait(1, u0)
    w2 = w2g.reshape(D, D)
    dw_full = jnp.transpose(dwg, (1, 0, 2)).reshape(width, D)
    w32 = jnp.pad(dw_full, ((0, 32 - width), (0, 0)))
    y0 = _dwconv_fwd(u0, w32, conv_dw_b, tmc, lc, width)
    m0, x_1 = _conv_out_fwd(y0, x2, conv_ln_g, conv_ln_b, w2, conv_pw2_b, row(g_mix_post, 0), tm)
    wu0, wd0 = gather_wait(2, x_1)
    up0, n0, x_2 = _mlp_fwd(x_1, row(g_ffn_pre, 0), wu0, wd0, row(g_ffn_post, 0), tm, "mlp0_fwd")

    wing, wog = gather_wait(3, x_2)
    wo = wog.reshape(D, D)
    win = jnp.transpose(wing[:, :, :cin], (1, 0, 2)).reshape(D, N_DEV * cin)
    wqkv = win[:, :3 * D]
    wf = jnp.pad(win[:, 3 * D:], ((0, 0), (0, LANES - H)))
    q, k, v, lf = _attn_in_fwd(x_2, row(g_mix_pre, 1), wqkv, wf, bf, tm, q_mul, H)
    fcum = _cumsum_rows(lf, jnp.zeros_like(lf), tb, False, "forget_cumsum")


    def key_rows(blk):
        r = jnp.transpose(fcum[:, :H].T.reshape(G, hpg, T // blk, blk), (0, 2, 1, 3))
        return jnp.pad(r, ((0, 0), (0, 0), (0, SUBLANES - hpg), (0, 0)))

    fk_rows = key_rows(tkb)
    o, o32, lse_aux = _flash_fwd(q, k, v, fcum, key_rows(tqf), dh=dh, tq=tqf, s_mul=s_mul)
    m1, x_3 = _attn_out_fwd(o, x_2, wo, row(g_mix_post, 1), tm)
    wu1, wd1 = gather_wait(4, x_3)
    up1, n1, x_4 = _mlp_fwd(x_3, row(g_ffn_pre, 1), wu1, wd1, row(g_ffn_post, 1), tm, "mlp1_fwd")

    dx, loss_blk = _loss_head(x_4, tgt, tm)

    def mlp_back(dx, n_l, x_in, up_l, l, wu, wd, dep):
        dxi, h, dm, dup, sums = _mlp_bwd(dx, n_l, x_in, up_l, row(g_ffn_pre, l), wu, wd, row(g_ffn_post, l),
                                         tmb, "mlp%d_bwd" % l, dep)
        dwu = _mm_tn(h, dup, nj=N_DEV, a_cols=D, g_cols=fs, a_by_j=False, g_by_j=True, tk=tkw,
                     name="mlp%d_dwu" % l)
        dwd = _mm_tn(up_l, dm, nj=N_DEV, a_cols=fs, g_cols=D, a_by_j=True, g_by_j=False, tk=tkw,
                     name="mlp%d_dwd" % l, act=True)
        return dxi, dwu, dwd, sums

    def push_grads(srcs, modes, name):
        items = [(i, None, mode) for i, mode in enumerate(modes)]
        lands = _seed_lands(srcs, modes, me_arr, name + "_own")
        sems, srcs_t, lands_t, token = _push_start(srcs, lands, [items], name + "_start")
        return (sems[0], srcs_t, lands_t, items), token

    def pull_grads(handle, after, name):
        sems, srcs_t, lands_t, items = handle
        return _push_wait(sems, srcs_t, lands_t, items, after, name + "_wait")

    rs = D // N_DEV
    dx, dwu1, dwd1, s_mlp1 = mlp_back(dx, n1, x_3, up1, 1, wu1, wd1, loss_blk)
    h_mlp1, tok = push_grads([dwu1, dwd1], ["slot", "slot"], "grads_mlp1")

    head_ind = jnp.asarray((np.arange(D)[:, None] // dh == np.arange(LANES)[None, :]).astype(np.float32))
    dm1, do, delta, s_ao = _attn_out_bwd(dx, m1, o32, wo, row(g_mix_post, 1), head_ind, tm, tok)
    dwo = _mm_tn(o, dm1, nj=1, a_cols=D, g_cols=D, a_by_j=False, g_by_j=False, tk=tkw, name="attn_dwo")
    dq, dk, dv, dfq, dfk = _flash_bwd(q, k, v, do, fcum, lse_aux, delta, fk_rows,
                                      dh=dh, tq=tkb, tk=tkb, s_mul=s_mul, dq_mul=scale)
    df_k = jnp.pad(jnp.transpose(dfk[:, :, :hpg, :], (0, 2, 1, 3)).reshape(H, T).T, ((0, 0), (0, LANES - H)))
    df_q = jnp.pad(jnp.transpose(dfq[:, :, :hpg], (1, 0, 2)).reshape(T, H), ((0, 0), (0, LANES - H)))
    dlf = _cumsum_rows(df_q, df_k, tb, True, "forget_cumsum_bwd")
    dx, h_at, df, s_ai, dbf = _attn_in_bwd(dx, x_2, row(g_mix_pre, 1), dq, dk, dv, dlf, lf, wqkv, wf, tm, H)
    dwq = _mm_tn(h_at, dq, nj=1, a_cols=D, g_cols=D, a_by_j=False, g_by_j=False, tk=tkw, name="attn_dwq")
    dwk = _mm_tn(h_at, dk, nj=1, a_cols=D, g_cols=D, a_by_j=False, g_by_j=False, tk=tkw, name="attn_dwk")
    dwv = _mm_tn(h_at, dv, nj=1, a_cols=D, g_cols=D, a_by_j=False, g_by_j=False, tk=tkw, name="attn_dwv")
    dwf = _mm_tn(h_at, df, nj=1, a_cols=D, g_cols=LANES, a_by_j=False, g_by_j=False, tk=tkw, name="attn_dwf")
    dwin = jnp.concatenate([dwq[0], dwk[0], dwv[0], dwf[0][:, :H]], axis=1)
    dwin = jnp.pad(jnp.transpose(dwin.reshape(D, N_DEV, cin), (1, 0, 2)), ((0, 0), (0, 0), (0, cin_w - cin)))
    h_attn, tok = push_grads([dwin, dwo.reshape(N_DEV, rs, D)], ["slot", "slot"], "grads_attn")

    dx, dwu0, dwd0, s_mlp0 = mlp_back(dx, n0, x_1, up0, 0, wu0, wd0, tok)
    h_mlp0, tok = push_grads([dwu0, dwd0], ["slot", "slot"], "grads_mlp0")

    dy0, dm0, z0, s_co = _conv_out_bwd(dx, m0, y0, conv_ln_g, conv_ln_b, w2, row(g_mix_post, 0), tm, tok)
    dw2 = _mm_tn(z0, dm0, nj=1, a_cols=D, g_cols=D, a_by_j=False, g_by_j=False, tk=tkw, name="conv_dw2")
    du0, ddw = _dwconv_bwd(dy0, u0, w32, tmc, lc, width)
    grad_x, h_cv, da0, s_ci, db1 = _conv_in_bwd(dx, du0, a0, x2, row(g_mix_pre, 0), w1g, tm)
    dw1 = _mm_tn(h_cv, da0, nj=N_DEV, a_cols=D, g_cols=(2 * D) // N_DEV, a_by_j=False, g_by_j=True, tk=tkw,
                 name="conv_dw1")
    ddw_s = jnp.transpose(ddw[:width].reshape(width, N_DEV, D // N_DEV), (1, 0, 2))

    def pad_row(a):
        return jnp.pad(a, ((0, 0), (0, D - a.shape[1])))

    def pack(gmp, gmq, gfp, gfq, b1, dwb, lng, lnb, b2, bfv, last):
        return jnp.concatenate([gmp, gmq, gfp, gfq, b1.reshape(2, D), dwb, lng, lnb, b2, pad_row(bfv), last],
                               axis=0)

    zero_row = jnp.zeros((1, D), F32)
    small_g = pack(
        jnp.concatenate([row(s_ci, 0), row(s_ai, 0)], axis=0),
        jnp.concatenate([row(s_co, 0), row(s_ao, 0)], axis=0),
        jnp.concatenate([row(s_mlp0, 1), row(s_mlp1, 1)], axis=0),
        jnp.concatenate([row(s_mlp0, 0), row(s_mlp1, 0)], axis=0),
        db1, row(s_co, 4), row(s_co, 1), row(s_co, 2), row(s_co, 3), dbf[:, :H],
        pad_row(loss_blk[0:1, 0:1]))
    h_conv, tok = push_grads([dw1, ddw_s, dw2.reshape(N_DEV, rs, D), small_g], ["slot", "slot", "slot", "whole"],
                             "grads_conv")

    def opt(recvs, w, m, v, name):
        shp = w.shape
        L, C = shp[0], shp[-1]
        R = int(np.prod(shp[1:-1]))
        outs = _adamw([r.reshape(N_DEV, R, r.shape[-1]) for r in recvs], w.reshape(L, R, C), m.reshape(L, R, C),
                      v.reshape(L, R, C), _row_block(R, C), name)
        return [t.reshape(shp) for t in outs]

    big = {}
    r_wu1, r_wd1 = pull_grads(h_mlp1, tok, "grads_mlp1")
    r_win, r_wo = pull_grads(h_attn, r_wd1, "grads_attn")
    big["attn_w_in"] = opt([r_win], attn_w_in, m_attn_w_in, v_attn_w_in, "adamw_win")
    big["attn_w_o"] = opt([r_wo], attn_w_o, m_attn_w_o, v_attn_w_o, "adamw_wo")
    r_wu0, r_wd0 = pull_grads(h_mlp0, big["attn_w_o"][0], "grads_mlp0")
    big["mlp_w_up"] = opt([r_wu0, r_wu1], mlp_w_up, m_mlp_w_up, v_mlp_w_up, "adamw_wup")
    big["mlp_w_down"] = opt([r_wd0, r_wd1], mlp_w_down, m_mlp_w_down, v_mlp_w_down, "adamw_wdown")
    r_w1, r_dw, r_w2, r_small = pull_grads(h_conv, big["mlp_w_down"][0], "grads_conv")
    big["conv_pw1_w"] = opt([r_w1], conv_pw1_w, m_conv_pw1_w, v_conv_pw1_w, "adamw_pw1")
    big["conv_dw_w"] = opt([r_dw], conv_dw_w, m_conv_dw_w, v_conv_dw_w, "adamw_dw")
    big["conv_pw2_w"] = opt([r_w2], conv_pw2_w, m_conv_pw2_w, v_conv_pw2_w, "adamw_pw2")
    small_w = pack(g_mix_pre, g_mix_post, g_ffn_pre, g_ffn_post, conv_pw1_b, conv_dw_b, conv_ln_g, conv_ln_b,
                   conv_pw2_b, attn_b_f, zero_row)
    small_m = pack(m_g_mix_pre, m_g_mix_post, m_g_ffn_pre, m_g_ffn_post, m_conv_pw1_b, m_conv_dw_b, m_conv_ln_g,
                   m_conv_ln_b, m_conv_pw2_b, m_attn_b_f, zero_row)
    small_v = pack(v_g_mix_pre, v_g_mix_post, v_g_ffn_pre, v_g_ffn_post, v_conv_pw1_b, v_conv_dw_b, v_conv_ln_g,
                   v_conv_ln_b, v_conv_pw2_b, v_attn_b_f, zero_row)
    sm = _adamw([r_small], small_w[None], small_m[None], small_v[None], small_w.shape[0], "adamw_small")
    sm = [t[0] for t in sm]
    loss = sm[0][15, 0]

    def unpack(t):
        return {"g_mix_pre": t[0:2], "g_mix_post": t[2:4], "g_ffn_pre": t[4:6], "g_ffn_post": t[6:8],
                "conv_pw1_b": t[8:10].reshape(1, 2 * D), "conv_dw_b": t[10:11], "conv_ln_g": t[11:12],
                "conv_ln_b": t[12:13], "conv_pw2_b": t[13:14], "attn_b_f": t[14:15, :H]}

    small = [unpack(t) for t in sm]
    names = ["g_mix_pre", "g_mix_post", "g_ffn_pre", "g_ffn_post", "conv_pw1_w", "conv_pw1_b", "conv_dw_w",
             "conv_dw_b", "conv_ln_g", "conv_ln_b", "conv_pw2_w", "conv_pw2_b", "attn_w_in", "attn_b_f",
             "attn_w_o", "mlp_w_up", "mlp_w_down"]
    outs = [loss, grad_x.reshape(1, T, D)]
    for kind in range(4):
        for nme in names:
            outs.append(big[nme][kind] if nme in big else small[kind][nme])
    return tuple(outs)
```

```python
import functools
import math

import numpy as np
import jax
import jax.numpy as jnp
from jax import lax
from jax.experimental import pallas as pl
from jax.experimental.pallas import tpu as pltpu

F32 = jnp.float32
BF16 = jnp.bfloat16

RMS_EPS = 1e-6
LN_EPS = 1e-5
MASK_VALUE = -1e30
ADAM_LR = 0.001
ADAM_B1 = 0.9
ADAM_B2 = 0.999
ADAM_EPS = 1e-08
ADAM_WD = 0.01
ADAM_STEP = 10

N_DEV = 8
LANES = 128
SUBLANES = 8
CONV_HALO = 32
CONV_ROWS = 32
FLASH_ROWS = 32
VMEM_LIMIT = 56 * 1024 * 1024

_pcall = pl.pallas_call


def _params(sem=None):
    if sem is None:
        return pltpu.CompilerParams(vmem_limit_bytes=VMEM_LIMIT)
    return pltpu.CompilerParams(dimension_semantics=sem, vmem_limit_bytes=VMEM_LIMIT)


def _dot(a, b):
    return jnp.dot(a, b, preferred_element_type=F32)


def _dot_nt(a, b):
    return lax.dot_general(a, b, (((1,), (1,)), ((), ())), preferred_element_type=F32)


def _dot_tn(a, b):
    return lax.dot_general(a, b, (((0,), (0,)), ((), ())), preferred_element_type=F32)


def _full(shape):
    nd = len(shape)
    return pl.BlockSpec(shape, lambda *g: (0,) * nd, pipeline_mode=pl.Buffered(1))


def _acc(shape):
    nd = len(shape)
    return pl.BlockSpec(shape, lambda *g: (0,) * nd)


def _rows(tm, cols):
    return pl.BlockSpec((tm, cols), lambda i: (i, 0))


def _rms(x, g):
    r = lax.rsqrt(jnp.mean(x * x, axis=-1, keepdims=True) + RMS_EPS)
    return x * r * g


def _rms_bwd(x, g, dy):
    r = lax.rsqrt(jnp.mean(x * x, axis=-1, keepdims=True) + RMS_EPS)
    n = x * r
    dg = jnp.sum(dy * n, axis=0, keepdims=True)
    dn = dy * g
    dx = r * (dn - n * jnp.mean(dn * n, axis=-1, keepdims=True))
    return dx, dg


def _sigmoid(x):
    return 1.0 / (1.0 + jnp.exp(-x))


def _mesh_pos():
    return lax.axis_index("x"), lax.axis_index("y"), lax.axis_index("c")


def _dev_index(px, py, pc):
    return 4 * px + 2 * py + pc


_HBM = pl.BlockSpec(memory_space=pltpu.HBM)
_SEM = pl.BlockSpec(memory_space=pltpu.SEMAPHORE)
_EFFECT = pltpu.SideEffectType.DATAFLOW_SIDE_EFFECTING


def _peer(r, x, y, c):
    p = ((1 - x) if r & 4 else x, (1 - y) if r & 2 else y, (1 - c) if r & 1 else c)
    return p, _dev_index(*p)


def _src_ref(refs, item, me_id=None, to_id=None):
    si, sub, mode = item
    r = refs[si] if sub is None else refs[si].at[sub]
    if mode == "slot":
        return r.at[to_id]
    if mode == "own":
        return r.at[me_id]
    return r


def _place_own(srcs, items, land_shapes, name, cast=False):
    ns, n = len(srcs), len(items)

    def body(*refs):
        src = refs[:ns]
        land = refs[ns:ns + n]
        stage = refs[ns + n:ns + 2 * n] if cast else None
        sems = refs[-1]
        me_id = _dev_index(*_mesh_pos())
        cps = []
        for a, item in enumerate(items):
            s = _src_ref(src, item, to_id=me_id)
            if cast:
                if s.shape != stage[a].shape:
                    stage[a][...] = jnp.zeros_like(stage[a])
                    stage[a][:, 0:s.shape[-1]] = s[...].astype(stage[a].dtype)
                else:
                    stage[a][...] = s[...].astype(stage[a].dtype)
                s = stage[a]
            cp = pltpu.make_async_copy(s, land[a].at[me_id], sems.at[a])
            cp.start()
            cps.append(cp)
        for cp in cps:
            cp.wait()

    return _pcall(
        body, name=name,
        out_shape=[jax.ShapeDtypeStruct(tuple(s), d) for s, d in land_shapes],
        in_specs=[pl.BlockSpec(memory_space=pltpu.VMEM if cast else pl.ANY)] * ns,
        out_specs=[pl.BlockSpec(memory_space=pl.ANY)] * n,
        scratch_shapes=([pltpu.VMEM(tuple(s[1:]), d) for s, d in land_shapes] if cast else [])
        + [pltpu.SemaphoreType.DMA((n,))],
        compiler_params=pltpu.CompilerParams(vmem_limit_bytes=VMEM_LIMIT),
    )(*srcs)


def _seed_lands(srcs, modes, me, name):
    n = len(srcs)
    parts = [tuple(s.shape[1:]) if mode == "slot" else tuple(s.shape) for s, mode in zip(srcs, modes)]

    def body(me_ref, *refs):
        s = pl.program_id(0)
        for a in range(n):
            v = refs[a][...]
            refs[n + a][...] = jnp.where(s == me_ref[0], v, jnp.zeros_like(v))

    def in_spec(part, mode):
        nd = len(part)
        if mode == "slot":
            return pl.BlockSpec((None,) + part, lambda s, me_ref: (me_ref[0],) + (0,) * nd)
        return pl.BlockSpec(part, lambda s, me_ref: (0,) * nd)

    def out_spec(part):
        nd = len(part)
        return pl.BlockSpec((None,) + part, lambda s, me_ref: (s,) + (0,) * nd)

    return _pcall(
        body, name=name,
        grid_spec=pltpu.PrefetchScalarGridSpec(
            num_scalar_prefetch=1, grid=(N_DEV,),
            in_specs=[in_spec(p, m) for p, m in zip(parts, modes)],
            out_specs=[out_spec(p) for p in parts]),
        out_shape=[jax.ShapeDtypeStruct((N_DEV,) + p, s.dtype) for p, s in zip(parts, srcs)],
        compiler_params=_params(("arbitrary",)),
    )(me, *srcs)


def _push_start(srcs, lands, groups, name):
    ns, n = len(srcs), len(lands)
    ng = len(groups)
    assert sum(len(g) for g in groups) == n

    def body(*refs):
        ops = refs[:ns + n]
        land = refs[ns:ns + n]
        sems = refs[ns + n:ns + n + 2 * ng]
        token = refs[-1]
        x, y, c = _mesh_pos()
        me_id = _dev_index(x, y, c)
        a = 0
        for gi, grp in enumerate(groups):
            for k, item in enumerate(grp):
                for r in range(1, N_DEV):
                    p, pid = _peer(r, x, y, c)
                    pltpu.make_async_remote_copy(
                        src_ref=_src_ref(ops, item, me_id=me_id, to_id=pid), dst_ref=land[a].at[me_id],
                        send_sem=sems[2 * gi].at[k * (N_DEV - 1) + r - 1],
                        recv_sem=sems[2 * gi + 1].at[k * (N_DEV - 1) + r - 1],
                        device_id=p, device_id_type=pl.DeviceIdType.MESH).start()
                a += 1
        token[...] = jnp.zeros_like(token)

    sem_shapes = []
    for grp in groups:
        sem_shapes += [pltpu.SemaphoreType.DMA((len(grp) * (N_DEV - 1),))] * 2
    arrs = list(srcs) + list(lands)
    res = _pcall(
        body, name=name,
        out_shape=tuple(sem_shapes) + tuple(pltpu.HBM(a.shape, a.dtype) for a in arrs)
        + (jax.ShapeDtypeStruct((SUBLANES, LANES), F32),),
        in_specs=[_HBM] * (ns + n),
        out_specs=tuple([_SEM] * (2 * ng)) + tuple([_HBM] * (ns + n)) + (pl.BlockSpec(memory_space=pltpu.VMEM),),
        input_output_aliases={i: 2 * ng + i for i in range(ns + n)},
        compiler_params=pltpu.CompilerParams(has_side_effects=_EFFECT),
    )(*[pltpu.with_memory_space_constraint(a, pltpu.HBM) for a in arrs])
    sems = [(res[2 * gi], res[2 * gi + 1]) for gi in range(ng)]
    thru = res[2 * ng:2 * ng + ns + n]
    return sems, list(thru[:ns]), list(thru[ns:]), res[-1]


def _push_wait(sems, srcs, lands, group, after, name):
    ns, n = len(srcs), len(lands)
    assert len(group) == n

    def body(*refs):
        ops = refs[:ns + n]
        land = refs[ns:ns + n]
        send_sems, recv_sems = refs[ns + n], refs[ns + n + 1]
        x, y, c = _mesh_pos()
        me_id = _dev_index(x, y, c)
        for k, item in enumerate(group):
            for r in range(1, N_DEV):
                p, pid = _peer(r, x, y, c)
                cp = pltpu.make_async_remote_copy(
                    src_ref=_src_ref(ops, item, me_id=me_id, to_id=pid), dst_ref=land[k].at[pid],
                    send_sem=send_sems.at[k * (N_DEV - 1) + r - 1], recv_sem=recv_sems.at[k * (N_DEV - 1) + r - 1],
                    device_id=p, device_id_type=pl.DeviceIdType.MESH)
                cp.wait_send()
                cp.wait_recv()

    arrs = list(srcs) + list(lands)
    res = _pcall(
        body, name=name,
        out_shape=tuple(pltpu.HBM(a.shape, a.dtype) for a in arrs),
        in_specs=[_HBM] * (ns + n) + [_SEM, _SEM, pl.BlockSpec(memory_space=pl.ANY)],
        out_specs=tuple([_HBM] * (ns + n)),
        input_output_aliases={i: i for i in range(ns + n)},
        compiler_params=pltpu.CompilerParams(has_side_effects=_EFFECT),
    )(*arrs, sems[0], sems[1], after)
    return list(res[ns:])


def _conv_in_fwd(x, g_pre, w1g, b1, tm):
    T, D = x.shape
    ns, _, cs = w1g.shape
    half = ns // 2

    def body(x_ref, g_ref, w_ref, b_ref, a_ref, u_ref):
        h = _rms(x_ref[...], g_ref[...]).astype(BF16)
        parts = []
        for s in range(ns):
            a_s = _dot(h, w_ref[s]) + b_ref[:, s * cs:(s + 1) * cs]
            a_ref[:, s * cs:(s + 1) * cs] = a_s
            parts.append(a_s)
        for s in range(half):
            u_ref[:, s * cs:(s + 1) * cs] = parts[s] * _sigmoid(parts[s + half])

    return _pcall(
        body, name="conv_in_fwd", grid=(T // tm,),
        in_specs=[_rows(tm, D), _full((1, D)), _full(w1g.shape), _full((1, 2 * D))],
        out_specs=[_rows(tm, 2 * D), _rows(tm, D)],
        out_shape=[jax.ShapeDtypeStruct((T, 2 * D), F32), jax.ShapeDtypeStruct((T, D), F32)],
        compiler_params=_params(("arbitrary",)),
    )(x, g_pre, w1g, b1)


def _shifted_copies(ext_ref, sh_ref, tm):
    n = tm + CONV_HALO - SUBLANES
    for b in range(1, SUBLANES):
        sh_ref[b - 1, 0:n, :] = ext_ref[b:b + n, :]


def _shifted_rows(ext_ref, sh_ref, off, r0, ls):
    b = off % SUBLANES
    a8 = off - b + r0
    src = ext_ref if b == 0 else sh_ref.at[b - 1]
    return src[a8:a8 + CONV_ROWS, ls]


def _dwconv_fwd(u, w32, b, tm, lc, width):
    T, D = u.shape
    hb = tm // CONV_HALO

    def body(u_ref, halo_ref, w_ref, b_ref, y_ref, ext_ref, sh_ref):
        i = pl.program_id(0)
        ext_ref[0:CONV_HALO, :] = jnp.where(i > 0, halo_ref[...], 0.0)
        ext_ref[CONV_HALO:, :] = u_ref[...]
        _shifted_copies(ext_ref, sh_ref, tm)
        for r0 in range(0, tm, CONV_ROWS):
            for l0 in range(0, lc, LANES):
                ls = slice(l0, l0 + LANES)
                acc = jnp.zeros((CONV_ROWS, LANES), F32) + b_ref[:, ls]
                for j in range(width):
                    off = CONV_HALO - (width - 1) + j
                    acc = acc + w_ref[j:j + 1, ls] * _shifted_rows(ext_ref, sh_ref, off, r0, ls)
                y_ref[r0:r0 + CONV_ROWS, ls] = acc

    return _pcall(
        body, name="dwconv_fwd", grid=(T // tm, D // lc),
        in_specs=[pl.BlockSpec((tm, lc), lambda i, l: (i, l)),
                  pl.BlockSpec((CONV_HALO, lc), lambda i, l: (jnp.maximum(i * hb - 1, 0), l)),
                  pl.BlockSpec((32, lc), lambda i, l: (0, l)),
                  pl.BlockSpec((1, lc), lambda i, l: (0, l))],
        out_specs=pl.BlockSpec((tm, lc), lambda i, l: (i, l)),
        out_shape=jax.ShapeDtypeStruct((T, D), F32),
        scratch_shapes=[pltpu.VMEM((tm + CONV_HALO, lc), F32),
                        pltpu.VMEM((SUBLANES - 1, tm + CONV_HALO, lc), F32)],
        compiler_params=_params(("arbitrary", "arbitrary")),
    )(u, u, w32, b)


def _ln_parts(y, g, b):
    mu = jnp.mean(y, axis=-1, keepdims=True)
    yc = y - mu
    rstd = lax.rsqrt(jnp.mean(yc * yc, axis=-1, keepdims=True) + LN_EPS)
    yhat = yc * rstd
    return yhat, rstd, yhat * g + b


def _conv_out_fwd(y, x, ln_g, ln_b, w2, b2, g_post, tm):
    T, D = x.shape

    def body(y_ref, x_ref, lg_ref, lb_ref, w_ref, b_ref, g_ref, m_ref, xo_ref):
        _, _, yn = _ln_parts(y_ref[...], lg_ref[...], lb_ref[...])
        z = (yn * _sigmoid(yn)).astype(BF16)
        m = _dot(z, w_ref[...]) + b_ref[...]
        m_ref[...] = m
        xo_ref[...] = x_ref[...] + _rms(m, g_ref[...])

    return _pcall(
        body, name="conv_out_fwd", grid=(T // tm,),
        in_specs=[_rows(tm, D), _rows(tm, D), _full((1, D)), _full((1, D)), _full((D, D)), _full((1, D)),
                  _full((1, D))],
        out_specs=[_rows(tm, D), _rows(tm, D)],
        out_shape=[jax.ShapeDtypeStruct((T, D), F32), jax.ShapeDtypeStruct((T, D), F32)],
        compiler_params=_params(("arbitrary",)),
    )(y, x, ln_g, ln_b, w2, b2, g_post)


def _conv_out_bwd(dxo, m, y, ln_g, ln_b, w2, g_post, tm, dep):
    T, D = m.shape

    def body(dxo_ref, m_ref, y_ref, lg_ref, lb_ref, w_ref, g_ref, dep_ref, dy_ref, dm_ref, z_ref, sums_ref):
        i = pl.program_id(0)
        dm, dgpost = _rms_bwd(m_ref[...], g_ref[...], dxo_ref[...])
        dmb = dm.astype(BF16)
        dm_ref[...] = dmb
        yhat, rstd, yn = _ln_parts(y_ref[...], lg_ref[...], lb_ref[...])
        sg = _sigmoid(yn)
        z_ref[...] = (yn * sg).astype(BF16)
        dz = _dot_nt(dmb, w_ref[...])
        dyn = dz * (sg + yn * sg * (1.0 - sg))
        dyh = dyn * lg_ref[...]
        dy = rstd * (dyh - jnp.mean(dyh, axis=-1, keepdims=True)
                     - yhat * jnp.mean(dyh * yhat, axis=-1, keepdims=True))
        dy_ref[...] = dy

        @pl.when(i == 0)
        def _():
            sums_ref[...] = jnp.zeros_like(sums_ref)
        sums_ref[0:1, :] += dgpost
        sums_ref[1:2, :] += jnp.sum(dyn * yhat, axis=0, keepdims=True)
        sums_ref[2:3, :] += jnp.sum(dyn, axis=0, keepdims=True)
        sums_ref[3:4, :] += jnp.sum(dm, axis=0, keepdims=True)
        sums_ref[4:5, :] += jnp.sum(dy, axis=0, keepdims=True)

    return _pcall(
        body, name="conv_out_bwd", grid=(T // tm,),
        in_specs=[_rows(tm, D), _rows(tm, D), _rows(tm, D), _full((1, D)), _full((1, D)), _full((D, D)),
                  _full((1, D)), pl.BlockSpec(memory_space=pl.ANY)],
        out_specs=[_rows(tm, D), _rows(tm, D), _rows(tm, D), _acc((SUBLANES, D))],
        out_shape=[jax.ShapeDtypeStruct((T, D), F32), jax.ShapeDtypeStruct((T, D), BF16),
                   jax.ShapeDtypeStruct((T, D), BF16), jax.ShapeDtypeStruct((SUBLANES, D), F32)],
        compiler_params=_params(("arbitrary",)),
    )(dxo, m, y, ln_g, ln_b, w2, g_post, dep)


def _dwconv_bwd(dy, u, w32, tm, lc, width):
    T, D = u.shape
    hb = tm // CONV_HALO
    nt = T // tm
    last_halo = T // CONV_HALO - 1

    def body(dy_ref, dyn_ref, u_ref, up_ref, w_ref, du_ref, dw_ref, exty_ref, extu_ref, acc_ref, shy_ref, shu_ref):
        i = pl.program_id(1)
        exty_ref[0:tm, :] = dy_ref[...]
        exty_ref[tm:, :] = jnp.where(i < nt - 1, dyn_ref[...], 0.0)
        extu_ref[0:CONV_HALO, :] = jnp.where(i > 0, up_ref[...], 0.0)
        extu_ref[CONV_HALO:, :] = u_ref[...]
        _shifted_copies(exty_ref, shy_ref, tm)
        _shifted_copies(extu_ref, shu_ref, tm)

        @pl.when(i == 0)
        def _():
            acc_ref[...] = jnp.zeros_like(acc_ref)

        for r0 in range(0, tm, CONV_ROWS):
            for l0 in range(0, lc, LANES):
                ls = slice(l0, l0 + LANES)
                dyc = exty_ref[r0:r0 + CONV_ROWS, ls]
                du = jnp.zeros((CONV_ROWS, LANES), F32)
                for j in range(width):
                    du = du + w_ref[j:j + 1, ls] * _shifted_rows(exty_ref, shy_ref, (width - 1) - j, r0, ls)
                    prod = dyc * _shifted_rows(extu_ref, shu_ref, CONV_HALO - (width - 1) + j, r0, ls)
                    acc_ref[j, :, ls] += prod.reshape(CONV_ROWS // SUBLANES, SUBLANES, LANES).sum(axis=0)
                du_ref[r0:r0 + CONV_ROWS, ls] = du

        @pl.when(i == nt - 1)
        def _():
            for j in range(32):
                dw_ref[j:j + 1, :] = jnp.sum(acc_ref[j], axis=0, keepdims=True)

    return _pcall(
        body, name="dwconv_bwd", grid=(D // lc, nt),
        in_specs=[pl.BlockSpec((tm, lc), lambda l, i: (i, l)),
                  pl.BlockSpec((CONV_HALO, lc), lambda l, i: (jnp.minimum((i + 1) * hb, last_halo), l)),
                  pl.BlockSpec((tm, lc), lambda l, i: (i, l)),
                  pl.BlockSpec((CONV_HALO, lc), lambda l, i: (jnp.maximum(i * hb - 1, 0), l)),
                  pl.BlockSpec((32, lc), lambda l, i: (0, l))],
        out_specs=[pl.BlockSpec((tm, lc), lambda l, i: (i, l)),
                   pl.BlockSpec((32, lc), lambda l, i: (0, l))],
        out_shape=[jax.ShapeDtypeStruct((T, D), F32), jax.ShapeDtypeStruct((32, D), F32)],
        scratch_shapes=[pltpu.VMEM((tm + CONV_HALO, lc), F32), pltpu.VMEM((tm + CONV_HALO, lc), F32),
                        pltpu.VMEM((32, SUBLANES, lc), F32),
                        pltpu.VMEM((SUBLANES - 1, tm + CONV_HALO, lc), F32),
                        pltpu.VMEM((SUBLANES - 1, tm + CONV_HALO, lc), F32)],
        compiler_params=_params(("arbitrary", "arbitrary")),
    )(dy, dy, u, u, w32)


def _conv_in_bwd(dxo, du, a, x, g_pre, w1g, tm):
    T, D = x.shape
    ns, _, cs = w1g.shape
    half = ns // 2

    def body(dxo_ref, du_ref, a_ref, x_ref, g_ref, w_ref, dxi_ref, h_ref, da_ref, sums_ref, db_ref):
        i = pl.program_id(0)
        xv = x_ref[...]
        h_ref[...] = _rms(xv, g_ref[...]).astype(BF16)
        dh = jnp.zeros((tm, D), F32)
        dbs = [None] * ns
        for s in range(half):
            a_u = a_ref[:, s * cs:(s + 1) * cs]
            sg = _sigmoid(a_ref[:, (s + half) * cs:(s + half + 1) * cs])
            du_s = du_ref[:, s * cs:(s + 1) * cs]
            da_u = du_s * sg
            da_g = du_s * a_u * sg * (1.0 - sg)
            for s2, v in ((s, da_u), (s + half, da_g)):
                vb = v.astype(BF16)
                da_ref[:, s2 * cs:(s2 + 1) * cs] = vb
                dbs[s2] = jnp.sum(v, axis=0, keepdims=True)
                dh = dh + _dot_nt(vb, w_ref[s2])
        dxi, dgpre = _rms_bwd(xv, g_ref[...], dh)
        dxi_ref[...] = dxo_ref[...] + dxi

        @pl.when(i == 0)
        def _():
            sums_ref[...] = jnp.zeros_like(sums_ref)
            db_ref[...] = jnp.zeros_like(db_ref)
        sums_ref[0:1, :] += dgpre
        for s in range(ns):
            db_ref[:, s * cs:(s + 1) * cs] += dbs[s]

    return _pcall(
        body, name="conv_in_bwd", grid=(T // tm,),
        in_specs=[_rows(tm, D), _rows(tm, D), _rows(tm, 2 * D), _rows(tm, D), _full((1, D)),
                  _full(w1g.shape)],
        out_specs=[_rows(tm, D), _rows(tm, D), _rows(tm, 2 * D), _acc((SUBLANES, D)), _acc((1, 2 * D))],
        out_shape=[jax.ShapeDtypeStruct((T, D), F32), jax.ShapeDtypeStruct((T, D), BF16),
                   jax.ShapeDtypeStruct((T, 2 * D), BF16), jax.ShapeDtypeStruct((SUBLANES, D), F32),
                   jax.ShapeDtypeStruct((1, 2 * D), F32)],
        compiler_params=_params(("arbitrary",)),
    )(dxo, du, a, x, g_pre, w1g)


def _mlp_fwd(x, g_pre, wug, wdg, g_post, tm, name):
    T, D = x.shape
    ns, _, fs = wug.shape

    def body(x_ref, gp_ref, wu_ref, wd_ref, gq_ref, up_ref, m_ref, xo_ref):
        xv = x_ref[...]
        h = _rms(xv, gp_ref[...]).astype(BF16)
        acc = jnp.zeros((tm, D), F32)
        for s in range(ns):
            up = _dot(h, wu_ref[s]).astype(BF16)
            up_ref[:, s * fs:(s + 1) * fs] = up
            act = jnp.square(jnp.maximum(up.astype(F32), 0.0)).astype(BF16)
            acc = acc + _dot(act, wd_ref[s])
        m_ref[...] = acc
        xo_ref[...] = xv + _rms(acc, gq_ref[...])

    return _pcall(
        body, name=name, grid=(T // tm,),
        in_specs=[_rows(tm, D), _full((1, D)), _full(wug.shape), _full(wdg.shape), _full((1, D))],
        out_specs=[_rows(tm, ns * fs), _rows(tm, D), _rows(tm, D)],
        out_shape=[jax.ShapeDtypeStruct((T, ns * fs), BF16), jax.ShapeDtypeStruct((T, D), F32),
                   jax.ShapeDtypeStruct((T, D), F32)],
        compiler_params=_params(("arbitrary",)),
    )(x, g_pre, wug, wdg, g_post)


def _mlp_bwd(dxo, m, x, up, g_pre, wug, wdg, g_post, tm, name, dep):
    T, D = x.shape
    ns, _, fs = wug.shape

    def body(dxo_ref, m_ref, x_ref, up_ref, gp_ref, wu_ref, wd_ref, gq_ref, dep_ref,
             dxi_ref, h_ref, dm_ref, dup_ref, sums_ref):
        i = pl.program_id(0)
        dxo = dxo_ref[...]
        dm, dgpost = _rms_bwd(m_ref[...], gq_ref[...], dxo)
        dmb = dm.astype(BF16)
        dm_ref[...] = dmb
        xv = x_ref[...]
        h_ref[...] = _rms(xv, gp_ref[...]).astype(BF16)
        dh = jnp.zeros((tm, D), F32)
        for s in range(ns):
            dact = _dot_nt(dmb, wd_ref[s])
            up = up_ref[:, s * fs:(s + 1) * fs].astype(F32)
            dup = (dact * (2.0 * jnp.maximum(up, 0.0))).astype(BF16)
            dup_ref[:, s * fs:(s + 1) * fs] = dup
            dh = dh + _dot_nt(dup, wu_ref[s])
        dxi, dgpre = _rms_bwd(xv, gp_ref[...], dh)
        dxi_ref[...] = dxo + dxi

        @pl.when(i == 0)
        def _():
            sums_ref[...] = jnp.zeros_like(sums_ref)
        sums_ref[0:1, :] += dgpost
        sums_ref[1:2, :] += dgpre

    return _pcall(
        body, name=name, grid=(T // tm,),
        in_specs=[_rows(tm, D), _rows(tm, D), _rows(tm, D), _rows(tm, ns * fs), _full((1, D)),
                  _full(wug.shape), _full(wdg.shape), _full((1, D)), pl.BlockSpec(memory_space=pl.ANY)],
        out_specs=[_rows(tm, D), _rows(tm, D), _rows(tm, D), _rows(tm, ns * fs), _acc((SUBLANES, D))],
        out_shape=[jax.ShapeDtypeStruct((T, D), F32), jax.ShapeDtypeStruct((T, D), BF16),
                   jax.ShapeDtypeStruct((T, D), BF16), jax.ShapeDtypeStruct((T, ns * fs), BF16),
                   jax.ShapeDtypeStruct((SUBLANES, D), F32)],
        compiler_params=_params(("arbitrary",)),
    )(dxo, m, x, up, g_pre, wug, wdg, g_post, dep)


def _mm_tn(a, g, *, nj, a_cols, g_cols, a_by_j, g_by_j, tk, name, act=False):
    T = a.shape[0]
    nk = T // tk

    def body(a_ref, g_ref, o_ref, acc_ref):
        k = pl.program_id(1)
        av = a_ref[...]
        if act:
            av = jnp.square(jnp.maximum(av.astype(F32), 0.0)).astype(BF16)
        p = _dot_tn(av, g_ref[...])

        @pl.when(k == 0)
        def _():
            acc_ref[...] = p

        @pl.when(k > 0)
        def _():
            acc_ref[...] += p

        @pl.when(k == nk - 1)
        def _():
            o_ref[...] = acc_ref[...].astype(BF16)

    return _pcall(
        body, name=name, grid=(nj, nk),
        in_specs=[pl.BlockSpec((tk, a_cols), (lambda j, k: (k, j)) if a_by_j else (lambda j, k: (k, 0))),
                  pl.BlockSpec((tk, g_cols), (lambda j, k: (k, j)) if g_by_j else (lambda j, k: (k, 0)))],
        out_specs=pl.BlockSpec((None, a_cols, g_cols), lambda j, k: (j, 0, 0)),
        out_shape=jax.ShapeDtypeStruct((nj, a_cols, g_cols), BF16),
        scratch_shapes=[pltpu.VMEM((a_cols, g_cols), F32)],
        compiler_params=_params(("arbitrary", "arbitrary")),
    )(a, g)


def _attn_in_fwd(x, g_pre, wqkv, wf, bf, tm, q_mul, n_heads):
    T, D = x.shape

    def body(x_ref, g_ref, w_ref, wf_ref, bf_ref, q_ref, k_ref, v_ref, lf_ref):
        h = _rms(x_ref[...], g_ref[...]).astype(BF16)
        q = _dot(h, w_ref[:, 0:D])
        if q_mul != 1.0:
            q = q * q_mul
        q_ref[...] = q.astype(BF16)
        k_ref[...] = _dot(h, w_ref[:, D:2 * D]).astype(BF16)
        v_ref[...] = _dot(h, w_ref[:, 2 * D:3 * D]).astype(BF16)
        fl = _dot(h, wf_ref[...]) + bf_ref[...]
        lf = jnp.minimum(fl, 0.0) - jnp.log(1.0 + jnp.exp(-jnp.abs(fl)))
        lane = lax.broadcasted_iota(jnp.int32, (1, LANES), 1)
        lf_ref[...] = jnp.where(lane < n_heads, lf, 0.0)

    return _pcall(
        body, name="attn_in_fwd", grid=(T // tm,),
        in_specs=[_rows(tm, D), _full((1, D)), _full((D, 3 * D)), _full((D, LANES)), _full((1, LANES))],
        out_specs=[_rows(tm, D), _rows(tm, D), _rows(tm, D), _rows(tm, LANES)],
        out_shape=[jax.ShapeDtypeStruct((T, D), BF16)] * 3 + [jax.ShapeDtypeStruct((T, LANES), F32)],
        compiler_params=_params(("arbitrary",)),
    )(x, g_pre, wqkv, wf, bf)


def _cumsum_rows(v, v2, tb, reverse, name):
    T, C = v.shape
    nb = T // tb

    def body(v_ref, v2_ref, o_ref, carry_ref):
        i = pl.program_id(0)

        @pl.when(i == 0)
        def _():
            carry_ref[...] = jnp.zeros_like(carry_ref)
        r = lax.broadcasted_iota(jnp.int32, (tb, tb), 0)
        c = lax.broadcasted_iota(jnp.int32, (tb, tb), 1)
        tri = jnp.where((c >= r) if reverse else (c <= r), 1.0, 0.0).astype(F32)
        out = jnp.dot(tri, v_ref[...] + v2_ref[...], precision=lax.Precision.HIGHEST,
                      preferred_element_type=F32) + carry_ref[...]
        o_ref[...] = out
        carry_ref[...] = out[0:1, :] if reverse else out[tb - 1:tb, :]

    idx = (lambda i: (nb - 1 - i, 0)) if reverse else (lambda i: (i, 0))
    return _pcall(
        body, name=name, grid=(nb,),
        in_specs=[pl.BlockSpec((tb, C), idx), pl.BlockSpec((tb, C), idx)],
        out_specs=pl.BlockSpec((tb, C), idx),
        out_shape=jax.ShapeDtypeStruct((T, C), F32),
        scratch_shapes=[pltpu.VMEM((1, C), F32)],
        compiler_params=_params(("arbitrary",)),
    )(v, v2)


def _head_col(v, lane, h):
    return jnp.sum(jnp.where(lane == h, v, 0.0), axis=1, keepdims=True)


def _flash_fwd(q, k, v, fq_aux, fk_rows, *, dh, tq, s_mul):
    T, D = q.shape
    G = D // LANES
    hpg = LANES // dh
    nq = T // tq
    k3 = k.reshape(nq, tq, D)
    v3 = v.reshape(nq, tq, D)

    rc = min(FLASH_ROWS, tq)

    def body(q_ref, k_ref, v_ref, fq_ref, fk_ref, o_ref, o32_ref, lse_ref, s_scr, p_scr):
        i = pl.program_id(1)
        lane = lax.broadcasted_iota(jnp.int32, (1, LANES), 1)
        q2 = q_ref[...]
        hmasks = [(lane >= hh * dh) & (lane < (hh + 1) * dh) for hh in range(hpg)]
        qms = [jnp.where(hm, q2, jnp.zeros_like(q2)) for hm in hmasks]

        nlb = tq // LANES
        sum_lane = [((hh + 1) % hpg) * dh for hh in range(hpg)]

        def scores(j, slot):
            kj = k_ref[j]
            for hh in range(hpg):
                s = _dot_nt(qms[hh], kj)
                s_scr[slot, hh] = s if s_mul == 1.0 else s * s_mul

        def soft(j, slot, carry, masked):
            vj = v_ref[j]
            out = []
            for hh in range(hpg):
                m_b, acc = carry[hh]
                fk_row = fk_ref[j, hh:hh + 1, :]
                def future(r0, cb):
                    return masked and cb * LANES > r0 + rc - 1

                mx = []
                for r0 in range(0, tq, rc):
                    rs = slice(r0, r0 + rc)
                    c = None
                    for cb in range(nlb):
                        if future(r0, cb):
                            continue
                        cs = slice(cb * LANES, (cb + 1) * LANES)
                        s = s_scr[slot, hh, rs, cs] - fk_row[:, cs]
                        if masked and (cb + 1) * LANES - 1 > r0:
                            ri = r0 + lax.broadcasted_iota(jnp.int32, (rc, LANES), 0)
                            ci = cb * LANES + lax.broadcasted_iota(jnp.int32, (rc, LANES), 1)
                            s = jnp.where(ci <= ri, s, MASK_VALUE)
                        s_scr[slot, hh, rs, cs] = s
                        c = s if c is None else jnp.maximum(c, s)
                    mx.append(c)
                row_max = jnp.max(jnp.concatenate(mx, axis=0), axis=1, keepdims=True)
                m_new = jnp.maximum(m_b, row_max)
                alpha = jnp.exp(m_b - m_new)
                for r0 in range(0, tq, rc):
                    rs = slice(r0, r0 + rc)
                    m_c = m_new[rs]
                    for cb in range(nlb):
                        cs = slice(cb * LANES, (cb + 1) * LANES)
                        if future(r0, cb):
                            p_scr[hh, rs, cs] = jnp.zeros((rc, LANES), BF16)
                        else:
                            p_scr[hh, rs, cs] = jnp.exp(s_scr[slot, hh, rs, cs] - m_c).astype(BF16)
                v_one = jnp.where(hmasks[hh], vj, jnp.ones_like(vj))
                out.append((m_new, alpha * acc + _dot(p_scr[hh], v_one)))
            return tuple(out)

        def step(j, carry, masked):
            scores(j, 0)
            return soft(j, 0, carry, masked)

        init = tuple((jnp.full((tq, LANES), MASK_VALUE, F32), jnp.zeros((tq, LANES), F32)) for _ in range(hpg))
        carry = lax.fori_loop(0, i, lambda j, cr: step(j, cr, False), init)
        carry = step(i, carry, True)
        fq = fq_ref[...]
        g0 = pl.program_id(0) * hpg
        o_all = jnp.zeros((tq, LANES), F32)
        lse_all = jnp.zeros((tq, LANES), F32)
        for hh in range(hpg):
            m_b, acc = carry[hh]
            l = acc[:, sum_lane[hh]:sum_lane[hh] + 1]
            o_all = jnp.where(hmasks[hh], acc * (1.0 / l), o_all)
            lse_all = jnp.where(lane == hh, m_b[:, 0:1] + jnp.log(l) + _head_col(fq, lane, g0 + hh), lse_all)
        o_ref[...] = o_all.astype(BF16)
        o32_ref[...] = o_all
        lse_ref[...] = lse_all

    return _pcall(
        body, name="flash_fwd", grid=(G, nq),
        in_specs=[pl.BlockSpec((tq, LANES), lambda g, i: (i, g)),
                  pl.BlockSpec((nq, tq, LANES), lambda g, i: (0, 0, g)),
                  pl.BlockSpec((nq, tq, LANES), lambda g, i: (0, 0, g)),
                  pl.BlockSpec((tq, LANES), lambda g, i: (i, 0)),
                  pl.BlockSpec((None, nq, SUBLANES, tq), lambda g, i: (g, 0, 0, 0))],
        out_specs=[pl.BlockSpec((tq, LANES), lambda g, i: (i, g)),
                   pl.BlockSpec((tq, LANES), lambda g, i: (i, g)),
                   pl.BlockSpec((None, tq, LANES), lambda g, i: (g, i, 0))],
        out_shape=[jax.ShapeDtypeStruct((T, D), BF16), jax.ShapeDtypeStruct((T, D), F32),
                   jax.ShapeDtypeStruct((G, T, LANES), F32)],
        scratch_shapes=[pltpu.VMEM((1, hpg, tq, tq), F32), pltpu.VMEM((hpg, tq, tq), BF16)],
        compiler_params=_params(("arbitrary", "arbitrary")),
    )(q, k3, v3, fq_aux, fk_rows)


def _flash_bwd(q, k, v, do, fq_aux, lse_aux, dl_aux, fk_rows, *, dh, tq, tk, s_mul, dq_mul):
    T, D = q.shape
    G = D // LANES
    hpg = LANES // dh
    nq = T // tq
    nk = T // tk
    per = tq // tk
    k3 = k.reshape(nk, tk, D)
    v3 = v.reshape(nk, tk, D)
    rc = min(FLASH_ROWS, tq)

    def body(q_ref, k_ref, v_ref, do_ref, fq_ref, lse_ref, dl_ref, fk_ref,
             dq_ref, dk_ref, dv_ref, dfq_ref, dfk_ref, dk_acc, dv_acc, s_scr, dp_scr, p_scr, ds_scr):
        i = pl.program_id(1)

        @pl.when(i == 0)
        def _():
            dk_acc[...] = jnp.zeros_like(dk_acc)
            dv_acc[...] = jnp.zeros_like(dv_acc)
            dfk_ref[...] = jnp.zeros_like(dfk_ref)

        lane = lax.broadcasted_iota(jnp.int32, (1, LANES), 1)
        q2 = q_ref[...]
        do2 = do_ref[...]
        g0 = pl.program_id(0) * hpg
        fq = fq_ref[...]
        lse = lse_ref[...]
        dl = dl_ref[...]
        hmasks = [(lane >= hh * dh) & (lane < (hh + 1) * dh) for hh in range(hpg)]
        qms = [jnp.where(hm, q2, jnp.zeros_like(q2)) for hm in hmasks]
        doms = [jnp.where(hm, do2, jnp.zeros_like(do2)) for hm in hmasks]
        q2t = q2.astype(F32).T.astype(BF16)
        do2t = do2.astype(F32).T.astype(BF16)
        ones_rows = jnp.ones((2 * SUBLANES, tk), BF16)
        q1t = [jnp.concatenate([q2t[hh * dh:(hh + 1) * dh], jnp.ones((2 * SUBLANES, tq), BF16)], axis=0)
               for hh in range(hpg)]
        c_bs = [jnp.broadcast_to(_head_col(fq, lane, g0 + hh) - lse[:, hh:hh + 1], (tq, LANES))
                for hh in range(hpg)]
        dl_bs = [jnp.broadcast_to(_head_col(dl, lane, g0 + hh), (tq, LANES)) for hh in range(hpg)]
        nlb = tk // LANES

        def step(j, carry, off):
            masked = off is not None
            kj = k_ref[j]
            vj = v_ref[j]
            kjt = kj.astype(F32).T.astype(BF16)
            for hh in range(hpg):
                s = _dot_nt(qms[hh], kj)
                s_scr[hh] = s if s_mul == 1.0 else s * s_mul
                dp_scr[hh] = _dot_nt(doms[hh], vj)
            out = []
            for hh in range(hpg):
                fk_row = fk_ref[j, hh:hh + 1, :]
                for r0 in range(0, tq, rc):
                    rs = slice(r0, r0 + rc)
                    if masked and r0 + rc <= off:
                        p_scr[hh, rs, :] = jnp.zeros((rc, tk), BF16)
                        ds_scr[hh, rs, :] = jnp.zeros((rc, tk), BF16)
                        continue
                    c_c = c_bs[hh][rs]
                    dl_c = dl_bs[hh][rs]
                    for cb in range(nlb):
                        cs = slice(cb * LANES, (cb + 1) * LANES)
                        if masked and off + cb * LANES > r0 + rc - 1:
                            p_scr[hh, rs, cs] = jnp.zeros((rc, LANES), BF16)
                            ds_scr[hh, rs, cs] = jnp.zeros((rc, LANES), BF16)
                            continue
                        e = (s_scr[hh, rs, cs] - fk_row[:, cs]) + c_c
                        if masked and off + (cb + 1) * LANES - 1 > r0:
                            ri = r0 + lax.broadcasted_iota(jnp.int32, (rc, LANES), 0)
                            ci = off + cb * LANES + lax.broadcasted_iota(jnp.int32, (rc, LANES), 1)
                            e = jnp.where(ci <= ri, e, MASK_VALUE)
                        p = jnp.exp(e)
                        ds = p * (dp_scr[hh, rs, cs] - dl_c)
                        p_scr[hh, rs, cs] = p.astype(BF16)
                        ds_scr[hh, rs, cs] = ds.astype(BF16)
                k1t = jnp.concatenate([kjt[hh * dh:(hh + 1) * dh], ones_rows], axis=0)
                out.append(carry[hh] + _dot_nt(k1t, ds_scr[hh]))
            dks = [_dot(q1t[hh], ds_scr[hh]) for hh in range(hpg)]
            for hh in range(hpg):
                dfk_ref[j, hh:hh + 1, :] += -dks[hh][dh:dh + 1, :]
            dk_acc[j] += jnp.concatenate([d[0:dh] for d in dks], axis=0)
            dv_acc[j] += jnp.concatenate(
                [_dot(do2t[hh * dh:(hh + 1) * dh], p_scr[hh]) for hh in range(hpg)], axis=0)
            return tuple(out)

        init = tuple(jnp.zeros((dh + 2 * SUBLANES, tq), F32) for _ in range(hpg))
        carry = lax.fori_loop(0, i * per, lambda j, cr: step(j, cr, None), init)
        for d in range(per):
            carry = step(i * per + d, carry, d * tk)
        dq_all = jnp.concatenate([carry[hh][0:dh] for hh in range(hpg)], axis=0).T
        dq_ref[...] = (dq_all * dq_mul).astype(BF16)
        dfq_ref[...] = jnp.concatenate([carry[hh][dh:dh + 1, :] for hh in range(hpg)]
                                       + [jnp.zeros((SUBLANES - hpg, tq), F32)], axis=0)

        @pl.when(i == nq - 1)
        def _():
            for jj in range(nk):
                dkv = dk_acc[jj].T
                if s_mul != 1.0:
                    dkv = dkv * s_mul
                dk_ref[jj] = dkv.astype(BF16)
                dv_ref[jj] = dv_acc[jj].T.astype(BF16)

    blk = pl.BlockSpec((tq, LANES), lambda g, i: (i, g))
    res = pl.BlockSpec((nk, tk, LANES), lambda g, i: (0, 0, g))
    res_in = pl.BlockSpec((nk, tk, LANES), lambda g, i: (0, 0, g), pipeline_mode=pl.Buffered(1))
    aux = pl.BlockSpec((None, tq, LANES), lambda g, i: (g, i, 0))
    heads = pl.BlockSpec((tq, LANES), lambda g, i: (i, 0))
    rows = pl.BlockSpec((None, nk, SUBLANES, tk), lambda g, i: (g, 0, 0, 0))
    dq, dk3, dv3, dfq, dfk = _pcall(
        body, name="flash_bwd", grid=(G, nq),
        in_specs=[blk, res_in, res_in, blk, heads, aux, heads, rows],
        out_specs=[blk, res, res, pl.BlockSpec((None, None, SUBLANES, tq), lambda g, i: (g, i, 0, 0)), rows],
        out_shape=[jax.ShapeDtypeStruct((T, D), BF16), jax.ShapeDtypeStruct((nk, tk, D), BF16),
                   jax.ShapeDtypeStruct((nk, tk, D), BF16), jax.ShapeDtypeStruct((G, nq, SUBLANES, tq), F32),
                   jax.ShapeDtypeStruct((G, nk, SUBLANES, tk), F32)],
        scratch_shapes=[pltpu.VMEM((nk, LANES, tk), F32), pltpu.VMEM((nk, LANES, tk), F32),
                        pltpu.VMEM((hpg, tq, tk), F32), pltpu.VMEM((hpg, tq, tk), F32),
                        pltpu.VMEM((hpg, tq, tk), BF16), pltpu.VMEM((hpg, tq, tk), BF16)],
        compiler_params=_params(("arbitrary", "arbitrary")),
    )(q, k3, v3, do, fq_aux, lse_aux, dl_aux, fk_rows)
    return dq, dk3.reshape(T, D), dv3.reshape(T, D), dfq, dfk


def _attn_out_fwd(o, x, wo, g_post, tm):
    T, D = x.shape

    def body(o_ref, x_ref, w_ref, g_ref, m_ref, xo_ref):
        m = _dot(o_ref[...], w_ref[...])
        m_ref[...] = m
        xo_ref[...] = x_ref[...] + _rms(m, g_ref[...])

    return _pcall(
        body, name="attn_out_fwd", grid=(T // tm,),
        in_specs=[_rows(tm, D), _rows(tm, D), _full((D, D)), _full((1, D))],
        out_specs=[_rows(tm, D), _rows(tm, D)],
        out_shape=[jax.ShapeDtypeStruct((T, D), F32), jax.ShapeDtypeStruct((T, D), F32)],
        compiler_params=_params(("arbitrary",)),
    )(o, x, wo, g_post)


def _attn_out_bwd(dxo, m, o, wo, g_post, head_ind, tm, dep):
    T, D = m.shape

    def body(dxo_ref, m_ref, o_ref, w_ref, g_ref, ind_ref, dep_ref, dm_ref, do_ref, dl_ref, sums_ref):
        i = pl.program_id(0)
        dm, dgpost = _rms_bwd(m_ref[...], g_ref[...], dxo_ref[...])
        dmb = dm.astype(BF16)
        dm_ref[...] = dmb
        dob = _dot_nt(dmb, w_ref[...]).astype(BF16)
        do_ref[...] = dob
        dl_ref[...] = jnp.dot(dob.astype(F32) * o_ref[...], ind_ref[...], precision=lax.Precision.HIGHEST,
                              preferred_element_type=F32)

        @pl.when(i == 0)
        def _():
            sums_ref[...] = jnp.zeros_like(sums_ref)
        sums_ref[0:1, :] += dgpost

    return _pcall(
        body, name="attn_out_bwd", grid=(T // tm,),
        in_specs=[_rows(tm, D), _rows(tm, D), _rows(tm, D), _full((D, D)), _full((1, D)), _full((D, LANES)),
                  pl.BlockSpec(memory_space=pl.ANY)],
        out_specs=[_rows(tm, D), _rows(tm, D), _rows(tm, LANES), _acc((SUBLANES, D))],
        out_shape=[jax.ShapeDtypeStruct((T, D), BF16), jax.ShapeDtypeStruct((T, D), BF16),
                   jax.ShapeDtypeStruct((T, LANES), F32), jax.ShapeDtypeStruct((SUBLANES, D), F32)],
        compiler_params=_params(("arbitrary",)),
    )(dxo, m, o, wo, g_post, head_ind, dep)


def _attn_in_bwd(dxo, x, g_pre, dq, dk, dv, dlf, lf, wqkv, wf, tm, n_heads):
    T, D = x.shape

    def body(dxo_ref, x_ref, g_ref, dq_ref, dk_ref, dv_ref, dlf_ref, lf_ref, w_ref, wf_ref,
             dxi_ref, h_ref, df_ref, sums_ref, dbf_ref):
        i = pl.program_id(0)
        xv = x_ref[...]
        h_ref[...] = _rms(xv, g_ref[...]).astype(BF16)
        lane = lax.broadcasted_iota(jnp.int32, (1, LANES), 1)
        df = jnp.where(lane < n_heads, dlf_ref[...] * (1.0 - jnp.exp(lf_ref[...])), 0.0)
        dfb = df.astype(BF16)
        df_ref[...] = dfb
        dh = (_dot_nt(dq_ref[...], w_ref[:, 0:D]) + _dot_nt(dk_ref[...], w_ref[:, D:2 * D])
              + _dot_nt(dv_ref[...], w_ref[:, 2 * D:3 * D]) + _dot_nt(dfb, wf_ref[...]))
        dxi, dgpre = _rms_bwd(xv, g_ref[...], dh)
        dxi_ref[...] = dxo_ref[...] + dxi

        @pl.when(i == 0)
        def _():
            sums_ref[...] = jnp.zeros_like(sums_ref)
            dbf_ref[...] = jnp.zeros_like(dbf_ref)
        sums_ref[0:1, :] += dgpre
        dbf_ref[...] += jnp.sum(df, axis=0, keepdims=True)

    return _pcall(
        body, name="attn_in_bwd", grid=(T // tm,),
        in_specs=[_rows(tm, D), _rows(tm, D), _full((1, D)), _rows(tm, D), _rows(tm, D), _rows(tm, D),
                  _rows(tm, LANES), _rows(tm, LANES), _full((D, 3 * D)), _full((D, LANES))],
        out_specs=[_rows(tm, D), _rows(tm, D), _rows(tm, LANES), _acc((SUBLANES, D)), _acc((1, LANES))],
        out_shape=[jax.ShapeDtypeStruct((T, D), F32), jax.ShapeDtypeStruct((T, D), BF16),
                   jax.ShapeDtypeStruct((T, LANES), BF16), jax.ShapeDtypeStruct((SUBLANES, D), F32),
                   jax.ShapeDtypeStruct((1, LANES), F32)],
        compiler_params=_params(("arbitrary",)),
    )(dxo, x, g_pre, dq, dk, dv, dlf, lf, wqkv, wf)


def _loss_head(y, target, tm):
    T, D = y.shape

    def body(y_ref, t_ref, dy_ref, loss_ref):
        i = pl.program_id(0)
        err = y_ref[...] - t_ref[...]
        dy_ref[...] = err * (1.0 / D)
        part = 0.5 * jnp.sum(jnp.mean(err * err, axis=-1, keepdims=True), axis=0, keepdims=True)

        @pl.when(i == 0)
        def _():
            loss_ref[...] = jnp.zeros_like(loss_ref)
        loss_ref[...] += part

    return _pcall(
        body, name="loss_head", grid=(T // tm,),
        in_specs=[_rows(tm, D), _rows(tm, D)],
        out_specs=[_rows(tm, D), _acc((SUBLANES, LANES))],
        out_shape=[jax.ShapeDtypeStruct((T, D), F32), jax.ShapeDtypeStruct((SUBLANES, LANES), F32)],
        compiler_params=_params(("arbitrary",)),
    )(y, target)


def _adamw(recvs, w, m, v, tr, name):
    L, R, C = w.shape
    assert len(recvs) == L
    c1 = 1.0 - ADAM_B1 ** ADAM_STEP
    c2 = 1.0 - ADAM_B2 ** ADAM_STEP

    def body(*refs):
        r_refs = refs[:L]
        w_ref, m_ref, v_ref, g_ref, d_ref, nm_ref, nv_ref = refs[L:]
        layer = pl.program_id(0)
        g = None
        for k in range(L):
            gk = r_refs[k][0, :, 0:C].astype(F32)
            for s in range(1, N_DEV):
                gk = gk + r_refs[k][s, :, 0:C].astype(F32)
            g = gk if g is None else jnp.where(layer == k, gk, g)
        nm = ADAM_B1 * m_ref[...] + (1.0 - ADAM_B1) * g
        nv = ADAM_B2 * v_ref[...] + (1.0 - ADAM_B2) * jnp.square(g)
        m_hat = nm / c1
        v_hat = nv / c2
        g_ref[...] = g
        d_ref[...] = -ADAM_LR * (m_hat / (jnp.sqrt(v_hat) + ADAM_EPS) + ADAM_WD * w_ref[...])
        nm_ref[...] = nm
        nv_ref[...] = nv

    def recv_spec(k):
        return pl.BlockSpec((N_DEV, tr, recvs[k].shape[-1]), lambda l, i: (0, jnp.where(l == k, i, 0), 0))

    blk = pl.BlockSpec((None, tr, C), lambda l, i: (l, i, 0))
    return _pcall(
        body, name=name, grid=(L, R // tr),
        in_specs=[recv_spec(k) for k in range(L)] + [blk] * 3,
        out_specs=[blk] * 4,
        out_shape=[jax.ShapeDtypeStruct((L, R, C), F32)] * 4,
        compiler_params=_params(("arbitrary", "arbitrary")),
    )(*recvs, w, m, v)


def _row_block(rows, cols):
    cap = max(SUBLANES, (256 * 1024) // max(cols, 1))
    best = None
    for t in range(SUBLANES, rows + 1, SUBLANES):
        if rows % t == 0 and t <= cap:
            best = t
    return rows if best is None else best


def kernel(x, g_mix_pre, g_mix_post, g_ffn_pre, g_ffn_post, conv_pw1_w, conv_pw1_b, conv_dw_w, conv_dw_b, conv_ln_g, conv_ln_b, conv_pw2_w, conv_pw2_b, attn_w_in, attn_b_f, attn_w_o, mlp_w_up, mlp_w_down, loss_target, m_g_mix_pre, m_g_mix_post, m_g_ffn_pre, m_g_ffn_post, m_conv_pw1_w, m_conv_pw1_b, m_conv_dw_w, m_conv_dw_b, m_conv_ln_g, m_conv_ln_b, m_conv_pw2_w, m_conv_pw2_b, m_attn_w_in, m_attn_b_f, m_attn_w_o, m_mlp_w_up, m_mlp_w_down, v_g_mix_pre, v_g_mix_post, v_g_ffn_pre, v_g_ffn_post, v_conv_pw1_w, v_conv_pw1_b, v_conv_dw_w, v_conv_dw_b, v_conv_ln_g, v_conv_ln_b, v_conv_pw2_w, v_conv_pw2_b, v_attn_w_in, v_attn_b_f, v_attn_w_o, v_mlp_w_up, v_mlp_w_down):
    _, T, D = x.shape
    H = attn_b_f.shape[-1]
    dh = D // H
    width = conv_dw_w.shape[1]
    cin = attn_w_in.shape[-1]
    fs = mlp_w_up.shape[-1]
    G = D // LANES
    hpg = LANES // dh
    assert T % 4 == 0 and D % LANES == 0 and LANES % dh == 0 and width <= CONV_HALO and H <= LANES

    tm = min(512, T // 4)
    tmb = min(256, T // 4)
    tqf = min(1024, T // 4)
    tkb = tm
    tmc = min(256, T // 4)
    lc = min(256, D)
    tkw = min(2048, T // 4)
    tb = min(256, T // 4)

    scale = float(dh) ** -0.5
    mant, _ = math.frexp(scale)
    q_mul = scale if mant == 0.5 else 1.0
    s_mul = 1.0 if mant == 0.5 else scale

    x2 = x.reshape(T, D)
    tgt = loss_target.reshape(T, D)

    w_srcs = [conv_pw1_w, conv_dw_w, conv_pw2_w, mlp_w_up, mlp_w_down, attn_w_in, attn_w_o]
    w_items = [(0, 0, "whole"), (1, 0, "whole"), (2, 0, "whole"), (3, 0, "whole"), (4, 0, "whole"),
               (5, 0, "whole"), (6, 0, "whole"), (3, 1, "whole"), (4, 1, "whole")]
    cin_w = -(-cin // LANES) * LANES
    w_lands = _place_own(
        w_srcs, w_items,
        [((N_DEV, D, cin_w) if si == 5 else (N_DEV,) + w_srcs[si].shape[1:], F32 if si == 1 else BF16)
         for si, _, _ in w_items],
        "stage_weights", cast=True)
    me_arr = _dev_index(*_mesh_pos()).astype(jnp.int32).reshape(1)
    w_groups = [[0], [1, 2], [3, 4], [5, 6], [7, 8]]
    g_sems, _, w_lands, g_token = _push_start(
        [], w_lands, [[(a, None, "own") for a in grp] for grp in w_groups], "gather_start")

    def gather_wait(gi, after):
        grp = w_groups[gi]
        return _push_wait(g_sems[gi], [], [w_lands[a] for a in grp], [(k, None, "own") for k in range(len(grp))],
                          after, "gather_wait%d" % gi)

    (w1g,) = gather_wait(0, g_token)
    bf = jnp.pad(attn_b_f, ((0, 0), (0, LANES - H)))

    row = lambda a, i: a[i:i + 1]

    a0, u0 = _conv_in_fwd(x2, row(g_mix_pre, 0), w1g, conv_pw1_b, tm)
    dwg, w2g = gather_wait(1, u0)
    w2 = w2g.reshape(D, D)
    dw_full = jnp.transpose(dwg, (1, 0, 2)).reshape(width, D)
    w32 = jnp.pad(dw_full, ((0, 32 - width), (0, 0)))
    y0 = _dwconv_fwd(u0, w32, conv_dw_b, tmc, lc, width)
    m0, x_1 = _conv_out_fwd(y0, x2, conv_ln_g, conv_ln_b, w2, conv_pw2_b, row(g_mix_post, 0), tm)
    wu0, wd0 = gather_wait(2, x_1)
    up0, n0, x_2 = _mlp_fwd(x_1, row(g_ffn_pre, 0), wu0, wd0, row(g_ffn_post, 0), tm, "mlp0_fwd")

    wing, wog = gather_wait(3, x_2)
    wo = wog.reshape(D, D)
    win = jnp.transpose(wing[:, :, :cin], (1, 0, 2)).reshape(D, N_DEV * cin)
    wqkv = win[:, :3 * D]
    wf = jnp.pad(win[:, 3 * D:], ((0, 0), (0, LANES - H)))
    q, k, v, lf = _attn_in_fwd(x_2, row(g_mix_pre, 1), wqkv, wf, bf, tm, q_mul, H)
    fcum = _cumsum_rows(lf, jnp.zeros_like(lf), tb, False, "forget_cumsum")


    def key_rows(blk):
        r = jnp.transpose(fcum[:, :H].T.reshape(G, hpg, T // blk, blk), (0, 2, 1, 3))
        return jnp.pad(r, ((0, 0), (0, 0), (0, SUBLANES - hpg), (0, 0)))

    fk_rows = key_rows(tkb)
    o, o32, lse_aux = _flash_fwd(q, k, v, fcum, key_rows(tqf), dh=dh, tq=tqf, s_mul=s_mul)
    m1, x_3 = _attn_out_fwd(o, x_2, wo, row(g_mix_post, 1), tm)
    wu1, wd1 = gather_wait(4, x_3)
    up1, n1, x_4 = _mlp_fwd(x_3, row(g_ffn_pre, 1), wu1, wd1, row(g_ffn_post, 1), tm, "mlp1_fwd")

    dx, loss_blk = _loss_head(x_4, tgt, tm)

    def mlp_back(dx, n_l, x_in, up_l, l, wu, wd, dep):
        dxi, h, dm, dup, sums = _mlp_bwd(dx, n_l, x_in, up_l, row(g_ffn_pre, l), wu, wd, row(g_ffn_post, l),
                                         tmb, "mlp%d_bwd" % l, dep)
        dwu = _mm_tn(h, dup, nj=N_DEV, a_cols=D, g_cols=fs, a_by_j=False, g_by_j=True, tk=tkw,
                     name="mlp%d_dwu" % l)
        dwd = _mm_tn(up_l, dm, nj=N_DEV, a_cols=fs, g_cols=D, a_by_j=True, g_by_j=False, tk=tkw,
                     name="mlp%d_dwd" % l, act=True)
        return dxi, dwu, dwd, sums

    def push_grads(srcs, modes, name):
        items = [(i, None, mode) for i, mode in enumerate(modes)]
        lands = _seed_lands(srcs, modes, me_arr, name + "_own")
        sems, srcs_t, lands_t, token = _push_start(srcs, lands, [items], name + "_start")
        return (sems[0], srcs_t, lands_t, items), token

    def pull_grads(handle, after, name):
        sems, srcs_t, lands_t, items = handle
        return _push_wait(sems, srcs_t, lands_t, items, after, name + "_wait")

    rs = D // N_DEV
    dx, dwu1, dwd1, s_mlp1 = mlp_back(dx, n1, x_3, up1, 1, wu1, wd1, loss_blk)
    h_mlp1, tok = push_grads([dwu1, dwd1], ["slot", "slot"], "grads_mlp1")

    head_ind = jnp.asarray((np.arange(D)[:, None] // dh == np.arange(LANES)[None, :]).astype(np.float32))
    dm1, do, delta, s_ao = _attn_out_bwd(dx, m1, o32, wo, row(g_mix_post, 1), head_ind, tm, tok)
    dwo = _mm_tn(o, dm1, nj=1, a_cols=D, g_cols=D, a_by_j=False, g_by_j=False, tk=tkw, name="attn_dwo")
    dq, dk, dv, dfq, dfk = _flash_bwd(q, k, v, do, fcum, lse_aux, delta, fk_rows,
                                      dh=dh, tq=tkb, tk=tkb, s_mul=s_mul, dq_mul=scale)
    def head_cols(r):
        return jnp.pad(jnp.transpose(r[:, :, :hpg, :], (0, 2, 1, 3)).reshape(H, T).T, ((0, 0), (0, LANES - H)))

    df_k = head_cols(dfk)
    df_q = head_cols(dfq)
    dlf = _cumsum_rows(df_q, df_k, tb, True, "forget_cumsum_bwd")
    dx, h_at, df, s_ai, dbf = _attn_in_bwd(dx, x_2, row(g_mix_pre, 1), dq, dk, dv, dlf, lf, wqkv, wf, tm, H)
    dwq = _mm_tn(h_at, dq, nj=1, a_cols=D, g_cols=D, a_by_j=False, g_by_j=False, tk=tkw, name="attn_dwq")
    dwk = _mm_tn(h_at, dk, nj=1, a_cols=D, g_cols=D, a_by_j=False, g_by_j=False, tk=tkw, name="attn_dwk")
    dwv = _mm_tn(h_at, dv, nj=1, a_cols=D, g_cols=D, a_by_j=False, g_by_j=False, tk=tkw, name="attn_dwv")
    dwf = _mm_tn(h_at, df, nj=1, a_cols=D, g_cols=LANES, a_by_j=False, g_by_j=False, tk=tkw, name="attn_dwf")
    dwin = jnp.concatenate([dwq[0], dwk[0], dwv[0], dwf[0][:, :H]], axis=1)
    dwin = jnp.pad(jnp.transpose(dwin.reshape(D, N_DEV, cin), (1, 0, 2)), ((0, 0), (0, 0), (0, cin_w - cin)))
    h_attn, tok = push_grads([dwin, dwo.reshape(N_DEV, rs, D)], ["slot", "slot"], "grads_attn")

    dx, dwu0, dwd0, s_mlp0 = mlp_back(dx, n0, x_1, up0, 0, wu0, wd0, tok)
    h_mlp0, tok = push_grads([dwu0, dwd0], ["slot", "slot"], "grads_mlp0")

    dy0, dm0, z0, s_co = _conv_out_bwd(dx, m0, y0, conv_ln_g, conv_ln_b, w2, row(g_mix_post, 0), tm, tok)
    dw2 = _mm_tn(z0, dm0, nj=1, a_cols=D, g_cols=D, a_by_j=False, g_by_j=False, tk=tkw, name="conv_dw2")
    du0, ddw = _dwconv_bwd(dy0, u0, w32, tmc, lc, width)
    grad_x, h_cv, da0, s_ci, db1 = _conv_in_bwd(dx, du0, a0, x2, row(g_mix_pre, 0), w1g, tm)
    dw1 = _mm_tn(h_cv, da0, nj=N_DEV, a_cols=D, g_cols=(2 * D) // N_DEV, a_by_j=False, g_by_j=True, tk=tkw,
                 name="conv_dw1")
    ddw_s = jnp.transpose(ddw[:width].reshape(width, N_DEV, D // N_DEV), (1, 0, 2))

    def pad_row(a):
        return jnp.pad(a, ((0, 0), (0, D - a.shape[1])))

    def pack(gmp, gmq, gfp, gfq, b1, dwb, lng, lnb, b2, bfv, last):
        return jnp.concatenate([gmp, gmq, gfp, gfq, b1.reshape(2, D), dwb, lng, lnb, b2, pad_row(bfv), last],
                               axis=0)

    zero_row = jnp.zeros((1, D), F32)
    small_g = pack(
        jnp.concatenate([row(s_ci, 0), row(s_ai, 0)], axis=0),
        jnp.concatenate([row(s_co, 0), row(s_ao, 0)], axis=0),
        jnp.concatenate([row(s_mlp0, 1), row(s_mlp1, 1)], axis=0),
        jnp.concatenate([row(s_mlp0, 0), row(s_mlp1, 0)], axis=0),
        db1, row(s_co, 4), row(s_co, 1), row(s_co, 2), row(s_co, 3), dbf[:, :H],
        pad_row(loss_blk[0:1, 0:1]))
    h_conv, tok = push_grads([dw1, ddw_s, dw2.reshape(N_DEV, rs, D), small_g], ["slot", "slot", "slot", "whole"],
                             "grads_conv")

    def opt(recvs, w, m, v, name):
        shp = w.shape
        L, C = shp[0], shp[-1]
        R = int(np.prod(shp[1:-1]))
        outs = _adamw([r.reshape(N_DEV, R, r.shape[-1]) for r in recvs], w.reshape(L, R, C), m.reshape(L, R, C),
                      v.reshape(L, R, C), _row_block(R, C), name)
        return [t.reshape(shp) for t in outs]

    big = {}
    r_wu1, r_wd1 = pull_grads(h_mlp1, tok, "grads_mlp1")
    r_win, r_wo = pull_grads(h_attn, r_wd1, "grads_attn")
    big["attn_w_in"] = opt([r_win], attn_w_in, m_attn_w_in, v_attn_w_in, "adamw_win")
    big["attn_w_o"] = opt([r_wo], attn_w_o, m_attn_w_o, v_attn_w_o, "adamw_wo")
    r_wu0, r_wd0 = pull_grads(h_mlp0, big["attn_w_o"][0], "grads_mlp0")
    big["mlp_w_up"] = opt([r_wu0, r_wu1], mlp_w_up, m_mlp_w_up, v_mlp_w_up, "adamw_wup")
    big["mlp_w_down"] = opt([r_wd0, r_wd1], mlp_w_down, m_mlp_w_down, v_mlp_w_down, "adamw_wdown")
    r_w1, r_dw, r_w2, r_small = pull_grads(h_conv, big["mlp_w_down"][0], "grads_conv")
    big["conv_pw1_w"] = opt([r_w1], conv_pw1_w, m_conv_pw1_w, v_conv_pw1_w, "adamw_pw1")
    big["conv_dw_w"] = opt([r_dw], conv_dw_w, m_conv_dw_w, v_conv_dw_w, "adamw_dw")
    big["conv_pw2_w"] = opt([r_w2], conv_pw2_w, m_conv_pw2_w, v_conv_pw2_w, "adamw_pw2")
    small_w = pack(g_mix_pre, g_mix_post, g_ffn_pre, g_ffn_post, conv_pw1_b, conv_dw_b, conv_ln_g, conv_ln_b,
                   conv_pw2_b, attn_b_f, zero_row)
    small_m = pack(m_g_mix_pre, m_g_mix_post, m_g_ffn_pre, m_g_ffn_post, m_conv_pw1_b, m_conv_dw_b, m_conv_ln_g,
                   m_conv_ln_b, m_conv_pw2_b, m_attn_b_f, zero_row)
    small_v = pack(v_g_mix_pre, v_g_mix_post, v_g_ffn_pre, v_g_ffn_post, v_conv_pw1_b, v_conv_dw_b, v_conv_ln_g,
                   v_conv_ln_b, v_conv_pw2_b, v_attn_b_f, zero_row)
    sm = _adamw([r_small], small_w[None], small_m[None], small_v[None], small_w.shape[0], "adamw_small")
    sm = [t[0] for t in sm]
    loss = sm[0][15, 0]

    def unpack(t):
        return {"g_mix_pre": t[0:2], "g_mix_post": t[2:4], "g_ffn_pre": t[4:6], "g_ffn_post": t[6:8],
                "conv_pw1_b": t[8:10].reshape(1, 2 * D), "conv_dw_b": t[10:11], "conv_ln_g": t[11:12],
                "conv_ln_b": t[12:13], "conv_pw2_b": t[13:14], "attn_b_f": t[14:15, :H]}

    small = [unpack(t) for t in sm]
    names = ["g_mix_pre", "g_mix_post", "g_ffn_pre", "g_ffn_post", "conv_pw1_w", "conv_pw1_b", "conv_dw_w",
             "conv_dw_b", "conv_ln_g", "conv_ln_b", "conv_pw2_w", "conv_pw2_b", "attn_w_in", "attn_b_f",
             "attn_w_o", "mlp_w_up", "mlp_w_down"]
    outs = [loss, grad_x.reshape(1, T, D)]
    for kind in range(4):
        for nme in names:
            outs.append(big[nme][kind] if nme in big else small[kind][nme])
    return tuple(outs)
```

```python
import functools
import math

import numpy as np
import jax
import jax.numpy as jnp
from jax import lax
from jax.experimental import pallas as pl
from jax.experimental.pallas import tpu as pltpu

F32 = jnp.float32
BF16 = jnp.bfloat16

RMS_EPS = 1e-6
LN_EPS = 1e-5
MASK_VALUE = -1e30
ADAM_LR = 0.001
ADAM_B1 = 0.9
ADAM_B2 = 0.999
ADAM_EPS = 1e-08
ADAM_WD = 0.01
ADAM_STEP = 10

N_DEV = 8
LANES = 128
SUBLANES = 8
CONV_HALO = 32
CONV_ROWS = 32
FLASH_ROWS = 32
VMEM_LIMIT = 56 * 1024 * 1024

_pcall = pl.pallas_call


def _params(sem=None):
    if sem is None:
        return pltpu.CompilerParams(vmem_limit_bytes=VMEM_LIMIT)
    return pltpu.CompilerParams(dimension_semantics=sem, vmem_limit_bytes=VMEM_LIMIT)


def _dot(a, b):
    return jnp.dot(a, b, preferred_element_type=F32)


def _dot_nt(a, b):
    return lax.dot_general(a, b, (((1,), (1,)), ((), ())), preferred_element_type=F32)


def _dot_tn(a, b):
    return lax.dot_general(a, b, (((0,), (0,)), ((), ())), preferred_element_type=F32)


def _full(shape):
    nd = len(shape)
    return pl.BlockSpec(shape, lambda *g: (0,) * nd, pipeline_mode=pl.Buffered(1))


def _acc(shape):
    nd = len(shape)
    return pl.BlockSpec(shape, lambda *g: (0,) * nd)


def _rows(tm, cols):
    return pl.BlockSpec((tm, cols), lambda i: (i, 0))


def _rms(x, g):
    r = lax.rsqrt(jnp.mean(x * x, axis=-1, keepdims=True) + RMS_EPS)
    return x * r * g


def _rms_bwd(x, g, dy):
    r = lax.rsqrt(jnp.mean(x * x, axis=-1, keepdims=True) + RMS_EPS)
    n = x * r
    dg = jnp.sum(dy * n, axis=0, keepdims=True)
    dn = dy * g
    dx = r * (dn - n * jnp.mean(dn * n, axis=-1, keepdims=True))
    return dx, dg


def _sigmoid(x):
    return 1.0 / (1.0 + jnp.exp(-x))


def _mesh_pos():
    return lax.axis_index("x"), lax.axis_index("y"), lax.axis_index("c")


def _dev_index(px, py, pc):
    return 4 * px + 2 * py + pc


_HBM = pl.BlockSpec(memory_space=pltpu.HBM)
_SEM = pl.BlockSpec(memory_space=pltpu.SEMAPHORE)
_EFFECT = pltpu.SideEffectType.DATAFLOW_SIDE_EFFECTING


def _peer(r, x, y, c):
    p = ((1 - x) if r & 4 else x, (1 - y) if r & 2 else y, (1 - c) if r & 1 else c)
    return p, _dev_index(*p)


def _src_ref(refs, item, me_id=None, to_id=None):
    si, sub, mode = item
    r = refs[si] if sub is None else refs[si].at[sub]
    if mode == "slot":
        return r.at[to_id]
    if mode == "own":
        return r.at[me_id]
    return r


def _place_own(srcs, items, land_shapes, name, cast=False):
    ns, n = len(srcs), len(items)

    def body(*refs):
        src = refs[:ns]
        land = refs[ns:ns + n]
        stage = refs[ns + n:ns + 2 * n] if cast else None
        sems = refs[-1]
        me_id = _dev_index(*_mesh_pos())
        cps = []
        for a, item in enumerate(items):
            s = _src_ref(src, item, to_id=me_id)
            if cast:
                if s.shape != stage[a].shape:
                    stage[a][...] = jnp.zeros_like(stage[a])
                    stage[a][:, 0:s.shape[-1]] = s[...].astype(stage[a].dtype)
                else:
                    stage[a][...] = s[...].astype(stage[a].dtype)
                s = stage[a]
            cp = pltpu.make_async_copy(s, land[a].at[me_id], sems.at[a])
            cp.start()
            cps.append(cp)
        for cp in cps:
            cp.wait()

    return _pcall(
        body, name=name,
        out_shape=[jax.ShapeDtypeStruct(tuple(s), d) for s, d in land_shapes],
        in_specs=[pl.BlockSpec(memory_space=pltpu.VMEM if cast else pl.ANY)] * ns,
        out_specs=[pl.BlockSpec(memory_space=pl.ANY)] * n,
        scratch_shapes=([pltpu.VMEM(tuple(s[1:]), d) for s, d in land_shapes] if cast else [])
        + [pltpu.SemaphoreType.DMA((n,))],
        compiler_params=pltpu.CompilerParams(vmem_limit_bytes=VMEM_LIMIT),
    )(*srcs)


def _seed_lands(srcs, modes, me, name):
    n = len(srcs)
    parts = [tuple(s.shape[1:]) if mode == "slot" else tuple(s.shape) for s, mode in zip(srcs, modes)]

    def body(me_ref, *refs):
        s = pl.program_id(0)
        for a in range(n):
            v = refs[a][...]
            refs[n + a][...] = jnp.where(s == me_ref[0], v, jnp.zeros_like(v))

    def in_spec(part, mode):
        nd = len(part)
        if mode == "slot":
            return pl.BlockSpec((None,) + part, lambda s, me_ref: (me_ref[0],) + (0,) * nd)
        return pl.BlockSpec(part, lambda s, me_ref: (0,) * nd)

    def out_spec(part):
        nd = len(part)
        return pl.BlockSpec((None,) + part, lambda s, me_ref: (s,) + (0,) * nd)

    return _pcall(
        body, name=name,
        grid_spec=pltpu.PrefetchScalarGridSpec(
            num_scalar_prefetch=1, grid=(N_DEV,),
            in_specs=[in_spec(p, m) for p, m in zip(parts, modes)],
            out_specs=[out_spec(p) for p in parts]),
        out_shape=[jax.ShapeDtypeStruct((N_DEV,) + p, s.dtype) for p, s in zip(parts, srcs)],
        compiler_params=_params(("arbitrary",)),
    )(me, *srcs)


def _push_start(srcs, lands, groups, name):
    ns, n = len(srcs), len(lands)
    ng = len(groups)
    assert sum(len(g) for g in groups) == n

    def body(*refs):
        ops = refs[:ns + n]
        land = refs[ns:ns + n]
        sems = refs[ns + n:ns + n + 2 * ng]
        token = refs[-1]
        x, y, c = _mesh_pos()
        me_id = _dev_index(x, y, c)
        a = 0
        for gi, grp in enumerate(groups):
            for k, item in enumerate(grp):
                for r in range(1, N_DEV):
                    p, pid = _peer(r, x, y, c)
                    pltpu.make_async_remote_copy(
                        src_ref=_src_ref(ops, item, me_id=me_id, to_id=pid), dst_ref=land[a].at[me_id],
                        send_sem=sems[2 * gi].at[k * (N_DEV - 1) + r - 1],
                        recv_sem=sems[2 * gi + 1].at[k * (N_DEV - 1) + r - 1],
                        device_id=p, device_id_type=pl.DeviceIdType.MESH).start()
                a += 1
        token[...] = jnp.zeros_like(token)

    sem_shapes = []
    for grp in groups:
        sem_shapes += [pltpu.SemaphoreType.DMA((len(grp) * (N_DEV - 1),))] * 2
    arrs = list(srcs) + list(lands)
    res = _pcall(
        body, name=name,
        out_shape=tuple(sem_shapes) + tuple(pltpu.HBM(a.shape, a.dtype) for a in arrs)
        + (jax.ShapeDtypeStruct((SUBLANES, LANES), F32),),
        in_specs=[_HBM] * (ns + n),
        out_specs=tuple([_SEM] * (2 * ng)) + tuple([_HBM] * (ns + n)) + (pl.BlockSpec(memory_space=pltpu.VMEM),),
        input_output_aliases={i: 2 * ng + i for i in range(ns + n)},
        compiler_params=pltpu.CompilerParams(has_side_effects=_EFFECT),
    )(*[pltpu.with_memory_space_constraint(a, pltpu.HBM) for a in arrs])
    sems = [(res[2 * gi], res[2 * gi + 1]) for gi in range(ng)]
    thru = res[2 * ng:2 * ng + ns + n]
    return sems, list(thru[:ns]), list(thru[ns:]), res[-1]


def _push_wait(sems, srcs, lands, group, after, name):
    ns, n = len(srcs), len(lands)
    assert len(group) == n

    def body(*refs):
        ops = refs[:ns + n]
        land = refs[ns:ns + n]
        send_sems, recv_sems = refs[ns + n], refs[ns + n + 1]
        x, y, c = _mesh_pos()
        me_id = _dev_index(x, y, c)
        for k, item in enumerate(group):
            for r in range(1, N_DEV):
                p, pid = _peer(r, x, y, c)
                cp = pltpu.make_async_remote_copy(
                    src_ref=_src_ref(ops, item, me_id=me_id, to_id=pid), dst_ref=land[k].at[pid],
                    send_sem=send_sems.at[k * (N_DEV - 1) + r - 1], recv_sem=recv_sems.at[k * (N_DEV - 1) + r - 1],
                    device_id=p, device_id_type=pl.DeviceIdType.MESH)
                cp.wait_send()
                cp.wait_recv()

    arrs = list(srcs) + list(lands)
    res = _pcall(
        body, name=name,
        out_shape=tuple(pltpu.HBM(a.shape, a.dtype) for a in arrs),
        in_specs=[_HBM] * (ns + n) + [_SEM, _SEM, pl.BlockSpec(memory_space=pl.ANY)],
        out_specs=tuple([_HBM] * (ns + n)),
        input_output_aliases={i: i for i in range(ns + n)},
        compiler_params=pltpu.CompilerParams(has_side_effects=_EFFECT),
    )(*arrs, sems[0], sems[1], after)
    return list(res[ns:])


def _conv_in_fwd(x, g_pre, w1g, b1, tm):
    T, D = x.shape
    ns, _, cs = w1g.shape
    half = ns // 2

    def body(x_ref, g_ref, w_ref, b_ref, a_ref, u_ref):
        h = _rms(x_ref[...], g_ref[...]).astype(BF16)
        parts = []
        for s in range(ns):
            a_s = _dot(h, w_ref[s]) + b_ref[:, s * cs:(s + 1) * cs]
            a_ref[:, s * cs:(s + 1) * cs] = a_s
            parts.append(a_s)
        for s in range(half):
            u_ref[:, s * cs:(s + 1) * cs] = parts[s] * _sigmoid(parts[s + half])

    return _pcall(
        body, name="conv_in_fwd", grid=(T // tm,),
        in_specs=[_rows(tm, D), _full((1, D)), _full(w1g.shape), _full((1, 2 * D))],
        out_specs=[_rows(tm, 2 * D), _rows(tm, D)],
        out_shape=[jax.ShapeDtypeStruct((T, 2 * D), F32), jax.ShapeDtypeStruct((T, D), F32)],
        compiler_params=_params(("arbitrary",)),
    )(x, g_pre, w1g, b1)


def _shifted_copies(ext_ref, sh_ref, tm):
    n = tm + CONV_HALO - SUBLANES
    for b in range(1, SUBLANES):
        sh_ref[b - 1, 0:n, :] = ext_ref[b:b + n, :]


def _shifted_rows(ext_ref, sh_ref, off, r0, ls):
    b = off % SUBLANES
    a8 = off - b + r0
    src = ext_ref if b == 0 else sh_ref.at[b - 1]
    return src[a8:a8 + CONV_ROWS, ls]


def _dwconv_fwd(u, w32, b, tm, lc, width):
    T, D = u.shape
    hb = tm // CONV_HALO

    def body(u_ref, halo_ref, w_ref, b_ref, y_ref, ext_ref, sh_ref):
        i = pl.program_id(0)
        ext_ref[0:CONV_HALO, :] = jnp.where(i > 0, halo_ref[...], 0.0)
        ext_ref[CONV_HALO:, :] = u_ref[...]
        _shifted_copies(ext_ref, sh_ref, tm)
        for r0 in range(0, tm, CONV_ROWS):
            for l0 in range(0, lc, LANES):
                ls = slice(l0, l0 + LANES)
                acc = jnp.zeros((CONV_ROWS, LANES), F32) + b_ref[:, ls]
                for j in range(width):
                    off = CONV_HALO - (width - 1) + j
                    acc = acc + w_ref[j:j + 1, ls] * _shifted_rows(ext_ref, sh_ref, off, r0, ls)
                y_ref[r0:r0 + CONV_ROWS, ls] = acc

    return _pcall(
        body, name="dwconv_fwd", grid=(T // tm, D // lc),
        in_specs=[pl.BlockSpec((tm, lc), lambda i, l: (i, l)),
                  pl.BlockSpec((CONV_HALO, lc), lambda i, l: (jnp.maximum(i * hb - 1, 0), l)),
                  pl.BlockSpec((32, lc), lambda i, l: (0, l)),
                  pl.BlockSpec((1, lc), lambda i, l: (0, l))],
        out_specs=pl.BlockSpec((tm, lc), lambda i, l: (i, l)),
        out_shape=jax.ShapeDtypeStruct((T, D), F32),
        scratch_shapes=[pltpu.VMEM((tm + CONV_HALO, lc), F32),
                        pltpu.VMEM((SUBLANES - 1, tm + CONV_HALO, lc), F32)],
        compiler_params=_params(("arbitrary", "arbitrary")),
    )(u, u, w32, b)


def _ln_parts(y, g, b):
    mu = jnp.mean(y, axis=-1, keepdims=True)
    yc = y - mu
    rstd = lax.rsqrt(jnp.mean(yc * yc, axis=-1, keepdims=True) + LN_EPS)
    yhat = yc * rstd
    return yhat, rstd, yhat * g + b


def _conv_out_fwd(y, x, ln_g, ln_b, w2, b2, g_post, tm):
    T, D = x.shape

    def body(y_ref, x_ref, lg_ref, lb_ref, w_ref, b_ref, g_ref, m_ref, xo_ref):
        _, _, yn = _ln_parts(y_ref[...], lg_ref[...], lb_ref[...])
        z = (yn * _sigmoid(yn)).astype(BF16)
        m = _dot(z, w_ref[...]) + b_ref[...]
        m_ref[...] = m
        xo_ref[...] = x_ref[...] + _rms(m, g_ref[...])

    return _pcall(
        body, name="conv_out_fwd", grid=(T // tm,),
        in_specs=[_rows(tm, D), _rows(tm, D), _full((1, D)), _full((1, D)), _full((D, D)), _full((1, D)),
                  _full((1, D))],
        out_specs=[_rows(tm, D), _rows(tm, D)],
        out_shape=[jax.ShapeDtypeStruct((T, D), F32), jax.ShapeDtypeStruct((T, D), F32)],
        compiler_params=_params(("arbitrary",)),
    )(y, x, ln_g, ln_b, w2, b2, g_post)


def _conv_out_bwd(dxo, m, y, ln_g, ln_b, w2, g_post, tm, dep):
    T, D = m.shape

    def body(dxo_ref, m_ref, y_ref, lg_ref, lb_ref, w_ref, g_ref, dep_ref, dy_ref, dm_ref, z_ref, sums_ref):
        i = pl.program_id(0)
        dm, dgpost = _rms_bwd(m_ref[...], g_ref[...], dxo_ref[...])
        dmb = dm.astype(BF16)
        dm_ref[...] = dmb
        yhat, rstd, yn = _ln_parts(y_ref[...], lg_ref[...], lb_ref[...])
        sg = _sigmoid(yn)
        z_ref[...] = (yn * sg).astype(BF16)
        dz = _dot_nt(dmb, w_ref[...])
        dyn = dz * (sg + yn * sg * (1.0 - sg))
        dyh = dyn * lg_ref[...]
        dy = rstd * (dyh - jnp.mean(dyh, axis=-1, keepdims=True)
                     - yhat * jnp.mean(dyh * yhat, axis=-1, keepdims=True))
        dy_ref[...] = dy

        @pl.when(i == 0)
        def _():
            sums_ref[...] = jnp.zeros_like(sums_ref)
        sums_ref[0:1, :] += dgpost
        sums_ref[1:2, :] += jnp.sum(dyn * yhat, axis=0, keepdims=True)
        sums_ref[2:3, :] += jnp.sum(dyn, axis=0, keepdims=True)
        sums_ref[3:4, :] += jnp.sum(dm, axis=0, keepdims=True)
        sums_ref[4:5, :] += jnp.sum(dy, axis=0, keepdims=True)

    return _pcall(
        body, name="conv_out_bwd", grid=(T // tm,),
        in_specs=[_rows(tm, D), _rows(tm, D), _rows(tm, D), _full((1, D)), _full((1, D)), _full((D, D)),
                  _full((1, D)), pl.BlockSpec(memory_space=pl.ANY)],
        out_specs=[_rows(tm, D), _rows(tm, D), _rows(tm, D), _acc((SUBLANES, D))],
        out_shape=[jax.ShapeDtypeStruct((T, D), F32), jax.ShapeDtypeStruct((T, D), BF16),
                   jax.ShapeDtypeStruct((T, D), BF16), jax.ShapeDtypeStruct((SUBLANES, D), F32)],
        compiler_params=_params(("arbitrary",)),
    )(dxo, m, y, ln_g, ln_b, w2, g_post, dep)


def _dwconv_bwd(dy, u, w32, tm, lc, width):
    T, D = u.shape
    hb = tm // CONV_HALO
    nt = T // tm
    last_halo = T // CONV_HALO - 1

    def body(dy_ref, dyn_ref, u_ref, up_ref, w_ref, du_ref, dw_ref, exty_ref, extu_ref, acc_ref, shy_ref, shu_ref):
        i = pl.program_id(1)
        exty_ref[0:tm, :] = dy_ref[...]
        exty_ref[tm:, :] = jnp.where(i < nt - 1, dyn_ref[...], 0.0)
        extu_ref[0:CONV_HALO, :] = jnp.where(i > 0, up_ref[...], 0.0)
        extu_ref[CONV_HALO:, :] = u_ref[...]
        _shifted_copies(exty_ref, shy_ref, tm)
        _shifted_copies(extu_ref, shu_ref, tm)

        @pl.when(i == 0)
        def _():
            acc_ref[...] = jnp.zeros_like(acc_ref)

        for r0 in range(0, tm, CONV_ROWS):
            for l0 in range(0, lc, LANES):
                ls = slice(l0, l0 + LANES)
                dyc = exty_ref[r0:r0 + CONV_ROWS, ls]
                du = jnp.zeros((CONV_ROWS, LANES), F32)
                for j in range(width):
                    du = du + w_ref[j:j + 1, ls] * _shifted_rows(exty_ref, shy_ref, (width - 1) - j, r0, ls)
                    prod = dyc * _shifted_rows(extu_ref, shu_ref, CONV_HALO - (width - 1) + j, r0, ls)
                    acc_ref[j, :, ls] += prod.reshape(CONV_ROWS // SUBLANES, SUBLANES, LANES).sum(axis=0)
                du_ref[r0:r0 + CONV_ROWS, ls] = du

        @pl.when(i == nt - 1)
        def _():
            for j in range(32):
                dw_ref[j:j + 1, :] = jnp.sum(acc_ref[j], axis=0, keepdims=True)

    return _pcall(
        body, name="dwconv_bwd", grid=(D // lc, nt),
        in_specs=[pl.BlockSpec((tm, lc), lambda l, i: (i, l)),
                  pl.BlockSpec((CONV_HALO, lc), lambda l, i: (jnp.minimum((i + 1) * hb, last_halo), l)),
                  pl.BlockSpec((tm, lc), lambda l, i: (i, l)),
                  pl.BlockSpec((CONV_HALO, lc), lambda l, i: (jnp.maximum(i * hb - 1, 0), l)),
                  pl.BlockSpec((32, lc), lambda l, i: (0, l))],
        out_specs=[pl.BlockSpec((tm, lc), lambda l, i: (i, l)),
                   pl.BlockSpec((32, lc), lambda l, i: (0, l))],
        out_shape=[jax.ShapeDtypeStruct((T, D), F32), jax.ShapeDtypeStruct((32, D), F32)],
        scratch_shapes=[pltpu.VMEM((tm + CONV_HALO, lc), F32), pltpu.VMEM((tm + CONV_HALO, lc), F32),
                        pltpu.VMEM((32, SUBLANES, lc), F32),
                        pltpu.VMEM((SUBLANES - 1, tm + CONV_HALO, lc), F32),
                        pltpu.VMEM((SUBLANES - 1, tm + CONV_HALO, lc), F32)],
        compiler_params=_params(("arbitrary", "arbitrary")),
    )(dy, dy, u, u, w32)


def _conv_in_bwd(dxo, du, a, x, g_pre, w1g, tm):
    T, D = x.shape
    ns, _, cs = w1g.shape
    half = ns // 2

    def body(dxo_ref, du_ref, a_ref, x_ref, g_ref, w_ref, dxi_ref, h_ref, da_ref, sums_ref, db_ref):
        i = pl.program_id(0)
        xv = x_ref[...]
        h_ref[...] = _rms(xv, g_ref[...]).astype(BF16)
        dh = jnp.zeros((tm, D), F32)
        dbs = [None] * ns
        for s in range(half):
            a_u = a_ref[:, s * cs:(s + 1) * cs]
            sg = _sigmoid(a_ref[:, (s + half) * cs:(s + half + 1) * cs])
            du_s = du_ref[:, s * cs:(s + 1) * cs]
            da_u = du_s * sg
            da_g = du_s * a_u * sg * (1.0 - sg)
            for s2, v in ((s, da_u), (s + half, da_g)):
                vb = v.astype(BF16)
                da_ref[:, s2 * cs:(s2 + 1) * cs] = vb
                dbs[s2] = jnp.sum(v, axis=0, keepdims=True)
                dh = dh + _dot_nt(vb, w_ref[s2])
        dxi, dgpre = _rms_bwd(xv, g_ref[...], dh)
        dxi_ref[...] = dxo_ref[...] + dxi

        @pl.when(i == 0)
        def _():
            sums_ref[...] = jnp.zeros_like(sums_ref)
            db_ref[...] = jnp.zeros_like(db_ref)
        sums_ref[0:1, :] += dgpre
        for s in range(ns):
            db_ref[:, s * cs:(s + 1) * cs] += dbs[s]

    return _pcall(
        body, name="conv_in_bwd", grid=(T // tm,),
        in_specs=[_rows(tm, D), _rows(tm, D), _rows(tm, 2 * D), _rows(tm, D), _full((1, D)),
                  _full(w1g.shape)],
        out_specs=[_rows(tm, D), _rows(tm, D), _rows(tm, 2 * D), _acc((SUBLANES, D)), _acc((1, 2 * D))],
        out_shape=[jax.ShapeDtypeStruct((T, D), F32), jax.ShapeDtypeStruct((T, D), BF16),
                   jax.ShapeDtypeStruct((T, 2 * D), BF16), jax.ShapeDtypeStruct((SUBLANES, D), F32),
                   jax.ShapeDtypeStruct((1, 2 * D), F32)],
        compiler_params=_params(("arbitrary",)),
    )(dxo, du, a, x, g_pre, w1g)


def _mlp_fwd(x, g_pre, wug, wdg, g_post, tm, name):
    T, D = x.shape
    ns, _, fs = wug.shape

    def body(x_ref, gp_ref, wu_ref, wd_ref, gq_ref, up_ref, m_ref, xo_ref):
        xv = x_ref[...]
        h = _rms(xv, gp_ref[...]).astype(BF16)
        acc = jnp.zeros((tm, D), F32)
        for s in range(ns):
            up = _dot(h, wu_ref[s]).astype(BF16)
            up_ref[:, s * fs:(s + 1) * fs] = up
            act = jnp.square(jnp.maximum(up.astype(F32), 0.0)).astype(BF16)
            acc = acc + _dot(act, wd_ref[s])
        m_ref[...] = acc
        xo_ref[...] = xv + _rms(acc, gq_ref[...])

    return _pcall(
        body, name=name, grid=(T // tm,),
        in_specs=[_rows(tm, D), _full((1, D)), _full(wug.shape), _full(wdg.shape), _full((1, D))],
        out_specs=[_rows(tm, ns * fs), _rows(tm, D), _rows(tm, D)],
        out_shape=[jax.ShapeDtypeStruct((T, ns * fs), BF16), jax.ShapeDtypeStruct((T, D), F32),
                   jax.ShapeDtypeStruct((T, D), F32)],
        compiler_params=_params(("arbitrary",)),
    )(x, g_pre, wug, wdg, g_post)


def _mlp_bwd(dxo, m, x, up, g_pre, wug, wdg, g_post, tm, name, dep, from_loss=False):
    T, D = x.shape
    ns, _, fs = wug.shape

    def body(dxo_ref, m_ref, x_ref, up_ref, gp_ref, wu_ref, wd_ref, gq_ref, dep_ref,
             dxi_ref, h_ref, dm_ref, dup_ref, sums_ref):
        i = pl.program_id(0)
        if from_loss:
            err = dxo_ref[...] - dep_ref[...]
            dxo = err * (1.0 / D)
        else:
            dxo = dxo_ref[...]
        dm, dgpost = _rms_bwd(m_ref[...], gq_ref[...], dxo)
        dmb = dm.astype(BF16)
        dm_ref[...] = dmb
        xv = x_ref[...]
        h_ref[...] = _rms(xv, gp_ref[...]).astype(BF16)
        dh = jnp.zeros((tm, D), F32)
        for s in range(ns):
            dact = _dot_nt(dmb, wd_ref[s])
            up = up_ref[:, s * fs:(s + 1) * fs].astype(F32)
            dup = (dact * (2.0 * jnp.maximum(up, 0.0))).astype(BF16)
            dup_ref[:, s * fs:(s + 1) * fs] = dup
            dh = dh + _dot_nt(dup, wu_ref[s])
        dxi, dgpre = _rms_bwd(xv, gp_ref[...], dh)
        dxi_ref[...] = dxo + dxi

        @pl.when(i == 0)
        def _():
            sums_ref[...] = jnp.zeros_like(sums_ref)
        sums_ref[0:1, :] += dgpost
        sums_ref[1:2, :] += dgpre
        if from_loss:
            sums_ref[2:3, :] += 0.5 * jnp.sum(jnp.mean(err * err, axis=-1, keepdims=True), axis=0, keepdims=True)

    return _pcall(
        body, name=name, grid=(T // tm,),
        in_specs=[_rows(tm, D), _rows(tm, D), _rows(tm, D), _rows(tm, ns * fs), _full((1, D)),
                  _full(wug.shape), _full(wdg.shape), _full((1, D)),
                  _rows(tm, D) if from_loss else pl.BlockSpec(memory_space=pl.ANY)],
        out_specs=[_rows(tm, D), _rows(tm, D), _rows(tm, D), _rows(tm, ns * fs), _acc((SUBLANES, D))],
        out_shape=[jax.ShapeDtypeStruct((T, D), F32), jax.ShapeDtypeStruct((T, D), BF16),
                   jax.ShapeDtypeStruct((T, D), BF16), jax.ShapeDtypeStruct((T, ns * fs), BF16),
                   jax.ShapeDtypeStruct((SUBLANES, D), F32)],
        compiler_params=_params(("arbitrary",)),
    )(dxo, m, x, up, g_pre, wug, wdg, g_post, dep)


def _mm_tn(a, g, *, nj, a_cols, g_cols, a_by_j, g_by_j, tk, name, act=False):
    T = a.shape[0]
    nk = T // tk

    def body(a_ref, g_ref, o_ref, acc_ref):
        k = pl.program_id(1)
        av = a_ref[...]
        if act:
            av = jnp.square(jnp.maximum(av.astype(F32), 0.0)).astype(BF16)
        p = _dot_tn(av, g_ref[...])

        @pl.when(k == 0)
        def _():
            acc_ref[...] = p

        @pl.when(k > 0)
        def _():
            acc_ref[...] += p

        @pl.when(k == nk - 1)
        def _():
            o_ref[...] = acc_ref[...].astype(BF16)

    return _pcall(
        body, name=name, grid=(nj, nk),
        in_specs=[pl.BlockSpec((tk, a_cols), (lambda j, k: (k, j)) if a_by_j else (lambda j, k: (k, 0))),
                  pl.BlockSpec((tk, g_cols), (lambda j, k: (k, j)) if g_by_j else (lambda j, k: (k, 0)))],
        out_specs=pl.BlockSpec((None, a_cols, g_cols), lambda j, k: (j, 0, 0)),
        out_shape=jax.ShapeDtypeStruct((nj, a_cols, g_cols), BF16),
        scratch_shapes=[pltpu.VMEM((a_cols, g_cols), F32)],
        compiler_params=_params(("arbitrary", "arbitrary")),
    )(a, g)


def _attn_in_fwd(x, g_pre, wqkv, wf, bf, tm, q_mul, n_heads):
    T, D = x.shape

    def body(x_ref, g_ref, w_ref, wf_ref, bf_ref, q_ref, k_ref, v_ref, lf_ref):
        h = _rms(x_ref[...], g_ref[...]).astype(BF16)
        q = _dot(h, w_ref[:, 0:D])
        if q_mul != 1.0:
            q = q * q_mul
        q_ref[...] = q.astype(BF16)
        k_ref[...] = _dot(h, w_ref[:, D:2 * D]).astype(BF16)
        v_ref[...] = _dot(h, w_ref[:, 2 * D:3 * D]).astype(BF16)
        fl = _dot(h, wf_ref[...]) + bf_ref[...]
        lf = jnp.minimum(fl, 0.0) - jnp.log(1.0 + jnp.exp(-jnp.abs(fl)))
        lane = lax.broadcasted_iota(jnp.int32, (1, LANES), 1)
        lf_ref[...] = jnp.where(lane < n_heads, lf, 0.0)

    return _pcall(
        body, name="attn_in_fwd", grid=(T // tm,),
        in_specs=[_rows(tm, D), _full((1, D)), _full((D, 3 * D)), _full((D, LANES)), _full((1, LANES))],
        out_specs=[_rows(tm, D), _rows(tm, D), _rows(tm, D), _rows(tm, LANES)],
        out_shape=[jax.ShapeDtypeStruct((T, D), BF16)] * 3 + [jax.ShapeDtypeStruct((T, LANES), F32)],
        compiler_params=_params(("arbitrary",)),
    )(x, g_pre, wqkv, wf, bf)


def _cumsum_rows(v, v2, tb, reverse, name):
    T, C = v.shape
    nb = T // tb

    def body(v_ref, v2_ref, o_ref, carry_ref):
        i = pl.program_id(0)

        @pl.when(i == 0)
        def _():
            carry_ref[...] = jnp.zeros_like(carry_ref)
        r = lax.broadcasted_iota(jnp.int32, (tb, tb), 0)
        c = lax.broadcasted_iota(jnp.int32, (tb, tb), 1)
        tri = jnp.where((c >= r) if reverse else (c <= r), 1.0, 0.0).astype(F32)
        out = jnp.dot(tri, v_ref[...] + v2_ref[...], precision=lax.Precision.HIGHEST,
                      preferred_element_type=F32) + carry_ref[...]
        o_ref[...] = out
        carry_ref[...] = out[0:1, :] if reverse else out[tb - 1:tb, :]

    idx = (lambda i: (nb - 1 - i, 0)) if reverse else (lambda i: (i, 0))
    return _pcall(
        body, name=name, grid=(nb,),
        in_specs=[pl.BlockSpec((tb, C), idx), pl.BlockSpec((tb, C), idx)],
        out_specs=pl.BlockSpec((tb, C), idx),
        out_shape=jax.ShapeDtypeStruct((T, C), F32),
        scratch_shapes=[pltpu.VMEM((1, C), F32)],
        compiler_params=_params(("arbitrary",)),
    )(v, v2)


def _head_col(v, lane, h):
    return jnp.sum(jnp.where(lane == h, v, 0.0), axis=1, keepdims=True)


def _flash_fwd(q, k, v, fq_aux, fk_rows, *, dh, tq, s_mul):
    T, D = q.shape
    G = D // LANES
    hpg = LANES // dh
    nq = T // tq
    k3 = k.reshape(nq, tq, D)
    v3 = v.reshape(nq, tq, D)

    rc = min(FLASH_ROWS, tq)

    def body(q_ref, k_ref, v_ref, fq_ref, fk_ref, o_ref, o32_ref, lse_ref, s_scr, p_scr):
        i = pl.program_id(1)
        lane = lax.broadcasted_iota(jnp.int32, (1, LANES), 1)
        q2 = q_ref[...]
        hmasks = [(lane >= hh * dh) & (lane < (hh + 1) * dh) for hh in range(hpg)]
        qms = [jnp.where(hm, q2, jnp.zeros_like(q2)) for hm in hmasks]

        nlb = tq // LANES
        sum_lane = [((hh + 1) % hpg) * dh for hh in range(hpg)]

        def scores(j, slot):
            kj = k_ref[j]
            for hh in range(hpg):
                s = _dot_nt(qms[hh], kj)
                s_scr[slot, hh] = s if s_mul == 1.0 else s * s_mul

        def soft(j, slot, carry, masked):
            vj = v_ref[j]
            out = []
            for hh in range(hpg):
                m_b, acc = carry[hh]
                fk_row = fk_ref[j, hh:hh + 1, :]
                def future(r0, cb):
                    return masked and cb * LANES > r0 + rc - 1

                mx = []
                for r0 in range(0, tq, rc):
                    rs = slice(r0, r0 + rc)
                    c = None
                    for cb in range(nlb):
                        if future(r0, cb):
                            continue
                        cs = slice(cb * LANES, (cb + 1) * LANES)
                        s = s_scr[slot, hh, rs, cs] - fk_row[:, cs]
                        if masked and (cb + 1) * LANES - 1 > r0:
                            ri = r0 + lax.broadcasted_iota(jnp.int32, (rc, LANES), 0)
                            ci = cb * LANES + lax.broadcasted_iota(jnp.int32, (rc, LANES), 1)
                            s = jnp.where(ci <= ri, s, MASK_VALUE)
                        s_scr[slot, hh, rs, cs] = s
                        c = s if c is None else jnp.maximum(c, s)
                    mx.append(c)
                row_max = jnp.max(jnp.concatenate(mx, axis=0), axis=1, keepdims=True)
                m_new = jnp.maximum(m_b, row_max)
                alpha = jnp.exp(m_b - m_new)
                for r0 in range(0, tq, rc):
                    rs = slice(r0, r0 + rc)
                    m_c = m_new[rs]
                    for cb in range(nlb):
                        cs = slice(cb * LANES, (cb + 1) * LANES)
                        if future(r0, cb):
                            p_scr[hh, rs, cs] = jnp.zeros((rc, LANES), BF16)
                        else:
                            p_scr[hh, rs, cs] = jnp.exp(s_scr[slot, hh, rs, cs] - m_c).astype(BF16)
                v_one = jnp.where(hmasks[hh], vj, jnp.ones_like(vj))
                out.append((m_new, alpha * acc + _dot(p_scr[hh], v_one)))
            return tuple(out)

        def step(j, carry, masked):
            scores(j, 0)
            return soft(j, 0, carry, masked)

        init = tuple((jnp.full((tq, LANES), MASK_VALUE, F32), jnp.zeros((tq, LANES), F32)) for _ in range(hpg))
        carry = lax.fori_loop(0, i, lambda j, cr: step(j, cr, False), init)
        carry = step(i, carry, True)
        fq = fq_ref[...]
        g0 = pl.program_id(0) * hpg
        o_all = jnp.zeros((tq, LANES), F32)
        lse_all = jnp.zeros((tq, LANES), F32)
        for hh in range(hpg):
            m_b, acc = carry[hh]
            l = acc[:, sum_lane[hh]:sum_lane[hh] + 1]
            o_all = jnp.where(hmasks[hh], acc * (1.0 / l), o_all)
            lse_all = jnp.where(lane == hh, m_b[:, 0:1] + jnp.log(l) + _head_col(fq, lane, g0 + hh), lse_all)
        o_ref[...] = o_all.astype(BF16)
        o32_ref[...] = o_all
        lse_ref[...] = lse_all

    return _pcall(
        body, name="flash_fwd", grid=(G, nq),
        in_specs=[pl.BlockSpec((tq, LANES), lambda g, i: (i, g)),
                  pl.BlockSpec((nq, tq, LANES), lambda g, i: (0, 0, g)),
                  pl.BlockSpec((nq, tq, LANES), lambda g, i: (0, 0, g)),
                  pl.BlockSpec((tq, LANES), lambda g, i: (i, 0)),
                  pl.BlockSpec((None, nq, SUBLANES, tq), lambda g, i: (g, 0, 0, 0))],
        out_specs=[pl.BlockSpec((tq, LANES), lambda g, i: (i, g)),
                   pl.BlockSpec((tq, LANES), lambda g, i: (i, g)),
                   pl.BlockSpec((None, tq, LANES), lambda g, i: (g, i, 0))],
        out_shape=[jax.ShapeDtypeStruct((T, D), BF16), jax.ShapeDtypeStruct((T, D), F32),
                   jax.ShapeDtypeStruct((G, T, LANES), F32)],
        scratch_shapes=[pltpu.VMEM((1, hpg, tq, tq), F32), pltpu.VMEM((hpg, tq, tq), BF16)],
        compiler_params=_params(("arbitrary", "arbitrary")),
    )(q, k3, v3, fq_aux, fk_rows)


def _flash_bwd(q, k, v, do, fq_aux, lse_aux, dl_aux, fk_rows, *, dh, tq, tk, s_mul, dq_mul):
    T, D = q.shape
    G = D // LANES
    hpg = LANES // dh
    nq = T // tq
    nk = T // tk
    per = tq // tk
    k3 = k.reshape(nk, tk, D)
    v3 = v.reshape(nk, tk, D)
    rc = min(FLASH_ROWS, tq)

    def body(q_ref, k_ref, v_ref, do_ref, fq_ref, lse_ref, dl_ref, fk_ref,
             dq_ref, dk_ref, dv_ref, dfq_ref, dfk_ref, dk_acc, dv_acc, s_scr, dp_scr, p_scr, ds_scr):
        i = pl.program_id(1)

        @pl.when(i == 0)
        def _():
            dk_acc[...] = jnp.zeros_like(dk_acc)
            dv_acc[...] = jnp.zeros_like(dv_acc)
            dfk_ref[...] = jnp.zeros_like(dfk_ref)

        lane = lax.broadcasted_iota(jnp.int32, (1, LANES), 1)
        q2 = q_ref[...]
        do2 = do_ref[...]
        g0 = pl.program_id(0) * hpg
        fq = fq_ref[...]
        lse = lse_ref[...]
        dl = dl_ref[...]
        hmasks = [(lane >= hh * dh) & (lane < (hh + 1) * dh) for hh in range(hpg)]
        qms = [jnp.where(hm, q2, jnp.zeros_like(q2)) for hm in hmasks]
        doms = [jnp.where(hm, do2, jnp.zeros_like(do2)) for hm in hmasks]
        q2t = q2.astype(F32).T.astype(BF16)
        do2t = do2.astype(F32).T.astype(BF16)
        ones_rows = jnp.ones((2 * SUBLANES, tk), BF16)
        q1t = [jnp.concatenate([q2t[hh * dh:(hh + 1) * dh], jnp.ones((2 * SUBLANES, tq), BF16)], axis=0)
               for hh in range(hpg)]
        c_bs = [jnp.broadcast_to(_head_col(fq, lane, g0 + hh) - lse[:, hh:hh + 1], (tq, LANES))
                for hh in range(hpg)]
        dl_bs = [jnp.broadcast_to(_head_col(dl, lane, g0 + hh), (tq, LANES)) for hh in range(hpg)]
        nlb = tk // LANES

        def step(j, carry, off, slot=0):
            masked = off is not None
            kj = k_ref[j]
            vj = v_ref[j]
            kjt = kj.astype(F32).T.astype(BF16)
            for hh in range(hpg):
                s = _dot_nt(qms[hh], kj)
                s_scr[slot, hh] = s if s_mul == 1.0 else s * s_mul
                dp_scr[slot, hh] = _dot_nt(doms[hh], vj)
            out = []
            for hh in range(hpg):
                fk_row = fk_ref[j, hh:hh + 1, :]
                for r0 in range(0, tq, rc):
                    rs = slice(r0, r0 + rc)
                    if masked and r0 + rc <= off:
                        p_scr[slot, hh, rs, :] = jnp.zeros((rc, tk), BF16)
                        ds_scr[slot, hh, rs, :] = jnp.zeros((rc, tk), BF16)
                        continue
                    c_c = c_bs[hh][rs]
                    dl_c = dl_bs[hh][rs]
                    for cb in range(nlb):
                        cs = slice(cb * LANES, (cb + 1) * LANES)
                        if masked and off + cb * LANES > r0 + rc - 1:
                            p_scr[slot, hh, rs, cs] = jnp.zeros((rc, LANES), BF16)
                            ds_scr[slot, hh, rs, cs] = jnp.zeros((rc, LANES), BF16)
                            continue
                        e = (s_scr[slot, hh, rs, cs] - fk_row[:, cs]) + c_c
                        if masked and off + (cb + 1) * LANES - 1 > r0:
                            ri = r0 + lax.broadcasted_iota(jnp.int32, (rc, LANES), 0)
                            ci = off + cb * LANES + lax.broadcasted_iota(jnp.int32, (rc, LANES), 1)
                            e = jnp.where(ci <= ri, e, MASK_VALUE)
                        p = jnp.exp(e)
                        ds = p * (dp_scr[slot, hh, rs, cs] - dl_c)
                        p_scr[slot, hh, rs, cs] = p.astype(BF16)
                        ds_scr[slot, hh, rs, cs] = ds.astype(BF16)
                k1t = jnp.concatenate([kjt[hh * dh:(hh + 1) * dh], ones_rows], axis=0)
                out.append(carry[hh] + _dot_nt(k1t, ds_scr[slot, hh]))
            dks = [_dot(q1t[hh], ds_scr[slot, hh]) for hh in range(hpg)]
            for hh in range(hpg):
                dfk_ref[j, hh:hh + 1, :] += -dks[hh][dh:dh + 1, :]
            dk_acc[j] += jnp.concatenate([d[0:dh] for d in dks], axis=0)
            dv_acc[j] += jnp.concatenate(
                [_dot(do2t[hh * dh:(hh + 1) * dh], p_scr[slot, hh]) for hh in range(hpg)], axis=0)
            return tuple(out)

        init = tuple(jnp.zeros((dh + 2 * SUBLANES, tq), F32) for _ in range(hpg))
        n_past = i * per

        def pair(t, cr):
            return step(2 * t + 1, step(2 * t, cr, None, 0), None, 1)

        carry = lax.fori_loop(0, n_past // 2, pair, init)
        carry = lax.cond(n_past % 2 == 1, lambda cr: step(n_past - 1, cr, None, 0), lambda cr: cr, carry)
        for d in range(per):
            carry = step(i * per + d, carry, d * tk, (d + 1) % 2)
        dq_all = jnp.concatenate([carry[hh][0:dh] for hh in range(hpg)], axis=0).T
        dq_ref[...] = (dq_all * dq_mul).astype(BF16)
        dfq_ref[...] = jnp.concatenate([carry[hh][dh:dh + 1, :] for hh in range(hpg)]
                                       + [jnp.zeros((SUBLANES - hpg, tq), F32)], axis=0)

        @pl.when(i == nq - 1)
        def _():
            for jj in range(nk):
                dkv = dk_acc[jj].T
                if s_mul != 1.0:
                    dkv = dkv * s_mul
                dk_ref[jj] = dkv.astype(BF16)
                dv_ref[jj] = dv_acc[jj].T.astype(BF16)

    blk = pl.BlockSpec((tq, LANES), lambda g, i: (i, g))
    res = pl.BlockSpec((nk, tk, LANES), lambda g, i: (0, 0, g))
    res_in = pl.BlockSpec((nk, tk, LANES), lambda g, i: (0, 0, g), pipeline_mode=pl.Buffered(1))
    aux = pl.BlockSpec((None, tq, LANES), lambda g, i: (g, i, 0))
    heads = pl.BlockSpec((tq, LANES), lambda g, i: (i, 0))
    rows = pl.BlockSpec((None, nk, SUBLANES, tk), lambda g, i: (g, 0, 0, 0))
    dq, dk3, dv3, dfq, dfk = _pcall(
        body, name="flash_bwd", grid=(G, nq),
        in_specs=[blk, res_in, res_in, blk, heads, aux, heads, rows],
        out_specs=[blk, res, res, pl.BlockSpec((None, None, SUBLANES, tq), lambda g, i: (g, i, 0, 0)), rows],
        out_shape=[jax.ShapeDtypeStruct((T, D), BF16), jax.ShapeDtypeStruct((nk, tk, D), BF16),
                   jax.ShapeDtypeStruct((nk, tk, D), BF16), jax.ShapeDtypeStruct((G, nq, SUBLANES, tq), F32),
                   jax.ShapeDtypeStruct((G, nk, SUBLANES, tk), F32)],
        scratch_shapes=[pltpu.VMEM((nk, LANES, tk), F32), pltpu.VMEM((nk, LANES, tk), F32),
                        pltpu.VMEM((2, hpg, tq, tk), F32), pltpu.VMEM((2, hpg, tq, tk), F32),
                        pltpu.VMEM((2, hpg, tq, tk), BF16), pltpu.VMEM((2, hpg, tq, tk), BF16)],
        compiler_params=_params(("arbitrary", "arbitrary")),
    )(q, k3, v3, do, fq_aux, lse_aux, dl_aux, fk_rows)
    return dq, dk3.reshape(T, D), dv3.reshape(T, D), dfq, dfk


def _attn_out_fwd(o, x, wo, g_post, tm):
    T, D = x.shape

    def body(o_ref, x_ref, w_ref, g_ref, m_ref, xo_ref):
        m = _dot(o_ref[...], w_ref[...])
        m_ref[...] = m
        xo_ref[...] = x_ref[...] + _rms(m, g_ref[...])

    return _pcall(
        body, name="attn_out_fwd", grid=(T // tm,),
        in_specs=[_rows(tm, D), _rows(tm, D), _full((D, D)), _full((1, D))],
        out_specs=[_rows(tm, D), _rows(tm, D)],
        out_shape=[jax.ShapeDtypeStruct((T, D), F32), jax.ShapeDtypeStruct((T, D), F32)],
        compiler_params=_params(("arbitrary",)),
    )(o, x, wo, g_post)


def _attn_out_bwd(dxo, m, o, wo, g_post, head_ind, tm, dep):
    T, D = m.shape

    def body(dxo_ref, m_ref, o_ref, w_ref, g_ref, ind_ref, dep_ref, dm_ref, do_ref, dl_ref, sums_ref):
        i = pl.program_id(0)
        dm, dgpost = _rms_bwd(m_ref[...], g_ref[...], dxo_ref[...])
        dmb = dm.astype(BF16)
        dm_ref[...] = dmb
        dob = _dot_nt(dmb, w_ref[...]).astype(BF16)
        do_ref[...] = dob
        dl_ref[...] = jnp.dot(dob.astype(F32) * o_ref[...], ind_ref[...], precision=lax.Precision.HIGHEST,
                              preferred_element_type=F32)

        @pl.when(i == 0)
        def _():
            sums_ref[...] = jnp.zeros_like(sums_ref)
        sums_ref[0:1, :] += dgpost

    return _pcall(
        body, name="attn_out_bwd", grid=(T // tm,),
        in_specs=[_rows(tm, D), _rows(tm, D), _rows(tm, D), _full((D, D)), _full((1, D)), _full((D, LANES)),
                  pl.BlockSpec(memory_space=pl.ANY)],
        out_specs=[_rows(tm, D), _rows(tm, D), _rows(tm, LANES), _acc((SUBLANES, D))],
        out_shape=[jax.ShapeDtypeStruct((T, D), BF16), jax.ShapeDtypeStruct((T, D), BF16),
                   jax.ShapeDtypeStruct((T, LANES), F32), jax.ShapeDtypeStruct((SUBLANES, D), F32)],
        compiler_params=_params(("arbitrary",)),
    )(dxo, m, o, wo, g_post, head_ind, dep)


def _attn_in_bwd(dxo, x, g_pre, dq, dk, dv, dlf, lf, wqkv, wf, tm, n_heads):
    T, D = x.shape

    def body(dxo_ref, x_ref, g_ref, dq_ref, dk_ref, dv_ref, dlf_ref, lf_ref, w_ref, wf_ref,
             dxi_ref, h_ref, df_ref, sums_ref, dbf_ref):
        i = pl.program_id(0)
        xv = x_ref[...]
        h_ref[...] = _rms(xv, g_ref[...]).astype(BF16)
        lane = lax.broadcasted_iota(jnp.int32, (1, LANES), 1)
        df = jnp.where(lane < n_heads, dlf_ref[...] * (1.0 - jnp.exp(lf_ref[...])), 0.0)
        dfb = df.astype(BF16)
        df_ref[...] = dfb
        dh = (_dot_nt(dq_ref[...], w_ref[:, 0:D]) + _dot_nt(dk_ref[...], w_ref[:, D:2 * D])
              + _dot_nt(dv_ref[...], w_ref[:, 2 * D:3 * D]) + _dot_nt(dfb, wf_ref[...]))
        dxi, dgpre = _rms_bwd(xv, g_ref[...], dh)
        dxi_ref[...] = dxo_ref[...] + dxi

        @pl.when(i == 0)
        def _():
            sums_ref[...] = jnp.zeros_like(sums_ref)
            dbf_ref[...] = jnp.zeros_like(dbf_ref)
        sums_ref[0:1, :] += dgpre
        dbf_ref[...] += jnp.sum(df, axis=0, keepdims=True)

    return _pcall(
        body, name="attn_in_bwd", grid=(T // tm,),
        in_specs=[_rows(tm, D), _rows(tm, D), _full((1, D)), _rows(tm, D), _rows(tm, D), _rows(tm, D),
                  _rows(tm, LANES), _rows(tm, LANES), _full((D, 3 * D)), _full((D, LANES))],
        out_specs=[_rows(tm, D), _rows(tm, D), _rows(tm, LANES), _acc((SUBLANES, D)), _acc((1, LANES))],
        out_shape=[jax.ShapeDtypeStruct((T, D), F32), jax.ShapeDtypeStruct((T, D), BF16),
                   jax.ShapeDtypeStruct((T, LANES), BF16), jax.ShapeDtypeStruct((SUBLANES, D), F32),
                   jax.ShapeDtypeStruct((1, LANES), F32)],
        compiler_params=_params(("arbitrary",)),
    )(dxo, x, g_pre, dq, dk, dv, dlf, lf, wqkv, wf)


def _adamw(recvs, w, m, v, tr, name):
    L, R, C = w.shape
    assert len(recvs) == L
    c1 = 1.0 - ADAM_B1 ** ADAM_STEP
    c2 = 1.0 - ADAM_B2 ** ADAM_STEP

    def body(*refs):
        r_refs = refs[:L]
        w_ref, m_ref, v_ref, g_ref, d_ref, nm_ref, nv_ref = refs[L:]
        layer = pl.program_id(0)
        g = None
        for k in range(L):
            gk = r_refs[k][0, :, 0:C].astype(F32)
            for s in range(1, N_DEV):
                gk = gk + r_refs[k][s, :, 0:C].astype(F32)
            g = gk if g is None else jnp.where(layer == k, gk, g)
        nm = ADAM_B1 * m_ref[...] + (1.0 - ADAM_B1) * g
        nv = ADAM_B2 * v_ref[...] + (1.0 - ADAM_B2) * jnp.square(g)
        m_hat = nm / c1
        v_hat = nv / c2
        g_ref[...] = g
        d_ref[...] = -ADAM_LR * (m_hat / (jnp.sqrt(v_hat) + ADAM_EPS) + ADAM_WD * w_ref[...])
        nm_ref[...] = nm
        nv_ref[...] = nv

    def recv_spec(k):
        return pl.BlockSpec((N_DEV, tr, recvs[k].shape[-1]), lambda l, i: (0, jnp.where(l == k, i, 0), 0))

    blk = pl.BlockSpec((None, tr, C), lambda l, i: (l, i, 0))
    return _pcall(
        body, name=name, grid=(L, R // tr),
        in_specs=[recv_spec(k) for k in range(L)] + [blk] * 3,
        out_specs=[blk] * 4,
        out_shape=[jax.ShapeDtypeStruct((L, R, C), F32)] * 4,
        compiler_params=_params(("arbitrary", "arbitrary")),
    )(*recvs, w, m, v)


def _row_block(rows, cols):
    cap = max(SUBLANES, (256 * 1024) // max(cols, 1))
    best = None
    for t in range(SUBLANES, rows + 1, SUBLANES):
        if rows % t == 0 and t <= cap:
            best = t
    return rows if best is None else best


def kernel(x, g_mix_pre, g_mix_post, g_ffn_pre, g_ffn_post, conv_pw1_w, conv_pw1_b, conv_dw_w, conv_dw_b, conv_ln_g, conv_ln_b, conv_pw2_w, conv_pw2_b, attn_w_in, attn_b_f, attn_w_o, mlp_w_up, mlp_w_down, loss_target, m_g_mix_pre, m_g_mix_post, m_g_ffn_pre, m_g_ffn_post, m_conv_pw1_w, m_conv_pw1_b, m_conv_dw_w, m_conv_dw_b, m_conv_ln_g, m_conv_ln_b, m_conv_pw2_w, m_conv_pw2_b, m_attn_w_in, m_attn_b_f, m_attn_w_o, m_mlp_w_up, m_mlp_w_down, v_g_mix_pre, v_g_mix_post, v_g_ffn_pre, v_g_ffn_post, v_conv_pw1_w, v_conv_pw1_b, v_conv_dw_w, v_conv_dw_b, v_conv_ln_g, v_conv_ln_b, v_conv_pw2_w, v_conv_pw2_b, v_attn_w_in, v_attn_b_f, v_attn_w_o, v_mlp_w_up, v_mlp_w_down):
    _, T, D = x.shape
    H = attn_b_f.shape[-1]
    dh = D // H
    width = conv_dw_w.shape[1]
    cin = attn_w_in.shape[-1]
    fs = mlp_w_up.shape[-1]
    G = D // LANES
    hpg = LANES // dh
    assert T % 4 == 0 and D % LANES == 0 and LANES % dh == 0 and width <= CONV_HALO and H <= LANES

    tm = min(512, T // 4)
    tmb = min(256, T // 4)
    tqf = min(1024, T // 4)
    tkb = tm
    tmc = min(256, T // 4)
    lc = min(256, D)
    tkw = min(2048, T // 4)
    tb = min(256, T // 4)

    scale = float(dh) ** -0.5
    mant, _ = math.frexp(scale)
    q_mul = scale if mant == 0.5 else 1.0
    s_mul = 1.0 if mant == 0.5 else scale

    x2 = x.reshape(T, D)
    tgt = loss_target.reshape(T, D)

    w_srcs = [conv_pw1_w, conv_dw_w, conv_pw2_w, mlp_w_up, mlp_w_down, attn_w_in, attn_w_o]
    w_items = [(0, 0, "whole"), (1, 0, "whole"), (2, 0, "whole"), (3, 0, "whole"), (4, 0, "whole"),
               (5, 0, "whole"), (6, 0, "whole"), (3, 1, "whole"), (4, 1, "whole")]
    cin_w = -(-cin // LANES) * LANES
    w_lands = _place_own(
        w_srcs, w_items,
        [((N_DEV, D, cin_w) if si == 5 else (N_DEV,) + w_srcs[si].shape[1:], F32 if si == 1 else BF16)
         for si, _, _ in w_items],
        "stage_weights", cast=True)
    me_arr = _dev_index(*_mesh_pos()).astype(jnp.int32).reshape(1)
    w_groups = [[0], [1, 2], [3, 4], [5, 6], [7, 8]]
    g_sems, _, w_lands, g_token = _push_start(
        [], w_lands, [[(a, None, "own") for a in grp] for grp in w_groups], "gather_start")

    def gather_wait(gi, after):
        grp = w_groups[gi]
        return _push_wait(g_sems[gi], [], [w_lands[a] for a in grp], [(k, None, "own") for k in range(len(grp))],
                          after, "gather_wait%d" % gi)

    (w1g,) = gather_wait(0, g_token)
    bf = jnp.pad(attn_b_f, ((0, 0), (0, LANES - H)))

    row = lambda a, i: a[i:i + 1]

    a0, u0 = _conv_in_fwd(x2, row(g_mix_pre, 0), w1g, conv_pw1_b, tm)
    dwg, w2g = gather_wait(1, u0)
    w2 = w2g.reshape(D, D)
    dw_full = jnp.transpose(dwg, (1, 0, 2)).reshape(width, D)
    w32 = jnp.pad(dw_full, ((0, 32 - width), (0, 0)))
    y0 = _dwconv_fwd(u0, w32, conv_dw_b, tmc, lc, width)
    m0, x_1 = _conv_out_fwd(y0, x2, conv_ln_g, conv_ln_b, w2, conv_pw2_b, row(g_mix_post, 0), tm)
    wu0, wd0 = gather_wait(2, x_1)
    up0, n0, x_2 = _mlp_fwd(x_1, row(g_ffn_pre, 0), wu0, wd0, row(g_ffn_post, 0), tm, "mlp0_fwd")

    wing, wog = gather_wait(3, x_2)
    wo = wog.reshape(D, D)
    win = jnp.transpose(wing[:, :, :cin], (1, 0, 2)).reshape(D, N_DEV * cin)
    wqkv = win[:, :3 * D]
    wf = jnp.pad(win[:, 3 * D:], ((0, 0), (0, LANES - H)))
    q, k, v, lf = _attn_in_fwd(x_2, row(g_mix_pre, 1), wqkv, wf, bf, tm, q_mul, H)
    fcum = _cumsum_rows(lf, jnp.zeros_like(lf), tb, False, "forget_cumsum")


    def key_rows(blk):
        r = jnp.transpose(fcum[:, :H].T.reshape(G, hpg, T // blk, blk), (0, 2, 1, 3))
        return jnp.pad(r, ((0, 0), (0, 0), (0, SUBLANES - hpg), (0, 0)))

    fk_rows = key_rows(tkb)
    o, o32, lse_aux = _flash_fwd(q, k, v, fcum, key_rows(tqf), dh=dh, tq=tqf, s_mul=s_mul)
    m1, x_3 = _attn_out_fwd(o, x_2, wo, row(g_mix_post, 1), tm)
    wu1, wd1 = gather_wait(4, x_3)
    up1, n1, x_4 = _mlp_fwd(x_3, row(g_ffn_pre, 1), wu1, wd1, row(g_ffn_post, 1), tm, "mlp1_fwd")

    def mlp_back(dx, n_l, x_in, up_l, l, wu, wd, dep, from_loss=False):
        dxi, h, dm, dup, sums = _mlp_bwd(dx, n_l, x_in, up_l, row(g_ffn_pre, l), wu, wd, row(g_ffn_post, l),
                                         tmb, "mlp%d_bwd" % l, dep, from_loss)
        dwu = _mm_tn(h, dup, nj=N_DEV, a_cols=D, g_cols=fs, a_by_j=False, g_by_j=True, tk=tkw,
                     name="mlp%d_dwu" % l)
        dwd = _mm_tn(up_l, dm, nj=N_DEV, a_cols=fs, g_cols=D, a_by_j=True, g_by_j=False, tk=tkw,
                     name="mlp%d_dwd" % l, act=True)
        return dxi, dwu, dwd, sums

    def push_grads(srcs, modes, name):
        items = [(i, None, mode) for i, mode in enumerate(modes)]
        lands = _seed_lands(srcs, modes, me_arr, name + "_own")
        sems, srcs_t, lands_t, token = _push_start(srcs, lands, [items], name + "_start")
        return (sems[0], srcs_t, lands_t, items), token

    def pull_grads(handle, after, name):
        sems, srcs_t, lands_t, items = handle
        return _push_wait(sems, srcs_t, lands_t, items, after, name + "_wait")

    rs = D // N_DEV
    dx, dwu1, dwd1, s_mlp1 = mlp_back(x_4, n1, x_3, up1, 1, wu1, wd1, tgt, from_loss=True)
    h_mlp1, tok = push_grads([dwu1, dwd1], ["slot", "slot"], "grads_mlp1")

    head_ind = jnp.asarray((np.arange(D)[:, None] // dh == np.arange(LANES)[None, :]).astype(np.float32))
    dm1, do, delta, s_ao = _attn_out_bwd(dx, m1, o32, wo, row(g_mix_post, 1), head_ind, tm, tok)
    dwo = _mm_tn(o, dm1, nj=1, a_cols=D, g_cols=D, a_by_j=False, g_by_j=False, tk=tkw, name="attn_dwo")
    dq, dk, dv, dfq, dfk = _flash_bwd(q, k, v, do, fcum, lse_aux, delta, fk_rows,
                                      dh=dh, tq=tkb, tk=tkb, s_mul=s_mul, dq_mul=scale)
    def head_cols(r):
        return jnp.pad(jnp.transpose(r[:, :, :hpg, :], (0, 2, 1, 3)).reshape(H, T).T, ((0, 0), (0, LANES - H)))

    df_k = head_cols(dfk)
    df_q = head_cols(dfq)
    dlf = _cumsum_rows(df_q, df_k, tb, True, "forget_cumsum_bwd")
    dx, h_at, df, s_ai, dbf = _attn_in_bwd(dx, x_2, row(g_mix_pre, 1), dq, dk, dv, dlf, lf, wqkv, wf, tm, H)
    dwq = _mm_tn(h_at, dq, nj=1, a_cols=D, g_cols=D, a_by_j=False, g_by_j=False, tk=tkw, name="attn_dwq")
    dwk = _mm_tn(h_at, dk, nj=1, a_cols=D, g_cols=D, a_by_j=False, g_by_j=False, tk=tkw, name="attn_dwk")
    dwv = _mm_tn(h_at, dv, nj=1, a_cols=D, g_cols=D, a_by_j=False, g_by_j=False, tk=tkw, name="attn_dwv")
    dwf = _mm_tn(h_at, df, nj=1, a_cols=D, g_cols=LANES, a_by_j=False, g_by_j=False, tk=tkw, name="attn_dwf")
    dwin = jnp.concatenate([dwq[0], dwk[0], dwv[0], dwf[0][:, :H]], axis=1)
    dwin = jnp.pad(jnp.transpose(dwin.reshape(D, N_DEV, cin), (1, 0, 2)), ((0, 0), (0, 0), (0, cin_w - cin)))
    h_attn, tok = push_grads([dwin, dwo.reshape(N_DEV, rs, D)], ["slot", "slot"], "grads_attn")

    dx, dwu0, dwd0, s_mlp0 = mlp_back(dx, n0, x_1, up0, 0, wu0, wd0, tok)
    h_mlp0, tok = push_grads([dwu0, dwd0], ["slot", "slot"], "grads_mlp0")

    dy0, dm0, z0, s_co = _conv_out_bwd(dx, m0, y0, conv_ln_g, conv_ln_b, w2, row(g_mix_post, 0), tm, tok)
    dw2 = _mm_tn(z0, dm0, nj=1, a_cols=D, g_cols=D, a_by_j=False, g_by_j=False, tk=tkw, name="conv_dw2")
    du0, ddw = _dwconv_bwd(dy0, u0, w32, tmc, lc, width)
    grad_x, h_cv, da0, s_ci, db1 = _conv_in_bwd(dx, du0, a0, x2, row(g_mix_pre, 0), w1g, tm)
    dw1 = _mm_tn(h_cv, da0, nj=N_DEV, a_cols=D, g_cols=(2 * D) // N_DEV, a_by_j=False, g_by_j=True, tk=tkw,
                 name="conv_dw1")
    ddw_s = jnp.transpose(ddw[:width].reshape(width, N_DEV, D // N_DEV), (1, 0, 2))

    def pad_row(a):
        return jnp.pad(a, ((0, 0), (0, D - a.shape[1])))

    def pack(gmp, gmq, gfp, gfq, b1, dwb, lng, lnb, b2, bfv, last):
        return jnp.concatenate([gmp, gmq, gfp, gfq, b1.reshape(2, D), dwb, lng, lnb, b2, pad_row(bfv), last],
                               axis=0)

    zero_row = jnp.zeros((1, D), F32)
    small_g = pack(
        jnp.concatenate([row(s_ci, 0), row(s_ai, 0)], axis=0),
        jnp.concatenate([row(s_co, 0), row(s_ao, 0)], axis=0),
        jnp.concatenate([row(s_mlp0, 1), row(s_mlp1, 1)], axis=0),
        jnp.concatenate([row(s_mlp0, 0), row(s_mlp1, 0)], axis=0),
        db1, row(s_co, 4), row(s_co, 1), row(s_co, 2), row(s_co, 3), dbf[:, :H],
        pad_row(row(s_mlp1, 2)[:, 0:1]))
    h_conv, tok = push_grads([dw1, ddw_s, dw2.reshape(N_DEV, rs, D), small_g], ["slot", "slot", "slot", "whole"],
                             "grads_conv")

    def opt(recvs, w, m, v, name):
        shp = w.shape
        L, C = shp[0], shp[-1]
        R = int(np.prod(shp[1:-1]))
        outs = _adamw([r.reshape(N_DEV, R, r.shape[-1]) for r in recvs], w.reshape(L, R, C), m.reshape(L, R, C),
                      v.reshape(L, R, C), _row_block(R, C), name)
        return [t.reshape(shp) for t in outs]

    big = {}
    r_wu1, r_wd1 = pull_grads(h_mlp1, tok, "grads_mlp1")
    r_win, r_wo = pull_grads(h_attn, r_wd1, "grads_attn")
    big["attn_w_in"] = opt([r_win], attn_w_in, m_attn_w_in, v_attn_w_in, "adamw_win")
    big["attn_w_o"] = opt([r_wo], attn_w_o, m_attn_w_o, v_attn_w_o, "adamw_wo")
    r_wu0, r_wd0 = pull_grads(h_mlp0, big["attn_w_o"][0], "grads_mlp0")
    big["mlp_w_up"] = opt([r_wu0, r_wu1], mlp_w_up, m_mlp_w_up, v_mlp_w_up, "adamw_wup")
    big["mlp_w_down"] = opt([r_wd0, r_wd1], mlp_w_down, m_mlp_w_down, v_mlp_w_down, "adamw_wdown")
    r_w1, r_dw, r_w2, r_small = pull_grads(h_conv, big["mlp_w_down"][0], "grads_conv")
    big["conv_pw1_w"] = opt([r_w1], conv_pw1_w, m_conv_pw1_w, v_conv_pw1_w, "adamw_pw1")
    big["conv_dw_w"] = opt([r_dw], conv_dw_w, m_conv_dw_w, v_conv_dw_w, "adamw_dw")
    big["conv_pw2_w"] = opt([r_w2], conv_pw2_w, m_conv_pw2_w, v_conv_pw2_w, "adamw_pw2")
    small_w = pack(g_mix_pre, g_mix_post, g_ffn_pre, g_ffn_post, conv_pw1_b, conv_dw_b, conv_ln_g, conv_ln_b,
                   conv_pw2_b, attn_b_f, zero_row)
    small_m = pack(m_g_mix_pre, m_g_mix_post, m_g_ffn_pre, m_g_ffn_post, m_conv_pw1_b, m_conv_dw_b, m_conv_ln_g,
                   m_conv_ln_b, m_conv_pw2_b, m_attn_b_f, zero_row)
    small_v = pack(v_g_mix_pre, v_g_mix_post, v_g_ffn_pre, v_g_ffn_post, v_conv_pw1_b, v_conv_dw_b, v_conv_ln_g,
                   v_conv_ln_b, v_conv_pw2_b, v_attn_b_f, zero_row)
    sm = _adamw([r_small], small_w[None], small_m[None], small_v[None], small_w.shape[0], "adamw_small")
    sm = [t[0] for t in sm]
    loss = sm[0][15, 0]

    def unpack(t):
        return {"g_mix_pre": t[0:2], "g_mix_post": t[2:4], "g_ffn_pre": t[4:6], "g_ffn_post": t[6:8],
                "conv_pw1_b": t[8:10].reshape(1, 2 * D), "conv_dw_b": t[10:11], "conv_ln_g": t[11:12],
                "conv_ln_b": t[12:13], "conv_pw2_b": t[13:14], "attn_b_f": t[14:15, :H]}

    small = [unpack(t) for t in sm]
    names = ["g_mix_pre", "g_mix_post", "g_ffn_pre", "g_ffn_post", "conv_pw1_w", "conv_pw1_b", "conv_dw_w",
             "conv_dw_b", "conv_ln_g", "conv_ln_b", "conv_pw2_w", "conv_pw2_b", "attn_w_in", "attn_b_f",
             "attn_w_o", "mlp_w_up", "mlp_w_down"]
    outs = [loss, grad_x.reshape(1, T, D)]
    for kind in range(4):
        for nme in names:
            outs.append(big[nme][kind] if nme in big else small[kind][nme])
    return tuple(outs)
```

```python
import functools
import math

import numpy as np
import jax
import jax.numpy as jnp
from jax import lax
from jax.experimental import pallas as pl
from jax.experimental.pallas import tpu as pltpu

F32 = jnp.float32
BF16 = jnp.bfloat16

RMS_EPS = 1e-6
LN_EPS = 1e-5
MASK_VALUE = -1e30
ADAM_LR = 0.001
ADAM_B1 = 0.9
ADAM_B2 = 0.999
ADAM_EPS = 1e-08
ADAM_WD = 0.01
ADAM_STEP = 10

N_DEV = 8
LANES = 128
SUBLANES = 8
CONV_HALO = 32
CONV_ROWS = 32
FLASH_ROWS = 32
VMEM_LIMIT = 56 * 1024 * 1024

_pcall = pl.pallas_call


def _params(sem=None):
    if sem is None:
        return pltpu.CompilerParams(vmem_limit_bytes=VMEM_LIMIT)
    return pltpu.CompilerParams(dimension_semantics=sem, vmem_limit_bytes=VMEM_LIMIT)


def _dot(a, b):
    return jnp.dot(a, b, preferred_element_type=F32)


def _dot_nt(a, b):
    return lax.dot_general(a, b, (((1,), (1,)), ((), ())), preferred_element_type=F32)


def _dot_tn(a, b):
    return lax.dot_general(a, b, (((0,), (0,)), ((), ())), preferred_element_type=F32)


def _full(shape):
    nd = len(shape)
    return pl.BlockSpec(shape, lambda *g: (0,) * nd, pipeline_mode=pl.Buffered(1))


def _acc(shape):
    nd = len(shape)
    return pl.BlockSpec(shape, lambda *g: (0,) * nd)


def _rows(tm, cols):
    return pl.BlockSpec((tm, cols), lambda i: (i, 0))


def _rms(x, g):
    r = lax.rsqrt(jnp.mean(x * x, axis=-1, keepdims=True) + RMS_EPS)
    return x * r * g


def _rms_bwd(x, g, dy):
    r = lax.rsqrt(jnp.mean(x * x, axis=-1, keepdims=True) + RMS_EPS)
    n = x * r
    dg = jnp.sum(dy * n, axis=0, keepdims=True)
    dn = dy * g
    dx = r * (dn - n * jnp.mean(dn * n, axis=-1, keepdims=True))
    return dx, dg


def _sigmoid(x):
    return 1.0 / (1.0 + jnp.exp(-x))


def _mesh_pos():
    return lax.axis_index("x"), lax.axis_index("y"), lax.axis_index("c")


def _dev_index(px, py, pc):
    return 4 * px + 2 * py + pc


_HBM = pl.BlockSpec(memory_space=pltpu.HBM)
_SEM = pl.BlockSpec(memory_space=pltpu.SEMAPHORE)
_EFFECT = pltpu.SideEffectType.DATAFLOW_SIDE_EFFECTING


def _peer(r, x, y, c):
    p = ((1 - x) if r & 4 else x, (1 - y) if r & 2 else y, (1 - c) if r & 1 else c)
    return p, _dev_index(*p)


def _src_ref(refs, item, me_id=None, to_id=None):
    si, sub, mode = item
    r = refs[si] if sub is None else refs[si].at[sub]
    if mode == "slot":
        return r.at[to_id]
    if mode == "own":
        return r.at[me_id]
    return r


def _place_own(srcs, items, land_shapes, name, cast=False):
    ns, n = len(srcs), len(items)

    def body(*refs):
        src = refs[:ns]
        land = refs[ns:ns + n]
        stage = refs[ns + n:ns + 2 * n] if cast else None
        sems = refs[-1]
        me_id = _dev_index(*_mesh_pos())
        cps = []
        for a, item in enumerate(items):
            s = _src_ref(src, item, to_id=me_id)
            if cast:
                if s.shape != stage[a].shape:
                    stage[a][...] = jnp.zeros_like(stage[a])
                    stage[a][:, 0:s.shape[-1]] = s[...].astype(stage[a].dtype)
                else:
                    stage[a][...] = s[...].astype(stage[a].dtype)
                s = stage[a]
            cp = pltpu.make_async_copy(s, land[a].at[me_id], sems.at[a])
            cp.start()
            cps.append(cp)
        for cp in cps:
            cp.wait()

    return _pcall(
        body, name=name,
        out_shape=[jax.ShapeDtypeStruct(tuple(s), d) for s, d in land_shapes],
        in_specs=[pl.BlockSpec(memory_space=pltpu.VMEM if cast else pl.ANY)] * ns,
        out_specs=[pl.BlockSpec(memory_space=pl.ANY)] * n,
        scratch_shapes=([pltpu.VMEM(tuple(s[1:]), d) for s, d in land_shapes] if cast else [])
        + [pltpu.SemaphoreType.DMA((n,))],
        compiler_params=pltpu.CompilerParams(vmem_limit_bytes=VMEM_LIMIT),
    )(*srcs)


def _seed_lands(srcs, modes, me, name):
    n = len(srcs)
    parts = [tuple(s.shape[1:]) if mode == "slot" else tuple(s.shape) for s, mode in zip(srcs, modes)]

    def body(me_ref, *refs):
        s = pl.program_id(0)
        for a in range(n):
            v = refs[a][...]
            refs[n + a][...] = jnp.where(s == me_ref[0], v, jnp.zeros_like(v))

    def in_spec(part, mode):
        nd = len(part)
        if mode == "slot":
            return pl.BlockSpec((None,) + part, lambda s, me_ref: (me_ref[0],) + (0,) * nd)
        return pl.BlockSpec(part, lambda s, me_ref: (0,) * nd)

    def out_spec(part):
        nd = len(part)
        return pl.BlockSpec((None,) + part, lambda s, me_ref: (s,) + (0,) * nd)

    return _pcall(
        body, name=name,
        grid_spec=pltpu.PrefetchScalarGridSpec(
            num_scalar_prefetch=1, grid=(N_DEV,),
            in_specs=[in_spec(p, m) for p, m in zip(parts, modes)],
            out_specs=[out_spec(p) for p in parts]),
        out_shape=[jax.ShapeDtypeStruct((N_DEV,) + p, s.dtype) for p, s in zip(parts, srcs)],
        compiler_params=_params(("arbitrary",)),
    )(me, *srcs)


def _push_start(srcs, lands, groups, name):
    ns, n = len(srcs), len(lands)
    ng = len(groups)
    assert sum(len(g) for g in groups) == n

    def body(*refs):
        ops = refs[:ns + n]
        land = refs[ns:ns + n]
        sems = refs[ns + n:ns + n + 2 * ng]
        token = refs[-1]
        x, y, c = _mesh_pos()
        me_id = _dev_index(x, y, c)
        a = 0
        for gi, grp in enumerate(groups):
            for k, item in enumerate(grp):
                for r in range(1, N_DEV):
                    p, pid = _peer(r, x, y, c)
                    pltpu.make_async_remote_copy(
                        src_ref=_src_ref(ops, item, me_id=me_id, to_id=pid), dst_ref=land[a].at[me_id],
                        send_sem=sems[2 * gi].at[k * (N_DEV - 1) + r - 1],
                        recv_sem=sems[2 * gi + 1].at[k * (N_DEV - 1) + r - 1],
                        device_id=p, device_id_type=pl.DeviceIdType.MESH).start()
                a += 1
        token[...] = jnp.zeros_like(token)

    sem_shapes = []
    for grp in groups:
        sem_shapes += [pltpu.SemaphoreType.DMA((len(grp) * (N_DEV - 1),))] * 2
    arrs = list(srcs) + list(lands)
    res = _pcall(
        body, name=name,
        out_shape=tuple(sem_shapes) + tuple(pltpu.HBM(a.shape, a.dtype) for a in arrs)
        + (jax.ShapeDtypeStruct((SUBLANES, LANES), F32),),
        in_specs=[_HBM] * (ns + n),
        out_specs=tuple([_SEM] * (2 * ng)) + tuple([_HBM] * (ns + n)) + (pl.BlockSpec(memory_space=pltpu.VMEM),),
        input_output_aliases={i: 2 * ng + i for i in range(ns + n)},
        compiler_params=pltpu.CompilerParams(has_side_effects=_EFFECT),
    )(*[pltpu.with_memory_space_constraint(a, pltpu.HBM) for a in arrs])
    sems = [(res[2 * gi], res[2 * gi + 1]) for gi in range(ng)]
    thru = res[2 * ng:2 * ng + ns + n]
    return sems, list(thru[:ns]), list(thru[ns:]), res[-1]


def _push_wait(sems, srcs, lands, group, after, name):
    ns, n = len(srcs), len(lands)
    assert len(group) == n

    def body(*refs):
        ops = refs[:ns + n]
        land = refs[ns:ns + n]
        send_sems, recv_sems = refs[ns + n], refs[ns + n + 1]
        x, y, c = _mesh_pos()
        me_id = _dev_index(x, y, c)
        for k, item in enumerate(group):
            for r in range(1, N_DEV):
                p, pid = _peer(r, x, y, c)
                cp = pltpu.make_async_remote_copy(
                    src_ref=_src_ref(ops, item, me_id=me_id, to_id=pid), dst_ref=land[k].at[pid],
                    send_sem=send_sems.at[k * (N_DEV - 1) + r - 1], recv_sem=recv_sems.at[k * (N_DEV - 1) + r - 1],
                    device_id=p, device_id_type=pl.DeviceIdType.MESH)
                cp.wait_send()
                cp.wait_recv()

    arrs = list(srcs) + list(lands)
    res = _pcall(
        body, name=name,
        out_shape=tuple(pltpu.HBM(a.shape, a.dtype) for a in arrs),
        in_specs=[_HBM] * (ns + n) + [_SEM, _SEM, pl.BlockSpec(memory_space=pl.ANY)],
        out_specs=tuple([_HBM] * (ns + n)),
        input_output_aliases={i: i for i in range(ns + n)},
        compiler_params=pltpu.CompilerParams(has_side_effects=_EFFECT),
    )(*arrs, sems[0], sems[1], after)
    return list(res[ns:])


def _conv_in_fwd(x, g_pre, w1g, b1, tm):
    T, D = x.shape
    ns, _, cs = w1g.shape
    half = ns // 2

    def body(x_ref, g_ref, w_ref, b_ref, a_ref, u_ref):
        h = _rms(x_ref[...], g_ref[...]).astype(BF16)
        parts = []
        for s in range(ns):
            a_s = _dot(h, w_ref[s]) + b_ref[:, s * cs:(s + 1) * cs]
            a_ref[:, s * cs:(s + 1) * cs] = a_s
            parts.append(a_s)
        for s in range(half):
            u_ref[:, s * cs:(s + 1) * cs] = parts[s] * _sigmoid(parts[s + half])

    return _pcall(
        body, name="conv_in_fwd", grid=(T // tm,),
        in_specs=[_rows(tm, D), _full((1, D)), _full(w1g.shape), _full((1, 2 * D))],
        out_specs=[_rows(tm, 2 * D), _rows(tm, D)],
        out_shape=[jax.ShapeDtypeStruct((T, 2 * D), F32), jax.ShapeDtypeStruct((T, D), F32)],
        compiler_params=_params(("arbitrary",)),
    )(x, g_pre, w1g, b1)


def _shifted_copies(ext_ref, sh_ref, tm):
    n = tm + CONV_HALO - SUBLANES
    for b in range(1, SUBLANES):
        sh_ref[b - 1, 0:n, :] = ext_ref[b:b + n, :]


def _shifted_rows(ext_ref, sh_ref, off, r0, ls):
    b = off % SUBLANES
    a8 = off - b + r0
    src = ext_ref if b == 0 else sh_ref.at[b - 1]
    return src[a8:a8 + CONV_ROWS, ls]


def _dwconv_fwd(u, w32, b, tm, lc, width):
    T, D = u.shape
    hb = tm // CONV_HALO

    def body(u_ref, halo_ref, w_ref, b_ref, y_ref, ext_ref, sh_ref):
        i = pl.program_id(0)
        ext_ref[0:CONV_HALO, :] = jnp.where(i > 0, halo_ref[...], 0.0)
        ext_ref[CONV_HALO:, :] = u_ref[...]
        _shifted_copies(ext_ref, sh_ref, tm)
        for r0 in range(0, tm, CONV_ROWS):
            for l0 in range(0, lc, LANES):
                ls = slice(l0, l0 + LANES)
                acc = jnp.zeros((CONV_ROWS, LANES), F32) + b_ref[:, ls]
                for j in range(width):
                    off = CONV_HALO - (width - 1) + j
                    acc = acc + w_ref[j:j + 1, ls] * _shifted_rows(ext_ref, sh_ref, off, r0, ls)
                y_ref[r0:r0 + CONV_ROWS, ls] = acc

    return _pcall(
        body, name="dwconv_fwd", grid=(T // tm, D // lc),
        in_specs=[pl.BlockSpec((tm, lc), lambda i, l: (i, l)),
                  pl.BlockSpec((CONV_HALO, lc), lambda i, l: (jnp.maximum(i * hb - 1, 0), l)),
                  pl.BlockSpec((32, lc), lambda i, l: (0, l)),
                  pl.BlockSpec((1, lc), lambda i, l: (0, l))],
        out_specs=pl.BlockSpec((tm, lc), lambda i, l: (i, l)),
        out_shape=jax.ShapeDtypeStruct((T, D), F32),
        scratch_shapes=[pltpu.VMEM((tm + CONV_HALO, lc), F32),
                        pltpu.VMEM((SUBLANES - 1, tm + CONV_HALO, lc), F32)],
        compiler_params=_params(("arbitrary", "arbitrary")),
    )(u, u, w32, b)


def _ln_parts(y, g, b):
    mu = jnp.mean(y, axis=-1, keepdims=True)
    yc = y - mu
    rstd = lax.rsqrt(jnp.mean(yc * yc, axis=-1, keepdims=True) + LN_EPS)
    yhat = yc * rstd
    return yhat, rstd, yhat * g + b


def _conv_out_fwd(y, x, ln_g, ln_b, w2, b2, g_post, tm):
    T, D = x.shape

    def body(y_ref, x_ref, lg_ref, lb_ref, w_ref, b_ref, g_ref, m_ref, xo_ref):
        _, _, yn = _ln_parts(y_ref[...], lg_ref[...], lb_ref[...])
        z = (yn * _sigmoid(yn)).astype(BF16)
        m = _dot(z, w_ref[...]) + b_ref[...]
        m_ref[...] = m
        xo_ref[...] = x_ref[...] + _rms(m, g_ref[...])

    return _pcall(
        body, name="conv_out_fwd", grid=(T // tm,),
        in_specs=[_rows(tm, D), _rows(tm, D), _full((1, D)), _full((1, D)), _full((D, D)), _full((1, D)),
                  _full((1, D))],
        out_specs=[_rows(tm, D), _rows(tm, D)],
        out_shape=[jax.ShapeDtypeStruct((T, D), F32), jax.ShapeDtypeStruct((T, D), F32)],
        compiler_params=_params(("arbitrary",)),
    )(y, x, ln_g, ln_b, w2, b2, g_post)


def _conv_out_bwd(dxo, m, y, ln_g, ln_b, w2, g_post, tm, dep):
    T, D = m.shape

    def body(dxo_ref, m_ref, y_ref, lg_ref, lb_ref, w_ref, g_ref, dep_ref, dy_ref, dm_ref, z_ref, sums_ref):
        i = pl.program_id(0)
        dm, dgpost = _rms_bwd(m_ref[...], g_ref[...], dxo_ref[...])
        dmb = dm.astype(BF16)
        dm_ref[...] = dmb
        yhat, rstd, yn = _ln_parts(y_ref[...], lg_ref[...], lb_ref[...])
        sg = _sigmoid(yn)
        z_ref[...] = (yn * sg).astype(BF16)
        dz = _dot_nt(dmb, w_ref[...])
        dyn = dz * (sg + yn * sg * (1.0 - sg))
        dyh = dyn * lg_ref[...]
        dy = rstd * (dyh - jnp.mean(dyh, axis=-1, keepdims=True)
                     - yhat * jnp.mean(dyh * yhat, axis=-1, keepdims=True))
        dy_ref[...] = dy

        @pl.when(i == 0)
        def _():
            sums_ref[...] = jnp.zeros_like(sums_ref)
        sums_ref[0:1, :] += dgpost
        sums_ref[1:2, :] += jnp.sum(dyn * yhat, axis=0, keepdims=True)
        sums_ref[2:3, :] += jnp.sum(dyn, axis=0, keepdims=True)
        sums_ref[3:4, :] += jnp.sum(dm, axis=0, keepdims=True)
        sums_ref[4:5, :] += jnp.sum(dy, axis=0, keepdims=True)

    return _pcall(
        body, name="conv_out_bwd", grid=(T // tm,),
        in_specs=[_rows(tm, D), _rows(tm, D), _rows(tm, D), _full((1, D)), _full((1, D)), _full((D, D)),
                  _full((1, D)), pl.BlockSpec(memory_space=pl.ANY)],
        out_specs=[_rows(tm, D), _rows(tm, D), _rows(tm, D), _acc((SUBLANES, D))],
        out_shape=[jax.ShapeDtypeStruct((T, D), F32), jax.ShapeDtypeStruct((T, D), BF16),
                   jax.ShapeDtypeStruct((T, D), BF16), jax.ShapeDtypeStruct((SUBLANES, D), F32)],
        compiler_params=_params(("arbitrary",)),
    )(dxo, m, y, ln_g, ln_b, w2, g_post, dep)


def _dwconv_bwd(dy, u, w32, tm, lc, width):
    T, D = u.shape
    hb = tm // CONV_HALO
    nt = T // tm
    last_halo = T // CONV_HALO - 1

    def body(dy_ref, dyn_ref, u_ref, up_ref, w_ref, du_ref, dw_ref, exty_ref, extu_ref, acc_ref, shy_ref, shu_ref):
        i = pl.program_id(1)
        exty_ref[0:tm, :] = dy_ref[...]
        exty_ref[tm:, :] = jnp.where(i < nt - 1, dyn_ref[...], 0.0)
        extu_ref[0:CONV_HALO, :] = jnp.where(i > 0, up_ref[...], 0.0)
        extu_ref[CONV_HALO:, :] = u_ref[...]
        _shifted_copies(exty_ref, shy_ref, tm)
        _shifted_copies(extu_ref, shu_ref, tm)

        @pl.when(i == 0)
        def _():
            acc_ref[...] = jnp.zeros_like(acc_ref)

        for r0 in range(0, tm, CONV_ROWS):
            for l0 in range(0, lc, LANES):
                ls = slice(l0, l0 + LANES)
                dyc = exty_ref[r0:r0 + CONV_ROWS, ls]
                du = jnp.zeros((CONV_ROWS, LANES), F32)
                for j in range(width):
                    du = du + w_ref[j:j + 1, ls] * _shifted_rows(exty_ref, shy_ref, (width - 1) - j, r0, ls)
                    prod = dyc * _shifted_rows(extu_ref, shu_ref, CONV_HALO - (width - 1) + j, r0, ls)
                    acc_ref[j, :, ls] += prod.reshape(CONV_ROWS // SUBLANES, SUBLANES, LANES).sum(axis=0)
                du_ref[r0:r0 + CONV_ROWS, ls] = du

        @pl.when(i == nt - 1)
        def _():
            for j in range(32):
                dw_ref[j:j + 1, :] = jnp.sum(acc_ref[j], axis=0, keepdims=True)

    return _pcall(
        body, name="dwconv_bwd", grid=(D // lc, nt),
        in_specs=[pl.BlockSpec((tm, lc), lambda l, i: (i, l)),
                  pl.BlockSpec((CONV_HALO, lc), lambda l, i: (jnp.minimum((i + 1) * hb, last_halo), l)),
                  pl.BlockSpec((tm, lc), lambda l, i: (i, l)),
                  pl.BlockSpec((CONV_HALO, lc), lambda l, i: (jnp.maximum(i * hb - 1, 0), l)),
                  pl.BlockSpec((32, lc), lambda l, i: (0, l))],
        out_specs=[pl.BlockSpec((tm, lc), lambda l, i: (i, l)),
                   pl.BlockSpec((32, lc), lambda l, i: (0, l))],
        out_shape=[jax.ShapeDtypeStruct((T, D), F32), jax.ShapeDtypeStruct((32, D), F32)],
        scratch_shapes=[pltpu.VMEM((tm + CONV_HALO, lc), F32), pltpu.VMEM((tm + CONV_HALO, lc), F32),
                        pltpu.VMEM((32, SUBLANES, lc), F32),
                        pltpu.VMEM((SUBLANES - 1, tm + CONV_HALO, lc), F32),
                        pltpu.VMEM((SUBLANES - 1, tm + CONV_HALO, lc), F32)],
        compiler_params=_params(("arbitrary", "arbitrary")),
    )(dy, dy, u, u, w32)


def _conv_in_bwd(dxo, du, a, x, g_pre, w1g, tm):
    T, D = x.shape
    ns, _, cs = w1g.shape
    half = ns // 2

    def body(dxo_ref, du_ref, a_ref, x_ref, g_ref, w_ref, dxi_ref, h_ref, da_ref, sums_ref, db_ref):
        i = pl.program_id(0)
        xv = x_ref[...]
        h_ref[...] = _rms(xv, g_ref[...]).astype(BF16)
        dh = jnp.zeros((tm, D), F32)
        dbs = [None] * ns
        for s in range(half):
            a_u = a_ref[:, s * cs:(s + 1) * cs]
            sg = _sigmoid(a_ref[:, (s + half) * cs:(s + half + 1) * cs])
            du_s = du_ref[:, s * cs:(s + 1) * cs]
            da_u = du_s * sg
            da_g = du_s * a_u * sg * (1.0 - sg)
            for s2, v in ((s, da_u), (s + half, da_g)):
                vb = v.astype(BF16)
                da_ref[:, s2 * cs:(s2 + 1) * cs] = vb
                dbs[s2] = jnp.sum(v, axis=0, keepdims=True)
                dh = dh + _dot_nt(vb, w_ref[s2])
        dxi, dgpre = _rms_bwd(xv, g_ref[...], dh)
        dxi_ref[...] = dxo_ref[...] + dxi

        @pl.when(i == 0)
        def _():
            sums_ref[...] = jnp.zeros_like(sums_ref)
            db_ref[...] = jnp.zeros_like(db_ref)
        sums_ref[0:1, :] += dgpre
        for s in range(ns):
            db_ref[:, s * cs:(s + 1) * cs] += dbs[s]

    return _pcall(
        body, name="conv_in_bwd", grid=(T // tm,),
        in_specs=[_rows(tm, D), _rows(tm, D), _rows(tm, 2 * D), _rows(tm, D), _full((1, D)),
                  _full(w1g.shape)],
        out_specs=[_rows(tm, D), _rows(tm, D), _rows(tm, 2 * D), _acc((SUBLANES, D)), _acc((1, 2 * D))],
        out_shape=[jax.ShapeDtypeStruct((T, D), F32), jax.ShapeDtypeStruct((T, D), BF16),
                   jax.ShapeDtypeStruct((T, 2 * D), BF16), jax.ShapeDtypeStruct((SUBLANES, D), F32),
                   jax.ShapeDtypeStruct((1, 2 * D), F32)],
        compiler_params=_params(("arbitrary",)),
    )(dxo, du, a, x, g_pre, w1g)


def _mlp_fwd(x, g_pre, wug, wdg, g_post, tm, name):
    T, D = x.shape
    ns, _, fs = wug.shape

    def body(x_ref, gp_ref, wu_ref, wd_ref, gq_ref, up_ref, m_ref, xo_ref):
        xv = x_ref[...]
        h = _rms(xv, gp_ref[...]).astype(BF16)
        acc = jnp.zeros((tm, D), F32)
        for s in range(ns):
            up = _dot(h, wu_ref[s]).astype(BF16)
            up_ref[:, s * fs:(s + 1) * fs] = up
            act = jnp.square(jnp.maximum(up.astype(F32), 0.0)).astype(BF16)
            acc = acc + _dot(act, wd_ref[s])
        m_ref[...] = acc
        xo_ref[...] = xv + _rms(acc, gq_ref[...])

    return _pcall(
        body, name=name, grid=(T // tm,),
        in_specs=[_rows(tm, D), _full((1, D)), _full(wug.shape), _full(wdg.shape), _full((1, D))],
        out_specs=[_rows(tm, ns * fs), _rows(tm, D), _rows(tm, D)],
        out_shape=[jax.ShapeDtypeStruct((T, ns * fs), BF16), jax.ShapeDtypeStruct((T, D), F32),
                   jax.ShapeDtypeStruct((T, D), F32)],
        compiler_params=_params(("arbitrary",)),
    )(x, g_pre, wug, wdg, g_post)


def _mlp_bwd(dxo, m, x, up, g_pre, wug, wdg, g_post, tm, name, dep, from_loss=False):
    T, D = x.shape
    ns, _, fs = wug.shape

    def body(dxo_ref, m_ref, x_ref, up_ref, gp_ref, wu_ref, wd_ref, gq_ref, dep_ref,
             dxi_ref, h_ref, dm_ref, dup_ref, sums_ref):
        i = pl.program_id(0)
        if from_loss:
            err = dxo_ref[...] - dep_ref[...]
            dxo = err * (1.0 / D)
        else:
            dxo = dxo_ref[...]
        dm, dgpost = _rms_bwd(m_ref[...], gq_ref[...], dxo)
        dmb = dm.astype(BF16)
        dm_ref[...] = dmb
        xv = x_ref[...]
        h_ref[...] = _rms(xv, gp_ref[...]).astype(BF16)
        dh = jnp.zeros((tm, D), F32)
        for s in range(ns):
            dact = _dot_nt(dmb, wd_ref[s])
            up = up_ref[:, s * fs:(s + 1) * fs].astype(F32)
            dup = (dact * (2.0 * jnp.maximum(up, 0.0))).astype(BF16)
            dup_ref[:, s * fs:(s + 1) * fs] = dup
            dh = dh + _dot_nt(dup, wu_ref[s])
        dxi, dgpre = _rms_bwd(xv, gp_ref[...], dh)
        dxi_ref[...] = dxo + dxi

        @pl.when(i == 0)
        def _():
            sums_ref[...] = jnp.zeros_like(sums_ref)
        sums_ref[0:1, :] += dgpost
        sums_ref[1:2, :] += dgpre
        if from_loss:
            sums_ref[2:3, :] += 0.5 * jnp.sum(jnp.mean(err * err, axis=-1, keepdims=True), axis=0, keepdims=True)

    return _pcall(
        body, name=name, grid=(T // tm,),
        in_specs=[_rows(tm, D), _rows(tm, D), _rows(tm, D), _rows(tm, ns * fs), _full((1, D)),
                  _full(wug.shape), _full(wdg.shape), _full((1, D)),
                  _rows(tm, D) if from_loss else pl.BlockSpec(memory_space=pl.ANY)],
        out_specs=[_rows(tm, D), _rows(tm, D), _rows(tm, D), _rows(tm, ns * fs), _acc((SUBLANES, D))],
        out_shape=[jax.ShapeDtypeStruct((T, D), F32), jax.ShapeDtypeStruct((T, D), BF16),
                   jax.ShapeDtypeStruct((T, D), BF16), jax.ShapeDtypeStruct((T, ns * fs), BF16),
                   jax.ShapeDtypeStruct((SUBLANES, D), F32)],
        compiler_params=_params(("arbitrary",)),
    )(dxo, m, x, up, g_pre, wug, wdg, g_post, dep)


def _mm_tn(a, g, *, nj, a_cols, g_cols, a_by_j, g_by_j, tk, name, act=False):
    T = a.shape[0]
    nk = T // tk

    def body(a_ref, g_ref, o_ref, acc_ref):
        k = pl.program_id(1)
        av = a_ref[...]
        if act:
            av = jnp.square(jnp.maximum(av.astype(F32), 0.0)).astype(BF16)
        p = _dot_tn(av, g_ref[...])

        @pl.when(k == 0)
        def _():
            acc_ref[...] = p

        @pl.when(k > 0)
        def _():
            acc_ref[...] += p

        @pl.when(k == nk - 1)
        def _():
            o_ref[...] = acc_ref[...].astype(BF16)

    return _pcall(
        body, name=name, grid=(nj, nk),
        in_specs=[pl.BlockSpec((tk, a_cols), (lambda j, k: (k, j)) if a_by_j else (lambda j, k: (k, 0))),
                  pl.BlockSpec((tk, g_cols), (lambda j, k: (k, j)) if g_by_j else (lambda j, k: (k, 0)))],
        out_specs=pl.BlockSpec((None, a_cols, g_cols), lambda j, k: (j, 0, 0)),
        out_shape=jax.ShapeDtypeStruct((nj, a_cols, g_cols), BF16),
        scratch_shapes=[pltpu.VMEM((a_cols, g_cols), F32)],
        compiler_params=_params(("arbitrary", "arbitrary")),
    )(a, g)


def _attn_in_fwd(x, g_pre, wqkv, wf, bf, tm, q_mul, n_heads):
    T, D = x.shape

    def body(x_ref, g_ref, w_ref, wf_ref, bf_ref, q_ref, k_ref, v_ref, lf_ref):
        h = _rms(x_ref[...], g_ref[...]).astype(BF16)
        q = _dot(h, w_ref[:, 0:D])
        if q_mul != 1.0:
            q = q * q_mul
        q_ref[...] = q.astype(BF16)
        k_ref[...] = _dot(h, w_ref[:, D:2 * D]).astype(BF16)
        v_ref[...] = _dot(h, w_ref[:, 2 * D:3 * D]).astype(BF16)
        fl = _dot(h, wf_ref[...]) + bf_ref[...]
        lf = jnp.minimum(fl, 0.0) - jnp.log(1.0 + jnp.exp(-jnp.abs(fl)))
        lane = lax.broadcasted_iota(jnp.int32, (1, LANES), 1)
        lf_ref[...] = jnp.where(lane < n_heads, lf, 0.0)

    return _pcall(
        body, name="attn_in_fwd", grid=(T // tm,),
        in_specs=[_rows(tm, D), _full((1, D)), _full((D, 3 * D)), _full((D, LANES)), _full((1, LANES))],
        out_specs=[_rows(tm, D), _rows(tm, D), _rows(tm, D), _rows(tm, LANES)],
        out_shape=[jax.ShapeDtypeStruct((T, D), BF16)] * 3 + [jax.ShapeDtypeStruct((T, LANES), F32)],
        compiler_params=_params(("arbitrary",)),
    )(x, g_pre, wqkv, wf, bf)


def _cumsum_rows(v, v2, tb, reverse, name):
    T, C = v.shape
    nb = T // tb

    def body(v_ref, v2_ref, o_ref, carry_ref):
        i = pl.program_id(0)

        @pl.when(i == 0)
        def _():
            carry_ref[...] = jnp.zeros_like(carry_ref)
        r = lax.broadcasted_iota(jnp.int32, (tb, tb), 0)
        c = lax.broadcasted_iota(jnp.int32, (tb, tb), 1)
        tri = jnp.where((c >= r) if reverse else (c <= r), 1.0, 0.0).astype(F32)
        out = jnp.dot(tri, v_ref[...] + v2_ref[...], precision=lax.Precision.HIGHEST,
                      preferred_element_type=F32) + carry_ref[...]
        o_ref[...] = out
        carry_ref[...] = out[0:1, :] if reverse else out[tb - 1:tb, :]

    idx = (lambda i: (nb - 1 - i, 0)) if reverse else (lambda i: (i, 0))
    return _pcall(
        body, name=name, grid=(nb,),
        in_specs=[pl.BlockSpec((tb, C), idx), pl.BlockSpec((tb, C), idx)],
        out_specs=pl.BlockSpec((tb, C), idx),
        out_shape=jax.ShapeDtypeStruct((T, C), F32),
        scratch_shapes=[pltpu.VMEM((1, C), F32)],
        compiler_params=_params(("arbitrary",)),
    )(v, v2)


def _head_col(v, lane, h):
    return jnp.sum(jnp.where(lane == h, v, 0.0), axis=1, keepdims=True)


def _flash_fwd(q, k, v, fq_aux, fk_rows, *, dh, tq, s_mul):
    T, D = q.shape
    G = D // LANES
    hpg = LANES // dh
    nq = T // tq
    k3 = k.reshape(nq, tq, D)
    v3 = v.reshape(nq, tq, D)

    rc = min(FLASH_ROWS, tq)

    def body(q_ref, k_ref, v_ref, fq_ref, fk_ref, o_ref, o32_ref, lse_ref, s_scr, p_scr):
        i = pl.program_id(1)
        lane = lax.broadcasted_iota(jnp.int32, (1, LANES), 1)
        q2 = q_ref[...]
        hmasks = [(lane >= hh * dh) & (lane < (hh + 1) * dh) for hh in range(hpg)]
        qms = [jnp.where(hm, q2, jnp.zeros_like(q2)) for hm in hmasks]

        nlb = tq // LANES
        sum_lane = [((hh + 1) % hpg) * dh for hh in range(hpg)]

        def scores(j, slot):
            kj = k_ref[j]
            for hh in range(hpg):
                s = _dot_nt(qms[hh], kj)
                s_scr[slot, hh] = s if s_mul == 1.0 else s * s_mul

        def soft(j, slot, carry, masked):
            vj = v_ref[j]
            out = []
            for hh in range(hpg):
                m_b, acc = carry[hh]
                fk_row = fk_ref[j, hh:hh + 1, :]
                def future(r0, cb):
                    return masked and cb * LANES > r0 + rc - 1

                mx = []
                for r0 in range(0, tq, rc):
                    rs = slice(r0, r0 + rc)
                    c = None
                    for cb in range(nlb):
                        if future(r0, cb):
                            continue
                        cs = slice(cb * LANES, (cb + 1) * LANES)
                        s = s_scr[slot, hh, rs, cs] - fk_row[:, cs]
                        if masked and (cb + 1) * LANES - 1 > r0:
                            ri = r0 + lax.broadcasted_iota(jnp.int32, (rc, LANES), 0)
                            ci = cb * LANES + lax.broadcasted_iota(jnp.int32, (rc, LANES), 1)
                            s = jnp.where(ci <= ri, s, MASK_VALUE)
                        s_scr[slot, hh, rs, cs] = s
                        c = s if c is None else jnp.maximum(c, s)
                    mx.append(c)
                row_max = jnp.max(jnp.concatenate(mx, axis=0), axis=1, keepdims=True)
                m_new = jnp.maximum(m_b, row_max)
                alpha = jnp.exp(m_b - m_new)
                for r0 in range(0, tq, rc):
                    rs = slice(r0, r0 + rc)
                    m_c = m_new[rs]
                    for cb in range(nlb):
                        cs = slice(cb * LANES, (cb + 1) * LANES)
                        if future(r0, cb):
                            p_scr[slot, hh, rs, cs] = jnp.zeros((rc, LANES), BF16)
                        else:
                            p_scr[slot, hh, rs, cs] = jnp.exp(s_scr[slot, hh, rs, cs] - m_c).astype(BF16)
                v_one = jnp.where(hmasks[hh], vj, jnp.ones_like(vj))
                out.append((m_new, alpha * acc + _dot(p_scr[slot, hh], v_one)))
            return tuple(out)

        def step(j, carry, masked, slot):
            scores(j, slot)
            return soft(j, slot, carry, masked)

        def pair(t, cr):
            scores(2 * t, 0)
            scores(2 * t + 1, 1)
            return soft(2 * t + 1, 1, soft(2 * t, 0, cr, False), False)

        init = tuple((jnp.full((tq, LANES), MASK_VALUE, F32), jnp.zeros((tq, LANES), F32)) for _ in range(hpg))
        carry = lax.fori_loop(0, i // 2, pair, init)
        carry = lax.cond(i % 2 == 1, lambda cr: step(i - 1, cr, False, 0), lambda cr: cr, carry)
        carry = step(i, carry, True, 1)
        fq = fq_ref[...]
        g0 = pl.program_id(0) * hpg
        o_all = jnp.zeros((tq, LANES), F32)
        lse_all = jnp.zeros((tq, LANES), F32)
        for hh in range(hpg):
            m_b, acc = carry[hh]
            l = acc[:, sum_lane[hh]:sum_lane[hh] + 1]
            o_all = jnp.where(hmasks[hh], acc * (1.0 / l), o_all)
            lse_all = jnp.where(lane == hh, m_b[:, 0:1] + jnp.log(l) + _head_col(fq, lane, g0 + hh), lse_all)
        o_ref[...] = o_all.astype(BF16)
        o32_ref[...] = o_all
        lse_ref[...] = lse_all

    return _pcall(
        body, name="flash_fwd", grid=(G, nq),
        in_specs=[pl.BlockSpec((tq, LANES), lambda g, i: (i, g)),
                  pl.BlockSpec((nq, tq, LANES), lambda g, i: (0, 0, g)),
                  pl.BlockSpec((nq, tq, LANES), lambda g, i: (0, 0, g)),
                  pl.BlockSpec((tq, LANES), lambda g, i: (i, 0)),
                  pl.BlockSpec((None, nq, SUBLANES, tq), lambda g, i: (g, 0, 0, 0))],
        out_specs=[pl.BlockSpec((tq, LANES), lambda g, i: (i, g)),
                   pl.BlockSpec((tq, LANES), lambda g, i: (i, g)),
                   pl.BlockSpec((None, tq, LANES), lambda g, i: (g, i, 0))],
        out_shape=[jax.ShapeDtypeStruct((T, D), BF16), jax.ShapeDtypeStruct((T, D), F32),
                   jax.ShapeDtypeStruct((G, T, LANES), F32)],
        scratch_shapes=[pltpu.VMEM((2, hpg, tq, tq), F32), pltpu.VMEM((2, hpg, tq, tq), BF16)],
        compiler_params=_params(("arbitrary", "arbitrary")),
    )(q, k3, v3, fq_aux, fk_rows)


def _flash_bwd(q, k, v, do, fq_aux, lse_aux, dl_aux, fk_rows, *, dh, tq, tk, s_mul, dq_mul):
    T, D = q.shape
    G = D // LANES
    hpg = LANES // dh
    nq = T // tq
    nk = T // tk
    per = tq // tk
    k3 = k.reshape(nk, tk, D)
    v3 = v.reshape(nk, tk, D)
    rc = min(FLASH_ROWS, tq)

    def body(q_ref, k_ref, v_ref, do_ref, fq_ref, lse_ref, dl_ref, fk_ref,
             dq_ref, dk_ref, dv_ref, dfq_ref, dfk_ref, dk_acc, dv_acc, s_scr, dp_scr, p_scr, ds_scr):
        i = pl.program_id(1)

        @pl.when(i == 0)
        def _():
            dk_acc[...] = jnp.zeros_like(dk_acc)
            dv_acc[...] = jnp.zeros_like(dv_acc)
            dfk_ref[...] = jnp.zeros_like(dfk_ref)

        lane = lax.broadcasted_iota(jnp.int32, (1, LANES), 1)
        q2 = q_ref[...]
        do2 = do_ref[...]
        g0 = pl.program_id(0) * hpg
        fq = fq_ref[...]
        lse = lse_ref[...]
        dl = dl_ref[...]
        hmasks = [(lane >= hh * dh) & (lane < (hh + 1) * dh) for hh in range(hpg)]
        qms = [jnp.where(hm, q2, jnp.zeros_like(q2)) for hm in hmasks]
        doms = [jnp.where(hm, do2, jnp.zeros_like(do2)) for hm in hmasks]
        q2t = q2.astype(F32).T.astype(BF16)
        do2t = do2.astype(F32).T.astype(BF16)
        ones_rows = jnp.ones((2 * SUBLANES, tk), BF16)
        q1t = [jnp.concatenate([q2t[hh * dh:(hh + 1) * dh], jnp.ones((2 * SUBLANES, tq), BF16)], axis=0)
               for hh in range(hpg)]
        c_bs = [jnp.broadcast_to(_head_col(fq, lane, g0 + hh) - lse[:, hh:hh + 1], (tq, LANES))
                for hh in range(hpg)]
        dl_bs = [jnp.broadcast_to(_head_col(dl, lane, g0 + hh), (tq, LANES)) for hh in range(hpg)]
        nlb = tk // LANES

        def step(j, carry, off, slot=0):
            masked = off is not None
            kj = k_ref[j]
            vj = v_ref[j]
            kjt = kj.astype(F32).T.astype(BF16)
            for hh in range(hpg):
                s = _dot_nt(qms[hh], kj)
                s_scr[slot, hh] = s if s_mul == 1.0 else s * s_mul
                dp_scr[slot, hh] = _dot_nt(doms[hh], vj)
            out = []
            for hh in range(hpg):
                fk_row = fk_ref[j, hh:hh + 1, :]
                for r0 in range(0, tq, rc):
                    rs = slice(r0, r0 + rc)
                    if masked and r0 + rc <= off:
                        p_scr[slot, hh, rs, :] = jnp.zeros((rc, tk), BF16)
                        ds_scr[slot, hh, rs, :] = jnp.zeros((rc, tk), BF16)
                        continue
                    c_c = c_bs[hh][rs]
                    dl_c = dl_bs[hh][rs]
                    for cb in range(nlb):
                        cs = slice(cb * LANES, (cb + 1) * LANES)
                        if masked and off + cb * LANES > r0 + rc - 1:
                            p_scr[slot, hh, rs, cs] = jnp.zeros((rc, LANES), BF16)
                            ds_scr[slot, hh, rs, cs] = jnp.zeros((rc, LANES), BF16)
                            continue
                        e = (s_scr[slot, hh, rs, cs] - fk_row[:, cs]) + c_c
                        if masked and off + (cb + 1) * LANES - 1 > r0:
                            ri = r0 + lax.broadcasted_iota(jnp.int32, (rc, LANES), 0)
                            ci = off + cb * LANES + lax.broadcasted_iota(jnp.int32, (rc, LANES), 1)
                            e = jnp.where(ci <= ri, e, MASK_VALUE)
                        p = jnp.exp(e)
                        ds = p * (dp_scr[slot, hh, rs, cs] - dl_c)
                        p_scr[slot, hh, rs, cs] = p.astype(BF16)
                        ds_scr[slot, hh, rs, cs] = ds.astype(BF16)
                k1t = jnp.concatenate([kjt[hh * dh:(hh + 1) * dh], ones_rows], axis=0)
                out.append(carry[hh] + _dot_nt(k1t, ds_scr[slot, hh]))
            dks = [_dot(q1t[hh], ds_scr[slot, hh]) for hh in range(hpg)]
            for hh in range(hpg):
                dfk_ref[j, hh:hh + 1, :] += -dks[hh][dh:dh + 1, :]
            dk_acc[j] += jnp.concatenate([d[0:dh] for d in dks], axis=0)
            dv_acc[j] += jnp.concatenate(
                [_dot(do2t[hh * dh:(hh + 1) * dh], p_scr[slot, hh]) for hh in range(hpg)], axis=0)
            return tuple(out)

        init = tuple(jnp.zeros((dh + 2 * SUBLANES, tq), F32) for _ in range(hpg))
        n_past = i * per

        def pair(t, cr):
            return step(2 * t + 1, step(2 * t, cr, None, 0), None, 1)

        carry = lax.fori_loop(0, n_past // 2, pair, init)
        carry = lax.cond(n_past % 2 == 1, lambda cr: step(n_past - 1, cr, None, 0), lambda cr: cr, carry)
        for d in range(per):
            carry = step(i * per + d, carry, d * tk, (d + 1) % 2)
        dq_all = jnp.concatenate([carry[hh][0:dh] for hh in range(hpg)], axis=0).T
        dq_ref[...] = (dq_all * dq_mul).astype(BF16)
        dfq_ref[...] = jnp.concatenate([carry[hh][dh:dh + 1, :] for hh in range(hpg)]
                                       + [jnp.zeros((SUBLANES - hpg, tq), F32)], axis=0)

        @pl.when(i == nq - 1)
        def _():
            for jj in range(nk):
                dkv = dk_acc[jj].T
                if s_mul != 1.0:
                    dkv = dkv * s_mul
                dk_ref[jj] = dkv.astype(BF16)
                dv_ref[jj] = dv_acc[jj].T.astype(BF16)

    blk = pl.BlockSpec((tq, LANES), lambda g, i: (i, g))
    res = pl.BlockSpec((nk, tk, LANES), lambda g, i: (0, 0, g))
    res_in = pl.BlockSpec((nk, tk, LANES), lambda g, i: (0, 0, g), pipeline_mode=pl.Buffered(1))
    aux = pl.BlockSpec((None, tq, LANES), lambda g, i: (g, i, 0))
    heads = pl.BlockSpec((tq, LANES), lambda g, i: (i, 0))
    rows = pl.BlockSpec((None, nk, SUBLANES, tk), lambda g, i: (g, 0, 0, 0))
    dq, dk3, dv3, dfq, dfk = _pcall(
        body, name="flash_bwd", grid=(G, nq),
        in_specs=[blk, res_in, res_in, blk, heads, aux, heads, rows],
        out_specs=[blk, res, res, pl.BlockSpec((None, None, SUBLANES, tq), lambda g, i: (g, i, 0, 0)), rows],
        out_shape=[jax.ShapeDtypeStruct((T, D), BF16), jax.ShapeDtypeStruct((nk, tk, D), BF16),
                   jax.ShapeDtypeStruct((nk, tk, D), BF16), jax.ShapeDtypeStruct((G, nq, SUBLANES, tq), F32),
                   jax.ShapeDtypeStruct((G, nk, SUBLANES, tk), F32)],
        scratch_shapes=[pltpu.VMEM((nk, LANES, tk), F32), pltpu.VMEM((nk, LANES, tk), F32),
                        pltpu.VMEM((2, hpg, tq, tk), F32), pltpu.VMEM((2, hpg, tq, tk), F32),
                        pltpu.VMEM((2, hpg, tq, tk), BF16), pltpu.VMEM((2, hpg, tq, tk), BF16)],
        compiler_params=_params(("arbitrary", "arbitrary")),
    )(q, k3, v3, do, fq_aux, lse_aux, dl_aux, fk_rows)
    return dq, dk3.reshape(T, D), dv3.reshape(T, D), dfq, dfk


def _attn_out_fwd(o, x, wo, g_post, tm):
    T, D = x.shape

    def body(o_ref, x_ref, w_ref, g_ref, m_ref, xo_ref):
        m = _dot(o_ref[...], w_ref[...])
        m_ref[...] = m
        xo_ref[...] = x_ref[...] + _rms(m, g_ref[...])

    return _pcall(
        body, name="attn_out_fwd", grid=(T // tm,),
        in_specs=[_rows(tm, D), _rows(tm, D), _full((D, D)), _full((1, D))],
        out_specs=[_rows(tm, D), _rows(tm, D)],
        out_shape=[jax.ShapeDtypeStruct((T, D), F32), jax.ShapeDtypeStruct((T, D), F32)],
        compiler_params=_params(("arbitrary",)),
    )(o, x, wo, g_post)


def _attn_out_bwd(dxo, m, o, wo, g_post, head_ind, tm, dep):
    T, D = m.shape

    def body(dxo_ref, m_ref, o_ref, w_ref, g_ref, ind_ref, dep_ref, dm_ref, do_ref, dl_ref, sums_ref):
        i = pl.program_id(0)
        dm, dgpost = _rms_bwd(m_ref[...], g_ref[...], dxo_ref[...])
        dmb = dm.astype(BF16)
        dm_ref[...] = dmb
        dob = _dot_nt(dmb, w_ref[...]).astype(BF16)
        do_ref[...] = dob
        dl_ref[...] = jnp.dot(dob.astype(F32) * o_ref[...], ind_ref[...], precision=lax.Precision.HIGHEST,
                              preferred_element_type=F32)

        @pl.when(i == 0)
        def _():
            sums_ref[...] = jnp.zeros_like(sums_ref)
        sums_ref[0:1, :] += dgpost

    return _pcall(
        body, name="attn_out_bwd", grid=(T // tm,),
        in_specs=[_rows(tm, D), _rows(tm, D), _rows(tm, D), _full((D, D)), _full((1, D)), _full((D, LANES)),
                  pl.BlockSpec(memory_space=pl.ANY)],
        out_specs=[_rows(tm, D), _rows(tm, D), _rows(tm, LANES), _acc((SUBLANES, D))],
        out_shape=[jax.ShapeDtypeStruct((T, D), BF16), jax.ShapeDtypeStruct((T, D), BF16),
                   jax.ShapeDtypeStruct((T, LANES), F32), jax.ShapeDtypeStruct((SUBLANES, D), F32)],
        compiler_params=_params(("arbitrary",)),
    )(dxo, m, o, wo, g_post, head_ind, dep)


def _attn_in_bwd(dxo, x, g_pre, dq, dk, dv, dlf, lf, wqkv, wf, tm, n_heads):
    T, D = x.shape

    def body(dxo_ref, x_ref, g_ref, dq_ref, dk_ref, dv_ref, dlf_ref, lf_ref, w_ref, wf_ref,
             dxi_ref, h_ref, df_ref, sums_ref, dbf_ref):
        i = pl.program_id(0)
        xv = x_ref[...]
        h_ref[...] = _rms(xv, g_ref[...]).astype(BF16)
        lane = lax.broadcasted_iota(jnp.int32, (1, LANES), 1)
        df = jnp.where(lane < n_heads, dlf_ref[...] * (1.0 - jnp.exp(lf_ref[...])), 0.0)
        dfb = df.astype(BF16)
        df_ref[...] = dfb
        dh = (_dot_nt(dq_ref[...], w_ref[:, 0:D]) + _dot_nt(dk_ref[...], w_ref[:, D:2 * D])
              + _dot_nt(dv_ref[...], w_ref[:, 2 * D:3 * D]) + _dot_nt(dfb, wf_ref[...]))
        dxi, dgpre = _rms_bwd(xv, g_ref[...], dh)
        dxi_ref[...] = dxo_ref[...] + dxi

        @pl.when(i == 0)
        def _():
            sums_ref[...] = jnp.zeros_like(sums_ref)
            dbf_ref[...] = jnp.zeros_like(dbf_ref)
        sums_ref[0:1, :] += dgpre
        dbf_ref[...] += jnp.sum(df, axis=0, keepdims=True)

    return _pcall(
        body, name="attn_in_bwd", grid=(T // tm,),
        in_specs=[_rows(tm, D), _rows(tm, D), _full((1, D)), _rows(tm, D), _rows(tm, D), _rows(tm, D),
                  _rows(tm, LANES), _rows(tm, LANES), _full((D, 3 * D)), _full((D, LANES))],
        out_specs=[_rows(tm, D), _rows(tm, D), _rows(tm, LANES), _acc((SUBLANES, D)), _acc((1, LANES))],
        out_shape=[jax.ShapeDtypeStruct((T, D), F32), jax.ShapeDtypeStruct((T, D), BF16),
                   jax.ShapeDtypeStruct((T, LANES), BF16), jax.ShapeDtypeStruct((SUBLANES, D), F32),
                   jax.ShapeDtypeStruct((1, LANES), F32)],
        compiler_params=_params(("arbitrary",)),
    )(dxo, x, g_pre, dq, dk, dv, dlf, lf, wqkv, wf)


def _adamw(recvs, w, m, v, tr, name):
    L, R, C = w.shape
    assert len(recvs) == L
    c1 = 1.0 - ADAM_B1 ** ADAM_STEP
    c2 = 1.0 - ADAM_B2 ** ADAM_STEP

    def body(*refs):
        r_refs = refs[:L]
        w_ref, m_ref, v_ref, g_ref, d_ref, nm_ref, nv_ref = refs[L:]
        layer = pl.program_id(0)
        g = None
        for k in range(L):
            gk = r_refs[k][0, :, 0:C].astype(F32)
            for s in range(1, N_DEV):
                gk = gk + r_refs[k][s, :, 0:C].astype(F32)
            g = gk if g is None else jnp.where(layer == k, gk, g)
        nm = ADAM_B1 * m_ref[...] + (1.0 - ADAM_B1) * g
        nv = ADAM_B2 * v_ref[...] + (1.0 - ADAM_B2) * jnp.square(g)
        m_hat = nm / c1
        v_hat = nv / c2
        g_ref[...] = g
        d_ref[...] = -ADAM_LR * (m_hat / (jnp.sqrt(v_hat) + ADAM_EPS) + ADAM_WD * w_ref[...])
        nm_ref[...] = nm
        nv_ref[...] = nv

    def recv_spec(k):
        return pl.BlockSpec((N_DEV, tr, recvs[k].shape[-1]), lambda l, i: (0, jnp.where(l == k, i, 0), 0))

    blk = pl.BlockSpec((None, tr, C), lambda l, i: (l, i, 0))
    return _pcall(
        body, name=name, grid=(L, R // tr),
        in_specs=[recv_spec(k) for k in range(L)] + [blk] * 3,
        out_specs=[blk] * 4,
        out_shape=[jax.ShapeDtypeStruct((L, R, C), F32)] * 4,
        compiler_params=_params(("arbitrary", "arbitrary")),
    )(*recvs, w, m, v)


def _row_block(rows, cols):
    cap = max(SUBLANES, (256 * 1024) // max(cols, 1))
    best = None
    for t in range(SUBLANES, rows + 1, SUBLANES):
        if rows % t == 0 and t <= cap:
            best = t
    return rows if best is None else best


def kernel(x, g_mix_pre, g_mix_post, g_ffn_pre, g_ffn_post, conv_pw1_w, conv_pw1_b, conv_dw_w, conv_dw_b, conv_ln_g, conv_ln_b, conv_pw2_w, conv_pw2_b, attn_w_in, attn_b_f, attn_w_o, mlp_w_up, mlp_w_down, loss_target, m_g_mix_pre, m_g_mix_post, m_g_ffn_pre, m_g_ffn_post, m_conv_pw1_w, m_conv_pw1_b, m_conv_dw_w, m_conv_dw_b, m_conv_ln_g, m_conv_ln_b, m_conv_pw2_w, m_conv_pw2_b, m_attn_w_in, m_attn_b_f, m_attn_w_o, m_mlp_w_up, m_mlp_w_down, v_g_mix_pre, v_g_mix_post, v_g_ffn_pre, v_g_ffn_post, v_conv_pw1_w, v_conv_pw1_b, v_conv_dw_w, v_conv_dw_b, v_conv_ln_g, v_conv_ln_b, v_conv_pw2_w, v_conv_pw2_b, v_attn_w_in, v_attn_b_f, v_attn_w_o, v_mlp_w_up, v_mlp_w_down):
    _, T, D = x.shape
    H = attn_b_f.shape[-1]
    dh = D // H
    width = conv_dw_w.shape[1]
    cin = attn_w_in.shape[-1]
    fs = mlp_w_up.shape[-1]
    G = D // LANES
    hpg = LANES // dh
    assert T % 4 == 0 and D % LANES == 0 and LANES % dh == 0 and width <= CONV_HALO and H <= LANES

    tm = min(512, T // 4)
    tmb = min(256, T // 4)
    tqf = min(1024, T // 4)
    tkb = tm
    tmc = min(256, T // 4)
    lc = min(256, D)
    tkw = min(2048, T // 4)
    tb = min(256, T // 4)

    scale = float(dh) ** -0.5
    mant, _ = math.frexp(scale)
    q_mul = scale if mant == 0.5 else 1.0
    s_mul = 1.0 if mant == 0.5 else scale

    x2 = x.reshape(T, D)
    tgt = loss_target.reshape(T, D)

    w_srcs = [conv_pw1_w, conv_dw_w, conv_pw2_w, mlp_w_up, mlp_w_down, attn_w_in, attn_w_o]
    w_items = [(0, 0, "whole"), (1, 0, "whole"), (2, 0, "whole"), (3, 0, "whole"), (4, 0, "whole"),
               (5, 0, "whole"), (6, 0, "whole"), (3, 1, "whole"), (4, 1, "whole")]
    cin_w = -(-cin // LANES) * LANES
    w_lands = _place_own(
        w_srcs, w_items,
        [((N_DEV, D, cin_w) if si == 5 else (N_DEV,) + w_srcs[si].shape[1:], F32 if si == 1 else BF16)
         for si, _, _ in w_items],
        "stage_weights", cast=True)
    me_arr = _dev_index(*_mesh_pos()).astype(jnp.int32).reshape(1)
    w_groups = [[0], [1, 2], [3, 4], [5, 6], [7, 8]]
    g_sems, _, w_lands, g_token = _push_start(
        [], w_lands, [[(a, None, "own") for a in grp] for grp in w_groups], "gather_start")

    def gather_wait(gi, after):
        grp = w_groups[gi]
        return _push_wait(g_sems[gi], [], [w_lands[a] for a in grp], [(k, None, "own") for k in range(len(grp))],
                          after, "gather_wait%d" % gi)

    (w1g,) = gather_wait(0, g_token)
    bf = jnp.pad(attn_b_f, ((0, 0), (0, LANES - H)))

    row = lambda a, i: a[i:i + 1]

    a0, u0 = _conv_in_fwd(x2, row(g_mix_pre, 0), w1g, conv_pw1_b, tm)
    dwg, w2g = gather_wait(1, u0)
    w2 = w2g.reshape(D, D)
    dw_full = jnp.transpose(dwg, (1, 0, 2)).reshape(width, D)
    w32 = jnp.pad(dw_full, ((0, 32 - width), (0, 0)))
    y0 = _dwconv_fwd(u0, w32, conv_dw_b, tmc, lc, width)
    m0, x_1 = _conv_out_fwd(y0, x2, conv_ln_g, conv_ln_b, w2, conv_pw2_b, row(g_mix_post, 0), tm)
    wu0, wd0 = gather_wait(2, x_1)
    up0, n0, x_2 = _mlp_fwd(x_1, row(g_ffn_pre, 0), wu0, wd0, row(g_ffn_post, 0), tm, "mlp0_fwd")

    wing, wog = gather_wait(3, x_2)
    wo = wog.reshape(D, D)
    win = jnp.transpose(wing[:, :, :cin], (1, 0, 2)).reshape(D, N_DEV * cin)
    wqkv = win[:, :3 * D]
    wf = jnp.pad(win[:, 3 * D:], ((0, 0), (0, LANES - H)))
    q, k, v, lf = _attn_in_fwd(x_2, row(g_mix_pre, 1), wqkv, wf, bf, tm, q_mul, H)
    fcum = _cumsum_rows(lf, jnp.zeros_like(lf), tb, False, "forget_cumsum")


    def key_rows(blk):
        r = jnp.transpose(fcum[:, :H].T.reshape(G, hpg, T // blk, blk), (0, 2, 1, 3))
        return jnp.pad(r, ((0, 0), (0, 0), (0, SUBLANES - hpg), (0, 0)))

    fk_rows = key_rows(tkb)
    o, o32, lse_aux = _flash_fwd(q, k, v, fcum, key_rows(tqf), dh=dh, tq=tqf, s_mul=s_mul)
    m1, x_3 = _attn_out_fwd(o, x_2, wo, row(g_mix_post, 1), tm)
    wu1, wd1 = gather_wait(4, x_3)
    up1, n1, x_4 = _mlp_fwd(x_3, row(g_ffn_pre, 1), wu1, wd1, row(g_ffn_post, 1), tm, "mlp1_fwd")

    def mlp_back(dx, n_l, x_in, up_l, l, wu, wd, dep, from_loss=False):
        dxi, h, dm, dup, sums = _mlp_bwd(dx, n_l, x_in, up_l, row(g_ffn_pre, l), wu, wd, row(g_ffn_post, l),
                                         tmb, "mlp%d_bwd" % l, dep, from_loss)
        dwu = _mm_tn(h, dup, nj=N_DEV, a_cols=D, g_cols=fs, a_by_j=False, g_by_j=True, tk=tkw,
                     name="mlp%d_dwu" % l)
        dwd = _mm_tn(up_l, dm, nj=N_DEV, a_cols=fs, g_cols=D, a_by_j=True, g_by_j=False, tk=tkw,
                     name="mlp%d_dwd" % l, act=True)
        return dxi, dwu, dwd, sums

    def push_grads(srcs, modes, name):
        items = [(i, None, mode) for i, mode in enumerate(modes)]
        lands = _seed_lands(srcs, modes, me_arr, name + "_own")
        sems, srcs_t, lands_t, token = _push_start(srcs, lands, [items], name + "_start")
        return (sems[0], srcs_t, lands_t, items), token

    def pull_grads(handle, after, name):
        sems, srcs_t, lands_t, items = handle
        return _push_wait(sems, srcs_t, lands_t, items, after, name + "_wait")

    rs = D // N_DEV
    dx, dwu1, dwd1, s_mlp1 = mlp_back(x_4, n1, x_3, up1, 1, wu1, wd1, tgt, from_loss=True)
    h_mlp1, tok = push_grads([dwu1, dwd1], ["slot", "slot"], "grads_mlp1")

    head_ind = jnp.asarray((np.arange(D)[:, None] // dh == np.arange(LANES)[None, :]).astype(np.float32))
    dm1, do, delta, s_ao = _attn_out_bwd(dx, m1, o32, wo, row(g_mix_post, 1), head_ind, tm, tok)
    dwo = _mm_tn(o, dm1, nj=1, a_cols=D, g_cols=D, a_by_j=False, g_by_j=False, tk=tkw, name="attn_dwo")
    dq, dk, dv, dfq, dfk = _flash_bwd(q, k, v, do, fcum, lse_aux, delta, fk_rows,
                                      dh=dh, tq=tkb, tk=tkb, s_mul=s_mul, dq_mul=scale)
    def head_cols(r):
        return jnp.pad(jnp.transpose(r[:, :, :hpg, :], (0, 2, 1, 3)).reshape(H, T).T, ((0, 0), (0, LANES - H)))

    df_k = head_cols(dfk)
    df_q = head_cols(dfq)
    dlf = _cumsum_rows(df_q, df_k, tb, True, "forget_cumsum_bwd")
    dx, h_at, df, s_ai, dbf = _attn_in_bwd(dx, x_2, row(g_mix_pre, 1), dq, dk, dv, dlf, lf, wqkv, wf, tm, H)
    dwq = _mm_tn(h_at, dq, nj=1, a_cols=D, g_cols=D, a_by_j=False, g_by_j=False, tk=tkw, name="attn_dwq")
    dwk = _mm_tn(h_at, dk, nj=1, a_cols=D, g_cols=D, a_by_j=False, g_by_j=False, tk=tkw, name="attn_dwk")
    dwv = _mm_tn(h_at, dv, nj=1, a_cols=D, g_cols=D, a_by_j=False, g_by_j=False, tk=tkw, name="attn_dwv")
    dwf = _mm_tn(h_at, df, nj=1, a_cols=D, g_cols=LANES, a_by_j=False, g_by_j=False, tk=tkw, name="attn_dwf")
    dwin = jnp.concatenate([dwq[0], dwk[0], dwv[0], dwf[0][:, :H]], axis=1)
    dwin = jnp.pad(jnp.transpose(dwin.reshape(D, N_DEV, cin), (1, 0, 2)), ((0, 0), (0, 0), (0, cin_w - cin)))
    h_attn, tok = push_grads([dwin, dwo.reshape(N_DEV, rs, D)], ["slot", "slot"], "grads_attn")

    dx, dwu0, dwd0, s_mlp0 = mlp_back(dx, n0, x_1, up0, 0, wu0, wd0, tok)
    h_mlp0, tok = push_grads([dwu0, dwd0], ["slot", "slot"], "grads_mlp0")

    dy0, dm0, z0, s_co = _conv_out_bwd(dx, m0, y0, conv_ln_g, conv_ln_b, w2, row(g_mix_post, 0), tm, tok)
    dw2 = _mm_tn(z0, dm0, nj=1, a_cols=D, g_cols=D, a_by_j=False, g_by_j=False, tk=tkw, name="conv_dw2")
    du0, ddw = _dwconv_bwd(dy0, u0, w32, tmc, lc, width)
    grad_x, h_cv, da0, s_ci, db1 = _conv_in_bwd(dx, du0, a0, x2, row(g_mix_pre, 0), w1g, tm)
    dw1 = _mm_tn(h_cv, da0, nj=N_DEV, a_cols=D, g_cols=(2 * D) // N_DEV, a_by_j=False, g_by_j=True, tk=tkw,
                 name="conv_dw1")
    ddw_s = jnp.transpose(ddw[:width].reshape(width, N_DEV, D // N_DEV), (1, 0, 2))

    def pad_row(a):
        return jnp.pad(a, ((0, 0), (0, D - a.shape[1])))

    def pack(gmp, gmq, gfp, gfq, b1, dwb, lng, lnb, b2, bfv, last):
        return jnp.concatenate([gmp, gmq, gfp, gfq, b1.reshape(2, D), dwb, lng, lnb, b2, pad_row(bfv), last],
                               axis=0)

    zero_row = jnp.zeros((1, D), F32)
    small_g = pack(
        jnp.concatenate([row(s_ci, 0), row(s_ai, 0)], axis=0),
        jnp.concatenate([row(s_co, 0), row(s_ao, 0)], axis=0),
        jnp.concatenate([row(s_mlp0, 1), row(s_mlp1, 1)], axis=0),
        jnp.concatenate([row(s_mlp0, 0), row(s_mlp1, 0)], axis=0),
        db1, row(s_co, 4), row(s_co, 1), row(s_co, 2), row(s_co, 3), dbf[:, :H],
        pad_row(row(s_mlp1, 2)[:, 0:1]))
    h_conv, tok = push_grads([dw1, ddw_s, dw2.reshape(N_DEV, rs, D), small_g], ["slot", "slot", "slot", "whole"],
                             "grads_conv")

    def opt(recvs, w, m, v, name):
        shp = w.shape
        L, C = shp[0], shp[-1]
        R = int(np.prod(shp[1:-1]))
        outs = _adamw([r.reshape(N_DEV, R, r.shape[-1]) for r in recvs], w.reshape(L, R, C), m.reshape(L, R, C),
                      v.reshape(L, R, C), _row_block(R, C), name)
        return [t.reshape(shp) for t in outs]

    big = {}
    r_wu1, r_wd1 = pull_grads(h_mlp1, tok, "grads_mlp1")
    r_win, r_wo = pull_grads(h_attn, r_wd1, "grads_attn")
    big["attn_w_in"] = opt([r_win], attn_w_in, m_attn_w_in, v_attn_w_in, "adamw_win")
    big["attn_w_o"] = opt([r_wo], attn_w_o, m_attn_w_o, v_attn_w_o, "adamw_wo")
    r_wu0, r_wd0 = pull_grads(h_mlp0, big["attn_w_o"][0], "grads_mlp0")
    big["mlp_w_up"] = opt([r_wu0, r_wu1], mlp_w_up, m_mlp_w_up, v_mlp_w_up, "adamw_wup")
    big["mlp_w_down"] = opt([r_wd0, r_wd1], mlp_w_down, m_mlp_w_down, v_mlp_w_down, "adamw_wdown")
    r_w1, r_dw, r_w2, r_small = pull_grads(h_conv, big["mlp_w_down"][0], "grads_conv")
    big["conv_pw1_w"] = opt([r_w1], conv_pw1_w, m_conv_pw1_w, v_conv_pw1_w, "adamw_pw1")
    big["conv_dw_w"] = opt([r_dw], conv_dw_w, m_conv_dw_w, v_conv_dw_w, "adamw_dw")
    big["conv_pw2_w"] = opt([r_w2], conv_pw2_w, m_conv_pw2_w, v_conv_pw2_w, "adamw_pw2")
    small_w = pack(g_mix_pre, g_mix_post, g_ffn_pre, g_ffn_post, conv_pw1_b, conv_dw_b, conv_ln_g, conv_ln_b,
                   conv_pw2_b, attn_b_f, zero_row)
    small_m = pack(m_g_mix_pre, m_g_mix_post, m_g_ffn_pre, m_g_ffn_post, m_conv_pw1_b, m_conv_dw_b, m_conv_ln_g,
                   m_conv_ln_b, m_conv_pw2_b, m_attn_b_f, zero_row)
    small_v = pack(v_g_mix_pre, v_g_mix_post, v_g_ffn_pre, v_g_ffn_post, v_conv_pw1_b, v_conv_dw_b, v_conv_ln_g,
                   v_conv_ln_b, v_conv_pw2_b, v_attn_b_f, zero_row)
    sm = _adamw([r_small], small_w[None], small_m[None], small_v[None], small_w.shape[0], "adamw_small")
    sm = [t[0] for t in sm]
    loss = sm[0][15, 0]

    def unpack(t):
        return {"g_mix_pre": t[0:2], "g_mix_post": t[2:4], "g_ffn_pre": t[4:6], "g_ffn_post": t[6:8],
                "conv_pw1_b": t[8:10].reshape(1, 2 * D), "conv_dw_b": t[10:11], "conv_ln_g": t[11:12],
                "conv_ln_b": t[12:13], "conv_pw2_b": t[13:14], "attn_b_f": t[14:15, :H]}

    small = [unpack(t) for t in sm]
    names = ["g_mix_pre", "g_mix_post", "g_ffn_pre", "g_ffn_post", "conv_pw1_w", "conv_pw1_b", "conv_dw_w",
             "conv_dw_b", "conv_ln_g", "conv_ln_b", "conv_pw2_w", "conv_pw2_b", "attn_w_in", "attn_b_f",
             "attn_w_o", "mlp_w_up", "mlp_w_down"]
    outs = [loss, grad_x.reshape(1, T, D)]
    for kind in range(4):
        for nme in names:
            outs.append(big[nme][kind] if nme in big else small[kind][nme])
    return tuple(outs)
```

```python
import functools
import math

import numpy as np
import jax
import jax.numpy as jnp
from jax import lax
from jax.experimental import pallas as pl
from jax.experimental.pallas import tpu as pltpu

F32 = jnp.float32
BF16 = jnp.bfloat16

RMS_EPS = 1e-6
LN_EPS = 1e-5
MASK_VALUE = -1e30
ADAM_LR = 0.001
ADAM_B1 = 0.9
ADAM_B2 = 0.999
ADAM_EPS = 1e-08
ADAM_WD = 0.01
ADAM_STEP = 10

N_DEV = 8
LANES = 128
SUBLANES = 8
CONV_HALO = 32
CONV_ROWS = 32
FLASH_ROWS = 32
VMEM_LIMIT = 56 * 1024 * 1024

_pcall = pl.pallas_call


def _params(sem=None):
    if sem is None:
        return pltpu.CompilerParams(vmem_limit_bytes=VMEM_LIMIT)
    return pltpu.CompilerParams(dimension_semantics=sem, vmem_limit_bytes=VMEM_LIMIT)


def _dot(a, b):
    return jnp.dot(a, b, preferred_element_type=F32)


def _dot_nt(a, b):
    return lax.dot_general(a, b, (((1,), (1,)), ((), ())), preferred_element_type=F32)


def _dot_tn(a, b):
    return lax.dot_general(a, b, (((0,), (0,)), ((), ())), preferred_element_type=F32)


def _full(shape):
    nd = len(shape)
    return pl.BlockSpec(shape, lambda *g: (0,) * nd, pipeline_mode=pl.Buffered(1))


def _acc(shape):
    nd = len(shape)
    return pl.BlockSpec(shape, lambda *g: (0,) * nd)


def _rows(tm, cols):
    return pl.BlockSpec((tm, cols), lambda i: (i, 0))


def _rms(x, g):
    r = lax.rsqrt(jnp.mean(x * x, axis=-1, keepdims=True) + RMS_EPS)
    return x * r * g


def _rms_bwd(x, g, dy):
    r = lax.rsqrt(jnp.mean(x * x, axis=-1, keepdims=True) + RMS_EPS)
    n = x * r
    dg = jnp.sum(dy * n, axis=0, keepdims=True)
    dn = dy * g
    dx = r * (dn - n * jnp.mean(dn * n, axis=-1, keepdims=True))
    return dx, dg


def _sigmoid(x):
    return 1.0 / (1.0 + jnp.exp(-x))


def _mesh_pos():
    return lax.axis_index("x"), lax.axis_index("y"), lax.axis_index("c")


def _dev_index(px, py, pc):
    return 4 * px + 2 * py + pc


_HBM = pl.BlockSpec(memory_space=pltpu.HBM)
_SEM = pl.BlockSpec(memory_space=pltpu.SEMAPHORE)
_EFFECT = pltpu.SideEffectType.DATAFLOW_SIDE_EFFECTING


def _peer(r, x, y, c):
    p = ((1 - x) if r & 4 else x, (1 - y) if r & 2 else y, (1 - c) if r & 1 else c)
    return p, _dev_index(*p)


def _src_ref(refs, item, me_id=None, to_id=None):
    si, sub, mode = item
    r = refs[si] if sub is None else refs[si].at[sub]
    if mode == "slot":
        return r.at[to_id]
    if mode == "own":
        return r.at[me_id]
    return r


def _place_own(srcs, items, land_shapes, name, cast=False):
    ns, n = len(srcs), len(items)

    def body(*refs):
        src = refs[:ns]
        land = refs[ns:ns + n]
        stage = refs[ns + n:ns + 2 * n] if cast else None
        sems = refs[-1]
        me_id = _dev_index(*_mesh_pos())
        cps = []
        for a, item in enumerate(items):
            s = _src_ref(src, item, to_id=me_id)
            if cast:
                if s.shape != stage[a].shape:
                    stage[a][...] = jnp.zeros_like(stage[a])
                    stage[a][:, 0:s.shape[-1]] = s[...].astype(stage[a].dtype)
                else:
                    stage[a][...] = s[...].astype(stage[a].dtype)
                s = stage[a]
            cp = pltpu.make_async_copy(s, land[a].at[me_id], sems.at[a])
            cp.start()
            cps.append(cp)
        for cp in cps:
            cp.wait()

    return _pcall(
        body, name=name,
        out_shape=[jax.ShapeDtypeStruct(tuple(s), d) for s, d in land_shapes],
        in_specs=[pl.BlockSpec(memory_space=pltpu.VMEM if cast else pl.ANY)] * ns,
        out_specs=[pl.BlockSpec(memory_space=pl.ANY)] * n,
        scratch_shapes=([pltpu.VMEM(tuple(s[1:]), d) for s, d in land_shapes] if cast else [])
        + [pltpu.SemaphoreType.DMA((n,))],
        compiler_params=pltpu.CompilerParams(vmem_limit_bytes=VMEM_LIMIT),
    )(*srcs)


def _seed_lands(srcs, modes, me, name):
    n = len(srcs)
    parts = [tuple(s.shape[1:]) if mode == "slot" else tuple(s.shape) for s, mode in zip(srcs, modes)]

    def body(me_ref, *refs):
        s = pl.program_id(0)
        for a in range(n):
            v = refs[a][...]
            refs[n + a][...] = jnp.where(s == me_ref[0], v, jnp.zeros_like(v))

    def in_spec(part, mode):
        nd = len(part)
        if mode == "slot":
            return pl.BlockSpec((None,) + part, lambda s, me_ref: (me_ref[0],) + (0,) * nd)
        return pl.BlockSpec(part, lambda s, me_ref: (0,) * nd)

    def out_spec(part):
        nd = len(part)
        return pl.BlockSpec((None,) + part, lambda s, me_ref: (s,) + (0,) * nd)

    return _pcall(
        body, name=name,
        grid_spec=pltpu.PrefetchScalarGridSpec(
            num_scalar_prefetch=1, grid=(N_DEV,),
            in_specs=[in_spec(p, m) for p, m in zip(parts, modes)],
            out_specs=[out_spec(p) for p in parts]),
        out_shape=[jax.ShapeDtypeStruct((N_DEV,) + p, s.dtype) for p, s in zip(parts, srcs)],
        compiler_params=_params(("arbitrary",)),
    )(me, *srcs)


def _push_start(srcs, lands, groups, name):
    ns, n = len(srcs), len(lands)
    ng = len(groups)
    assert sum(len(g) for g in groups) == n

    def body(*refs):
        ops = refs[:ns + n]
        land = refs[ns:ns + n]
        sems = refs[ns + n:ns + n + 2 * ng]
        token = refs[-1]
        x, y, c = _mesh_pos()
        me_id = _dev_index(x, y, c)
        a = 0
        for gi, grp in enumerate(groups):
            for k, item in enumerate(grp):
                for r in range(1, N_DEV):
                    p, pid = _peer(r, x, y, c)
                    pltpu.make_async_remote_copy(
                        src_ref=_src_ref(ops, item, me_id=me_id, to_id=pid), dst_ref=land[a].at[me_id],
                        send_sem=sems[2 * gi].at[k * (N_DEV - 1) + r - 1],
                        recv_sem=sems[2 * gi + 1].at[k * (N_DEV - 1) + r - 1],
                        device_id=p, device_id_type=pl.DeviceIdType.MESH).start()
                a += 1
        token[...] = jnp.zeros_like(token)

    sem_shapes = []
    for grp in groups:
        sem_shapes += [pltpu.SemaphoreType.DMA((len(grp) * (N_DEV - 1),))] * 2
    arrs = list(srcs) + list(lands)
    res = _pcall(
        body, name=name,
        out_shape=tuple(sem_shapes) + tuple(pltpu.HBM(a.shape, a.dtype) for a in arrs)
        + (jax.ShapeDtypeStruct((SUBLANES, LANES), F32),),
        in_specs=[_HBM] * (ns + n),
        out_specs=tuple([_SEM] * (2 * ng)) + tuple([_HBM] * (ns + n)) + (pl.BlockSpec(memory_space=pltpu.VMEM),),
        input_output_aliases={i: 2 * ng + i for i in range(ns + n)},
        compiler_params=pltpu.CompilerParams(has_side_effects=_EFFECT),
    )(*[pltpu.with_memory_space_constraint(a, pltpu.HBM) for a in arrs])
    sems = [(res[2 * gi], res[2 * gi + 1]) for gi in range(ng)]
    thru = res[2 * ng:2 * ng + ns + n]
    return sems, list(thru[:ns]), list(thru[ns:]), res[-1]


def _push_wait(sems, srcs, lands, group, after, name):
    ns, n = len(srcs), len(lands)
    assert len(group) == n

    def body(*refs):
        ops = refs[:ns + n]
        land = refs[ns:ns + n]
        send_sems, recv_sems = refs[ns + n], refs[ns + n + 1]
        x, y, c = _mesh_pos()
        me_id = _dev_index(x, y, c)
        for k, item in enumerate(group):
            for r in range(1, N_DEV):
                p, pid = _peer(r, x, y, c)
                cp = pltpu.make_async_remote_copy(
                    src_ref=_src_ref(ops, item, me_id=me_id, to_id=pid), dst_ref=land[k].at[pid],
                    send_sem=send_sems.at[k * (N_DEV - 1) + r - 1], recv_sem=recv_sems.at[k * (N_DEV - 1) + r - 1],
                    device_id=p, device_id_type=pl.DeviceIdType.MESH)
                cp.wait_send()
                cp.wait_recv()

    arrs = list(srcs) + list(lands)
    res = _pcall(
        body, name=name,
        out_shape=tuple(pltpu.HBM(a.shape, a.dtype) for a in arrs),
        in_specs=[_HBM] * (ns + n) + [_SEM, _SEM, pl.BlockSpec(memory_space=pl.ANY)],
        out_specs=tuple([_HBM] * (ns + n)),
        input_output_aliases={i: i for i in range(ns + n)},
        compiler_params=pltpu.CompilerParams(has_side_effects=_EFFECT),
    )(*arrs, sems[0], sems[1], after)
    return list(res[ns:])


def _conv_in_fwd(x, g_pre, w1g, b1, tm):
    T, D = x.shape
    ns, _, cs = w1g.shape
    half = ns // 2

    def body(x_ref, g_ref, w_ref, b_ref, a_ref, u_ref):
        h = _rms(x_ref[...], g_ref[...]).astype(BF16)
        parts = []
        for s in range(ns):
            a_s = _dot(h, w_ref[s]) + b_ref[:, s * cs:(s + 1) * cs]
            a_ref[:, s * cs:(s + 1) * cs] = a_s
            parts.append(a_s)
        for s in range(half):
            u_ref[:, s * cs:(s + 1) * cs] = parts[s] * _sigmoid(parts[s + half])

    return _pcall(
        body, name="conv_in_fwd", grid=(T // tm,),
        in_specs=[_rows(tm, D), _full((1, D)), _full(w1g.shape), _full((1, 2 * D))],
        out_specs=[_rows(tm, 2 * D), _rows(tm, D)],
        out_shape=[jax.ShapeDtypeStruct((T, 2 * D), F32), jax.ShapeDtypeStruct((T, D), F32)],
        compiler_params=_params(("arbitrary",)),
    )(x, g_pre, w1g, b1)


def _shifted_copies(ext_ref, sh_ref, tm):
    n = tm + CONV_HALO - SUBLANES
    for b in range(1, SUBLANES):
        sh_ref[b - 1, 0:n, :] = ext_ref[b:b + n, :]


def _shifted_rows(ext_ref, sh_ref, off, r0, ls):
    b = off % SUBLANES
    a8 = off - b + r0
    src = ext_ref if b == 0 else sh_ref.at[b - 1]
    return src[a8:a8 + CONV_ROWS, ls]


def _dwconv_fwd(u, w32, b, tm, lc, width):
    T, D = u.shape
    hb = tm // CONV_HALO

    def body(u_ref, halo_ref, w_ref, b_ref, y_ref, ext_ref, sh_ref):
        i = pl.program_id(0)
        ext_ref[0:CONV_HALO, :] = jnp.where(i > 0, halo_ref[...], 0.0)
        ext_ref[CONV_HALO:, :] = u_ref[...]
        _shifted_copies(ext_ref, sh_ref, tm)
        for r0 in range(0, tm, CONV_ROWS):
            for l0 in range(0, lc, LANES):
                ls = slice(l0, l0 + LANES)
                acc = jnp.zeros((CONV_ROWS, LANES), F32) + b_ref[:, ls]
                for j in range(width):
                    off = CONV_HALO - (width - 1) + j
                    acc = acc + w_ref[j:j + 1, ls] * _shifted_rows(ext_ref, sh_ref, off, r0, ls)
                y_ref[r0:r0 + CONV_ROWS, ls] = acc

    return _pcall(
        body, name="dwconv_fwd", grid=(T // tm, D // lc),
        in_specs=[pl.BlockSpec((tm, lc), lambda i, l: (i, l)),
                  pl.BlockSpec((CONV_HALO, lc), lambda i, l: (jnp.maximum(i * hb - 1, 0), l)),
                  pl.BlockSpec((32, lc), lambda i, l: (0, l)),
                  pl.BlockSpec((1, lc), lambda i, l: (0, l))],
        out_specs=pl.BlockSpec((tm, lc), lambda i, l: (i, l)),
        out_shape=jax.ShapeDtypeStruct((T, D), F32),
        scratch_shapes=[pltpu.VMEM((tm + CONV_HALO, lc), F32),
                        pltpu.VMEM((SUBLANES - 1, tm + CONV_HALO, lc), F32)],
        compiler_params=_params(("arbitrary", "arbitrary")),
    )(u, u, w32, b)


def _ln_parts(y, g, b):
    mu = jnp.mean(y, axis=-1, keepdims=True)
    yc = y - mu
    rstd = lax.rsqrt(jnp.mean(yc * yc, axis=-1, keepdims=True) + LN_EPS)
    yhat = yc * rstd
    return yhat, rstd, yhat * g + b


def _conv_out_fwd(y, x, ln_g, ln_b, w2, b2, g_post, tm):
    T, D = x.shape

    def body(y_ref, x_ref, lg_ref, lb_ref, w_ref, b_ref, g_ref, m_ref, xo_ref):
        _, _, yn = _ln_parts(y_ref[...], lg_ref[...], lb_ref[...])
        z = (yn * _sigmoid(yn)).astype(BF16)
        m = _dot(z, w_ref[...]) + b_ref[...]
        m_ref[...] = m
        xo_ref[...] = x_ref[...] + _rms(m, g_ref[...])

    return _pcall(
        body, name="conv_out_fwd", grid=(T // tm,),
        in_specs=[_rows(tm, D), _rows(tm, D), _full((1, D)), _full((1, D)), _full((D, D)), _full((1, D)),
                  _full((1, D))],
        out_specs=[_rows(tm, D), _rows(tm, D)],
        out_shape=[jax.ShapeDtypeStruct((T, D), F32), jax.ShapeDtypeStruct((T, D), F32)],
        compiler_params=_params(("arbitrary",)),
    )(y, x, ln_g, ln_b, w2, b2, g_post)


def _conv_out_bwd(dxo, m, y, ln_g, ln_b, w2, g_post, tm, dep):
    T, D = m.shape

    def body(dxo_ref, m_ref, y_ref, lg_ref, lb_ref, w_ref, g_ref, dep_ref, dy_ref, dm_ref, z_ref, sums_ref):
        i = pl.program_id(0)
        dm, dgpost = _rms_bwd(m_ref[...], g_ref[...], dxo_ref[...])
        dmb = dm.astype(BF16)
        dm_ref[...] = dmb
        yhat, rstd, yn = _ln_parts(y_ref[...], lg_ref[...], lb_ref[...])
        sg = _sigmoid(yn)
        z_ref[...] = (yn * sg).astype(BF16)
        dz = _dot_nt(dmb, w_ref[...])
        dyn = dz * (sg + yn * sg * (1.0 - sg))
        dyh = dyn * lg_ref[...]
        dy = rstd * (dyh - jnp.mean(dyh, axis=-1, keepdims=True)
                     - yhat * jnp.mean(dyh * yhat, axis=-1, keepdims=True))
        dy_ref[...] = dy

        @pl.when(i == 0)
        def _():
            sums_ref[...] = jnp.zeros_like(sums_ref)
        sums_ref[0:1, :] += dgpost
        sums_ref[1:2, :] += jnp.sum(dyn * yhat, axis=0, keepdims=True)
        sums_ref[2:3, :] += jnp.sum(dyn, axis=0, keepdims=True)
        sums_ref[3:4, :] += jnp.sum(dm, axis=0, keepdims=True)
        sums_ref[4:5, :] += jnp.sum(dy, axis=0, keepdims=True)

    return _pcall(
        body, name="conv_out_bwd", grid=(T // tm,),
        in_specs=[_rows(tm, D), _rows(tm, D), _rows(tm, D), _full((1, D)), _full((1, D)), _full((D, D)),
                  _full((1, D)), pl.BlockSpec(memory_space=pl.ANY)],
        out_specs=[_rows(tm, D), _rows(tm, D), _rows(tm, D), _acc((SUBLANES, D))],
        out_shape=[jax.ShapeDtypeStruct((T, D), F32), jax.ShapeDtypeStruct((T, D), BF16),
                   jax.ShapeDtypeStruct((T, D), BF16), jax.ShapeDtypeStruct((SUBLANES, D), F32)],
        compiler_params=_params(("arbitrary",)),
    )(dxo, m, y, ln_g, ln_b, w2, g_post, dep)


def _dwconv_bwd(dy, u, w32, tm, lc, width):
    T, D = u.shape
    hb = tm // CONV_HALO
    nt = T // tm
    last_halo = T // CONV_HALO - 1

    def body(dy_ref, dyn_ref, u_ref, up_ref, w_ref, du_ref, dw_ref, exty_ref, extu_ref, acc_ref, shy_ref, shu_ref):
        i = pl.program_id(1)
        exty_ref[0:tm, :] = dy_ref[...]
        exty_ref[tm:, :] = jnp.where(i < nt - 1, dyn_ref[...], 0.0)
        extu_ref[0:CONV_HALO, :] = jnp.where(i > 0, up_ref[...], 0.0)
        extu_ref[CONV_HALO:, :] = u_ref[...]
        _shifted_copies(exty_ref, shy_ref, tm)
        _shifted_copies(extu_ref, shu_ref, tm)

        @pl.when(i == 0)
        def _():
            acc_ref[...] = jnp.zeros_like(acc_ref)

        for r0 in range(0, tm, CONV_ROWS):
            for l0 in range(0, lc, LANES):
                ls = slice(l0, l0 + LANES)
                dyc = exty_ref[r0:r0 + CONV_ROWS, ls]
                du = jnp.zeros((CONV_ROWS, LANES), F32)
                for j in range(width):
                    du = du + w_ref[j:j + 1, ls] * _shifted_rows(exty_ref, shy_ref, (width - 1) - j, r0, ls)
                    prod = dyc * _shifted_rows(extu_ref, shu_ref, CONV_HALO - (width - 1) + j, r0, ls)
                    acc_ref[j, :, ls] += prod.reshape(CONV_ROWS // SUBLANES, SUBLANES, LANES).sum(axis=0)
                du_ref[r0:r0 + CONV_ROWS, ls] = du

        @pl.when(i == nt - 1)
        def _():
            for j in range(32):
                dw_ref[j:j + 1, :] = jnp.sum(acc_ref[j], axis=0, keepdims=True)

    return _pcall(
        body, name="dwconv_bwd", grid=(D // lc, nt),
        in_specs=[pl.BlockSpec((tm, lc), lambda l, i: (i, l)),
                  pl.BlockSpec((CONV_HALO, lc), lambda l, i: (jnp.minimum((i + 1) * hb, last_halo), l)),
                  pl.BlockSpec((tm, lc), lambda l, i: (i, l)),
                  pl.BlockSpec((CONV_HALO, lc), lambda l, i: (jnp.maximum(i * hb - 1, 0), l)),
                  pl.BlockSpec((32, lc), lambda l, i: (0, l))],
        out_specs=[pl.BlockSpec((tm, lc), lambda l, i: (i, l)),
                   pl.BlockSpec((32, lc), lambda l, i: (0, l))],
        out_shape=[jax.ShapeDtypeStruct((T, D), F32), jax.ShapeDtypeStruct((32, D), F32)],
        scratch_shapes=[pltpu.VMEM((tm + CONV_HALO, lc), F32), pltpu.VMEM((tm + CONV_HALO, lc), F32),
                        pltpu.VMEM((32, SUBLANES, lc), F32),
                        pltpu.VMEM((SUBLANES - 1, tm + CONV_HALO, lc), F32),
                        pltpu.VMEM((SUBLANES - 1, tm + CONV_HALO, lc), F32)],
        compiler_params=_params(("arbitrary", "arbitrary")),
    )(dy, dy, u, u, w32)


def _conv_in_bwd(dxo, du, a, x, g_pre, w1g, tm, dep):
    T, D = x.shape
    ns, _, cs = w1g.shape
    half = ns // 2

    def body(dxo_ref, du_ref, a_ref, x_ref, g_ref, w_ref, dep_ref, dxi_ref, h_ref, da_ref, sums_ref, db_ref):
        i = pl.program_id(0)
        xv = x_ref[...]
        h_ref[...] = _rms(xv, g_ref[...]).astype(BF16)
        dh = jnp.zeros((tm, D), F32)
        dbs = [None] * ns
        for s in range(half):
            a_u = a_ref[:, s * cs:(s + 1) * cs]
            sg = _sigmoid(a_ref[:, (s + half) * cs:(s + half + 1) * cs])
            du_s = du_ref[:, s * cs:(s + 1) * cs]
            da_u = du_s * sg
            da_g = du_s * a_u * sg * (1.0 - sg)
            for s2, v in ((s, da_u), (s + half, da_g)):
                vb = v.astype(BF16)
                da_ref[:, s2 * cs:(s2 + 1) * cs] = vb
                dbs[s2] = jnp.sum(v, axis=0, keepdims=True)
                dh = dh + _dot_nt(vb, w_ref[s2])
        dxi, dgpre = _rms_bwd(xv, g_ref[...], dh)
        dxi_ref[...] = dxo_ref[...] + dxi

        @pl.when(i == 0)
        def _():
            sums_ref[...] = jnp.zeros_like(sums_ref)
            db_ref[...] = jnp.zeros_like(db_ref)
        sums_ref[0:1, :] += dgpre
        for s in range(ns):
            db_ref[:, s * cs:(s + 1) * cs] += dbs[s]

    return _pcall(
        body, name="conv_in_bwd", grid=(T // tm,),
        in_specs=[_rows(tm, D), _rows(tm, D), _rows(tm, 2 * D), _rows(tm, D), _full((1, D)),
                  _full(w1g.shape), pl.BlockSpec(memory_space=pl.ANY)],
        out_specs=[_rows(tm, D), _rows(tm, D), _rows(tm, 2 * D), _acc((SUBLANES, D)), _acc((1, 2 * D))],
        out_shape=[jax.ShapeDtypeStruct((T, D), F32), jax.ShapeDtypeStruct((T, D), BF16),
                   jax.ShapeDtypeStruct((T, 2 * D), BF16), jax.ShapeDtypeStruct((SUBLANES, D), F32),
                   jax.ShapeDtypeStruct((1, 2 * D), F32)],
        compiler_params=_params(("arbitrary",)),
    )(dxo, du, a, x, g_pre, w1g, dep)


def _mlp_fwd(x, g_pre, wug, wdg, g_post, tm, name):
    T, D = x.shape
    ns, _, fs = wug.shape

    def body(x_ref, gp_ref, wu_ref, wd_ref, gq_ref, up_ref, m_ref, xo_ref):
        xv = x_ref[...]
        h = _rms(xv, gp_ref[...]).astype(BF16)
        acc = jnp.zeros((tm, D), F32)
        for s in range(ns):
            up = _dot(h, wu_ref[s]).astype(BF16)
            up_ref[:, s * fs:(s + 1) * fs] = up
            act = jnp.square(jnp.maximum(up.astype(F32), 0.0)).astype(BF16)
            acc = acc + _dot(act, wd_ref[s])
        m_ref[...] = acc
        xo_ref[...] = xv + _rms(acc, gq_ref[...])

    return _pcall(
        body, name=name, grid=(T // tm,),
        in_specs=[_rows(tm, D), _full((1, D)), _full(wug.shape), _full(wdg.shape), _full((1, D))],
        out_specs=[_rows(tm, ns * fs), _rows(tm, D), _rows(tm, D)],
        out_shape=[jax.ShapeDtypeStruct((T, ns * fs), BF16), jax.ShapeDtypeStruct((T, D), F32),
                   jax.ShapeDtypeStruct((T, D), F32)],
        compiler_params=_params(("arbitrary",)),
    )(x, g_pre, wug, wdg, g_post)


def _mlp_bwd(dxo, m, x, up, g_pre, wug, wdg, g_post, tm, name, dep, from_loss=False):
    T, D = x.shape
    ns, _, fs = wug.shape

    def body(dxo_ref, m_ref, x_ref, up_ref, gp_ref, wu_ref, wd_ref, gq_ref, dep_ref,
             dxi_ref, h_ref, dm_ref, dup_ref, sums_ref):
        i = pl.program_id(0)
        if from_loss:
            err = dxo_ref[...] - dep_ref[...]
            dxo = err * (1.0 / D)
        else:
            dxo = dxo_ref[...]
        dm, dgpost = _rms_bwd(m_ref[...], gq_ref[...], dxo)
        dmb = dm.astype(BF16)
        dm_ref[...] = dmb
        xv = x_ref[...]
        h_ref[...] = _rms(xv, gp_ref[...]).astype(BF16)
        dh = jnp.zeros((tm, D), F32)
        for s in range(ns):
            dact = _dot_nt(dmb, wd_ref[s])
            up = up_ref[:, s * fs:(s + 1) * fs].astype(F32)
            dup = (dact * (2.0 * jnp.maximum(up, 0.0))).astype(BF16)
            dup_ref[:, s * fs:(s + 1) * fs] = dup
            dh = dh + _dot_nt(dup, wu_ref[s])
        dxi, dgpre = _rms_bwd(xv, gp_ref[...], dh)
        dxi_ref[...] = dxo + dxi

        @pl.when(i == 0)
        def _():
            sums_ref[...] = jnp.zeros_like(sums_ref)
        sums_ref[0:1, :] += dgpost
        sums_ref[1:2, :] += dgpre
        if from_loss:
            sums_ref[2:3, :] += 0.5 * jnp.sum(jnp.mean(err * err, axis=-1, keepdims=True), axis=0, keepdims=True)

    return _pcall(
        body, name=name, grid=(T // tm,),
        in_specs=[_rows(tm, D), _rows(tm, D), _rows(tm, D), _rows(tm, ns * fs), _full((1, D)),
                  _full(wug.shape), _full(wdg.shape), _full((1, D)),
                  _rows(tm, D) if from_loss else pl.BlockSpec(memory_space=pl.ANY)],
        out_specs=[_rows(tm, D), _rows(tm, D), _rows(tm, D), _rows(tm, ns * fs), _acc((SUBLANES, D))],
        out_shape=[jax.ShapeDtypeStruct((T, D), F32), jax.ShapeDtypeStruct((T, D), BF16),
                   jax.ShapeDtypeStruct((T, D), BF16), jax.ShapeDtypeStruct((T, ns * fs), BF16),
                   jax.ShapeDtypeStruct((SUBLANES, D), F32)],
        compiler_params=_params(("arbitrary",)),
    )(dxo, m, x, up, g_pre, wug, wdg, g_post, dep)


def _mm_tn(a, g, *, nj, a_cols, g_cols, a_by_j, g_by_j, tk, name, act=False):
    T = a.shape[0]
    nk = T // tk

    def body(a_ref, g_ref, o_ref, acc_ref):
        k = pl.program_id(1)
        av = a_ref[...]
        if act:
            av = jnp.square(jnp.maximum(av.astype(F32), 0.0)).astype(BF16)
        p = _dot_tn(av, g_ref[...])

        @pl.when(k == 0)
        def _():
            acc_ref[...] = p

        @pl.when(k > 0)
        def _():
            acc_ref[...] += p

        @pl.when(k == nk - 1)
        def _():
            o_ref[...] = acc_ref[...].astype(BF16)

    return _pcall(
        body, name=name, grid=(nj, nk),
        in_specs=[pl.BlockSpec((tk, a_cols), (lambda j, k: (k, j)) if a_by_j else (lambda j, k: (k, 0))),
                  pl.BlockSpec((tk, g_cols), (lambda j, k: (k, j)) if g_by_j else (lambda j, k: (k, 0)))],
        out_specs=pl.BlockSpec((None, a_cols, g_cols), lambda j, k: (j, 0, 0)),
        out_shape=jax.ShapeDtypeStruct((nj, a_cols, g_cols), BF16),
        scratch_shapes=[pltpu.VMEM((a_cols, g_cols), F32)],
        compiler_params=_params(("arbitrary", "arbitrary")),
    )(a, g)


def _attn_in_fwd(x, g_pre, wqkv, wf, bf, tm, q_mul, n_heads):
    T, D = x.shape

    def body(x_ref, g_ref, w_ref, wf_ref, bf_ref, q_ref, k_ref, v_ref, lf_ref):
        h = _rms(x_ref[...], g_ref[...]).astype(BF16)
        q = _dot(h, w_ref[:, 0:D])
        if q_mul != 1.0:
            q = q * q_mul
        q_ref[...] = q.astype(BF16)
        k_ref[...] = _dot(h, w_ref[:, D:2 * D]).astype(BF16)
        v_ref[...] = _dot(h, w_ref[:, 2 * D:3 * D]).astype(BF16)
        fl = _dot(h, wf_ref[...]) + bf_ref[...]
        lf = jnp.minimum(fl, 0.0) - jnp.log(1.0 + jnp.exp(-jnp.abs(fl)))
        lane = lax.broadcasted_iota(jnp.int32, (1, LANES), 1)
        lf_ref[...] = jnp.where(lane < n_heads, lf, 0.0)

    return _pcall(
        body, name="attn_in_fwd", grid=(T // tm,),
        in_specs=[_rows(tm, D), _full((1, D)), _full((D, 3 * D)), _full((D, LANES)), _full((1, LANES))],
        out_specs=[_rows(tm, D), _rows(tm, D), _rows(tm, D), _rows(tm, LANES)],
        out_shape=[jax.ShapeDtypeStruct((T, D), BF16)] * 3 + [jax.ShapeDtypeStruct((T, LANES), F32)],
        compiler_params=_params(("arbitrary",)),
    )(x, g_pre, wqkv, wf, bf)


def _cumsum_rows(v, v2, tb, reverse, name):
    T, C = v.shape
    nb = T // tb

    def body(v_ref, v2_ref, o_ref, carry_ref):
        i = pl.program_id(0)

        @pl.when(i == 0)
        def _():
            carry_ref[...] = jnp.zeros_like(carry_ref)
        r = lax.broadcasted_iota(jnp.int32, (tb, tb), 0)
        c = lax.broadcasted_iota(jnp.int32, (tb, tb), 1)
        tri = jnp.where((c >= r) if reverse else (c <= r), 1.0, 0.0).astype(F32)
        out = jnp.dot(tri, v_ref[...] + v2_ref[...], precision=lax.Precision.HIGHEST,
                      preferred_element_type=F32) + carry_ref[...]
        o_ref[...] = out
        carry_ref[...] = out[0:1, :] if reverse else out[tb - 1:tb, :]

    idx = (lambda i: (nb - 1 - i, 0)) if reverse else (lambda i: (i, 0))
    return _pcall(
        body, name=name, grid=(nb,),
        in_specs=[pl.BlockSpec((tb, C), idx), pl.BlockSpec((tb, C), idx)],
        out_specs=pl.BlockSpec((tb, C), idx),
        out_shape=jax.ShapeDtypeStruct((T, C), F32),
        scratch_shapes=[pltpu.VMEM((1, C), F32)],
        compiler_params=_params(("arbitrary",)),
    )(v, v2)


def _head_col(v, lane, h):
    return jnp.sum(jnp.where(lane == h, v, 0.0), axis=1, keepdims=True)


def _flash_fwd(q, k, v, fq_aux, fk_rows, *, dh, tq, s_mul):
    T, D = q.shape
    G = D // LANES
    hpg = LANES // dh
    nq = T // tq
    k3 = k.reshape(nq, tq, D)
    v3 = v.reshape(nq, tq, D)

    rc = min(FLASH_ROWS, tq)

    def body(q_ref, k_ref, v_ref, fq_ref, fk_ref, o_ref, o32_ref, lse_ref, s_scr, p_scr):
        i = pl.program_id(1)
        lane = lax.broadcasted_iota(jnp.int32, (1, LANES), 1)
        q2 = q_ref[...]
        hmasks = [(lane >= hh * dh) & (lane < (hh + 1) * dh) for hh in range(hpg)]
        qms = [jnp.where(hm, q2, jnp.zeros_like(q2)) for hm in hmasks]

        nlb = tq // LANES
        sum_lane = [((hh + 1) % hpg) * dh for hh in range(hpg)]

        def scores(j, slot):
            kj = k_ref[j]
            for hh in range(hpg):
                s = _dot_nt(qms[hh], kj)
                s_scr[slot, hh] = s if s_mul == 1.0 else s * s_mul

        def soft(j, slot, carry, masked):
            vj = v_ref[j]
            out = []
            for hh in range(hpg):
                m_b, acc = carry[hh]
                fk_row = fk_ref[j, hh:hh + 1, :]
                def future(r0, cb):
                    return masked and cb * LANES > r0 + rc - 1

                mx = []
                for r0 in range(0, tq, rc):
                    rs = slice(r0, r0 + rc)
                    c = None
                    for cb in range(nlb):
                        if future(r0, cb):
                            continue
                        cs = slice(cb * LANES, (cb + 1) * LANES)
                        s = s_scr[slot, hh, rs, cs] - fk_row[:, cs]
                        if masked and (cb + 1) * LANES - 1 > r0:
                            ri = r0 + lax.broadcasted_iota(jnp.int32, (rc, LANES), 0)
                            ci = cb * LANES + lax.broadcasted_iota(jnp.int32, (rc, LANES), 1)
                            s = jnp.where(ci <= ri, s, MASK_VALUE)
                        s_scr[slot, hh, rs, cs] = s
                        c = s if c is None else jnp.maximum(c, s)
                    mx.append(c)
                row_max = jnp.max(jnp.concatenate(mx, axis=0), axis=1, keepdims=True)
                m_new = jnp.maximum(m_b, row_max)
                alpha = jnp.exp(m_b - m_new)
                for r0 in range(0, tq, rc):
                    rs = slice(r0, r0 + rc)
                    m_c = m_new[rs]
                    for cb in range(nlb):
                        cs = slice(cb * LANES, (cb + 1) * LANES)
                        if future(r0, cb):
                            p_scr[slot, hh, rs, cs] = jnp.zeros((rc, LANES), BF16)
                        else:
                            p_scr[slot, hh, rs, cs] = jnp.exp(s_scr[slot, hh, rs, cs] - m_c).astype(BF16)
                v_one = jnp.where(hmasks[hh], vj, jnp.ones_like(vj))
                out.append((m_new, alpha * acc + _dot(p_scr[slot, hh], v_one)))
            return tuple(out)

        def step(j, carry, masked, slot):
            scores(j, slot)
            return soft(j, slot, carry, masked)

        def pair(t, cr):
            scores(2 * t, 0)
            scores(2 * t + 1, 1)
            return soft(2 * t + 1, 1, soft(2 * t, 0, cr, False), False)

        init = tuple((jnp.full((tq, LANES), MASK_VALUE, F32), jnp.zeros((tq, LANES), F32)) for _ in range(hpg))
        carry = lax.fori_loop(0, i // 2, pair, init)
        carry = lax.cond(i % 2 == 1, lambda cr: step(i - 1, cr, False, 0), lambda cr: cr, carry)
        carry = step(i, carry, True, 1)
        fq = fq_ref[...]
        g0 = pl.program_id(0) * hpg
        o_all = jnp.zeros((tq, LANES), F32)
        lse_all = jnp.zeros((tq, LANES), F32)
        for hh in range(hpg):
            m_b, acc = carry[hh]
            l = acc[:, sum_lane[hh]:sum_lane[hh] + 1]
            o_all = jnp.where(hmasks[hh], acc * (1.0 / l), o_all)
            lse_all = jnp.where(lane == hh, m_b[:, 0:1] + jnp.log(l) + _head_col(fq, lane, g0 + hh), lse_all)
        o_ref[...] = o_all.astype(BF16)
        o32_ref[...] = o_all
        lse_ref[...] = lse_all

    return _pcall(
        body, name="flash_fwd", grid=(G, nq),
        in_specs=[pl.BlockSpec((tq, LANES), lambda g, i: (i, g)),
                  pl.BlockSpec((nq, tq, LANES), lambda g, i: (0, 0, g)),
                  pl.BlockSpec((nq, tq, LANES), lambda g, i: (0, 0, g)),
                  pl.BlockSpec((tq, LANES), lambda g, i: (i, 0)),
                  pl.BlockSpec((None, nq, SUBLANES, tq), lambda g, i: (g, 0, 0, 0))],
        out_specs=[pl.BlockSpec((tq, LANES), lambda g, i: (i, g)),
                   pl.BlockSpec((tq, LANES), lambda g, i: (i, g)),
                   pl.BlockSpec((None, tq, LANES), lambda g, i: (g, i, 0))],
        out_shape=[jax.ShapeDtypeStruct((T, D), BF16), jax.ShapeDtypeStruct((T, D), F32),
                   jax.ShapeDtypeStruct((G, T, LANES), F32)],
        scratch_shapes=[pltpu.VMEM((2, hpg, tq, tq), F32), pltpu.VMEM((2, hpg, tq, tq), BF16)],
        compiler_params=_params(("arbitrary", "arbitrary")),
    )(q, k3, v3, fq_aux, fk_rows)


def _flash_bwd(q, k, v, do, fq_aux, lse_aux, dl_aux, fk_rows, *, dh, tq, tk, s_mul, dq_mul):
    T, D = q.shape
    G = D // LANES
    hpg = LANES // dh
    nq = T // tq
    nk = T // tk
    per = tq // tk
    k3 = k.reshape(nk, tk, D)
    v3 = v.reshape(nk, tk, D)
    rc = min(FLASH_ROWS, tq)

    def body(q_ref, k_ref, v_ref, do_ref, fq_ref, lse_ref, dl_ref, fk_ref,
             dq_ref, dk_ref, dv_ref, dfq_ref, dfk_ref, dk_acc, dv_acc, s_scr, dp_scr, p_scr, ds_scr):
        i = pl.program_id(1)

        @pl.when(i == 0)
        def _():
            dk_acc[...] = jnp.zeros_like(dk_acc)
            dv_acc[...] = jnp.zeros_like(dv_acc)
            dfk_ref[...] = jnp.zeros_like(dfk_ref)

        lane = lax.broadcasted_iota(jnp.int32, (1, LANES), 1)
        q2 = q_ref[...]
        do2 = do_ref[...]
        g0 = pl.program_id(0) * hpg
        fq = fq_ref[...]
        lse = lse_ref[...]
        dl = dl_ref[...]
        hmasks = [(lane >= hh * dh) & (lane < (hh + 1) * dh) for hh in range(hpg)]
        qms = [jnp.where(hm, q2, jnp.zeros_like(q2)) for hm in hmasks]
        doms = [jnp.where(hm, do2, jnp.zeros_like(do2)) for hm in hmasks]
        q2t = q2.astype(F32).T.astype(BF16)
        do2t = do2.astype(F32).T.astype(BF16)
        ones_rows = jnp.ones((2 * SUBLANES, tk), BF16)
        q1t = [jnp.concatenate([q2t[hh * dh:(hh + 1) * dh], jnp.ones((2 * SUBLANES, tq), BF16)], axis=0)
               for hh in range(hpg)]
        c_bs = [jnp.broadcast_to(_head_col(fq, lane, g0 + hh) - lse[:, hh:hh + 1], (tq, LANES))
                for hh in range(hpg)]
        dl_bs = [jnp.broadcast_to(_head_col(dl, lane, g0 + hh), (tq, LANES)) for hh in range(hpg)]
        nlb = tk // LANES

        def step(j, carry, off, slot=0):
            masked = off is not None
            kj = k_ref[j]
            vj = v_ref[j]
            kjt = kj.astype(F32).T.astype(BF16)
            for hh in range(hpg):
                s = _dot_nt(qms[hh], kj)
                s_scr[slot, hh] = s if s_mul == 1.0 else s * s_mul
                dp_scr[slot, hh] = _dot_nt(doms[hh], vj)
            out = []
            for hh in range(hpg):
                fk_row = fk_ref[j, hh:hh + 1, :]
                for r0 in range(0, tq, rc):
                    rs = slice(r0, r0 + rc)
                    if masked and r0 + rc <= off:
                        p_scr[slot, hh, rs, :] = jnp.zeros((rc, tk), BF16)
                        ds_scr[slot, hh, rs, :] = jnp.zeros((rc, tk), BF16)
                        continue
                    c_c = c_bs[hh][rs]
                    dl_c = dl_bs[hh][rs]
                    for cb in range(nlb):
                        cs = slice(cb * LANES, (cb + 1) * LANES)
                        if masked and off + cb * LANES > r0 + rc - 1:
                            p_scr[slot, hh, rs, cs] = jnp.zeros((rc, LANES), BF16)
                            ds_scr[slot, hh, rs, cs] = jnp.zeros((rc, LANES), BF16)
                            continue
                        e = (s_scr[slot, hh, rs, cs] - fk_row[:, cs]) + c_c
                        if masked and off + (cb + 1) * LANES - 1 > r0:
                            ri = r0 + lax.broadcasted_iota(jnp.int32, (rc, LANES), 0)
                            ci = off + cb * LANES + lax.broadcasted_iota(jnp.int32, (rc, LANES), 1)
                            e = jnp.where(ci <= ri, e, MASK_VALUE)
                        p = jnp.exp(e)
                        ds = p * (dp_scr[slot, hh, rs, cs] - dl_c)
                        p_scr[slot, hh, rs, cs] = p.astype(BF16)
                        ds_scr[slot, hh, rs, cs] = ds.astype(BF16)
                k1t = jnp.concatenate([kjt[hh * dh:(hh + 1) * dh], ones_rows], axis=0)
                out.append(carry[hh] + _dot_nt(k1t, ds_scr[slot, hh]))
            dks = [_dot(q1t[hh], ds_scr[slot, hh]) for hh in range(hpg)]
            for hh in range(hpg):
                dfk_ref[j, hh:hh + 1, :] += -dks[hh][dh:dh + 1, :]
            dk_acc[j] += jnp.concatenate([d[0:dh] for d in dks], axis=0)
            dv_acc[j] += jnp.concatenate(
                [_dot(do2t[hh * dh:(hh + 1) * dh], p_scr[slot, hh]) for hh in range(hpg)], axis=0)
            return tuple(out)

        init = tuple(jnp.zeros((dh + 2 * SUBLANES, tq), F32) for _ in range(hpg))
        n_past = i * per

        def pair(t, cr):
            return step(2 * t + 1, step(2 * t, cr, None, 0), None, 1)

        carry = lax.fori_loop(0, n_past // 2, pair, init)
        carry = lax.cond(n_past % 2 == 1, lambda cr: step(n_past - 1, cr, None, 0), lambda cr: cr, carry)
        for d in range(per):
            carry = step(i * per + d, carry, d * tk, (d + 1) % 2)
        dq_all = jnp.concatenate([carry[hh][0:dh] for hh in range(hpg)], axis=0).T
        dq_ref[...] = (dq_all * dq_mul).astype(BF16)
        dfq_ref[...] = jnp.concatenate([carry[hh][dh:dh + 1, :] for hh in range(hpg)]
                                       + [jnp.zeros((SUBLANES - hpg, tq), F32)], axis=0)

        @pl.when(i == nq - 1)
        def _():
            for jj in range(nk):
                dkv = dk_acc[jj].T
                if s_mul != 1.0:
                    dkv = dkv * s_mul
                dk_ref[jj] = dkv.astype(BF16)
                dv_ref[jj] = dv_acc[jj].T.astype(BF16)

    blk = pl.BlockSpec((tq, LANES), lambda g, i: (i, g))
    res = pl.BlockSpec((nk, tk, LANES), lambda g, i: (0, 0, g))
    res_in = pl.BlockSpec((nk, tk, LANES), lambda g, i: (0, 0, g), pipeline_mode=pl.Buffered(1))
    aux = pl.BlockSpec((None, tq, LANES), lambda g, i: (g, i, 0))
    heads = pl.BlockSpec((tq, LANES), lambda g, i: (i, 0))
    rows = pl.BlockSpec((None, nk, SUBLANES, tk), lambda g, i: (g, 0, 0, 0))
    dq, dk3, dv3, dfq, dfk = _pcall(
        body, name="flash_bwd", grid=(G, nq),
        in_specs=[blk, res_in, res_in, blk, heads, aux, heads, rows],
        out_specs=[blk, res, res, pl.BlockSpec((None, None, SUBLANES, tq), lambda g, i: (g, i, 0, 0)), rows],
        out_shape=[jax.ShapeDtypeStruct((T, D), BF16), jax.ShapeDtypeStruct((nk, tk, D), BF16),
                   jax.ShapeDtypeStruct((nk, tk, D), BF16), jax.ShapeDtypeStruct((G, nq, SUBLANES, tq), F32),
                   jax.ShapeDtypeStruct((G, nk, SUBLANES, tk), F32)],
        scratch_shapes=[pltpu.VMEM((nk, LANES, tk), F32), pltpu.VMEM((nk, LANES, tk), F32),
                        pltpu.VMEM((2, hpg, tq, tk), F32), pltpu.VMEM((2, hpg, tq, tk), F32),
                        pltpu.VMEM((2, hpg, tq, tk), BF16), pltpu.VMEM((2, hpg, tq, tk), BF16)],
        compiler_params=_params(("arbitrary", "arbitrary")),
    )(q, k3, v3, do, fq_aux, lse_aux, dl_aux, fk_rows)
    return dq, dk3.reshape(T, D), dv3.reshape(T, D), dfq, dfk


def _attn_out_fwd(o, x, wo, g_post, tm):
    T, D = x.shape

    def body(o_ref, x_ref, w_ref, g_ref, m_ref, xo_ref):
        m = _dot(o_ref[...], w_ref[...])
        m_ref[...] = m
        xo_ref[...] = x_ref[...] + _rms(m, g_ref[...])

    return _pcall(
        body, name="attn_out_fwd", grid=(T // tm,),
        in_specs=[_rows(tm, D), _rows(tm, D), _full((D, D)), _full((1, D))],
        out_specs=[_rows(tm, D), _rows(tm, D)],
        out_shape=[jax.ShapeDtypeStruct((T, D), F32), jax.ShapeDtypeStruct((T, D), F32)],
        compiler_params=_params(("arbitrary",)),
    )(o, x, wo, g_post)


def _attn_out_bwd(dxo, m, o, wo, g_post, head_ind, tm, dep):
    T, D = m.shape

    def body(dxo_ref, m_ref, o_ref, w_ref, g_ref, ind_ref, dep_ref, dm_ref, do_ref, dl_ref, sums_ref):
        i = pl.program_id(0)
        dm, dgpost = _rms_bwd(m_ref[...], g_ref[...], dxo_ref[...])
        dmb = dm.astype(BF16)
        dm_ref[...] = dmb
        dob = _dot_nt(dmb, w_ref[...]).astype(BF16)
        do_ref[...] = dob
        dl_ref[...] = jnp.dot(dob.astype(F32) * o_ref[...], ind_ref[...], precision=lax.Precision.HIGHEST,
                              preferred_element_type=F32)

        @pl.when(i == 0)
        def _():
            sums_ref[...] = jnp.zeros_like(sums_ref)
        sums_ref[0:1, :] += dgpost

    return _pcall(
        body, name="attn_out_bwd", grid=(T // tm,),
        in_specs=[_rows(tm, D), _rows(tm, D), _rows(tm, D), _full((D, D)), _full((1, D)), _full((D, LANES)),
                  pl.BlockSpec(memory_space=pl.ANY)],
        out_specs=[_rows(tm, D), _rows(tm, D), _rows(tm, LANES), _acc((SUBLANES, D))],
        out_shape=[jax.ShapeDtypeStruct((T, D), BF16), jax.ShapeDtypeStruct((T, D), BF16),
                   jax.ShapeDtypeStruct((T, LANES), F32), jax.ShapeDtypeStruct((SUBLANES, D), F32)],
        compiler_params=_params(("arbitrary",)),
    )(dxo, m, o, wo, g_post, head_ind, dep)


def _attn_in_bwd(dxo, x, g_pre, dq, dk, dv, dlf, lf, wqkv, wf, tm, n_heads):
    T, D = x.shape

    def body(dxo_ref, x_ref, g_ref, dq_ref, dk_ref, dv_ref, dlf_ref, lf_ref, w_ref, wf_ref,
             dxi_ref, h_ref, df_ref, sums_ref, dbf_ref):
        i = pl.program_id(0)
        xv = x_ref[...]
        h_ref[...] = _rms(xv, g_ref[...]).astype(BF16)
        lane = lax.broadcasted_iota(jnp.int32, (1, LANES), 1)
        df = jnp.where(lane < n_heads, dlf_ref[...] * (1.0 - jnp.exp(lf_ref[...])), 0.0)
        dfb = df.astype(BF16)
        df_ref[...] = dfb
        dh = (_dot_nt(dq_ref[...], w_ref[:, 0:D]) + _dot_nt(dk_ref[...], w_ref[:, D:2 * D])
              + _dot_nt(dv_ref[...], w_ref[:, 2 * D:3 * D]) + _dot_nt(dfb, wf_ref[...]))
        dxi, dgpre = _rms_bwd(xv, g_ref[...], dh)
        dxi_ref[...] = dxo_ref[...] + dxi

        @pl.when(i == 0)
        def _():
            sums_ref[...] = jnp.zeros_like(sums_ref)
            dbf_ref[...] = jnp.zeros_like(dbf_ref)
        sums_ref[0:1, :] += dgpre
        dbf_ref[...] += jnp.sum(df, axis=0, keepdims=True)

    return _pcall(
        body, name="attn_in_bwd", grid=(T // tm,),
        in_specs=[_rows(tm, D), _rows(tm, D), _full((1, D)), _rows(tm, D), _rows(tm, D), _rows(tm, D),
                  _rows(tm, LANES), _rows(tm, LANES), _full((D, 3 * D)), _full((D, LANES))],
        out_specs=[_rows(tm, D), _rows(tm, D), _rows(tm, LANES), _acc((SUBLANES, D)), _acc((1, LANES))],
        out_shape=[jax.ShapeDtypeStruct((T, D), F32), jax.ShapeDtypeStruct((T, D), BF16),
                   jax.ShapeDtypeStruct((T, LANES), BF16), jax.ShapeDtypeStruct((SUBLANES, D), F32),
                   jax.ShapeDtypeStruct((1, LANES), F32)],
        compiler_params=_params(("arbitrary",)),
    )(dxo, x, g_pre, dq, dk, dv, dlf, lf, wqkv, wf)


def _adamw(recvs, w, m, v, tr, name):
    L, R, C = w.shape
    assert len(recvs) == L
    c1 = 1.0 - ADAM_B1 ** ADAM_STEP
    c2 = 1.0 - ADAM_B2 ** ADAM_STEP

    def body(*refs):
        r_refs = refs[:L]
        w_ref, m_ref, v_ref, g_ref, d_ref, nm_ref, nv_ref = refs[L:]
        layer = pl.program_id(0)
        g = None
        for k in range(L):
            gk = r_refs[k][0, :, 0:C].astype(F32)
            for s in range(1, N_DEV):
                gk = gk + r_refs[k][s, :, 0:C].astype(F32)
            g = gk if g is None else jnp.where(layer == k, gk, g)
        nm = ADAM_B1 * m_ref[...] + (1.0 - ADAM_B1) * g
        nv = ADAM_B2 * v_ref[...] + (1.0 - ADAM_B2) * jnp.square(g)
        m_hat = nm / c1
        v_hat = nv / c2
        g_ref[...] = g
        d_ref[...] = -ADAM_LR * (m_hat / (jnp.sqrt(v_hat) + ADAM_EPS) + ADAM_WD * w_ref[...])
        nm_ref[...] = nm
        nv_ref[...] = nv

    def recv_spec(k):
        return pl.BlockSpec((N_DEV, tr, recvs[k].shape[-1]), lambda l, i: (0, jnp.where(l == k, i, 0), 0))

    blk = pl.BlockSpec((None, tr, C), lambda l, i: (l, i, 0))
    return _pcall(
        body, name=name, grid=(L, R // tr),
        in_specs=[recv_spec(k) for k in range(L)] + [blk] * 3,
        out_specs=[blk] * 4,
        out_shape=[jax.ShapeDtypeStruct((L, R, C), F32)] * 4,
        compiler_params=_params(("arbitrary", "arbitrary")),
    )(*recvs, w, m, v)


def _row_block(rows, cols):
    cap = max(SUBLANES, (256 * 1024) // max(cols, 1))
    best = None
    for t in range(SUBLANES, rows + 1, SUBLANES):
        if rows % t == 0 and t <= cap:
            best = t
    return rows if best is None else best


def kernel(x, g_mix_pre, g_mix_post, g_ffn_pre, g_ffn_post, conv_pw1_w, conv_pw1_b, conv_dw_w, conv_dw_b, conv_ln_g, conv_ln_b, conv_pw2_w, conv_pw2_b, attn_w_in, attn_b_f, attn_w_o, mlp_w_up, mlp_w_down, loss_target, m_g_mix_pre, m_g_mix_post, m_g_ffn_pre, m_g_ffn_post, m_conv_pw1_w, m_conv_pw1_b, m_conv_dw_w, m_conv_dw_b, m_conv_ln_g, m_conv_ln_b, m_conv_pw2_w, m_conv_pw2_b, m_attn_w_in, m_attn_b_f, m_attn_w_o, m_mlp_w_up, m_mlp_w_down, v_g_mix_pre, v_g_mix_post, v_g_ffn_pre, v_g_ffn_post, v_conv_pw1_w, v_conv_pw1_b, v_conv_dw_w, v_conv_dw_b, v_conv_ln_g, v_conv_ln_b, v_conv_pw2_w, v_conv_pw2_b, v_attn_w_in, v_attn_b_f, v_attn_w_o, v_mlp_w_up, v_mlp_w_down):
    _, T, D = x.shape
    H = attn_b_f.shape[-1]
    dh = D // H
    width = conv_dw_w.shape[1]
    cin = attn_w_in.shape[-1]
    fs = mlp_w_up.shape[-1]
    G = D // LANES
    hpg = LANES // dh
    assert T % 4 == 0 and D % LANES == 0 and LANES % dh == 0 and width <= CONV_HALO and H <= LANES

    tm = min(512, T // 4)
    tmb = min(256, T // 4)
    tqf = min(1024, T // 4)
    tkb = tm
    tmc = min(256, T // 4)
    lc = min(256, D)
    tkw = min(4096, T // 2)
    tb = min(256, T // 4)

    scale = float(dh) ** -0.5
    mant, _ = math.frexp(scale)
    q_mul = scale if mant == 0.5 else 1.0
    s_mul = 1.0 if mant == 0.5 else scale

    x2 = x.reshape(T, D)
    tgt = loss_target.reshape(T, D)

    w_srcs = [conv_pw1_w, conv_dw_w, conv_pw2_w, mlp_w_up, mlp_w_down, attn_w_in, attn_w_o]
    w_items = [(0, 0, "whole"), (1, 0, "whole"), (2, 0, "whole"), (3, 0, "whole"), (4, 0, "whole"),
               (5, 0, "whole"), (6, 0, "whole"), (3, 1, "whole"), (4, 1, "whole")]
    cin_w = -(-cin // LANES) * LANES
    w_lands = _place_own(
        w_srcs, w_items,
        [((N_DEV, D, cin_w) if si == 5 else (N_DEV,) + w_srcs[si].shape[1:], F32 if si == 1 else BF16)
         for si, _, _ in w_items],
        "stage_weights", cast=True)
    me_arr = _dev_index(*_mesh_pos()).astype(jnp.int32).reshape(1)
    w_groups = [[0], [1, 2], [3, 4], [5, 6], [7, 8]]
    g_sems, _, w_lands, g_token = _push_start(
        [], w_lands, [[(a, None, "own") for a in grp] for grp in w_groups], "gather_start")

    def gather_wait(gi, after):
        grp = w_groups[gi]
        return _push_wait(g_sems[gi], [], [w_lands[a] for a in grp], [(k, None, "own") for k in range(len(grp))],
                          after, "gather_wait%d" % gi)

    (w1g,) = gather_wait(0, g_token)
    bf = jnp.pad(attn_b_f, ((0, 0), (0, LANES - H)))

    row = lambda a, i: a[i:i + 1]

    a0, u0 = _conv_in_fwd(x2, row(g_mix_pre, 0), w1g, conv_pw1_b, tm)
    dwg, w2g = gather_wait(1, u0)
    w2 = w2g.reshape(D, D)
    dw_full = jnp.transpose(dwg, (1, 0, 2)).reshape(width, D)
    w32 = jnp.pad(dw_full, ((0, 32 - width), (0, 0)))
    y0 = _dwconv_fwd(u0, w32, conv_dw_b, tmc, lc, width)
    m0, x_1 = _conv_out_fwd(y0, x2, conv_ln_g, conv_ln_b, w2, conv_pw2_b, row(g_mix_post, 0), tm)
    wu0, wd0 = gather_wait(2, x_1)
    up0, n0, x_2 = _mlp_fwd(x_1, row(g_ffn_pre, 0), wu0, wd0, row(g_ffn_post, 0), tm, "mlp0_fwd")

    wing, wog = gather_wait(3, x_2)
    wo = wog.reshape(D, D)
    win = jnp.transpose(wing[:, :, :cin], (1, 0, 2)).reshape(D, N_DEV * cin)
    wqkv = win[:, :3 * D]
    wf = jnp.pad(win[:, 3 * D:], ((0, 0), (0, LANES - H)))
    q, k, v, lf = _attn_in_fwd(x_2, row(g_mix_pre, 1), wqkv, wf, bf, tm, q_mul, H)
    fcum = _cumsum_rows(lf, jnp.zeros_like(lf), tb, False, "forget_cumsum")


    def key_rows(blk):
        r = jnp.transpose(fcum[:, :H].T.reshape(G, hpg, T // blk, blk), (0, 2, 1, 3))
        return jnp.pad(r, ((0, 0), (0, 0), (0, SUBLANES - hpg), (0, 0)))

    fk_rows = key_rows(tkb)
    o, o32, lse_aux = _flash_fwd(q, k, v, fcum, key_rows(tqf), dh=dh, tq=tqf, s_mul=s_mul)
    m1, x_3 = _attn_out_fwd(o, x_2, wo, row(g_mix_post, 1), tm)
    wu1, wd1 = gather_wait(4, x_3)
    up1, n1, x_4 = _mlp_fwd(x_3, row(g_ffn_pre, 1), wu1, wd1, row(g_ffn_post, 1), tm, "mlp1_fwd")

    def mlp_back(dx, n_l, x_in, up_l, l, wu, wd, dep, from_loss=False):
        dxi, h, dm, dup, sums = _mlp_bwd(dx, n_l, x_in, up_l, row(g_ffn_pre, l), wu, wd, row(g_ffn_post, l),
                                         tmb, "mlp%d_bwd" % l, dep, from_loss)
        dwu = _mm_tn(h, dup, nj=N_DEV, a_cols=D, g_cols=fs, a_by_j=False, g_by_j=True, tk=tkw,
                     name="mlp%d_dwu" % l)
        dwd = _mm_tn(up_l, dm, nj=N_DEV, a_cols=fs, g_cols=D, a_by_j=True, g_by_j=False, tk=tkw,
                     name="mlp%d_dwd" % l, act=True)
        return dxi, dwu, dwd, sums

    def push_grads(srcs, modes, name):
        items = [(i, None, mode) for i, mode in enumerate(modes)]
        lands = _seed_lands(srcs, modes, me_arr, name + "_own")
        sems, srcs_t, lands_t, token = _push_start(srcs, lands, [items], name + "_start")
        return (sems[0], srcs_t, lands_t, items), token

    def pull_grads(handle, after, name):
        sems, srcs_t, lands_t, items = handle
        return _push_wait(sems, srcs_t, lands_t, items, after, name + "_wait")

    rs = D // N_DEV
    dx, dwu1, dwd1, s_mlp1 = mlp_back(x_4, n1, x_3, up1, 1, wu1, wd1, tgt, from_loss=True)
    h_mlp1, tok = push_grads([dwu1, dwd1], ["slot", "slot"], "grads_mlp1")

    head_ind = jnp.asarray((np.arange(D)[:, None] // dh == np.arange(LANES)[None, :]).astype(np.float32))
    dm1, do, delta, s_ao = _attn_out_bwd(dx, m1, o32, wo, row(g_mix_post, 1), head_ind, tm, tok)
    dwo = _mm_tn(o, dm1, nj=1, a_cols=D, g_cols=D, a_by_j=False, g_by_j=False, tk=tkw, name="attn_dwo")
    dq, dk, dv, dfq, dfk = _flash_bwd(q, k, v, do, fcum, lse_aux, delta, fk_rows,
                                      dh=dh, tq=tkb, tk=tkb, s_mul=s_mul, dq_mul=scale)
    def head_cols(r):
        return jnp.pad(jnp.transpose(r[:, :, :hpg, :], (0, 2, 1, 3)).reshape(H, T).T, ((0, 0), (0, LANES - H)))

    df_k = head_cols(dfk)
    df_q = head_cols(dfq)
    dlf = _cumsum_rows(df_q, df_k, tb, True, "forget_cumsum_bwd")
    dx, h_at, df, s_ai, dbf = _attn_in_bwd(dx, x_2, row(g_mix_pre, 1), dq, dk, dv, dlf, lf, wqkv, wf, tm, H)
    dwq = _mm_tn(h_at, dq, nj=1, a_cols=D, g_cols=D, a_by_j=False, g_by_j=False, tk=tkw, name="attn_dwq")
    dwk = _mm_tn(h_at, dk, nj=1, a_cols=D, g_cols=D, a_by_j=False, g_by_j=False, tk=tkw, name="attn_dwk")
    dwv = _mm_tn(h_at, dv, nj=1, a_cols=D, g_cols=D, a_by_j=False, g_by_j=False, tk=tkw, name="attn_dwv")
    dwf = _mm_tn(h_at, df, nj=1, a_cols=D, g_cols=LANES, a_by_j=False, g_by_j=False, tk=tkw, name="attn_dwf")
    dwin = jnp.concatenate([dwq[0], dwk[0], dwv[0], dwf[0][:, :H]], axis=1)
    dwin = jnp.pad(jnp.transpose(dwin.reshape(D, N_DEV, cin), (1, 0, 2)), ((0, 0), (0, 0), (0, cin_w - cin)))
    h_attn, tok = push_grads([dwin, dwo.reshape(N_DEV, rs, D)], ["slot", "slot"], "grads_attn")

    dx, dwu0, dwd0, s_mlp0 = mlp_back(dx, n0, x_1, up0, 0, wu0, wd0, tok)
    h_mlp0, tok = push_grads([dwu0, dwd0], ["slot", "slot"], "grads_mlp0")

    dy0, dm0, z0, s_co = _conv_out_bwd(dx, m0, y0, conv_ln_g, conv_ln_b, w2, row(g_mix_post, 0), tm, tok)
    dw2 = _mm_tn(z0, dm0, nj=1, a_cols=D, g_cols=D, a_by_j=False, g_by_j=False, tk=tkw, name="conv_dw2")
    h_pw2, tok = push_grads([dw2.reshape(N_DEV, rs, D)], ["slot"], "grads_pw2")
    du0, ddw = _dwconv_bwd(dy0, u0, w32, tmc, lc, width)
    grad_x, h_cv, da0, s_ci, db1 = _conv_in_bwd(dx, du0, a0, x2, row(g_mix_pre, 0), w1g, tm, tok)
    dw1 = _mm_tn(h_cv, da0, nj=N_DEV, a_cols=D, g_cols=(2 * D) // N_DEV, a_by_j=False, g_by_j=True, tk=tkw,
                 name="conv_dw1")
    ddw_s = jnp.transpose(ddw[:width].reshape(width, N_DEV, D // N_DEV), (1, 0, 2))

    def pad_row(a):
        return jnp.pad(a, ((0, 0), (0, D - a.shape[1])))

    def pack(gmp, gmq, gfp, gfq, b1, dwb, lng, lnb, b2, bfv, last):
        return jnp.concatenate([gmp, gmq, gfp, gfq, b1.reshape(2, D), dwb, lng, lnb, b2, pad_row(bfv), last],
                               axis=0)

    zero_row = jnp.zeros((1, D), F32)
    small_g = pack(
        jnp.concatenate([row(s_ci, 0), row(s_ai, 0)], axis=0),
        jnp.concatenate([row(s_co, 0), row(s_ao, 0)], axis=0),
        jnp.concatenate([row(s_mlp0, 1), row(s_mlp1, 1)], axis=0),
        jnp.concatenate([row(s_mlp0, 0), row(s_mlp1, 0)], axis=0),
        db1, row(s_co, 4), row(s_co, 1), row(s_co, 2), row(s_co, 3), dbf[:, :H],
        pad_row(row(s_mlp1, 2)[:, 0:1]))
    h_conv, tok = push_grads([dw1, ddw_s, small_g], ["slot", "slot", "whole"], "grads_conv")

    def opt(recvs, w, m, v, name):
        shp = w.shape
        L, C = shp[0], shp[-1]
        R = int(np.prod(shp[1:-1]))
        outs = _adamw([r.reshape(N_DEV, R, r.shape[-1]) for r in recvs], w.reshape(L, R, C), m.reshape(L, R, C),
                      v.reshape(L, R, C), _row_block(R, C), name)
        return [t.reshape(shp) for t in outs]

    big = {}
    r_wu1, r_wd1 = pull_grads(h_mlp1, tok, "grads_mlp1")
    r_win, r_wo = pull_grads(h_attn, r_wd1, "grads_attn")
    big["attn_w_in"] = opt([r_win], attn_w_in, m_attn_w_in, v_attn_w_in, "adamw_win")
    big["attn_w_o"] = opt([r_wo], attn_w_o, m_attn_w_o, v_attn_w_o, "adamw_wo")
    r_wu0, r_wd0 = pull_grads(h_mlp0, big["attn_w_o"][0], "grads_mlp0")
    big["mlp_w_up"] = opt([r_wu0, r_wu1], mlp_w_up, m_mlp_w_up, v_mlp_w_up, "adamw_wup")
    big["mlp_w_down"] = opt([r_wd0, r_wd1], mlp_w_down, m_mlp_w_down, v_mlp_w_down, "adamw_wdown")
    (r_w2,) = pull_grads(h_pw2, big["mlp_w_down"][0], "grads_pw2")
    big["conv_pw2_w"] = opt([r_w2], conv_pw2_w, m_conv_pw2_w, v_conv_pw2_w, "adamw_pw2")
    r_w1, r_dw, r_small = pull_grads(h_conv, big["conv_pw2_w"][0], "grads_conv")
    big["conv_pw1_w"] = opt([r_w1], conv_pw1_w, m_conv_pw1_w, v_conv_pw1_w, "adamw_pw1")
    big["conv_dw_w"] = opt([r_dw], conv_dw_w, m_conv_dw_w, v_conv_dw_w, "adamw_dw")
    small_w = pack(g_mix_pre, g_mix_post, g_ffn_pre, g_ffn_post, conv_pw1_b, conv_dw_b, conv_ln_g, conv_ln_b,
                   conv_pw2_b, attn_b_f, zero_row)
    small_m = pack(m_g_mix_pre, m_g_mix_post, m_g_ffn_pre, m_g_ffn_post, m_conv_pw1_b, m_conv_dw_b, m_conv_ln_g,
                   m_conv_ln_b, m_conv_pw2_b, m_attn_b_f, zero_row)
    small_v = pack(v_g_mix_pre, v_g_mix_post, v_g_ffn_pre, v_g_ffn_post, v_conv_pw1_b, v_conv_dw_b, v_conv_ln_g,
                   v_conv_ln_b, v_conv_pw2_b, v_attn_b_f, zero_row)
    sm = _adamw([r_small], small_w[None], small_m[None], small_v[None], small_w.shape[0], "adamw_small")
    sm = [t[0] for t in sm]
    loss = sm[0][15, 0]

    def unpack(t):
        return {"g_mix_pre": t[0:2], "g_mix_post": t[2:4], "g_ffn_pre": t[4:6], "g_ffn_post": t[6:8],
                "conv_pw1_b": t[8:10].reshape(1, 2 * D), "conv_dw_b": t[10:11], "conv_ln_g": t[11:12],
                "conv_ln_b": t[12:13], "conv_pw2_b": t[13:14], "attn_b_f": t[14:15, :H]}

    small = [unpack(t) for t in sm]
    names = ["g_mix_pre", "g_mix_post", "g_ffn_pre", "g_ffn_post", "conv_pw1_w", "conv_pw1_b", "conv_dw_w",
             "conv_dw_b", "conv_ln_g", "conv_ln_b", "conv_pw2_w", "conv_pw2_b", "attn_w_in", "attn_b_f",
             "attn_w_o", "mlp_w_up", "mlp_w_down"]
    outs = [loss, grad_x.reshape(1, T, D)]
    for kind in range(4):
        for nme in names:
            outs.append(big[nme][kind] if nme in big else small[kind][nme])
    return tuple(outs)
```

```python
import math

import numpy as np
import jax
import jax.numpy as jnp
from jax import lax
from jax.experimental import pallas as pl
from jax.experimental.pallas import tpu as pltpu

F32 = jnp.float32
BF16 = jnp.bfloat16

RMS_EPS = 1e-6
LN_EPS = 1e-5
MASK_VALUE = -1e30
ADAM_LR = 0.001
ADAM_B1 = 0.9
ADAM_B2 = 0.999
ADAM_EPS = 1e-08
ADAM_WD = 0.01
ADAM_STEP = 10

N_DEV = 8
LANES = 128
SUBLANES = 8
CONV_HALO = 32
CONV_ROWS = 32
FLASH_ROWS = 32
VMEM_LIMIT = 56 * 1024 * 1024

_pcall = pl.pallas_call


def _params(sem=None):
    if sem is None:
        return pltpu.CompilerParams(vmem_limit_bytes=VMEM_LIMIT)
    return pltpu.CompilerParams(dimension_semantics=sem, vmem_limit_bytes=VMEM_LIMIT)


def _dot(a, b):
    return jnp.dot(a, b, preferred_element_type=F32)


def _dot_nt(a, b):
    return lax.dot_general(a, b, (((1,), (1,)), ((), ())), preferred_element_type=F32)


def _dot_tn(a, b):
    return lax.dot_general(a, b, (((0,), (0,)), ((), ())), preferred_element_type=F32)


def _full(shape):
    nd = len(shape)
    return pl.BlockSpec(shape, lambda *g: (0,) * nd, pipeline_mode=pl.Buffered(1))


def _acc(shape):
    nd = len(shape)
    return pl.BlockSpec(shape, lambda *g: (0,) * nd)


def _rows(tm, cols):
    return pl.BlockSpec((tm, cols), lambda i: (i, 0))


def _rms(x, g):
    r = lax.rsqrt(jnp.mean(x * x, axis=-1, keepdims=True) + RMS_EPS)
    return x * r * g


def _rms_bwd(x, g, dy):
    r = lax.rsqrt(jnp.mean(x * x, axis=-1, keepdims=True) + RMS_EPS)
    n = x * r
    dg = jnp.sum(dy * n, axis=0, keepdims=True)
    dn = dy * g
    dx = r * (dn - n * jnp.mean(dn * n, axis=-1, keepdims=True))
    return dx, dg


def _sigmoid(x):
    return 1.0 / (1.0 + jnp.exp(-x))


def _mesh_pos():
    return lax.axis_index("x"), lax.axis_index("y"), lax.axis_index("c")


def _dev_index(px, py, pc):
    return 4 * px + 2 * py + pc


_HBM = pl.BlockSpec(memory_space=pltpu.HBM)
_SEM = pl.BlockSpec(memory_space=pltpu.SEMAPHORE)
_EFFECT = pltpu.SideEffectType.DATAFLOW_SIDE_EFFECTING


def _peer(r, x, y, c):
    p = ((1 - x) if r & 4 else x, (1 - y) if r & 2 else y, (1 - c) if r & 1 else c)
    return p, _dev_index(*p)


def _src_ref(refs, item, me_id=None, to_id=None):
    si, sub, mode = item
    r = refs[si] if sub is None else refs[si].at[sub]
    if mode == "slot":
        return r.at[to_id]
    if mode == "own":
        return r.at[me_id]
    return r


def _place_own(srcs, items, land_shapes, name, cast=False):
    ns, n = len(srcs), len(items)

    def body(*refs):
        src = refs[:ns]
        land = refs[ns:ns + n]
        stage = refs[ns + n:ns + 2 * n] if cast else None
        sems = refs[-1]
        me_id = _dev_index(*_mesh_pos())
        cps = []
        for a, item in enumerate(items):
            s = _src_ref(src, item, to_id=me_id)
            if cast:
                if s.shape != stage[a].shape:
                    stage[a][...] = jnp.zeros_like(stage[a])
                    stage[a][:, 0:s.shape[-1]] = s[...].astype(stage[a].dtype)
                else:
                    stage[a][...] = s[...].astype(stage[a].dtype)
                s = stage[a]
            cp = pltpu.make_async_copy(s, land[a].at[me_id], sems.at[a])
            cp.start()
            cps.append(cp)
        for cp in cps:
            cp.wait()

    return _pcall(
        body, name=name,
        out_shape=[jax.ShapeDtypeStruct(tuple(s), d) for s, d in land_shapes],
        in_specs=[pl.BlockSpec(memory_space=pltpu.VMEM if cast else pl.ANY)] * ns,
        out_specs=[pl.BlockSpec(memory_space=pl.ANY)] * n,
        scratch_shapes=([pltpu.VMEM(tuple(s[1:]), d) for s, d in land_shapes] if cast else [])
        + [pltpu.SemaphoreType.DMA((n,))],
        compiler_params=pltpu.CompilerParams(vmem_limit_bytes=VMEM_LIMIT),
    )(*srcs)


def _seed_lands(srcs, modes, me, name):
    n = len(srcs)
    parts = [tuple(s.shape[1:]) if mode == "slot" else tuple(s.shape) for s, mode in zip(srcs, modes)]

    def body(me_ref, *refs):
        s = pl.program_id(0)
        for a in range(n):
            v = refs[a][...]
            refs[n + a][...] = jnp.where(s == me_ref[0], v, jnp.zeros_like(v))

    def in_spec(part, mode):
        nd = len(part)
        if mode == "slot":
            return pl.BlockSpec((None,) + part, lambda s, me_ref: (me_ref[0],) + (0,) * nd)
        return pl.BlockSpec(part, lambda s, me_ref: (0,) * nd)

    def out_spec(part):
        nd = len(part)
        return pl.BlockSpec((None,) + part, lambda s, me_ref: (s,) + (0,) * nd)

    return _pcall(
        body, name=name,
        grid_spec=pltpu.PrefetchScalarGridSpec(
            num_scalar_prefetch=1, grid=(N_DEV,),
            in_specs=[in_spec(p, m) for p, m in zip(parts, modes)],
            out_specs=[out_spec(p) for p in parts]),
        out_shape=[jax.ShapeDtypeStruct((N_DEV,) + p, s.dtype) for p, s in zip(parts, srcs)],
        compiler_params=_params(("arbitrary",)),
    )(me, *srcs)


def _push_start(srcs, lands, groups, name):
    ns, n = len(srcs), len(lands)
    ng = len(groups)
    assert sum(len(g) for g in groups) == n

    def body(*refs):
        ops = refs[:ns + n]
        land = refs[ns:ns + n]
        sems = refs[ns + n:ns + n + 2 * ng]
        token = refs[-1]
        x, y, c = _mesh_pos()
        me_id = _dev_index(x, y, c)
        a = 0
        for gi, grp in enumerate(groups):
            for k, item in enumerate(grp):
                for r in range(1, N_DEV):
                    p, pid = _peer(r, x, y, c)
                    pltpu.make_async_remote_copy(
                        src_ref=_src_ref(ops, item, me_id=me_id, to_id=pid), dst_ref=land[a].at[me_id],
                        send_sem=sems[2 * gi].at[k * (N_DEV - 1) + r - 1],
                        recv_sem=sems[2 * gi + 1].at[k * (N_DEV - 1) + r - 1],
                        device_id=p, device_id_type=pl.DeviceIdType.MESH).start()
                a += 1
        token[...] = jnp.zeros_like(token)

    sem_shapes = []
    for grp in groups:
        sem_shapes += [pltpu.SemaphoreType.DMA((len(grp) * (N_DEV - 1),))] * 2
    arrs = list(srcs) + list(lands)
    res = _pcall(
        body, name=name,
        out_shape=tuple(sem_shapes) + tuple(pltpu.HBM(a.shape, a.dtype) for a in arrs)
        + (jax.ShapeDtypeStruct((SUBLANES, LANES), F32),),
        in_specs=[_HBM] * (ns + n),
        out_specs=tuple([_SEM] * (2 * ng)) + tuple([_HBM] * (ns + n)) + (pl.BlockSpec(memory_space=pltpu.VMEM),),
        input_output_aliases={i: 2 * ng + i for i in range(ns + n)},
        compiler_params=pltpu.CompilerParams(has_side_effects=_EFFECT),
    )(*[pltpu.with_memory_space_constraint(a, pltpu.HBM) for a in arrs])
    sems = [(res[2 * gi], res[2 * gi + 1]) for gi in range(ng)]
    thru = res[2 * ng:2 * ng + ns + n]
    return sems, list(thru[:ns]), list(thru[ns:]), res[-1]


def _push_wait(sems, srcs, lands, group, after, name):
    ns, n = len(srcs), len(lands)
    assert len(group) == n

    def body(*refs):
        ops = refs[:ns + n]
        land = refs[ns:ns + n]
        send_sems, recv_sems = refs[ns + n], refs[ns + n + 1]
        x, y, c = _mesh_pos()
        me_id = _dev_index(x, y, c)
        for k, item in enumerate(group):
            for r in range(1, N_DEV):
                p, pid = _peer(r, x, y, c)
                cp = pltpu.make_async_remote_copy(
                    src_ref=_src_ref(ops, item, me_id=me_id, to_id=pid), dst_ref=land[k].at[pid],
                    send_sem=send_sems.at[k * (N_DEV - 1) + r - 1], recv_sem=recv_sems.at[k * (N_DEV - 1) + r - 1],
                    device_id=p, device_id_type=pl.DeviceIdType.MESH)
                cp.wait_send()
                cp.wait_recv()

    arrs = list(srcs) + list(lands)
    res = _pcall(
        body, name=name,
        out_shape=tuple(pltpu.HBM(a.shape, a.dtype) for a in arrs),
        in_specs=[_HBM] * (ns + n) + [_SEM, _SEM, pl.BlockSpec(memory_space=pl.ANY)],
        out_specs=tuple([_HBM] * (ns + n)),
        input_output_aliases={i: i for i in range(ns + n)},
        compiler_params=pltpu.CompilerParams(has_side_effects=_EFFECT),
    )(*arrs, sems[0], sems[1], after)
    return list(res[ns:])


def _conv_in_fwd(x, g_pre, w1g, b1, tm):
    T, D = x.shape
    ns, _, cs = w1g.shape
    half = ns // 2

    def body(x_ref, g_ref, w_ref, b_ref, a_ref, u_ref):
        h = _rms(x_ref[...], g_ref[...]).astype(BF16)
        parts = []
        for s in range(ns):
            a_s = _dot(h, w_ref[s]) + b_ref[:, s * cs:(s + 1) * cs]
            a_ref[:, s * cs:(s + 1) * cs] = a_s
            parts.append(a_s)
        for s in range(half):
            u_ref[:, s * cs:(s + 1) * cs] = parts[s] * _sigmoid(parts[s + half])

    return _pcall(
        body, name="conv_in_fwd", grid=(T // tm,),
        in_specs=[_rows(tm, D), _full((1, D)), _full(w1g.shape), _full((1, 2 * D))],
        out_specs=[_rows(tm, 2 * D), _rows(tm, D)],
        out_shape=[jax.ShapeDtypeStruct((T, 2 * D), F32), jax.ShapeDtypeStruct((T, D), F32)],
        compiler_params=_params(("arbitrary",)),
    )(x, g_pre, w1g, b1)


def _shifted_copies(ext_ref, sh_ref, tm):
    n = tm + CONV_HALO - SUBLANES
    for b in range(1, SUBLANES):
        sh_ref[b - 1, 0:n, :] = ext_ref[b:b + n, :]


def _shifted_rows(ext_ref, sh_ref, off, r0, ls):
    b = off % SUBLANES
    a8 = off - b + r0
    src = ext_ref if b == 0 else sh_ref.at[b - 1]
    return src[a8:a8 + CONV_ROWS, ls]


def _dwconv_fwd(u, w32, b, tm, lc, width):
    T, D = u.shape
    hb = tm // CONV_HALO

    def body(u_ref, halo_ref, w_ref, b_ref, y_ref, ext_ref, sh_ref):
        i = pl.program_id(0)
        ext_ref[0:CONV_HALO, :] = jnp.where(i > 0, halo_ref[...], 0.0)
        ext_ref[CONV_HALO:, :] = u_ref[...]
        _shifted_copies(ext_ref, sh_ref, tm)
        for r0 in range(0, tm, CONV_ROWS):
            for l0 in range(0, lc, LANES):
                ls = slice(l0, l0 + LANES)
                acc = jnp.zeros((CONV_ROWS, LANES), F32) + b_ref[:, ls]
                for j in range(width):
                    off = CONV_HALO - (width - 1) + j
                    acc = acc + w_ref[j:j + 1, ls] * _shifted_rows(ext_ref, sh_ref, off, r0, ls)
                y_ref[r0:r0 + CONV_ROWS, ls] = acc

    return _pcall(
        body, name="dwconv_fwd", grid=(T // tm, D // lc),
        in_specs=[pl.BlockSpec((tm, lc), lambda i, l: (i, l)),
                  pl.BlockSpec((CONV_HALO, lc), lambda i, l: (jnp.maximum(i * hb - 1, 0), l)),
                  pl.BlockSpec((32, lc), lambda i, l: (0, l)),
                  pl.BlockSpec((1, lc), lambda i, l: (0, l))],
        out_specs=pl.BlockSpec((tm, lc), lambda i, l: (i, l)),
        out_shape=jax.ShapeDtypeStruct((T, D), F32),
        scratch_shapes=[pltpu.VMEM((tm + CONV_HALO, lc), F32),
                        pltpu.VMEM((SUBLANES - 1, tm + CONV_HALO, lc), F32)],
        compiler_params=_params(("arbitrary", "arbitrary")),
    )(u, u, w32, b)


def _ln_parts(y, g, b):
    mu = jnp.mean(y, axis=-1, keepdims=True)
    yc = y - mu
    rstd = lax.rsqrt(jnp.mean(yc * yc, axis=-1, keepdims=True) + LN_EPS)
    yhat = yc * rstd
    return yhat, rstd, yhat * g + b


def _conv_out_fwd(y, x, ln_g, ln_b, w2, b2, g_post, tm):
    T, D = x.shape

    def body(y_ref, x_ref, lg_ref, lb_ref, w_ref, b_ref, g_ref, m_ref, xo_ref):
        _, _, yn = _ln_parts(y_ref[...], lg_ref[...], lb_ref[...])
        z = (yn * _sigmoid(yn)).astype(BF16)
        m = _dot(z, w_ref[...]) + b_ref[...]
        m_ref[...] = m
        xo_ref[...] = x_ref[...] + _rms(m, g_ref[...])

    return _pcall(
        body, name="conv_out_fwd", grid=(T // tm,),
        in_specs=[_rows(tm, D), _rows(tm, D), _full((1, D)), _full((1, D)), _full((D, D)), _full((1, D)),
                  _full((1, D))],
        out_specs=[_rows(tm, D), _rows(tm, D)],
        out_shape=[jax.ShapeDtypeStruct((T, D), F32), jax.ShapeDtypeStruct((T, D), F32)],
        compiler_params=_params(("arbitrary",)),
    )(y, x, ln_g, ln_b, w2, b2, g_post)


def _conv_out_bwd(dxo, m, y, ln_g, ln_b, w2, g_post, tm, dep):
    T, D = m.shape

    def body(dxo_ref, m_ref, y_ref, lg_ref, lb_ref, w_ref, g_ref, dep_ref, dy_ref, dm_ref, z_ref, sums_ref):
        i = pl.program_id(0)
        dm, dgpost = _rms_bwd(m_ref[...], g_ref[...], dxo_ref[...])
        dmb = dm.astype(BF16)
        dm_ref[...] = dmb
        yhat, rstd, yn = _ln_parts(y_ref[...], lg_ref[...], lb_ref[...])
        sg = _sigmoid(yn)
        z_ref[...] = (yn * sg).astype(BF16)
        dz = _dot_nt(dmb, w_ref[...])
        dyn = dz * (sg + yn * sg * (1.0 - sg))
        dyh = dyn * lg_ref[...]
        dy = rstd * (dyh - jnp.mean(dyh, axis=-1, keepdims=True)
                     - yhat * jnp.mean(dyh * yhat, axis=-1, keepdims=True))
        dy_ref[...] = dy

        @pl.when(i == 0)
        def _():
            sums_ref[...] = jnp.zeros_like(sums_ref)
        sums_ref[0:1, :] += dgpost
        sums_ref[1:2, :] += jnp.sum(dyn * yhat, axis=0, keepdims=True)
        sums_ref[2:3, :] += jnp.sum(dyn, axis=0, keepdims=True)
        sums_ref[3:4, :] += jnp.sum(dm, axis=0, keepdims=True)
        sums_ref[4:5, :] += jnp.sum(dy, axis=0, keepdims=True)

    return _pcall(
        body, name="conv_out_bwd", grid=(T // tm,),
        in_specs=[_rows(tm, D), _rows(tm, D), _rows(tm, D), _full((1, D)), _full((1, D)), _full((D, D)),
                  _full((1, D)), pl.BlockSpec(memory_space=pl.ANY)],
        out_specs=[_rows(tm, D), _rows(tm, D), _rows(tm, D), _acc((SUBLANES, D))],
        out_shape=[jax.ShapeDtypeStruct((T, D), F32), jax.ShapeDtypeStruct((T, D), BF16),
                   jax.ShapeDtypeStruct((T, D), BF16), jax.ShapeDtypeStruct((SUBLANES, D), F32)],
        compiler_params=_params(("arbitrary",)),
    )(dxo, m, y, ln_g, ln_b, w2, g_post, dep)


def _dwconv_bwd(dy, u, w32, tm, lc, width):
    T, D = u.shape
    hb = tm // CONV_HALO
    nt = T // tm
    last_halo = T // CONV_HALO - 1

    def body(dy_ref, dyn_ref, u_ref, up_ref, w_ref, du_ref, dw_ref, exty_ref, extu_ref, acc_ref, shy_ref, shu_ref):
        i = pl.program_id(1)
        exty_ref[0:tm, :] = dy_ref[...]
        exty_ref[tm:, :] = jnp.where(i < nt - 1, dyn_ref[...], 0.0)
        extu_ref[0:CONV_HALO, :] = jnp.where(i > 0, up_ref[...], 0.0)
        extu_ref[CONV_HALO:, :] = u_ref[...]
        _shifted_copies(exty_ref, shy_ref, tm)
        _shifted_copies(extu_ref, shu_ref, tm)

        @pl.when(i == 0)
        def _():
            acc_ref[...] = jnp.zeros_like(acc_ref)

        for r0 in range(0, tm, CONV_ROWS):
            for l0 in range(0, lc, LANES):
                ls = slice(l0, l0 + LANES)
                dyc = exty_ref[r0:r0 + CONV_ROWS, ls]
                du = jnp.zeros((CONV_ROWS, LANES), F32)
                for j in range(width):
                    du = du + w_ref[j:j + 1, ls] * _shifted_rows(exty_ref, shy_ref, (width - 1) - j, r0, ls)
                    prod = dyc * _shifted_rows(extu_ref, shu_ref, CONV_HALO - (width - 1) + j, r0, ls)
                    acc_ref[j, :, ls] += prod.reshape(CONV_ROWS // SUBLANES, SUBLANES, LANES).sum(axis=0)
                du_ref[r0:r0 + CONV_ROWS, ls] = du

        @pl.when(i == nt - 1)
        def _():
            for j in range(32):
                dw_ref[j:j + 1, :] = jnp.sum(acc_ref[j], axis=0, keepdims=True)

    return _pcall(
        body, name="dwconv_bwd", grid=(D // lc, nt),
        in_specs=[pl.BlockSpec((tm, lc), lambda l, i: (i, l)),
                  pl.BlockSpec((CONV_HALO, lc), lambda l, i: (jnp.minimum((i + 1) * hb, last_halo), l)),
                  pl.BlockSpec((tm, lc), lambda l, i: (i, l)),
                  pl.BlockSpec((CONV_HALO, lc), lambda l, i: (jnp.maximum(i * hb - 1, 0), l)),
                  pl.BlockSpec((32, lc), lambda l, i: (0, l))],
        out_specs=[pl.BlockSpec((tm, lc), lambda l, i: (i, l)),
                   pl.BlockSpec((32, lc), lambda l, i: (0, l))],
        out_shape=[jax.ShapeDtypeStruct((T, D), F32), jax.ShapeDtypeStruct((32, D), F32)],
        scratch_shapes=[pltpu.VMEM((tm + CONV_HALO, lc), F32), pltpu.VMEM((tm + CONV_HALO, lc), F32),
                        pltpu.VMEM((32, SUBLANES, lc), F32),
                        pltpu.VMEM((SUBLANES - 1, tm + CONV_HALO, lc), F32),
                        pltpu.VMEM((SUBLANES - 1, tm + CONV_HALO, lc), F32)],
        compiler_params=_params(("arbitrary", "arbitrary")),
    )(dy, dy, u, u, w32)


def _conv_in_bwd(dxo, du, a, x, g_pre, w1g, tm, dep):
    T, D = x.shape
    ns, _, cs = w1g.shape
    half = ns // 2

    def body(dxo_ref, du_ref, a_ref, x_ref, g_ref, w_ref, dep_ref, dxi_ref, h_ref, da_ref, sums_ref, db_ref):
        i = pl.program_id(0)
        xv = x_ref[...]
        h_ref[...] = _rms(xv, g_ref[...]).astype(BF16)
        dh = jnp.zeros((tm, D), F32)
        dbs = [None] * ns
        for s in range(half):
            a_u = a_ref[:, s * cs:(s + 1) * cs]
            sg = _sigmoid(a_ref[:, (s + half) * cs:(s + half + 1) * cs])
            du_s = du_ref[:, s * cs:(s + 1) * cs]
            da_u = du_s * sg
            da_g = du_s * a_u * sg * (1.0 - sg)
            for s2, v in ((s, da_u), (s + half, da_g)):
                vb = v.astype(BF16)
                da_ref[:, s2 * cs:(s2 + 1) * cs] = vb
                dbs[s2] = jnp.sum(v, axis=0, keepdims=True)
                dh = dh + _dot_nt(vb, w_ref[s2])
        dxi, dgpre = _rms_bwd(xv, g_ref[...], dh)
        dxi_ref[...] = dxo_ref[...] + dxi

        @pl.when(i == 0)
        def _():
            sums_ref[...] = jnp.zeros_like(sums_ref)
            db_ref[...] = jnp.zeros_like(db_ref)
        sums_ref[0:1, :] += dgpre
        for s in range(ns):
            db_ref[:, s * cs:(s + 1) * cs] += dbs[s]

    return _pcall(
        body, name="conv_in_bwd", grid=(T // tm,),
        in_specs=[_rows(tm, D), _rows(tm, D), _rows(tm, 2 * D), _rows(tm, D), _full((1, D)),
                  _full(w1g.shape), pl.BlockSpec(memory_space=pl.ANY)],
        out_specs=[_rows(tm, D), _rows(tm, D), _rows(tm, 2 * D), _acc((SUBLANES, D)), _acc((1, 2 * D))],
        out_shape=[jax.ShapeDtypeStruct((T, D), F32), jax.ShapeDtypeStruct((T, D), BF16),
                   jax.ShapeDtypeStruct((T, 2 * D), BF16), jax.ShapeDtypeStruct((SUBLANES, D), F32),
                   jax.ShapeDtypeStruct((1, 2 * D), F32)],
        compiler_params=_params(("arbitrary",)),
    )(dxo, du, a, x, g_pre, w1g, dep)


def _mlp_fwd(x, g_pre, wug, wdg, g_post, tm, name):
    T, D = x.shape
    ns, _, fs = wug.shape

    def body(x_ref, gp_ref, wu_ref, wd_ref, gq_ref, up_ref, m_ref, xo_ref):
        xv = x_ref[...]
        h = _rms(xv, gp_ref[...]).astype(BF16)
        acc = jnp.zeros((tm, D), F32)
        for s in range(ns):
            up = _dot(h, wu_ref[s]).astype(BF16)
            up_ref[:, s * fs:(s + 1) * fs] = up
            act = jnp.square(jnp.maximum(up.astype(F32), 0.0)).astype(BF16)
            acc = acc + _dot(act, wd_ref[s])
        m_ref[...] = acc
        xo_ref[...] = xv + _rms(acc, gq_ref[...])

    return _pcall(
        body, name=name, grid=(T // tm,),
        in_specs=[_rows(tm, D), _full((1, D)), _full(wug.shape), _full(wdg.shape), _full((1, D))],
        out_specs=[_rows(tm, ns * fs), _rows(tm, D), _rows(tm, D)],
        out_shape=[jax.ShapeDtypeStruct((T, ns * fs), BF16), jax.ShapeDtypeStruct((T, D), F32),
                   jax.ShapeDtypeStruct((T, D), F32)],
        compiler_params=_params(("arbitrary",)),
    )(x, g_pre, wug, wdg, g_post)


def _mlp_bwd(dxo, m, x, up, g_pre, wug, wdg, g_post, tm, name, dep, from_loss=False):
    T, D = x.shape
    ns, _, fs = wug.shape

    def body(dxo_ref, m_ref, x_ref, up_ref, gp_ref, wu_ref, wd_ref, gq_ref, dep_ref,
             dxi_ref, h_ref, dm_ref, dup_ref, sums_ref):
        i = pl.program_id(0)
        if from_loss:
            err = dxo_ref[...] - dep_ref[...]
            dxo = err * (1.0 / D)
        else:
            dxo = dxo_ref[...]
        dm, dgpost = _rms_bwd(m_ref[...], gq_ref[...], dxo)
        dmb = dm.astype(BF16)
        dm_ref[...] = dmb
        xv = x_ref[...]
        h_ref[...] = _rms(xv, gp_ref[...]).astype(BF16)
        dh = jnp.zeros((tm, D), F32)
        for s in range(ns):
            dact = _dot_nt(dmb, wd_ref[s])
            up = up_ref[:, s * fs:(s + 1) * fs].astype(F32)
            dup = (dact * (2.0 * jnp.maximum(up, 0.0))).astype(BF16)
            dup_ref[:, s * fs:(s + 1) * fs] = dup
            dh = dh + _dot_nt(dup, wu_ref[s])
        dxi, dgpre = _rms_bwd(xv, gp_ref[...], dh)
        dxi_ref[...] = dxo + dxi

        @pl.when(i == 0)
        def _():
            sums_ref[...] = jnp.zeros_like(sums_ref)
        sums_ref[0:1, :] += dgpost
        sums_ref[1:2, :] += dgpre
        if from_loss:
            sums_ref[2:3, :] += 0.5 * jnp.sum(jnp.mean(err * err, axis=-1, keepdims=True), axis=0, keepdims=True)

    return _pcall(
        body, name=name, grid=(T // tm,),
        in_specs=[_rows(tm, D), _rows(tm, D), _rows(tm, D), _rows(tm, ns * fs), _full((1, D)),
                  _full(wug.shape), _full(wdg.shape), _full((1, D)),
                  _rows(tm, D) if from_loss else pl.BlockSpec(memory_space=pl.ANY)],
        out_specs=[_rows(tm, D), _rows(tm, D), _rows(tm, D), _rows(tm, ns * fs), _acc((SUBLANES, D))],
        out_shape=[jax.ShapeDtypeStruct((T, D), F32), jax.ShapeDtypeStruct((T, D), BF16),
                   jax.ShapeDtypeStruct((T, D), BF16), jax.ShapeDtypeStruct((T, ns * fs), BF16),
                   jax.ShapeDtypeStruct((SUBLANES, D), F32)],
        compiler_params=_params(("arbitrary",)),
    )(dxo, m, x, up, g_pre, wug, wdg, g_post, dep)


def _mm_tn(a, g, *, nj, a_cols, g_cols, a_by_j, g_by_j, tk, name, act=False):
    T = a.shape[0]
    nk = T // tk

    def body(a_ref, g_ref, o_ref, acc_ref):
        k = pl.program_id(1)
        av = a_ref[...]
        if act:
            av = jnp.square(jnp.maximum(av.astype(F32), 0.0)).astype(BF16)
        p = _dot_tn(av, g_ref[...])

        @pl.when(k == 0)
        def _():
            acc_ref[...] = p

        @pl.when(k > 0)
        def _():
            acc_ref[...] += p

        @pl.when(k == nk - 1)
        def _():
            o_ref[...] = acc_ref[...].astype(BF16)

    return _pcall(
        body, name=name, grid=(nj, nk),
        in_specs=[pl.BlockSpec((tk, a_cols), (lambda j, k: (k, j)) if a_by_j else (lambda j, k: (k, 0))),
                  pl.BlockSpec((tk, g_cols), (lambda j, k: (k, j)) if g_by_j else (lambda j, k: (k, 0)))],
        out_specs=pl.BlockSpec((None, a_cols, g_cols), lambda j, k: (j, 0, 0)),
        out_shape=jax.ShapeDtypeStruct((nj, a_cols, g_cols), BF16),
        scratch_shapes=[pltpu.VMEM((a_cols, g_cols), F32)],
        compiler_params=_params(("arbitrary", "arbitrary")),
    )(a, g)


def _attn_in_fwd(x, g_pre, wqkv, wf, bf, tm, q_mul, n_heads):
    T, D = x.shape

    def body(x_ref, g_ref, w_ref, wf_ref, bf_ref, q_ref, k_ref, v_ref, lf_ref):
        h = _rms(x_ref[...], g_ref[...]).astype(BF16)
        q = _dot(h, w_ref[:, 0:D])
        if q_mul != 1.0:
            q = q * q_mul
        q_ref[...] = q.astype(BF16)
        k_ref[...] = _dot(h, w_ref[:, D:2 * D]).astype(BF16)
        v_ref[...] = _dot(h, w_ref[:, 2 * D:3 * D]).astype(BF16)
        fl = _dot(h, wf_ref[...]) + bf_ref[...]
        lf = jnp.minimum(fl, 0.0) - jnp.log(1.0 + jnp.exp(-jnp.abs(fl)))
        lane = lax.broadcasted_iota(jnp.int32, (1, LANES), 1)
        lf_ref[...] = jnp.where(lane < n_heads, lf, 0.0)

    return _pcall(
        body, name="attn_in_fwd", grid=(T // tm,),
        in_specs=[_rows(tm, D), _full((1, D)), _full((D, 3 * D)), _full((D, LANES)), _full((1, LANES))],
        out_specs=[_rows(tm, D), _rows(tm, D), _rows(tm, D), _rows(tm, LANES)],
        out_shape=[jax.ShapeDtypeStruct((T, D), BF16)] * 3 + [jax.ShapeDtypeStruct((T, LANES), F32)],
        compiler_params=_params(("arbitrary",)),
    )(x, g_pre, wqkv, wf, bf)


def _cumsum_rows(v, v2, tb, reverse, name):
    T, C = v.shape
    nb = T // tb

    def body(v_ref, v2_ref, o_ref, carry_ref):
        i = pl.program_id(0)

        @pl.when(i == 0)
        def _():
            carry_ref[...] = jnp.zeros_like(carry_ref)
        r = lax.broadcasted_iota(jnp.int32, (tb, tb), 0)
        c = lax.broadcasted_iota(jnp.int32, (tb, tb), 1)
        tri = jnp.where((c >= r) if reverse else (c <= r), 1.0, 0.0).astype(F32)
        out = jnp.dot(tri, v_ref[...] + v2_ref[...], precision=lax.Precision.HIGHEST,
                      preferred_element_type=F32) + carry_ref[...]
        o_ref[...] = out
        carry_ref[...] = out[0:1, :] if reverse else out[tb - 1:tb, :]

    idx = (lambda i: (nb - 1 - i, 0)) if reverse else (lambda i: (i, 0))
    return _pcall(
        body, name=name, grid=(nb,),
        in_specs=[pl.BlockSpec((tb, C), idx), pl.BlockSpec((tb, C), idx)],
        out_specs=pl.BlockSpec((tb, C), idx),
        out_shape=jax.ShapeDtypeStruct((T, C), F32),
        scratch_shapes=[pltpu.VMEM((1, C), F32)],
        compiler_params=_params(("arbitrary",)),
    )(v, v2)


def _head_col(v, lane, h):
    return jnp.sum(jnp.where(lane == h, v, 0.0), axis=1, keepdims=True)


def _flash_fwd(q, k, v, fq_aux, fk_rows, *, dh, tq, s_mul):
    T, D = q.shape
    G = D // LANES
    hpg = LANES // dh
    nq = T // tq
    k3 = k.reshape(nq, tq, D)
    v3 = v.reshape(nq, tq, D)

    rc = min(FLASH_ROWS, tq)

    def body(q_ref, k_ref, v_ref, fq_ref, fk_ref, o_ref, o32_ref, lse_ref, s_scr, p_scr):
        i = pl.program_id(1)
        lane = lax.broadcasted_iota(jnp.int32, (1, LANES), 1)
        q2 = q_ref[...]
        hmasks = [(lane >= hh * dh) & (lane < (hh + 1) * dh) for hh in range(hpg)]
        qms = [jnp.where(hm, q2, jnp.zeros_like(q2)) for hm in hmasks]

        nlb = tq // LANES
        ones_rows = jnp.ones((2 * SUBLANES, tq), BF16)

        def scores(j, slot):
            kj = k_ref[j]
            for hh in range(hpg):
                s = _dot_nt(qms[hh], kj)
                s_scr[slot, hh] = s if s_mul == 1.0 else s * s_mul

        def soft(j, slot, carry, masked):
            vjt = v_ref[j].astype(F32).T.astype(BF16)
            out = []
            for hh in range(hpg):
                m_b, acc = carry[hh]
                fk_row = fk_ref[j, hh:hh + 1, :]
                def future(r0, cb):
                    return masked and cb * LANES > r0 + rc - 1

                mx = []
                for r0 in range(0, tq, rc):
                    rs = slice(r0, r0 + rc)
                    c = None
                    for cb in range(nlb):
                        if future(r0, cb):
                            continue
                        cs = slice(cb * LANES, (cb + 1) * LANES)
                        s = s_scr[slot, hh, rs, cs] - fk_row[:, cs]
                        if masked and (cb + 1) * LANES - 1 > r0:
                            ri = r0 + lax.broadcasted_iota(jnp.int32, (rc, LANES), 0)
                            ci = cb * LANES + lax.broadcasted_iota(jnp.int32, (rc, LANES), 1)
                            s = jnp.where(ci <= ri, s, MASK_VALUE)
                        s_scr[slot, hh, rs, cs] = s
                        c = s if c is None else jnp.maximum(c, s)
                    mx.append(c)
                row_max = jnp.max(jnp.concatenate(mx, axis=0), axis=1, keepdims=True)
                m_new = jnp.maximum(m_b, row_max)
                alpha = jnp.exp(m_b - m_new)
                for r0 in range(0, tq, rc):
                    rs = slice(r0, r0 + rc)
                    m_c = m_new[rs]
                    for cb in range(nlb):
                        cs = slice(cb * LANES, (cb + 1) * LANES)
                        if future(r0, cb):
                            p_scr[slot, hh, rs, cs] = jnp.zeros((rc, LANES), BF16)
                        else:
                            p_scr[slot, hh, rs, cs] = jnp.exp(s_scr[slot, hh, rs, cs] - m_c).astype(BF16)
                v_one = jnp.concatenate([vjt[hh * dh:(hh + 1) * dh], ones_rows], axis=0)
                alpha_row = alpha.T[0:1, :]
                out.append((m_new, alpha_row * acc + _dot_nt(v_one, p_scr[slot, hh])))
            return tuple(out)

        def step(j, carry, masked, slot):
            scores(j, slot)
            return soft(j, slot, carry, masked)

        def pair(t, cr):
            scores(2 * t, 0)
            scores(2 * t + 1, 1)
            return soft(2 * t + 1, 1, soft(2 * t, 0, cr, False), False)

        init = tuple((jnp.full((tq, LANES), MASK_VALUE, F32), jnp.zeros((dh + 2 * SUBLANES, tq), F32))
                     for _ in range(hpg))
        carry = lax.fori_loop(0, i // 2, pair, init)
        carry = lax.cond(i % 2 == 1, lambda cr: step(i - 1, cr, False, 0), lambda cr: cr, carry)
        carry = step(i, carry, True, 1)
        fq = fq_ref[...]
        g0 = pl.program_id(0) * hpg
        lse_all = jnp.zeros((tq, LANES), F32)
        o_rows = [carry[hh][1][0:dh] * (1.0 / carry[hh][1][dh:dh + 1, :]) for hh in range(hpg)]
        o_all = jnp.concatenate(o_rows, axis=0).T
        sums = [carry[hh][1][dh:dh + SUBLANES] for hh in range(hpg)]
        l_cols = jnp.concatenate(sums + [jnp.zeros((LANES - hpg * SUBLANES, tq), F32)], axis=0).T
        for hh in range(hpg):
            l_col = l_cols[:, SUBLANES * hh:SUBLANES * hh + 1]
            lse_all = jnp.where(lane == hh,
                                carry[hh][0][:, 0:1] + jnp.log(l_col) + _head_col(fq, lane, g0 + hh), lse_all)
        o_ref[...] = o_all.astype(BF16)
        o32_ref[...] = o_all
        lse_ref[...] = lse_all

    return _pcall(
        body, name="flash_fwd", grid=(G, nq),
        in_specs=[pl.BlockSpec((tq, LANES), lambda g, i: (i, g)),
                  pl.BlockSpec((nq, tq, LANES), lambda g, i: (0, 0, g)),
                  pl.BlockSpec((nq, tq, LANES), lambda g, i: (0, 0, g)),
                  pl.BlockSpec((tq, LANES), lambda g, i: (i, 0)),
                  pl.BlockSpec((None, nq, SUBLANES, tq), lambda g, i: (g, 0, 0, 0))],
        out_specs=[pl.BlockSpec((tq, LANES), lambda g, i: (i, g)),
                   pl.BlockSpec((tq, LANES), lambda g, i: (i, g)),
                   pl.BlockSpec((None, tq, LANES), lambda g, i: (g, i, 0))],
        out_shape=[jax.ShapeDtypeStruct((T, D), BF16), jax.ShapeDtypeStruct((T, D), F32),
                   jax.ShapeDtypeStruct((G, T, LANES), F32)],
        scratch_shapes=[pltpu.VMEM((2, hpg, tq, tq), F32), pltpu.VMEM((2, hpg, tq, tq), BF16)],
        compiler_params=_params(("arbitrary", "arbitrary")),
    )(q, k3, v3, fq_aux, fk_rows)


def _flash_bwd(q, k, v, do, fq_aux, lse_aux, dl_aux, fk_rows, *, dh, tq, tk, s_mul, dq_mul):
    T, D = q.shape
    G = D // LANES
    hpg = LANES // dh
    nq = T // tq
    nk = T // tk
    per = tq // tk
    k3 = k.reshape(nk, tk, D)
    v3 = v.reshape(nk, tk, D)
    rc = min(FLASH_ROWS, tq)

    def body(q_ref, k_ref, v_ref, do_ref, fq_ref, lse_ref, dl_ref, fk_ref,
             dq_ref, dk_ref, dv_ref, dfq_ref, dfk_ref, dk_acc, dv_acc, s_scr, dp_scr, p_scr, ds_scr):
        i = pl.program_id(1)

        @pl.when(i == 0)
        def _():
            dk_acc[...] = jnp.zeros_like(dk_acc)
            dv_acc[...] = jnp.zeros_like(dv_acc)
            dfk_ref[...] = jnp.zeros_like(dfk_ref)

        lane = lax.broadcasted_iota(jnp.int32, (1, LANES), 1)
        q2 = q_ref[...]
        do2 = do_ref[...]
        g0 = pl.program_id(0) * hpg
        fq = fq_ref[...]
        lse = lse_ref[...]
        dl = dl_ref[...]
        hmasks = [(lane >= hh * dh) & (lane < (hh + 1) * dh) for hh in range(hpg)]
        qms = [jnp.where(hm, q2, jnp.zeros_like(q2)) for hm in hmasks]
        doms = [jnp.where(hm, do2, jnp.zeros_like(do2)) for hm in hmasks]
        q2t = q2.astype(F32).T.astype(BF16)
        do2t = do2.astype(F32).T.astype(BF16)
        ones_rows = jnp.ones((2 * SUBLANES, tk), BF16)
        q1t = [jnp.concatenate([q2t[hh * dh:(hh + 1) * dh], jnp.ones((2 * SUBLANES, tq), BF16)], axis=0)
               for hh in range(hpg)]
        c_bs = [jnp.broadcast_to(_head_col(fq, lane, g0 + hh) - lse[:, hh:hh + 1], (tq, LANES))
                for hh in range(hpg)]
        dl_bs = [jnp.broadcast_to(_head_col(dl, lane, g0 + hh), (tq, LANES)) for hh in range(hpg)]
        nlb = tk // LANES

        def step(j, carry, off, slot=0):
            masked = off is not None
            kj = k_ref[j]
            vj = v_ref[j]
            kjt = kj.astype(F32).T.astype(BF16)
            for hh in range(hpg):
                s = _dot_nt(qms[hh], kj)
                s_scr[slot, hh] = s if s_mul == 1.0 else s * s_mul
                dp_scr[slot, hh] = _dot_nt(doms[hh], vj)
            out = []
            for hh in range(hpg):
                fk_row = fk_ref[j, hh:hh + 1, :]
                for r0 in range(0, tq, rc):
                    rs = slice(r0, r0 + rc)
                    if masked and r0 + rc <= off:
                        p_scr[slot, hh, rs, :] = jnp.zeros((rc, tk), BF16)
                        ds_scr[slot, hh, rs, :] = jnp.zeros((rc, tk), BF16)
                        continue
                    c_c = c_bs[hh][rs]
                    dl_c = dl_bs[hh][rs]
                    for cb in range(nlb):
                        cs = slice(cb * LANES, (cb + 1) * LANES)
                        if masked and off + cb * LANES > r0 + rc - 1:
                            p_scr[slot, hh, rs, cs] = jnp.zeros((rc, LANES), BF16)
                            ds_scr[slot, hh, rs, cs] = jnp.zeros((rc, LANES), BF16)
                            continue
                        e = (s_scr[slot, hh, rs, cs] - fk_row[:, cs]) + c_c
                        if masked and off + (cb + 1) * LANES - 1 > r0:
                            ri = r0 + lax.broadcasted_iota(jnp.int32, (rc, LANES), 0)
                            ci = off + cb * LANES + lax.broadcasted_iota(jnp.int32, (rc, LANES), 1)
                            e = jnp.where(ci <= ri, e, MASK_VALUE)
                        p = jnp.exp(e)
                        ds = p * (dp_scr[slot, hh, rs, cs] - dl_c)
                        p_scr[slot, hh, rs, cs] = p.astype(BF16)
                        ds_scr[slot, hh, rs, cs] = ds.astype(BF16)
                k1t = jnp.concatenate([kjt[hh * dh:(hh + 1) * dh], ones_rows], axis=0)
                out.append(carry[hh] + _dot_nt(k1t, ds_scr[slot, hh]))
            dks = [_dot(q1t[hh], ds_scr[slot, hh]) for hh in range(hpg)]
            for hh in range(hpg):
                dfk_ref[j, hh:hh + 1, :] += -dks[hh][dh:dh + 1, :]
            dk_acc[j] += jnp.concatenate([d[0:dh] for d in dks], axis=0)
            dv_acc[j] += jnp.concatenate(
                [_dot(do2t[hh * dh:(hh + 1) * dh], p_scr[slot, hh]) for hh in range(hpg)], axis=0)
            return tuple(out)

        init = tuple(jnp.zeros((dh + 2 * SUBLANES, tq), F32) for _ in range(hpg))
        n_past = i * per

        def pair(t, cr):
            return step(2 * t + 1, step(2 * t, cr, None, 0), None, 1)

        carry = lax.fori_loop(0, n_past // 2, pair, init)
        carry = lax.cond(n_past % 2 == 1, lambda cr: step(n_past - 1, cr, None, 0), lambda cr: cr, carry)
        for d in range(per):
            carry = step(i * per + d, carry, d * tk, (d + 1) % 2)
        dq_all = jnp.concatenate([carry[hh][0:dh] for hh in range(hpg)], axis=0).T
        dq_ref[...] = (dq_all * dq_mul).astype(BF16)
        dfq_ref[...] = jnp.concatenate([carry[hh][dh:dh + 1, :] for hh in range(hpg)]
                                       + [jnp.zeros((SUBLANES - hpg, tq), F32)], axis=0)

        @pl.when(i == nq - 1)
        def _():
            for jj in range(nk):
                dkv = dk_acc[jj].T
                if s_mul != 1.0:
                    dkv = dkv * s_mul
                dk_ref[jj] = dkv.astype(BF16)
                dv_ref[jj] = dv_acc[jj].T.astype(BF16)

    blk = pl.BlockSpec((tq, LANES), lambda g, i: (i, g))
    res = pl.BlockSpec((nk, tk, LANES), lambda g, i: (0, 0, g))
    res_in = pl.BlockSpec((nk, tk, LANES), lambda g, i: (0, 0, g), pipeline_mode=pl.Buffered(1))
    aux = pl.BlockSpec((None, tq, LANES), lambda g, i: (g, i, 0))
    heads = pl.BlockSpec((tq, LANES), lambda g, i: (i, 0))
    rows = pl.BlockSpec((None, nk, SUBLANES, tk), lambda g, i: (g, 0, 0, 0))
    dq, dk3, dv3, dfq, dfk = _pcall(
        body, name="flash_bwd", grid=(G, nq),
        in_specs=[blk, res_in, res_in, blk, heads, aux, heads, rows],
        out_specs=[blk, res, res, pl.BlockSpec((None, None, SUBLANES, tq), lambda g, i: (g, i, 0, 0)), rows],
        out_shape=[jax.ShapeDtypeStruct((T, D), BF16), jax.ShapeDtypeStruct((nk, tk, D), BF16),
                   jax.ShapeDtypeStruct((nk, tk, D), BF16), jax.ShapeDtypeStruct((G, nq, SUBLANES, tq), F32),
                   jax.ShapeDtypeStruct((G, nk, SUBLANES, tk), F32)],
        scratch_shapes=[pltpu.VMEM((nk, LANES, tk), F32), pltpu.VMEM((nk, LANES, tk), F32),
                        pltpu.VMEM((2, hpg, tq, tk), F32), pltpu.VMEM((2, hpg, tq, tk), F32),
                        pltpu.VMEM((2, hpg, tq, tk), BF16), pltpu.VMEM((2, hpg, tq, tk), BF16)],
        compiler_params=_params(("arbitrary", "arbitrary")),
    )(q, k3, v3, do, fq_aux, lse_aux, dl_aux, fk_rows)
    return dq, dk3.reshape(T, D), dv3.reshape(T, D), dfq, dfk


def _attn_out_fwd(o, x, wo, g_post, tm):
    T, D = x.shape

    def body(o_ref, x_ref, w_ref, g_ref, m_ref, xo_ref):
        m = _dot(o_ref[...], w_ref[...])
        m_ref[...] = m
        xo_ref[...] = x_ref[...] + _rms(m, g_ref[...])

    return _pcall(
        body, name="attn_out_fwd", grid=(T // tm,),
        in_specs=[_rows(tm, D), _rows(tm, D), _full((D, D)), _full((1, D))],
        out_specs=[_rows(tm, D), _rows(tm, D)],
        out_shape=[jax.ShapeDtypeStruct((T, D), F32), jax.ShapeDtypeStruct((T, D), F32)],
        compiler_params=_params(("arbitrary",)),
    )(o, x, wo, g_post)


def _attn_out_bwd(dxo, m, o, wo, g_post, head_ind, tm, dep):
    T, D = m.shape

    def body(dxo_ref, m_ref, o_ref, w_ref, g_ref, ind_ref, dep_ref, dm_ref, do_ref, dl_ref, sums_ref):
        i = pl.program_id(0)
        dm, dgpost = _rms_bwd(m_ref[...], g_ref[...], dxo_ref[...])
        dmb = dm.astype(BF16)
        dm_ref[...] = dmb
        dob = _dot_nt(dmb, w_ref[...]).astype(BF16)
        do_ref[...] = dob
        dl_ref[...] = jnp.dot(dob.astype(F32) * o_ref[...], ind_ref[...], precision=lax.Precision.HIGHEST,
                              preferred_element_type=F32)

        @pl.when(i == 0)
        def _():
            sums_ref[...] = jnp.zeros_like(sums_ref)
        sums_ref[0:1, :] += dgpost

    return _pcall(
        body, name="attn_out_bwd", grid=(T // tm,),
        in_specs=[_rows(tm, D), _rows(tm, D), _rows(tm, D), _full((D, D)), _full((1, D)), _full((D, LANES)),
                  pl.BlockSpec(memory_space=pl.ANY)],
        out_specs=[_rows(tm, D), _rows(tm, D), _rows(tm, LANES), _acc((SUBLANES, D))],
        out_shape=[jax.ShapeDtypeStruct((T, D), BF16), jax.ShapeDtypeStruct((T, D), BF16),
                   jax.ShapeDtypeStruct((T, LANES), F32), jax.ShapeDtypeStruct((SUBLANES, D), F32)],
        compiler_params=_params(("arbitrary",)),
    )(dxo, m, o, wo, g_post, head_ind, dep)


def _attn_in_bwd(dxo, x, g_pre, dq, dk, dv, dlf, lf, wqkv, wf, tm, n_heads):
    T, D = x.shape

    def body(dxo_ref, x_ref, g_ref, dq_ref, dk_ref, dv_ref, dlf_ref, lf_ref, w_ref, wf_ref,
             dxi_ref, h_ref, df_ref, sums_ref, dbf_ref):
        i = pl.program_id(0)
        xv = x_ref[...]
        h_ref[...] = _rms(xv, g_ref[...]).astype(BF16)
        lane = lax.broadcasted_iota(jnp.int32, (1, LANES), 1)
        df = jnp.where(lane < n_heads, dlf_ref[...] * (1.0 - jnp.exp(lf_ref[...])), 0.0)
        dfb = df.astype(BF16)
        df_ref[...] = dfb
        dh = (_dot_nt(dq_ref[...], w_ref[:, 0:D]) + _dot_nt(dk_ref[...], w_ref[:, D:2 * D])
              + _dot_nt(dv_ref[...], w_ref[:, 2 * D:3 * D]) + _dot_nt(dfb, wf_ref[...]))
        dxi, dgpre = _rms_bwd(xv, g_ref[...], dh)
        dxi_ref[...] = dxo_ref[...] + dxi

        @pl.when(i == 0)
        def _():
            sums_ref[...] = jnp.zeros_like(sums_ref)
            dbf_ref[...] = jnp.zeros_like(dbf_ref)
        sums_ref[0:1, :] += dgpre
        dbf_ref[...] += jnp.sum(df, axis=0, keepdims=True)

    return _pcall(
        body, name="attn_in_bwd", grid=(T // tm,),
        in_specs=[_rows(tm, D), _rows(tm, D), _full((1, D)), _rows(tm, D), _rows(tm, D), _rows(tm, D),
                  _rows(tm, LANES), _rows(tm, LANES), _full((D, 3 * D)), _full((D, LANES))],
        out_specs=[_rows(tm, D), _rows(tm, D), _rows(tm, LANES), _acc((SUBLANES, D)), _acc((1, LANES))],
        out_shape=[jax.ShapeDtypeStruct((T, D), F32), jax.ShapeDtypeStruct((T, D), BF16),
                   jax.ShapeDtypeStruct((T, LANES), BF16), jax.ShapeDtypeStruct((SUBLANES, D), F32),
                   jax.ShapeDtypeStruct((1, LANES), F32)],
        compiler_params=_params(("arbitrary",)),
    )(dxo, x, g_pre, dq, dk, dv, dlf, lf, wqkv, wf)


def _adamw(recvs, w, m, v, tr, name):
    L, R, C = w.shape
    assert len(recvs) == L
    c1 = 1.0 - ADAM_B1 ** ADAM_STEP
    c2 = 1.0 - ADAM_B2 ** ADAM_STEP

    def body(*refs):
        r_refs = refs[:L]
        w_ref, m_ref, v_ref, g_ref, d_ref, nm_ref, nv_ref = refs[L:]
        layer = pl.program_id(0)
        g = None
        for k in range(L):
            gk = r_refs[k][0, :, 0:C].astype(F32)
            for s in range(1, N_DEV):
                gk = gk + r_refs[k][s, :, 0:C].astype(F32)
            g = gk if g is None else jnp.where(layer == k, gk, g)
        nm = ADAM_B1 * m_ref[...] + (1.0 - ADAM_B1) * g
        nv = ADAM_B2 * v_ref[...] + (1.0 - ADAM_B2) * jnp.square(g)
        m_hat = nm / c1
        v_hat = nv / c2
        g_ref[...] = g
        d_ref[...] = -ADAM_LR * (m_hat / (jnp.sqrt(v_hat) + ADAM_EPS) + ADAM_WD * w_ref[...])
        nm_ref[...] = nm
        nv_ref[...] = nv

    def recv_spec(k):
        return pl.BlockSpec((N_DEV, tr, recvs[k].shape[-1]), lambda l, i: (0, jnp.where(l == k, i, 0), 0))

    blk = pl.BlockSpec((None, tr, C), lambda l, i: (l, i, 0))
    return _pcall(
        body, name=name, grid=(L, R // tr),
        in_specs=[recv_spec(k) for k in range(L)] + [blk] * 3,
        out_specs=[blk] * 4,
        out_shape=[jax.ShapeDtypeStruct((L, R, C), F32)] * 4,
        compiler_params=_params(("arbitrary", "arbitrary")),
    )(*recvs, w, m, v)


def _row_block(rows, cols):
    cap = max(SUBLANES, (256 * 1024) // max(cols, 1))
    best = None
    for t in range(SUBLANES, rows + 1, SUBLANES):
        if rows % t == 0 and t <= cap:
            best = t
    return rows if best is None else best


def kernel(x, g_mix_pre, g_mix_post, g_ffn_pre, g_ffn_post, conv_pw1_w, conv_pw1_b, conv_dw_w, conv_dw_b, conv_ln_g, conv_ln_b, conv_pw2_w, conv_pw2_b, attn_w_in, attn_b_f, attn_w_o, mlp_w_up, mlp_w_down, loss_target, m_g_mix_pre, m_g_mix_post, m_g_ffn_pre, m_g_ffn_post, m_conv_pw1_w, m_conv_pw1_b, m_conv_dw_w, m_conv_dw_b, m_conv_ln_g, m_conv_ln_b, m_conv_pw2_w, m_conv_pw2_b, m_attn_w_in, m_attn_b_f, m_attn_w_o, m_mlp_w_up, m_mlp_w_down, v_g_mix_pre, v_g_mix_post, v_g_ffn_pre, v_g_ffn_post, v_conv_pw1_w, v_conv_pw1_b, v_conv_dw_w, v_conv_dw_b, v_conv_ln_g, v_conv_ln_b, v_conv_pw2_w, v_conv_pw2_b, v_attn_w_in, v_attn_b_f, v_attn_w_o, v_mlp_w_up, v_mlp_w_down):
    _, T, D = x.shape
    H = attn_b_f.shape[-1]
    dh = D // H
    width = conv_dw_w.shape[1]
    cin = attn_w_in.shape[-1]
    fs = mlp_w_up.shape[-1]
    G = D // LANES
    hpg = LANES // dh
    assert T % 4 == 0 and D % LANES == 0 and LANES % dh == 0 and width <= CONV_HALO and H <= LANES

    tm = min(512, T // 4)
    tmb = min(256, T // 4)
    tqf = min(1024, T // 4)
    tkb = tm
    tmc = min(256, T // 4)
    lc = min(256, D)
    tkw = min(4096, T // 2)
    tb = min(256, T // 4)

    scale = float(dh) ** -0.5
    mant, _ = math.frexp(scale)
    q_mul = scale if mant == 0.5 else 1.0
    s_mul = 1.0 if mant == 0.5 else scale

    x2 = x.reshape(T, D)
    tgt = loss_target.reshape(T, D)

    w_srcs = [conv_pw1_w, conv_dw_w, conv_pw2_w, mlp_w_up, mlp_w_down, attn_w_in, attn_w_o]
    w_items = [(0, 0, "whole"), (1, 0, "whole"), (2, 0, "whole"), (3, 0, "whole"), (4, 0, "whole"),
               (5, 0, "whole"), (6, 0, "whole"), (3, 1, "whole"), (4, 1, "whole")]
    cin_w = -(-cin // LANES) * LANES
    w_lands = _place_own(
        w_srcs, w_items,
        [((N_DEV, D, cin_w) if si == 5 else (N_DEV,) + w_srcs[si].shape[1:], F32 if si == 1 else BF16)
         for si, _, _ in w_items],
        "stage_weights", cast=True)
    me_arr = _dev_index(*_mesh_pos()).astype(jnp.int32).reshape(1)
    w_groups = [[0], [1, 2], [3, 4], [5, 6], [7, 8]]
    g_sems, _, w_lands, g_token = _push_start(
        [], w_lands, [[(a, None, "own") for a in grp] for grp in w_groups], "gather_start")

    def gather_wait(gi, after):
        grp = w_groups[gi]
        return _push_wait(g_sems[gi], [], [w_lands[a] for a in grp], [(k, None, "own") for k in range(len(grp))],
                          after, "gather_wait%d" % gi)

    (w1g,) = gather_wait(0, g_token)
    bf = jnp.pad(attn_b_f, ((0, 0), (0, LANES - H)))

    row = lambda a, i: a[i:i + 1]

    a0, u0 = _conv_in_fwd(x2, row(g_mix_pre, 0), w1g, conv_pw1_b, tm)
    dwg, w2g = gather_wait(1, u0)
    w2 = w2g.reshape(D, D)
    dw_full = jnp.transpose(dwg, (1, 0, 2)).reshape(width, D)
    w32 = jnp.pad(dw_full, ((0, 32 - width), (0, 0)))
    y0 = _dwconv_fwd(u0, w32, conv_dw_b, tmc, lc, width)
    m0, x_1 = _conv_out_fwd(y0, x2, conv_ln_g, conv_ln_b, w2, conv_pw2_b, row(g_mix_post, 0), tm)
    wu0, wd0 = gather_wait(2, x_1)
    up0, n0, x_2 = _mlp_fwd(x_1, row(g_ffn_pre, 0), wu0, wd0, row(g_ffn_post, 0), tm, "mlp0_fwd")

    wing, wog = gather_wait(3, x_2)
    wo = wog.reshape(D, D)
    win = jnp.transpose(wing[:, :, :cin], (1, 0, 2)).reshape(D, N_DEV * cin)
    wqkv = win[:, :3 * D]
    wf = jnp.pad(win[:, 3 * D:], ((0, 0), (0, LANES - H)))
    q, k, v, lf = _attn_in_fwd(x_2, row(g_mix_pre, 1), wqkv, wf, bf, tm, q_mul, H)
    fcum = _cumsum_rows(lf, jnp.zeros_like(lf), tb, False, "forget_cumsum")


    def key_rows(blk):
        r = jnp.transpose(fcum[:, :H].T.reshape(G, hpg, T // blk, blk), (0, 2, 1, 3))
        return jnp.pad(r, ((0, 0), (0, 0), (0, SUBLANES - hpg), (0, 0)))

    fk_rows = key_rows(tkb)
    o, o32, lse_aux = _flash_fwd(q, k, v, fcum, key_rows(tqf), dh=dh, tq=tqf, s_mul=s_mul)
    m1, x_3 = _attn_out_fwd(o, x_2, wo, row(g_mix_post, 1), tm)
    wu1, wd1 = gather_wait(4, x_3)
    up1, n1, x_4 = _mlp_fwd(x_3, row(g_ffn_pre, 1), wu1, wd1, row(g_ffn_post, 1), tm, "mlp1_fwd")

    def mlp_back(dx, n_l, x_in, up_l, l, wu, wd, dep, from_loss=False):
        dxi, h, dm, dup, sums = _mlp_bwd(dx, n_l, x_in, up_l, row(g_ffn_pre, l), wu, wd, row(g_ffn_post, l),
                                         tmb, "mlp%d_bwd" % l, dep, from_loss)
        dwu = _mm_tn(h, dup, nj=N_DEV, a_cols=D, g_cols=fs, a_by_j=False, g_by_j=True, tk=tkw,
                     name="mlp%d_dwu" % l)
        dwd = _mm_tn(up_l, dm, nj=N_DEV, a_cols=fs, g_cols=D, a_by_j=True, g_by_j=False, tk=tkw,
                     name="mlp%d_dwd" % l, act=True)
        return dxi, dwu, dwd, sums

    def push_grads(srcs, modes, name):
        items = [(i, None, mode) for i, mode in enumerate(modes)]
        lands = _seed_lands(srcs, modes, me_arr, name + "_own")
        sems, srcs_t, lands_t, token = _push_start(srcs, lands, [items], name + "_start")
        return (sems[0], srcs_t, lands_t, items), token

    def pull_grads(handle, after, name):
        sems, srcs_t, lands_t, items = handle
        return _push_wait(sems, srcs_t, lands_t, items, after, name + "_wait")

    rs = D // N_DEV
    dx, dwu1, dwd1, s_mlp1 = mlp_back(x_4, n1, x_3, up1, 1, wu1, wd1, tgt, from_loss=True)
    h_mlp1, tok = push_grads([dwu1, dwd1], ["slot", "slot"], "grads_mlp1")

    head_ind = jnp.asarray((np.arange(D)[:, None] // dh == np.arange(LANES)[None, :]).astype(np.float32))
    dm1, do, delta, s_ao = _attn_out_bwd(dx, m1, o32, wo, row(g_mix_post, 1), head_ind, tm, tok)
    dwo = _mm_tn(o, dm1, nj=1, a_cols=D, g_cols=D, a_by_j=False, g_by_j=False, tk=tkw, name="attn_dwo")
    dq, dk, dv, dfq, dfk = _flash_bwd(q, k, v, do, fcum, lse_aux, delta, fk_rows,
                                      dh=dh, tq=tkb, tk=tkb, s_mul=s_mul, dq_mul=scale)
    def head_cols(r):
        return jnp.pad(jnp.transpose(r[:, :, :hpg, :], (0, 2, 1, 3)).reshape(H, T).T, ((0, 0), (0, LANES - H)))

    df_k = head_cols(dfk)
    df_q = head_cols(dfq)
    dlf = _cumsum_rows(df_q, df_k, tb, True, "forget_cumsum_bwd")
    dx, h_at, df, s_ai, dbf = _attn_in_bwd(dx, x_2, row(g_mix_pre, 1), dq, dk, dv, dlf, lf, wqkv, wf, tm, H)
    dwq = _mm_tn(h_at, dq, nj=1, a_cols=D, g_cols=D, a_by_j=False, g_by_j=False, tk=tkw, name="attn_dwq")
    dwk = _mm_tn(h_at, dk, nj=1, a_cols=D, g_cols=D, a_by_j=False, g_by_j=False, tk=tkw, name="attn_dwk")
    dwv = _mm_tn(h_at, dv, nj=1, a_cols=D, g_cols=D, a_by_j=False, g_by_j=False, tk=tkw, name="attn_dwv")
    dwf = _mm_tn(h_at, df, nj=1, a_cols=D, g_cols=LANES, a_by_j=False, g_by_j=False, tk=tkw, name="attn_dwf")
    dwin = jnp.concatenate([dwq[0], dwk[0], dwv[0], dwf[0][:, :H]], axis=1)
    dwin = jnp.pad(jnp.transpose(dwin.reshape(D, N_DEV, cin), (1, 0, 2)), ((0, 0), (0, 0), (0, cin_w - cin)))
    h_attn, tok = push_grads([dwin, dwo.reshape(N_DEV, rs, D)], ["slot", "slot"], "grads_attn")

    dx, dwu0, dwd0, s_mlp0 = mlp_back(dx, n0, x_1, up0, 0, wu0, wd0, tok)
    h_mlp0, tok = push_grads([dwu0, dwd0], ["slot", "slot"], "grads_mlp0")

    dy0, dm0, z0, s_co = _conv_out_bwd(dx, m0, y0, conv_ln_g, conv_ln_b, w2, row(g_mix_post, 0), tm, tok)
    dw2 = _mm_tn(z0, dm0, nj=1, a_cols=D, g_cols=D, a_by_j=False, g_by_j=False, tk=tkw, name="conv_dw2")
    h_pw2, tok = push_grads([dw2.reshape(N_DEV, rs, D)], ["slot"], "grads_pw2")
    du0, ddw = _dwconv_bwd(dy0, u0, w32, tmc, lc, width)
    grad_x, h_cv, da0, s_ci, db1 = _conv_in_bwd(dx, du0, a0, x2, row(g_mix_pre, 0), w1g, tm, tok)
    dw1 = _mm_tn(h_cv, da0, nj=N_DEV, a_cols=D, g_cols=(2 * D) // N_DEV, a_by_j=False, g_by_j=True, tk=tkw,
                 name="conv_dw1")
    ddw_s = jnp.transpose(ddw[:width].reshape(width, N_DEV, D // N_DEV), (1, 0, 2))

    def pad_row(a):
        return jnp.pad(a, ((0, 0), (0, D - a.shape[1])))

    def pack(gmp, gmq, gfp, gfq, b1, dwb, lng, lnb, b2, bfv, last):
        return jnp.concatenate([gmp, gmq, gfp, gfq, b1.reshape(2, D), dwb, lng, lnb, b2, pad_row(bfv), last],
                               axis=0)

    zero_row = jnp.zeros((1, D), F32)
    small_g = pack(
        jnp.concatenate([row(s_ci, 0), row(s_ai, 0)], axis=0),
        jnp.concatenate([row(s_co, 0), row(s_ao, 0)], axis=0),
        jnp.concatenate([row(s_mlp0, 1), row(s_mlp1, 1)], axis=0),
        jnp.concatenate([row(s_mlp0, 0), row(s_mlp1, 0)], axis=0),
        db1, row(s_co, 4), row(s_co, 1), row(s_co, 2), row(s_co, 3), dbf[:, :H],
        pad_row(row(s_mlp1, 2)[:, 0:1]))
    h_conv, tok = push_grads([dw1, ddw_s, small_g], ["slot", "slot", "whole"], "grads_conv")

    def opt(recvs, w, m, v, name):
        shp = w.shape
        L, C = shp[0], shp[-1]
        R = int(np.prod(shp[1:-1]))
        outs = _adamw([r.reshape(N_DEV, R, r.shape[-1]) for r in recvs], w.reshape(L, R, C), m.reshape(L, R, C),
                      v.reshape(L, R, C), _row_block(R, C), name)
        return [t.reshape(shp) for t in outs]

    big = {}
    r_wu1, r_wd1 = pull_grads(h_mlp1, tok, "grads_mlp1")
    r_win, r_wo = pull_grads(h_attn, r_wd1, "grads_attn")
    big["attn_w_in"] = opt([r_win], attn_w_in, m_attn_w_in, v_attn_w_in, "adamw_win")
    big["attn_w_o"] = opt([r_wo], attn_w_o, m_attn_w_o, v_attn_w_o, "adamw_wo")
    r_wu0, r_wd0 = pull_grads(h_mlp0, big["attn_w_o"][0], "grads_mlp0")
    big["mlp_w_up"] = opt([r_wu0, r_wu1], mlp_w_up, m_mlp_w_up, v_mlp_w_up, "adamw_wup")
    big["mlp_w_down"] = opt([r_wd0, r_wd1], mlp_w_down, m_mlp_w_down, v_mlp_w_down, "adamw_wdown")
    (r_w2,) = pull_grads(h_pw2, big["mlp_w_down"][0], "grads_pw2")
    big["conv_pw2_w"] = opt([r_w2], conv_pw2_w, m_conv_pw2_w, v_conv_pw2_w, "adamw_pw2")
    r_w1, r_dw, r_small = pull_grads(h_conv, big["conv_pw2_w"][0], "grads_conv")
    big["conv_pw1_w"] = opt([r_w1], conv_pw1_w, m_conv_pw1_w, v_conv_pw1_w, "adamw_pw1")
    big["conv_dw_w"] = opt([r_dw], conv_dw_w, m_conv_dw_w, v_conv_dw_w, "adamw_dw")
    small_w = pack(g_mix_pre, g_mix_post, g_ffn_pre, g_ffn_post, conv_pw1_b, conv_dw_b, conv_ln_g, conv_ln_b,
                   conv_pw2_b, attn_b_f, zero_row)
    small_m = pack(m_g_mix_pre, m_g_mix_post, m_g_ffn_pre, m_g_ffn_post, m_conv_pw1_b, m_conv_dw_b, m_conv_ln_g,
                   m_conv_ln_b, m_conv_pw2_b, m_attn_b_f, zero_row)
    small_v = pack(v_g_mix_pre, v_g_mix_post, v_g_ffn_pre, v_g_ffn_post, v_conv_pw1_b, v_conv_dw_b, v_conv_ln_g,
                   v_conv_ln_b, v_conv_pw2_b, v_attn_b_f, zero_row)
    sm = _adamw([r_small], small_w[None], small_m[None], small_v[None], small_w.shape[0], "adamw_small")
    sm = [t[0] for t in sm]
    loss = sm[0][15, 0]

    def unpack(t):
        return {"g_mix_pre": t[0:2], "g_mix_post": t[2:4], "g_ffn_pre": t[4:6], "g_ffn_post": t[6:8],
                "conv_pw1_b": t[8:10].reshape(1, 2 * D), "conv_dw_b": t[10:11], "conv_ln_g": t[11:12],
                "conv_ln_b": t[12:13], "conv_pw2_b": t[13:14], "attn_b_f": t[14:15, :H]}

    small = [unpack(t) for t in sm]
    names = ["g_mix_pre", "g_mix_post", "g_ffn_pre", "g_ffn_post", "conv_pw1_w", "conv_pw1_b", "conv_dw_w",
             "conv_dw_b", "conv_ln_g", "conv_ln_b", "conv_pw2_w", "conv_pw2_b", "attn_w_in", "attn_b_f",
             "attn_w_o", "mlp_w_up", "mlp_w_down"]
    outs = [loss, grad_x.reshape(1, T, D)]
    for kind in range(4):
        for nme in names:
            outs.append(big[nme][kind] if nme in big else small[kind][nme])
    return tuple(outs)
```

```python
import math

import numpy as np
import jax
import jax.numpy as jnp
from jax import lax
from jax.experimental import pallas as pl
from jax.experimental.pallas import tpu as pltpu

F32 = jnp.float32
BF16 = jnp.bfloat16

RMS_EPS = 1e-6
LN_EPS = 1e-5
MASK_VALUE = -1e30
ADAM_LR = 0.001
ADAM_B1 = 0.9
ADAM_B2 = 0.999
ADAM_EPS = 1e-08
ADAM_WD = 0.01
ADAM_STEP = 10

N_DEV = 8
LANES = 128
SUBLANES = 8
CONV_HALO = 32
CONV_ROWS = 32
FLASH_ROWS = 32
VMEM_LIMIT = 56 * 1024 * 1024

_pcall = pl.pallas_call


def _params(sem=None):
    if sem is None:
        return pltpu.CompilerParams(vmem_limit_bytes=VMEM_LIMIT)
    return pltpu.CompilerParams(dimension_semantics=sem, vmem_limit_bytes=VMEM_LIMIT)


def _dot(a, b):
    return jnp.dot(a, b, preferred_element_type=F32)


def _dot_nt(a, b):
    return lax.dot_general(a, b, (((1,), (1,)), ((), ())), preferred_element_type=F32)


def _dot_tn(a, b):
    return lax.dot_general(a, b, (((0,), (0,)), ((), ())), preferred_element_type=F32)


def _full(shape):
    nd = len(shape)
    return pl.BlockSpec(shape, lambda *g: (0,) * nd, pipeline_mode=pl.Buffered(1))


def _acc(shape):
    nd = len(shape)
    return pl.BlockSpec(shape, lambda *g: (0,) * nd)


def _rows(tm, cols):
    return pl.BlockSpec((tm, cols), lambda i: (i, 0))


def _rms(x, g):
    r = lax.rsqrt(jnp.mean(x * x, axis=-1, keepdims=True) + RMS_EPS)
    return x * r * g


def _rms_bwd(x, g, dy):
    r = lax.rsqrt(jnp.mean(x * x, axis=-1, keepdims=True) + RMS_EPS)
    n = x * r
    dg = jnp.sum(dy * n, axis=0, keepdims=True)
    dn = dy * g
    dx = r * (dn - n * jnp.mean(dn * n, axis=-1, keepdims=True))
    return dx, dg


def _sigmoid(x):
    return 1.0 / (1.0 + jnp.exp(-x))


def _mesh_pos():
    return lax.axis_index("x"), lax.axis_index("y"), lax.axis_index("c")


def _dev_index(px, py, pc):
    return 4 * px + 2 * py + pc


_HBM = pl.BlockSpec(memory_space=pltpu.HBM)
_SEM = pl.BlockSpec(memory_space=pltpu.SEMAPHORE)
_EFFECT = pltpu.SideEffectType.DATAFLOW_SIDE_EFFECTING


def _peer(r, x, y, c):
    p = ((1 - x) if r & 4 else x, (1 - y) if r & 2 else y, (1 - c) if r & 1 else c)
    return p, _dev_index(*p)


def _src_ref(refs, item, me_id=None, to_id=None):
    si, sub, mode = item
    r = refs[si] if sub is None else refs[si].at[sub]
    if mode == "slot":
        return r.at[to_id]
    if mode == "own":
        return r.at[me_id]
    return r


def _place_own(srcs, items, land_shapes, name, cast=False):
    ns, n = len(srcs), len(items)

    def body(*refs):
        src = refs[:ns]
        land = refs[ns:ns + n]
        stage = refs[ns + n:ns + 2 * n] if cast else None
        sems = refs[-1]
        me_id = _dev_index(*_mesh_pos())
        cps = []
        for a, item in enumerate(items):
            s = _src_ref(src, item, to_id=me_id)
            if cast:
                if s.shape != stage[a].shape:
                    stage[a][...] = jnp.zeros_like(stage[a])
                    stage[a][:, 0:s.shape[-1]] = s[...].astype(stage[a].dtype)
                else:
                    stage[a][...] = s[...].astype(stage[a].dtype)
                s = stage[a]
            cp = pltpu.make_async_copy(s, land[a].at[me_id], sems.at[a])
            cp.start()
            cps.append(cp)
        for cp in cps:
            cp.wait()

    return _pcall(
        body, name=name,
        out_shape=[jax.ShapeDtypeStruct(tuple(s), d) for s, d in land_shapes],
        in_specs=[pl.BlockSpec(memory_space=pltpu.VMEM if cast else pl.ANY)] * ns,
        out_specs=[pl.BlockSpec(memory_space=pl.ANY)] * n,
        scratch_shapes=([pltpu.VMEM(tuple(s[1:]), d) for s, d in land_shapes] if cast else [])
        + [pltpu.SemaphoreType.DMA((n,))],
        compiler_params=pltpu.CompilerParams(vmem_limit_bytes=VMEM_LIMIT),
    )(*srcs)


def _seed_lands(srcs, modes, me, name):
    n = len(srcs)
    parts = [tuple(s.shape[1:]) if mode == "slot" else tuple(s.shape) for s, mode in zip(srcs, modes)]

    def body(me_ref, *refs):
        s = pl.program_id(0)
        for a in range(n):
            v = refs[a][...]
            refs[n + a][...] = jnp.where(s == me_ref[0], v, jnp.zeros_like(v))

    def in_spec(part, mode):
        nd = len(part)
        if mode == "slot":
            return pl.BlockSpec((None,) + part, lambda s, me_ref: (me_ref[0],) + (0,) * nd)
        return pl.BlockSpec(part, lambda s, me_ref: (0,) * nd)

    def out_spec(part):
        nd = len(part)
        return pl.BlockSpec((None,) + part, lambda s, me_ref: (s,) + (0,) * nd)

    return _pcall(
        body, name=name,
        grid_spec=pltpu.PrefetchScalarGridSpec(
            num_scalar_prefetch=1, grid=(N_DEV,),
            in_specs=[in_spec(p, m) for p, m in zip(parts, modes)],
            out_specs=[out_spec(p) for p in parts]),
        out_shape=[jax.ShapeDtypeStruct((N_DEV,) + p, s.dtype) for p, s in zip(parts, srcs)],
        compiler_params=_params(("arbitrary",)),
    )(me, *srcs)


def _push_start(srcs, lands, groups, name):
    ns, n = len(srcs), len(lands)
    ng = len(groups)
    assert sum(len(g) for g in groups) == n

    def body(*refs):
        ops = refs[:ns + n]
        land = refs[ns:ns + n]
        sems = refs[ns + n:ns + n + 2 * ng]
        token = refs[-1]
        x, y, c = _mesh_pos()
        me_id = _dev_index(x, y, c)
        a = 0
        for gi, grp in enumerate(groups):
            for k, item in enumerate(grp):
                for r in range(1, N_DEV):
                    p, pid = _peer(r, x, y, c)
                    pltpu.make_async_remote_copy(
                        src_ref=_src_ref(ops, item, me_id=me_id, to_id=pid), dst_ref=land[a].at[me_id],
                        send_sem=sems[2 * gi].at[k * (N_DEV - 1) + r - 1],
                        recv_sem=sems[2 * gi + 1].at[k * (N_DEV - 1) + r - 1],
                        device_id=p, device_id_type=pl.DeviceIdType.MESH).start()
                a += 1
        token[...] = jnp.zeros_like(token)

    sem_shapes = []
    for grp in groups:
        sem_shapes += [pltpu.SemaphoreType.DMA((len(grp) * (N_DEV - 1),))] * 2
    arrs = list(srcs) + list(lands)
    res = _pcall(
        body, name=name,
        out_shape=tuple(sem_shapes) + tuple(pltpu.HBM(a.shape, a.dtype) for a in arrs)
        + (jax.ShapeDtypeStruct((SUBLANES, LANES), F32),),
        in_specs=[_HBM] * (ns + n),
        out_specs=tuple([_SEM] * (2 * ng)) + tuple([_HBM] * (ns + n)) + (pl.BlockSpec(memory_space=pltpu.VMEM),),
        input_output_aliases={i: 2 * ng + i for i in range(ns + n)},
        compiler_params=pltpu.CompilerParams(has_side_effects=_EFFECT),
    )(*[pltpu.with_memory_space_constraint(a, pltpu.HBM) for a in arrs])
    sems = [(res[2 * gi], res[2 * gi + 1]) for gi in range(ng)]
    thru = res[2 * ng:2 * ng + ns + n]
    return sems, list(thru[:ns]), list(thru[ns:]), res[-1]


def _push_wait(sems, srcs, lands, group, after, name):
    ns, n = len(srcs), len(lands)
    assert len(group) == n

    def body(*refs):
        ops = refs[:ns + n]
        land = refs[ns:ns + n]
        send_sems, recv_sems = refs[ns + n], refs[ns + n + 1]
        x, y, c = _mesh_pos()
        me_id = _dev_index(x, y, c)
        for k, item in enumerate(group):
            for r in range(1, N_DEV):
                p, pid = _peer(r, x, y, c)
                cp = pltpu.make_async_remote_copy(
                    src_ref=_src_ref(ops, item, me_id=me_id, to_id=pid), dst_ref=land[k].at[pid],
                    send_sem=send_sems.at[k * (N_DEV - 1) + r - 1], recv_sem=recv_sems.at[k * (N_DEV - 1) + r - 1],
                    device_id=p, device_id_type=pl.DeviceIdType.MESH)
                cp.wait_send()
                cp.wait_recv()

    arrs = list(srcs) + list(lands)
    res = _pcall(
        body, name=name,
        out_shape=tuple(pltpu.HBM(a.shape, a.dtype) for a in arrs),
        in_specs=[_HBM] * (ns + n) + [_SEM, _SEM, pl.BlockSpec(memory_space=pl.ANY)],
        out_specs=tuple([_HBM] * (ns + n)),
        input_output_aliases={i: i for i in range(ns + n)},
        compiler_params=pltpu.CompilerParams(has_side_effects=_EFFECT),
    )(*arrs, sems[0], sems[1], after)
    return list(res[ns:])


def _conv_in_fwd(x, g_pre, w1g, b1, tm):
    T, D = x.shape
    ns, _, cs = w1g.shape
    half = ns // 2

    def body(x_ref, g_ref, w_ref, b_ref, a_ref, u_ref):
        h = _rms(x_ref[...], g_ref[...]).astype(BF16)
        parts = []
        for s in range(ns):
            a_s = _dot(h, w_ref[s]) + b_ref[:, s * cs:(s + 1) * cs]
            a_ref[:, s * cs:(s + 1) * cs] = a_s
            parts.append(a_s)
        for s in range(half):
            u_ref[:, s * cs:(s + 1) * cs] = parts[s] * _sigmoid(parts[s + half])

    return _pcall(
        body, name="conv_in_fwd", grid=(T // tm,),
        in_specs=[_rows(tm, D), _full((1, D)), _full(w1g.shape), _full((1, 2 * D))],
        out_specs=[_rows(tm, 2 * D), _rows(tm, D)],
        out_shape=[jax.ShapeDtypeStruct((T, 2 * D), F32), jax.ShapeDtypeStruct((T, D), F32)],
        compiler_params=_params(("arbitrary",)),
    )(x, g_pre, w1g, b1)


def _shifted_copies(ext_ref, sh_ref, tm):
    n = tm + CONV_HALO - SUBLANES
    for b in range(1, SUBLANES):
        sh_ref[b - 1, 0:n, :] = ext_ref[b:b + n, :]


def _shifted_rows(ext_ref, sh_ref, off, r0, ls):
    b = off % SUBLANES
    a8 = off - b + r0
    src = ext_ref if b == 0 else sh_ref.at[b - 1]
    return src[a8:a8 + CONV_ROWS, ls]


def _dwconv_fwd(u, w32, b, tm, lc, width):
    T, D = u.shape
    hb = tm // CONV_HALO

    def body(u_ref, halo_ref, w_ref, b_ref, y_ref, ext_ref, sh_ref):
        i = pl.program_id(0)
        ext_ref[0:CONV_HALO, :] = jnp.where(i > 0, halo_ref[...], 0.0)
        ext_ref[CONV_HALO:, :] = u_ref[...]
        _shifted_copies(ext_ref, sh_ref, tm)
        for r0 in range(0, tm, CONV_ROWS):
            for l0 in range(0, lc, LANES):
                ls = slice(l0, l0 + LANES)
                acc = jnp.zeros((CONV_ROWS, LANES), F32) + b_ref[:, ls]
                for j in range(width):
                    off = CONV_HALO - (width - 1) + j
                    acc = acc + w_ref[j:j + 1, ls] * _shifted_rows(ext_ref, sh_ref, off, r0, ls)
                y_ref[r0:r0 + CONV_ROWS, ls] = acc

    return _pcall(
        body, name="dwconv_fwd", grid=(T // tm, D // lc),
        in_specs=[pl.BlockSpec((tm, lc), lambda i, l: (i, l)),
                  pl.BlockSpec((CONV_HALO, lc), lambda i, l: (jnp.maximum(i * hb - 1, 0), l)),
                  pl.BlockSpec((32, lc), lambda i, l: (0, l)),
                  pl.BlockSpec((1, lc), lambda i, l: (0, l))],
        out_specs=pl.BlockSpec((tm, lc), lambda i, l: (i, l)),
        out_shape=jax.ShapeDtypeStruct((T, D), F32),
        scratch_shapes=[pltpu.VMEM((tm + CONV_HALO, lc), F32),
                        pltpu.VMEM((SUBLANES - 1, tm + CONV_HALO, lc), F32)],
        compiler_params=_params(("arbitrary", "arbitrary")),
    )(u, u, w32, b)


def _ln_parts(y, g, b):
    mu = jnp.mean(y, axis=-1, keepdims=True)
    yc = y - mu
    rstd = lax.rsqrt(jnp.mean(yc * yc, axis=-1, keepdims=True) + LN_EPS)
    yhat = yc * rstd
    return yhat, rstd, yhat * g + b


def _conv_out_fwd(y, x, ln_g, ln_b, w2, b2, g_post, tm):
    T, D = x.shape

    def body(y_ref, x_ref, lg_ref, lb_ref, w_ref, b_ref, g_ref, m_ref, xo_ref):
        _, _, yn = _ln_parts(y_ref[...], lg_ref[...], lb_ref[...])
        z = (yn * _sigmoid(yn)).astype(BF16)
        m = _dot(z, w_ref[...]) + b_ref[...]
        m_ref[...] = m
        xo_ref[...] = x_ref[...] + _rms(m, g_ref[...])

    return _pcall(
        body, name="conv_out_fwd", grid=(T // tm,),
        in_specs=[_rows(tm, D), _rows(tm, D), _full((1, D)), _full((1, D)), _full((D, D)), _full((1, D)),
                  _full((1, D))],
        out_specs=[_rows(tm, D), _rows(tm, D)],
        out_shape=[jax.ShapeDtypeStruct((T, D), F32), jax.ShapeDtypeStruct((T, D), F32)],
        compiler_params=_params(("arbitrary",)),
    )(y, x, ln_g, ln_b, w2, b2, g_post)


def _conv_out_bwd(dxo, m, y, ln_g, ln_b, w2, g_post, tm, dep):
    T, D = m.shape

    def body(dxo_ref, m_ref, y_ref, lg_ref, lb_ref, w_ref, g_ref, dep_ref, dy_ref, dm_ref, z_ref, sums_ref):
        i = pl.program_id(0)
        dm, dgpost = _rms_bwd(m_ref[...], g_ref[...], dxo_ref[...])
        dmb = dm.astype(BF16)
        dm_ref[...] = dmb
        yhat, rstd, yn = _ln_parts(y_ref[...], lg_ref[...], lb_ref[...])
        sg = _sigmoid(yn)
        z_ref[...] = (yn * sg).astype(BF16)
        dz = _dot_nt(dmb, w_ref[...])
        dyn = dz * (sg + yn * sg * (1.0 - sg))
        dyh = dyn * lg_ref[...]
        dy = rstd * (dyh - jnp.mean(dyh, axis=-1, keepdims=True)
                     - yhat * jnp.mean(dyh * yhat, axis=-1, keepdims=True))
        dy_ref[...] = dy

        @pl.when(i == 0)
        def _():
            sums_ref[...] = jnp.zeros_like(sums_ref)
        sums_ref[0:1, :] += dgpost
        sums_ref[1:2, :] += jnp.sum(dyn * yhat, axis=0, keepdims=True)
        sums_ref[2:3, :] += jnp.sum(dyn, axis=0, keepdims=True)
        sums_ref[3:4, :] += jnp.sum(dm, axis=0, keepdims=True)
        sums_ref[4:5, :] += jnp.sum(dy, axis=0, keepdims=True)

    return _pcall(
        body, name="conv_out_bwd", grid=(T // tm,),
        in_specs=[_rows(tm, D), _rows(tm, D), _rows(tm, D), _full((1, D)), _full((1, D)), _full((D, D)),
                  _full((1, D)), pl.BlockSpec(memory_space=pl.ANY)],
        out_specs=[_rows(tm, D), _rows(tm, D), _rows(tm, D), _acc((SUBLANES, D))],
        out_shape=[jax.ShapeDtypeStruct((T, D), F32), jax.ShapeDtypeStruct((T, D), BF16),
                   jax.ShapeDtypeStruct((T, D), BF16), jax.ShapeDtypeStruct((SUBLANES, D), F32)],
        compiler_params=_params(("arbitrary",)),
    )(dxo, m, y, ln_g, ln_b, w2, g_post, dep)


def _dwconv_bwd(dy, u, w32, tm, lc, width):
    T, D = u.shape
    hb = tm // CONV_HALO
    nt = T // tm
    last_halo = T // CONV_HALO - 1

    def body(dy_ref, dyn_ref, u_ref, up_ref, w_ref, du_ref, dw_ref, exty_ref, extu_ref, acc_ref, shy_ref, shu_ref):
        i = pl.program_id(1)
        exty_ref[0:tm, :] = dy_ref[...]
        exty_ref[tm:, :] = jnp.where(i < nt - 1, dyn_ref[...], 0.0)
        extu_ref[0:CONV_HALO, :] = jnp.where(i > 0, up_ref[...], 0.0)
        extu_ref[CONV_HALO:, :] = u_ref[...]
        _shifted_copies(exty_ref, shy_ref, tm)
        _shifted_copies(extu_ref, shu_ref, tm)

        @pl.when(i == 0)
        def _():
            acc_ref[...] = jnp.zeros_like(acc_ref)

        for r0 in range(0, tm, CONV_ROWS):
            for l0 in range(0, lc, LANES):
                ls = slice(l0, l0 + LANES)
                dyc = exty_ref[r0:r0 + CONV_ROWS, ls]
                du = jnp.zeros((CONV_ROWS, LANES), F32)
                for j in range(width):
                    du = du + w_ref[j:j + 1, ls] * _shifted_rows(exty_ref, shy_ref, (width - 1) - j, r0, ls)
                    prod = dyc * _shifted_rows(extu_ref, shu_ref, CONV_HALO - (width - 1) + j, r0, ls)
                    acc_ref[j, :, ls] += prod.reshape(CONV_ROWS // SUBLANES, SUBLANES, LANES).sum(axis=0)
                du_ref[r0:r0 + CONV_ROWS, ls] = du

        @pl.when(i == nt - 1)
        def _():
            for j in range(32):
                dw_ref[j:j + 1, :] = jnp.sum(acc_ref[j], axis=0, keepdims=True)

    return _pcall(
        body, name="dwconv_bwd", grid=(D // lc, nt),
        in_specs=[pl.BlockSpec((tm, lc), lambda l, i: (i, l)),
                  pl.BlockSpec((CONV_HALO, lc), lambda l, i: (jnp.minimum((i + 1) * hb, last_halo), l)),
                  pl.BlockSpec((tm, lc), lambda l, i: (i, l)),
                  pl.BlockSpec((CONV_HALO, lc), lambda l, i: (jnp.maximum(i * hb - 1, 0), l)),
                  pl.BlockSpec((32, lc), lambda l, i: (0, l))],
        out_specs=[pl.BlockSpec((tm, lc), lambda l, i: (i, l)),
                   pl.BlockSpec((32, lc), lambda l, i: (0, l))],
        out_shape=[jax.ShapeDtypeStruct((T, D), F32), jax.ShapeDtypeStruct((32, D), F32)],
        scratch_shapes=[pltpu.VMEM((tm + CONV_HALO, lc), F32), pltpu.VMEM((tm + CONV_HALO, lc), F32),
                        pltpu.VMEM((32, SUBLANES, lc), F32),
                        pltpu.VMEM((SUBLANES - 1, tm + CONV_HALO, lc), F32),
                        pltpu.VMEM((SUBLANES - 1, tm + CONV_HALO, lc), F32)],
        compiler_params=_params(("arbitrary", "arbitrary")),
    )(dy, dy, u, u, w32)


def _conv_in_bwd(dxo, du, a, x, g_pre, w1g, tm, dep):
    T, D = x.shape
    ns, _, cs = w1g.shape
    half = ns // 2

    def body(dxo_ref, du_ref, a_ref, x_ref, g_ref, w_ref, dep_ref, dxi_ref, h_ref, da_ref, sums_ref, db_ref):
        i = pl.program_id(0)
        xv = x_ref[...]
        h_ref[...] = _rms(xv, g_ref[...]).astype(BF16)
        dh = jnp.zeros((tm, D), F32)
        dbs = [None] * ns
        for s in range(half):
            a_u = a_ref[:, s * cs:(s + 1) * cs]
            sg = _sigmoid(a_ref[:, (s + half) * cs:(s + half + 1) * cs])
            du_s = du_ref[:, s * cs:(s + 1) * cs]
            da_u = du_s * sg
            da_g = du_s * a_u * sg * (1.0 - sg)
            for s2, v in ((s, da_u), (s + half, da_g)):
                vb = v.astype(BF16)
                da_ref[:, s2 * cs:(s2 + 1) * cs] = vb
                dbs[s2] = jnp.sum(v, axis=0, keepdims=True)
                dh = dh + _dot_nt(vb, w_ref[s2])
        dxi, dgpre = _rms_bwd(xv, g_ref[...], dh)
        dxi_ref[...] = dxo_ref[...] + dxi

        @pl.when(i == 0)
        def _():
            sums_ref[...] = jnp.zeros_like(sums_ref)
            db_ref[...] = jnp.zeros_like(db_ref)
        sums_ref[0:1, :] += dgpre
        for s in range(ns):
            db_ref[:, s * cs:(s + 1) * cs] += dbs[s]

    return _pcall(
        body, name="conv_in_bwd", grid=(T // tm,),
        in_specs=[_rows(tm, D), _rows(tm, D), _rows(tm, 2 * D), _rows(tm, D), _full((1, D)),
                  _full(w1g.shape), pl.BlockSpec(memory_space=pl.ANY)],
        out_specs=[_rows(tm, D), _rows(tm, D), _rows(tm, 2 * D), _acc((SUBLANES, D)), _acc((1, 2 * D))],
        out_shape=[jax.ShapeDtypeStruct((T, D), F32), jax.ShapeDtypeStruct((T, D), BF16),
                   jax.ShapeDtypeStruct((T, 2 * D), BF16), jax.ShapeDtypeStruct((SUBLANES, D), F32),
                   jax.ShapeDtypeStruct((1, 2 * D), F32)],
        compiler_params=_params(("arbitrary",)),
    )(dxo, du, a, x, g_pre, w1g, dep)


def _mlp_fwd(x, g_pre, wug, wdg, g_post, tm, name):
    T, D = x.shape
    ns, _, fs = wug.shape

    def body(x_ref, gp_ref, wu_ref, wd_ref, gq_ref, up_ref, m_ref, xo_ref):
        xv = x_ref[...]
        h = _rms(xv, gp_ref[...]).astype(BF16)
        acc = jnp.zeros((tm, D), F32)
        for s in range(ns):
            up = _dot(h, wu_ref[s]).astype(BF16)
            up_ref[:, s * fs:(s + 1) * fs] = up
            act = jnp.square(jnp.maximum(up.astype(F32), 0.0)).astype(BF16)
            acc = acc + _dot(act, wd_ref[s])
        m_ref[...] = acc
        xo_ref[...] = xv + _rms(acc, gq_ref[...])

    return _pcall(
        body, name=name, grid=(T // tm,),
        in_specs=[_rows(tm, D), _full((1, D)), _full(wug.shape), _full(wdg.shape), _full((1, D))],
        out_specs=[_rows(tm, ns * fs), _rows(tm, D), _rows(tm, D)],
        out_shape=[jax.ShapeDtypeStruct((T, ns * fs), BF16), jax.ShapeDtypeStruct((T, D), F32),
                   jax.ShapeDtypeStruct((T, D), F32)],
        compiler_params=_params(("arbitrary",)),
    )(x, g_pre, wug, wdg, g_post)


def _mlp_bwd(dxo, m, x, up, g_pre, wug, wdg, g_post, tm, name, dep, from_loss=False):
    T, D = x.shape
    ns, _, fs = wug.shape

    def body(dxo_ref, m_ref, x_ref, up_ref, gp_ref, wu_ref, wd_ref, gq_ref, dep_ref,
             dxi_ref, h_ref, dm_ref, dup_ref, sums_ref):
        i = pl.program_id(0)
        if from_loss:
            err = dxo_ref[...] - dep_ref[...]
            dxo = err * (1.0 / D)
        else:
            dxo = dxo_ref[...]
        dm, dgpost = _rms_bwd(m_ref[...], gq_ref[...], dxo)
        dmb = dm.astype(BF16)
        dm_ref[...] = dmb
        xv = x_ref[...]
        h_ref[...] = _rms(xv, gp_ref[...]).astype(BF16)
        dh = jnp.zeros((tm, D), F32)
        for s in range(ns):
            dact = _dot_nt(dmb, wd_ref[s])
            up = up_ref[:, s * fs:(s + 1) * fs].astype(F32)
            dup = (dact * (2.0 * jnp.maximum(up, 0.0))).astype(BF16)
            dup_ref[:, s * fs:(s + 1) * fs] = dup
            dh = dh + _dot_nt(dup, wu_ref[s])
        dxi, dgpre = _rms_bwd(xv, gp_ref[...], dh)
        dxi_ref[...] = dxo + dxi

        @pl.when(i == 0)
        def _():
            sums_ref[...] = jnp.zeros_like(sums_ref)
        sums_ref[0:1, :] += dgpost
        sums_ref[1:2, :] += dgpre
        if from_loss:
            sums_ref[2:3, :] += 0.5 * jnp.sum(jnp.mean(err * err, axis=-1, keepdims=True), axis=0, keepdims=True)

    return _pcall(
        body, name=name, grid=(T // tm,),
        in_specs=[_rows(tm, D), _rows(tm, D), _rows(tm, D), _rows(tm, ns * fs), _full((1, D)),
                  _full(wug.shape), _full(wdg.shape), _full((1, D)),
                  _rows(tm, D) if from_loss else pl.BlockSpec(memory_space=pl.ANY)],
        out_specs=[_rows(tm, D), _rows(tm, D), _rows(tm, D), _rows(tm, ns * fs), _acc((SUBLANES, D))],
        out_shape=[jax.ShapeDtypeStruct((T, D), F32), jax.ShapeDtypeStruct((T, D), BF16),
                   jax.ShapeDtypeStruct((T, D), BF16), jax.ShapeDtypeStruct((T, ns * fs), BF16),
                   jax.ShapeDtypeStruct((SUBLANES, D), F32)],
        compiler_params=_params(("arbitrary",)),
    )(dxo, m, x, up, g_pre, wug, wdg, g_post, dep)


def _mm_tn(a, g, *, nj, a_cols, g_cols, a_by_j, g_by_j, tk, name, act=False):
    T = a.shape[0]
    nk = T // tk

    def body(a_ref, g_ref, o_ref, acc_ref):
        k = pl.program_id(1)
        av = a_ref[...]
        if act:
            av = jnp.square(jnp.maximum(av.astype(F32), 0.0)).astype(BF16)
        p = _dot_tn(av, g_ref[...])

        @pl.when(k == 0)
        def _():
            acc_ref[...] = p

        @pl.when(k > 0)
        def _():
            acc_ref[...] += p

        @pl.when(k == nk - 1)
        def _():
            o_ref[...] = acc_ref[...].astype(BF16)

    return _pcall(
        body, name=name, grid=(nj, nk),
        in_specs=[pl.BlockSpec((tk, a_cols), (lambda j, k: (k, j)) if a_by_j else (lambda j, k: (k, 0))),
                  pl.BlockSpec((tk, g_cols), (lambda j, k: (k, j)) if g_by_j else (lambda j, k: (k, 0)))],
        out_specs=pl.BlockSpec((None, a_cols, g_cols), lambda j, k: (j, 0, 0)),
        out_shape=jax.ShapeDtypeStruct((nj, a_cols, g_cols), BF16),
        scratch_shapes=[pltpu.VMEM((a_cols, g_cols), F32)],
        compiler_params=_params(("arbitrary", "arbitrary")),
    )(a, g)


def _attn_in_fwd(x, g_pre, wqkv, wf, bf, tm, q_mul, n_heads):
    T, D = x.shape

    def body(x_ref, g_ref, w_ref, wf_ref, bf_ref, q_ref, k_ref, v_ref, lf_ref):
        h = _rms(x_ref[...], g_ref[...]).astype(BF16)
        q = _dot(h, w_ref[:, 0:D])
        if q_mul != 1.0:
            q = q * q_mul
        q_ref[...] = q.astype(BF16)
        k_ref[...] = _dot(h, w_ref[:, D:2 * D]).astype(BF16)
        v_ref[...] = _dot(h, w_ref[:, 2 * D:3 * D]).astype(BF16)
        fl = _dot(h, wf_ref[...]) + bf_ref[...]
        lf = jnp.minimum(fl, 0.0) - jnp.log(1.0 + jnp.exp(-jnp.abs(fl)))
        lane = lax.broadcasted_iota(jnp.int32, (1, LANES), 1)
        lf_ref[...] = jnp.where(lane < n_heads, lf, 0.0)

    return _pcall(
        body, name="attn_in_fwd", grid=(T // tm,),
        in_specs=[_rows(tm, D), _full((1, D)), _full((D, 3 * D)), _full((D, LANES)), _full((1, LANES))],
        out_specs=[_rows(tm, D), _rows(tm, D), _rows(tm, D), _rows(tm, LANES)],
        out_shape=[jax.ShapeDtypeStruct((T, D), BF16)] * 3 + [jax.ShapeDtypeStruct((T, LANES), F32)],
        compiler_params=_params(("arbitrary",)),
    )(x, g_pre, wqkv, wf, bf)


def _cumsum_rows(v, v2, tb, reverse, name):
    T, C = v.shape
    nb = T // tb

    def body(v_ref, v2_ref, o_ref, carry_ref):
        i = pl.program_id(0)

        @pl.when(i == 0)
        def _():
            carry_ref[...] = jnp.zeros_like(carry_ref)
        r = lax.broadcasted_iota(jnp.int32, (tb, tb), 0)
        c = lax.broadcasted_iota(jnp.int32, (tb, tb), 1)
        tri = jnp.where((c >= r) if reverse else (c <= r), 1.0, 0.0).astype(F32)
        out = jnp.dot(tri, v_ref[...] + v2_ref[...], precision=lax.Precision.HIGHEST,
                      preferred_element_type=F32) + carry_ref[...]
        o_ref[...] = out
        carry_ref[...] = out[0:1, :] if reverse else out[tb - 1:tb, :]

    idx = (lambda i: (nb - 1 - i, 0)) if reverse else (lambda i: (i, 0))
    return _pcall(
        body, name=name, grid=(nb,),
        in_specs=[pl.BlockSpec((tb, C), idx), pl.BlockSpec((tb, C), idx)],
        out_specs=pl.BlockSpec((tb, C), idx),
        out_shape=jax.ShapeDtypeStruct((T, C), F32),
        scratch_shapes=[pltpu.VMEM((1, C), F32)],
        compiler_params=_params(("arbitrary",)),
    )(v, v2)


def _head_col(v, lane, h):
    return jnp.sum(jnp.where(lane == h, v, 0.0), axis=1, keepdims=True)


def _flash_fwd(q, k, v, fq_aux, fk_rows, *, dh, tq, s_mul):
    T, D = q.shape
    G = D // LANES
    hpg = LANES // dh
    nq = T // tq
    k3 = k.reshape(nq, tq, D)
    v3 = v.reshape(nq, tq, D)

    rc = min(FLASH_ROWS, tq)

    def body(q_ref, k_ref, v_ref, fq_ref, fk_ref, o_ref, o32_ref, lse_ref, s_scr, p_scr):
        i = pl.program_id(1)
        lane = lax.broadcasted_iota(jnp.int32, (1, LANES), 1)
        q2 = q_ref[...]
        hmasks = [(lane >= hh * dh) & (lane < (hh + 1) * dh) for hh in range(hpg)]
        qms = [jnp.where(hm, q2, jnp.zeros_like(q2)) for hm in hmasks]

        nlb = tq // LANES
        sum_lane = [((hh + 1) % hpg) * dh for hh in range(hpg)]

        def scores(j, slot):
            kj = k_ref[j]
            for hh in range(hpg):
                s = _dot_nt(qms[hh], kj)
                s_scr[slot, hh] = s if s_mul == 1.0 else s * s_mul

        def soft(j, slot, carry, masked):
            vj = v_ref[j]
            out = []
            for hh in range(hpg):
                m_b, acc = carry[hh]
                fk_row = fk_ref[j, hh:hh + 1, :]
                def future(r0, cb):
                    return masked and cb * LANES > r0 + rc - 1

                mx = []
                for r0 in range(0, tq, rc):
                    rs = slice(r0, r0 + rc)
                    c = None
                    for cb in range(nlb):
                        if future(r0, cb):
                            continue
                        cs = slice(cb * LANES, (cb + 1) * LANES)
                        s = s_scr[slot, hh, rs, cs] - fk_row[:, cs]
                        if masked and (cb + 1) * LANES - 1 > r0:
                            ri = r0 + lax.broadcasted_iota(jnp.int32, (rc, LANES), 0)
                            ci = cb * LANES + lax.broadcasted_iota(jnp.int32, (rc, LANES), 1)
                            s = jnp.where(ci <= ri, s, MASK_VALUE)
                        s_scr[slot, hh, rs, cs] = s
                        c = s if c is None else jnp.maximum(c, s)
                    mx.append(c)
                row_max = jnp.max(jnp.concatenate(mx, axis=0), axis=1, keepdims=True)
                m_new = jnp.maximum(m_b, row_max)
                alpha = jnp.exp(m_b - m_new)
                for r0 in range(0, tq, rc):
                    rs = slice(r0, r0 + rc)
                    m_c = m_new[rs]
                    for cb in range(nlb):
                        cs = slice(cb * LANES, (cb + 1) * LANES)
                        if future(r0, cb):
                            p_scr[slot, hh, rs, cs] = jnp.zeros((rc, LANES), BF16)
                        else:
                            p_scr[slot, hh, rs, cs] = jnp.exp(s_scr[slot, hh, rs, cs] - m_c).astype(BF16)
                v_one = jnp.where(hmasks[hh], vj, jnp.ones_like(vj))
                out.append((m_new, alpha * acc + _dot(p_scr[slot, hh], v_one)))
            return tuple(out)

        def step(j, carry, masked, slot):
            scores(j, slot)
            return soft(j, slot, carry, masked)

        def pair(t, cr):
            scores(2 * t, 0)
            scores(2 * t + 1, 1)
            return soft(2 * t + 1, 1, soft(2 * t, 0, cr, False), False)

        init = tuple((jnp.full((tq, LANES), MASK_VALUE, F32), jnp.zeros((tq, LANES), F32)) for _ in range(hpg))
        carry = lax.fori_loop(0, i // 2, pair, init)
        carry = lax.cond(i % 2 == 1, lambda cr: step(i - 1, cr, False, 0), lambda cr: cr, carry)
        carry = step(i, carry, True, 1)
        fq = fq_ref[...]
        g0 = pl.program_id(0) * hpg
        o_all = jnp.zeros((tq, LANES), F32)
        lse_all = jnp.zeros((tq, LANES), F32)
        for hh in range(hpg):
            m_b, acc = carry[hh]
            l = acc[:, sum_lane[hh]:sum_lane[hh] + 1]
            o_all = jnp.where(hmasks[hh], acc * (1.0 / l), o_all)
            lse_all = jnp.where(lane == hh, m_b[:, 0:1] + jnp.log(l) + _head_col(fq, lane, g0 + hh), lse_all)
        o_ref[...] = o_all.astype(BF16)
        o32_ref[...] = o_all
        lse_ref[...] = lse_all

    return _pcall(
        body, name="flash_fwd", grid=(G, nq),
        in_specs=[pl.BlockSpec((tq, LANES), lambda g, i: (i, g)),
                  pl.BlockSpec((nq, tq, LANES), lambda g, i: (0, 0, g)),
                  pl.BlockSpec((nq, tq, LANES), lambda g, i: (0, 0, g)),
                  pl.BlockSpec((tq, LANES), lambda g, i: (i, 0)),
                  pl.BlockSpec((None, nq, SUBLANES, tq), lambda g, i: (g, 0, 0, 0))],
        out_specs=[pl.BlockSpec((tq, LANES), lambda g, i: (i, g)),
                   pl.BlockSpec((tq, LANES), lambda g, i: (i, g)),
                   pl.BlockSpec((None, tq, LANES), lambda g, i: (g, i, 0))],
        out_shape=[jax.ShapeDtypeStruct((T, D), BF16), jax.ShapeDtypeStruct((T, D), F32),
                   jax.ShapeDtypeStruct((G, T, LANES), F32)],
        scratch_shapes=[pltpu.VMEM((2, hpg, tq, tq), F32), pltpu.VMEM((2, hpg, tq, tq), BF16)],
        compiler_params=_params(("arbitrary", "arbitrary")),
    )(q, k3, v3, fq_aux, fk_rows)


def _flash_bwd(q, k, v, do, fq_aux, lse_aux, dl_aux, fk_rows, *, dh, tq, tk, s_mul, dq_mul):
    T, D = q.shape
    G = D // LANES
    hpg = LANES // dh
    nq = T // tq
    nk = T // tk
    per = tq // tk
    k3 = k.reshape(nk, tk, D)
    v3 = v.reshape(nk, tk, D)
    rc = min(FLASH_ROWS, tq)

    def body(q_ref, k_ref, v_ref, do_ref, fq_ref, lse_ref, dl_ref, fk_ref,
             dq_ref, dk_ref, dv_ref, dfq_ref, dfk_ref, dk_acc, dv_acc, s_scr, dp_scr, p_scr, ds_scr):
        i = pl.program_id(1)

        @pl.when(i == 0)
        def _():
            dk_acc[...] = jnp.zeros_like(dk_acc)
            dv_acc[...] = jnp.zeros_like(dv_acc)
            dfk_ref[...] = jnp.zeros_like(dfk_ref)

        lane = lax.broadcasted_iota(jnp.int32, (1, LANES), 1)
        q2 = q_ref[...]
        do2 = do_ref[...]
        g0 = pl.program_id(0) * hpg
        fq = fq_ref[...]
        lse = lse_ref[...]
        dl = dl_ref[...]
        hmasks = [(lane >= hh * dh) & (lane < (hh + 1) * dh) for hh in range(hpg)]
        qms = [jnp.where(hm, q2, jnp.zeros_like(q2)) for hm in hmasks]
        doms = [jnp.where(hm, do2, jnp.zeros_like(do2)) for hm in hmasks]
        q2t = q2.astype(F32).T.astype(BF16)
        do2t = do2.astype(F32).T.astype(BF16)
        ones_rows = jnp.ones((2 * SUBLANES, tk), BF16)
        q1t = [jnp.concatenate([q2t[hh * dh:(hh + 1) * dh], jnp.ones((2 * SUBLANES, tq), BF16)], axis=0)
               for hh in range(hpg)]
        c_bs = [jnp.broadcast_to(_head_col(fq, lane, g0 + hh) - lse[:, hh:hh + 1], (tq, LANES))
                for hh in range(hpg)]
        dl_bs = [jnp.broadcast_to(_head_col(dl, lane, g0 + hh), (tq, LANES)) for hh in range(hpg)]
        nlb = tk // LANES

        def step(j, carry, off, slot=0):
            masked = off is not None
            kj = k_ref[j]
            vj = v_ref[j]
            kjt = kj.astype(F32).T.astype(BF16)
            for hh in range(hpg):
                s = _dot_nt(qms[hh], kj)
                s_scr[slot, hh] = s if s_mul == 1.0 else s * s_mul
                dp_scr[slot, hh] = _dot_nt(doms[hh], vj)
            out = []
            for hh in range(hpg):
                fk_row = fk_ref[j, hh:hh + 1, :]
                for r0 in range(0, tq, rc):
                    rs = slice(r0, r0 + rc)
                    if masked and r0 + rc <= off:
                        p_scr[slot, hh, rs, :] = jnp.zeros((rc, tk), BF16)
                        ds_scr[slot, hh, rs, :] = jnp.zeros((rc, tk), BF16)
                        continue
                    c_c = c_bs[hh][rs]
                    dl_c = dl_bs[hh][rs]
                    for cb in range(nlb):
                        cs = slice(cb * LANES, (cb + 1) * LANES)
                        if masked and off + cb * LANES > r0 + rc - 1:
                            p_scr[slot, hh, rs, cs] = jnp.zeros((rc, LANES), BF16)
                            ds_scr[slot, hh, rs, cs] = jnp.zeros((rc, LANES), BF16)
                            continue
                        e = (s_scr[slot, hh, rs, cs] - fk_row[:, cs]) + c_c
                        if masked and off + (cb + 1) * LANES - 1 > r0:
                            ri = r0 + lax.broadcasted_iota(jnp.int32, (rc, LANES), 0)
                            ci = off + cb * LANES + lax.broadcasted_iota(jnp.int32, (rc, LANES), 1)
                            e = jnp.where(ci <= ri, e, MASK_VALUE)
                        p = jnp.exp(e)
                        ds = p * (dp_scr[slot, hh, rs, cs] - dl_c)
                        p_scr[slot, hh, rs, cs] = p.astype(BF16)
                        ds_scr[slot, hh, rs, cs] = ds.astype(BF16)
                k1t = jnp.concatenate([kjt[hh * dh:(hh + 1) * dh], ones_rows], axis=0)
                out.append(carry[hh] + _dot_nt(k1t, ds_scr[slot, hh]))
            dks = [_dot(q1t[hh], ds_scr[slot, hh]) for hh in range(hpg)]
            for hh in range(hpg):
                dfk_ref[j, hh:hh + 1, :] += -dks[hh][dh:dh + 1, :]
            dk_acc[j] += jnp.concatenate([d[0:dh] for d in dks], axis=0)
            dv_acc[j] += jnp.concatenate(
                [_dot(do2t[hh * dh:(hh + 1) * dh], p_scr[slot, hh]) for hh in range(hpg)], axis=0)
            return tuple(out)

        init = tuple(jnp.zeros((dh + 2 * SUBLANES, tq), F32) for _ in range(hpg))
        n_past = i * per

        def pair(t, cr):
            return step(2 * t + 1, step(2 * t, cr, None, 0), None, 1)

        carry = lax.fori_loop(0, n_past // 2, pair, init)
        carry = lax.cond(n_past % 2 == 1, lambda cr: step(n_past - 1, cr, None, 0), lambda cr: cr, carry)
        for d in range(per):
            carry = step(i * per + d, carry, d * tk, (d + 1) % 2)
        dq_all = jnp.concatenate([carry[hh][0:dh] for hh in range(hpg)], axis=0).T
        dq_ref[...] = (dq_all * dq_mul).astype(BF16)
        dfq_ref[...] = jnp.concatenate([carry[hh][dh:dh + 1, :] for hh in range(hpg)]
                                       + [jnp.zeros((SUBLANES - hpg, tq), F32)], axis=0)

        @pl.when(i == nq - 1)
        def _():
            for jj in range(nk):
                dkv = dk_acc[jj].T
                if s_mul != 1.0:
                    dkv = dkv * s_mul
                dk_ref[jj] = dkv.astype(BF16)
                dv_ref[jj] = dv_acc[jj].T.astype(BF16)

    blk = pl.BlockSpec((tq, LANES), lambda g, i: (i, g))
    res = pl.BlockSpec((nk, tk, LANES), lambda g, i: (0, 0, g))
    res_in = pl.BlockSpec((nk, tk, LANES), lambda g, i: (0, 0, g), pipeline_mode=pl.Buffered(1))
    aux = pl.BlockSpec((None, tq, LANES), lambda g, i: (g, i, 0))
    heads = pl.BlockSpec((tq, LANES), lambda g, i: (i, 0))
    rows = pl.BlockSpec((None, nk, SUBLANES, tk), lambda g, i: (g, 0, 0, 0))
    dq, dk3, dv3, dfq, dfk = _pcall(
        body, name="flash_bwd", grid=(G, nq),
        in_specs=[blk, res_in, res_in, blk, heads, aux, heads, rows],
        out_specs=[blk, res, res, pl.BlockSpec((None, None, SUBLANES, tq), lambda g, i: (g, i, 0, 0)), rows],
        out_shape=[jax.ShapeDtypeStruct((T, D), BF16), jax.ShapeDtypeStruct((nk, tk, D), BF16),
                   jax.ShapeDtypeStruct((nk, tk, D), BF16), jax.ShapeDtypeStruct((G, nq, SUBLANES, tq), F32),
                   jax.ShapeDtypeStruct((G, nk, SUBLANES, tk), F32)],
        scratch_shapes=[pltpu.VMEM((nk, LANES, tk), F32), pltpu.VMEM((nk, LANES, tk), F32),
                        pltpu.VMEM((2, hpg, tq, tk), F32), pltpu.VMEM((2, hpg, tq, tk), F32),
                        pltpu.VMEM((2, hpg, tq, tk), BF16), pltpu.VMEM((2, hpg, tq, tk), BF16)],
        compiler_params=_params(("arbitrary", "arbitrary")),
    )(q, k3, v3, do, fq_aux, lse_aux, dl_aux, fk_rows)
    return dq, dk3.reshape(T, D), dv3.reshape(T, D), dfq, dfk


def _attn_out_fwd(o, x, wo, g_post, tm):
    T, D = x.shape

    def body(o_ref, x_ref, w_ref, g_ref, m_ref, xo_ref):
        m = _dot(o_ref[...], w_ref[...])
        m_ref[...] = m
        xo_ref[...] = x_ref[...] + _rms(m, g_ref[...])

    return _pcall(
        body, name="attn_out_fwd", grid=(T // tm,),
        in_specs=[_rows(tm, D), _rows(tm, D), _full((D, D)), _full((1, D))],
        out_specs=[_rows(tm, D), _rows(tm, D)],
        out_shape=[jax.ShapeDtypeStruct((T, D), F32), jax.ShapeDtypeStruct((T, D), F32)],
        compiler_params=_params(("arbitrary",)),
    )(o, x, wo, g_post)


def _attn_out_bwd(dxo, m, o, wo, g_post, head_ind, tm, dep):
    T, D = m.shape

    def body(dxo_ref, m_ref, o_ref, w_ref, g_ref, ind_ref, dep_ref, dm_ref, do_ref, dl_ref, sums_ref):
        i = pl.program_id(0)
        dm, dgpost = _rms_bwd(m_ref[...], g_ref[...], dxo_ref[...])
        dmb = dm.astype(BF16)
        dm_ref[...] = dmb
        dob = _dot_nt(dmb, w_ref[...]).astype(BF16)
        do_ref[...] = dob
        dl_ref[...] = jnp.dot(dob.astype(F32) * o_ref[...], ind_ref[...], precision=lax.Precision.HIGHEST,
                              preferred_element_type=F32)

        @pl.when(i == 0)
        def _():
            sums_ref[...] = jnp.zeros_like(sums_ref)
        sums_ref[0:1, :] += dgpost

    return _pcall(
        body, name="attn_out_bwd", grid=(T // tm,),
        in_specs=[_rows(tm, D), _rows(tm, D), _rows(tm, D), _full((D, D)), _full((1, D)), _full((D, LANES)),
                  pl.BlockSpec(memory_space=pl.ANY)],
        out_specs=[_rows(tm, D), _rows(tm, D), _rows(tm, LANES), _acc((SUBLANES, D))],
        out_shape=[jax.ShapeDtypeStruct((T, D), BF16), jax.ShapeDtypeStruct((T, D), BF16),
                   jax.ShapeDtypeStruct((T, LANES), F32), jax.ShapeDtypeStruct((SUBLANES, D), F32)],
        compiler_params=_params(("arbitrary",)),
    )(dxo, m, o, wo, g_post, head_ind, dep)


def _attn_in_bwd(dxo, x, g_pre, dq, dk, dv, dlf, lf, wqkv, wf, tm, n_heads):
    T, D = x.shape

    def body(dxo_ref, x_ref, g_ref, dq_ref, dk_ref, dv_ref, dlf_ref, lf_ref, w_ref, wf_ref,
             dxi_ref, h_ref, df_ref, sums_ref, dbf_ref):
        i = pl.program_id(0)
        xv = x_ref[...]
        h_ref[...] = _rms(xv, g_ref[...]).astype(BF16)
        lane = lax.broadcasted_iota(jnp.int32, (1, LANES), 1)
        df = jnp.where(lane < n_heads, dlf_ref[...] * (1.0 - jnp.exp(lf_ref[...])), 0.0)
        dfb = df.astype(BF16)
        df_ref[...] = dfb
        dh = (_dot_nt(dq_ref[...], w_ref[:, 0:D]) + _dot_nt(dk_ref[...], w_ref[:, D:2 * D])
              + _dot_nt(dv_ref[...], w_ref[:, 2 * D:3 * D]) + _dot_nt(dfb, wf_ref[...]))
        dxi, dgpre = _rms_bwd(xv, g_ref[...], dh)
        dxi_ref[...] = dxo_ref[...] + dxi

        @pl.when(i == 0)
        def _():
            sums_ref[...] = jnp.zeros_like(sums_ref)
            dbf_ref[...] = jnp.zeros_like(dbf_ref)
        sums_ref[0:1, :] += dgpre
        dbf_ref[...] += jnp.sum(df, axis=0, keepdims=True)

    return _pcall(
        body, name="attn_in_bwd", grid=(T // tm,),
        in_specs=[_rows(tm, D), _rows(tm, D), _full((1, D)), _rows(tm, D), _rows(tm, D), _rows(tm, D),
                  _rows(tm, LANES), _rows(tm, LANES), _full((D, 3 * D)), _full((D, LANES))],
        out_specs=[_rows(tm, D), _rows(tm, D), _rows(tm, LANES), _acc((SUBLANES, D)), _acc((1, LANES))],
        out_shape=[jax.ShapeDtypeStruct((T, D), F32), jax.ShapeDtypeStruct((T, D), BF16),
                   jax.ShapeDtypeStruct((T, LANES), BF16), jax.ShapeDtypeStruct((SUBLANES, D), F32),
                   jax.ShapeDtypeStruct((1, LANES), F32)],
        compiler_params=_params(("arbitrary",)),
    )(dxo, x, g_pre, dq, dk, dv, dlf, lf, wqkv, wf)


def _adamw(recvs, w, m, v, tr, name):
    L, R, C = w.shape
    assert len(recvs) == L
    c1 = 1.0 - ADAM_B1 ** ADAM_STEP
    c2 = 1.0 - ADAM_B2 ** ADAM_STEP

    def body(*refs):
        r_refs = refs[:L]
        w_ref, m_ref, v_ref, g_ref, d_ref, nm_ref, nv_ref = refs[L:]
        layer = pl.program_id(0)
        g = None
        for k in range(L):
            gk = r_refs[k][0, :, 0:C].astype(F32)
            for s in range(1, N_DEV):
                gk = gk + r_refs[k][s, :, 0:C].astype(F32)
            g = gk if g is None else jnp.where(layer == k, gk, g)
        nm = ADAM_B1 * m_ref[...] + (1.0 - ADAM_B1) * g
        nv = ADAM_B2 * v_ref[...] + (1.0 - ADAM_B2) * jnp.square(g)
        m_hat = nm / c1
        v_hat = nv / c2
        g_ref[...] = g
        d_ref[...] = -ADAM_LR * (m_hat / (jnp.sqrt(v_hat) + ADAM_EPS) + ADAM_WD * w_ref[...])
        nm_ref[...] = nm
        nv_ref[...] = nv

    def recv_spec(k):
        return pl.BlockSpec((N_DEV, tr, recvs[k].shape[-1]), lambda l, i: (0, jnp.where(l == k, i, 0), 0))

    blk = pl.BlockSpec((None, tr, C), lambda l, i: (l, i, 0))
    return _pcall(
        body, name=name, grid=(L, R // tr),
        in_specs=[recv_spec(k) for k in range(L)] + [blk] * 3,
        out_specs=[blk] * 4,
        out_shape=[jax.ShapeDtypeStruct((L, R, C), F32)] * 4,
        compiler_params=_params(("arbitrary", "arbitrary")),
    )(*recvs, w, m, v)


def _row_block(rows, cols):
    cap = max(SUBLANES, (256 * 1024) // max(cols, 1))
    best = None
    for t in range(SUBLANES, rows + 1, SUBLANES):
        if rows % t == 0 and t <= cap:
            best = t
    return rows if best is None else best


def kernel(x, g_mix_pre, g_mix_post, g_ffn_pre, g_ffn_post, conv_pw1_w, conv_pw1_b, conv_dw_w, conv_dw_b, conv_ln_g, conv_ln_b, conv_pw2_w, conv_pw2_b, attn_w_in, attn_b_f, attn_w_o, mlp_w_up, mlp_w_down, loss_target, m_g_mix_pre, m_g_mix_post, m_g_ffn_pre, m_g_ffn_post, m_conv_pw1_w, m_conv_pw1_b, m_conv_dw_w, m_conv_dw_b, m_conv_ln_g, m_conv_ln_b, m_conv_pw2_w, m_conv_pw2_b, m_attn_w_in, m_attn_b_f, m_attn_w_o, m_mlp_w_up, m_mlp_w_down, v_g_mix_pre, v_g_mix_post, v_g_ffn_pre, v_g_ffn_post, v_conv_pw1_w, v_conv_pw1_b, v_conv_dw_w, v_conv_dw_b, v_conv_ln_g, v_conv_ln_b, v_conv_pw2_w, v_conv_pw2_b, v_attn_w_in, v_attn_b_f, v_attn_w_o, v_mlp_w_up, v_mlp_w_down):
    _, T, D = x.shape
    H = attn_b_f.shape[-1]
    dh = D // H
    width = conv_dw_w.shape[1]
    cin = attn_w_in.shape[-1]
    fs = mlp_w_up.shape[-1]
    G = D // LANES
    hpg = LANES // dh
    assert T % 4 == 0 and D % LANES == 0 and LANES % dh == 0 and width <= CONV_HALO and H <= LANES

    tm = min(512, T // 4)
    tmb = min(256, T // 4)
    tqf = min(1024, T // 4)
    tkb = tm
    tmc = min(256, T // 4)
    lcf = min(512, D)
    lcb = min(256, D)
    tkw = min(4096, T // 2)
    tb = min(256, T // 4)

    scale = float(dh) ** -0.5
    mant, _ = math.frexp(scale)
    q_mul = scale if mant == 0.5 else 1.0
    s_mul = 1.0 if mant == 0.5 else scale

    x2 = x.reshape(T, D)
    tgt = loss_target.reshape(T, D)

    w_srcs = [conv_pw1_w, conv_dw_w, conv_pw2_w, mlp_w_up, mlp_w_down, attn_w_in, attn_w_o]
    w_items = [(0, 0, "whole"), (1, 0, "whole"), (2, 0, "whole"), (3, 0, "whole"), (4, 0, "whole"),
               (5, 0, "whole"), (6, 0, "whole"), (3, 1, "whole"), (4, 1, "whole")]
    cin_w = -(-cin // LANES) * LANES
    w_lands = _place_own(
        w_srcs, w_items,
        [((N_DEV, D, cin_w) if si == 5 else (N_DEV,) + w_srcs[si].shape[1:], F32 if si == 1 else BF16)
         for si, _, _ in w_items],
        "stage_weights", cast=True)
    me_arr = _dev_index(*_mesh_pos()).astype(jnp.int32).reshape(1)
    w_groups = [[0], [1, 2], [3, 4], [5, 6], [7, 8]]
    g_sems, _, w_lands, g_token = _push_start(
        [], w_lands, [[(a, None, "own") for a in grp] for grp in w_groups], "gather_start")

    def gather_wait(gi, after):
        grp = w_groups[gi]
        return _push_wait(g_sems[gi], [], [w_lands[a] for a in grp], [(k, None, "own") for k in range(len(grp))],
                          after, "gather_wait%d" % gi)

    (w1g,) = gather_wait(0, g_token)
    bf = jnp.pad(attn_b_f, ((0, 0), (0, LANES - H)))

    row = lambda a, i: a[i:i + 1]

    a0, u0 = _conv_in_fwd(x2, row(g_mix_pre, 0), w1g, conv_pw1_b, tm)
    dwg, w2g = gather_wait(1, u0)
    w2 = w2g.reshape(D, D)
    dw_full = jnp.transpose(dwg, (1, 0, 2)).reshape(width, D)
    w32 = jnp.pad(dw_full, ((0, 32 - width), (0, 0)))
    y0 = _dwconv_fwd(u0, w32, conv_dw_b, tmc, lcf, width)
    m0, x_1 = _conv_out_fwd(y0, x2, conv_ln_g, conv_ln_b, w2, conv_pw2_b, row(g_mix_post, 0), tm)
    wu0, wd0 = gather_wait(2, x_1)
    up0, n0, x_2 = _mlp_fwd(x_1, row(g_ffn_pre, 0), wu0, wd0, row(g_ffn_post, 0), tm, "mlp0_fwd")

    wing, wog = gather_wait(3, x_2)
    wo = wog.reshape(D, D)
    win = jnp.transpose(wing[:, :, :cin], (1, 0, 2)).reshape(D, N_DEV * cin)
    wqkv = win[:, :3 * D]
    wf = jnp.pad(win[:, 3 * D:], ((0, 0), (0, LANES - H)))
    q, k, v, lf = _attn_in_fwd(x_2, row(g_mix_pre, 1), wqkv, wf, bf, tm, q_mul, H)
    fcum = _cumsum_rows(lf, jnp.zeros_like(lf), tb, False, "forget_cumsum")


    def key_rows(blk):
        r = jnp.transpose(fcum[:, :H].T.reshape(G, hpg, T // blk, blk), (0, 2, 1, 3))
        return jnp.pad(r, ((0, 0), (0, 0), (0, SUBLANES - hpg), (0, 0)))

    fk_rows = key_rows(tkb)
    o, o32, lse_aux = _flash_fwd(q, k, v, fcum, key_rows(tqf), dh=dh, tq=tqf, s_mul=s_mul)
    m1, x_3 = _attn_out_fwd(o, x_2, wo, row(g_mix_post, 1), tm)
    wu1, wd1 = gather_wait(4, x_3)
    up1, n1, x_4 = _mlp_fwd(x_3, row(g_ffn_pre, 1), wu1, wd1, row(g_ffn_post, 1), tm, "mlp1_fwd")

    def mlp_back(dx, n_l, x_in, up_l, l, wu, wd, dep, from_loss=False):
        dxi, h, dm, dup, sums = _mlp_bwd(dx, n_l, x_in, up_l, row(g_ffn_pre, l), wu, wd, row(g_ffn_post, l),
                                         tmb, "mlp%d_bwd" % l, dep, from_loss)
        dwu = _mm_tn(h, dup, nj=N_DEV, a_cols=D, g_cols=fs, a_by_j=False, g_by_j=True, tk=tkw,
                     name="mlp%d_dwu" % l)
        dwd = _mm_tn(up_l, dm, nj=N_DEV, a_cols=fs, g_cols=D, a_by_j=True, g_by_j=False, tk=tkw,
                     name="mlp%d_dwd" % l, act=True)
        return dxi, dwu, dwd, sums

    def push_grads(srcs, modes, name):
        items = [(i, None, mode) for i, mode in enumerate(modes)]
        lands = _seed_lands(srcs, modes, me_arr, name + "_own")
        sems, srcs_t, lands_t, token = _push_start(srcs, lands, [items], name + "_start")
        return (sems[0], srcs_t, lands_t, items), token

    def pull_grads(handle, after, name):
        sems, srcs_t, lands_t, items = handle
        return _push_wait(sems, srcs_t, lands_t, items, after, name + "_wait")

    rs = D // N_DEV
    dx, dwu1, dwd1, s_mlp1 = mlp_back(x_4, n1, x_3, up1, 1, wu1, wd1, tgt, from_loss=True)
    h_mlp1, tok = push_grads([dwu1, dwd1], ["slot", "slot"], "grads_mlp1")

    head_ind = jnp.asarray((np.arange(D)[:, None] // dh == np.arange(LANES)[None, :]).astype(np.float32))
    dm1, do, delta, s_ao = _attn_out_bwd(dx, m1, o32, wo, row(g_mix_post, 1), head_ind, tm, tok)
    dwo = _mm_tn(o, dm1, nj=1, a_cols=D, g_cols=D, a_by_j=False, g_by_j=False, tk=tkw, name="attn_dwo")
    dq, dk, dv, dfq, dfk = _flash_bwd(q, k, v, do, fcum, lse_aux, delta, fk_rows,
                                      dh=dh, tq=tkb, tk=tkb, s_mul=s_mul, dq_mul=scale)
    def head_cols(r):
        return jnp.pad(jnp.transpose(r[:, :, :hpg, :], (0, 2, 1, 3)).reshape(H, T).T, ((0, 0), (0, LANES - H)))

    df_k = head_cols(dfk)
    df_q = head_cols(dfq)
    dlf = _cumsum_rows(df_q, df_k, tb, True, "forget_cumsum_bwd")
    dx, h_at, df, s_ai, dbf = _attn_in_bwd(dx, x_2, row(g_mix_pre, 1), dq, dk, dv, dlf, lf, wqkv, wf, tm, H)
    dwq = _mm_tn(h_at, dq, nj=1, a_cols=D, g_cols=D, a_by_j=False, g_by_j=False, tk=tkw, name="attn_dwq")
    dwk = _mm_tn(h_at, dk, nj=1, a_cols=D, g_cols=D, a_by_j=False, g_by_j=False, tk=tkw, name="attn_dwk")
    dwv = _mm_tn(h_at, dv, nj=1, a_cols=D, g_cols=D, a_by_j=False, g_by_j=False, tk=tkw, name="attn_dwv")
    dwf = _mm_tn(h_at, df, nj=1, a_cols=D, g_cols=LANES, a_by_j=False, g_by_j=False, tk=tkw, name="attn_dwf")
    dwin = jnp.concatenate([dwq[0], dwk[0], dwv[0], dwf[0][:, :H]], axis=1)
    dwin = jnp.pad(jnp.transpose(dwin.reshape(D, N_DEV, cin), (1, 0, 2)), ((0, 0), (0, 0), (0, cin_w - cin)))
    h_attn, tok = push_grads([dwin, dwo.reshape(N_DEV, rs, D)], ["slot", "slot"], "grads_attn")

    dx, dwu0, dwd0, s_mlp0 = mlp_back(dx, n0, x_1, up0, 0, wu0, wd0, tok)
    h_mlp0, tok = push_grads([dwu0, dwd0], ["slot", "slot"], "grads_mlp0")

    dy0, dm0, z0, s_co = _conv_out_bwd(dx, m0, y0, conv_ln_g, conv_ln_b, w2, row(g_mix_post, 0), tm, tok)
    dw2 = _mm_tn(z0, dm0, nj=1, a_cols=D, g_cols=D, a_by_j=False, g_by_j=False, tk=tkw, name="conv_dw2")
    h_pw2, tok = push_grads([dw2.reshape(N_DEV, rs, D)], ["slot"], "grads_pw2")
    du0, ddw = _dwconv_bwd(dy0, u0, w32, tmc, lcb, width)
    grad_x, h_cv, da0, s_ci, db1 = _conv_in_bwd(dx, du0, a0, x2, row(g_mix_pre, 0), w1g, tm, tok)
    dw1 = _mm_tn(h_cv, da0, nj=N_DEV, a_cols=D, g_cols=(2 * D) // N_DEV, a_by_j=False, g_by_j=True, tk=tkw,
                 name="conv_dw1")
    ddw_s = jnp.transpose(ddw[:width].reshape(width, N_DEV, D // N_DEV), (1, 0, 2))

    def pad_row(a):
        return jnp.pad(a, ((0, 0), (0, D - a.shape[1])))

    def pack(gmp, gmq, gfp, gfq, b1, dwb, lng, lnb, b2, bfv, last):
        return jnp.concatenate([gmp, gmq, gfp, gfq, b1.reshape(2, D), dwb, lng, lnb, b2, pad_row(bfv), last],
                               axis=0)

    zero_row = jnp.zeros((1, D), F32)
    small_g = pack(
        jnp.concatenate([row(s_ci, 0), row(s_ai, 0)], axis=0),
        jnp.concatenate([row(s_co, 0), row(s_ao, 0)], axis=0),
        jnp.concatenate([row(s_mlp0, 1), row(s_mlp1, 1)], axis=0),
        jnp.concatenate([row(s_mlp0, 0), row(s_mlp1, 0)], axis=0),
        db1, row(s_co, 4), row(s_co, 1), row(s_co, 2), row(s_co, 3), dbf[:, :H],
        pad_row(row(s_mlp1, 2)[:, 0:1]))
    h_conv, tok = push_grads([dw1, ddw_s, small_g], ["slot", "slot", "whole"], "grads_conv")

    def opt(recvs, w, m, v, name):
        shp = w.shape
        L, C = shp[0], shp[-1]
        R = int(np.prod(shp[1:-1]))
        outs = _adamw([r.reshape(N_DEV, R, r.shape[-1]) for r in recvs], w.reshape(L, R, C), m.reshape(L, R, C),
                      v.reshape(L, R, C), _row_block(R, C), name)
        return [t.reshape(shp) for t in outs]

    big = {}
    r_wu1, r_wd1 = pull_grads(h_mlp1, tok, "grads_mlp1")
    r_win, r_wo = pull_grads(h_attn, r_wd1, "grads_attn")
    big["attn_w_in"] = opt([r_win], attn_w_in, m_attn_w_in, v_attn_w_in, "adamw_win")
    big["attn_w_o"] = opt([r_wo], attn_w_o, m_attn_w_o, v_attn_w_o, "adamw_wo")
    r_wu0, r_wd0 = pull_grads(h_mlp0, big["attn_w_o"][0], "grads_mlp0")
    big["mlp_w_up"] = opt([r_wu0, r_wu1], mlp_w_up, m_mlp_w_up, v_mlp_w_up, "adamw_wup")
    big["mlp_w_down"] = opt([r_wd0, r_wd1], mlp_w_down, m_mlp_w_down, v_mlp_w_down, "adamw_wdown")
    (r_w2,) = pull_grads(h_pw2, big["mlp_w_down"][0], "grads_pw2")
    big["conv_pw2_w"] = opt([r_w2], conv_pw2_w, m_conv_pw2_w, v_conv_pw2_w, "adamw_pw2")
    r_w1, r_dw, r_small = pull_grads(h_conv, big["conv_pw2_w"][0], "grads_conv")
    big["conv_pw1_w"] = opt([r_w1], conv_pw1_w, m_conv_pw1_w, v_conv_pw1_w, "adamw_pw1")
    big["conv_dw_w"] = opt([r_dw], conv_dw_w, m_conv_dw_w, v_conv_dw_w, "adamw_dw")
    small_w = pack(g_mix_pre, g_mix_post, g_ffn_pre, g_ffn_post, conv_pw1_b, conv_dw_b, conv_ln_g, conv_ln_b,
                   conv_pw2_b, attn_b_f, zero_row)
    small_m = pack(m_g_mix_pre, m_g_mix_post, m_g_ffn_pre, m_g_ffn_post, m_conv_pw1_b, m_conv_dw_b, m_conv_ln_g,
                   m_conv_ln_b, m_conv_pw2_b, m_attn_b_f, zero_row)
    small_v = pack(v_g_mix_pre, v_g_mix_post, v_g_ffn_pre, v_g_ffn_post, v_conv_pw1_b, v_conv_dw_b, v_conv_ln_g,
                   v_conv_ln_b, v_conv_pw2_b, v_attn_b_f, zero_row)
    sm = _adamw([r_small], small_w[None], small_m[None], small_v[None], small_w.shape[0], "adamw_small")
    sm = [t[0] for t in sm]
    loss = sm[0][15, 0]

    def unpack(t):
        return {"g_mix_pre": t[0:2], "g_mix_post": t[2:4], "g_ffn_pre": t[4:6], "g_ffn_post": t[6:8],
                "conv_pw1_b": t[8:10].reshape(1, 2 * D), "conv_dw_b": t[10:11], "conv_ln_g": t[11:12],
                "conv_ln_b": t[12:13], "conv_pw2_b": t[13:14], "attn_b_f": t[14:15, :H]}

    small = [unpack(t) for t in sm]
    names = ["g_mix_pre", "g_mix_post", "g_ffn_pre", "g_ffn_post", "conv_pw1_w", "conv_pw1_b", "conv_dw_w",
             "conv_dw_b", "conv_ln_g", "conv_ln_b", "conv_pw2_w", "conv_pw2_b", "attn_w_in", "attn_b_f",
             "attn_w_o", "mlp_w_up", "mlp_w_down"]
    outs = [loss, grad_x.reshape(1, T, D)]
    for kind in range(4):
        for nme in names:
            outs.append(big[nme][kind] if nme in big else small[kind][nme])
    return tuple(outs)
```

```python
import math

import numpy as np
import jax
import jax.numpy as jnp
from jax import lax
from jax.experimental import pallas as pl
from jax.experimental.pallas import tpu as pltpu

F32 = jnp.float32
BF16 = jnp.bfloat16

RMS_EPS = 1e-6
LN_EPS = 1e-5
MASK_VALUE = -1e30
ADAM_LR = 0.001
ADAM_B1 = 0.9
ADAM_B2 = 0.999
ADAM_EPS = 1e-08
ADAM_WD = 0.01
ADAM_STEP = 10

N_DEV = 8
LANES = 128
SUBLANES = 8
CONV_HALO = 32
CONV_ROWS = 32
FLASH_ROWS = 32
VMEM_LIMIT = 56 * 1024 * 1024

_pcall = pl.pallas_call


def _params(sem=None):
    if sem is None:
        return pltpu.CompilerParams(vmem_limit_bytes=VMEM_LIMIT)
    return pltpu.CompilerParams(dimension_semantics=sem, vmem_limit_bytes=VMEM_LIMIT)


def _dot(a, b):
    return jnp.dot(a, b, preferred_element_type=F32)


def _dot_nt(a, b):
    return lax.dot_general(a, b, (((1,), (1,)), ((), ())), preferred_element_type=F32)


def _dot_tn(a, b):
    return lax.dot_general(a, b, (((0,), (0,)), ((), ())), preferred_element_type=F32)


def _full(shape):
    nd = len(shape)
    return pl.BlockSpec(shape, lambda *g: (0,) * nd, pipeline_mode=pl.Buffered(1))


def _acc(shape):
    nd = len(shape)
    return pl.BlockSpec(shape, lambda *g: (0,) * nd)


def _rows(tm, cols):
    return pl.BlockSpec((tm, cols), lambda i: (i, 0))


def _rms(x, g):
    r = lax.rsqrt(jnp.mean(x * x, axis=-1, keepdims=True) + RMS_EPS)
    return x * r * g


def _rms_bwd(x, g, dy):
    r = lax.rsqrt(jnp.mean(x * x, axis=-1, keepdims=True) + RMS_EPS)
    n = x * r
    dg = jnp.sum(dy * n, axis=0, keepdims=True)
    dn = dy * g
    dx = r * (dn - n * jnp.mean(dn * n, axis=-1, keepdims=True))
    return dx, dg


def _sigmoid(x):
    return 1.0 / (1.0 + jnp.exp(-x))


def _mesh_pos():
    return lax.axis_index("x"), lax.axis_index("y"), lax.axis_index("c")


def _dev_index(px, py, pc):
    return 4 * px + 2 * py + pc


_HBM = pl.BlockSpec(memory_space=pltpu.HBM)
_SEM = pl.BlockSpec(memory_space=pltpu.SEMAPHORE)
_EFFECT = pltpu.SideEffectType.DATAFLOW_SIDE_EFFECTING


def _peer(r, x, y, c):
    p = ((1 - x) if r & 4 else x, (1 - y) if r & 2 else y, (1 - c) if r & 1 else c)
    return p, _dev_index(*p)


def _src_ref(refs, item, me_id=None, to_id=None):
    si, sub, mode = item
    r = refs[si] if sub is None else refs[si].at[sub]
    if mode == "slot":
        return r.at[to_id]
    if mode == "own":
        return r.at[me_id]
    return r


def _place_own(srcs, items, land_shapes, name, cast=False):
    ns, n = len(srcs), len(items)

    def body(*refs):
        src = refs[:ns]
        land = refs[ns:ns + n]
        stage = refs[ns + n:ns + 2 * n] if cast else None
        sems = refs[-1]
        me_id = _dev_index(*_mesh_pos())
        cps = []
        for a, item in enumerate(items):
            s = _src_ref(src, item, to_id=me_id)
            if cast:
                if s.shape != stage[a].shape:
                    stage[a][...] = jnp.zeros_like(stage[a])
                    stage[a][:, 0:s.shape[-1]] = s[...].astype(stage[a].dtype)
                else:
                    stage[a][...] = s[...].astype(stage[a].dtype)
                s = stage[a]
            cp = pltpu.make_async_copy(s, land[a].at[me_id], sems.at[a])
            cp.start()
            cps.append(cp)
        for cp in cps:
            cp.wait()

    return _pcall(
        body, name=name,
        out_shape=[jax.ShapeDtypeStruct(tuple(s), d) for s, d in land_shapes],
        in_specs=[pl.BlockSpec(memory_space=pltpu.VMEM if cast else pl.ANY)] * ns,
        out_specs=[pl.BlockSpec(memory_space=pl.ANY)] * n,
        scratch_shapes=([pltpu.VMEM(tuple(s[1:]), d) for s, d in land_shapes] if cast else [])
        + [pltpu.SemaphoreType.DMA((n,))],
        compiler_params=pltpu.CompilerParams(vmem_limit_bytes=VMEM_LIMIT),
    )(*srcs)


def _seed_lands(srcs, modes, me, name):
    n = len(srcs)
    parts = [tuple(s.shape[1:]) if mode == "slot" else tuple(s.shape) for s, mode in zip(srcs, modes)]

    def body(me_ref, *refs):
        s = pl.program_id(0)
        for a in range(n):
            v = refs[a][...]
            refs[n + a][...] = jnp.where(s == me_ref[0], v, jnp.zeros_like(v))

    def in_spec(part, mode):
        nd = len(part)
        if mode == "slot":
            return pl.BlockSpec((None,) + part, lambda s, me_ref: (me_ref[0],) + (0,) * nd)
        return pl.BlockSpec(part, lambda s, me_ref: (0,) * nd)

    def out_spec(part):
        nd = len(part)
        return pl.BlockSpec((None,) + part, lambda s, me_ref: (s,) + (0,) * nd)

    return _pcall(
        body, name=name,
        grid_spec=pltpu.PrefetchScalarGridSpec(
            num_scalar_prefetch=1, grid=(N_DEV,),
            in_specs=[in_spec(p, m) for p, m in zip(parts, modes)],
            out_specs=[out_spec(p) for p in parts]),
        out_shape=[jax.ShapeDtypeStruct((N_DEV,) + p, s.dtype) for p, s in zip(parts, srcs)],
        compiler_params=_params(("arbitrary",)),
    )(me, *srcs)


def _push_start(srcs, lands, groups, name):
    ns, n = len(srcs), len(lands)
    ng = len(groups)
    assert sum(len(g) for g in groups) == n

    def body(*refs):
        ops = refs[:ns + n]
        land = refs[ns:ns + n]
        sems = refs[ns + n:ns + n + 2 * ng]
        token = refs[-1]
        x, y, c = _mesh_pos()
        me_id = _dev_index(x, y, c)
        a = 0
        for gi, grp in enumerate(groups):
            for k, item in enumerate(grp):
                for r in range(1, N_DEV):
                    p, pid = _peer(r, x, y, c)
                    pltpu.make_async_remote_copy(
                        src_ref=_src_ref(ops, item, me_id=me_id, to_id=pid), dst_ref=land[a].at[me_id],
                        send_sem=sems[2 * gi].at[k * (N_DEV - 1) + r - 1],
                        recv_sem=sems[2 * gi + 1].at[k * (N_DEV - 1) + r - 1],
                        device_id=p, device_id_type=pl.DeviceIdType.MESH).start()
                a += 1
        token[...] = jnp.zeros_like(token)

    sem_shapes = []
    for grp in groups:
        sem_shapes += [pltpu.SemaphoreType.DMA((len(grp) * (N_DEV - 1),))] * 2
    arrs = list(srcs) + list(lands)
    res = _pcall(
        body, name=name,
        out_shape=tuple(sem_shapes) + tuple(pltpu.HBM(a.shape, a.dtype) for a in arrs)
        + (jax.ShapeDtypeStruct((SUBLANES, LANES), F32),),
        in_specs=[_HBM] * (ns + n),
        out_specs=tuple([_SEM] * (2 * ng)) + tuple([_HBM] * (ns + n)) + (pl.BlockSpec(memory_space=pltpu.VMEM),),
        input_output_aliases={i: 2 * ng + i for i in range(ns + n)},
        compiler_params=pltpu.CompilerParams(has_side_effects=_EFFECT),
    )(*[pltpu.with_memory_space_constraint(a, pltpu.HBM) for a in arrs])
    sems = [(res[2 * gi], res[2 * gi + 1]) for gi in range(ng)]
    thru = res[2 * ng:2 * ng + ns + n]
    return sems, list(thru[:ns]), list(thru[ns:]), res[-1]


def _push_wait(sems, srcs, lands, group, after, name):
    ns, n = len(srcs), len(lands)
    assert len(group) == n

    def body(*refs):
        ops = refs[:ns + n]
        land = refs[ns:ns + n]
        send_sems, recv_sems = refs[ns + n], refs[ns + n + 1]
        x, y, c = _mesh_pos()
        me_id = _dev_index(x, y, c)
        for k, item in enumerate(group):
            for r in range(1, N_DEV):
                p, pid = _peer(r, x, y, c)
                cp = pltpu.make_async_remote_copy(
                    src_ref=_src_ref(ops, item, me_id=me_id, to_id=pid), dst_ref=land[k].at[pid],
                    send_sem=send_sems.at[k * (N_DEV - 1) + r - 1], recv_sem=recv_sems.at[k * (N_DEV - 1) + r - 1],
                    device_id=p, device_id_type=pl.DeviceIdType.MESH)
                cp.wait_send()
                cp.wait_recv()

    arrs = list(srcs) + list(lands)
    res = _pcall(
        body, name=name,
        out_shape=tuple(pltpu.HBM(a.shape, a.dtype) for a in arrs),
        in_specs=[_HBM] * (ns + n) + [_SEM, _SEM, pl.BlockSpec(memory_space=pl.ANY)],
        out_specs=tuple([_HBM] * (ns + n)),
        input_output_aliases={i: i for i in range(ns + n)},
        compiler_params=pltpu.CompilerParams(has_side_effects=_EFFECT),
    )(*arrs, sems[0], sems[1], after)
    return list(res[ns:])


def _conv_in_fwd(x, g_pre, w1g, b1, tm):
    T, D = x.shape
    ns, _, cs = w1g.shape
    half = ns // 2

    def body(x_ref, g_ref, w_ref, b_ref, a_ref, u_ref):
        h = _rms(x_ref[...], g_ref[...]).astype(BF16)
        parts = []
        for s in range(ns):
            a_s = _dot(h, w_ref[s]) + b_ref[:, s * cs:(s + 1) * cs]
            a_ref[:, s * cs:(s + 1) * cs] = a_s
            parts.append(a_s)
        for s in range(half):
            u_ref[:, s * cs:(s + 1) * cs] = parts[s] * _sigmoid(parts[s + half])

    return _pcall(
        body, name="conv_in_fwd", grid=(T // tm,),
        in_specs=[_rows(tm, D), _full((1, D)), _full(w1g.shape), _full((1, 2 * D))],
        out_specs=[_rows(tm, 2 * D), _rows(tm, D)],
        out_shape=[jax.ShapeDtypeStruct((T, 2 * D), F32), jax.ShapeDtypeStruct((T, D), F32)],
        compiler_params=_params(("arbitrary",)),
    )(x, g_pre, w1g, b1)


def _shifted_copies(ext_ref, sh_ref, tm):
    n = tm + CONV_HALO - SUBLANES
    for b in range(1, SUBLANES):
        sh_ref[b - 1, 0:n, :] = ext_ref[b:b + n, :]


def _shifted_rows(ext_ref, sh_ref, off, r0, ls):
    b = off % SUBLANES
    a8 = off - b + r0
    src = ext_ref if b == 0 else sh_ref.at[b - 1]
    return src[a8:a8 + CONV_ROWS, ls]


def _dwconv_fwd(u, w32, b, tm, lc, width):
    T, D = u.shape
    hb = tm // CONV_HALO

    def body(u_ref, halo_ref, w_ref, b_ref, y_ref, ext_ref, sh_ref):
        i = pl.program_id(0)
        ext_ref[0:CONV_HALO, :] = jnp.where(i > 0, halo_ref[...], 0.0)
        ext_ref[CONV_HALO:, :] = u_ref[...]
        _shifted_copies(ext_ref, sh_ref, tm)
        for r0 in range(0, tm, CONV_ROWS):
            for l0 in range(0, lc, LANES):
                ls = slice(l0, l0 + LANES)
                acc = jnp.zeros((CONV_ROWS, LANES), F32) + b_ref[:, ls]
                for j in range(width):
                    off = CONV_HALO - (width - 1) + j
                    acc = acc + w_ref[j:j + 1, ls] * _shifted_rows(ext_ref, sh_ref, off, r0, ls)
                y_ref[r0:r0 + CONV_ROWS, ls] = acc

    return _pcall(
        body, name="dwconv_fwd", grid=(T // tm, D // lc),
        in_specs=[pl.BlockSpec((tm, lc), lambda i, l: (i, l)),
                  pl.BlockSpec((CONV_HALO, lc), lambda i, l: (jnp.maximum(i * hb - 1, 0), l)),
                  pl.BlockSpec((32, lc), lambda i, l: (0, l)),
                  pl.BlockSpec((1, lc), lambda i, l: (0, l))],
        out_specs=pl.BlockSpec((tm, lc), lambda i, l: (i, l)),
        out_shape=jax.ShapeDtypeStruct((T, D), F32),
        scratch_shapes=[pltpu.VMEM((tm + CONV_HALO, lc), F32),
                        pltpu.VMEM((SUBLANES - 1, tm + CONV_HALO, lc), F32)],
        compiler_params=_params(("arbitrary", "arbitrary")),
    )(u, u, w32, b)


def _ln_parts(y, g, b):
    mu = jnp.mean(y, axis=-1, keepdims=True)
    yc = y - mu
    rstd = lax.rsqrt(jnp.mean(yc * yc, axis=-1, keepdims=True) + LN_EPS)
    yhat = yc * rstd
    return yhat, rstd, yhat * g + b


def _conv_out_fwd(y, x, ln_g, ln_b, w2, b2, g_post, tm):
    T, D = x.shape

    def body(y_ref, x_ref, lg_ref, lb_ref, w_ref, b_ref, g_ref, m_ref, xo_ref):
        _, _, yn = _ln_parts(y_ref[...], lg_ref[...], lb_ref[...])
        z = (yn * _sigmoid(yn)).astype(BF16)
        m = _dot(z, w_ref[...]) + b_ref[...]
        m_ref[...] = m
        xo_ref[...] = x_ref[...] + _rms(m, g_ref[...])

    return _pcall(
        body, name="conv_out_fwd", grid=(T // tm,),
        in_specs=[_rows(tm, D), _rows(tm, D), _full((1, D)), _full((1, D)), _full((D, D)), _full((1, D)),
                  _full((1, D))],
        out_specs=[_rows(tm, D), _rows(tm, D)],
        out_shape=[jax.ShapeDtypeStruct((T, D), F32), jax.ShapeDtypeStruct((T, D), F32)],
        compiler_params=_params(("arbitrary",)),
    )(y, x, ln_g, ln_b, w2, b2, g_post)


def _conv_out_bwd(dxo, m, y, ln_g, ln_b, w2, g_post, tm, dep):
    T, D = m.shape

    def body(dxo_ref, m_ref, y_ref, lg_ref, lb_ref, w_ref, g_ref, dep_ref, dy_ref, dm_ref, z_ref, sums_ref):
        i = pl.program_id(0)
        dm, dgpost = _rms_bwd(m_ref[...], g_ref[...], dxo_ref[...])
        dmb = dm.astype(BF16)
        dm_ref[...] = dmb
        yhat, rstd, yn = _ln_parts(y_ref[...], lg_ref[...], lb_ref[...])
        sg = _sigmoid(yn)
        z_ref[...] = (yn * sg).astype(BF16)
        dz = _dot_nt(dmb, w_ref[...])
        dyn = dz * (sg + yn * sg * (1.0 - sg))
        dyh = dyn * lg_ref[...]
        dy = rstd * (dyh - jnp.mean(dyh, axis=-1, keepdims=True)
                     - yhat * jnp.mean(dyh * yhat, axis=-1, keepdims=True))
        dy_ref[...] = dy

        @pl.when(i == 0)
        def _():
            sums_ref[...] = jnp.zeros_like(sums_ref)
        sums_ref[0:1, :] += dgpost
        sums_ref[1:2, :] += jnp.sum(dyn * yhat, axis=0, keepdims=True)
        sums_ref[2:3, :] += jnp.sum(dyn, axis=0, keepdims=True)
        sums_ref[3:4, :] += jnp.sum(dm, axis=0, keepdims=True)
        sums_ref[4:5, :] += jnp.sum(dy, axis=0, keepdims=True)

    return _pcall(
        body, name="conv_out_bwd", grid=(T // tm,),
        in_specs=[_rows(tm, D), _rows(tm, D), _rows(tm, D), _full((1, D)), _full((1, D)), _full((D, D)),
                  _full((1, D)), pl.BlockSpec(memory_space=pl.ANY)],
        out_specs=[_rows(tm, D), _rows(tm, D), _rows(tm, D), _acc((SUBLANES, D))],
        out_shape=[jax.ShapeDtypeStruct((T, D), F32), jax.ShapeDtypeStruct((T, D), BF16),
                   jax.ShapeDtypeStruct((T, D), BF16), jax.ShapeDtypeStruct((SUBLANES, D), F32)],
        compiler_params=_params(("arbitrary",)),
    )(dxo, m, y, ln_g, ln_b, w2, g_post, dep)


def _dwconv_bwd(dy, u, w32, tm, lc, width):
    T, D = u.shape
    hb = tm // CONV_HALO
    nt = T // tm
    last_halo = T // CONV_HALO - 1

    def body(dy_ref, dyn_ref, u_ref, up_ref, w_ref, du_ref, dw_ref, exty_ref, extu_ref, acc_ref, shy_ref, shu_ref):
        i = pl.program_id(1)
        exty_ref[0:tm, :] = dy_ref[...]
        exty_ref[tm:, :] = jnp.where(i < nt - 1, dyn_ref[...], 0.0)
        extu_ref[0:CONV_HALO, :] = jnp.where(i > 0, up_ref[...], 0.0)
        extu_ref[CONV_HALO:, :] = u_ref[...]
        _shifted_copies(exty_ref, shy_ref, tm)
        _shifted_copies(extu_ref, shu_ref, tm)

        @pl.when(i == 0)
        def _():
            acc_ref[...] = jnp.zeros_like(acc_ref)

        for r0 in range(0, tm, CONV_ROWS):
            for l0 in range(0, lc, LANES):
                ls = slice(l0, l0 + LANES)
                dyc = exty_ref[r0:r0 + CONV_ROWS, ls]
                du = jnp.zeros((CONV_ROWS, LANES), F32)
                for j in range(width):
                    du = du + w_ref[j:j + 1, ls] * _shifted_rows(exty_ref, shy_ref, (width - 1) - j, r0, ls)
                    prod = dyc * _shifted_rows(extu_ref, shu_ref, CONV_HALO - (width - 1) + j, r0, ls)
                    acc_ref[j, :, ls] += prod.reshape(CONV_ROWS // SUBLANES, SUBLANES, LANES).sum(axis=0)
                du_ref[r0:r0 + CONV_ROWS, ls] = du

        @pl.when(i == nt - 1)
        def _():
            for j in range(32):
                dw_ref[j:j + 1, :] = jnp.sum(acc_ref[j], axis=0, keepdims=True)

    return _pcall(
        body, name="dwconv_bwd", grid=(D // lc, nt),
        in_specs=[pl.BlockSpec((tm, lc), lambda l, i: (i, l)),
                  pl.BlockSpec((CONV_HALO, lc), lambda l, i: (jnp.minimum((i + 1) * hb, last_halo), l)),
                  pl.BlockSpec((tm, lc), lambda l, i: (i, l)),
                  pl.BlockSpec((CONV_HALO, lc), lambda l, i: (jnp.maximum(i * hb - 1, 0), l)),
                  pl.BlockSpec((32, lc), lambda l, i: (0, l))],
        out_specs=[pl.BlockSpec((tm, lc), lambda l, i: (i, l)),
                   pl.BlockSpec((32, lc), lambda l, i: (0, l))],
        out_shape=[jax.ShapeDtypeStruct((T, D), F32), jax.ShapeDtypeStruct((32, D), F32)],
        scratch_shapes=[pltpu.VMEM((tm + CONV_HALO, lc), F32), pltpu.VMEM((tm + CONV_HALO, lc), F32),
                        pltpu.VMEM((32, SUBLANES, lc), F32),
                        pltpu.VMEM((SUBLANES - 1, tm + CONV_HALO, lc), F32),
                        pltpu.VMEM((SUBLANES - 1, tm + CONV_HALO, lc), F32)],
        compiler_params=_params(("arbitrary", "arbitrary")),
    )(dy, dy, u, u, w32)


def _conv_in_bwd(dxo, du, a, x, g_pre, w1g, tm, dep):
    T, D = x.shape
    ns, _, cs = w1g.shape
    half = ns // 2

    def body(dxo_ref, du_ref, a_ref, x_ref, g_ref, w_ref, dep_ref, dxi_ref, h_ref, da_ref, sums_ref, db_ref):
        i = pl.program_id(0)
        xv = x_ref[...]
        h_ref[...] = _rms(xv, g_ref[...]).astype(BF16)
        dh = jnp.zeros((tm, D), F32)
        dbs = [None] * ns
        for s in range(half):
            a_u = a_ref[:, s * cs:(s + 1) * cs]
            sg = _sigmoid(a_ref[:, (s + half) * cs:(s + half + 1) * cs])
            du_s = du_ref[:, s * cs:(s + 1) * cs]
            da_u = du_s * sg
            da_g = du_s * a_u * sg * (1.0 - sg)
            for s2, v in ((s, da_u), (s + half, da_g)):
                vb = v.astype(BF16)
                da_ref[:, s2 * cs:(s2 + 1) * cs] = vb
                dbs[s2] = jnp.sum(v, axis=0, keepdims=True)
                dh = dh + _dot_nt(vb, w_ref[s2])
        dxi, dgpre = _rms_bwd(xv, g_ref[...], dh)
        dxi_ref[...] = dxo_ref[...] + dxi

        @pl.when(i == 0)
        def _():
            sums_ref[...] = jnp.zeros_like(sums_ref)
            db_ref[...] = jnp.zeros_like(db_ref)
        sums_ref[0:1, :] += dgpre
        for s in range(ns):
            db_ref[:, s * cs:(s + 1) * cs] += dbs[s]

    return _pcall(
        body, name="conv_in_bwd", grid=(T // tm,),
        in_specs=[_rows(tm, D), _rows(tm, D), _rows(tm, 2 * D), _rows(tm, D), _full((1, D)),
                  _full(w1g.shape), pl.BlockSpec(memory_space=pl.ANY)],
        out_specs=[_rows(tm, D), _rows(tm, D), _rows(tm, 2 * D), _acc((SUBLANES, D)), _acc((1, 2 * D))],
        out_shape=[jax.ShapeDtypeStruct((T, D), F32), jax.ShapeDtypeStruct((T, D), BF16),
                   jax.ShapeDtypeStruct((T, 2 * D), BF16), jax.ShapeDtypeStruct((SUBLANES, D), F32),
                   jax.ShapeDtypeStruct((1, 2 * D), F32)],
        compiler_params=_params(("arbitrary",)),
    )(dxo, du, a, x, g_pre, w1g, dep)


def _mlp_fwd(x, g_pre, wug, wdg, g_post, tm, name):
    T, D = x.shape
    ns, _, fs = wug.shape

    def body(x_ref, gp_ref, wu_ref, wd_ref, gq_ref, up_ref, m_ref, xo_ref):
        xv = x_ref[...]
        h = _rms(xv, gp_ref[...]).astype(BF16)
        acc = jnp.zeros((tm, D), F32)
        for s in range(ns):
            up = _dot(h, wu_ref[s]).astype(BF16)
            up_ref[:, s * fs:(s + 1) * fs] = up
            act = jnp.square(jnp.maximum(up.astype(F32), 0.0)).astype(BF16)
            acc = acc + _dot(act, wd_ref[s])
        m_ref[...] = acc
        xo_ref[...] = xv + _rms(acc, gq_ref[...])

    return _pcall(
        body, name=name, grid=(T // tm,),
        in_specs=[_rows(tm, D), _full((1, D)), _full(wug.shape), _full(wdg.shape), _full((1, D))],
        out_specs=[_rows(tm, ns * fs), _rows(tm, D), _rows(tm, D)],
        out_shape=[jax.ShapeDtypeStruct((T, ns * fs), BF16), jax.ShapeDtypeStruct((T, D), F32),
                   jax.ShapeDtypeStruct((T, D), F32)],
        compiler_params=_params(("arbitrary",)),
    )(x, g_pre, wug, wdg, g_post)


def _mlp_bwd(dxo, m, x, up, g_pre, wug, wdg, g_post, tm, name, dep, from_loss=False):
    T, D = x.shape
    ns, _, fs = wug.shape

    def body(dxo_ref, m_ref, x_ref, up_ref, gp_ref, wu_ref, wd_ref, gq_ref, dep_ref,
             dxi_ref, h_ref, dm_ref, dup_ref, sums_ref):
        i = pl.program_id(0)
        if from_loss:
            err = dxo_ref[...] - dep_ref[...]
            dxo = err * (1.0 / D)
        else:
            dxo = dxo_ref[...]
        dm, dgpost = _rms_bwd(m_ref[...], gq_ref[...], dxo)
        dmb = dm.astype(BF16)
        dm_ref[...] = dmb
        xv = x_ref[...]
        h_ref[...] = _rms(xv, gp_ref[...]).astype(BF16)
        dh = jnp.zeros((tm, D), F32)
        for s in range(ns):
            dact = _dot_nt(dmb, wd_ref[s])
            up = up_ref[:, s * fs:(s + 1) * fs].astype(F32)
            dup = (dact * (2.0 * jnp.maximum(up, 0.0))).astype(BF16)
            dup_ref[:, s * fs:(s + 1) * fs] = dup
            dh = dh + _dot_nt(dup, wu_ref[s])
        dxi, dgpre = _rms_bwd(xv, gp_ref[...], dh)
        dxi_ref[...] = dxo + dxi

        @pl.when(i == 0)
        def _():
            sums_ref[...] = jnp.zeros_like(sums_ref)
        sums_ref[0:1, :] += dgpost
        sums_ref[1:2, :] += dgpre
        if from_loss:
            sums_ref[2:3, :] += 0.5 * jnp.sum(jnp.mean(err * err, axis=-1, keepdims=True), axis=0, keepdims=True)

    return _pcall(
        body, name=name, grid=(T // tm,),
        in_specs=[_rows(tm, D), _rows(tm, D), _rows(tm, D), _rows(tm, ns * fs), _full((1, D)),
                  _full(wug.shape), _full(wdg.shape), _full((1, D)),
                  _rows(tm, D) if from_loss else pl.BlockSpec(memory_space=pl.ANY)],
        out_specs=[_rows(tm, D), _rows(tm, D), _rows(tm, D), _rows(tm, ns * fs), _acc((SUBLANES, D))],
        out_shape=[jax.ShapeDtypeStruct((T, D), F32), jax.ShapeDtypeStruct((T, D), BF16),
                   jax.ShapeDtypeStruct((T, D), BF16), jax.ShapeDtypeStruct((T, ns * fs), BF16),
                   jax.ShapeDtypeStruct((SUBLANES, D), F32)],
        compiler_params=_params(("arbitrary",)),
    )(dxo, m, x, up, g_pre, wug, wdg, g_post, dep)


def _mm_tn(a, g, *, nj, a_cols, g_cols, a_by_j, g_by_j, tk, name, act=False):
    T = a.shape[0]
    nk = T // tk

    def body(a_ref, g_ref, o_ref, acc_ref):
        k = pl.program_id(1)
        av = a_ref[...]
        if act:
            av = jnp.square(jnp.maximum(av.astype(F32), 0.0)).astype(BF16)
        p = _dot_tn(av, g_ref[...])

        @pl.when(k == 0)
        def _():
            acc_ref[...] = p

        @pl.when(k > 0)
        def _():
            acc_ref[...] += p

        @pl.when(k == nk - 1)
        def _():
            o_ref[...] = acc_ref[...].astype(BF16)

    return _pcall(
        body, name=name, grid=(nj, nk),
        in_specs=[pl.BlockSpec((tk, a_cols), (lambda j, k: (k, j)) if a_by_j else (lambda j, k: (k, 0))),
                  pl.BlockSpec((tk, g_cols), (lambda j, k: (k, j)) if g_by_j else (lambda j, k: (k, 0)))],
        out_specs=pl.BlockSpec((None, a_cols, g_cols), lambda j, k: (j, 0, 0)),
        out_shape=jax.ShapeDtypeStruct((nj, a_cols, g_cols), BF16),
        scratch_shapes=[pltpu.VMEM((a_cols, g_cols), F32)],
        compiler_params=_params(("arbitrary", "arbitrary")),
    )(a, g)


def _attn_in_fwd(x, g_pre, wqkv, wf, bf, tm, q_mul, n_heads):
    T, D = x.shape

    def body(x_ref, g_ref, w_ref, wf_ref, bf_ref, q_ref, k_ref, v_ref, lf_ref):
        h = _rms(x_ref[...], g_ref[...]).astype(BF16)
        q = _dot(h, w_ref[:, 0:D])
        if q_mul != 1.0:
            q = q * q_mul
        q_ref[...] = q.astype(BF16)
        k_ref[...] = _dot(h, w_ref[:, D:2 * D]).astype(BF16)
        v_ref[...] = _dot(h, w_ref[:, 2 * D:3 * D]).astype(BF16)
        fl = _dot(h, wf_ref[...]) + bf_ref[...]
        lf = jnp.minimum(fl, 0.0) - jnp.log(1.0 + jnp.exp(-jnp.abs(fl)))
        lane = lax.broadcasted_iota(jnp.int32, (1, LANES), 1)
        lf_ref[...] = jnp.where(lane < n_heads, lf, 0.0)

    return _pcall(
        body, name="attn_in_fwd", grid=(T // tm,),
        in_specs=[_rows(tm, D), _full((1, D)), _full((D, 3 * D)), _full((D, LANES)), _full((1, LANES))],
        out_specs=[_rows(tm, D), _rows(tm, D), _rows(tm, D), _rows(tm, LANES)],
        out_shape=[jax.ShapeDtypeStruct((T, D), BF16)] * 3 + [jax.ShapeDtypeStruct((T, LANES), F32)],
        compiler_params=_params(("arbitrary",)),
    )(x, g_pre, wqkv, wf, bf)


def _cumsum_rows(v, v2, tb, reverse, name):
    T, C = v.shape
    nb = T // tb

    def body(v_ref, v2_ref, o_ref, carry_ref):
        i = pl.program_id(0)

        @pl.when(i == 0)
        def _():
            carry_ref[...] = jnp.zeros_like(carry_ref)
        r = lax.broadcasted_iota(jnp.int32, (tb, tb), 0)
        c = lax.broadcasted_iota(jnp.int32, (tb, tb), 1)
        tri = jnp.where((c >= r) if reverse else (c <= r), 1.0, 0.0).astype(F32)
        out = jnp.dot(tri, v_ref[...] + v2_ref[...], precision=lax.Precision.HIGHEST,
                      preferred_element_type=F32) + carry_ref[...]
        o_ref[...] = out
        carry_ref[...] = out[0:1, :] if reverse else out[tb - 1:tb, :]

    idx = (lambda i: (nb - 1 - i, 0)) if reverse else (lambda i: (i, 0))
    return _pcall(
        body, name=name, grid=(nb,),
        in_specs=[pl.BlockSpec((tb, C), idx), pl.BlockSpec((tb, C), idx)],
        out_specs=pl.BlockSpec((tb, C), idx),
        out_shape=jax.ShapeDtypeStruct((T, C), F32),
        scratch_shapes=[pltpu.VMEM((1, C), F32)],
        compiler_params=_params(("arbitrary",)),
    )(v, v2)


def _head_col(v, lane, h):
    return jnp.sum(jnp.where(lane == h, v, 0.0), axis=1, keepdims=True)


def _flash_fwd(q, k, v, fq_aux, fk_rows, *, dh, tq, s_mul):
    T, D = q.shape
    G = D // LANES
    hpg = LANES // dh
    nq = T // tq
    k3 = k.reshape(nq, tq, D)
    v3 = v.reshape(nq, tq, D)

    rc = min(FLASH_ROWS, tq)

    def body(q_ref, k_ref, v_ref, fq_ref, fk_ref, o_ref, o32_ref, lse_ref, s_scr, p_scr):
        i = pl.program_id(1)
        lane = lax.broadcasted_iota(jnp.int32, (1, LANES), 1)
        q2 = q_ref[...]
        hmasks = [(lane >= hh * dh) & (lane < (hh + 1) * dh) for hh in range(hpg)]
        qms = [jnp.where(hm, q2, jnp.zeros_like(q2)) for hm in hmasks]

        nlb = tq // LANES
        sum_lane = [((hh + 1) % hpg) * dh for hh in range(hpg)]

        def scores(j, slot):
            kj = k_ref[j]
            for hh in range(hpg):
                s = _dot_nt(qms[hh], kj)
                s_scr[slot, hh] = s if s_mul == 1.0 else s * s_mul

        def soft(j, slot, carry, masked):
            vj = v_ref[j]
            out = []
            for hh in range(hpg):
                m_b, acc = carry[hh]
                fk_row = fk_ref[j, hh:hh + 1, :]
                def future(r0, cb):
                    return masked and cb * LANES > r0 + rc - 1

                mx = []
                for r0 in range(0, tq, rc):
                    rs = slice(r0, r0 + rc)
                    c = None
                    for cb in range(nlb):
                        if future(r0, cb):
                            continue
                        cs = slice(cb * LANES, (cb + 1) * LANES)
                        s = s_scr[slot, hh, rs, cs] - fk_row[:, cs]
                        if masked and (cb + 1) * LANES - 1 > r0:
                            ri = r0 + lax.broadcasted_iota(jnp.int32, (rc, LANES), 0)
                            ci = cb * LANES + lax.broadcasted_iota(jnp.int32, (rc, LANES), 1)
                            s = jnp.where(ci <= ri, s, MASK_VALUE)
                        s_scr[slot, hh, rs, cs] = s
                        c = s if c is None else jnp.maximum(c, s)
                    mx.append(c)
                row_max = jnp.max(jnp.concatenate(mx, axis=0), axis=1, keepdims=True)
                m_new = jnp.maximum(m_b, row_max)
                alpha = jnp.exp(m_b - m_new)
                for r0 in range(0, tq, rc):
                    rs = slice(r0, r0 + rc)
                    m_c = m_new[rs]
                    for cb in range(nlb):
                        cs = slice(cb * LANES, (cb + 1) * LANES)
                        if future(r0, cb):
                            p_scr[slot, hh, rs, cs] = jnp.zeros((rc, LANES), BF16)
                        else:
                            p_scr[slot, hh, rs, cs] = jnp.exp(s_scr[slot, hh, rs, cs] - m_c).astype(BF16)
                v_one = jnp.where(hmasks[hh], vj, jnp.ones_like(vj))
                out.append((m_new, alpha * acc + _dot(p_scr[slot, hh], v_one)))
            return tuple(out)

        def step(j, carry, masked, slot):
            scores(j, slot)
            return soft(j, slot, carry, masked)

        def pair(t, cr):
            scores(2 * t, 0)
            scores(2 * t + 1, 1)
            return soft(2 * t + 1, 1, soft(2 * t, 0, cr, False), False)

        init = tuple((jnp.full((tq, LANES), MASK_VALUE, F32), jnp.zeros((tq, LANES), F32)) for _ in range(hpg))
        carry = lax.fori_loop(0, i // 2, pair, init)
        carry = lax.cond(i % 2 == 1, lambda cr: step(i - 1, cr, False, 0), lambda cr: cr, carry)
        carry = step(i, carry, True, 1)
        fq = fq_ref[...]
        g0 = pl.program_id(0) * hpg
        o_all = jnp.zeros((tq, LANES), F32)
        lse_all = jnp.zeros((tq, LANES), F32)
        for hh in range(hpg):
            m_b, acc = carry[hh]
            l = acc[:, sum_lane[hh]:sum_lane[hh] + 1]
            o_all = jnp.where(hmasks[hh], acc * (1.0 / l), o_all)
            lse_all = jnp.where(lane == hh, m_b[:, 0:1] + jnp.log(l) + _head_col(fq, lane, g0 + hh), lse_all)
        o_ref[...] = o_all.astype(BF16)
        o32_ref[...] = o_all
        lse_ref[...] = lse_all

    return _pcall(
        body, name="flash_fwd", grid=(G, nq),
        in_specs=[pl.BlockSpec((tq, LANES), lambda g, i: (i, g)),
                  pl.BlockSpec((nq, tq, LANES), lambda g, i: (0, 0, g)),
                  pl.BlockSpec((nq, tq, LANES), lambda g, i: (0, 0, g)),
                  pl.BlockSpec((tq, LANES), lambda g, i: (i, 0)),
                  pl.BlockSpec((None, nq, SUBLANES, tq), lambda g, i: (g, 0, 0, 0))],
        out_specs=[pl.BlockSpec((tq, LANES), lambda g, i: (i, g)),
                   pl.BlockSpec((tq, LANES), lambda g, i: (i, g)),
                   pl.BlockSpec((None, tq, LANES), lambda g, i: (g, i, 0))],
        out_shape=[jax.ShapeDtypeStruct((T, D), BF16), jax.ShapeDtypeStruct((T, D), F32),
                   jax.ShapeDtypeStruct((G, T, LANES), F32)],
        scratch_shapes=[pltpu.VMEM((2, hpg, tq, tq), F32), pltpu.VMEM((2, hpg, tq, tq), BF16)],
        compiler_params=_params(("arbitrary", "arbitrary")),
    )(q, k3, v3, fq_aux, fk_rows)


def _flash_bwd(q, k, v, do, fq_aux, lse_aux, dl_aux, fk_rows, *, dh, tq, tk, s_mul, dq_mul):
    T, D = q.shape
    G = D // LANES
    hpg = LANES // dh
    nq = T // tq
    nk = T // tk
    per = tq // tk
    k3 = k.reshape(nk, tk, D)
    v3 = v.reshape(nk, tk, D)
    rc = min(FLASH_ROWS, tq)

    def body(q_ref, k_ref, v_ref, do_ref, fq_ref, lse_ref, dl_ref, fk_ref,
             dq_ref, dk_ref, dv_ref, dfq_ref, dfk_ref, dk_acc, dv_acc, s_scr, dp_scr, p_scr, ds_scr):
        i = pl.program_id(1)

        @pl.when(i == 0)
        def _():
            dk_acc[...] = jnp.zeros_like(dk_acc)
            dv_acc[...] = jnp.zeros_like(dv_acc)
            dfk_ref[...] = jnp.zeros_like(dfk_ref)

        lane = lax.broadcasted_iota(jnp.int32, (1, LANES), 1)
        q2 = q_ref[...]
        do2 = do_ref[...]
        g0 = pl.program_id(0) * hpg
        fq = fq_ref[...]
        lse = lse_ref[...]
        dl = dl_ref[...]
        hmasks = [(lane >= hh * dh) & (lane < (hh + 1) * dh) for hh in range(hpg)]
        qms = [jnp.where(hm, q2, jnp.zeros_like(q2)) for hm in hmasks]
        doms = [jnp.where(hm, do2, jnp.zeros_like(do2)) for hm in hmasks]
        q2t = q2.astype(F32).T.astype(BF16)
        do2t = do2.astype(F32).T.astype(BF16)
        ones_rows = jnp.ones((2 * SUBLANES, tk), BF16)
        q1t = [jnp.concatenate([q2t[hh * dh:(hh + 1) * dh], jnp.ones((2 * SUBLANES, tq), BF16)], axis=0)
               for hh in range(hpg)]
        c_bs = [jnp.broadcast_to(_head_col(fq, lane, g0 + hh) - lse[:, hh:hh + 1], (tq, LANES))
                for hh in range(hpg)]
        dl_bs = [jnp.broadcast_to(_head_col(dl, lane, g0 + hh), (tq, LANES)) for hh in range(hpg)]
        nlb = tk // LANES

        def step(j, carry, off, slot=0):
            masked = off is not None
            kj = k_ref[j]
            vj = v_ref[j]
            kjt = kj.astype(F32).T.astype(BF16)
            for hh in range(hpg):
                s = _dot_nt(qms[hh], kj)
                s_scr[slot, hh] = s if s_mul == 1.0 else s * s_mul
                dp_scr[slot, hh] = _dot_nt(doms[hh], vj)
            out = []
            for hh in range(hpg):
                fk_row = fk_ref[j, hh:hh + 1, :]
                for r0 in range(0, tq, rc):
                    rs = slice(r0, r0 + rc)
                    if masked and r0 + rc <= off:
                        p_scr[slot, hh, rs, :] = jnp.zeros((rc, tk), BF16)
                        ds_scr[slot, hh, rs, :] = jnp.zeros((rc, tk), BF16)
                        continue
                    c_c = c_bs[hh][rs]
                    dl_c = dl_bs[hh][rs]
                    for cb in range(nlb):
                        cs = slice(cb * LANES, (cb + 1) * LANES)
                        if masked and off + cb * LANES > r0 + rc - 1:
                            p_scr[slot, hh, rs, cs] = jnp.zeros((rc, LANES), BF16)
                            ds_scr[slot, hh, rs, cs] = jnp.zeros((rc, LANES), BF16)
                            continue
                        e = (s_scr[slot, hh, rs, cs] - fk_row[:, cs]) + c_c
                        if masked and off + (cb + 1) * LANES - 1 > r0:
                            ri = r0 + lax.broadcasted_iota(jnp.int32, (rc, LANES), 0)
                            ci = off + cb * LANES + lax.broadcasted_iota(jnp.int32, (rc, LANES), 1)
                            e = jnp.where(ci <= ri, e, MASK_VALUE)
                        p = jnp.exp(e)
                        ds = p * (dp_scr[slot, hh, rs, cs] - dl_c)
                        p_scr[slot, hh, rs, cs] = p.astype(BF16)
                        ds_scr[slot, hh, rs, cs] = ds.astype(BF16)
                k1t = jnp.concatenate([kjt[hh * dh:(hh + 1) * dh], ones_rows], axis=0)
                out.append(carry[hh] + _dot_nt(k1t, ds_scr[slot, hh]))
            dks = [_dot(q1t[hh], ds_scr[slot, hh]) for hh in range(hpg)]
            for hh in range(hpg):
                dfk_ref[j, hh:hh + 1, :] += -dks[hh][dh:dh + 1, :]
            dk_acc[j] += jnp.concatenate([d[0:dh] for d in dks], axis=0)
            dv_acc[j] += jnp.concatenate(
                [_dot(do2t[hh * dh:(hh + 1) * dh], p_scr[slot, hh]) for hh in range(hpg)], axis=0)
            return tuple(out)

        init = tuple(jnp.zeros((dh + 2 * SUBLANES, tq), F32) for _ in range(hpg))
        n_past = i * per

        def pair(t, cr):
            return step(2 * t + 1, step(2 * t, cr, None, 0), None, 1)

        carry = lax.fori_loop(0, n_past // 2, pair, init)
        carry = lax.cond(n_past % 2 == 1, lambda cr: step(n_past - 1, cr, None, 0), lambda cr: cr, carry)
        for d in range(per):
            carry = step(i * per + d, carry, d * tk, (d + 1) % 2)
        dq_all = jnp.concatenate([carry[hh][0:dh] for hh in range(hpg)], axis=0).T
        dq_ref[...] = (dq_all * dq_mul).astype(BF16)
        dfq_ref[...] = jnp.concatenate([carry[hh][dh:dh + 1, :] for hh in range(hpg)]
                                       + [jnp.zeros((SUBLANES - hpg, tq), F32)], axis=0)

        @pl.when(i == nq - 1)
        def _():
            for jj in range(nk):
                dkv = dk_acc[jj].T
                if s_mul != 1.0:
                    dkv = dkv * s_mul
                dk_ref[jj] = dkv.astype(BF16)
                dv_ref[jj] = dv_acc[jj].T.astype(BF16)

    blk = pl.BlockSpec((tq, LANES), lambda g, i: (i, g))
    res = pl.BlockSpec((nk, tk, LANES), lambda g, i: (0, 0, g))
    res_in = pl.BlockSpec((nk, tk, LANES), lambda g, i: (0, 0, g), pipeline_mode=pl.Buffered(1))
    aux = pl.BlockSpec((None, tq, LANES), lambda g, i: (g, i, 0))
    heads = pl.BlockSpec((tq, LANES), lambda g, i: (i, 0))
    rows = pl.BlockSpec((None, nk, SUBLANES, tk), lambda g, i: (g, 0, 0, 0))
    dq, dk3, dv3, dfq, dfk = _pcall(
        body, name="flash_bwd", grid=(G, nq),
        in_specs=[blk, res_in, res_in, blk, heads, aux, heads, rows],
        out_specs=[blk, res, res, pl.BlockSpec((None, None, SUBLANES, tq), lambda g, i: (g, i, 0, 0)), rows],
        out_shape=[jax.ShapeDtypeStruct((T, D), BF16), jax.ShapeDtypeStruct((nk, tk, D), BF16),
                   jax.ShapeDtypeStruct((nk, tk, D), BF16), jax.ShapeDtypeStruct((G, nq, SUBLANES, tq), F32),
                   jax.ShapeDtypeStruct((G, nk, SUBLANES, tk), F32)],
        scratch_shapes=[pltpu.VMEM((nk, LANES, tk), F32), pltpu.VMEM((nk, LANES, tk), F32),
                        pltpu.VMEM((2, hpg, tq, tk), F32), pltpu.VMEM((2, hpg, tq, tk), F32),
                        pltpu.VMEM((2, hpg, tq, tk), BF16), pltpu.VMEM((2, hpg, tq, tk), BF16)],
        compiler_params=_params(("arbitrary", "arbitrary")),
    )(q, k3, v3, do, fq_aux, lse_aux, dl_aux, fk_rows)
    return dq, dk3.reshape(T, D), dv3.reshape(T, D), dfq, dfk


def _attn_out_fwd(o, x, wo, g_post, tm):
    T, D = x.shape

    def body(o_ref, x_ref, w_ref, g_ref, m_ref, xo_ref):
        m = _dot(o_ref[...], w_ref[...])
        m_ref[...] = m
        xo_ref[...] = x_ref[...] + _rms(m, g_ref[...])

    return _pcall(
        body, name="attn_out_fwd", grid=(T // tm,),
        in_specs=[_rows(tm, D), _rows(tm, D), _full((D, D)), _full((1, D))],
        out_specs=[_rows(tm, D), _rows(tm, D)],
        out_shape=[jax.ShapeDtypeStruct((T, D), F32), jax.ShapeDtypeStruct((T, D), F32)],
        compiler_params=_params(("arbitrary",)),
    )(o, x, wo, g_post)


def _attn_out_bwd(dxo, m, o, wo, g_post, head_ind, tm, dep):
    T, D = m.shape

    def body(dxo_ref, m_ref, o_ref, w_ref, g_ref, ind_ref, dep_ref, dm_ref, do_ref, dl_ref, sums_ref):
        i = pl.program_id(0)
        dm, dgpost = _rms_bwd(m_ref[...], g_ref[...], dxo_ref[...])
        dmb = dm.astype(BF16)
        dm_ref[...] = dmb
        dob = _dot_nt(dmb, w_ref[...]).astype(BF16)
        do_ref[...] = dob
        dl_ref[...] = jnp.dot(dob.astype(F32) * o_ref[...], ind_ref[...], precision=lax.Precision.HIGHEST,
                              preferred_element_type=F32)

        @pl.when(i == 0)
        def _():
            sums_ref[...] = jnp.zeros_like(sums_ref)
        sums_ref[0:1, :] += dgpost

    return _pcall(
        body, name="attn_out_bwd", grid=(T // tm,),
        in_specs=[_rows(tm, D), _rows(tm, D), _rows(tm, D), _full((D, D)), _full((1, D)), _full((D, LANES)),
                  pl.BlockSpec(memory_space=pl.ANY)],
        out_specs=[_rows(tm, D), _rows(tm, D), _rows(tm, LANES), _acc((SUBLANES, D))],
        out_shape=[jax.ShapeDtypeStruct((T, D), BF16), jax.ShapeDtypeStruct((T, D), BF16),
                   jax.ShapeDtypeStruct((T, LANES), F32), jax.ShapeDtypeStruct((SUBLANES, D), F32)],
        compiler_params=_params(("arbitrary",)),
    )(dxo, m, o, wo, g_post, head_ind, dep)


def _attn_in_bwd(dxo, x, g_pre, dq, dk, dv, dlf, lf, wqkv, wf, tm, n_heads):
    T, D = x.shape

    def body(dxo_ref, x_ref, g_ref, dq_ref, dk_ref, dv_ref, dlf_ref, lf_ref, w_ref, wf_ref,
             dxi_ref, h_ref, df_ref, sums_ref, dbf_ref):
        i = pl.program_id(0)
        xv = x_ref[...]
        h_ref[...] = _rms(xv, g_ref[...]).astype(BF16)
        lane = lax.broadcasted_iota(jnp.int32, (1, LANES), 1)
        df = jnp.where(lane < n_heads, dlf_ref[...] * (1.0 - jnp.exp(lf_ref[...])), 0.0)
        dfb = df.astype(BF16)
        df_ref[...] = dfb
        dh = (_dot_nt(dq_ref[...], w_ref[:, 0:D]) + _dot_nt(dk_ref[...], w_ref[:, D:2 * D])
              + _dot_nt(dv_ref[...], w_ref[:, 2 * D:3 * D]) + _dot_nt(dfb, wf_ref[...]))
        dxi, dgpre = _rms_bwd(xv, g_ref[...], dh)
        dxi_ref[...] = dxo_ref[...] + dxi

        @pl.when(i == 0)
        def _():
            sums_ref[...] = jnp.zeros_like(sums_ref)
            dbf_ref[...] = jnp.zeros_like(dbf_ref)
        sums_ref[0:1, :] += dgpre
        dbf_ref[...] += jnp.sum(df, axis=0, keepdims=True)

    return _pcall(
        body, name="attn_in_bwd", grid=(T // tm,),
        in_specs=[_rows(tm, D), _rows(tm, D), _full((1, D)), _rows(tm, D), _rows(tm, D), _rows(tm, D),
                  _rows(tm, LANES), _rows(tm, LANES), _full((D, 3 * D)), _full((D, LANES))],
        out_specs=[_rows(tm, D), _rows(tm, D), _rows(tm, LANES), _acc((SUBLANES, D)), _acc((1, LANES))],
        out_shape=[jax.ShapeDtypeStruct((T, D), F32), jax.ShapeDtypeStruct((T, D), BF16),
                   jax.ShapeDtypeStruct((T, LANES), BF16), jax.ShapeDtypeStruct((SUBLANES, D), F32),
                   jax.ShapeDtypeStruct((1, LANES), F32)],
        compiler_params=_params(("arbitrary",)),
    )(dxo, x, g_pre, dq, dk, dv, dlf, lf, wqkv, wf)


def _adamw(recvs, w, m, v, tr, name):
    L, R, C = w.shape
    assert len(recvs) == L
    c1 = 1.0 - ADAM_B1 ** ADAM_STEP
    c2 = 1.0 - ADAM_B2 ** ADAM_STEP

    def body(*refs):
        r_refs = refs[:L]
        w_ref, m_ref, v_ref, g_ref, d_ref, nm_ref, nv_ref = refs[L:]
        layer = pl.program_id(0)
        g = None
        for k in range(L):
            gk = r_refs[k][0, :, 0:C].astype(F32)
            for s in range(1, N_DEV):
                gk = gk + r_refs[k][s, :, 0:C].astype(F32)
            g = gk if g is None else jnp.where(layer == k, gk, g)
        nm = ADAM_B1 * m_ref[...] + (1.0 - ADAM_B1) * g
        nv = ADAM_B2 * v_ref[...] + (1.0 - ADAM_B2) * jnp.square(g)
        m_hat = nm / c1
        v_hat = nv / c2
        g_ref[...] = g
        d_ref[...] = -ADAM_LR * (m_hat / (jnp.sqrt(v_hat) + ADAM_EPS) + ADAM_WD * w_ref[...])
        nm_ref[...] = nm
        nv_ref[...] = nv

    def recv_spec(k):
        return pl.BlockSpec((N_DEV, tr, recvs[k].shape[-1]), lambda l, i: (0, jnp.where(l == k, i, 0), 0))

    blk = pl.BlockSpec((None, tr, C), lambda l, i: (l, i, 0))
    return _pcall(
        body, name=name, grid=(L, R // tr),
        in_specs=[recv_spec(k) for k in range(L)] + [blk] * 3,
        out_specs=[blk] * 4,
        out_shape=[jax.ShapeDtypeStruct((L, R, C), F32)] * 4,
        compiler_params=_params(("arbitrary", "arbitrary")),
    )(*recvs, w, m, v)


def _row_block(rows, cols):
    cap = max(SUBLANES, (256 * 1024) // max(cols, 1))
    best = None
    for t in range(SUBLANES, rows + 1, SUBLANES):
        if rows % t == 0 and t <= cap:
            best = t
    return rows if best is None else best


def kernel(x, g_mix_pre, g_mix_post, g_ffn_pre, g_ffn_post, conv_pw1_w, conv_pw1_b, conv_dw_w, conv_dw_b, conv_ln_g, conv_ln_b, conv_pw2_w, conv_pw2_b, attn_w_in, attn_b_f, attn_w_o, mlp_w_up, mlp_w_down, loss_target, m_g_mix_pre, m_g_mix_post, m_g_ffn_pre, m_g_ffn_post, m_conv_pw1_w, m_conv_pw1_b, m_conv_dw_w, m_conv_dw_b, m_conv_ln_g, m_conv_ln_b, m_conv_pw2_w, m_conv_pw2_b, m_attn_w_in, m_attn_b_f, m_attn_w_o, m_mlp_w_up, m_mlp_w_down, v_g_mix_pre, v_g_mix_post, v_g_ffn_pre, v_g_ffn_post, v_conv_pw1_w, v_conv_pw1_b, v_conv_dw_w, v_conv_dw_b, v_conv_ln_g, v_conv_ln_b, v_conv_pw2_w, v_conv_pw2_b, v_attn_w_in, v_attn_b_f, v_attn_w_o, v_mlp_w_up, v_mlp_w_down):
    _, T, D = x.shape
    H = attn_b_f.shape[-1]
    dh = D // H
    width = conv_dw_w.shape[1]
    cin = attn_w_in.shape[-1]
    fs = mlp_w_up.shape[-1]
    G = D // LANES
    hpg = LANES // dh
    assert T % 4 == 0 and D % LANES == 0 and LANES % dh == 0 and width <= CONV_HALO and H <= LANES

    tm = min(512, T // 4)
    tmb = min(256, T // 4)
    tqf = min(1024, T // 4)
    tkb = tm
    tmc = min(256, T // 4)
    lcf = min(512, D)
    lcb = min(256, D)
    tkw = min(4096, T // 2)
    tkd = min(2048, T // 4)
    tb = min(256, T // 4)

    scale = float(dh) ** -0.5
    mant, _ = math.frexp(scale)
    q_mul = scale if mant == 0.5 else 1.0
    s_mul = 1.0 if mant == 0.5 else scale

    x2 = x.reshape(T, D)
    tgt = loss_target.reshape(T, D)

    w_srcs = [conv_pw1_w, conv_dw_w, conv_pw2_w, mlp_w_up, mlp_w_down, attn_w_in, attn_w_o]
    w_items = [(0, 0, "whole"), (1, 0, "whole"), (2, 0, "whole"), (3, 0, "whole"), (4, 0, "whole"),
               (5, 0, "whole"), (6, 0, "whole"), (3, 1, "whole"), (4, 1, "whole")]
    cin_w = -(-cin // LANES) * LANES
    w_lands = _place_own(
        w_srcs, w_items,
        [((N_DEV, D, cin_w) if si == 5 else (N_DEV,) + w_srcs[si].shape[1:], F32 if si == 1 else BF16)
         for si, _, _ in w_items],
        "stage_weights", cast=True)
    me_arr = _dev_index(*_mesh_pos()).astype(jnp.int32).reshape(1)
    w_groups = [[0], [1, 2], [3, 4], [5, 6], [7, 8]]
    g_sems, _, w_lands, g_token = _push_start(
        [], w_lands, [[(a, None, "own") for a in grp] for grp in w_groups], "gather_start")

    def gather_wait(gi, after):
        grp = w_groups[gi]
        return _push_wait(g_sems[gi], [], [w_lands[a] for a in grp], [(k, None, "own") for k in range(len(grp))],
                          after, "gather_wait%d" % gi)

    (w1g,) = gather_wait(0, g_token)
    bf = jnp.pad(attn_b_f, ((0, 0), (0, LANES - H)))

    row = lambda a, i: a[i:i + 1]

    a0, u0 = _conv_in_fwd(x2, row(g_mix_pre, 0), w1g, conv_pw1_b, tm)
    dwg, w2g = gather_wait(1, u0)
    w2 = w2g.reshape(D, D)
    dw_full = jnp.transpose(dwg, (1, 0, 2)).reshape(width, D)
    w32 = jnp.pad(dw_full, ((0, 32 - width), (0, 0)))
    y0 = _dwconv_fwd(u0, w32, conv_dw_b, tmc, lcf, width)
    m0, x_1 = _conv_out_fwd(y0, x2, conv_ln_g, conv_ln_b, w2, conv_pw2_b, row(g_mix_post, 0), tm)
    wu0, wd0 = gather_wait(2, x_1)
    up0, n0, x_2 = _mlp_fwd(x_1, row(g_ffn_pre, 0), wu0, wd0, row(g_ffn_post, 0), tm, "mlp0_fwd")

    wing, wog = gather_wait(3, x_2)
    wo = wog.reshape(D, D)
    win = jnp.transpose(wing[:, :, :cin], (1, 0, 2)).reshape(D, N_DEV * cin)
    wqkv = win[:, :3 * D]
    wf = jnp.pad(win[:, 3 * D:], ((0, 0), (0, LANES - H)))
    q, k, v, lf = _attn_in_fwd(x_2, row(g_mix_pre, 1), wqkv, wf, bf, tm, q_mul, H)
    fcum = _cumsum_rows(lf, jnp.zeros_like(lf), tb, False, "forget_cumsum")


    def key_rows(blk):
        r = jnp.transpose(fcum[:, :H].T.reshape(G, hpg, T // blk, blk), (0, 2, 1, 3))
        return jnp.pad(r, ((0, 0), (0, 0), (0, SUBLANES - hpg), (0, 0)))

    fk_rows = key_rows(tkb)
    o, o32, lse_aux = _flash_fwd(q, k, v, fcum, key_rows(tqf), dh=dh, tq=tqf, s_mul=s_mul)
    m1, x_3 = _attn_out_fwd(o, x_2, wo, row(g_mix_post, 1), tm)
    wu1, wd1 = gather_wait(4, x_3)
    up1, n1, x_4 = _mlp_fwd(x_3, row(g_ffn_pre, 1), wu1, wd1, row(g_ffn_post, 1), tm, "mlp1_fwd")

    def mlp_back(dx, n_l, x_in, up_l, l, wu, wd, dep, from_loss=False):
        dxi, h, dm, dup, sums = _mlp_bwd(dx, n_l, x_in, up_l, row(g_ffn_pre, l), wu, wd, row(g_ffn_post, l),
                                         tmb, "mlp%d_bwd" % l, dep, from_loss)
        dwu = _mm_tn(h, dup, nj=N_DEV, a_cols=D, g_cols=fs, a_by_j=False, g_by_j=True, tk=tkw,
                     name="mlp%d_dwu" % l)
        dwd = _mm_tn(up_l, dm, nj=N_DEV, a_cols=fs, g_cols=D, a_by_j=True, g_by_j=False, tk=tkw,
                     name="mlp%d_dwd" % l, act=True)
        return dxi, dwu, dwd, sums

    def push_grads(srcs, modes, name):
        items = [(i, None, mode) for i, mode in enumerate(modes)]
        lands = _seed_lands(srcs, modes, me_arr, name + "_own")
        sems, srcs_t, lands_t, token = _push_start(srcs, lands, [items], name + "_start")
        return (sems[0], srcs_t, lands_t, items), token

    def pull_grads(handle, after, name):
        sems, srcs_t, lands_t, items = handle
        return _push_wait(sems, srcs_t, lands_t, items, after, name + "_wait")

    rs = D // N_DEV
    dx, dwu1, dwd1, s_mlp1 = mlp_back(x_4, n1, x_3, up1, 1, wu1, wd1, tgt, from_loss=True)
    h_mlp1, tok = push_grads([dwu1, dwd1], ["slot", "slot"], "grads_mlp1")

    head_ind = jnp.asarray((np.arange(D)[:, None] // dh == np.arange(LANES)[None, :]).astype(np.float32))
    dm1, do, delta, s_ao = _attn_out_bwd(dx, m1, o32, wo, row(g_mix_post, 1), head_ind, tm, tok)
    dwo = _mm_tn(o, dm1, nj=1, a_cols=D, g_cols=D, a_by_j=False, g_by_j=False, tk=tkd, name="attn_dwo")
    dq, dk, dv, dfq, dfk = _flash_bwd(q, k, v, do, fcum, lse_aux, delta, fk_rows,
                                      dh=dh, tq=tkb, tk=tkb, s_mul=s_mul, dq_mul=scale)
    def head_cols(r):
        return jnp.pad(jnp.transpose(r[:, :, :hpg, :], (0, 2, 1, 3)).reshape(H, T).T, ((0, 0), (0, LANES - H)))

    df_k = head_cols(dfk)
    df_q = head_cols(dfq)
    dlf = _cumsum_rows(df_q, df_k, tb, True, "forget_cumsum_bwd")
    dx, h_at, df, s_ai, dbf = _attn_in_bwd(dx, x_2, row(g_mix_pre, 1), dq, dk, dv, dlf, lf, wqkv, wf, tm, H)
    dwq = _mm_tn(h_at, dq, nj=1, a_cols=D, g_cols=D, a_by_j=False, g_by_j=False, tk=tkd, name="attn_dwq")
    dwk = _mm_tn(h_at, dk, nj=1, a_cols=D, g_cols=D, a_by_j=False, g_by_j=False, tk=tkd, name="attn_dwk")
    dwv = _mm_tn(h_at, dv, nj=1, a_cols=D, g_cols=D, a_by_j=False, g_by_j=False, tk=tkd, name="attn_dwv")
    dwf = _mm_tn(h_at, df, nj=1, a_cols=D, g_cols=LANES, a_by_j=False, g_by_j=False, tk=tkw, name="attn_dwf")
    dwin = jnp.concatenate([dwq[0], dwk[0], dwv[0], dwf[0][:, :H]], axis=1)
    dwin = jnp.pad(jnp.transpose(dwin.reshape(D, N_DEV, cin), (1, 0, 2)), ((0, 0), (0, 0), (0, cin_w - cin)))
    h_attn, tok = push_grads([dwin, dwo.reshape(N_DEV, rs, D)], ["slot", "slot"], "grads_attn")

    dx, dwu0, dwd0, s_mlp0 = mlp_back(dx, n0, x_1, up0, 0, wu0, wd0, tok)
    h_mlp0, tok = push_grads([dwu0, dwd0], ["slot", "slot"], "grads_mlp0")

    dy0, dm0, z0, s_co = _conv_out_bwd(dx, m0, y0, conv_ln_g, conv_ln_b, w2, row(g_mix_post, 0), tm, tok)
    dw2 = _mm_tn(z0, dm0, nj=1, a_cols=D, g_cols=D, a_by_j=False, g_by_j=False, tk=tkd, name="conv_dw2")
    h_pw2, tok = push_grads([dw2.reshape(N_DEV, rs, D)], ["slot"], "grads_pw2")
    du0, ddw = _dwconv_bwd(dy0, u0, w32, tmc, lcb, width)
    grad_x, h_cv, da0, s_ci, db1 = _conv_in_bwd(dx, du0, a0, x2, row(g_mix_pre, 0), w1g, tm, tok)
    dw1 = _mm_tn(h_cv, da0, nj=N_DEV, a_cols=D, g_cols=(2 * D) // N_DEV, a_by_j=False, g_by_j=True, tk=tkw,
                 name="conv_dw1")
    ddw_s = jnp.transpose(ddw[:width].reshape(width, N_DEV, D // N_DEV), (1, 0, 2))

    def pad_row(a):
        return jnp.pad(a, ((0, 0), (0, D - a.shape[1])))

    def pack(gmp, gmq, gfp, gfq, b1, dwb, lng, lnb, b2, bfv, last):
        return jnp.concatenate([gmp, gmq, gfp, gfq, b1.reshape(2, D), dwb, lng, lnb, b2, pad_row(bfv), last],
                               axis=0)

    zero_row = jnp.zeros((1, D), F32)
    small_g = pack(
        jnp.concatenate([row(s_ci, 0), row(s_ai, 0)], axis=0),
        jnp.concatenate([row(s_co, 0), row(s_ao, 0)], axis=0),
        jnp.concatenate([row(s_mlp0, 1), row(s_mlp1, 1)], axis=0),
        jnp.concatenate([row(s_mlp0, 0), row(s_mlp1, 0)], axis=0),
        db1, row(s_co, 4), row(s_co, 1), row(s_co, 2), row(s_co, 3), dbf[:, :H],
        pad_row(row(s_mlp1, 2)[:, 0:1]))
    h_conv, tok = push_grads([dw1, ddw_s, small_g], ["slot", "slot", "whole"], "grads_conv")

    def opt(recvs, w, m, v, name):
        shp = w.shape
        L, C = shp[0], shp[-1]
        R = int(np.prod(shp[1:-1]))
        outs = _adamw([r.reshape(N_DEV, R, r.shape[-1]) for r in recvs], w.reshape(L, R, C), m.reshape(L, R, C),
                      v.reshape(L, R, C), _row_block(R, C), name)
        return [t.reshape(shp) for t in outs]

    big = {}
    r_wu1, r_wd1 = pull_grads(h_mlp1, tok, "grads_mlp1")
    r_win, r_wo = pull_grads(h_attn, r_wd1, "grads_attn")
    big["attn_w_in"] = opt([r_win], attn_w_in, m_attn_w_in, v_attn_w_in, "adamw_win")
    big["attn_w_o"] = opt([r_wo], attn_w_o, m_attn_w_o, v_attn_w_o, "adamw_wo")
    r_wu0, r_wd0 = pull_grads(h_mlp0, big["attn_w_o"][0], "grads_mlp0")
    big["mlp_w_up"] = opt([r_wu0, r_wu1], mlp_w_up, m_mlp_w_up, v_mlp_w_up, "adamw_wup")
    big["mlp_w_down"] = opt([r_wd0, r_wd1], mlp_w_down, m_mlp_w_down, v_mlp_w_down, "adamw_wdown")
    (r_w2,) = pull_grads(h_pw2, big["mlp_w_down"][0], "grads_pw2")
    big["conv_pw2_w"] = opt([r_w2], conv_pw2_w, m_conv_pw2_w, v_conv_pw2_w, "adamw_pw2")
    r_w1, r_dw, r_small = pull_grads(h_conv, big["conv_pw2_w"][0], "grads_conv")
    big["conv_pw1_w"] = opt([r_w1], conv_pw1_w, m_conv_pw1_w, v_conv_pw1_w, "adamw_pw1")
    big["conv_dw_w"] = opt([r_dw], conv_dw_w, m_conv_dw_w, v_conv_dw_w, "adamw_dw")
    small_w = pack(g_mix_pre, g_mix_post, g_ffn_pre, g_ffn_post, conv_pw1_b, conv_dw_b, conv_ln_g, conv_ln_b,
                   conv_pw2_b, attn_b_f, zero_row)
    small_m = pack(m_g_mix_pre, m_g_mix_post, m_g_ffn_pre, m_g_ffn_post, m_conv_pw1_b, m_conv_dw_b, m_conv_ln_g,
                   m_conv_ln_b, m_conv_pw2_b, m_attn_b_f, zero_row)
    small_v = pack(v_g_mix_pre, v_g_mix_post, v_g_ffn_pre, v_g_ffn_post, v_conv_pw1_b, v_conv_dw_b, v_conv_ln_g,
                   v_conv_ln_b, v_conv_pw2_b, v_attn_b_f, zero_row)
    sm = _adamw([r_small], small_w[None], small_m[None], small_v[None], small_w.shape[0], "adamw_small")
    sm = [t[0] for t in sm]
    loss = sm[0][15, 0]

    def unpack(t):
        return {"g_mix_pre": t[0:2], "g_mix_post": t[2:4], "g_ffn_pre": t[4:6], "g_ffn_post": t[6:8],
                "conv_pw1_b": t[8:10].reshape(1, 2 * D), "conv_dw_b": t[10:11], "conv_ln_g": t[11:12],
                "conv_ln_b": t[12:13], "conv_pw2_b": t[13:14], "attn_b_f": t[14:15, :H]}

    small = [unpack(t) for t in sm]
    names = ["g_mix_pre", "g_mix_post", "g_ffn_pre", "g_ffn_post", "conv_pw1_w", "conv_pw1_b", "conv_dw_w",
             "conv_dw_b", "conv_ln_g", "conv_ln_b", "conv_pw2_w", "conv_pw2_b", "attn_w_in", "attn_b_f",
             "attn_w_o", "mlp_w_up", "mlp_w_down"]
    outs = [loss, grad_x.reshape(1, T, D)]
    for kind in range(4):
        for nme in names:
            outs.append(big[nme][kind] if nme in big else small[kind][nme])
    return tuple(outs)
```

```python
import math

import numpy as np
import jax
import jax.numpy as jnp
from jax import lax
from jax.experimental import pallas as pl
from jax.experimental.pallas import tpu as pltpu

F32 = jnp.float32
BF16 = jnp.bfloat16

RMS_EPS = 1e-6
LN_EPS = 1e-5
MASK_VALUE = -1e30
ADAM_LR = 0.001
ADAM_B1 = 0.9
ADAM_B2 = 0.999
ADAM_EPS = 1e-08
ADAM_WD = 0.01
ADAM_STEP = 10

N_DEV = 8
LANES = 128
SUBLANES = 8
CONV_HALO = 32
CONV_ROWS = 32
FLASH_ROWS = 32
VMEM_LIMIT = 56 * 1024 * 1024

_pcall = pl.pallas_call


def _params(sem=None):
    if sem is None:
        return pltpu.CompilerParams(vmem_limit_bytes=VMEM_LIMIT)
    return pltpu.CompilerParams(dimension_semantics=sem, vmem_limit_bytes=VMEM_LIMIT)


def _dot(a, b):
    return jnp.dot(a, b, preferred_element_type=F32)


def _dot_nt(a, b):
    return lax.dot_general(a, b, (((1,), (1,)), ((), ())), preferred_element_type=F32)


def _dot_tn(a, b):
    return lax.dot_general(a, b, (((0,), (0,)), ((), ())), preferred_element_type=F32)


def _full(shape):
    nd = len(shape)
    return pl.BlockSpec(shape, lambda *g: (0,) * nd, pipeline_mode=pl.Buffered(1))


def _acc(shape):
    nd = len(shape)
    return pl.BlockSpec(shape, lambda *g: (0,) * nd)


def _rows(tm, cols):
    return pl.BlockSpec((tm, cols), lambda i: (i, 0))


def _rms(x, g):
    r = lax.rsqrt(jnp.mean(x * x, axis=-1, keepdims=True) + RMS_EPS)
    return x * r * g


def _rms_bwd(x, g, dy):
    r = lax.rsqrt(jnp.mean(x * x, axis=-1, keepdims=True) + RMS_EPS)
    n = x * r
    dg = jnp.sum(dy * n, axis=0, keepdims=True)
    dn = dy * g
    dx = r * (dn - n * jnp.mean(dn * n, axis=-1, keepdims=True))
    return dx, dg


def _sigmoid(x):
    return 1.0 / (1.0 + jnp.exp(-x))


def _mesh_pos():
    return lax.axis_index("x"), lax.axis_index("y"), lax.axis_index("c")


def _dev_index(px, py, pc):
    return 4 * px + 2 * py + pc


_HBM = pl.BlockSpec(memory_space=pltpu.HBM)
_SEM = pl.BlockSpec(memory_space=pltpu.SEMAPHORE)
_EFFECT = pltpu.SideEffectType.DATAFLOW_SIDE_EFFECTING


def _peer(r, x, y, c):
    p = ((1 - x) if r & 4 else x, (1 - y) if r & 2 else y, (1 - c) if r & 1 else c)
    return p, _dev_index(*p)


def _src_ref(refs, item, me_id=None, to_id=None):
    si, sub, mode = item
    r = refs[si] if sub is None else refs[si].at[sub]
    if mode == "slot":
        return r.at[to_id]
    if mode == "own":
        return r.at[me_id]
    return r


def _place_own(srcs, items, land_shapes, name, cast=False):
    ns, n = len(srcs), len(items)

    def body(*refs):
        src = refs[:ns]
        land = refs[ns:ns + n]
        stage = refs[ns + n:ns + 2 * n] if cast else None
        sems = refs[-1]
        me_id = _dev_index(*_mesh_pos())
        cps = []
        for a, item in enumerate(items):
            s = _src_ref(src, item, to_id=me_id)
            if cast:
                if s.shape != stage[a].shape:
                    stage[a][...] = jnp.zeros_like(stage[a])
                    stage[a][:, 0:s.shape[-1]] = s[...].astype(stage[a].dtype)
                else:
                    stage[a][...] = s[...].astype(stage[a].dtype)
                s = stage[a]
            cp = pltpu.make_async_copy(s, land[a].at[me_id], sems.at[a])
            cp.start()
            cps.append(cp)
        for cp in cps:
            cp.wait()

    return _pcall(
        body, name=name,
        out_shape=[jax.ShapeDtypeStruct(tuple(s), d) for s, d in land_shapes],
        in_specs=[pl.BlockSpec(memory_space=pltpu.VMEM if cast else pl.ANY)] * ns,
        out_specs=[pl.BlockSpec(memory_space=pl.ANY)] * n,
        scratch_shapes=([pltpu.VMEM(tuple(s[1:]), d) for s, d in land_shapes] if cast else [])
        + [pltpu.SemaphoreType.DMA((n,))],
        compiler_params=pltpu.CompilerParams(vmem_limit_bytes=VMEM_LIMIT),
    )(*srcs)


def _seed_lands(srcs, modes, me, name):
    n = len(srcs)
    parts = [tuple(s.shape[1:]) if mode == "slot" else tuple(s.shape) for s, mode in zip(srcs, modes)]

    def body(me_ref, *refs):
        s = pl.program_id(0)
        for a in range(n):
            v = refs[a][...]
            refs[n + a][...] = jnp.where(s == me_ref[0], v, jnp.zeros_like(v))

    def in_spec(part, mode):
        nd = len(part)
        if mode == "slot":
            return pl.BlockSpec((None,) + part, lambda s, me_ref: (me_ref[0],) + (0,) * nd)
        return pl.BlockSpec(part, lambda s, me_ref: (0,) * nd)

    def out_spec(part):
        nd = len(part)
        return pl.BlockSpec((None,) + part, lambda s, me_ref: (s,) + (0,) * nd)

    return _pcall(
        body, name=name,
        grid_spec=pltpu.PrefetchScalarGridSpec(
            num_scalar_prefetch=1, grid=(N_DEV,),
            in_specs=[in_spec(p, m) for p, m in zip(parts, modes)],
            out_specs=[out_spec(p) for p in parts]),
        out_shape=[jax.ShapeDtypeStruct((N_DEV,) + p, s.dtype) for p, s in zip(parts, srcs)],
        compiler_params=_params(("arbitrary",)),
    )(me, *srcs)


def _push_start(srcs, lands, groups, name):
    ns, n = len(srcs), len(lands)
    ng = len(groups)
    assert sum(len(g) for g in groups) == n

    def body(*refs):
        ops = refs[:ns + n]
        land = refs[ns:ns + n]
        sems = refs[ns + n:ns + n + 2 * ng]
        token = refs[-1]
        x, y, c = _mesh_pos()
        me_id = _dev_index(x, y, c)
        a = 0
        for gi, grp in enumerate(groups):
            for k, item in enumerate(grp):
                for r in range(1, N_DEV):
                    p, pid = _peer(r, x, y, c)
                    pltpu.make_async_remote_copy(
                        src_ref=_src_ref(ops, item, me_id=me_id, to_id=pid), dst_ref=land[a].at[me_id],
                        send_sem=sems[2 * gi].at[k * (N_DEV - 1) + r - 1],
                        recv_sem=sems[2 * gi + 1].at[k * (N_DEV - 1) + r - 1],
                        device_id=p, device_id_type=pl.DeviceIdType.MESH).start()
                a += 1
        token[...] = jnp.zeros_like(token)

    sem_shapes = []
    for grp in groups:
        sem_shapes += [pltpu.SemaphoreType.DMA((len(grp) * (N_DEV - 1),))] * 2
    arrs = list(srcs) + list(lands)
    res = _pcall(
        body, name=name,
        out_shape=tuple(sem_shapes) + tuple(pltpu.HBM(a.shape, a.dtype) for a in arrs)
        + (jax.ShapeDtypeStruct((SUBLANES, LANES), F32),),
        in_specs=[_HBM] * (ns + n),
        out_specs=tuple([_SEM] * (2 * ng)) + tuple([_HBM] * (ns + n)) + (pl.BlockSpec(memory_space=pltpu.VMEM),),
        input_output_aliases={i: 2 * ng + i for i in range(ns + n)},
        compiler_params=pltpu.CompilerParams(has_side_effects=_EFFECT),
    )(*[pltpu.with_memory_space_constraint(a, pltpu.HBM) for a in arrs])
    sems = [(res[2 * gi], res[2 * gi + 1]) for gi in range(ng)]
    thru = res[2 * ng:2 * ng + ns + n]
    return sems, list(thru[:ns]), list(thru[ns:]), res[-1]


def _push_wait(sems, srcs, lands, group, after, name):
    ns, n = len(srcs), len(lands)
    assert len(group) == n

    def body(*refs):
        ops = refs[:ns + n]
        land = refs[ns:ns + n]
        send_sems, recv_sems = refs[ns + n], refs[ns + n + 1]
        x, y, c = _mesh_pos()
        me_id = _dev_index(x, y, c)
        for k, item in enumerate(group):
            for r in range(1, N_DEV):
                p, pid = _peer(r, x, y, c)
                cp = pltpu.make_async_remote_copy(
                    src_ref=_src_ref(ops, item, me_id=me_id, to_id=pid), dst_ref=land[k].at[pid],
                    send_sem=send_sems.at[k * (N_DEV - 1) + r - 1], recv_sem=recv_sems.at[k * (N_DEV - 1) + r - 1],
                    device_id=p, device_id_type=pl.DeviceIdType.MESH)
                cp.wait_send()
                cp.wait_recv()

    arrs = list(srcs) + list(lands)
    res = _pcall(
        body, name=name,
        out_shape=tuple(pltpu.HBM(a.shape, a.dtype) for a in arrs),
        in_specs=[_HBM] * (ns + n) + [_SEM, _SEM, pl.BlockSpec(memory_space=pl.ANY)],
        out_specs=tuple([_HBM] * (ns + n)),
        input_output_aliases={i: i for i in range(ns + n)},
        compiler_params=pltpu.CompilerParams(has_side_effects=_EFFECT),
    )(*arrs, sems[0], sems[1], after)
    return list(res[ns:])


def _conv_in_fwd(x, g_pre, w1g, b1, tm):
    T, D = x.shape
    ns, _, cs = w1g.shape
    half = ns // 2

    def body(x_ref, g_ref, w_ref, b_ref, a_ref, u_ref):
        h = _rms(x_ref[...], g_ref[...]).astype(BF16)
        parts = []
        for s in range(ns):
            a_s = _dot(h, w_ref[s]) + b_ref[:, s * cs:(s + 1) * cs]
            a_ref[:, s * cs:(s + 1) * cs] = a_s
            parts.append(a_s)
        for s in range(half):
            u_ref[:, s * cs:(s + 1) * cs] = parts[s] * _sigmoid(parts[s + half])

    return _pcall(
        body, name="conv_in_fwd", grid=(T // tm,),
        in_specs=[_rows(tm, D), _full((1, D)), _full(w1g.shape), _full((1, 2 * D))],
        out_specs=[_rows(tm, 2 * D), _rows(tm, D)],
        out_shape=[jax.ShapeDtypeStruct((T, 2 * D), F32), jax.ShapeDtypeStruct((T, D), F32)],
        compiler_params=_params(("arbitrary",)),
    )(x, g_pre, w1g, b1)


def _shifted_copies(ext_ref, sh_ref, tm):
    n = tm + CONV_HALO - SUBLANES
    for b in range(1, SUBLANES):
        sh_ref[b - 1, 0:n, :] = ext_ref[b:b + n, :]


def _shifted_rows(ext_ref, sh_ref, off, r0, ls):
    b = off % SUBLANES
    a8 = off - b + r0
    src = ext_ref if b == 0 else sh_ref.at[b - 1]
    return src[a8:a8 + CONV_ROWS, ls]


def _dwconv_fwd(u, w32, b, tm, lc, width):
    T, D = u.shape
    hb = tm // CONV_HALO

    def body(u_ref, halo_ref, w_ref, b_ref, y_ref, ext_ref, sh_ref):
        i = pl.program_id(0)
        ext_ref[0:CONV_HALO, :] = jnp.where(i > 0, halo_ref[...], 0.0)
        ext_ref[CONV_HALO:, :] = u_ref[...]
        _shifted_copies(ext_ref, sh_ref, tm)
        for r0 in range(0, tm, CONV_ROWS):
            for l0 in range(0, lc, LANES):
                ls = slice(l0, l0 + LANES)
                acc = jnp.zeros((CONV_ROWS, LANES), F32) + b_ref[:, ls]
                for j in range(width):
                    off = CONV_HALO - (width - 1) + j
                    acc = acc + w_ref[j:j + 1, ls] * _shifted_rows(ext_ref, sh_ref, off, r0, ls)
                y_ref[r0:r0 + CONV_ROWS, ls] = acc

    return _pcall(
        body, name="dwconv_fwd", grid=(T // tm, D // lc),
        in_specs=[pl.BlockSpec((tm, lc), lambda i, l: (i, l)),
                  pl.BlockSpec((CONV_HALO, lc), lambda i, l: (jnp.maximum(i * hb - 1, 0), l)),
                  pl.BlockSpec((32, lc), lambda i, l: (0, l)),
                  pl.BlockSpec((1, lc), lambda i, l: (0, l))],
        out_specs=pl.BlockSpec((tm, lc), lambda i, l: (i, l)),
        out_shape=jax.ShapeDtypeStruct((T, D), F32),
        scratch_shapes=[pltpu.VMEM((tm + CONV_HALO, lc), F32),
                        pltpu.VMEM((SUBLANES - 1, tm + CONV_HALO, lc), F32)],
        compiler_params=_params(("arbitrary", "arbitrary")),
    )(u, u, w32, b)


def _ln_parts(y, g, b):
    mu = jnp.mean(y, axis=-1, keepdims=True)
    yc = y - mu
    rstd = lax.rsqrt(jnp.mean(yc * yc, axis=-1, keepdims=True) + LN_EPS)
    yhat = yc * rstd
    return yhat, rstd, yhat * g + b


def _conv_out_fwd(y, x, ln_g, ln_b, w2, b2, g_post, tm):
    T, D = x.shape

    def body(y_ref, x_ref, lg_ref, lb_ref, w_ref, b_ref, g_ref, m_ref, xo_ref):
        _, _, yn = _ln_parts(y_ref[...], lg_ref[...], lb_ref[...])
        z = (yn * _sigmoid(yn)).astype(BF16)
        m = _dot(z, w_ref[...]) + b_ref[...]
        m_ref[...] = m
        xo_ref[...] = x_ref[...] + _rms(m, g_ref[...])

    return _pcall(
        body, name="conv_out_fwd", grid=(T // tm,),
        in_specs=[_rows(tm, D), _rows(tm, D), _full((1, D)), _full((1, D)), _full((D, D)), _full((1, D)),
                  _full((1, D))],
        out_specs=[_rows(tm, D), _rows(tm, D)],
        out_shape=[jax.ShapeDtypeStruct((T, D), F32), jax.ShapeDtypeStruct((T, D), F32)],
        compiler_params=_params(("arbitrary",)),
    )(y, x, ln_g, ln_b, w2, b2, g_post)


def _conv_out_bwd(dxo, m, y, ln_g, ln_b, w2, g_post, tm, dep):
    T, D = m.shape

    def body(dxo_ref, m_ref, y_ref, lg_ref, lb_ref, w_ref, g_ref, dep_ref, dy_ref, dm_ref, z_ref, sums_ref):
        i = pl.program_id(0)
        dm, dgpost = _rms_bwd(m_ref[...], g_ref[...], dxo_ref[...])
        dmb = dm.astype(BF16)
        dm_ref[...] = dmb
        yhat, rstd, yn = _ln_parts(y_ref[...], lg_ref[...], lb_ref[...])
        sg = _sigmoid(yn)
        z_ref[...] = (yn * sg).astype(BF16)
        dz = _dot_nt(dmb, w_ref[...])
        dyn = dz * (sg + yn * sg * (1.0 - sg))
        dyh = dyn * lg_ref[...]
        dy = rstd * (dyh - jnp.mean(dyh, axis=-1, keepdims=True)
                     - yhat * jnp.mean(dyh * yhat, axis=-1, keepdims=True))
        dy_ref[...] = dy

        @pl.when(i == 0)
        def _():
            sums_ref[...] = jnp.zeros_like(sums_ref)
        sums_ref[0:1, :] += dgpost
        sums_ref[1:2, :] += jnp.sum(dyn * yhat, axis=0, keepdims=True)
        sums_ref[2:3, :] += jnp.sum(dyn, axis=0, keepdims=True)
        sums_ref[3:4, :] += jnp.sum(dm, axis=0, keepdims=True)
        sums_ref[4:5, :] += jnp.sum(dy, axis=0, keepdims=True)

    return _pcall(
        body, name="conv_out_bwd", grid=(T // tm,),
        in_specs=[_rows(tm, D), _rows(tm, D), _rows(tm, D), _full((1, D)), _full((1, D)), _full((D, D)),
                  _full((1, D)), pl.BlockSpec(memory_space=pl.ANY)],
        out_specs=[_rows(tm, D), _rows(tm, D), _rows(tm, D), _acc((SUBLANES, D))],
        out_shape=[jax.ShapeDtypeStruct((T, D), F32), jax.ShapeDtypeStruct((T, D), BF16),
                   jax.ShapeDtypeStruct((T, D), BF16), jax.ShapeDtypeStruct((SUBLANES, D), F32)],
        compiler_params=_params(("arbitrary",)),
    )(dxo, m, y, ln_g, ln_b, w2, g_post, dep)


def _dwconv_bwd(dy, u, w32, tm, lc, width):
    T, D = u.shape
    hb = tm // CONV_HALO
    nt = T // tm
    last_halo = T // CONV_HALO - 1

    def body(dy_ref, dyn_ref, u_ref, up_ref, w_ref, du_ref, dw_ref, exty_ref, extu_ref, acc_ref, shy_ref, shu_ref):
        i = pl.program_id(1)
        exty_ref[0:tm, :] = dy_ref[...]
        exty_ref[tm:, :] = jnp.where(i < nt - 1, dyn_ref[...], 0.0)
        extu_ref[0:CONV_HALO, :] = jnp.where(i > 0, up_ref[...], 0.0)
        extu_ref[CONV_HALO:, :] = u_ref[...]
        _shifted_copies(exty_ref, shy_ref, tm)
        _shifted_copies(extu_ref, shu_ref, tm)

        @pl.when(i == 0)
        def _():
            acc_ref[...] = jnp.zeros_like(acc_ref)

        for r0 in range(0, tm, CONV_ROWS):
            for l0 in range(0, lc, LANES):
                ls = slice(l0, l0 + LANES)
                dyc = exty_ref[r0:r0 + CONV_ROWS, ls]
                du = jnp.zeros((CONV_ROWS, LANES), F32)
                for j in range(width):
                    du = du + w_ref[j:j + 1, ls] * _shifted_rows(exty_ref, shy_ref, (width - 1) - j, r0, ls)
                    prod = dyc * _shifted_rows(extu_ref, shu_ref, CONV_HALO - (width - 1) + j, r0, ls)
                    acc_ref[j, :, ls] += prod.reshape(CONV_ROWS // SUBLANES, SUBLANES, LANES).sum(axis=0)
                du_ref[r0:r0 + CONV_ROWS, ls] = du

        @pl.when(i == nt - 1)
        def _():
            for j in range(32):
                dw_ref[j:j + 1, :] = jnp.sum(acc_ref[j], axis=0, keepdims=True)

    return _pcall(
        body, name="dwconv_bwd", grid=(D // lc, nt),
        in_specs=[pl.BlockSpec((tm, lc), lambda l, i: (i, l)),
                  pl.BlockSpec((CONV_HALO, lc), lambda l, i: (jnp.minimum((i + 1) * hb, last_halo), l)),
                  pl.BlockSpec((tm, lc), lambda l, i: (i, l)),
                  pl.BlockSpec((CONV_HALO, lc), lambda l, i: (jnp.maximum(i * hb - 1, 0), l)),
                  pl.BlockSpec((32, lc), lambda l, i: (0, l))],
        out_specs=[pl.BlockSpec((tm, lc), lambda l, i: (i, l)),
                   pl.BlockSpec((32, lc), lambda l, i: (0, l))],
        out_shape=[jax.ShapeDtypeStruct((T, D), F32), jax.ShapeDtypeStruct((32, D), F32)],
        scratch_shapes=[pltpu.VMEM((tm + CONV_HALO, lc), F32), pltpu.VMEM((tm + CONV_HALO, lc), F32),
                        pltpu.VMEM((32, SUBLANES, lc), F32),
                        pltpu.VMEM((SUBLANES - 1, tm + CONV_HALO, lc), F32),
                        pltpu.VMEM((SUBLANES - 1, tm + CONV_HALO, lc), F32)],
        compiler_params=_params(("arbitrary", "arbitrary")),
    )(dy, dy, u, u, w32)


def _conv_in_bwd(dxo, du, a, x, g_pre, w1g, tm, dep):
    T, D = x.shape
    ns, _, cs = w1g.shape
    half = ns // 2

    def body(dxo_ref, du_ref, a_ref, x_ref, g_ref, w_ref, dep_ref, dxi_ref, h_ref, da_ref, sums_ref, db_ref):
        i = pl.program_id(0)
        xv = x_ref[...]
        h_ref[...] = _rms(xv, g_ref[...]).astype(BF16)
        dh = jnp.zeros((tm, D), F32)
        dbs = [None] * ns
        for s in range(half):
            a_u = a_ref[:, s * cs:(s + 1) * cs]
            sg = _sigmoid(a_ref[:, (s + half) * cs:(s + half + 1) * cs])
            du_s = du_ref[:, s * cs:(s + 1) * cs]
            da_u = du_s * sg
            da_g = du_s * a_u * sg * (1.0 - sg)
            for s2, v in ((s, da_u), (s + half, da_g)):
                vb = v.astype(BF16)
                da_ref[:, s2 * cs:(s2 + 1) * cs] = vb
                dbs[s2] = jnp.sum(v, axis=0, keepdims=True)
                dh = dh + _dot_nt(vb, w_ref[s2])
        dxi, dgpre = _rms_bwd(xv, g_ref[...], dh)
        dxi_ref[...] = dxo_ref[...] + dxi

        @pl.when(i == 0)
        def _():
            sums_ref[...] = jnp.zeros_like(sums_ref)
            db_ref[...] = jnp.zeros_like(db_ref)
        sums_ref[0:1, :] += dgpre
        for s in range(ns):
            db_ref[:, s * cs:(s + 1) * cs] += dbs[s]

    return _pcall(
        body, name="conv_in_bwd", grid=(T // tm,),
        in_specs=[_rows(tm, D), _rows(tm, D), _rows(tm, 2 * D), _rows(tm, D), _full((1, D)),
                  _full(w1g.shape), pl.BlockSpec(memory_space=pl.ANY)],
        out_specs=[_rows(tm, D), _rows(tm, D), _rows(tm, 2 * D), _acc((SUBLANES, D)), _acc((1, 2 * D))],
        out_shape=[jax.ShapeDtypeStruct((T, D), F32), jax.ShapeDtypeStruct((T, D), BF16),
                   jax.ShapeDtypeStruct((T, 2 * D), BF16), jax.ShapeDtypeStruct((SUBLANES, D), F32),
                   jax.ShapeDtypeStruct((1, 2 * D), F32)],
        compiler_params=_params(("arbitrary",)),
    )(dxo, du, a, x, g_pre, w1g, dep)


def _mlp_fwd(x, g_pre, wug, wdg, g_post, tm, name):
    T, D = x.shape
    ns, _, fs = wug.shape

    def body(x_ref, gp_ref, wu_ref, wd_ref, gq_ref, up_ref, m_ref, xo_ref):
        xv = x_ref[...]
        h = _rms(xv, gp_ref[...]).astype(BF16)
        acc = jnp.zeros((tm, D), F32)
        for s in range(ns):
            up = _dot(h, wu_ref[s]).astype(BF16)
            up_ref[:, s * fs:(s + 1) * fs] = up
            act = jnp.square(jnp.maximum(up.astype(F32), 0.0)).astype(BF16)
            acc = acc + _dot(act, wd_ref[s])
        m_ref[...] = acc
        xo_ref[...] = xv + _rms(acc, gq_ref[...])

    return _pcall(
        body, name=name, grid=(T // tm,),
        in_specs=[_rows(tm, D), _full((1, D)), _full(wug.shape), _full(wdg.shape), _full((1, D))],
        out_specs=[_rows(tm, ns * fs), _rows(tm, D), _rows(tm, D)],
        out_shape=[jax.ShapeDtypeStruct((T, ns * fs), BF16), jax.ShapeDtypeStruct((T, D), F32),
                   jax.ShapeDtypeStruct((T, D), F32)],
        compiler_params=_params(("arbitrary",)),
    )(x, g_pre, wug, wdg, g_post)


def _mlp_bwd(dxo, m, x, up, g_pre, wug, wdg, g_post, tm, name, dep, from_loss=False):
    T, D = x.shape
    ns, _, fs = wug.shape

    def body(dxo_ref, m_ref, x_ref, up_ref, gp_ref, wu_ref, wd_ref, gq_ref, dep_ref,
             dxi_ref, h_ref, dm_ref, dup_ref, sums_ref):
        i = pl.program_id(0)
        if from_loss:
            err = dxo_ref[...] - dep_ref[...]
            dxo = err * (1.0 / D)
        else:
            dxo = dxo_ref[...]
        dm, dgpost = _rms_bwd(m_ref[...], gq_ref[...], dxo)
        dmb = dm.astype(BF16)
        dm_ref[...] = dmb
        xv = x_ref[...]
        h_ref[...] = _rms(xv, gp_ref[...]).astype(BF16)
        dh = jnp.zeros((tm, D), F32)
        for s in range(ns):
            dact = _dot_nt(dmb, wd_ref[s])
            up = up_ref[:, s * fs:(s + 1) * fs].astype(F32)
            dup = (dact * (2.0 * jnp.maximum(up, 0.0))).astype(BF16)
            dup_ref[:, s * fs:(s + 1) * fs] = dup
            dh = dh + _dot_nt(dup, wu_ref[s])
        dxi, dgpre = _rms_bwd(xv, gp_ref[...], dh)
        dxi_ref[...] = dxo + dxi

        @pl.when(i == 0)
        def _():
            sums_ref[...] = jnp.zeros_like(sums_ref)
        sums_ref[0:1, :] += dgpost
        sums_ref[1:2, :] += dgpre
        if from_loss:
            sums_ref[2:3, :] += 0.5 * jnp.sum(jnp.mean(err * err, axis=-1, keepdims=True), axis=0, keepdims=True)

    return _pcall(
        body, name=name, grid=(T // tm,),
        in_specs=[_rows(tm, D), _rows(tm, D), _rows(tm, D), _rows(tm, ns * fs), _full((1, D)),
                  _full(wug.shape), _full(wdg.shape), _full((1, D)),
                  _rows(tm, D) if from_loss else pl.BlockSpec(memory_space=pl.ANY)],
        out_specs=[_rows(tm, D), _rows(tm, D), _rows(tm, D), _rows(tm, ns * fs), _acc((SUBLANES, D))],
        out_shape=[jax.ShapeDtypeStruct((T, D), F32), jax.ShapeDtypeStruct((T, D), BF16),
                   jax.ShapeDtypeStruct((T, D), BF16), jax.ShapeDtypeStruct((T, ns * fs), BF16),
                   jax.ShapeDtypeStruct((SUBLANES, D), F32)],
        compiler_params=_params(("arbitrary",)),
    )(dxo, m, x, up, g_pre, wug, wdg, g_post, dep)


def _mm_tn(a, g, *, nj, a_cols, g_cols, a_by_j, g_by_j, tk, name, act=False):
    T = a.shape[0]
    nk = T // tk

    def body(a_ref, g_ref, o_ref, acc_ref):
        k = pl.program_id(1)
        av = a_ref[...]
        if act:
            av = jnp.square(jnp.maximum(av.astype(F32), 0.0)).astype(BF16)
        p = _dot_tn(av, g_ref[...])

        @pl.when(k == 0)
        def _():
            acc_ref[...] = p

        @pl.when(k > 0)
        def _():
            acc_ref[...] += p

        @pl.when(k == nk - 1)
        def _():
            o_ref[...] = acc_ref[...].astype(BF16)

    return _pcall(
        body, name=name, grid=(nj, nk),
        in_specs=[pl.BlockSpec((tk, a_cols), (lambda j, k: (k, j)) if a_by_j else (lambda j, k: (k, 0))),
                  pl.BlockSpec((tk, g_cols), (lambda j, k: (k, j)) if g_by_j else (lambda j, k: (k, 0)))],
        out_specs=pl.BlockSpec((None, a_cols, g_cols), lambda j, k: (j, 0, 0)),
        out_shape=jax.ShapeDtypeStruct((nj, a_cols, g_cols), BF16),
        scratch_shapes=[pltpu.VMEM((a_cols, g_cols), F32)],
        compiler_params=_params(("arbitrary", "arbitrary")),
    )(a, g)


def _attn_in_fwd(x, g_pre, wqkv, wf, bf, tm, q_mul, n_heads):
    T, D = x.shape

    def body(x_ref, g_ref, w_ref, wf_ref, bf_ref, q_ref, k_ref, v_ref, lf_ref):
        h = _rms(x_ref[...], g_ref[...]).astype(BF16)
        q = _dot(h, w_ref[:, 0:D])
        if q_mul != 1.0:
            q = q * q_mul
        q_ref[...] = q.astype(BF16)
        k_ref[...] = _dot(h, w_ref[:, D:2 * D]).astype(BF16)
        v_ref[...] = _dot(h, w_ref[:, 2 * D:3 * D]).astype(BF16)
        fl = _dot(h, wf_ref[...]) + bf_ref[...]
        lf = jnp.minimum(fl, 0.0) - jnp.log(1.0 + jnp.exp(-jnp.abs(fl)))
        lane = lax.broadcasted_iota(jnp.int32, (1, LANES), 1)
        lf_ref[...] = jnp.where(lane < n_heads, lf, 0.0)

    return _pcall(
        body, name="attn_in_fwd", grid=(T // tm,),
        in_specs=[_rows(tm, D), _full((1, D)), _full((D, 3 * D)), _full((D, LANES)), _full((1, LANES))],
        out_specs=[_rows(tm, D), _rows(tm, D), _rows(tm, D), _rows(tm, LANES)],
        out_shape=[jax.ShapeDtypeStruct((T, D), BF16)] * 3 + [jax.ShapeDtypeStruct((T, LANES), F32)],
        compiler_params=_params(("arbitrary",)),
    )(x, g_pre, wqkv, wf, bf)


def _cumsum_rows(v, v2, tb, reverse, name):
    T, C = v.shape
    nb = T // tb

    def body(v_ref, v2_ref, o_ref, carry_ref):
        i = pl.program_id(0)

        @pl.when(i == 0)
        def _():
            carry_ref[...] = jnp.zeros_like(carry_ref)
        r = lax.broadcasted_iota(jnp.int32, (tb, tb), 0)
        c = lax.broadcasted_iota(jnp.int32, (tb, tb), 1)
        tri = jnp.where((c >= r) if reverse else (c <= r), 1.0, 0.0).astype(F32)
        out = jnp.dot(tri, v_ref[...] + v2_ref[...], precision=lax.Precision.HIGHEST,
                      preferred_element_type=F32) + carry_ref[...]
        o_ref[...] = out
        carry_ref[...] = out[0:1, :] if reverse else out[tb - 1:tb, :]

    idx = (lambda i: (nb - 1 - i, 0)) if reverse else (lambda i: (i, 0))
    return _pcall(
        body, name=name, grid=(nb,),
        in_specs=[pl.BlockSpec((tb, C), idx), pl.BlockSpec((tb, C), idx)],
        out_specs=pl.BlockSpec((tb, C), idx),
        out_shape=jax.ShapeDtypeStruct((T, C), F32),
        scratch_shapes=[pltpu.VMEM((1, C), F32)],
        compiler_params=_params(("arbitrary",)),
    )(v, v2)


def _head_col(v, lane, h):
    return jnp.sum(jnp.where(lane == h, v, 0.0), axis=1, keepdims=True)


def _flash_fwd(q, k, v, fq_aux, fk_rows, *, dh, tq, s_mul):
    T, D = q.shape
    G = D // LANES
    hpg = LANES // dh
    nq = T // tq
    k3 = k.reshape(nq, tq, D)
    v3 = v.reshape(nq, tq, D)

    rc = min(FLASH_ROWS, tq)

    def body(q_ref, k_ref, v_ref, fq_ref, fk_ref, o_ref, o32_ref, lse_ref, s_scr, p_scr):
        i = pl.program_id(1)
        lane = lax.broadcasted_iota(jnp.int32, (1, LANES), 1)
        q2 = q_ref[...]
        hmasks = [(lane >= hh * dh) & (lane < (hh + 1) * dh) for hh in range(hpg)]
        qms = [jnp.where(hm, q2, jnp.zeros_like(q2)) for hm in hmasks]

        nlb = tq // LANES
        sum_lane = [((hh + 1) % hpg) * dh for hh in range(hpg)]

        def scores(j, slot):
            kj = k_ref[j]
            for hh in range(hpg):
                s = _dot_nt(qms[hh], kj)
                s_scr[slot, hh] = s if s_mul == 1.0 else s * s_mul

        def soft(j, slot, carry, masked):
            vj = v_ref[j]
            out = []
            for hh in range(hpg):
                m_b, acc = carry[hh]
                fk_row = fk_ref[j, hh:hh + 1, :]
                def future(r0, cb):
                    return masked and cb * LANES > r0 + rc - 1

                mx = []
                for r0 in range(0, tq, rc):
                    rs = slice(r0, r0 + rc)
                    c = None
                    for cb in range(nlb):
                        if future(r0, cb):
                            continue
                        cs = slice(cb * LANES, (cb + 1) * LANES)
                        s = s_scr[slot, hh, rs, cs] - fk_row[:, cs]
                        if masked and (cb + 1) * LANES - 1 > r0:
                            ri = r0 + lax.broadcasted_iota(jnp.int32, (rc, LANES), 0)
                            ci = cb * LANES + lax.broadcasted_iota(jnp.int32, (rc, LANES), 1)
                            s = jnp.where(ci <= ri, s, MASK_VALUE)
                        s_scr[slot, hh, rs, cs] = s
                        c = s if c is None else jnp.maximum(c, s)
                    mx.append(c)
                row_max = jnp.max(jnp.concatenate(mx, axis=0), axis=1, keepdims=True)
                m_new = jnp.maximum(m_b, row_max)
                alpha = jnp.exp(m_b - m_new)
                for r0 in range(0, tq, rc):
                    rs = slice(r0, r0 + rc)
                    m_c = m_new[rs]
                    for cb in range(nlb):
                        cs = slice(cb * LANES, (cb + 1) * LANES)
                        if future(r0, cb):
                            p_scr[slot, hh, rs, cs] = jnp.zeros((rc, LANES), BF16)
                        else:
                            p_scr[slot, hh, rs, cs] = jnp.exp(s_scr[slot, hh, rs, cs] - m_c).astype(BF16)
                v_one = jnp.where(hmasks[hh], vj, jnp.ones_like(vj))
                out.append((m_new, alpha * acc + _dot(p_scr[slot, hh], v_one)))
            return tuple(out)

        def step(j, carry, masked, slot):
            scores(j, slot)
            return soft(j, slot, carry, masked)

        def pair(t, cr):
            scores(2 * t, 0)
            scores(2 * t + 1, 1)
            return soft(2 * t + 1, 1, soft(2 * t, 0, cr, False), False)

        init = tuple((jnp.full((tq, LANES), MASK_VALUE, F32), jnp.zeros((tq, LANES), F32)) for _ in range(hpg))
        carry = lax.fori_loop(0, i // 2, pair, init)
        carry = lax.cond(i % 2 == 1, lambda cr: step(i - 1, cr, False, 0), lambda cr: cr, carry)
        carry = step(i, carry, True, 1)
        fq = fq_ref[...]
        g0 = pl.program_id(0) * hpg
        o_all = jnp.zeros((tq, LANES), F32)
        lse_all = jnp.zeros((tq, LANES), F32)
        for hh in range(hpg):
            m_b, acc = carry[hh]
            l = acc[:, sum_lane[hh]:sum_lane[hh] + 1]
            o_all = jnp.where(hmasks[hh], acc * (1.0 / l), o_all)
            lse_all = jnp.where(lane == hh, m_b[:, 0:1] + jnp.log(l) + _head_col(fq, lane, g0 + hh), lse_all)
        o_ref[...] = o_all.astype(BF16)
        o32_ref[...] = o_all
        lse_ref[...] = lse_all

    return _pcall(
        body, name="flash_fwd", grid=(G, nq),
        in_specs=[pl.BlockSpec((tq, LANES), lambda g, i: (i, g)),
                  pl.BlockSpec((nq, tq, LANES), lambda g, i: (0, 0, g)),
                  pl.BlockSpec((nq, tq, LANES), lambda g, i: (0, 0, g)),
                  pl.BlockSpec((tq, LANES), lambda g, i: (i, 0)),
                  pl.BlockSpec((None, nq, SUBLANES, tq), lambda g, i: (g, 0, 0, 0))],
        out_specs=[pl.BlockSpec((tq, LANES), lambda g, i: (i, g)),
                   pl.BlockSpec((tq, LANES), lambda g, i: (i, g)),
                   pl.BlockSpec((None, tq, LANES), lambda g, i: (g, i, 0))],
        out_shape=[jax.ShapeDtypeStruct((T, D), BF16), jax.ShapeDtypeStruct((T, D), F32),
                   jax.ShapeDtypeStruct((G, T, LANES), F32)],
        scratch_shapes=[pltpu.VMEM((2, hpg, tq, tq), F32), pltpu.VMEM((2, hpg, tq, tq), BF16)],
        compiler_params=_params(("arbitrary", "arbitrary")),
    )(q, k3, v3, fq_aux, fk_rows)


def _flash_bwd(q, k, v, do, fq_aux, lse_aux, dl_aux, fk_rows, *, dh, tq, tk, s_mul, dq_mul):
    T, D = q.shape
    G = D // LANES
    hpg = LANES // dh
    nq = T // tq
    nk = T // tk
    per = tq // tk
    k3 = k.reshape(nk, tk, D)
    v3 = v.reshape(nk, tk, D)
    rc = min(FLASH_ROWS, tq)

    def body(q_ref, k_ref, v_ref, do_ref, fq_ref, lse_ref, dl_ref, fk_ref,
             dq_ref, dk_ref, dv_ref, dfq_ref, dfk_ref, dk_acc, dv_acc, s_scr, dp_scr, p_scr, ds_scr):
        i = pl.program_id(1)

        @pl.when(i == 0)
        def _():
            dk_acc[...] = jnp.zeros_like(dk_acc)
            dv_acc[...] = jnp.zeros_like(dv_acc)
            dfk_ref[...] = jnp.zeros_like(dfk_ref)

        lane = lax.broadcasted_iota(jnp.int32, (1, LANES), 1)
        q2 = q_ref[...]
        do2 = do_ref[...]
        g0 = pl.program_id(0) * hpg
        fq = fq_ref[...]
        lse = lse_ref[...]
        dl = dl_ref[...]
        hmasks = [(lane >= hh * dh) & (lane < (hh + 1) * dh) for hh in range(hpg)]
        qms = [jnp.where(hm, q2, jnp.zeros_like(q2)) for hm in hmasks]
        doms = [jnp.where(hm, do2, jnp.zeros_like(do2)) for hm in hmasks]
        q2t = q2.astype(F32).T.astype(BF16)
        do2t = do2.astype(F32).T.astype(BF16)
        ones_rows = jnp.ones((2 * SUBLANES, tk), BF16)
        q1t = [jnp.concatenate([q2t[hh * dh:(hh + 1) * dh], jnp.ones((2 * SUBLANES, tq), BF16)], axis=0)
               for hh in range(hpg)]
        c_bs = [jnp.broadcast_to(_head_col(fq, lane, g0 + hh) - lse[:, hh:hh + 1], (tq, LANES))
                for hh in range(hpg)]
        dl_bs = [jnp.broadcast_to(_head_col(dl, lane, g0 + hh), (tq, LANES)) for hh in range(hpg)]
        nlb = tk // LANES

        def step(j, carry, off, slot=0):
            masked = off is not None
            kj = k_ref[j]
            vj = v_ref[j]
            kjt = kj.astype(F32).T.astype(BF16)
            for hh in range(hpg):
                s = _dot_nt(qms[hh], kj)
                s_scr[slot, hh] = s if s_mul == 1.0 else s * s_mul
                dp_scr[slot, hh] = _dot_nt(doms[hh], vj)
            out = []
            for hh in range(hpg):
                fk_row = fk_ref[j, hh:hh + 1, :]
                for r0 in range(0, tq, rc):
                    rs = slice(r0, r0 + rc)
                    if masked and r0 + rc <= off:
                        p_scr[slot, hh, rs, :] = jnp.zeros((rc, tk), BF16)
                        ds_scr[slot, hh, rs, :] = jnp.zeros((rc, tk), BF16)
                        continue
                    c_c = c_bs[hh][rs]
                    dl_c = dl_bs[hh][rs]
                    for cb in range(nlb):
                        cs = slice(cb * LANES, (cb + 1) * LANES)
                        if masked and off + cb * LANES > r0 + rc - 1:
                            p_scr[slot, hh, rs, cs] = jnp.zeros((rc, LANES), BF16)
                            ds_scr[slot, hh, rs, cs] = jnp.zeros((rc, LANES), BF16)
                            continue
                        e = (s_scr[slot, hh, rs, cs] - fk_row[:, cs]) + c_c
                        if masked and off + (cb + 1) * LANES - 1 > r0:
                            ri = r0 + lax.broadcasted_iota(jnp.int32, (rc, LANES), 0)
                            ci = off + cb * LANES + lax.broadcasted_iota(jnp.int32, (rc, LANES), 1)
                            e = jnp.where(ci <= ri, e, MASK_VALUE)
                        p = jnp.exp(e)
                        ds = p * (dp_scr[slot, hh, rs, cs] - dl_c)
                        p_scr[slot, hh, rs, cs] = p.astype(BF16)
                        ds_scr[slot, hh, rs, cs] = ds.astype(BF16)
                k1t = jnp.concatenate([kjt[hh * dh:(hh + 1) * dh], ones_rows], axis=0)
                out.append(carry[hh] + _dot_nt(k1t, ds_scr[slot, hh]))
            dks = [_dot(q1t[hh], ds_scr[slot, hh]) for hh in range(hpg)]
            for hh in range(hpg):
                dfk_ref[j, hh:hh + 1, :] += -dks[hh][dh:dh + 1, :]
            dk_acc[j] += jnp.concatenate([d[0:dh] for d in dks], axis=0)
            dv_acc[j] += jnp.concatenate(
                [_dot(do2t[hh * dh:(hh + 1) * dh], p_scr[slot, hh]) for hh in range(hpg)], axis=0)
            return tuple(out)

        init = tuple(jnp.zeros((dh + 2 * SUBLANES, tq), F32) for _ in range(hpg))
        n_past = i * per

        def pair(t, cr):
            return step(2 * t + 1, step(2 * t, cr, None, 0), None, 1)

        carry = lax.fori_loop(0, n_past // 2, pair, init)
        carry = lax.cond(n_past % 2 == 1, lambda cr: step(n_past - 1, cr, None, 0), lambda cr: cr, carry)
        for d in range(per):
            carry = step(i * per + d, carry, d * tk, (d + 1) % 2)
        dq_all = jnp.concatenate([carry[hh][0:dh] for hh in range(hpg)], axis=0).T
        dq_ref[...] = (dq_all * dq_mul).astype(BF16)
        dfq_ref[...] = jnp.concatenate([carry[hh][dh:dh + 1, :] for hh in range(hpg)]
                                       + [jnp.zeros((SUBLANES - hpg, tq), F32)], axis=0)

        @pl.when(i == nq - 1)
        def _():
            for jj in range(nk):
                dkv = dk_acc[jj].T
                if s_mul != 1.0:
                    dkv = dkv * s_mul
                dk_ref[jj] = dkv.astype(BF16)
                dv_ref[jj] = dv_acc[jj].T.astype(BF16)

    blk = pl.BlockSpec((tq, LANES), lambda g, i: (i, g))
    res = pl.BlockSpec((nk, tk, LANES), lambda g, i: (0, 0, g))
    res_in = pl.BlockSpec((nk, tk, LANES), lambda g, i: (0, 0, g), pipeline_mode=pl.Buffered(1))
    aux = pl.BlockSpec((None, tq, LANES), lambda g, i: (g, i, 0))
    heads = pl.BlockSpec((tq, LANES), lambda g, i: (i, 0))
    rows = pl.BlockSpec((None, nk, SUBLANES, tk), lambda g, i: (g, 0, 0, 0))
    dq, dk3, dv3, dfq, dfk = _pcall(
        body, name="flash_bwd", grid=(G, nq),
        in_specs=[blk, res_in, res_in, blk, heads, aux, heads, rows],
        out_specs=[blk, res, res, pl.BlockSpec((None, None, SUBLANES, tq), lambda g, i: (g, i, 0, 0)), rows],
        out_shape=[jax.ShapeDtypeStruct((T, D), BF16), jax.ShapeDtypeStruct((nk, tk, D), BF16),
                   jax.ShapeDtypeStruct((nk, tk, D), BF16), jax.ShapeDtypeStruct((G, nq, SUBLANES, tq), F32),
                   jax.ShapeDtypeStruct((G, nk, SUBLANES, tk), F32)],
        scratch_shapes=[pltpu.VMEM((nk, LANES, tk), F32), pltpu.VMEM((nk, LANES, tk), F32),
                        pltpu.VMEM((2, hpg, tq, tk), F32), pltpu.VMEM((2, hpg, tq, tk), F32),
                        pltpu.VMEM((2, hpg, tq, tk), BF16), pltpu.VMEM((2, hpg, tq, tk), BF16)],
        compiler_params=_params(("arbitrary", "arbitrary")),
    )(q, k3, v3, do, fq_aux, lse_aux, dl_aux, fk_rows)
    return dq, dk3.reshape(T, D), dv3.reshape(T, D), dfq, dfk


def _attn_out_fwd(o, x, wo, g_post, tm):
    T, D = x.shape

    def body(o_ref, x_ref, w_ref, g_ref, m_ref, xo_ref):
        m = _dot(o_ref[...], w_ref[...])
        m_ref[...] = m
        xo_ref[...] = x_ref[...] + _rms(m, g_ref[...])

    return _pcall(
        body, name="attn_out_fwd", grid=(T // tm,),
        in_specs=[_rows(tm, D), _rows(tm, D), _full((D, D)), _full((1, D))],
        out_specs=[_rows(tm, D), _rows(tm, D)],
        out_shape=[jax.ShapeDtypeStruct((T, D), F32), jax.ShapeDtypeStruct((T, D), F32)],
        compiler_params=_params(("arbitrary",)),
    )(o, x, wo, g_post)


def _attn_out_bwd(dxo, m, o, wo, g_post, head_ind, tm, dep):
    T, D = m.shape

    def body(dxo_ref, m_ref, o_ref, w_ref, g_ref, ind_ref, dep_ref, dm_ref, do_ref, dl_ref, sums_ref):
        i = pl.program_id(0)
        dm, dgpost = _rms_bwd(m_ref[...], g_ref[...], dxo_ref[...])
        dmb = dm.astype(BF16)
        dm_ref[...] = dmb
        dob = _dot_nt(dmb, w_ref[...]).astype(BF16)
        do_ref[...] = dob
        dl_ref[...] = jnp.dot(dob.astype(F32) * o_ref[...], ind_ref[...], precision=lax.Precision.HIGHEST,
                              preferred_element_type=F32)

        @pl.when(i == 0)
        def _():
            sums_ref[...] = jnp.zeros_like(sums_ref)
        sums_ref[0:1, :] += dgpost

    return _pcall(
        body, name="attn_out_bwd", grid=(T // tm,),
        in_specs=[_rows(tm, D), _rows(tm, D), _rows(tm, D), _full((D, D)), _full((1, D)), _full((D, LANES)),
                  pl.BlockSpec(memory_space=pl.ANY)],
        out_specs=[_rows(tm, D), _rows(tm, D), _rows(tm, LANES), _acc((SUBLANES, D))],
        out_shape=[jax.ShapeDtypeStruct((T, D), BF16), jax.ShapeDtypeStruct((T, D), BF16),
                   jax.ShapeDtypeStruct((T, LANES), F32), jax.ShapeDtypeStruct((SUBLANES, D), F32)],
        compiler_params=_params(("arbitrary",)),
    )(dxo, m, o, wo, g_post, head_ind, dep)


def _attn_in_bwd(dxo, x, g_pre, dq, dk, dv, dlf, lf, wqkv, wf, tm, n_heads):
    T, D = x.shape

    def body(dxo_ref, x_ref, g_ref, dq_ref, dk_ref, dv_ref, dlf_ref, lf_ref, w_ref, wf_ref,
             dxi_ref, h_ref, df_ref, sums_ref, dbf_ref):
        i = pl.program_id(0)
        xv = x_ref[...]
        h_ref[...] = _rms(xv, g_ref[...]).astype(BF16)
        lane = lax.broadcasted_iota(jnp.int32, (1, LANES), 1)
        df = jnp.where(lane < n_heads, dlf_ref[...] * (1.0 - jnp.exp(lf_ref[...])), 0.0)
        dfb = df.astype(BF16)
        df_ref[...] = dfb
        dh = (_dot_nt(dq_ref[...], w_ref[:, 0:D]) + _dot_nt(dk_ref[...], w_ref[:, D:2 * D])
              + _dot_nt(dv_ref[...], w_ref[:, 2 * D:3 * D]) + _dot_nt(dfb, wf_ref[...]))
        dxi, dgpre = _rms_bwd(xv, g_ref[...], dh)
        dxi_ref[...] = dxo_ref[...] + dxi

        @pl.when(i == 0)
        def _():
            sums_ref[...] = jnp.zeros_like(sums_ref)
            dbf_ref[...] = jnp.zeros_like(dbf_ref)
        sums_ref[0:1, :] += dgpre
        dbf_ref[...] += jnp.sum(df, axis=0, keepdims=True)

    return _pcall(
        body, name="attn_in_bwd", grid=(T // tm,),
        in_specs=[_rows(tm, D), _rows(tm, D), _full((1, D)), _rows(tm, D), _rows(tm, D), _rows(tm, D),
                  _rows(tm, LANES), _rows(tm, LANES), _full((D, 3 * D)), _full((D, LANES))],
        out_specs=[_rows(tm, D), _rows(tm, D), _rows(tm, LANES), _acc((SUBLANES, D)), _acc((1, LANES))],
        out_shape=[jax.ShapeDtypeStruct((T, D), F32), jax.ShapeDtypeStruct((T, D), BF16),
                   jax.ShapeDtypeStruct((T, LANES), BF16), jax.ShapeDtypeStruct((SUBLANES, D), F32),
                   jax.ShapeDtypeStruct((1, LANES), F32)],
        compiler_params=_params(("arbitrary",)),
    )(dxo, x, g_pre, dq, dk, dv, dlf, lf, wqkv, wf)


def _adamw(recvs, w, m, v, tr, name):
    L, R, C = w.shape
    assert len(recvs) == L
    c1 = 1.0 - ADAM_B1 ** ADAM_STEP
    c2 = 1.0 - ADAM_B2 ** ADAM_STEP

    def body(*refs):
        r_refs = refs[:L]
        w_ref, m_ref, v_ref, g_ref, d_ref, nm_ref, nv_ref = refs[L:]
        layer = pl.program_id(0)
        g = None
        for k in range(L):
            gk = r_refs[k][0, :, 0:C].astype(F32)
            for s in range(1, N_DEV):
                gk = gk + r_refs[k][s, :, 0:C].astype(F32)
            g = gk if g is None else jnp.where(layer == k, gk, g)
        nm = ADAM_B1 * m_ref[...] + (1.0 - ADAM_B1) * g
        nv = ADAM_B2 * v_ref[...] + (1.0 - ADAM_B2) * jnp.square(g)
        m_hat = nm / c1
        v_hat = nv / c2
        g_ref[...] = g
        d_ref[...] = -ADAM_LR * (m_hat / (jnp.sqrt(v_hat) + ADAM_EPS) + ADAM_WD * w_ref[...])
        nm_ref[...] = nm
        nv_ref[...] = nv

    def recv_spec(k):
        return pl.BlockSpec((N_DEV, tr, recvs[k].shape[-1]), lambda l, i: (0, jnp.where(l == k, i, 0), 0))

    blk = pl.BlockSpec((None, tr, C), lambda l, i: (l, i, 0))
    return _pcall(
        body, name=name, grid=(L, R // tr),
        in_specs=[recv_spec(k) for k in range(L)] + [blk] * 3,
        out_specs=[blk] * 4,
        out_shape=[jax.ShapeDtypeStruct((L, R, C), F32)] * 4,
        compiler_params=_params(("arbitrary", "arbitrary")),
    )(*recvs, w, m, v)


def _row_block(rows, cols):
    cap = max(SUBLANES, (256 * 1024) // max(cols, 1))
    best = None
    for t in range(SUBLANES, rows + 1, SUBLANES):
        if rows % t == 0 and t <= cap:
            best = t
    return rows if best is None else best


def kernel(x, g_mix_pre, g_mix_post, g_ffn_pre, g_ffn_post, conv_pw1_w, conv_pw1_b, conv_dw_w, conv_dw_b, conv_ln_g, conv_ln_b, conv_pw2_w, conv_pw2_b, attn_w_in, attn_b_f, attn_w_o, mlp_w_up, mlp_w_down, loss_target, m_g_mix_pre, m_g_mix_post, m_g_ffn_pre, m_g_ffn_post, m_conv_pw1_w, m_conv_pw1_b, m_conv_dw_w, m_conv_dw_b, m_conv_ln_g, m_conv_ln_b, m_conv_pw2_w, m_conv_pw2_b, m_attn_w_in, m_attn_b_f, m_attn_w_o, m_mlp_w_up, m_mlp_w_down, v_g_mix_pre, v_g_mix_post, v_g_ffn_pre, v_g_ffn_post, v_conv_pw1_w, v_conv_pw1_b, v_conv_dw_w, v_conv_dw_b, v_conv_ln_g, v_conv_ln_b, v_conv_pw2_w, v_conv_pw2_b, v_attn_w_in, v_attn_b_f, v_attn_w_o, v_mlp_w_up, v_mlp_w_down):
    _, T, D = x.shape
    H = attn_b_f.shape[-1]
    dh = D // H
    width = conv_dw_w.shape[1]
    cin = attn_w_in.shape[-1]
    fs = mlp_w_up.shape[-1]
    G = D // LANES
    hpg = LANES // dh
    assert T % 4 == 0 and D % LANES == 0 and LANES % dh == 0 and width <= CONV_HALO and H <= LANES

    tm = min(512, T // 4)
    tmb = min(256, T // 4)
    tqf = min(1024, T // 4)
    tkb = tm
    tmc = min(256, T // 4)
    lcf = min(512, D)
    lcb = min(256, D)
    tkw = min(4096, T // 2)
    tkd = min(2048, T // 4)
    tb = min(512, T // 4)

    scale = float(dh) ** -0.5
    mant, _ = math.frexp(scale)
    q_mul = scale if mant == 0.5 else 1.0
    s_mul = 1.0 if mant == 0.5 else scale

    x2 = x.reshape(T, D)
    tgt = loss_target.reshape(T, D)

    w_srcs = [conv_pw1_w, conv_dw_w, conv_pw2_w, mlp_w_up, mlp_w_down, attn_w_in, attn_w_o]
    w_items = [(0, 0, "whole"), (1, 0, "whole"), (2, 0, "whole"), (3, 0, "whole"), (4, 0, "whole"),
               (5, 0, "whole"), (6, 0, "whole"), (3, 1, "whole"), (4, 1, "whole")]
    cin_w = -(-cin // LANES) * LANES
    w_lands = _place_own(
        w_srcs, w_items,
        [((N_DEV, D, cin_w) if si == 5 else (N_DEV,) + w_srcs[si].shape[1:], F32 if si == 1 else BF16)
         for si, _, _ in w_items],
        "stage_weights", cast=True)
    me_arr = _dev_index(*_mesh_pos()).astype(jnp.int32).reshape(1)
    w_groups = [[0], [1, 2], [3, 4], [5, 6], [7, 8]]
    g_sems, _, w_lands, g_token = _push_start(
        [], w_lands, [[(a, None, "own") for a in grp] for grp in w_groups], "gather_start")

    def gather_wait(gi, after):
        grp = w_groups[gi]
        return _push_wait(g_sems[gi], [], [w_lands[a] for a in grp], [(k, None, "own") for k in range(len(grp))],
                          after, "gather_wait%d" % gi)

    (w1g,) = gather_wait(0, g_token)
    bf = jnp.pad(attn_b_f, ((0, 0), (0, LANES - H)))

    row = lambda a, i: a[i:i + 1]

    a0, u0 = _conv_in_fwd(x2, row(g_mix_pre, 0), w1g, conv_pw1_b, tm)
    dwg, w2g = gather_wait(1, u0)
    w2 = w2g.reshape(D, D)
    dw_full = jnp.transpose(dwg, (1, 0, 2)).reshape(width, D)
    w32 = jnp.pad(dw_full, ((0, 32 - width), (0, 0)))
    y0 = _dwconv_fwd(u0, w32, conv_dw_b, tmc, lcf, width)
    m0, x_1 = _conv_out_fwd(y0, x2, conv_ln_g, conv_ln_b, w2, conv_pw2_b, row(g_mix_post, 0), tm)
    wu0, wd0 = gather_wait(2, x_1)
    up0, n0, x_2 = _mlp_fwd(x_1, row(g_ffn_pre, 0), wu0, wd0, row(g_ffn_post, 0), tm, "mlp0_fwd")

    wing, wog = gather_wait(3, x_2)
    wo = wog.reshape(D, D)
    win = jnp.transpose(wing[:, :, :cin], (1, 0, 2)).reshape(D, N_DEV * cin)
    wqkv = win[:, :3 * D]
    wf = jnp.pad(win[:, 3 * D:], ((0, 0), (0, LANES - H)))
    q, k, v, lf = _attn_in_fwd(x_2, row(g_mix_pre, 1), wqkv, wf, bf, tm, q_mul, H)
    fcum = _cumsum_rows(lf, jnp.zeros_like(lf), tb, False, "forget_cumsum")


    def key_rows(blk):
        r = jnp.transpose(fcum[:, :H].T.reshape(G, hpg, T // blk, blk), (0, 2, 1, 3))
        return jnp.pad(r, ((0, 0), (0, 0), (0, SUBLANES - hpg), (0, 0)))

    fk_rows = key_rows(tkb)
    o, o32, lse_aux = _flash_fwd(q, k, v, fcum, key_rows(tqf), dh=dh, tq=tqf, s_mul=s_mul)
    m1, x_3 = _attn_out_fwd(o, x_2, wo, row(g_mix_post, 1), tm)
    wu1, wd1 = gather_wait(4, x_3)
    up1, n1, x_4 = _mlp_fwd(x_3, row(g_ffn_pre, 1), wu1, wd1, row(g_ffn_post, 1), tm, "mlp1_fwd")

    def mlp_back(dx, n_l, x_in, up_l, l, wu, wd, dep, from_loss=False):
        dxi, h, dm, dup, sums = _mlp_bwd(dx, n_l, x_in, up_l, row(g_ffn_pre, l), wu, wd, row(g_ffn_post, l),
                                         tmb, "mlp%d_bwd" % l, dep, from_loss)
        dwu = _mm_tn(h, dup, nj=N_DEV, a_cols=D, g_cols=fs, a_by_j=False, g_by_j=True, tk=tkw,
                     name="mlp%d_dwu" % l)
        dwd = _mm_tn(up_l, dm, nj=N_DEV, a_cols=fs, g_cols=D, a_by_j=True, g_by_j=False, tk=tkw,
                     name="mlp%d_dwd" % l, act=True)
        return dxi, dwu, dwd, sums

    def push_grads(srcs, modes, name):
        items = [(i, None, mode) for i, mode in enumerate(modes)]
        lands = _seed_lands(srcs, modes, me_arr, name + "_own")
        sems, srcs_t, lands_t, token = _push_start(srcs, lands, [items], name + "_start")
        return (sems[0], srcs_t, lands_t, items), token

    def pull_grads(handle, after, name):
        sems, srcs_t, lands_t, items = handle
        return _push_wait(sems, srcs_t, lands_t, items, after, name + "_wait")

    rs = D // N_DEV
    dx, dwu1, dwd1, s_mlp1 = mlp_back(x_4, n1, x_3, up1, 1, wu1, wd1, tgt, from_loss=True)
    h_mlp1, tok = push_grads([dwu1, dwd1], ["slot", "slot"], "grads_mlp1")

    head_ind = jnp.asarray((np.arange(D)[:, None] // dh == np.arange(LANES)[None, :]).astype(np.float32))
    dm1, do, delta, s_ao = _attn_out_bwd(dx, m1, o32, wo, row(g_mix_post, 1), head_ind, tm, tok)
    dwo = _mm_tn(o, dm1, nj=1, a_cols=D, g_cols=D, a_by_j=False, g_by_j=False, tk=tkd, name="attn_dwo")
    dq, dk, dv, dfq, dfk = _flash_bwd(q, k, v, do, fcum, lse_aux, delta, fk_rows,
                                      dh=dh, tq=tkb, tk=tkb, s_mul=s_mul, dq_mul=scale)
    def head_cols(r):
        return jnp.pad(jnp.transpose(r[:, :, :hpg, :], (0, 2, 1, 3)).reshape(H, T).T, ((0, 0), (0, LANES - H)))

    df_k = head_cols(dfk)
    df_q = head_cols(dfq)
    dlf = _cumsum_rows(df_q, df_k, tb, True, "forget_cumsum_bwd")
    dx, h_at, df, s_ai, dbf = _attn_in_bwd(dx, x_2, row(g_mix_pre, 1), dq, dk, dv, dlf, lf, wqkv, wf, tm, H)
    dwq = _mm_tn(h_at, dq, nj=1, a_cols=D, g_cols=D, a_by_j=False, g_by_j=False, tk=tkd, name="attn_dwq")
    dwk = _mm_tn(h_at, dk, nj=1, a_cols=D, g_cols=D, a_by_j=False, g_by_j=False, tk=tkd, name="attn_dwk")
    dwv = _mm_tn(h_at, dv, nj=1, a_cols=D, g_cols=D, a_by_j=False, g_by_j=False, tk=tkd, name="attn_dwv")
    dwf = _mm_tn(h_at, df, nj=1, a_cols=D, g_cols=LANES, a_by_j=False, g_by_j=False, tk=tkw, name="attn_dwf")
    dwin = jnp.concatenate([dwq[0], dwk[0], dwv[0], dwf[0][:, :H]], axis=1)
    dwin = jnp.pad(jnp.transpose(dwin.reshape(D, N_DEV, cin), (1, 0, 2)), ((0, 0), (0, 0), (0, cin_w - cin)))
    h_attn, tok = push_grads([dwin, dwo.reshape(N_DEV, rs, D)], ["slot", "slot"], "grads_attn")

    dx, dwu0, dwd0, s_mlp0 = mlp_back(dx, n0, x_1, up0, 0, wu0, wd0, tok)
    h_mlp0, tok = push_grads([dwu0, dwd0], ["slot", "slot"], "grads_mlp0")

    dy0, dm0, z0, s_co = _conv_out_bwd(dx, m0, y0, conv_ln_g, conv_ln_b, w2, row(g_mix_post, 0), tm, tok)
    dw2 = _mm_tn(z0, dm0, nj=1, a_cols=D, g_cols=D, a_by_j=False, g_by_j=False, tk=tkd, name="conv_dw2")
    h_pw2, tok = push_grads([dw2.reshape(N_DEV, rs, D)], ["slot"], "grads_pw2")
    du0, ddw = _dwconv_bwd(dy0, u0, w32, tmc, lcb, width)
    grad_x, h_cv, da0, s_ci, db1 = _conv_in_bwd(dx, du0, a0, x2, row(g_mix_pre, 0), w1g, tm, tok)
    dw1 = _mm_tn(h_cv, da0, nj=N_DEV, a_cols=D, g_cols=(2 * D) // N_DEV, a_by_j=False, g_by_j=True, tk=tkw,
                 name="conv_dw1")
    ddw_s = jnp.transpose(ddw[:width].reshape(width, N_DEV, D // N_DEV), (1, 0, 2))

    def pad_row(a):
        return jnp.pad(a, ((0, 0), (0, D - a.shape[1])))

    def pack(gmp, gmq, gfp, gfq, b1, dwb, lng, lnb, b2, bfv, last):
        return jnp.concatenate([gmp, gmq, gfp, gfq, b1.reshape(2, D), dwb, lng, lnb, b2, pad_row(bfv), last],
                               axis=0)

    zero_row = jnp.zeros((1, D), F32)
    small_g = pack(
        jnp.concatenate([row(s_ci, 0), row(s_ai, 0)], axis=0),
        jnp.concatenate([row(s_co, 0), row(s_ao, 0)], axis=0),
        jnp.concatenate([row(s_mlp0, 1), row(s_mlp1, 1)], axis=0),
        jnp.concatenate([row(s_mlp0, 0), row(s_mlp1, 0)], axis=0),
        db1, row(s_co, 4), row(s_co, 1), row(s_co, 2), row(s_co, 3), dbf[:, :H],
        pad_row(row(s_mlp1, 2)[:, 0:1]))
    h_conv, tok = push_grads([dw1, ddw_s, small_g], ["slot", "slot", "whole"], "grads_conv")

    def opt(recvs, w, m, v, name):
        shp = w.shape
        L, C = shp[0], shp[-1]
        R = int(np.prod(shp[1:-1]))
        outs = _adamw([r.reshape(N_DEV, R, r.shape[-1]) for r in recvs], w.reshape(L, R, C), m.reshape(L, R, C),
                      v.reshape(L, R, C), _row_block(R, C), name)
        return [t.reshape(shp) for t in outs]

    big = {}
    r_wu1, r_wd1 = pull_grads(h_mlp1, tok, "grads_mlp1")
    r_win, r_wo = pull_grads(h_attn, r_wd1, "grads_attn")
    big["attn_w_in"] = opt([r_win], attn_w_in, m_attn_w_in, v_attn_w_in, "adamw_win")
    big["attn_w_o"] = opt([r_wo], attn_w_o, m_attn_w_o, v_attn_w_o, "adamw_wo")
    r_wu0, r_wd0 = pull_grads(h_mlp0, big["attn_w_o"][0], "grads_mlp0")
    big["mlp_w_up"] = opt([r_wu0, r_wu1], mlp_w_up, m_mlp_w_up, v_mlp_w_up, "adamw_wup")
    big["mlp_w_down"] = opt([r_wd0, r_wd1], mlp_w_down, m_mlp_w_down, v_mlp_w_down, "adamw_wdown")
    (r_w2,) = pull_grads(h_pw2, big["mlp_w_down"][0], "grads_pw2")
    big["conv_pw2_w"] = opt([r_w2], conv_pw2_w, m_conv_pw2_w, v_conv_pw2_w, "adamw_pw2")
    r_w1, r_dw, r_small = pull_grads(h_conv, big["conv_pw2_w"][0], "grads_conv")
    big["conv_pw1_w"] = opt([r_w1], conv_pw1_w, m_conv_pw1_w, v_conv_pw1_w, "adamw_pw1")
    big["conv_dw_w"] = opt([r_dw], conv_dw_w, m_conv_dw_w, v_conv_dw_w, "adamw_dw")
    small_w = pack(g_mix_pre, g_mix_post, g_ffn_pre, g_ffn_post, conv_pw1_b, conv_dw_b, conv_ln_g, conv_ln_b,
                   conv_pw2_b, attn_b_f, zero_row)
    small_m = pack(m_g_mix_pre, m_g_mix_post, m_g_ffn_pre, m_g_ffn_post, m_conv_pw1_b, m_conv_dw_b, m_conv_ln_g,
                   m_conv_ln_b, m_conv_pw2_b, m_attn_b_f, zero_row)
    small_v = pack(v_g_mix_pre, v_g_mix_post, v_g_ffn_pre, v_g_ffn_post, v_conv_pw1_b, v_conv_dw_b, v_conv_ln_g,
                   v_conv_ln_b, v_conv_pw2_b, v_attn_b_f, zero_row)
    sm = _adamw([r_small], small_w[None], small_m[None], small_v[None], small_w.shape[0], "adamw_small")
    sm = [t[0] for t in sm]
    loss = sm[0][15, 0]

    def unpack(t):
        return {"g_mix_pre": t[0:2], "g_mix_post": t[2:4], "g_ffn_pre": t[4:6], "g_ffn_post": t[6:8],
                "conv_pw1_b": t[8:10].reshape(1, 2 * D), "conv_dw_b": t[10:11], "conv_ln_g": t[11:12],
                "conv_ln_b": t[12:13], "conv_pw2_b": t[13:14], "attn_b_f": t[14:15, :H]}

    small = [unpack(t) for t in sm]
    names = ["g_mix_pre", "g_mix_post", "g_ffn_pre", "g_ffn_post", "conv_pw1_w", "conv_pw1_b", "conv_dw_w",
             "conv_dw_b", "conv_ln_g", "conv_ln_b", "conv_pw2_w", "conv_pw2_b", "attn_w_in", "attn_b_f",
             "attn_w_o", "mlp_w_up", "mlp_w_down"]
    outs = [loss, grad_x.reshape(1, T, D)]
    for kind in range(4):
        for nme in names:
            outs.append(big[nme][kind] if nme in big else small[kind][nme])
    return tuple(outs)
```
